```python
import jax, jax.numpy as jnp
from jax import lax
import numpy as np

D_MODEL = 1024
BATCH = 2
SEQ = 16384
DEPTH = 1

HEAD_DIM = 64
N_ATTN_HEADS = 12
ATTN_WIDTH = N_ATTN_HEADS * HEAD_DIM
CONV_CHANNELS = D_MODEL // 4
MIX_WIDTH = ATTN_WIDTH + CONV_CHANNELS
IN_WIDTH = 3 * ATTN_WIDTH + 2 * CONV_CHANNELS
CONV_WIDTH = 31
DILATED_BRANCHES = ((128, 1), (512, 4), (2048, 16))
ATTN_BLOCK = 128
ROPE_THETA = 500000.0
ROPE_DIM = HEAD_DIM // 4
N_EXPERTS = 32
TOP_K = 4
D_FF = D_MODEL
SWIGLU_LIMIT = 7.0
SWIGLU_ALPHA = 1.702
MOE_BLOCK = 128
PLE_DIM = 256
NORM_EPS = 1e-6

kernel_name = "hymba_dilated_conformer_moe_ple_layer"


def rms_norm(x, g):
    xf = x.astype(jnp.float32)
    y = xf * lax.rsqrt(jnp.mean(xf * xf, axis=-1, keepdims=True) + NORM_EPS)
    return (y * g.astype(jnp.float32)).astype(x.dtype)


def layer_norm(x, g, b):
    xf = x.astype(jnp.float32)
    mu = jnp.mean(xf, axis=-1, keepdims=True)
    var = jnp.mean(jnp.square(xf - mu), axis=-1, keepdims=True)
    y = (xf - mu) * lax.rsqrt(var + NORM_EPS)
    return (y * g.astype(jnp.float32) + b.astype(jnp.float32)).astype(x.dtype)


def partial_rotary(t, positions):
    half = ROPE_DIM // 2
    inv_freq = ROPE_THETA ** (-jnp.arange(0, ROPE_DIM, 2, dtype=jnp.float32) / ROPE_DIM)
    ang = positions.astype(jnp.float32)[..., None] * inv_freq
    cos = jnp.cos(ang)[:, :, None, :]
    sin = jnp.sin(ang)[:, :, None, :]
    tr = t[..., :ROPE_DIM].astype(jnp.float32)
    t1, t2 = tr[..., :half], tr[..., half:]
    rot = jnp.concatenate([t1 * cos - t2 * sin, t2 * cos + t1 * sin], axis=-1)
    return jnp.concatenate([rot.astype(t.dtype), t[..., ROPE_DIM:]], axis=-1)


def banded_causal_attention(q, k, v, window):
    n, length, h, dh = q.shape
    blk = ATTN_BLOCK
    nb = -(-length // blk)
    pad = nb * blk - length
    if pad:
        cfg = ((0, 0), (0, pad), (0, 0), (0, 0))
        q, k, v = jnp.pad(q, cfg), jnp.pad(k, cfg), jnp.pad(v, cfg)
    qb = q.reshape(n, nb, blk, h, dh)
    kb = k.reshape(n, nb, blk, h, dh)
    vb = v.reshape(n, nb, blk, h, dh)
    zk = jnp.zeros_like(kb[:, :1])
    kk = jnp.concatenate([jnp.concatenate([zk, kb[:, :-1]], axis=1), kb], axis=2)
    vv = jnp.concatenate([jnp.concatenate([zk, vb[:, :-1]], axis=1), vb], axis=2)
    s = jnp.einsum('nbqhd,nbkhd->nbhqk', qb, kk, preferred_element_type=jnp.float32)
    s = s * (HEAD_DIM ** -0.5)
    qi = jnp.arange(blk)[:, None]
    kj = jnp.arange(2 * blk)[None, :]
    dist = qi + blk - kj
    bidx = jnp.arange(nb)[:, None, None]
    valid = (dist >= 0) & (dist <= window) & (bidx * blk + kj - blk >= 0)
    s = jnp.where(valid[None, :, None], s, -jnp.inf)
    m = jnp.max(s, axis=-1, keepdims=True)
    e = jnp.exp(s - m)
    den = jnp.sum(e, axis=-1, keepdims=True)
    o = jnp.einsum('nbhqk,nbkhd->nbqhd', (e / den).astype(v.dtype), vv,
                   preferred_element_type=jnp.float32)
    lse = (m + jnp.log(den))[..., 0]
    o = o.reshape(n, nb * blk, h, dh)[:, :length]
    lse = lse.transpose(0, 1, 3, 2).reshape(n, nb * blk, h)[:, :length]
    return o, lse


def dilated_attention(q, k, v):
    b, s, h, dh = q.shape
    outs, lses = [], []
    for window, dil in DILATED_BRANCHES:
        sub = s // dil

        def to_sub(t):
            return t.reshape(b, sub, dil, h, dh).transpose(0, 2, 1, 3, 4).reshape(b * dil, sub, h, dh)

        o, l = banded_causal_attention(to_sub(q), to_sub(k), to_sub(v), window // dil)
        outs.append(o.reshape(b, dil, sub, h, dh).transpose(0, 2, 1, 3, 4).reshape(b, s, h, dh))
        lses.append(l.reshape(b, dil, sub, h).transpose(0, 2, 1, 3).reshape(b, s, h))
    wts = jax.nn.softmax(jnp.stack(lses, axis=0), axis=0)
    return jnp.sum(wts[..., None] * jnp.stack(outs, axis=0), axis=0)


def conformer_conv(c, w_dw, b_dw, g_ln, b_ln):
    a, g = jnp.split(c, 2, axis=-1)
    glu = a * jax.nn.sigmoid(g)
    y = lax.conv_general_dilated(glu, w_dw, window_strides=(1,),
                                 padding=[(CONV_WIDTH - 1, 0)],
                                 dimension_numbers=('NWC', 'WIO', 'NWC'),
                                 feature_group_count=CONV_CHANNELS)
    y = y + b_dw
    y = layer_norm(y, g_ln, b_ln)
    return jax.nn.silu(y)


def moe_ffn(xt, w_router, b_router, w_gate_up, b_gate_up, w_down, b_down):
    t, d = xt.shape
    g = MOE_BLOCK
    logits = (xt @ w_router + b_router).astype(jnp.float32)
    top_val, top_idx = lax.top_k(logits, TOP_K)
    gates = jax.nn.softmax(top_val, axis=-1)
    flat_e = top_idx.reshape(-1)
    flat_tok = jnp.repeat(jnp.arange(t, dtype=jnp.int32), TOP_K)
    flat_g = gates.reshape(-1)
    order = jnp.argsort(flat_e)
    se, stok, sg = flat_e[order], flat_tok[order], flat_g[order]
    counts = jnp.bincount(flat_e, length=N_EXPERTS)
    padded = (counts + g - 1) // g * g
    start = jnp.cumsum(counts) - counts
    pend = jnp.cumsum(padded)
    pstart = pend - padded
    dest = pstart[se] + jnp.arange(t * TOP_K, dtype=jnp.int32) - start[se]
    cap = t * TOP_K + N_EXPERTS * g
    buf_tok = jnp.full((cap,), t, dtype=jnp.int32).at[dest].set(stok)
    buf_g = jnp.zeros((cap,), jnp.float32).at[dest].set(sg)
    n_blk = cap // g
    blk_e = jnp.minimum(jnp.searchsorted(pend, jnp.arange(n_blk) * g, side='right'), N_EXPERTS - 1)
    x_pad = jnp.concatenate([xt, jnp.zeros((1, d), xt.dtype)], axis=0)
    xb = x_pad[buf_tok].reshape(n_blk, g, d)

    def expert_rows(args):
        rows, e = args
        gu = rows @ w_gate_up[e] + b_gate_up[e]
        gate = jnp.minimum(gu[:, :D_FF], SWIGLU_LIMIT)
        up = jnp.clip(gu[:, D_FF:], -SWIGLU_LIMIT, SWIGLU_LIMIT)
        hid = (up + 1.0) * (gate * jax.nn.sigmoid(gate * SWIGLU_ALPHA))
        return hid @ w_down[e] + b_down[e]

    yb = lax.map(expert_rows, (xb, blk_e))
    y = jax.ops.segment_sum(yb.reshape(cap, d).astype(jnp.float32) * buf_g[:, None],
                            buf_tok, num_segments=t + 1)[:t]
    return y.astype(xt.dtype)


def setup_inputs(seed: int = 0) -> dict:
    key = jax.random.key(seed)
    ks = jax.random.split(key, 24)
    f32 = jnp.float32
    nrm = lambda k, shape, scale: jax.random.normal(k, shape, f32) * scale
    gain = lambda k, shape: 1.0 + 0.05 * jax.random.normal(k, shape, f32)
    x = jax.random.normal(ks[0], (BATCH, SEQ, D_MODEL), f32)
    p = jax.random.normal(ks[1], (DEPTH, BATCH, SEQ, PLE_DIM), f32)
    offsets = jax.random.randint(ks[2], (BATCH, 1), 0, 4096, dtype=jnp.int32)
    positions = offsets + jnp.arange(SEQ, dtype=jnp.int32)[None, :]
    return {
        "x": x,
        "p": p,
        "positions": positions,
        "g_mix_pre": gain(ks[3], (DEPTH, D_MODEL)),
        "w_in": nrm(ks[4], (DEPTH, D_MODEL, IN_WIDTH), D_MODEL ** -0.5),
        "w_dw": nrm(ks[5], (DEPTH, CONV_WIDTH, 1, CONV_CHANNELS), CONV_WIDTH ** -0.5),
        "b_dw": nrm(ks[6], (DEPTH, CONV_CHANNELS), 0.01),
        "g_conv_ln": gain(ks[7], (DEPTH, CONV_CHANNELS)),
        "b_conv_ln": nrm(ks[8], (DEPTH, CONV_CHANNELS), 0.01),
        "w_out": nrm(ks[9], (DEPTH, MIX_WIDTH, D_MODEL), MIX_WIDTH ** -0.5),
        "g_mix_post": gain(ks[10], (DEPTH, D_MODEL)),
        "g_ffn_pre": gain(ks[11], (DEPTH, D_MODEL)),
        "w_router": nrm(ks[12], (DEPTH, D_MODEL, N_EXPERTS), D_MODEL ** -0.5),
        "b_router": nrm(ks[13], (DEPTH, N_EXPERTS), 0.01),
        "w_gate_up": nrm(ks[14], (DEPTH, N_EXPERTS, D_MODEL, 2 * D_FF), D_MODEL ** -0.5),
        "b_gate_up": nrm(ks[15], (DEPTH, N_EXPERTS, 2 * D_FF), 0.01),
        "w_down": nrm(ks[16], (DEPTH, N_EXPERTS, D_FF, D_MODEL), D_FF ** -0.5),
        "b_down": nrm(ks[17], (DEPTH, N_EXPERTS, D_MODEL), 0.01),
        "g_ffn_post": gain(ks[18], (DEPTH, D_MODEL)),
        "w_ple": nrm(ks[19], (DEPTH, PLE_DIM, D_MODEL), PLE_DIM ** -0.5),
        "w_ple_gate": nrm(ks[20], (DEPTH, D_MODEL, D_MODEL), D_MODEL ** -0.5),
        "g_ple_post": gain(ks[21], (DEPTH, D_MODEL)),
    }


def reference(x, p, positions, g_mix_pre, w_in, w_dw, b_dw, g_conv_ln, b_conv_ln, w_out,
              g_mix_post, g_ffn_pre, w_router, b_router, w_gate_up, b_gate_up, w_down, b_down,
              g_ffn_post, w_ple, w_ple_gate, g_ple_post):
    b, s, d = x.shape
    h = x
    for i in range(DEPTH):
        u = rms_norm(h, g_mix_pre[i])
        proj = u @ w_in[i]
        q = proj[..., :ATTN_WIDTH].reshape(b, s, N_ATTN_HEADS, HEAD_DIM)
        k = proj[..., ATTN_WIDTH:2 * ATTN_WIDTH].reshape(b, s, N_ATTN_HEADS, HEAD_DIM)
        v = proj[..., 2 * ATTN_WIDTH:3 * ATTN_WIDTH].reshape(b, s, N_ATTN_HEADS, HEAD_DIM)
        c = proj[..., 3 * ATTN_WIDTH:]
        q = partial_rotary(q, positions)
        k = partial_rotary(k, positions)
        attn = dilated_attention(q, k, v).reshape(b, s, ATTN_WIDTH).astype(h.dtype)
        conv = conformer_conv(c, w_dw[i], b_dw[i], g_conv_ln[i], b_conv_ln[i])
        mix = jnp.concatenate([attn, conv.astype(h.dtype)], axis=-1) @ w_out[i]
        h = h + rms_norm(mix, g_mix_post[i])
        u = rms_norm(h, g_ffn_pre[i])
        ffn = moe_ffn(u.reshape(b * s, d), w_router[i], b_router[i], w_gate_up[i],
                      b_gate_up[i], w_down[i], b_down[i]).reshape(b, s, d)
        h = h + rms_norm(ffn, g_ffn_post[i])
        ple = (p[i] @ w_ple[i]) * jax.nn.sigmoid(h @ w_ple_gate[i])
        h = h + rms_norm(ple, g_ple_post[i])
    return h
```

```python
import functools

import numpy as np
import jax
import jax.numpy as jnp
from jax import lax
from jax.experimental import pallas as pl
from jax.experimental.pallas import tpu as pltpu

F32 = jnp.float32
BF16 = jnp.bfloat16

HEAD_DIM = 64
N_HEADS = 12
ATTN_WIDTH = N_HEADS * HEAD_DIM
CONV_CHANNELS = 256
CONV_WIDTH = 31
ROPE_DIM = HEAD_DIM // 4
ROPE_THETA = 500000.0
N_EXPERTS = 32
TOP_K = 4
SWIGLU_LIMIT = 7.0
SWIGLU_ALPHA = 1.702
NORM_EPS = 1e-6
WINDOW = 128
N_PLANES = 16
SPAN = N_PLANES * WINDOW
LANES = 128
NEG = -1e30
EXPERT_ROWS = 256
VMEM_LIMIT = 56 * 1024 * 1024


def _residue_of_plane(p):
    return 4 * (p % 4) + p // 4


def _rms(xv, g):
    var = jnp.mean(xv * xv, axis=-1, keepdims=True)
    return xv * lax.rsqrt(var + NORM_EPS) * g


def _rope_kernel(pos_ref, invf_ref, sgn_ref, c_ref, s_ref):
    ang = pos_ref[0].astype(F32) * invf_ref[...]
    c_ref[0] = jnp.cos(ang)
    s_ref[0] = jnp.sin(ang) * sgn_ref[...]


def _rope_tables(positions):
    b, s = positions.shape
    sm = s // N_PLANES
    mt = min(sm, 128)
    plane_res = np.array([_residue_of_plane(p) for p in range(N_PLANES)])
    pos_planes = positions.reshape(b, sm, N_PLANES).transpose(0, 2, 1)[:, plane_res]
    lane = np.arange(LANES) % HEAD_DIM
    inv_freq = ROPE_THETA ** (-jnp.arange(0, ROPE_DIM, 2, dtype=F32) / ROPE_DIM)
    invf = jnp.where(lane < ROPE_DIM, jnp.tile(inv_freq, LANES // (ROPE_DIM // 2)), 0.0)[None, :]
    sgn = jnp.asarray(np.where(lane < ROPE_DIM // 2, -1.0, 1.0), F32)[None, :]
    row = pl.BlockSpec((1, LANES), lambda i, j: (0, 0))
    out = pl.BlockSpec((1, N_PLANES, mt, LANES), lambda i, j: (i, 0, j, 0))
    return pl.pallas_call(
        _rope_kernel,
        grid=(b, sm // mt),
        in_specs=[pl.BlockSpec((1, N_PLANES, mt, 1), lambda i, j: (i, 0, j, 0)), row, row],
        out_specs=[out, out],
        out_shape=[jax.ShapeDtypeStruct((b, N_PLANES, sm, LANES), F32)] * 2,
        name="rope_tables",
    )(pos_planes[..., None], invf, sgn)


PERM_TOKENS = 256
PERM_ROWS = PERM_TOKENS // N_PLANES


def _plane_permutation():
    perm = np.zeros((PERM_TOKENS, PERM_TOKENS), np.float32)
    for p in range(N_PLANES):
        for ml in range(PERM_ROWS):
            perm[PERM_ROWS * p + ml, N_PLANES * ml + _residue_of_plane(p)] = 1.0
    return perm


def _inproj_kernel(x_ref, c_ref, s_ref, g_ref, perm_ref, wqkv_ref, wc_ref, q_ref, k_ref, v_ref, glu_ref):
    g = g_ref[...]
    tt = x_ref.shape[1]
    lane = lax.broadcasted_iota(jnp.int32, (1, LANES), 1) % HEAD_DIM
    first_half = lane < ROPE_DIM // 2

    def rotary(t, cos, sin):
        outs = []
        for j in range(ATTN_WIDTH // LANES):
            tj = t[:, j * LANES:(j + 1) * LANES]
            partner = jnp.where(first_half, pltpu.roll(tj, LANES - ROPE_DIM // 2, 1),
                                pltpu.roll(tj, ROPE_DIM // 2, 1))
            outs.append(tj * cos + partner * sin)
        return jnp.concatenate(outs, axis=1)

    un = _rms(x_ref[0], g).astype(BF16)
    pc = jnp.dot(un, wc_ref[...], preferred_element_type=F32)
    glu_ref[0] = pc[:, :CONV_CHANNELS] * jax.nn.sigmoid(pc[:, CONV_CHANNELS:])

    for sub in range(tt // PERM_TOKENS):
        rows = slice(sub * PERM_ROWS, (sub + 1) * PERM_ROWS)
        u = jnp.dot(perm_ref[...], un[sub * PERM_TOKENS:(sub + 1) * PERM_TOKENS],
                    preferred_element_type=F32).astype(BF16)
        cos = jnp.concatenate([c_ref[0, p, rows, :] for p in range(N_PLANES)], axis=0)
        sin = jnp.concatenate([s_ref[0, p, rows, :] for p in range(N_PLANES)], axis=0)
        proj = jnp.dot(u, wqkv_ref[...], preferred_element_type=F32)
        q = (rotary(proj[:, :ATTN_WIDTH], cos, sin) * (HEAD_DIM ** -0.5)).astype(BF16)
        k = rotary(proj[:, ATTN_WIDTH:2 * ATTN_WIDTH], cos, sin).astype(BF16)
        v = proj[:, 2 * ATTN_WIDTH:].astype(BF16)
        for p in range(N_PLANES):
            chunk = slice(p * PERM_ROWS, (p + 1) * PERM_ROWS)
            q_ref[0, p, rows, :] = q[chunk]
            k_ref[0, p, rows, :] = k[chunk]
            v_ref[0, p, rows, :] = v[chunk]


def _input_projection(x, cos_t, sin_t, g_mix_pre, w_in):
    b, s, d = x.shape
    tt = 1024
    mc = tt // N_PLANES
    wqkv = w_in[:, :3 * ATTN_WIDTH].astype(BF16)
    wc = w_in[:, 3 * ATTN_WIDTH:].astype(BF16)
    perm = jnp.asarray(_plane_permutation(), BF16)
    plane = lambda w: pl.BlockSpec((1, N_PLANES, mc, w), lambda i, j: (i, 0, j, 0))
    plane_shape = jax.ShapeDtypeStruct((b, N_PLANES, s // N_PLANES, ATTN_WIDTH), BF16)
    tok = lambda w: pl.BlockSpec((1, tt, w), lambda i, j: (i, j, 0))
    const = lambda shape: pl.BlockSpec(shape, lambda i, j: (0, 0))
    return pl.pallas_call(
        _inproj_kernel,
        grid=(b, s // tt),
        in_specs=[tok(d), plane(LANES), plane(LANES), const((1, d)), const(perm.shape),
                  const(wqkv.shape), const(wc.shape)],
        out_specs=[plane(ATTN_WIDTH), plane(ATTN_WIDTH), plane(ATTN_WIDTH), tok(CONV_CHANNELS)],
        out_shape=[plane_shape, plane_shape, plane_shape,
                   jax.ShapeDtypeStruct((b, s, CONV_CHANNELS), F32)],
        compiler_params=pltpu.CompilerParams(vmem_limit_bytes=VMEM_LIMIT),
        name="input_projection",
    )(x, cos_t, sin_t, g_mix_pre[None, :], perm, wqkv, wc)


def _attention_biases():
    mq = np.arange(WINDOW)[:, None]
    kj = np.arange(2 * WINDOW)[None, :]
    j16 = mq + WINDOW - kj
    prev16 = np.broadcast_to(kj < WINDOW, j16.shape)
    row = np.arange(128)[:, None]
    col = np.arange(256)[None, :]
    j4 = 4 * (row % 32 - (col % 64 - 32)) + row // 32 - col // 64
    prev4 = np.broadcast_to(col % 64 < 32, j4.shape)
    row = np.arange(256)[:, None]
    col = np.arange(512)[None, :]
    res = np.vectorize(_residue_of_plane)
    j1 = 16 * (row % 16 - (col % 32 - 16)) + res(row // 16) - res(col // 32)
    prev1 = np.broadcast_to(col % 32 < 16, j1.shape)
    band = lambda j: np.where((j >= 0) & (j <= WINDOW), 0.0, NEG).astype(np.float32)
    cols = lambda m: np.where(m, NEG, 0.0).astype(np.float32)
    return [jnp.asarray(a) for a in (band(j16), band(j4), band(j1), cols(prev16), cols(prev4), cols(prev1))]


def _attn_tile(qt, kt, vt, bias):
    s = lax.dot_general(qt, kt, (((1,), (1,)), ((), ())), preferred_element_type=F32) + bias
    m = jnp.max(s, axis=-1, keepdims=True)
    e = jnp.exp(s - m)
    l = jnp.sum(e, axis=-1, keepdims=True)
    o = jnp.dot(e.astype(BF16), vt, preferred_element_type=F32)
    return m, l, o


def _attn_kernel(q_ref, kc_ref, kp_ref, vc_ref, vp_ref, b16_ref, b4_ref, b1_ref,
                 p16_ref, p4_ref, p1_ref, o_ref, m_scr, l_scr, acc_scr):
    no_prev = (pl.program_id(1) == 0).astype(F32)
    bias16 = b16_ref[...] + no_prev * p16_ref[...]
    bias4_first = b4_ref[...] + no_prev * p4_ref[...]
    bias1_first = b1_ref[...] + no_prev * p1_ref[...]

    for hh in range(LANES // HEAD_DIM):
        hl = slice(hh * HEAD_DIM, (hh + 1) * HEAD_DIM)

        def put(branch, start, size, m, l, o, off):
            m_scr[branch, pl.ds(start, size), :] = m[off:off + size]
            l_scr[branch, pl.ds(start, size), :] = l[off:off + size]
            acc_scr[branch, pl.ds(start, size), :] = o[off:off + size]

        def body16(p, carry):
            kt = jnp.concatenate([kp_ref[0, p, :, hl], kc_ref[0, p, :, hl]], axis=0)
            vt = jnp.concatenate([vp_ref[0, p, :, hl], vc_ref[0, p, :, hl]], axis=0)
            m, l, o = _attn_tile(q_ref[0, p, :, hl], kt, vt, bias16)
            put(0, pl.multiple_of(p * WINDOW, WINDOW), WINDOW, m, l, o, 0)
            return carry
        lax.fori_loop(0, N_PLANES, body16, 0)

        def body4(c, carry):
            for i in range(4):
                qt = jnp.concatenate([q_ref[0, 4 * c + a, 32 * i:32 * i + 32, hl] for a in range(4)], axis=0)
                if i == 0:
                    ks = [x for a in range(4) for x in (kp_ref[0, 4 * c + a, 96:128, hl], kc_ref[0, 4 * c + a, 0:32, hl])]
                    vs = [x for a in range(4) for x in (vp_ref[0, 4 * c + a, 96:128, hl], vc_ref[0, 4 * c + a, 0:32, hl])]
                else:
                    ks = [kc_ref[0, 4 * c + a, 32 * i - 32:32 * i + 32, hl] for a in range(4)]
                    vs = [vc_ref[0, 4 * c + a, 32 * i - 32:32 * i + 32, hl] for a in range(4)]
                m, l, o = _attn_tile(qt, jnp.concatenate(ks, axis=0), jnp.concatenate(vs, axis=0),
                                     bias4_first if i == 0 else b4_ref[...])
                for a in range(4):
                    put(1, pl.multiple_of((4 * c + a) * WINDOW + 32 * i, 32), 32, m, l, o, 32 * a)
            return carry
        lax.fori_loop(0, 4, body4, 0)

        def tile1(i, first):
            rq = pl.ds(pl.multiple_of(16 * i, 16), 16)
            qt = jnp.concatenate([q_ref[0, p, rq, hl] for p in range(N_PLANES)], axis=0)
            if first:
                ks = [x for p in range(N_PLANES) for x in (kp_ref[0, p, 112:128, hl], kc_ref[0, p, 0:16, hl])]
                vs = [x for p in range(N_PLANES) for x in (vp_ref[0, p, 112:128, hl], vc_ref[0, p, 0:16, hl])]
            else:
                rk = pl.ds(pl.multiple_of(16 * i - 16, 16), 32)
                ks = [kc_ref[0, p, rk, hl] for p in range(N_PLANES)]
                vs = [vc_ref[0, p, rk, hl] for p in range(N_PLANES)]
            m, l, o = _attn_tile(qt, jnp.concatenate(ks, axis=0), jnp.concatenate(vs, axis=0),
                                 bias1_first if first else b1_ref[...])
            for p in range(N_PLANES):
                put(2, pl.multiple_of(p * WINDOW + 16 * i, 16), 16, m, l, o, 16 * p)

        tile1(0, True)

        def body1(i, carry):
            tile1(i, False)
            return carry
        lax.fori_loop(1, WINDOW // 16, body1, 0)

        def combine(p, carry):
            rows = pl.ds(pl.multiple_of(p * WINDOW, WINDOW), WINDOW)
            ms = [m_scr[b, rows, :] for b in range(3)]
            mx = jnp.maximum(jnp.maximum(ms[0], ms[1]), ms[2])
            ws = [jnp.exp(m - mx) for m in ms]
            den = ws[0] * l_scr[0, rows, :] + ws[1] * l_scr[1, rows, :] + ws[2] * l_scr[2, rows, :]
            num = ws[0] * acc_scr[0, rows, :] + ws[1] * acc_scr[1, rows, :] + ws[2] * acc_scr[2, rows, :]
            o_ref[0, p, :, hl] = (num / den).astype(BF16)
            return carry
        lax.fori_loop(0, N_PLANES, combine, 0)


def _dilated_attention(q, k, v):
    b, _, sm, _ = q.shape
    n_span = sm // WINDOW
    cur = pl.BlockSpec((1, N_PLANES, WINDOW, LANES), lambda i, j, h: (i, 0, j, h))
    prev = pl.BlockSpec((1, N_PLANES, WINDOW, LANES), lambda i, j, h: (i, 0, jnp.maximum(j - 1, 0), h))
    biases = _attention_biases()
    bias_specs = [pl.BlockSpec(a.shape, lambda i, j, h: (0, 0)) for a in biases]
    return pl.pallas_call(
        _attn_kernel,
        grid=(b, n_span, ATTN_WIDTH // LANES),
        in_specs=[cur, cur, prev, cur, prev] + bias_specs,
        out_specs=cur,
        out_shape=jax.ShapeDtypeStruct(q.shape, BF16),
        scratch_shapes=[pltpu.VMEM((3, SPAN, 1), F32), pltpu.VMEM((3, SPAN, 1), F32),
                        pltpu.VMEM((3, SPAN, HEAD_DIM), F32)],
        compiler_params=pltpu.CompilerParams(vmem_limit_bytes=VMEM_LIMIT),
        name="dilated_attention",
    )(q, k, k, v, v, *biases)


CONV_HALO = 32


def _conv_kernel(cur_ref, prev_ref, w_ref, b_ref, g_ref, bl_ref, o_ref, scr, *, chunk):
    tt = cur_ref.shape[1]
    has_prev = pl.program_id(1) > 0
    scr[0:CONV_HALO, :] = jnp.where(has_prev, prev_ref[0], 0.0)
    scr[CONV_HALO:CONV_HALO + tt, :] = cur_ref[0]
    lead = CONV_HALO - (CONV_WIDTH - 1)
    for c0 in range(0, tt, chunk):
        acc = jnp.zeros((chunk, CONV_CHANNELS), F32)
        for j in range(CONV_WIDTH):
            acc = acc + w_ref[j:j + 1, :] * scr[c0 + lead + j:c0 + lead + j + chunk, :]
        y = acc + b_ref[...]
        mu = jnp.mean(y, axis=-1, keepdims=True)
        var = jnp.mean(jnp.square(y - mu), axis=-1, keepdims=True)
        yn = (y - mu) * lax.rsqrt(var + NORM_EPS) * g_ref[...] + bl_ref[...]
        o_ref[0, c0:c0 + chunk, :] = (yn * jax.nn.sigmoid(yn)).astype(BF16)


def _conformer_conv(glu, w_dw, b_dw, g_ln, b_ln):
    b, s, c = glu.shape
    tt = 512
    row = pl.BlockSpec((1, c), lambda i, j: (0, 0))
    return pl.pallas_call(
        functools.partial(_conv_kernel, chunk=128),
        grid=(b, s // tt),
        in_specs=[pl.BlockSpec((1, tt, c), lambda i, j: (i, j, 0)),
                  pl.BlockSpec((1, CONV_HALO, c), lambda i, j: (i, jnp.maximum(j * (tt // CONV_HALO) - 1, 0), 0)),
                  pl.BlockSpec((CONV_WIDTH, c), lambda i, j: (0, 0)), row, row, row],
        out_specs=pl.BlockSpec((1, tt, c), lambda i, j: (i, j, 0)),
        out_shape=jax.ShapeDtypeStruct((b, s, c), BF16),
        scratch_shapes=[pltpu.VMEM((CONV_HALO + tt, c), F32)],
        name="conformer_conv",
    )(glu, glu, w_dw[:, 0, :], b_dw[None, :], g_ln[None, :], b_ln[None, :])


def _pack_bf16_pairs(u):
    w = u.shape[1] // 2
    ub = u.astype(BF16).astype(F32)
    lo = lax.bitcast_convert_type(ub[:, :w], jnp.uint32) >> 16
    hi = lax.bitcast_convert_type(ub[:, w:], jnp.uint32) & jnp.uint32(0xFFFF0000)
    return lo | hi


def _unpack_bf16_pairs(word):
    lo = lax.bitcast_convert_type(word << 16, F32).astype(BF16)
    hi = lax.bitcast_convert_type(word & jnp.uint32(0xFFFF0000), F32).astype(BF16)
    return lo, hi


def _outproj_kernel(attn_ref, conv_ref, x_ref, permt_ref, woa_ref, woc_ref, gpost_ref, gffn_ref, wrt_ref, br_ref,
                    h_ref, u_ref, idx_ref, gate_ref):
    tt = x_ref.shape[1]
    nat = []
    for sub in range(tt // PERM_TOKENS):
        rows = slice(sub * PERM_ROWS, (sub + 1) * PERM_ROWS)
        a = jnp.concatenate([attn_ref[0, p, rows, :] for p in range(N_PLANES)], axis=0)
        nat.append(jnp.dot(permt_ref[...], a, preferred_element_type=F32).astype(BF16))
    mix = (jnp.dot(jnp.concatenate(nat, axis=0), woa_ref[...], preferred_element_type=F32)
           + jnp.dot(conv_ref[0], woc_ref[...], preferred_element_type=F32))
    h = x_ref[0] + _rms(mix, gpost_ref[...])
    h_ref[0] = h
    u = _rms(h, gffn_ref[...])
    u_ref[...] = _pack_bf16_pairs(u)
    logits = lax.dot_general(wrt_ref[...], u, (((1,), (1,)), ((), ())), precision=lax.Precision.HIGHEST,
                             preferred_element_type=F32) + br_ref[...]
    rows = lax.broadcasted_iota(jnp.int32, logits.shape, 0)
    vals = logits
    tops, idxs = [], []
    for _ in range(TOP_K):
        mx = jnp.max(vals, axis=0, keepdims=True)
        ix = jnp.min(jnp.where(vals == mx, rows, N_EXPERTS), axis=0, keepdims=True)
        tops.append(mx)
        idxs.append(ix)
        vals = jnp.where(rows == ix, -jnp.inf, vals)
    ex = [jnp.exp(t - tops[0]) for t in tops]
    den = ex[0] + ex[1] + ex[2] + ex[3]
    idx_ref[...] = jnp.concatenate(idxs, axis=0)
    gate_ref[...] = jnp.concatenate([e / den for e in ex] + [jnp.zeros((8 - TOP_K, tt), F32)], axis=0)


def _output_projection(attn, conv, x, w_out, g_mix_post, g_ffn_pre, w_router, b_router):
    b, s, d = x.shape
    tt = 512
    mc = tt // N_PLANES
    n_t = s // tt
    woa = w_out[:ATTN_WIDTH].astype(BF16)
    woc = w_out[ATTN_WIDTH:].astype(BF16)
    permt = jnp.asarray(_plane_permutation().T, BF16)
    const = lambda shape: pl.BlockSpec(shape, lambda i, j: (0, 0))
    flat = lambda w: pl.BlockSpec((tt, w), lambda i, j: (i * n_t + j, 0))
    lanes = lambda r: pl.BlockSpec((r, tt), lambda i, j: (0, i * n_t + j))
    return pl.pallas_call(
        _outproj_kernel,
        grid=(b, n_t),
        in_specs=[pl.BlockSpec((1, N_PLANES, mc, ATTN_WIDTH), lambda i, j: (i, 0, j, 0)),
                  pl.BlockSpec((1, tt, CONV_CHANNELS), lambda i, j: (i, j, 0)),
                  pl.BlockSpec((1, tt, d), lambda i, j: (i, j, 0)),
                  const(permt.shape), const(woa.shape), const(woc.shape), const((1, d)), const((1, d)),
                  const((N_EXPERTS, d)), const((N_EXPERTS, 1))],
        out_specs=[pl.BlockSpec((1, tt, d), lambda i, j: (i, j, 0)), flat(d // 2), lanes(TOP_K), lanes(8)],
        out_shape=[jax.ShapeDtypeStruct((b, s, d), F32),
                   jax.ShapeDtypeStruct((b * s, d // 2), jnp.uint32),
                   jax.ShapeDtypeStruct((TOP_K, b * s), jnp.int32),
                   jax.ShapeDtypeStruct((8, b * s), F32)],
        compiler_params=pltpu.CompilerParams(vmem_limit_bytes=VMEM_LIMIT),
        name="output_projection_router",
    )(attn, conv, x, permt, woa, woc, g_mix_post[None, :], g_ffn_pre[None, :], w_router.T, b_router[:, None])


def _rank_kernel(idx_ref, rank_ref, cnt_ref, carry):
    tt = idx_ref.shape[1]

    @pl.when(pl.program_id(0) == 0)
    def _():
        carry[...] = jnp.zeros_like(carry)

    rows = lax.broadcasted_iota(jnp.int32, (N_EXPERTS, tt), 0)
    hot = [rows == idx_ref[k:k + 1, :] for k in range(TOP_K)]
    memb = sum(h.astype(F32) for h in hot)
    earlier = (lax.broadcasted_iota(jnp.int32, (tt, tt), 0)
               < lax.broadcasted_iota(jnp.int32, (tt, tt), 1)).astype(BF16)
    before = jnp.dot(memb.astype(BF16), earlier, preferred_element_type=F32) + carry[...]
    rank_ref[...] = jnp.concatenate(
        [jnp.sum(jnp.where(h, before, 0.0), axis=0, keepdims=True) for h in hot], axis=0).astype(jnp.int32)
    carry[...] = carry[...] + jnp.sum(memb, axis=1, keepdims=True)
    cnt_ref[...] = carry[...].astype(jnp.int32)


def _routing_ranks(idx):
    _, t = idx.shape
    tt = 512
    return pl.pallas_call(
        _rank_kernel,
        grid=(t // tt,),
        in_specs=[pl.BlockSpec((TOP_K, tt), lambda i: (0, i))],
        out_specs=[pl.BlockSpec((TOP_K, tt), lambda i: (0, i)), pl.BlockSpec((N_EXPERTS, 1), lambda i: (0, 0))],
        out_shape=[jax.ShapeDtypeStruct((TOP_K, t), jnp.int32), jax.ShapeDtypeStruct((N_EXPERTS, 1), jnp.int32)],
        scratch_shapes=[pltpu.VMEM((N_EXPERTS, 1), F32)],
        compiler_params=pltpu.CompilerParams(dimension_semantics=("arbitrary",)),
        name="routing_ranks",
    )(idx)


def _dispatch_kernel(dest_ref, u_ref, xb_in, xb_out, sem):
    del xb_in
    tt = u_ref.shape[0]

    def body(t, carry):
        for k in range(TOP_K):
            pltpu.make_async_copy(u_ref.at[pl.ds(t, 1), :], xb_out.at[pl.ds(dest_ref[k, t], 1), :], sem).start()
        return carry
    lax.fori_loop(0, tt, body, 0)
    for k in range(TOP_K):
        pltpu.make_async_copy(u_ref, xb_out.at[pl.ds(0, tt), :], sem).wait()


def _dispatch(dest, u_packed, cap):
    t, w = u_packed.shape
    tt = 256
    xb0 = jnp.zeros((cap, w), jnp.uint32)
    return pl.pallas_call(
        _dispatch_kernel,
        grid=(t // tt,),
        in_specs=[pl.BlockSpec((TOP_K, tt), lambda i: (0, i), memory_space=pltpu.SMEM),
                  pl.BlockSpec((tt, w), lambda i: (i, 0)),
                  pl.BlockSpec(memory_space=pl.ANY)],
        out_specs=pl.BlockSpec(memory_space=pl.ANY),
        out_shape=jax.ShapeDtypeStruct((cap, w), jnp.uint32),
        scratch_shapes=[pltpu.SemaphoreType.DMA],
        input_output_aliases={2: 0},
        compiler_params=pltpu.CompilerParams(dimension_semantics=("arbitrary",)),
        name="moe_dispatch",
    )(dest, u_packed, xb0)


def _expert_kernel(blk_e_ref, nused_ref, xb_ref, wgu_ref, bgu_ref, wd_ref, bd_ref, yb_ref):
    del blk_e_ref
    d_ff = wd_ref.shape[1]
    half = xb_ref.shape[1]

    @pl.when(pl.program_id(0) < nused_ref[0])
    def _():
        lo, hi = _unpack_bf16_pairs(xb_ref[...])
        gu = (jnp.dot(lo, wgu_ref[0, :half, :], preferred_element_type=F32)
              + jnp.dot(hi, wgu_ref[0, half:, :], preferred_element_type=F32) + bgu_ref[0])
        gate = jnp.minimum(gu[:, :d_ff], SWIGLU_LIMIT)
        up = jnp.clip(gu[:, d_ff:], -SWIGLU_LIMIT, SWIGLU_LIMIT)
        hid = (up + 1.0) * (gate * jax.nn.sigmoid(gate * SWIGLU_ALPHA))
        yb_ref[...] = jnp.dot(hid.astype(BF16), wd_ref[0], preferred_element_type=F32) + bd_ref[0]

    @pl.when(pl.program_id(0) >= nused_ref[0])
    def _():
        yb_ref[...] = jnp.zeros_like(yb_ref)


def _experts(xb, blk_e, nused, w_gate_up, b_gate_up, w_down, b_down):
    cap, half = xb.shape
    e, d, ff2 = w_gate_up.shape
    d_ff = w_down.shape[1]
    n_blk = cap // EXPERT_ROWS
    blk = lambda j, be, nu: jnp.minimum(j, nu[0] - 1)
    grid_spec = pltpu.PrefetchScalarGridSpec(
        num_scalar_prefetch=2,
        grid=(n_blk,),
        in_specs=[pl.BlockSpec((EXPERT_ROWS, half), lambda j, be, nu: (blk(j, be, nu), 0)),
                  pl.BlockSpec((1, d, ff2), lambda j, be, nu: (be[blk(j, be, nu)], 0, 0)),
                  pl.BlockSpec((1, 1, ff2), lambda j, be, nu: (be[blk(j, be, nu)], 0, 0)),
                  pl.BlockSpec((1, d_ff, d), lambda j, be, nu: (be[blk(j, be, nu)], 0, 0)),
                  pl.BlockSpec((1, 1, d), lambda j, be, nu: (be[blk(j, be, nu)], 0, 0))],
        out_specs=pl.BlockSpec((EXPERT_ROWS, d), lambda j, be, nu: (j, 0)),
    )
    return pl.pallas_call(
        _expert_kernel,
        grid_spec=grid_spec,
        out_shape=jax.ShapeDtypeStruct((cap, d), F32),
        compiler_params=pltpu.CompilerParams(dimension_semantics=("arbitrary",), vmem_limit_bytes=VMEM_LIMIT),
        name="moe_experts",
    )(blk_e, nused, xb, w_gate_up.astype(BF16), b_gate_up[:, None, :], w_down.astype(BF16), b_down[:, None, :])


def _combine_kernel(dest_ref, gate_ref, h_ref, p_ref, yb_hbm, gffn_ref, wple_ref, wpg_ref, gple_ref,
                    o_ref, buf, sem):
    tt = h_ref.shape[0]

    def body(t, carry):
        for k in range(TOP_K):
            pltpu.make_async_copy(yb_hbm.at[pl.ds(dest_ref[k, t], 1), :], buf.at[k, pl.ds(t, 1), :], sem).start()
        return carry
    lax.fori_loop(0, tt, body, 0)
    for k in range(TOP_K):
        pltpu.make_async_copy(yb_hbm.at[pl.ds(0, tt), :], buf.at[k], sem).wait()
    gt = gate_ref[...].T
    y = gt[:, 0:1] * buf[0]
    for k in range(1, TOP_K):
        y = y + gt[:, k:k + 1] * buf[k]
    h2 = h_ref[...] + _rms(y, gffn_ref[...])
    ple = (jnp.dot(p_ref[...].astype(BF16), wple_ref[...], preferred_element_type=F32)
           * jax.nn.sigmoid(jnp.dot(h2.astype(BF16), wpg_ref[...], preferred_element_type=F32)))
    o_ref[...] = h2 + _rms(ple, gple_ref[...])


def _combine(dest, gates, h, p, yb, g_ffn_post, w_ple, w_ple_gate, g_ple_post):
    t, d = h.shape
    tt = 256
    const = lambda shape: pl.BlockSpec(shape, lambda i: (0, 0))
    return pl.pallas_call(
        _combine_kernel,
        grid=(t // tt,),
        in_specs=[pl.BlockSpec((TOP_K, tt), lambda i: (0, i), memory_space=pltpu.SMEM),
                  pl.BlockSpec((8, tt), lambda i: (0, i)),
                  pl.BlockSpec((tt, d), lambda i: (i, 0)),
                  pl.BlockSpec((tt, p.shape[1]), lambda i: (i, 0)),
                  pl.BlockSpec(memory_space=pl.ANY),
                  const((1, d)), const(w_ple.shape), const(w_ple_gate.shape), const((1, d))],
        out_specs=pl.BlockSpec((tt, d), lambda i: (i, 0)),
        out_shape=jax.ShapeDtypeStruct((t, d), F32),
        scratch_shapes=[pltpu.VMEM((TOP_K, tt, d), F32), pltpu.SemaphoreType.DMA],
        compiler_params=pltpu.CompilerParams(dimension_semantics=("arbitrary",), vmem_limit_bytes=VMEM_LIMIT),
        name="moe_combine_ple",
    )(dest, gates, h, p, yb, g_ffn_post[None, :], w_ple.astype(BF16), w_ple_gate.astype(BF16), g_ple_post[None, :])


def _layer(h, p, positions, g_mix_pre, w_in, w_dw, b_dw, g_conv_ln, b_conv_ln, w_out, g_mix_post, g_ffn_pre,
           w_router, b_router, w_gate_up, b_gate_up, w_down, b_down, g_ffn_post, w_ple, w_ple_gate, g_ple_post):
    b, s, d = h.shape
    t = b * s
    cos_t, sin_t = _rope_tables(positions)
    q, k, v, glu = _input_projection(h, cos_t, sin_t, g_mix_pre, w_in)
    attn = _dilated_attention(q, k, v)
    conv = _conformer_conv(glu, w_dw, b_dw, g_conv_ln, b_conv_ln)
    h1, u_packed, idx, gates = _output_projection(attn, conv, h, w_out, g_mix_post, g_ffn_pre, w_router, b_router)
    rank, counts = _routing_ranks(idx)
    padded = (counts[:, 0] + EXPERT_ROWS - 1) // EXPERT_ROWS * EXPERT_ROWS
    pend = jnp.cumsum(padded)
    pstart = pend - padded
    dest = pstart[idx] + rank
    cap = t * TOP_K + N_EXPERTS * EXPERT_ROWS
    n_blk = cap // EXPERT_ROWS
    blk_e = jnp.minimum(jnp.searchsorted(pend, jnp.arange(n_blk, dtype=jnp.int32) * EXPERT_ROWS, side="right"),
                        N_EXPERTS - 1).astype(jnp.int32)
    nused = (pend[-1:] // EXPERT_ROWS).astype(jnp.int32)
    xb = _dispatch(dest, u_packed, cap)
    yb = _experts(xb, blk_e, nused, w_gate_up, b_gate_up, w_down, b_down)
    out = _combine(dest, gates, h1.reshape(t, d), p.reshape(t, -1), yb, g_ffn_post, w_ple, w_ple_gate, g_ple_post)
    return out.reshape(b, s, d)


def kernel(x, p, positions, g_mix_pre, w_in, w_dw, b_dw, g_conv_ln, b_conv_ln, w_out, g_mix_post, g_ffn_pre,
           w_router, b_router, w_gate_up, b_gate_up, w_down, b_down, g_ffn_post, w_ple, w_ple_gate, g_ple_post):
    h = x
    for i in range(p.shape[0]):
        h = _layer(h, p[i], positions, g_mix_pre[i], w_in[i], w_dw[i], b_dw[i], g_conv_ln[i], b_conv_ln[i],
                   w_out[i], g_mix_post[i], g_ffn_pre[i], w_router[i], b_router[i], w_gate_up[i], b_gate_up[i],
                   w_down[i], b_down[i], g_ffn_post[i], w_ple[i], w_ple_gate[i], g_ple_post[i])
    return h
```

```python
import functools

import numpy as np
import jax
import jax.numpy as jnp
from jax import lax
from jax.experimental import pallas as pl
from jax.experimental.pallas import tpu as pltpu

F32 = jnp.float32
BF16 = jnp.bfloat16

HEAD_DIM = 64
N_HEADS = 12
ATTN_WIDTH = N_HEADS * HEAD_DIM
CONV_CHANNELS = 256
CONV_WIDTH = 31
ROPE_DIM = HEAD_DIM // 4
ROPE_THETA = 500000.0
N_EXPERTS = 32
TOP_K = 4
SWIGLU_LIMIT = 7.0
SWIGLU_ALPHA = 1.702
NORM_EPS = 1e-6
WINDOW = 128
N_PLANES = 16
SPAN = N_PLANES * WINDOW
LANES = 128
NEG = -1e30
EXPERT_ROWS = 256
VMEM_LIMIT = 56 * 1024 * 1024


def _residue_of_plane(p):
    return 4 * (p % 4) + p // 4


def _rms(xv, g):
    var = jnp.mean(xv * xv, axis=-1, keepdims=True)
    return xv * lax.rsqrt(var + NORM_EPS) * g


def _rope_kernel(pos_ref, invf_ref, sgn_ref, c_ref, s_ref):
    ang = pos_ref[0].astype(F32) * invf_ref[...]
    c_ref[0] = jnp.cos(ang)
    s_ref[0] = jnp.sin(ang) * sgn_ref[...]


def _rope_tables(positions):
    b, s = positions.shape
    sm = s // N_PLANES
    mt = min(sm, 128)
    plane_res = np.array([_residue_of_plane(p) for p in range(N_PLANES)])
    pos_planes = positions.reshape(b, sm, N_PLANES).transpose(0, 2, 1)[:, plane_res]
    lane = np.arange(LANES) % HEAD_DIM
    inv_freq = ROPE_THETA ** (-jnp.arange(0, ROPE_DIM, 2, dtype=F32) / ROPE_DIM)
    invf = jnp.where(lane < ROPE_DIM, jnp.tile(inv_freq, LANES // (ROPE_DIM // 2)), 0.0)[None, :]
    sgn = jnp.asarray(np.where(lane < ROPE_DIM // 2, -1.0, 1.0), F32)[None, :]
    row = pl.BlockSpec((1, LANES), lambda i, j: (0, 0))
    out = pl.BlockSpec((1, N_PLANES, mt, LANES), lambda i, j: (i, 0, j, 0))
    return pl.pallas_call(
        _rope_kernel,
        grid=(b, sm // mt),
        in_specs=[pl.BlockSpec((1, N_PLANES, mt, 1), lambda i, j: (i, 0, j, 0)), row, row],
        out_specs=[out, out],
        out_shape=[jax.ShapeDtypeStruct((b, N_PLANES, sm, LANES), F32)] * 2,
        name="rope_tables",
    )(pos_planes[..., None], invf, sgn)


PERM_TOKENS = 256
PERM_ROWS = PERM_TOKENS // N_PLANES


def _plane_permutation():
    perm = np.zeros((PERM_TOKENS, PERM_TOKENS), np.float32)
    for p in range(N_PLANES):
        for ml in range(PERM_ROWS):
            perm[PERM_ROWS * p + ml, N_PLANES * ml + _residue_of_plane(p)] = 1.0
    return perm


def _inproj_kernel(x_ref, c_ref, s_ref, g_ref, perm_ref, wqkv_ref, wc_ref, q_ref, k_ref, v_ref, glu_ref):
    g = g_ref[...]
    tt = x_ref.shape[1]
    lane = lax.broadcasted_iota(jnp.int32, (1, LANES), 1) % HEAD_DIM
    first_half = lane < ROPE_DIM // 2

    def rotary(t, cos, sin):
        outs = []
        for j in range(ATTN_WIDTH // LANES):
            tj = t[:, j * LANES:(j + 1) * LANES]
            partner = jnp.where(first_half, pltpu.roll(tj, LANES - ROPE_DIM // 2, 1),
                                pltpu.roll(tj, ROPE_DIM // 2, 1))
            outs.append(tj * cos + partner * sin)
        return jnp.concatenate(outs, axis=1)

    un = _rms(x_ref[0], g).astype(BF16)
    pc = jnp.dot(un, wc_ref[...], preferred_element_type=F32)
    glu_ref[0] = pc[:, :CONV_CHANNELS] * jax.nn.sigmoid(pc[:, CONV_CHANNELS:])

    for sub in range(tt // PERM_TOKENS):
        rows = slice(sub * PERM_ROWS, (sub + 1) * PERM_ROWS)
        u = jnp.dot(perm_ref[...], un[sub * PERM_TOKENS:(sub + 1) * PERM_TOKENS],
                    preferred_element_type=F32).astype(BF16)
        cos = jnp.concatenate([c_ref[0, p, rows, :] for p in range(N_PLANES)], axis=0)
        sin = jnp.concatenate([s_ref[0, p, rows, :] for p in range(N_PLANES)], axis=0)
        proj = jnp.dot(u, wqkv_ref[...], preferred_element_type=F32)
        q = (rotary(proj[:, :ATTN_WIDTH], cos, sin) * (HEAD_DIM ** -0.5)).astype(BF16)
        k = rotary(proj[:, ATTN_WIDTH:2 * ATTN_WIDTH], cos, sin).astype(BF16)
        v = proj[:, 2 * ATTN_WIDTH:].astype(BF16)
        for p in range(N_PLANES):
            chunk = slice(p * PERM_ROWS, (p + 1) * PERM_ROWS)
            q_ref[0, p, rows, :] = q[chunk]
            k_ref[0, p, rows, :] = k[chunk]
            v_ref[0, p, rows, :] = v[chunk]


def _input_projection(x, cos_t, sin_t, g_mix_pre, w_in):
    b, s, d = x.shape
    tt = 1024
    mc = tt // N_PLANES
    wqkv = w_in[:, :3 * ATTN_WIDTH].astype(BF16)
    wc = w_in[:, 3 * ATTN_WIDTH:].astype(BF16)
    perm = jnp.asarray(_plane_permutation(), BF16)
    plane = lambda w: pl.BlockSpec((1, N_PLANES, mc, w), lambda i, j: (i, 0, j, 0))
    plane_shape = jax.ShapeDtypeStruct((b, N_PLANES, s // N_PLANES, ATTN_WIDTH), BF16)
    tok = lambda w: pl.BlockSpec((1, tt, w), lambda i, j: (i, j, 0))
    const = lambda shape: pl.BlockSpec(shape, lambda i, j: (0, 0))
    return pl.pallas_call(
        _inproj_kernel,
        grid=(b, s // tt),
        in_specs=[tok(d), plane(LANES), plane(LANES), const((1, d)), const(perm.shape),
                  const(wqkv.shape), const(wc.shape)],
        out_specs=[plane(ATTN_WIDTH), plane(ATTN_WIDTH), plane(ATTN_WIDTH), tok(CONV_CHANNELS)],
        out_shape=[plane_shape, plane_shape, plane_shape,
                   jax.ShapeDtypeStruct((b, s, CONV_CHANNELS), F32)],
        compiler_params=pltpu.CompilerParams(vmem_limit_bytes=VMEM_LIMIT),
        name="input_projection",
    )(x, cos_t, sin_t, g_mix_pre[None, :], perm, wqkv, wc)


def _attention_biases():
    band = lambda j: np.where((j >= 0) & (j <= WINDOW), 0.0, NEG).astype(np.float32)
    cols = lambda m: np.where(m, NEG, 0.0).astype(np.float32)
    twice = lambda a: np.concatenate([a, a], axis=0)
    mq = np.arange(WINDOW)[:, None]
    kj = np.arange(2 * WINDOW)[None, :]
    j16 = mq + WINDOW - kj
    prev16 = kj < WINDOW
    row = np.arange(128)[:, None]
    col = np.arange(256)[None, :]
    j4 = 4 * (row % 32 - (col % 64 - 32)) + row // 32 - col // 64
    prev4 = col % 64 < 32
    row = np.arange(256)[:, None]
    col = np.arange(512)[None, :]
    res = np.vectorize(_residue_of_plane)
    j1 = 16 * (row % 16 - (col % 32 - 16)) + res(row // 16) - res(col // 32)
    prev1 = col % 32 < 16
    b1 = np.stack([twice(band(j1[:128])), twice(band(j1[128:]))])
    return [jnp.asarray(a) for a in (twice(band(j16)), twice(band(j4)), b1, cols(prev16), cols(prev4), cols(prev1))]


def _attn_kernel(q_ref, kc_ref, kp_ref, vc_ref, vp_ref, b16_ref, b4_ref, b1_ref, p16_ref, p4_ref, p1_ref,
                 o_ref, m_scr, l_scr, a_scr, bias16_scr):
    no_prev = (pl.program_id(1) == 0).astype(F32)
    head0 = lax.broadcasted_iota(jnp.int32, (1, LANES), 1) < HEAD_DIM
    bias16_scr[...] = b16_ref[...] + no_prev * p16_ref[...]

    def tile(qt, kt, vt, bias):
        n = qt.shape[0]
        zero = jnp.zeros_like(qt)
        q2 = jnp.concatenate([jnp.where(head0, qt, zero), jnp.where(head0, zero, qt)], axis=0)
        s = lax.dot_general(q2, kt, (((1,), (1,)), ((), ())), preferred_element_type=F32) + bias
        m = jnp.max(s, axis=-1, keepdims=True)
        e = jnp.exp(s - m).astype(BF16)
        va = jnp.concatenate([vt, jnp.ones_like(vt)], axis=1)
        o = jnp.dot(e, va, preferred_element_type=F32)
        pick = lambda top, bot: jnp.where(head0, top, bot)
        mm = pick(jnp.broadcast_to(m[:n], (n, LANES)), jnp.broadcast_to(m[n:], (n, LANES)))
        return mm, pick(o[:n, LANES:], o[n:, LANES:]), pick(o[:n, :LANES], o[n:, :LANES])

    def put(branch, start, size, stats, off):
        for scr, val in zip((m_scr, l_scr, a_scr), stats):
            scr[branch, pl.ds(start, size), :] = val[off:off + size]

    def body16(i, carry):
        for p in (2 * i, 2 * i + 1):
            kt = jnp.concatenate([kp_ref[0, p], kc_ref[0, p]], axis=0)
            vt = jnp.concatenate([vp_ref[0, p], vc_ref[0, p]], axis=0)
            put(0, pl.multiple_of(p * WINDOW, WINDOW), WINDOW, tile(q_ref[0, p], kt, vt, bias16_scr[...]), 0)
        return carry
    lax.fori_loop(0, N_PLANES // 2, body16, 0)

    def body4(c, carry):
        for i in range(4):
            qt = jnp.concatenate([q_ref[0, 4 * c + a, 32 * i:32 * i + 32, :] for a in range(4)], axis=0)
            if i == 0:
                ks = [x for a in range(4) for x in (kp_ref[0, 4 * c + a, 96:128, :], kc_ref[0, 4 * c + a, 0:32, :])]
                vs = [x for a in range(4) for x in (vp_ref[0, 4 * c + a, 96:128, :], vc_ref[0, 4 * c + a, 0:32, :])]
                bias = b4_ref[...] + no_prev * p4_ref[...]
            else:
                ks = [kc_ref[0, 4 * c + a, 32 * i - 32:32 * i + 32, :] for a in range(4)]
                vs = [vc_ref[0, 4 * c + a, 32 * i - 32:32 * i + 32, :] for a in range(4)]
                bias = b4_ref[...]
            stats = tile(qt, jnp.concatenate(ks, axis=0), jnp.concatenate(vs, axis=0), bias)
            for a in range(4):
                put(1, pl.multiple_of((4 * c + a) * WINDOW + 32 * i, 32), 32, stats, 32 * a)
        return carry
    lax.fori_loop(0, 4, body4, 0)

    def tile1(i, first):
        rq = pl.ds(pl.multiple_of(16 * i, 16), 16)
        if first:
            ks = [x for p in range(N_PLANES) for x in (kp_ref[0, p, 112:128, :], kc_ref[0, p, 0:16, :])]
            vs = [x for p in range(N_PLANES) for x in (vp_ref[0, p, 112:128, :], vc_ref[0, p, 0:16, :])]
        else:
            rk = pl.ds(pl.multiple_of(16 * i - 16, 16), 32)
            ks = [kc_ref[0, p, rk, :] for p in range(N_PLANES)]
            vs = [vc_ref[0, p, rk, :] for p in range(N_PLANES)]
        kt = jnp.concatenate(ks, axis=0)
        vt = jnp.concatenate(vs, axis=0)
        for half in range(2):
            planes = range(8 * half, 8 * half + 8)
            qt = jnp.concatenate([q_ref[0, p, rq, :] for p in planes], axis=0)
            bias = b1_ref[half] + no_prev * p1_ref[...] if first else b1_ref[half]
            stats = tile(qt, kt, vt, bias)
            for p in planes:
                put(2, pl.multiple_of(p * WINDOW + 16 * i, 16), 16, stats, 16 * (p - 8 * half))

    tile1(0, True)

    def body1(i, carry):
        tile1(i, False)
        return carry
    lax.fori_loop(1, WINDOW // 16, body1, 0)

    def combine(i, carry):
        for p in (2 * i, 2 * i + 1):
            rows = pl.ds(pl.multiple_of(p * WINDOW, WINDOW), WINDOW)
            ms = [m_scr[b, rows, :] for b in range(3)]
            mx = jnp.maximum(jnp.maximum(ms[0], ms[1]), ms[2])
            ws = [jnp.exp(m - mx) for m in ms]
            den = ws[0] * l_scr[0, rows, :] + ws[1] * l_scr[1, rows, :] + ws[2] * l_scr[2, rows, :]
            num = ws[0] * a_scr[0, rows, :] + ws[1] * a_scr[1, rows, :] + ws[2] * a_scr[2, rows, :]
            o_ref[0, p] = (num / den).astype(BF16)
        return carry
    lax.fori_loop(0, N_PLANES // 2, combine, 0)


def _dilated_attention(q, k, v):
    b, _, sm, _ = q.shape
    n_span = sm // WINDOW
    cur = pl.BlockSpec((1, N_PLANES, WINDOW, LANES), lambda i, j, h: (i, 0, j, h))
    prev = pl.BlockSpec((1, N_PLANES, WINDOW, LANES), lambda i, j, h: (i, 0, jnp.maximum(j - 1, 0), h))
    biases = _attention_biases()
    bias_specs = [pl.BlockSpec(a.shape, lambda i, j, h, nd=a.ndim: (0,) * nd) for a in biases]
    stats = pltpu.VMEM((3, SPAN, LANES), F32)
    return pl.pallas_call(
        _attn_kernel,
        grid=(b, n_span, ATTN_WIDTH // LANES),
        in_specs=[cur, cur, prev, cur, prev] + bias_specs,
        out_specs=cur,
        out_shape=jax.ShapeDtypeStruct(q.shape, BF16),
        scratch_shapes=[stats, stats, stats, pltpu.VMEM((2 * WINDOW, 2 * WINDOW), F32)],
        compiler_params=pltpu.CompilerParams(vmem_limit_bytes=VMEM_LIMIT),
        name="dilated_attention",
    )(q, k, k, v, v, *biases)


CONV_HALO = 32


def _conv_kernel(cur_ref, prev_ref, w_ref, b_ref, g_ref, bl_ref, o_ref, scr, *, chunk):
    tt = cur_ref.shape[1]
    has_prev = pl.program_id(1) > 0
    scr[0:CONV_HALO, :] = jnp.where(has_prev, prev_ref[0], 0.0)
    scr[CONV_HALO:CONV_HALO + tt, :] = cur_ref[0]
    lead = CONV_HALO - (CONV_WIDTH - 1)
    for c0 in range(0, tt, chunk):
        acc = jnp.zeros((chunk, CONV_CHANNELS), F32)
        for j in range(CONV_WIDTH):
            acc = acc + w_ref[j:j + 1, :] * scr[c0 + lead + j:c0 + lead + j + chunk, :]
        y = acc + b_ref[...]
        mu = jnp.mean(y, axis=-1, keepdims=True)
        var = jnp.mean(jnp.square(y - mu), axis=-1, keepdims=True)
        yn = (y - mu) * lax.rsqrt(var + NORM_EPS) * g_ref[...] + bl_ref[...]
        o_ref[0, c0:c0 + chunk, :] = (yn * jax.nn.sigmoid(yn)).astype(BF16)


def _conformer_conv(glu, w_dw, b_dw, g_ln, b_ln):
    b, s, c = glu.shape
    tt = 512
    row = pl.BlockSpec((1, c), lambda i, j: (0, 0))
    return pl.pallas_call(
        functools.partial(_conv_kernel, chunk=128),
        grid=(b, s // tt),
        in_specs=[pl.BlockSpec((1, tt, c), lambda i, j: (i, j, 0)),
                  pl.BlockSpec((1, CONV_HALO, c), lambda i, j: (i, jnp.maximum(j * (tt // CONV_HALO) - 1, 0), 0)),
                  pl.BlockSpec((CONV_WIDTH, c), lambda i, j: (0, 0)), row, row, row],
        out_specs=pl.BlockSpec((1, tt, c), lambda i, j: (i, j, 0)),
        out_shape=jax.ShapeDtypeStruct((b, s, c), BF16),
        scratch_shapes=[pltpu.VMEM((CONV_HALO + tt, c), F32)],
        name="conformer_conv",
    )(glu, glu, w_dw[:, 0, :], b_dw[None, :], g_ln[None, :], b_ln[None, :])


def _outproj_kernel(attn_ref, conv_ref, x_ref, permt_ref, woa_ref, woc_ref, gpost_ref, gffn_ref, wrt_ref, br_ref,
                    h_ref, u_ref, idx_ref, gate_ref):
    tt = x_ref.shape[1]
    nat = []
    for sub in range(tt // PERM_TOKENS):
        rows = slice(sub * PERM_ROWS, (sub + 1) * PERM_ROWS)
        a = jnp.concatenate([attn_ref[0, p, rows, :] for p in range(N_PLANES)], axis=0)
        nat.append(jnp.dot(permt_ref[...], a, preferred_element_type=F32).astype(BF16))
    mix = (jnp.dot(jnp.concatenate(nat, axis=0), woa_ref[...], preferred_element_type=F32)
           + jnp.dot(conv_ref[0], woc_ref[...], preferred_element_type=F32))
    h = x_ref[0] + _rms(mix, gpost_ref[...])
    h_ref[0] = h
    u = _rms(h, gffn_ref[...])
    u_ref[...] = u
    logits = lax.dot_general(wrt_ref[...], u, (((1,), (1,)), ((), ())), precision=lax.Precision.HIGHEST,
                             preferred_element_type=F32) + br_ref[...]
    rows = lax.broadcasted_iota(jnp.int32, logits.shape, 0)
    vals = logits
    tops, idxs = [], []
    for _ in range(TOP_K):
        mx = jnp.max(vals, axis=0, keepdims=True)
        ix = jnp.min(jnp.where(vals == mx, rows, N_EXPERTS), axis=0, keepdims=True)
        tops.append(mx)
        idxs.append(ix)
        vals = jnp.where(rows == ix, -jnp.inf, vals)
    ex = [jnp.exp(t - tops[0]) for t in tops]
    den = ex[0] + ex[1] + ex[2] + ex[3]
    idx_ref[...] = jnp.concatenate(idxs, axis=0)
    gate_ref[...] = jnp.concatenate([e / den for e in ex] + [jnp.zeros((8 - TOP_K, tt), F32)], axis=0)


def _output_projection(attn, conv, x, w_out, g_mix_post, g_ffn_pre, w_router, b_router):
    b, s, d = x.shape
    tt = 512
    mc = tt // N_PLANES
    n_t = s // tt
    woa = w_out[:ATTN_WIDTH].astype(BF16)
    woc = w_out[ATTN_WIDTH:].astype(BF16)
    permt = jnp.asarray(_plane_permutation().T, BF16)
    const = lambda shape: pl.BlockSpec(shape, lambda i, j: (0, 0))
    flat = lambda w: pl.BlockSpec((tt, w), lambda i, j: (i * n_t + j, 0))
    lanes = lambda r: pl.BlockSpec((r, tt), lambda i, j: (0, i * n_t + j))
    return pl.pallas_call(
        _outproj_kernel,
        grid=(b, n_t),
        in_specs=[pl.BlockSpec((1, N_PLANES, mc, ATTN_WIDTH), lambda i, j: (i, 0, j, 0)),
                  pl.BlockSpec((1, tt, CONV_CHANNELS), lambda i, j: (i, j, 0)),
                  pl.BlockSpec((1, tt, d), lambda i, j: (i, j, 0)),
                  const(permt.shape), const(woa.shape), const(woc.shape), const((1, d)), const((1, d)),
                  const((N_EXPERTS, d)), const((N_EXPERTS, 1))],
        out_specs=[pl.BlockSpec((1, tt, d), lambda i, j: (i, j, 0)), flat(d), lanes(TOP_K), lanes(8)],
        out_shape=[jax.ShapeDtypeStruct((b, s, d), F32),
                   jax.ShapeDtypeStruct((b * s, d), F32),
                   jax.ShapeDtypeStruct((TOP_K, b * s), jnp.int32),
                   jax.ShapeDtypeStruct((8, b * s), F32)],
        compiler_params=pltpu.CompilerParams(vmem_limit_bytes=VMEM_LIMIT),
        name="output_projection_router",
    )(attn, conv, x, permt, woa, woc, g_mix_post[None, :], g_ffn_pre[None, :], w_router.T, b_router[:, None])


def _route_kernel(idx_ref, dest_ref, pend_ref, blk_ref, carry, pstart):
    phase = pl.program_id(0)
    step = pl.program_id(1)
    tt = idx_ref.shape[1]
    rows = lax.broadcasted_iota(jnp.int32, (N_EXPERTS, tt), 0)
    hot = [rows == idx_ref[k:k + 1, :] for k in range(TOP_K)]
    memb = sum(h.astype(F32) for h in hot)

    @pl.when((phase == 0) & (step == 0))
    def _():
        carry[...] = jnp.zeros_like(carry)

    @pl.when((phase == 1) & (step == 0))
    def _():
        counts = carry[...]
        padded = jnp.floor((counts + (EXPERT_ROWS - 1)) * (1.0 / EXPERT_ROWS)) * EXPERT_ROWS
        tri = (lax.broadcasted_iota(jnp.int32, (N_EXPERTS, N_EXPERTS), 1)
               <= lax.broadcasted_iota(jnp.int32, (N_EXPERTS, N_EXPERTS), 0)).astype(F32)
        pend = jnp.dot(tri, padded, precision=lax.Precision.HIGHEST, preferred_element_type=F32)
        pstart[...] = pend - padded
        pend_ref[...] = pend.astype(jnp.int32)
        starts = lax.broadcasted_iota(jnp.int32, (N_EXPERTS, blk_ref.shape[1]), 1) * EXPERT_ROWS
        ended = (pend.astype(jnp.int32) <= starts).astype(jnp.int32)
        blk_ref[...] = jnp.minimum(jnp.sum(ended, axis=0, keepdims=True), N_EXPERTS - 1)
        carry[...] = jnp.zeros_like(carry)

    @pl.when(phase == 1)
    def _():
        earlier = (lax.broadcasted_iota(jnp.int32, (tt, tt), 0)
                   < lax.broadcasted_iota(jnp.int32, (tt, tt), 1)).astype(BF16)
        row = jnp.dot(memb.astype(BF16), earlier, preferred_element_type=F32) + (carry[...] + pstart[...])
        dest_ref[...] = jnp.concatenate(
            [jnp.sum(jnp.where(h, row, 0.0), axis=0, keepdims=True) for h in hot], axis=0).astype(jnp.int32)

    carry[...] = carry[...] + jnp.sum(memb, axis=1, keepdims=True)


def _routing(idx, n_blk):
    _, t = idx.shape
    tt = 512
    blk_lanes = -(-n_blk // LANES) * LANES
    return pl.pallas_call(
        _route_kernel,
        grid=(2, t // tt),
        in_specs=[pl.BlockSpec((TOP_K, tt), lambda ph, i: (0, i))],
        out_specs=[pl.BlockSpec((TOP_K, tt), lambda ph, i: (0, i * ph)),
                   pl.BlockSpec((N_EXPERTS, 1), lambda ph, i: (0, 0)),
                   pl.BlockSpec((1, blk_lanes), lambda ph, i: (0, 0))],
        out_shape=[jax.ShapeDtypeStruct((TOP_K, t), jnp.int32), jax.ShapeDtypeStruct((N_EXPERTS, 1), jnp.int32),
                   jax.ShapeDtypeStruct((1, blk_lanes), jnp.int32)],
        scratch_shapes=[pltpu.VMEM((N_EXPERTS, 1), F32), pltpu.VMEM((N_EXPERTS, 1), F32)],
        compiler_params=pltpu.CompilerParams(dimension_semantics=("arbitrary", "arbitrary")),
        name="moe_routing",
    )(idx)


def _dispatch_kernel(dest_ref, u_ref, xb_in, xb_out, sem):
    del xb_in
    tt = u_ref.shape[0]

    def body(t, carry):
        for k in range(TOP_K):
            pltpu.make_async_copy(u_ref.at[pl.ds(t, 1), :], xb_out.at[pl.ds(dest_ref[k, t], 1), :], sem).start()
        return carry
    lax.fori_loop(0, tt, body, 0)
    for k in range(TOP_K):
        pltpu.make_async_copy(u_ref, xb_out.at[pl.ds(0, tt), :], sem).wait()


def _dispatch(dest, u, cap):
    t, w = u.shape
    tt = 256
    xb0 = jnp.zeros((cap, w), F32)
    return pl.pallas_call(
        _dispatch_kernel,
        grid=(t // tt,),
        in_specs=[pl.BlockSpec((TOP_K, tt), lambda i: (0, i), memory_space=pltpu.SMEM),
                  pl.BlockSpec((tt, w), lambda i: (i, 0)),
                  pl.BlockSpec(memory_space=pl.ANY)],
        out_specs=pl.BlockSpec(memory_space=pl.ANY),
        out_shape=jax.ShapeDtypeStruct((cap, w), F32),
        scratch_shapes=[pltpu.SemaphoreType.DMA],
        input_output_aliases={2: 0},
        compiler_params=pltpu.CompilerParams(dimension_semantics=("arbitrary",)),
        name="moe_dispatch",
    )(dest, u, xb0)


def _expert_kernel(blk_e_ref, nused_ref, xb_ref, wgu_ref, bgu_ref, wd_ref, bd_ref, yb_ref):
    del blk_e_ref
    d_ff = wd_ref.shape[1]

    @pl.when(pl.program_id(0) < nused_ref[0])
    def _():
        gu = jnp.dot(xb_ref[...].astype(BF16), wgu_ref[0], preferred_element_type=F32) + bgu_ref[0]
        gate = jnp.minimum(gu[:, :d_ff], SWIGLU_LIMIT)
        up = jnp.clip(gu[:, d_ff:], -SWIGLU_LIMIT, SWIGLU_LIMIT)
        hid = (up + 1.0) * (gate * jax.nn.sigmoid(gate * SWIGLU_ALPHA))
        yb_ref[...] = jnp.dot(hid.astype(BF16), wd_ref[0], preferred_element_type=F32) + bd_ref[0]

    @pl.when(pl.program_id(0) >= nused_ref[0])
    def _():
        yb_ref[...] = jnp.zeros_like(yb_ref)


def _experts(xb, blk_e, nused, w_gate_up, b_gate_up, w_down, b_down):
    cap, _ = xb.shape
    e, d, ff2 = w_gate_up.shape
    d_ff = w_down.shape[1]
    n_blk = cap // EXPERT_ROWS
    blk = lambda j, be, nu: jnp.minimum(j, nu[0] - 1)
    grid_spec = pltpu.PrefetchScalarGridSpec(
        num_scalar_prefetch=2,
        grid=(n_blk,),
        in_specs=[pl.BlockSpec((EXPERT_ROWS, d), lambda j, be, nu: (blk(j, be, nu), 0)),
                  pl.BlockSpec((1, d, ff2), lambda j, be, nu: (be[blk(j, be, nu)], 0, 0)),
                  pl.BlockSpec((1, 1, ff2), lambda j, be, nu: (be[blk(j, be, nu)], 0, 0)),
                  pl.BlockSpec((1, d_ff, d), lambda j, be, nu: (be[blk(j, be, nu)], 0, 0)),
                  pl.BlockSpec((1, 1, d), lambda j, be, nu: (be[blk(j, be, nu)], 0, 0))],
        out_specs=pl.BlockSpec((EXPERT_ROWS, d), lambda j, be, nu: (j, 0)),
    )
    return pl.pallas_call(
        _expert_kernel,
        grid_spec=grid_spec,
        out_shape=jax.ShapeDtypeStruct((cap, d), F32),
        compiler_params=pltpu.CompilerParams(dimension_semantics=("arbitrary",), vmem_limit_bytes=VMEM_LIMIT),
        name="moe_experts",
    )(blk_e, nused, xb, w_gate_up.astype(BF16), b_gate_up[:, None, :], w_down.astype(BF16), b_down[:, None, :])


def _combine_kernel(dest_ref, gate_ref, h_ref, p_ref, yb_hbm, gffn_ref, wple_ref, wpg_ref, gple_ref,
                    o_ref, buf, sem):
    tt = h_ref.shape[0]

    def body(t, carry):
        for k in range(TOP_K):
            pltpu.make_async_copy(yb_hbm.at[pl.ds(dest_ref[k, t], 1), :], buf.at[k, pl.ds(t, 1), :], sem).start()
        return carry
    lax.fori_loop(0, tt, body, 0)
    for k in range(TOP_K):
        pltpu.make_async_copy(yb_hbm.at[pl.ds(0, tt), :], buf.at[k], sem).wait()
    gt = gate_ref[...].T
    y = gt[:, 0:1] * buf[0]
    for k in range(1, TOP_K):
        y = y + gt[:, k:k + 1] * buf[k]
    h2 = h_ref[...] + _rms(y, gffn_ref[...])
    ple = (jnp.dot(p_ref[...].astype(BF16), wple_ref[...], preferred_element_type=F32)
           * jax.nn.sigmoid(jnp.dot(h2.astype(BF16), wpg_ref[...], preferred_element_type=F32)))
    o_ref[...] = h2 + _rms(ple, gple_ref[...])


def _combine(dest, gates, h, p, yb, g_ffn_post, w_ple, w_ple_gate, g_ple_post):
    t, d = h.shape
    tt = 256
    const = lambda shape: pl.BlockSpec(shape, lambda i: (0, 0))
    return pl.pallas_call(
        _combine_kernel,
        grid=(t // tt,),
        in_specs=[pl.BlockSpec((TOP_K, tt), lambda i: (0, i), memory_space=pltpu.SMEM),
                  pl.BlockSpec((8, tt), lambda i: (0, i)),
                  pl.BlockSpec((tt, d), lambda i: (i, 0)),
                  pl.BlockSpec((tt, p.shape[1]), lambda i: (i, 0)),
                  pl.BlockSpec(memory_space=pl.ANY),
                  const((1, d)), const(w_ple.shape), const(w_ple_gate.shape), const((1, d))],
        out_specs=pl.BlockSpec((tt, d), lambda i: (i, 0)),
        out_shape=jax.ShapeDtypeStruct((t, d), F32),
        scratch_shapes=[pltpu.VMEM((TOP_K, tt, d), F32), pltpu.SemaphoreType.DMA],
        compiler_params=pltpu.CompilerParams(dimension_semantics=("arbitrary",), vmem_limit_bytes=VMEM_LIMIT),
        name="moe_combine_ple",
    )(dest, gates, h, p, yb, g_ffn_post[None, :], w_ple.astype(BF16), w_ple_gate.astype(BF16), g_ple_post[None, :])


def _layer(h, p, positions, g_mix_pre, w_in, w_dw, b_dw, g_conv_ln, b_conv_ln, w_out, g_mix_post, g_ffn_pre,
           w_router, b_router, w_gate_up, b_gate_up, w_down, b_down, g_ffn_post, w_ple, w_ple_gate, g_ple_post):
    b, s, d = h.shape
    t = b * s
    cos_t, sin_t = _rope_tables(positions)
    q, k, v, glu = _input_projection(h, cos_t, sin_t, g_mix_pre, w_in)
    attn = _dilated_attention(q, k, v)
    conv = _conformer_conv(glu, w_dw, b_dw, g_conv_ln, b_conv_ln)
    h1, u_ffn, idx, gates = _output_projection(attn, conv, h, w_out, g_mix_post, g_ffn_pre, w_router, b_router)
    cap = t * TOP_K + N_EXPERTS * EXPERT_ROWS
    n_blk = cap // EXPERT_ROWS
    dest, pend, blk = _routing(idx, n_blk)
    blk_e = blk[0, :n_blk]
    nused = pend[N_EXPERTS - 1] // EXPERT_ROWS
    xb = _dispatch(dest, u_ffn, cap)
    yb = _experts(xb, blk_e, nused, w_gate_up, b_gate_up, w_down, b_down)
    out = _combine(dest, gates, h1.reshape(t, d), p.reshape(t, -1), yb, g_ffn_post, w_ple, w_ple_gate, g_ple_post)
    return out.reshape(b, s, d)


def kernel(x, p, positions, g_mix_pre, w_in, w_dw, b_dw, g_conv_ln, b_conv_ln, w_out, g_mix_post, g_ffn_pre,
           w_router, b_router, w_gate_up, b_gate_up, w_down, b_down, g_ffn_post, w_ple, w_ple_gate, g_ple_post):
    h = x
    for i in range(p.shape[0]):
        h = _layer(h, p[i], positions, g_mix_pre[i], w_in[i], w_dw[i], b_dw[i], g_conv_ln[i], b_conv_ln[i],
                   w_out[i], g_mix_post[i], g_ffn_pre[i], w_router[i], b_router[i], w_gate_up[i], b_gate_up[i],
                   w_down[i], b_down[i], g_ffn_post[i], w_ple[i], w_ple_gate[i], g_ple_post[i])
    return h
```

```python
import functools

import numpy as np
import jax
import jax.numpy as jnp
from jax import lax
from jax.experimental import pallas as pl
from jax.experimental.pallas import tpu as pltpu

F32 = jnp.float32
BF16 = jnp.bfloat16

HEAD_DIM = 64
N_HEADS = 12
ATTN_WIDTH = N_HEADS * HEAD_DIM
CONV_CHANNELS = 256
CONV_WIDTH = 31
ROPE_DIM = HEAD_DIM // 4
ROPE_THETA = 500000.0
N_EXPERTS = 32
TOP_K = 4
SWIGLU_LIMIT = 7.0
SWIGLU_ALPHA = 1.702
NORM_EPS = 1e-6
WINDOW = 128
N_PLANES = 16
SPAN = N_PLANES * WINDOW
LANES = 128
NEG = -1e30
EXPERT_ROWS = 256
VMEM_LIMIT = 56 * 1024 * 1024


def _residue_of_plane(p):
    return 4 * (p % 4) + p // 4


def _rms(xv, g):
    var = jnp.mean(xv * xv, axis=-1, keepdims=True)
    return xv * lax.rsqrt(var + NORM_EPS) * g


def _rope_kernel(pos_ref, invf_ref, sgn_ref, c_ref, s_ref):
    ang = pos_ref[0].astype(F32) * invf_ref[...]
    c_ref[0] = jnp.cos(ang)
    s_ref[0] = jnp.sin(ang) * sgn_ref[...]


def _rope_tables(positions):
    b, s = positions.shape
    sm = s // N_PLANES
    mt = min(sm, 128)
    plane_res = np.array([_residue_of_plane(p) for p in range(N_PLANES)])
    pos_planes = positions.reshape(b, sm, N_PLANES).transpose(0, 2, 1)[:, plane_res]
    lane = np.arange(LANES) % HEAD_DIM
    inv_freq = ROPE_THETA ** (-jnp.arange(0, ROPE_DIM, 2, dtype=F32) / ROPE_DIM)
    invf = jnp.where(lane < ROPE_DIM, jnp.tile(inv_freq, LANES // (ROPE_DIM // 2)), 0.0)[None, :]
    sgn = jnp.asarray(np.where(lane < ROPE_DIM // 2, -1.0, 1.0), F32)[None, :]
    row = pl.BlockSpec((1, LANES), lambda i, j: (0, 0))
    out = pl.BlockSpec((1, N_PLANES, mt, LANES), lambda i, j: (i, 0, j, 0))
    return pl.pallas_call(
        _rope_kernel,
        grid=(b, sm // mt),
        in_specs=[pl.BlockSpec((1, N_PLANES, mt, 1), lambda i, j: (i, 0, j, 0)), row, row],
        out_specs=[out, out],
        out_shape=[jax.ShapeDtypeStruct((b, N_PLANES, sm, LANES), F32)] * 2,
        name="rope_tables",
    )(pos_planes[..., None], invf, sgn)


PERM_TOKENS = 256
PERM_ROWS = PERM_TOKENS // N_PLANES


def _plane_permutation():
    perm = np.zeros((PERM_TOKENS, PERM_TOKENS), np.float32)
    for p in range(N_PLANES):
        for ml in range(PERM_ROWS):
            perm[PERM_ROWS * p + ml, N_PLANES * ml + _residue_of_plane(p)] = 1.0
    return perm


def _inproj_kernel(x_ref, c_ref, s_ref, g_ref, perm_ref, wqkv_ref, wc_ref, q_ref, k_ref, v_ref, glu_ref):
    g = g_ref[...]
    tt = x_ref.shape[1]
    lane = lax.broadcasted_iota(jnp.int32, (1, LANES), 1) % HEAD_DIM
    first_half = lane < ROPE_DIM // 2

    def rotary(t, cos, sin):
        outs = []
        for j in range(ATTN_WIDTH // LANES):
            tj = t[:, j * LANES:(j + 1) * LANES]
            partner = jnp.where(first_half, pltpu.roll(tj, LANES - ROPE_DIM // 2, 1),
                                pltpu.roll(tj, ROPE_DIM // 2, 1))
            outs.append(tj * cos + partner * sin)
        return jnp.concatenate(outs, axis=1)

    un = _rms(x_ref[0], g).astype(BF16)
    pc = jnp.dot(un, wc_ref[...], preferred_element_type=F32)
    glu_ref[0] = pc[:, :CONV_CHANNELS] * jax.nn.sigmoid(pc[:, CONV_CHANNELS:])

    for sub in range(tt // PERM_TOKENS):
        rows = slice(sub * PERM_ROWS, (sub + 1) * PERM_ROWS)
        u = jnp.dot(perm_ref[...], un[sub * PERM_TOKENS:(sub + 1) * PERM_TOKENS],
                    preferred_element_type=F32).astype(BF16)
        cos = jnp.concatenate([c_ref[0, p, rows, :] for p in range(N_PLANES)], axis=0)
        sin = jnp.concatenate([s_ref[0, p, rows, :] for p in range(N_PLANES)], axis=0)
        proj = jnp.dot(u, wqkv_ref[...], preferred_element_type=F32)
        q = (rotary(proj[:, :ATTN_WIDTH], cos, sin) * (HEAD_DIM ** -0.5)).astype(BF16)
        k = rotary(proj[:, ATTN_WIDTH:2 * ATTN_WIDTH], cos, sin).astype(BF16)
        v = proj[:, 2 * ATTN_WIDTH:].astype(BF16)
        for p in range(N_PLANES):
            chunk = slice(p * PERM_ROWS, (p + 1) * PERM_ROWS)
            q_ref[0, p, rows, :] = q[chunk]
            k_ref[0, p, rows, :] = k[chunk]
            v_ref[0, p, rows, :] = v[chunk]


def _input_projection(x, cos_t, sin_t, g_mix_pre, w_in):
    b, s, d = x.shape
    tt = 1024
    mc = tt // N_PLANES
    wqkv = w_in[:, :3 * ATTN_WIDTH].astype(BF16)
    wc = w_in[:, 3 * ATTN_WIDTH:].astype(BF16)
    perm = jnp.asarray(_plane_permutation(), BF16)
    plane = lambda w: pl.BlockSpec((1, N_PLANES, mc, w), lambda i, j: (i, 0, j, 0))
    plane_shape = jax.ShapeDtypeStruct((b, N_PLANES, s // N_PLANES, ATTN_WIDTH), BF16)
    tok = lambda w: pl.BlockSpec((1, tt, w), lambda i, j: (i, j, 0))
    const = lambda shape: pl.BlockSpec(shape, lambda i, j: (0, 0))
    return pl.pallas_call(
        _inproj_kernel,
        grid=(b, s // tt),
        in_specs=[tok(d), plane(LANES), plane(LANES), const((1, d)), const(perm.shape),
                  const(wqkv.shape), const(wc.shape)],
        out_specs=[plane(ATTN_WIDTH), plane(ATTN_WIDTH), plane(ATTN_WIDTH), tok(CONV_CHANNELS)],
        out_shape=[plane_shape, plane_shape, plane_shape,
                   jax.ShapeDtypeStruct((b, s, CONV_CHANNELS), F32)],
        compiler_params=pltpu.CompilerParams(vmem_limit_bytes=VMEM_LIMIT),
        name="input_projection",
    )(x, cos_t, sin_t, g_mix_pre[None, :], perm, wqkv, wc)


def _attention_biases():
    band = lambda j: np.where((j >= 0) & (j <= WINDOW), 0.0, NEG).astype(np.float32)
    cols = lambda m: np.where(m, NEG, 0.0).astype(np.float32)
    twice = lambda a: np.concatenate([a, a], axis=0)
    mq = np.arange(WINDOW)[:, None]
    kj = np.arange(2 * WINDOW)[None, :]
    j16 = mq + WINDOW - kj
    prev16 = kj < WINDOW
    row = np.arange(128)[:, None]
    col = np.arange(256)[None, :]
    j4 = 4 * (row % 32 - (col % 64 - 32)) + row // 32 - col // 64
    prev4 = col % 64 < 32
    row = np.arange(256)[:, None]
    col = np.arange(512)[None, :]
    res = np.vectorize(_residue_of_plane)
    j1 = 16 * (row % 16 - (col % 32 - 16)) + res(row // 16) - res(col // 32)
    prev1 = col % 32 < 16
    b1 = np.stack([twice(band(j1[:128])), twice(band(j1[128:]))])
    return [jnp.asarray(a) for a in (twice(band(j16)), twice(band(j4)), b1, cols(prev16), cols(prev4), cols(prev1))]


def _attn_kernel(q_ref, kc_ref, kp_ref, vc_ref, vp_ref, b16_ref, b4_ref, b1_ref, p16_ref, p4_ref, p1_ref,
                 o_ref, m_scr, l_scr, a_scr, bias16_scr):
    no_prev = (pl.program_id(1) == 0).astype(F32)
    head0 = lax.broadcasted_iota(jnp.int32, (1, LANES), 1) < HEAD_DIM
    bias16_scr[...] = b16_ref[...] + no_prev * p16_ref[...]

    def tile(qt, kt, vt, bias):
        n = qt.shape[0]
        zero = jnp.zeros_like(qt)
        q2 = jnp.concatenate([jnp.where(head0, qt, zero), jnp.where(head0, zero, qt)], axis=0)
        s = lax.dot_general(q2, kt, (((1,), (1,)), ((), ())), preferred_element_type=F32) + bias
        m = jnp.max(s, axis=-1, keepdims=True)
        e = jnp.exp(s - m).astype(BF16)
        va = jnp.concatenate([vt, jnp.ones_like(vt)], axis=1)
        o = jnp.dot(e, va, preferred_element_type=F32)
        pick = lambda top, bot: jnp.where(head0, top, bot)
        mm = pick(jnp.broadcast_to(m[:n], (n, LANES)), jnp.broadcast_to(m[n:], (n, LANES)))
        return mm, pick(o[:n, LANES:], o[n:, LANES:]), pick(o[:n, :LANES], o[n:, :LANES])

    def put(branch, start, size, stats, off):
        for scr, val in zip((m_scr, l_scr, a_scr), stats):
            scr[branch, pl.ds(start, size), :] = val[off:off + size]

    def body16(i, carry):
        for p in (2 * i, 2 * i + 1):
            kt = jnp.concatenate([kp_ref[0, p], kc_ref[0, p]], axis=0)
            vt = jnp.concatenate([vp_ref[0, p], vc_ref[0, p]], axis=0)
            put(0, pl.multiple_of(p * WINDOW, WINDOW), WINDOW, tile(q_ref[0, p], kt, vt, bias16_scr[...]), 0)
        return carry
    lax.fori_loop(0, N_PLANES // 2, body16, 0)

    def body4(c, carry):
        for i in range(4):
            qt = jnp.concatenate([q_ref[0, 4 * c + a, 32 * i:32 * i + 32, :] for a in range(4)], axis=0)
            if i == 0:
                ks = [x for a in range(4) for x in (kp_ref[0, 4 * c + a, 96:128, :], kc_ref[0, 4 * c + a, 0:32, :])]
                vs = [x for a in range(4) for x in (vp_ref[0, 4 * c + a, 96:128, :], vc_ref[0, 4 * c + a, 0:32, :])]
                bias = b4_ref[...] + no_prev * p4_ref[...]
            else:
                ks = [kc_ref[0, 4 * c + a, 32 * i - 32:32 * i + 32, :] for a in range(4)]
                vs = [vc_ref[0, 4 * c + a, 32 * i - 32:32 * i + 32, :] for a in range(4)]
                bias = b4_ref[...]
            stats = tile(qt, jnp.concatenate(ks, axis=0), jnp.concatenate(vs, axis=0), bias)
            for a in range(4):
                put(1, pl.multiple_of((4 * c + a) * WINDOW + 32 * i, 32), 32, stats, 32 * a)
        return carry
    lax.fori_loop(0, 4, body4, 0)

    def tile1(i, first):
        rq = pl.ds(pl.multiple_of(16 * i, 16), 16)
        if first:
            ks = [x for p in range(N_PLANES) for x in (kp_ref[0, p, 112:128, :], kc_ref[0, p, 0:16, :])]
            vs = [x for p in range(N_PLANES) for x in (vp_ref[0, p, 112:128, :], vc_ref[0, p, 0:16, :])]
        else:
            rk = pl.ds(pl.multiple_of(16 * i - 16, 16), 32)
            ks = [kc_ref[0, p, rk, :] for p in range(N_PLANES)]
            vs = [vc_ref[0, p, rk, :] for p in range(N_PLANES)]
        kt = jnp.concatenate(ks, axis=0)
        vt = jnp.concatenate(vs, axis=0)
        for half in range(2):
            planes = range(8 * half, 8 * half + 8)
            qt = jnp.concatenate([q_ref[0, p, rq, :] for p in planes], axis=0)
            bias = b1_ref[half] + no_prev * p1_ref[...] if first else b1_ref[half]
            stats = tile(qt, kt, vt, bias)
            for p in planes:
                put(2, pl.multiple_of(p * WINDOW + 16 * i, 16), 16, stats, 16 * (p - 8 * half))

    tile1(0, True)

    def body1(i, carry):
        tile1(i, False)
        return carry
    lax.fori_loop(1, WINDOW // 16, body1, 0)

    def combine(i, carry):
        for p in (2 * i, 2 * i + 1):
            rows = pl.ds(pl.multiple_of(p * WINDOW, WINDOW), WINDOW)
            ms = [m_scr[b, rows, :] for b in range(3)]
            mx = jnp.maximum(jnp.maximum(ms[0], ms[1]), ms[2])
            ws = [jnp.exp(m - mx) for m in ms]
            den = ws[0] * l_scr[0, rows, :] + ws[1] * l_scr[1, rows, :] + ws[2] * l_scr[2, rows, :]
            num = ws[0] * a_scr[0, rows, :] + ws[1] * a_scr[1, rows, :] + ws[2] * a_scr[2, rows, :]
            o_ref[0, p] = (num / den).astype(BF16)
        return carry
    lax.fori_loop(0, N_PLANES // 2, combine, 0)


def _dilated_attention(q, k, v):
    b, _, sm, _ = q.shape
    n_span = sm // WINDOW
    cur = pl.BlockSpec((1, N_PLANES, WINDOW, LANES), lambda i, j, h: (i, 0, j, h))
    prev = pl.BlockSpec((1, N_PLANES, WINDOW, LANES), lambda i, j, h: (i, 0, jnp.maximum(j - 1, 0), h))
    biases = _attention_biases()
    bias_specs = [pl.BlockSpec(a.shape, lambda i, j, h, nd=a.ndim: (0,) * nd) for a in biases]
    stats = pltpu.VMEM((3, SPAN, LANES), F32)
    return pl.pallas_call(
        _attn_kernel,
        grid=(b, n_span, ATTN_WIDTH // LANES),
        in_specs=[cur, cur, prev, cur, prev] + bias_specs,
        out_specs=cur,
        out_shape=jax.ShapeDtypeStruct(q.shape, BF16),
        scratch_shapes=[stats, stats, stats, pltpu.VMEM((2 * WINDOW, 2 * WINDOW), F32)],
        compiler_params=pltpu.CompilerParams(vmem_limit_bytes=VMEM_LIMIT),
        name="dilated_attention",
    )(q, k, k, v, v, *biases)


CONV_HALO = 32


def _conv_kernel(cur_ref, prev_ref, w_ref, b_ref, g_ref, bl_ref, o_ref, scr, *, chunk):
    tt = cur_ref.shape[1]
    has_prev = pl.program_id(1) > 0
    scr[0:CONV_HALO, :] = jnp.where(has_prev, prev_ref[0], 0.0)
    scr[CONV_HALO:CONV_HALO + tt, :] = cur_ref[0]
    lead = CONV_HALO - (CONV_WIDTH - 1)
    for c0 in range(0, tt, chunk):
        acc = jnp.zeros((chunk, CONV_CHANNELS), F32)
        for j in range(CONV_WIDTH):
            acc = acc + w_ref[j:j + 1, :] * scr[c0 + lead + j:c0 + lead + j + chunk, :]
        y = acc + b_ref[...]
        mu = jnp.mean(y, axis=-1, keepdims=True)
        var = jnp.mean(jnp.square(y - mu), axis=-1, keepdims=True)
        yn = (y - mu) * lax.rsqrt(var + NORM_EPS) * g_ref[...] + bl_ref[...]
        o_ref[0, c0:c0 + chunk, :] = (yn * jax.nn.sigmoid(yn)).astype(BF16)


def _conformer_conv(glu, w_dw, b_dw, g_ln, b_ln):
    b, s, c = glu.shape
    tt = 512
    row = pl.BlockSpec((1, c), lambda i, j: (0, 0))
    return pl.pallas_call(
        functools.partial(_conv_kernel, chunk=128),
        grid=(b, s // tt),
        in_specs=[pl.BlockSpec((1, tt, c), lambda i, j: (i, j, 0)),
                  pl.BlockSpec((1, CONV_HALO, c), lambda i, j: (i, jnp.maximum(j * (tt // CONV_HALO) - 1, 0), 0)),
                  pl.BlockSpec((CONV_WIDTH, c), lambda i, j: (0, 0)), row, row, row],
        out_specs=pl.BlockSpec((1, tt, c), lambda i, j: (i, j, 0)),
        out_shape=jax.ShapeDtypeStruct((b, s, c), BF16),
        scratch_shapes=[pltpu.VMEM((CONV_HALO + tt, c), F32)],
        name="conformer_conv",
    )(glu, glu, w_dw[:, 0, :], b_dw[None, :], g_ln[None, :], b_ln[None, :])


def _outproj_kernel(attn_ref, conv_ref, x_ref, permt_ref, woa_ref, woc_ref, gpost_ref, gffn_ref, wrt_ref, br_ref,
                    h_ref, u_ref, idx_ref, gate_ref):
    tt = x_ref.shape[1]
    nat = []
    for sub in range(tt // PERM_TOKENS):
        rows = slice(sub * PERM_ROWS, (sub + 1) * PERM_ROWS)
        a = jnp.concatenate([attn_ref[0, p, rows, :] for p in range(N_PLANES)], axis=0)
        nat.append(jnp.dot(permt_ref[...], a, preferred_element_type=F32).astype(BF16))
    mix = (jnp.dot(jnp.concatenate(nat, axis=0), woa_ref[...], preferred_element_type=F32)
           + jnp.dot(conv_ref[0], woc_ref[...], preferred_element_type=F32))
    h = x_ref[0] + _rms(mix, gpost_ref[...])
    h_ref[0] = h
    u = _rms(h, gffn_ref[...])
    u_ref[...] = u
    logits = lax.dot_general(wrt_ref[...], u, (((1,), (1,)), ((), ())), precision=lax.Precision.HIGHEST,
                             preferred_element_type=F32) + br_ref[...]
    rows = lax.broadcasted_iota(jnp.int32, logits.shape, 0)
    vals = logits
    tops, idxs = [], []
    for _ in range(TOP_K):
        mx = jnp.max(vals, axis=0, keepdims=True)
        ix = jnp.min(jnp.where(vals == mx, rows, N_EXPERTS), axis=0, keepdims=True)
        tops.append(mx)
        idxs.append(ix)
        vals = jnp.where(rows == ix, -jnp.inf, vals)
    ex = [jnp.exp(t - tops[0]) for t in tops]
    den = ex[0] + ex[1] + ex[2] + ex[3]
    idx_ref[...] = jnp.concatenate(idxs, axis=0)
    gate_ref[...] = jnp.concatenate([e / den for e in ex] + [jnp.zeros((8 - TOP_K, tt), F32)], axis=0)


def _output_projection(attn, conv, x, w_out, g_mix_post, g_ffn_pre, w_router, b_router):
    b, s, d = x.shape
    tt = 512
    mc = tt // N_PLANES
    n_t = s // tt
    woa = w_out[:ATTN_WIDTH].astype(BF16)
    woc = w_out[ATTN_WIDTH:].astype(BF16)
    permt = jnp.asarray(_plane_permutation().T, BF16)
    const = lambda shape: pl.BlockSpec(shape, lambda i, j: (0, 0))
    flat = lambda w: pl.BlockSpec((tt, w), lambda i, j: (i * n_t + j, 0))
    lanes = lambda r: pl.BlockSpec((r, tt), lambda i, j: (0, i * n_t + j))
    return pl.pallas_call(
        _outproj_kernel,
        grid=(b, n_t),
        in_specs=[pl.BlockSpec((1, N_PLANES, mc, ATTN_WIDTH), lambda i, j: (i, 0, j, 0)),
                  pl.BlockSpec((1, tt, CONV_CHANNELS), lambda i, j: (i, j, 0)),
                  pl.BlockSpec((1, tt, d), lambda i, j: (i, j, 0)),
                  const(permt.shape), const(woa.shape), const(woc.shape), const((1, d)), const((1, d)),
                  const((N_EXPERTS, d)), const((N_EXPERTS, 1))],
        out_specs=[pl.BlockSpec((1, tt, d), lambda i, j: (i, j, 0)), flat(d), lanes(TOP_K), lanes(8)],
        out_shape=[jax.ShapeDtypeStruct((b, s, d), F32),
                   jax.ShapeDtypeStruct((b * s, d), F32),
                   jax.ShapeDtypeStruct((TOP_K, b * s), jnp.int32),
                   jax.ShapeDtypeStruct((8, b * s), F32)],
        compiler_params=pltpu.CompilerParams(vmem_limit_bytes=VMEM_LIMIT),
        name="output_projection_router",
    )(attn, conv, x, permt, woa, woc, g_mix_post[None, :], g_ffn_pre[None, :], w_router.T, b_router[:, None])


def _route_kernel(idx_ref, dest_ref, pend_ref, pad_ref, blk_ref, carry, pstart):
    phase = pl.program_id(0)
    step = pl.program_id(1)
    tt = idx_ref.shape[1]
    rows = lax.broadcasted_iota(jnp.int32, (N_EXPERTS, tt), 0)
    hot = [rows == idx_ref[k:k + 1, :] for k in range(TOP_K)]
    memb = sum(h.astype(F32) for h in hot)

    @pl.when((phase == 0) & (step == 0))
    def _():
        carry[...] = jnp.zeros_like(carry)

    @pl.when((phase == 1) & (step == 0))
    def _():
        counts = carry[...]
        padded = jnp.floor((counts + (EXPERT_ROWS - 1)) * (1.0 / EXPERT_ROWS)) * EXPERT_ROWS
        tri = (lax.broadcasted_iota(jnp.int32, (N_EXPERTS, N_EXPERTS), 1)
               <= lax.broadcasted_iota(jnp.int32, (N_EXPERTS, N_EXPERTS), 0)).astype(F32)
        pend = jnp.dot(tri, padded, precision=lax.Precision.HIGHEST, preferred_element_type=F32)
        pstart[...] = pend - padded
        pend_ref[...] = pend.astype(jnp.int32)
        pad_ref[...] = (pend - padded + counts).astype(jnp.int32)
        starts = lax.broadcasted_iota(jnp.int32, (N_EXPERTS, blk_ref.shape[1]), 1) * EXPERT_ROWS
        ended = (pend.astype(jnp.int32) <= starts).astype(jnp.int32)
        blk_ref[...] = jnp.minimum(jnp.sum(ended, axis=0, keepdims=True), N_EXPERTS - 1)
        carry[...] = jnp.zeros_like(carry)

    @pl.when(phase == 1)
    def _():
        earlier = (lax.broadcasted_iota(jnp.int32, (tt, tt), 0)
                   < lax.broadcasted_iota(jnp.int32, (tt, tt), 1)).astype(BF16)
        row = jnp.dot(memb.astype(BF16), earlier, preferred_element_type=F32) + (carry[...] + pstart[...])
        dest_ref[...] = jnp.concatenate(
            [jnp.sum(jnp.where(h, row, 0.0), axis=0, keepdims=True) for h in hot], axis=0).astype(jnp.int32)

    carry[...] = carry[...] + jnp.sum(memb, axis=1, keepdims=True)


def _routing(idx, n_blk):
    _, t = idx.shape
    tt = 512
    blk_lanes = -(-n_blk // LANES) * LANES
    return pl.pallas_call(
        _route_kernel,
        grid=(2, t // tt),
        in_specs=[pl.BlockSpec((TOP_K, tt), lambda ph, i: (0, i))],
        out_specs=[pl.BlockSpec((TOP_K, tt), lambda ph, i: (0, i * ph)),
                   pl.BlockSpec((N_EXPERTS, 1), lambda ph, i: (0, 0)),
                   pl.BlockSpec((N_EXPERTS, 1), lambda ph, i: (0, 0)),
                   pl.BlockSpec((1, blk_lanes), lambda ph, i: (0, 0))],
        out_shape=[jax.ShapeDtypeStruct((TOP_K, t), jnp.int32), jax.ShapeDtypeStruct((N_EXPERTS, 1), jnp.int32),
                   jax.ShapeDtypeStruct((N_EXPERTS, 1), jnp.int32), jax.ShapeDtypeStruct((1, blk_lanes), jnp.int32)],
        scratch_shapes=[pltpu.VMEM((N_EXPERTS, 1), F32), pltpu.VMEM((N_EXPERTS, 1), F32)],
        compiler_params=pltpu.CompilerParams(dimension_semantics=("arbitrary", "arbitrary")),
        name="moe_routing",
    )(idx)


def _slot_kernel(pad_ref, pend_ref, dest_ref, slot_ref, *, n_tok):
    i = pl.program_id(0)
    tt = dest_ref.shape[1]

    @pl.when(i == 0)
    def _():
        def fill(r, c):
            slot_ref[r] = TOP_K * n_tok + lax.rem(r, EXPERT_ROWS)
            return c

        def per_expert(e, carry):
            return lax.fori_loop(pad_ref[e], pend_ref[e], fill, carry)
        lax.fori_loop(0, N_EXPERTS, per_expert, 0)
        lax.fori_loop(pend_ref[N_EXPERTS - 1], slot_ref.shape[0], fill, 0)

    def body(g, carry):
        for u in range(8):
            tl = g * 8 + u
            for k in range(TOP_K):
                slot_ref[dest_ref[k, tl]] = k * n_tok + i * tt + tl
        return carry
    lax.fori_loop(0, tt // 8, body, 0)


def _row_slots(dest, pad, pend, cap):
    _, t = dest.shape
    tt = 1024
    smem = pl.BlockSpec(memory_space=pltpu.SMEM)
    return pl.pallas_call(
        functools.partial(_slot_kernel, n_tok=t),
        grid=(t // tt,),
        in_specs=[smem, smem, pl.BlockSpec((TOP_K, tt), lambda i: (0, i), memory_space=pltpu.SMEM)],
        out_specs=smem,
        out_shape=jax.ShapeDtypeStruct((cap,), jnp.int32),
        compiler_params=pltpu.CompilerParams(dimension_semantics=("arbitrary",)),
        name="moe_row_slots",
    )(pad, pend, dest)


DMA_GROUPS = 4


def _expert_kernel(blk_e_ref, nused_ref, slot_ref, u_hbm, wgu_ref, bgu_ref, wd_ref, bd_ref, yk_hbm,
                   xg0, xg1, yo0, yo1, gsem, ssem):
    del blk_e_ref
    j = pl.program_id(0)
    nused = nused_ref[0]
    n_tok = u_hbm.shape[0]
    d_ff = wd_ref.shape[1]
    spare = TOP_K * n_tok
    group = EXPERT_ROWS // DMA_GROUPS
    chunk = d_ff // DMA_GROUPS

    token_of = (lambda s: s & (n_tok - 1)) if n_tok & (n_tok - 1) == 0 else (lambda s: lax.rem(s, n_tok))

    def gather(block, buf, sem, rows):
        for r in rows:
            tok = token_of(slot_ref[block * EXPERT_ROWS + r])
            pltpu.make_async_copy(u_hbm.at[pl.ds(tok, 1), :], buf.at[pl.ds(r, 1), :], sem).start()

    def scatter(block, real, buf, sem, rows):
        for r in rows:
            s = jnp.where(real, slot_ref[block * EXPERT_ROWS + r], spare + r)
            pltpu.make_async_copy(buf.at[pl.ds(r, 1), :], yk_hbm.at[pl.ds(s, 1), :], sem).start()

    def wait_gather(buf, sem):
        pltpu.make_async_copy(u_hbm.at[pl.ds(0, EXPERT_ROWS), :], buf, sem).wait()

    def wait_scatter(buf, sem):
        pltpu.make_async_copy(buf, yk_hbm.at[pl.ds(0, EXPERT_ROWS), :], sem).wait()

    def step(xg_c, yo_c, gs_c, ss_c, xg_o, yo_o, gs_o, ss_o):
        wait_gather(xg_c, gs_c)
        nxt = jnp.minimum(j + 1, nused - 1)
        prv = jnp.maximum(j - 1, 0)
        x = xg_c[...].astype(BF16)
        acc = None
        for c in range(DMA_GROUPS):
            rows = range(c * group, (c + 1) * group)
            gather(nxt, xg_o, gs_o, rows)
            scatter(prv, j > 0, yo_o, ss_o, rows)
            cols = slice(c * chunk, (c + 1) * chunk)
            ucols = slice(d_ff + c * chunk, d_ff + (c + 1) * chunk)
            gate = jnp.minimum(jnp.dot(x, wgu_ref[0, :, cols], preferred_element_type=F32) + bgu_ref[0, :, cols],
                               SWIGLU_LIMIT)
            up = jnp.clip(jnp.dot(x, wgu_ref[0, :, ucols], preferred_element_type=F32) + bgu_ref[0, :, ucols],
                          -SWIGLU_LIMIT, SWIGLU_LIMIT)
            hid = (up + 1.0) * (gate * jax.nn.sigmoid(gate * SWIGLU_ALPHA))
            part = jnp.dot(hid.astype(BF16), wd_ref[0, cols, :], preferred_element_type=F32)
            acc = part if acc is None else acc + part
        yo_c[...] = acc + bd_ref[0]
        wait_scatter(yo_o, ss_o)

        @pl.when(j == nused - 1)
        def _():
            scatter(j, True, yo_c, ss_c, range(EXPERT_ROWS))
            wait_scatter(yo_c, ss_c)
            wait_gather(xg_o, gs_o)

    @pl.when(j == 0)
    def _():
        yo1[...] = jnp.zeros_like(yo1)
        gather(0, xg0, gsem.at[0], range(EXPERT_ROWS))

    even = (xg0, yo0, gsem.at[0], ssem.at[0])
    odd = (xg1, yo1, gsem.at[1], ssem.at[1])

    @pl.when((j < nused) & (j % 2 == 0))
    def _():
        step(*even, *odd)

    @pl.when((j < nused) & (j % 2 == 1))
    def _():
        step(*odd, *even)


def _experts(u, slot, blk_e, nused, w_gate_up, b_gate_up, w_down, b_down):
    t, d = u.shape
    e, _, ff2 = w_gate_up.shape
    d_ff = w_down.shape[1]
    n_blk = slot.shape[0] // EXPERT_ROWS
    expert = lambda j, be, nu, sl: (be[jnp.minimum(j, nu[0] - 1)], 0, 0)
    grid_spec = pltpu.PrefetchScalarGridSpec(
        num_scalar_prefetch=3,
        grid=(n_blk,),
        in_specs=[pl.BlockSpec(memory_space=pl.ANY),
                  pl.BlockSpec((1, d, ff2), expert), pl.BlockSpec((1, 1, ff2), expert),
                  pl.BlockSpec((1, d_ff, d), expert), pl.BlockSpec((1, 1, d), expert)],
        out_specs=pl.BlockSpec(memory_space=pl.ANY),
        scratch_shapes=[pltpu.VMEM((EXPERT_ROWS, d), F32)] * 4 + [pltpu.SemaphoreType.DMA((2,))] * 2,
    )
    return pl.pallas_call(
        _expert_kernel,
        grid_spec=grid_spec,
        out_shape=jax.ShapeDtypeStruct((TOP_K * t + EXPERT_ROWS, d), F32),
        compiler_params=pltpu.CompilerParams(dimension_semantics=("arbitrary",), vmem_limit_bytes=VMEM_LIMIT),
        name="moe_experts",
    )(blk_e, nused, slot, u, w_gate_up.astype(BF16), b_gate_up[:, None, :], w_down.astype(BF16), b_down[:, None, :])


def _combine_kernel(gate_ref, h_ref, p_ref, y0_ref, y1_ref, y2_ref, y3_ref, gffn_ref, wple_ref, wpg_ref, gple_ref,
                    o_ref):
    gt = gate_ref[...].T
    y = gt[:, 0:1] * y0_ref[...]
    for k, y_ref in enumerate((y1_ref, y2_ref, y3_ref), start=1):
        y = y + gt[:, k:k + 1] * y_ref[...]
    h2 = h_ref[...] + _rms(y, gffn_ref[...])
    ple = (jnp.dot(p_ref[...].astype(BF16), wple_ref[...], preferred_element_type=F32)
           * jax.nn.sigmoid(jnp.dot(h2.astype(BF16), wpg_ref[...], preferred_element_type=F32)))
    o_ref[...] = h2 + _rms(ple, gple_ref[...])


def _combine(gates, h, p, yk, g_ffn_post, w_ple, w_ple_gate, g_ple_post):
    t, d = h.shape
    tt = 256
    n_t = t // tt
    const = lambda shape: pl.BlockSpec(shape, lambda i: (0, 0))
    choice = lambda k: pl.BlockSpec((tt, d), lambda i: (k * n_t + i, 0))
    return pl.pallas_call(
        _combine_kernel,
        grid=(n_t,),
        in_specs=[pl.BlockSpec((8, tt), lambda i: (0, i)),
                  pl.BlockSpec((tt, d), lambda i: (i, 0)),
                  pl.BlockSpec((tt, p.shape[1]), lambda i: (i, 0)),
                  choice(0), choice(1), choice(2), choice(3),
                  const((1, d)), const(w_ple.shape), const(w_ple_gate.shape), const((1, d))],
        out_specs=pl.BlockSpec((tt, d), lambda i: (i, 0)),
        out_shape=jax.ShapeDtypeStruct((t, d), F32),
        compiler_params=pltpu.CompilerParams(vmem_limit_bytes=VMEM_LIMIT),
        name="moe_combine_ple",
    )(gates, h, p, yk, yk, yk, yk, g_ffn_post[None, :], w_ple.astype(BF16), w_ple_gate.astype(BF16),
      g_ple_post[None, :])


def _layer(h, p, positions, g_mix_pre, w_in, w_dw, b_dw, g_conv_ln, b_conv_ln, w_out, g_mix_post, g_ffn_pre,
           w_router, b_router, w_gate_up, b_gate_up, w_down, b_down, g_ffn_post, w_ple, w_ple_gate, g_ple_post):
    b, s, d = h.shape
    t = b * s
    cos_t, sin_t = _rope_tables(positions)
    q, k, v, glu = _input_projection(h, cos_t, sin_t, g_mix_pre, w_in)
    attn = _dilated_attention(q, k, v)
    conv = _conformer_conv(glu, w_dw, b_dw, g_conv_ln, b_conv_ln)
    h1, u_ffn, idx, gates = _output_projection(attn, conv, h, w_out, g_mix_post, g_ffn_pre, w_router, b_router)
    cap = t * TOP_K + N_EXPERTS * EXPERT_ROWS
    n_blk = cap // EXPERT_ROWS
    dest, pend, pad, blk = _routing(idx, n_blk)
    blk_e = blk[0, :n_blk]
    nused = pend[N_EXPERTS - 1] // EXPERT_ROWS
    slot = _row_slots(dest, pad[:, 0], pend[:, 0], cap)
    yk = _experts(u_ffn, slot, blk_e, nused, w_gate_up, b_gate_up, w_down, b_down)
    out = _combine(gates, h1.reshape(t, d), p.reshape(t, -1), yk, g_ffn_post, w_ple, w_ple_gate, g_ple_post)
    return out.reshape(b, s, d)


def kernel(x, p, positions, g_mix_pre, w_in, w_dw, b_dw, g_conv_ln, b_conv_ln, w_out, g_mix_post, g_ffn_pre,
           w_router, b_router, w_gate_up, b_gate_up, w_down, b_down, g_ffn_post, w_ple, w_ple_gate, g_ple_post):
    h = x
    for i in range(p.shape[0]):
        h = _layer(h, p[i], positions, g_mix_pre[i], w_in[i], w_dw[i], b_dw[i], g_conv_ln[i], b_conv_ln[i],
                   w_out[i], g_mix_post[i], g_ffn_pre[i], w_router[i], b_router[i], w_gate_up[i], b_gate_up[i],
                   w_down[i], b_down[i], g_ffn_post[i], w_ple[i], w_ple_gate[i], g_ple_post[i])
    return h
```

```python
import functools

import numpy as np
import jax
import jax.numpy as jnp
from jax import lax
from jax.experimental import pallas as pl
from jax.experimental.pallas import tpu as pltpu

F32 = jnp.float32
BF16 = jnp.bfloat16

HEAD_DIM = 64
N_HEADS = 12
ATTN_WIDTH = N_HEADS * HEAD_DIM
CONV_CHANNELS = 256
CONV_WIDTH = 31
ROPE_DIM = HEAD_DIM // 4
ROPE_THETA = 500000.0
N_EXPERTS = 32
TOP_K = 4
SWIGLU_LIMIT = 7.0
SWIGLU_ALPHA = 1.702
NORM_EPS = 1e-6
WINDOW = 128
N_PLANES = 16
SPAN = N_PLANES * WINDOW
LANES = 128
NEG = -1e30
EXPERT_ROWS = 256
VMEM_LIMIT = 56 * 1024 * 1024


def _residue_of_plane(p):
    return 4 * (p % 4) + p // 4


def _rms(xv, g):
    var = jnp.mean(xv * xv, axis=-1, keepdims=True)
    return xv * lax.rsqrt(var + NORM_EPS) * g


def _rope_kernel(pos_ref, invf_ref, sgn_ref, c_ref, s_ref):
    ang = pos_ref[0].astype(F32) * invf_ref[...]
    c_ref[0] = jnp.cos(ang)
    s_ref[0] = jnp.sin(ang) * sgn_ref[...]


def _rope_tables(positions):
    b, s = positions.shape
    sm = s // N_PLANES
    mt = min(sm, 128)
    plane_res = np.array([_residue_of_plane(p) for p in range(N_PLANES)])
    pos_planes = positions.reshape(b, sm, N_PLANES).transpose(0, 2, 1)[:, plane_res]
    lane = np.arange(LANES) % HEAD_DIM
    inv_freq = ROPE_THETA ** (-jnp.arange(0, ROPE_DIM, 2, dtype=F32) / ROPE_DIM)
    invf = jnp.where(lane < ROPE_DIM, jnp.tile(inv_freq, LANES // (ROPE_DIM // 2)), 0.0)[None, :]
    sgn = jnp.asarray(np.where(lane < ROPE_DIM // 2, -1.0, 1.0), F32)[None, :]
    row = pl.BlockSpec((1, LANES), lambda i, j: (0, 0))
    out = pl.BlockSpec((1, N_PLANES, mt, LANES), lambda i, j: (i, 0, j, 0))
    return pl.pallas_call(
        _rope_kernel,
        grid=(b, sm // mt),
        in_specs=[pl.BlockSpec((1, N_PLANES, mt, 1), lambda i, j: (i, 0, j, 0)), row, row],
        out_specs=[out, out],
        out_shape=[jax.ShapeDtypeStruct((b, N_PLANES, sm, LANES), F32)] * 2,
        name="rope_tables",
    )(pos_planes[..., None], invf, sgn)


PERM_TOKENS = 256
PERM_ROWS = PERM_TOKENS // N_PLANES


def _plane_permutation():
    perm = np.zeros((PERM_TOKENS, PERM_TOKENS), np.float32)
    for p in range(N_PLANES):
        for ml in range(PERM_ROWS):
            perm[PERM_ROWS * p + ml, N_PLANES * ml + _residue_of_plane(p)] = 1.0
    return perm


def _inproj_kernel(x_ref, c_ref, s_ref, g_ref, perm_ref, wqkv_ref, wc_ref, q_ref, k_ref, v_ref, glu_ref):
    g = g_ref[...]
    tt = x_ref.shape[1]
    lane = lax.broadcasted_iota(jnp.int32, (1, LANES), 1) % HEAD_DIM
    first_half = lane < ROPE_DIM // 2

    def rotary(t, cos, sin):
        outs = []
        for j in range(ATTN_WIDTH // LANES):
            tj = t[:, j * LANES:(j + 1) * LANES]
            partner = jnp.where(first_half, pltpu.roll(tj, LANES - ROPE_DIM // 2, 1),
                                pltpu.roll(tj, ROPE_DIM // 2, 1))
            outs.append(tj * cos + partner * sin)
        return jnp.concatenate(outs, axis=1)

    un = _rms(x_ref[0], g).astype(BF16)
    pc = jnp.dot(un, wc_ref[...], preferred_element_type=F32)
    glu_ref[0] = pc[:, :CONV_CHANNELS] * jax.nn.sigmoid(pc[:, CONV_CHANNELS:])

    for sub in range(tt // PERM_TOKENS):
        rows = slice(sub * PERM_ROWS, (sub + 1) * PERM_ROWS)
        u = jnp.dot(perm_ref[...], un[sub * PERM_TOKENS:(sub + 1) * PERM_TOKENS],
                    preferred_element_type=F32).astype(BF16)
        cos = jnp.concatenate([c_ref[0, p, rows, :] for p in range(N_PLANES)], axis=0)
        sin = jnp.concatenate([s_ref[0, p, rows, :] for p in range(N_PLANES)], axis=0)
        proj = jnp.dot(u, wqkv_ref[...], preferred_element_type=F32)
        q = (rotary(proj[:, :ATTN_WIDTH], cos, sin) * (HEAD_DIM ** -0.5)).astype(BF16)
        k = rotary(proj[:, ATTN_WIDTH:2 * ATTN_WIDTH], cos, sin).astype(BF16)
        v = proj[:, 2 * ATTN_WIDTH:].astype(BF16)
        for p in range(N_PLANES):
            chunk = slice(p * PERM_ROWS, (p + 1) * PERM_ROWS)
            q_ref[0, p, rows, :] = q[chunk]
            k_ref[0, p, rows, :] = k[chunk]
            v_ref[0, p, rows, :] = v[chunk]


def _input_projection(x, cos_t, sin_t, g_mix_pre, w_in):
    b, s, d = x.shape
    tt = 1024
    mc = tt // N_PLANES
    wqkv = w_in[:, :3 * ATTN_WIDTH].astype(BF16)
    wc = w_in[:, 3 * ATTN_WIDTH:].astype(BF16)
    perm = jnp.asarray(_plane_permutation(), BF16)
    plane = lambda w: pl.BlockSpec((1, N_PLANES, mc, w), lambda i, j: (i, 0, j, 0))
    plane_shape = jax.ShapeDtypeStruct((b, N_PLANES, s // N_PLANES, ATTN_WIDTH), BF16)
    tok = lambda w: pl.BlockSpec((1, tt, w), lambda i, j: (i, j, 0))
    const = lambda shape: pl.BlockSpec(shape, lambda i, j: (0, 0))
    return pl.pallas_call(
        _inproj_kernel,
        grid=(b, s // tt),
        in_specs=[tok(d), plane(LANES), plane(LANES), const((1, d)), const(perm.shape),
                  const(wqkv.shape), const(wc.shape)],
        out_specs=[plane(ATTN_WIDTH), plane(ATTN_WIDTH), plane(ATTN_WIDTH), tok(CONV_CHANNELS)],
        out_shape=[plane_shape, plane_shape, plane_shape,
                   jax.ShapeDtypeStruct((b, s, CONV_CHANNELS), F32)],
        compiler_params=pltpu.CompilerParams(vmem_limit_bytes=VMEM_LIMIT),
        name="input_projection",
    )(x, cos_t, sin_t, g_mix_pre[None, :], perm, wqkv, wc)


def _attention_biases():
    band = lambda j: np.where((j >= 0) & (j <= WINDOW), 0.0, NEG).astype(np.float32)
    cols = lambda m: np.where(m, NEG, 0.0).astype(np.float32)
    twice = lambda a: np.concatenate([a, a], axis=0)
    mq = np.arange(WINDOW)[:, None]
    kj = np.arange(2 * WINDOW)[None, :]
    j16 = mq + WINDOW - kj
    prev16 = kj < WINDOW
    row = np.arange(128)[:, None]
    col = np.arange(256)[None, :]
    j4 = 4 * (row % 32 - (col % 64 - 32)) + row // 32 - col // 64
    prev4 = col % 64 < 32
    row = np.arange(256)[:, None]
    col = np.arange(512)[None, :]
    res = np.vectorize(_residue_of_plane)
    j1 = 16 * (row % 16 - (col % 32 - 16)) + res(row // 16) - res(col // 32)
    prev1 = col % 32 < 16
    b1 = np.stack([twice(band(j1[:128])), twice(band(j1[128:]))])
    return [jnp.asarray(a) for a in (twice(band(j16)), twice(band(j4)), b1, cols(prev16), cols(prev4), cols(prev1))]


UNROLL = 4


def _attn_kernel(q_ref, kc_ref, kp_ref, vc_ref, vp_ref, b16_ref, b4_ref, b1_ref, p16_ref, p4_ref, p1_ref,
                 o_ref, m_scr, l_scr, a_scr, bias16_scr):
    no_prev = (pl.program_id(1) == 0).astype(F32)
    head0 = lax.broadcasted_iota(jnp.int32, (1, LANES), 1) < HEAD_DIM
    bias16_scr[...] = b16_ref[...] + no_prev * p16_ref[...]

    def tile(qt, kt, vt, bias):
        n = qt.shape[0]
        zero = jnp.zeros_like(qt)
        q2 = jnp.concatenate([jnp.where(head0, qt, zero), jnp.where(head0, zero, qt)], axis=0)
        s = lax.dot_general(q2, kt, (((1,), (1,)), ((), ())), preferred_element_type=F32) + bias
        m = jnp.max(s, axis=-1, keepdims=True)
        e = jnp.exp(s - m).astype(BF16)
        va = jnp.concatenate([vt, jnp.ones_like(vt)], axis=1)
        o = jnp.dot(e, va, preferred_element_type=F32)
        pick = lambda top, bot: jnp.where(head0, top, bot)
        mm = pick(jnp.broadcast_to(m[:n], (n, LANES)), jnp.broadcast_to(m[n:], (n, LANES)))
        return mm, pick(o[:n, LANES:], o[n:, LANES:]), pick(o[:n, :LANES], o[n:, :LANES])

    def put(branch, start, size, stats, off):
        for scr, val in zip((m_scr, l_scr, a_scr), stats):
            scr[branch, pl.ds(start, size), :] = val[off:off + size]

    def body16(i, carry):
        for p in [UNROLL * i + u for u in range(UNROLL)]:
            kt = jnp.concatenate([kp_ref[0, p], kc_ref[0, p]], axis=0)
            vt = jnp.concatenate([vp_ref[0, p], vc_ref[0, p]], axis=0)
            put(0, pl.multiple_of(p * WINDOW, WINDOW), WINDOW, tile(q_ref[0, p], kt, vt, bias16_scr[...]), 0)
        return carry
    lax.fori_loop(0, N_PLANES // UNROLL, body16, 0)

    def body4(g, carry):
        for c, i in [(2 * g + cc, ii) for cc in range(2) for ii in range(4)]:
            qt = jnp.concatenate([q_ref[0, 4 * c + a, 32 * i:32 * i + 32, :] for a in range(4)], axis=0)
            if i == 0:
                ks = [x for a in range(4) for x in (kp_ref[0, 4 * c + a, 96:128, :], kc_ref[0, 4 * c + a, 0:32, :])]
                vs = [x for a in range(4) for x in (vp_ref[0, 4 * c + a, 96:128, :], vc_ref[0, 4 * c + a, 0:32, :])]
                bias = b4_ref[...] + no_prev * p4_ref[...]
            else:
                ks = [kc_ref[0, 4 * c + a, 32 * i - 32:32 * i + 32, :] for a in range(4)]
                vs = [vc_ref[0, 4 * c + a, 32 * i - 32:32 * i + 32, :] for a in range(4)]
                bias = b4_ref[...]
            stats = tile(qt, jnp.concatenate(ks, axis=0), jnp.concatenate(vs, axis=0), bias)
            for a in range(4):
                put(1, pl.multiple_of((4 * c + a) * WINDOW + 32 * i, 32), 32, stats, 32 * a)
        return carry
    lax.fori_loop(0, 2, body4, 0)

    def tile1(i, first):
        rq = pl.ds(pl.multiple_of(16 * i, 16), 16)
        if first:
            ks = [x for p in range(N_PLANES) for x in (kp_ref[0, p, 112:128, :], kc_ref[0, p, 0:16, :])]
            vs = [x for p in range(N_PLANES) for x in (vp_ref[0, p, 112:128, :], vc_ref[0, p, 0:16, :])]
        else:
            rk = pl.ds(pl.multiple_of(16 * i - 16, 16), 32)
            ks = [kc_ref[0, p, rk, :] for p in range(N_PLANES)]
            vs = [vc_ref[0, p, rk, :] for p in range(N_PLANES)]
        kt = jnp.concatenate(ks, axis=0)
        vt = jnp.concatenate(vs, axis=0)
        for half in range(2):
            planes = range(8 * half, 8 * half + 8)
            qt = jnp.concatenate([q_ref[0, p, rq, :] for p in planes], axis=0)
            bias = b1_ref[half] + no_prev * p1_ref[...] if first else b1_ref[half]
            stats = tile(qt, kt, vt, bias)
            for p in planes:
                put(2, pl.multiple_of(p * WINDOW + 16 * i, 16), 16, stats, 16 * (p - 8 * half))

    tile1(0, True)
    tile1(1, False)

    def body1(g, carry):
        tile1(2 * g, False)
        tile1(2 * g + 1, False)
        return carry
    lax.fori_loop(1, WINDOW // 32, body1, 0)

    def combine(i, carry):
        for p in (2 * i, 2 * i + 1):
            rows = pl.ds(pl.multiple_of(p * WINDOW, WINDOW), WINDOW)
            ms = [m_scr[b, rows, :] for b in range(3)]
            mx = jnp.maximum(jnp.maximum(ms[0], ms[1]), ms[2])
            ws = [jnp.exp(m - mx) for m in ms]
            den = ws[0] * l_scr[0, rows, :] + ws[1] * l_scr[1, rows, :] + ws[2] * l_scr[2, rows, :]
            num = ws[0] * a_scr[0, rows, :] + ws[1] * a_scr[1, rows, :] + ws[2] * a_scr[2, rows, :]
            o_ref[0, p] = (num / den).astype(BF16)
        return carry
    lax.fori_loop(0, N_PLANES // 2, combine, 0)


def _dilated_attention(q, k, v):
    b, _, sm, _ = q.shape
    n_span = sm // WINDOW
    cur = pl.BlockSpec((1, N_PLANES, WINDOW, LANES), lambda i, j, h: (i, 0, j, h))
    prev = pl.BlockSpec((1, N_PLANES, WINDOW, LANES), lambda i, j, h: (i, 0, jnp.maximum(j - 1, 0), h))
    biases = _attention_biases()
    bias_specs = [pl.BlockSpec(a.shape, lambda i, j, h, nd=a.ndim: (0,) * nd) for a in biases]
    stats = pltpu.VMEM((3, SPAN, LANES), F32)
    return pl.pallas_call(
        _attn_kernel,
        grid=(b, n_span, ATTN_WIDTH // LANES),
        in_specs=[cur, cur, prev, cur, prev] + bias_specs,
        out_specs=cur,
        out_shape=jax.ShapeDtypeStruct(q.shape, BF16),
        scratch_shapes=[stats, stats, stats, pltpu.VMEM((2 * WINDOW, 2 * WINDOW), F32)],
        compiler_params=pltpu.CompilerParams(vmem_limit_bytes=VMEM_LIMIT),
        name="dilated_attention",
    )(q, k, k, v, v, *biases)


CONV_HALO = 32


def _conv_kernel(cur_ref, prev_ref, w_ref, b_ref, g_ref, bl_ref, o_ref, scr, *, chunk):
    tt = cur_ref.shape[1]
    has_prev = pl.program_id(1) > 0
    scr[0:CONV_HALO, :] = jnp.where(has_prev, prev_ref[0], 0.0)
    scr[CONV_HALO:CONV_HALO + tt, :] = cur_ref[0]
    lead = CONV_HALO - (CONV_WIDTH - 1)
    for c0 in range(0, tt, chunk):
        acc = jnp.zeros((chunk, CONV_CHANNELS), F32)
        for j in range(CONV_WIDTH):
            acc = acc + w_ref[j:j + 1, :] * scr[c0 + lead + j:c0 + lead + j + chunk, :]
        y = acc + b_ref[...]
        mu = jnp.mean(y, axis=-1, keepdims=True)
        var = jnp.mean(jnp.square(y - mu), axis=-1, keepdims=True)
        yn = (y - mu) * lax.rsqrt(var + NORM_EPS) * g_ref[...] + bl_ref[...]
        o_ref[0, c0:c0 + chunk, :] = (yn * jax.nn.sigmoid(yn)).astype(BF16)


def _conformer_conv(glu, w_dw, b_dw, g_ln, b_ln):
    b, s, c = glu.shape
    tt = 512
    row = pl.BlockSpec((1, c), lambda i, j: (0, 0))
    return pl.pallas_call(
        functools.partial(_conv_kernel, chunk=128),
        grid=(b, s // tt),
        in_specs=[pl.BlockSpec((1, tt, c), lambda i, j: (i, j, 0)),
                  pl.BlockSpec((1, CONV_HALO, c), lambda i, j: (i, jnp.maximum(j * (tt // CONV_HALO) - 1, 0), 0)),
                  pl.BlockSpec((CONV_WIDTH, c), lambda i, j: (0, 0)), row, row, row],
        out_specs=pl.BlockSpec((1, tt, c), lambda i, j: (i, j, 0)),
        out_shape=jax.ShapeDtypeStruct((b, s, c), BF16),
        scratch_shapes=[pltpu.VMEM((CONV_HALO + tt, c), F32)],
        name="conformer_conv",
    )(glu, glu, w_dw[:, 0, :], b_dw[None, :], g_ln[None, :], b_ln[None, :])


def _outproj_kernel(attn_ref, conv_ref, x_ref, permt_ref, woa_ref, woc_ref, gpost_ref, gffn_ref, wrt_ref, br_ref,
                    h_ref, u_ref, idx_ref, gate_ref):
    tt = x_ref.shape[1]
    nat = []
    for sub in range(tt // PERM_TOKENS):
        rows = slice(sub * PERM_ROWS, (sub + 1) * PERM_ROWS)
        a = jnp.concatenate([attn_ref[0, p, rows, :] for p in range(N_PLANES)], axis=0)
        nat.append(jnp.dot(permt_ref[...], a, preferred_element_type=F32).astype(BF16))
    mix = (jnp.dot(jnp.concatenate(nat, axis=0), woa_ref[...], preferred_element_type=F32)
           + jnp.dot(conv_ref[0], woc_ref[...], preferred_element_type=F32))
    h = x_ref[0] + _rms(mix, gpost_ref[...])
    h_ref[0] = h
    u = _rms(h, gffn_ref[...])
    u_ref[...] = u
    logits = lax.dot_general(wrt_ref[...], u, (((1,), (1,)), ((), ())), precision=lax.Precision.HIGHEST,
                             preferred_element_type=F32) + br_ref[...]
    rows = lax.broadcasted_iota(jnp.int32, logits.shape, 0)
    vals = logits
    tops, idxs = [], []
    for _ in range(TOP_K):
        mx = jnp.max(vals, axis=0, keepdims=True)
        ix = jnp.min(jnp.where(vals == mx, rows, N_EXPERTS), axis=0, keepdims=True)
        tops.append(mx)
        idxs.append(ix)
        vals = jnp.where(rows == ix, -jnp.inf, vals)
    ex = [jnp.exp(t - tops[0]) for t in tops]
    den = ex[0] + ex[1] + ex[2] + ex[3]
    idx_ref[...] = jnp.concatenate(idxs, axis=0)
    gate_ref[...] = jnp.concatenate([e / den for e in ex] + [jnp.zeros((8 - TOP_K, tt), F32)], axis=0)


def _output_projection(attn, conv, x, w_out, g_mix_post, g_ffn_pre, w_router, b_router):
    b, s, d = x.shape
    tt = 512
    mc = tt // N_PLANES
    n_t = s // tt
    woa = w_out[:ATTN_WIDTH].astype(BF16)
    woc = w_out[ATTN_WIDTH:].astype(BF16)
    permt = jnp.asarray(_plane_permutation().T, BF16)
    const = lambda shape: pl.BlockSpec(shape, lambda i, j: (0, 0))
    flat = lambda w: pl.BlockSpec((tt, w), lambda i, j: (i * n_t + j, 0))
    lanes = lambda r: pl.BlockSpec((r, tt), lambda i, j: (0, i * n_t + j))
    return pl.pallas_call(
        _outproj_kernel,
        grid=(b, n_t),
        in_specs=[pl.BlockSpec((1, N_PLANES, mc, ATTN_WIDTH), lambda i, j: (i, 0, j, 0)),
                  pl.BlockSpec((1, tt, CONV_CHANNELS), lambda i, j: (i, j, 0)),
                  pl.BlockSpec((1, tt, d), lambda i, j: (i, j, 0)),
                  const(permt.shape), const(woa.shape), const(woc.shape), const((1, d)), const((1, d)),
                  const((N_EXPERTS, d)), const((N_EXPERTS, 1))],
        out_specs=[pl.BlockSpec((1, tt, d), lambda i, j: (i, j, 0)), flat(d), lanes(TOP_K), lanes(8)],
        out_shape=[jax.ShapeDtypeStruct((b, s, d), F32),
                   jax.ShapeDtypeStruct((b * s, d), F32),
                   jax.ShapeDtypeStruct((TOP_K, b * s), jnp.int32),
                   jax.ShapeDtypeStruct((8, b * s), F32)],
        compiler_params=pltpu.CompilerParams(vmem_limit_bytes=VMEM_LIMIT),
        name="output_projection_router",
    )(attn, conv, x, permt, woa, woc, g_mix_post[None, :], g_ffn_pre[None, :], w_router.T, b_router[:, None])


def _route_kernel(idx_ref, dest_ref, pend_ref, pad_ref, blk_ref, carry, pstart):
    phase = pl.program_id(0)
    step = pl.program_id(1)
    tt = idx_ref.shape[1]
    rows = lax.broadcasted_iota(jnp.int32, (N_EXPERTS, tt), 0)
    hot = [rows == idx_ref[k:k + 1, :] for k in range(TOP_K)]
    memb = sum(h.astype(F32) for h in hot)

    @pl.when((phase == 0) & (step == 0))
    def _():
        carry[...] = jnp.zeros_like(carry)

    @pl.when((phase == 1) & (step == 0))
    def _():
        counts = carry[...]
        padded = jnp.floor((counts + (EXPERT_ROWS - 1)) * (1.0 / EXPERT_ROWS)) * EXPERT_ROWS
        tri = (lax.broadcasted_iota(jnp.int32, (N_EXPERTS, N_EXPERTS), 1)
               <= lax.broadcasted_iota(jnp.int32, (N_EXPERTS, N_EXPERTS), 0)).astype(F32)
        pend = jnp.dot(tri, padded, precision=lax.Precision.HIGHEST, preferred_element_type=F32)
        pstart[...] = pend - padded
        pend_ref[...] = pend.astype(jnp.int32)
        pad_ref[...] = (pend - padded + counts).astype(jnp.int32)
        starts = lax.broadcasted_iota(jnp.int32, (N_EXPERTS, blk_ref.shape[1]), 1) * EXPERT_ROWS
        ended = (pend.astype(jnp.int32) <= starts).astype(jnp.int32)
        blk_ref[...] = jnp.minimum(jnp.sum(ended, axis=0, keepdims=True), N_EXPERTS - 1)
        carry[...] = jnp.zeros_like(carry)

    @pl.when(phase == 1)
    def _():
        earlier = (lax.broadcasted_iota(jnp.int32, (tt, tt), 0)
                   < lax.broadcasted_iota(jnp.int32, (tt, tt), 1)).astype(BF16)
        row = jnp.dot(memb.astype(BF16), earlier, preferred_element_type=F32) + (carry[...] + pstart[...])
        dest_ref[...] = jnp.concatenate(
            [jnp.sum(jnp.where(h, row, 0.0), axis=0, keepdims=True) for h in hot], axis=0).astype(jnp.int32)

    carry[...] = carry[...] + jnp.sum(memb, axis=1, keepdims=True)


def _routing(idx, n_blk):
    _, t = idx.shape
    tt = 512
    blk_lanes = -(-n_blk // LANES) * LANES
    return pl.pallas_call(
        _route_kernel,
        grid=(2, t // tt),
        in_specs=[pl.BlockSpec((TOP_K, tt), lambda ph, i: (0, i))],
        out_specs=[pl.BlockSpec((TOP_K, tt), lambda ph, i: (0, i * ph)),
                   pl.BlockSpec((N_EXPERTS, 1), lambda ph, i: (0, 0)),
                   pl.BlockSpec((N_EXPERTS, 1), lambda ph, i: (0, 0)),
                   pl.BlockSpec((1, blk_lanes), lambda ph, i: (0, 0))],
        out_shape=[jax.ShapeDtypeStruct((TOP_K, t), jnp.int32), jax.ShapeDtypeStruct((N_EXPERTS, 1), jnp.int32),
                   jax.ShapeDtypeStruct((N_EXPERTS, 1), jnp.int32), jax.ShapeDtypeStruct((1, blk_lanes), jnp.int32)],
        scratch_shapes=[pltpu.VMEM((N_EXPERTS, 1), F32), pltpu.VMEM((N_EXPERTS, 1), F32)],
        compiler_params=pltpu.CompilerParams(dimension_semantics=("arbitrary", "arbitrary")),
        name="moe_routing",
    )(idx)


def _slot_kernel(pad_ref, pend_ref, dest_ref, slot_ref, *, n_tok):
    i = pl.program_id(0)
    tt = dest_ref.shape[1]

    @pl.when(i == 0)
    def _():
        def fill(r, c):
            slot_ref[r] = TOP_K * n_tok + lax.rem(r, EXPERT_ROWS)
            return c

        def per_expert(e, carry):
            return lax.fori_loop(pad_ref[e], pend_ref[e], fill, carry)
        lax.fori_loop(0, N_EXPERTS, per_expert, 0)
        lax.fori_loop(pend_ref[N_EXPERTS - 1], slot_ref.shape[0], fill, 0)

    def body(g, carry):
        for u in range(8):
            tl = g * 8 + u
            for k in range(TOP_K):
                slot_ref[dest_ref[k, tl]] = k * n_tok + i * tt + tl
        return carry
    lax.fori_loop(0, tt // 8, body, 0)


def _row_slots(dest, pad, pend, cap):
    _, t = dest.shape
    tt = 1024
    smem = pl.BlockSpec(memory_space=pltpu.SMEM)
    return pl.pallas_call(
        functools.partial(_slot_kernel, n_tok=t),
        grid=(t // tt,),
        in_specs=[smem, smem, pl.BlockSpec((TOP_K, tt), lambda i: (0, i), memory_space=pltpu.SMEM)],
        out_specs=smem,
        out_shape=jax.ShapeDtypeStruct((cap,), jnp.int32),
        compiler_params=pltpu.CompilerParams(dimension_semantics=("arbitrary",)),
        name="moe_row_slots",
    )(pad, pend, dest)


def _expert_kernel(blk_e_ref, nused_ref, slot_ref, u_hbm, wgu_ref, bgu_ref, wd_ref, bd_ref, yk_hbm,
                   xg0, xg1, yo0, yo1, gsem, ssem, fence):
    del blk_e_ref
    j = pl.program_id(0)
    nused = nused_ref[0]
    n_tok = u_hbm.shape[0]
    d_ff = wd_ref.shape[1]
    spare = TOP_K * n_tok

    token_of = (lambda s: s & (n_tok - 1)) if n_tok & (n_tok - 1) == 0 else (lambda s: lax.rem(s, n_tok))

    def gather(block, buf, sem, rows):
        for r in rows:
            tok = token_of(slot_ref[block * EXPERT_ROWS + r])
            pltpu.make_async_copy(u_hbm.at[pl.ds(tok, 1), :], buf.at[pl.ds(r, 1), :], sem).start()

    def scatter(block, real, buf, sem, rows):
        for r in rows:
            s = jnp.where(real, slot_ref[block * EXPERT_ROWS + r], spare + r)
            pltpu.make_async_copy(buf.at[pl.ds(r, 1), :], yk_hbm.at[pl.ds(s, 1), :], sem).start()

    def wait_gather(buf, sem):
        pltpu.make_async_copy(u_hbm.at[pl.ds(0, EXPERT_ROWS), :], buf, sem).wait()

    def wait_scatter(buf, sem):
        pltpu.make_async_copy(buf, yk_hbm.at[pl.ds(0, EXPERT_ROWS), :], sem).wait()

    def step(xg_c, yo_c, gs_c, ss_c, xg_o, yo_o, gs_o, ss_o):
        wait_gather(xg_c, gs_c)
        nxt = jnp.minimum(j + 1, nused - 1)
        prv = jnp.maximum(j - 1, 0)
        gather(nxt, xg_o, gs_o, range(EXPERT_ROWS))
        scatter(prv, j > 0, yo_o, ss_o, range(EXPERT_ROWS))
        pl.semaphore_signal(fence, 1)
        pl.semaphore_wait(fence, 1)
        gu = jnp.dot(xg_c[...].astype(BF16), wgu_ref[0], preferred_element_type=F32) + bgu_ref[0]
        gate = jnp.minimum(gu[:, :d_ff], SWIGLU_LIMIT)
        up = jnp.clip(gu[:, d_ff:], -SWIGLU_LIMIT, SWIGLU_LIMIT)
        hid = (up + 1.0) * (gate * jax.nn.sigmoid(gate * SWIGLU_ALPHA))
        yo_c[...] = jnp.dot(hid.astype(BF16), wd_ref[0], preferred_element_type=F32) + bd_ref[0]
        wait_scatter(yo_o, ss_o)

        @pl.when(j == nused - 1)
        def _():
            scatter(j, True, yo_c, ss_c, range(EXPERT_ROWS))
            wait_scatter(yo_c, ss_c)
            wait_gather(xg_o, gs_o)

    @pl.when(j == 0)
    def _():
        yo1[...] = jnp.zeros_like(yo1)
        gather(0, xg0, gsem.at[0], range(EXPERT_ROWS))

    even = (xg0, yo0, gsem.at[0], ssem.at[0])
    odd = (xg1, yo1, gsem.at[1], ssem.at[1])

    @pl.when((j < nused) & (j % 2 == 0))
    def _():
        step(*even, *odd)

    @pl.when((j < nused) & (j % 2 == 1))
    def _():
        step(*odd, *even)


def _experts(u, slot, blk_e, nused, w_gate_up, b_gate_up, w_down, b_down):
    t, d = u.shape
    e, _, ff2 = w_gate_up.shape
    d_ff = w_down.shape[1]
    n_blk = slot.shape[0] // EXPERT_ROWS
    expert = lambda j, be, nu, sl: (be[jnp.minimum(j, nu[0] - 1)], 0, 0)
    grid_spec = pltpu.PrefetchScalarGridSpec(
        num_scalar_prefetch=3,
        grid=(n_blk,),
        in_specs=[pl.BlockSpec(memory_space=pl.ANY),
                  pl.BlockSpec((1, d, ff2), expert), pl.BlockSpec((1, 1, ff2), expert),
                  pl.BlockSpec((1, d_ff, d), expert), pl.BlockSpec((1, 1, d), expert)],
        out_specs=pl.BlockSpec(memory_space=pl.ANY),
        scratch_shapes=([pltpu.VMEM((EXPERT_ROWS, d), F32)] * 4 + [pltpu.SemaphoreType.DMA((2,))] * 2
                        + [pltpu.SemaphoreType.REGULAR]),
    )
    return pl.pallas_call(
        _expert_kernel,
        grid_spec=grid_spec,
        out_shape=jax.ShapeDtypeStruct((TOP_K * t + EXPERT_ROWS, d), F32),
        compiler_params=pltpu.CompilerParams(dimension_semantics=("arbitrary",), vmem_limit_bytes=VMEM_LIMIT),
        name="moe_experts",
    )(blk_e, nused, slot, u, w_gate_up.astype(BF16), b_gate_up[:, None, :], w_down.astype(BF16), b_down[:, None, :])


def _combine_kernel(gate_ref, h_ref, p_ref, y0_ref, y1_ref, y2_ref, y3_ref, gffn_ref, wple_ref, wpg_ref, gple_ref,
                    o_ref):
    gt = gate_ref[...].T
    y = gt[:, 0:1] * y0_ref[...]
    for k, y_ref in enumerate((y1_ref, y2_ref, y3_ref), start=1):
        y = y + gt[:, k:k + 1] * y_ref[...]
    h2 = h_ref[...] + _rms(y, gffn_ref[...])
    ple = (jnp.dot(p_ref[...].astype(BF16), wple_ref[...], preferred_element_type=F32)
           * jax.nn.sigmoid(jnp.dot(h2.astype(BF16), wpg_ref[...], preferred_element_type=F32)))
    o_ref[...] = h2 + _rms(ple, gple_ref[...])


def _combine(gates, h, p, yk, g_ffn_post, w_ple, w_ple_gate, g_ple_post):
    t, d = h.shape
    tt = 256
    n_t = t // tt
    const = lambda shape: pl.BlockSpec(shape, lambda i: (0, 0))
    choice = lambda k: pl.BlockSpec((tt, d), lambda i: (k * n_t + i, 0))
    return pl.pallas_call(
        _combine_kernel,
        grid=(n_t,),
        in_specs=[pl.BlockSpec((8, tt), lambda i: (0, i)),
                  pl.BlockSpec((tt, d), lambda i: (i, 0)),
                  pl.BlockSpec((tt, p.shape[1]), lambda i: (i, 0)),
                  choice(0), choice(1), choice(2), choice(3),
                  const((1, d)), const(w_ple.shape), const(w_ple_gate.shape), const((1, d))],
        out_specs=pl.BlockSpec((tt, d), lambda i: (i, 0)),
        out_shape=jax.ShapeDtypeStruct((t, d), F32),
        compiler_params=pltpu.CompilerParams(vmem_limit_bytes=VMEM_LIMIT),
        name="moe_combine_ple",
    )(gates, h, p, yk, yk, yk, yk, g_ffn_post[None, :], w_ple.astype(BF16), w_ple_gate.astype(BF16),
      g_ple_post[None, :])


def _layer(h, p, positions, g_mix_pre, w_in, w_dw, b_dw, g_conv_ln, b_conv_ln, w_out, g_mix_post, g_ffn_pre,
           w_router, b_router, w_gate_up, b_gate_up, w_down, b_down, g_ffn_post, w_ple, w_ple_gate, g_ple_post):
    b, s, d = h.shape
    t = b * s
    cos_t, sin_t = _rope_tables(positions)
    q, k, v, glu = _input_projection(h, cos_t, sin_t, g_mix_pre, w_in)
    attn = _dilated_attention(q, k, v)
    conv = _conformer_conv(glu, w_dw, b_dw, g_conv_ln, b_conv_ln)
    h1, u_ffn, idx, gates = _output_projection(attn, conv, h, w_out, g_mix_post, g_ffn_pre, w_router, b_router)
    cap = t * TOP_K + N_EXPERTS * EXPERT_ROWS
    n_blk = cap // EXPERT_ROWS
    dest, pend, pad, blk = _routing(idx, n_blk)
    blk_e = blk[0, :n_blk]
    nused = pend[N_EXPERTS - 1] // EXPERT_ROWS
    slot = _row_slots(dest, pad[:, 0], pend[:, 0], cap)
    yk = _experts(u_ffn, slot, blk_e, nused, w_gate_up, b_gate_up, w_down, b_down)
    out = _combine(gates, h1.reshape(t, d), p.reshape(t, -1), yk, g_ffn_post, w_ple, w_ple_gate, g_ple_post)
    return out.reshape(b, s, d)


def kernel(x, p, positions, g_mix_pre, w_in, w_dw, b_dw, g_conv_ln, b_conv_ln, w_out, g_mix_post, g_ffn_pre,
           w_router, b_router, w_gate_up, b_gate_up, w_down, b_down, g_ffn_post, w_ple, w_ple_gate, g_ple_post):
    h = x
    for i in range(p.shape[0]):
        h = _layer(h, p[i], positions, g_mix_pre[i], w_in[i], w_dw[i], b_dw[i], g_conv_ln[i], b_conv_ln[i],
                   w_out[i], g_mix_post[i], g_ffn_pre[i], w_router[i], b_router[i], w_gate_up[i], b_gate_up[i],
                   w_down[i], b_down[i], g_ffn_post[i], w_ple[i], w_ple_gate[i], g_ple_post[i])
    return h
```

```python
import functools

import numpy as np
import jax
import jax.numpy as jnp
from jax import lax
from jax.experimental import pallas as pl
from jax.experimental.pallas import tpu as pltpu

F32 = jnp.float32
BF16 = jnp.bfloat16

HEAD_DIM = 64
N_HEADS = 12
ATTN_WIDTH = N_HEADS * HEAD_DIM
CONV_CHANNELS = 256
CONV_WIDTH = 31
ROPE_DIM = HEAD_DIM // 4
ROPE_THETA = 500000.0
N_EXPERTS = 32
TOP_K = 4
SWIGLU_LIMIT = 7.0
SWIGLU_ALPHA = 1.702
NORM_EPS = 1e-6
WINDOW = 128
N_PLANES = 16
SPAN = N_PLANES * WINDOW
LANES = 128
NEG = -1e30
EXPERT_ROWS = 256
VMEM_LIMIT = 56 * 1024 * 1024


def _residue_of_plane(p):
    return 4 * (p % 4) + p // 4


def _rms(xv, g):
    var = jnp.mean(xv * xv, axis=-1, keepdims=True)
    return xv * lax.rsqrt(var + NORM_EPS) * g


def _rope_kernel(pos_ref, invf_ref, sgn_ref, c_ref, s_ref):
    ang = pos_ref[0].astype(F32) * invf_ref[...]
    c_ref[0] = jnp.cos(ang)
    s_ref[0] = jnp.sin(ang) * sgn_ref[...]


def _rope_tables(positions):
    b, s = positions.shape
    sm = s // N_PLANES
    mt = min(sm, 128)
    plane_res = np.array([_residue_of_plane(p) for p in range(N_PLANES)])
    pos_planes = positions.reshape(b, sm, N_PLANES).transpose(0, 2, 1)[:, plane_res]
    lane = np.arange(LANES) % HEAD_DIM
    inv_freq = ROPE_THETA ** (-jnp.arange(0, ROPE_DIM, 2, dtype=F32) / ROPE_DIM)
    invf = jnp.where(lane < ROPE_DIM, jnp.tile(inv_freq, LANES // (ROPE_DIM // 2)), 0.0)[None, :]
    sgn = jnp.asarray(np.where(lane < ROPE_DIM // 2, -1.0, 1.0), F32)[None, :]
    row = pl.BlockSpec((1, LANES), lambda i, j: (0, 0))
    out = pl.BlockSpec((1, N_PLANES, mt, LANES), lambda i, j: (i, 0, j, 0))
    return pl.pallas_call(
        _rope_kernel,
        grid=(b, sm // mt),
        in_specs=[pl.BlockSpec((1, N_PLANES, mt, 1), lambda i, j: (i, 0, j, 0)), row, row],
        out_specs=[out, out],
        out_shape=[jax.ShapeDtypeStruct((b, N_PLANES, sm, LANES), F32)] * 2,
        name="rope_tables",
    )(pos_planes[..., None], invf, sgn)


PERM_TOKENS = 256
PERM_ROWS = PERM_TOKENS // N_PLANES


def _plane_permutation():
    perm = np.zeros((PERM_TOKENS, PERM_TOKENS), np.float32)
    for p in range(N_PLANES):
        for ml in range(PERM_ROWS):
            perm[PERM_ROWS * p + ml, N_PLANES * ml + _residue_of_plane(p)] = 1.0
    return perm


def _inproj_kernel(x_ref, c_ref, s_ref, g_ref, perm_ref, wqkv_ref, wc_ref, q_ref, k_ref, v_ref, glu_ref):
    g = g_ref[...]
    tt = x_ref.shape[1]
    lane = lax.broadcasted_iota(jnp.int32, (1, LANES), 1) % HEAD_DIM
    first_half = lane < ROPE_DIM // 2

    def rotary(t, cos, sin):
        outs = []
        for j in range(ATTN_WIDTH // LANES):
            tj = t[:, j * LANES:(j + 1) * LANES]
            partner = jnp.where(first_half, pltpu.roll(tj, LANES - ROPE_DIM // 2, 1),
                                pltpu.roll(tj, ROPE_DIM // 2, 1))
            outs.append(tj * cos + partner * sin)
        return jnp.concatenate(outs, axis=1)

    un = _rms(x_ref[0], g).astype(BF16)
    pc = jnp.dot(un, wc_ref[...], preferred_element_type=F32)
    glu_ref[0] = pc[:, :CONV_CHANNELS] * jax.nn.sigmoid(pc[:, CONV_CHANNELS:])

    for sub in range(tt // PERM_TOKENS):
        rows = slice(sub * PERM_ROWS, (sub + 1) * PERM_ROWS)
        u = jnp.dot(perm_ref[...], un[sub * PERM_TOKENS:(sub + 1) * PERM_TOKENS],
                    preferred_element_type=F32).astype(BF16)
        cos = jnp.concatenate([c_ref[0, p, rows, :] for p in range(N_PLANES)], axis=0)
        sin = jnp.concatenate([s_ref[0, p, rows, :] for p in range(N_PLANES)], axis=0)
        proj = jnp.dot(u, wqkv_ref[...], preferred_element_type=F32)
        q = (rotary(proj[:, :ATTN_WIDTH], cos, sin) * (HEAD_DIM ** -0.5)).astype(BF16)
        k = rotary(proj[:, ATTN_WIDTH:2 * ATTN_WIDTH], cos, sin).astype(BF16)
        v = proj[:, 2 * ATTN_WIDTH:].astype(BF16)
        for p in range(N_PLANES):
            chunk = slice(p * PERM_ROWS, (p + 1) * PERM_ROWS)
            q_ref[0, p, rows, :] = q[chunk]
            k_ref[0, p, rows, :] = k[chunk]
            v_ref[0, p, rows, :] = v[chunk]


def _input_projection(x, cos_t, sin_t, g_mix_pre, w_in):
    b, s, d = x.shape
    tt = 1024
    mc = tt // N_PLANES
    wqkv = w_in[:, :3 * ATTN_WIDTH].astype(BF16)
    wc = w_in[:, 3 * ATTN_WIDTH:].astype(BF16)
    perm = jnp.asarray(_plane_permutation(), BF16)
    plane = lambda w: pl.BlockSpec((1, N_PLANES, mc, w), lambda i, j: (i, 0, j, 0))
    plane_shape = jax.ShapeDtypeStruct((b, N_PLANES, s // N_PLANES, ATTN_WIDTH), BF16)
    tok = lambda w: pl.BlockSpec((1, tt, w), lambda i, j: (i, j, 0))
    const = lambda shape: pl.BlockSpec(shape, lambda i, j: (0, 0))
    return pl.pallas_call(
        _inproj_kernel,
        grid=(b, s // tt),
        in_specs=[tok(d), plane(LANES), plane(LANES), const((1, d)), const(perm.shape),
                  const(wqkv.shape), const(wc.shape)],
        out_specs=[plane(ATTN_WIDTH), plane(ATTN_WIDTH), plane(ATTN_WIDTH), tok(CONV_CHANNELS)],
        out_shape=[plane_shape, plane_shape, plane_shape,
                   jax.ShapeDtypeStruct((b, s, CONV_CHANNELS), F32)],
        compiler_params=pltpu.CompilerParams(vmem_limit_bytes=VMEM_LIMIT),
        name="input_projection",
    )(x, cos_t, sin_t, g_mix_pre[None, :], perm, wqkv, wc)


def _attention_biases():
    band = lambda j: np.where((j >= 0) & (j <= WINDOW), 0.0, NEG).astype(np.float32)
    cols = lambda m: np.where(m, NEG, 0.0).astype(np.float32)
    twice = lambda a: np.concatenate([a, a], axis=0)
    mq = np.arange(WINDOW)[:, None]
    kj = np.arange(2 * WINDOW)[None, :]
    j16 = mq + WINDOW - kj
    prev16 = kj < WINDOW
    row = np.arange(128)[:, None]
    col = np.arange(256)[None, :]
    j4 = 4 * (row % 32 - (col % 64 - 32)) + row // 32 - col // 64
    prev4 = col % 64 < 32
    row = np.arange(256)[:, None]
    col = np.arange(512)[None, :]
    res = np.vectorize(_residue_of_plane)
    j1 = 16 * (row % 16 - (col % 32 - 16)) + res(row // 16) - res(col // 32)
    prev1 = col % 32 < 16
    b1 = np.stack([twice(band(j1[:128])), twice(band(j1[128:]))])
    return [jnp.asarray(a) for a in (twice(band(j16)), twice(band(j4)), b1, cols(prev16), cols(prev4), cols(prev1))]


UNROLL = 4


def _attn_kernel(q_ref, kc_ref, kp_ref, vc_ref, vp_ref, b16_ref, b4_ref, b1_ref, p16_ref, p4_ref, p1_ref,
                 o_ref, m_scr, l_scr, a_scr, bias16_scr):
    no_prev = (pl.program_id(1) == 0).astype(F32)
    head0 = lax.broadcasted_iota(jnp.int32, (1, LANES), 1) < HEAD_DIM
    bias16_scr[...] = b16_ref[...] + no_prev * p16_ref[...]

    def tile(qt, kt, vt, bias):
        n = qt.shape[0]
        zero = jnp.zeros_like(qt)
        q2 = jnp.concatenate([jnp.where(head0, qt, zero), jnp.where(head0, zero, qt)], axis=0)
        s = lax.dot_general(q2, kt, (((1,), (1,)), ((), ())), preferred_element_type=F32) + bias
        m = jnp.max(s, axis=-1, keepdims=True)
        e = jnp.exp(s - m).astype(BF16)
        va = jnp.concatenate([vt, jnp.ones_like(vt)], axis=1)
        o = jnp.dot(e, va, preferred_element_type=F32)
        pick = lambda top, bot: jnp.where(head0, top, bot)
        mm = pick(jnp.broadcast_to(m[:n], (n, LANES)), jnp.broadcast_to(m[n:], (n, LANES)))
        return mm, pick(o[:n, LANES:], o[n:, LANES:]), pick(o[:n, :LANES], o[n:, :LANES])

    def put(branch, start, size, stats, off):
        for scr, val in zip((m_scr, l_scr, a_scr), stats):
            scr[branch, pl.ds(start, size), :] = val[off:off + size]

    def body16(i, carry):
        for p in [UNROLL * i + u for u in range(UNROLL)]:
            kt = jnp.concatenate([kp_ref[0, p], kc_ref[0, p]], axis=0)
            vt = jnp.concatenate([vp_ref[0, p], vc_ref[0, p]], axis=0)
            put(0, pl.multiple_of(p * WINDOW, WINDOW), WINDOW, tile(q_ref[0, p], kt, vt, bias16_scr[...]), 0)
        return carry
    lax.fori_loop(0, N_PLANES // UNROLL, body16, 0)

    def body4(g, carry):
        for c, i in [(2 * g + cc, ii) for cc in range(2) for ii in range(4)]:
            qt = jnp.concatenate([q_ref[0, 4 * c + a, 32 * i:32 * i + 32, :] for a in range(4)], axis=0)
            if i == 0:
                ks = [x for a in range(4) for x in (kp_ref[0, 4 * c + a, 96:128, :], kc_ref[0, 4 * c + a, 0:32, :])]
                vs = [x for a in range(4) for x in (vp_ref[0, 4 * c + a, 96:128, :], vc_ref[0, 4 * c + a, 0:32, :])]
                bias = b4_ref[...] + no_prev * p4_ref[...]
            else:
                ks = [kc_ref[0, 4 * c + a, 32 * i - 32:32 * i + 32, :] for a in range(4)]
                vs = [vc_ref[0, 4 * c + a, 32 * i - 32:32 * i + 32, :] for a in range(4)]
                bias = b4_ref[...]
            stats = tile(qt, jnp.concatenate(ks, axis=0), jnp.concatenate(vs, axis=0), bias)
            for a in range(4):
                put(1, pl.multiple_of((4 * c + a) * WINDOW + 32 * i, 32), 32, stats, 32 * a)
        return carry
    lax.fori_loop(0, 2, body4, 0)

    def tile1(i, first):
        rq = pl.ds(pl.multiple_of(16 * i, 16), 16)
        if first:
            ks = [x for p in range(N_PLANES) for x in (kp_ref[0, p, 112:128, :], kc_ref[0, p, 0:16, :])]
            vs = [x for p in range(N_PLANES) for x in (vp_ref[0, p, 112:128, :], vc_ref[0, p, 0:16, :])]
        else:
            rk = pl.ds(pl.multiple_of(16 * i - 16, 16), 32)
            ks = [kc_ref[0, p, rk, :] for p in range(N_PLANES)]
            vs = [vc_ref[0, p, rk, :] for p in range(N_PLANES)]
        kt = jnp.concatenate(ks, axis=0)
        vt = jnp.concatenate(vs, axis=0)
        for half in range(2):
            planes = range(8 * half, 8 * half + 8)
            qt = jnp.concatenate([q_ref[0, p, rq, :] for p in planes], axis=0)
            bias = b1_ref[half] + no_prev * p1_ref[...] if first else b1_ref[half]
            stats = tile(qt, kt, vt, bias)
            for p in planes:
                put(2, pl.multiple_of(p * WINDOW + 16 * i, 16), 16, stats, 16 * (p - 8 * half))

    tile1(0, True)
    tile1(1, False)

    def body1(g, carry):
        tile1(2 * g, False)
        tile1(2 * g + 1, False)
        return carry
    lax.fori_loop(1, WINDOW // 32, body1, 0)

    def combine(i, carry):
        for p in (2 * i, 2 * i + 1):
            rows = pl.ds(pl.multiple_of(p * WINDOW, WINDOW), WINDOW)
            ms = [m_scr[b, rows, :] for b in range(3)]
            mx = jnp.maximum(jnp.maximum(ms[0], ms[1]), ms[2])
            ws = [jnp.exp(m - mx) for m in ms]
            den = ws[0] * l_scr[0, rows, :] + ws[1] * l_scr[1, rows, :] + ws[2] * l_scr[2, rows, :]
            num = ws[0] * a_scr[0, rows, :] + ws[1] * a_scr[1, rows, :] + ws[2] * a_scr[2, rows, :]
            o_ref[0, p] = (num / den).astype(BF16)
        return carry
    lax.fori_loop(0, N_PLANES // 2, combine, 0)


def _dilated_attention(q, k, v):
    b, _, sm, _ = q.shape
    n_span = sm // WINDOW
    cur = pl.BlockSpec((1, N_PLANES, WINDOW, LANES), lambda i, j, h: (i, 0, j, h))
    prev = pl.BlockSpec((1, N_PLANES, WINDOW, LANES), lambda i, j, h: (i, 0, jnp.maximum(j - 1, 0), h))
    biases = _attention_biases()
    bias_specs = [pl.BlockSpec(a.shape, lambda i, j, h, nd=a.ndim: (0,) * nd) for a in biases]
    stats = pltpu.VMEM((3, SPAN, LANES), F32)
    return pl.pallas_call(
        _attn_kernel,
        grid=(b, n_span, ATTN_WIDTH // LANES),
        in_specs=[cur, cur, prev, cur, prev] + bias_specs,
        out_specs=cur,
        out_shape=jax.ShapeDtypeStruct(q.shape, BF16),
        scratch_shapes=[stats, stats, stats, pltpu.VMEM((2 * WINDOW, 2 * WINDOW), F32)],
        compiler_params=pltpu.CompilerParams(vmem_limit_bytes=VMEM_LIMIT),
        name="dilated_attention",
    )(q, k, k, v, v, *biases)


CONV_HALO = 32


def _conv_kernel(cur_ref, prev_ref, w_ref, b_ref, g_ref, bl_ref, o_ref, scr, *, chunk):
    tt = cur_ref.shape[1]
    has_prev = pl.program_id(1) > 0
    scr[0:CONV_HALO, :] = jnp.where(has_prev, prev_ref[0], 0.0)
    scr[CONV_HALO:CONV_HALO + tt, :] = cur_ref[0]
    lead = CONV_HALO - (CONV_WIDTH - 1)
    for c0 in range(0, tt, chunk):
        acc = jnp.zeros((chunk, CONV_CHANNELS), F32)
        for j in range(CONV_WIDTH):
            acc = acc + w_ref[j:j + 1, :] * scr[c0 + lead + j:c0 + lead + j + chunk, :]
        y = acc + b_ref[...]
        mu = jnp.mean(y, axis=-1, keepdims=True)
        var = jnp.mean(jnp.square(y - mu), axis=-1, keepdims=True)
        yn = (y - mu) * lax.rsqrt(var + NORM_EPS) * g_ref[...] + bl_ref[...]
        o_ref[0, c0:c0 + chunk, :] = (yn * jax.nn.sigmoid(yn)).astype(BF16)


def _conformer_conv(glu, w_dw, b_dw, g_ln, b_ln):
    b, s, c = glu.shape
    tt = 512
    row = pl.BlockSpec((1, c), lambda i, j: (0, 0))
    return pl.pallas_call(
        functools.partial(_conv_kernel, chunk=128),
        grid=(b, s // tt),
        in_specs=[pl.BlockSpec((1, tt, c), lambda i, j: (i, j, 0)),
                  pl.BlockSpec((1, CONV_HALO, c), lambda i, j: (i, jnp.maximum(j * (tt // CONV_HALO) - 1, 0), 0)),
                  pl.BlockSpec((CONV_WIDTH, c), lambda i, j: (0, 0)), row, row, row],
        out_specs=pl.BlockSpec((1, tt, c), lambda i, j: (i, j, 0)),
        out_shape=jax.ShapeDtypeStruct((b, s, c), BF16),
        scratch_shapes=[pltpu.VMEM((CONV_HALO + tt, c), F32)],
        name="conformer_conv",
    )(glu, glu, w_dw[:, 0, :], b_dw[None, :], g_ln[None, :], b_ln[None, :])


def _outproj_kernel(attn_ref, conv_ref, x_ref, permt_ref, woa_ref, woc_ref, gpost_ref, gffn_ref, wr_ref, br_ref,
                    h_ref, u_ref, idx_ref, gate_ref):
    tt = x_ref.shape[1]
    nat = []
    for sub in range(tt // PERM_TOKENS):
        rows = slice(sub * PERM_ROWS, (sub + 1) * PERM_ROWS)
        a = jnp.concatenate([attn_ref[0, p, rows, :] for p in range(N_PLANES)], axis=0)
        nat.append(jnp.dot(permt_ref[...], a, preferred_element_type=F32).astype(BF16))
    mix = (jnp.dot(jnp.concatenate(nat, axis=0), woa_ref[...], preferred_element_type=F32)
           + jnp.dot(conv_ref[0], woc_ref[...], preferred_element_type=F32))
    h = x_ref[0] + _rms(mix, gpost_ref[...])
    h_ref[0] = h
    u = _rms(h, gffn_ref[...])
    u_ref[...] = u
    logits = lax.dot_general(wr_ref[...], u, (((1,), (1,)), ((), ())), precision=lax.Precision.HIGHEST,
                             preferred_element_type=F32) + br_ref[...]
    rows = lax.broadcasted_iota(jnp.int32, logits.shape, 0)
    vals = logits
    tops, idxs = [], []
    for _ in range(TOP_K):
        mx = jnp.max(vals, axis=0, keepdims=True)
        ix = jnp.min(jnp.where(vals == mx, rows, N_EXPERTS), axis=0, keepdims=True)
        tops.append(mx)
        idxs.append(ix)
        vals = jnp.where(rows == ix, -jnp.inf, vals)
    ex = [jnp.exp(t - tops[0]) for t in tops]
    den = ex[0] + ex[1] + ex[2] + ex[3]
    idx_ref[...] = jnp.concatenate(idxs, axis=0)
    gate_ref[...] = jnp.concatenate([e / den for e in ex] + [jnp.zeros((8 - TOP_K, tt), F32)], axis=0)


def _output_projection(attn, conv, x, w_out, g_mix_post, g_ffn_pre, w_router, b_router):
    b, s, d = x.shape
    tt = 512
    mc = tt // N_PLANES
    n_t = s // tt
    woa = w_out[:ATTN_WIDTH].astype(BF16)
    woc = w_out[ATTN_WIDTH:].astype(BF16)
    permt = jnp.asarray(_plane_permutation().T, BF16)
    const = lambda shape: pl.BlockSpec(shape, lambda i, j: (0, 0))
    flat = lambda w: pl.BlockSpec((tt, w), lambda i, j: (i * n_t + j, 0))
    lanes = lambda r: pl.BlockSpec((r, tt), lambda i, j: (0, i * n_t + j))
    return pl.pallas_call(
        _outproj_kernel,
        grid=(b, n_t),
        in_specs=[pl.BlockSpec((1, N_PLANES, mc, ATTN_WIDTH), lambda i, j: (i, 0, j, 0)),
                  pl.BlockSpec((1, tt, CONV_CHANNELS), lambda i, j: (i, j, 0)),
                  pl.BlockSpec((1, tt, d), lambda i, j: (i, j, 0)),
                  const(permt.shape), const(woa.shape), const(woc.shape), const((1, d)), const((1, d)),
                  const((N_EXPERTS, d)), const((N_EXPERTS, 1))],
        out_specs=[pl.BlockSpec((1, tt, d), lambda i, j: (i, j, 0)), flat(d), lanes(TOP_K), lanes(8)],
        out_shape=[jax.ShapeDtypeStruct((b, s, d), F32),
                   jax.ShapeDtypeStruct((b * s, d), F32),
                   jax.ShapeDtypeStruct((TOP_K, b * s), jnp.int32),
                   jax.ShapeDtypeStruct((8, b * s), F32)],
        compiler_params=pltpu.CompilerParams(vmem_limit_bytes=VMEM_LIMIT),
        name="output_projection_router",
    )(attn, conv, x, permt, woa, woc, g_mix_post[None, :], g_ffn_pre[None, :], w_router.T, b_router[:, None])


def _route_kernel(idx_ref, dest_ref, pend_ref, pad_ref, blk_ref, carry, pstart):
    phase = pl.program_id(0)
    step = pl.program_id(1)
    tt = idx_ref.shape[1]
    rows = lax.broadcasted_iota(jnp.int32, (N_EXPERTS, tt), 0)
    hot = [rows == idx_ref[k:k + 1, :] for k in range(TOP_K)]
    memb = sum(h.astype(F32) for h in hot)

    @pl.when((phase == 0) & (step == 0))
    def _():
        carry[...] = jnp.zeros_like(carry)

    @pl.when((phase == 1) & (step == 0))
    def _():
        counts = carry[...]
        padded = jnp.floor((counts + (EXPERT_ROWS - 1)) * (1.0 / EXPERT_ROWS)) * EXPERT_ROWS
        tri = (lax.broadcasted_iota(jnp.int32, (N_EXPERTS, N_EXPERTS), 1)
               <= lax.broadcasted_iota(jnp.int32, (N_EXPERTS, N_EXPERTS), 0)).astype(F32)
        pend = jnp.dot(tri, padded, precision=lax.Precision.HIGHEST, preferred_element_type=F32)
        pstart[...] = pend - padded
        pend_ref[...] = pend.astype(jnp.int32)
        pad_ref[...] = (pend - padded + counts).astype(jnp.int32)
        starts = lax.broadcasted_iota(jnp.int32, (N_EXPERTS, blk_ref.shape[1]), 1) * EXPERT_ROWS
        ended = (pend.astype(jnp.int32) <= starts).astype(jnp.int32)
        blk_ref[...] = jnp.minimum(jnp.sum(ended, axis=0, keepdims=True), N_EXPERTS - 1)
        carry[...] = jnp.zeros_like(carry)

    @pl.when(phase == 1)
    def _():
        earlier = (lax.broadcasted_iota(jnp.int32, (tt, tt), 0)
                   < lax.broadcasted_iota(jnp.int32, (tt, tt), 1)).astype(BF16)
        row = jnp.dot(memb.astype(BF16), earlier, preferred_element_type=F32) + (carry[...] + pstart[...])
        dest_ref[...] = jnp.concatenate(
            [jnp.sum(jnp.where(h, row, 0.0), axis=0, keepdims=True) for h in hot], axis=0).astype(jnp.int32)

    carry[...] = carry[...] + jnp.sum(memb, axis=1, keepdims=True)


def _routing(idx, n_blk):
    _, t = idx.shape
    tt = 512
    blk_lanes = -(-n_blk // LANES) * LANES
    return pl.pallas_call(
        _route_kernel,
        grid=(2, t // tt),
        in_specs=[pl.BlockSpec((TOP_K, tt), lambda ph, i: (0, i))],
        out_specs=[pl.BlockSpec((TOP_K, tt), lambda ph, i: (0, i * ph)),
                   pl.BlockSpec((N_EXPERTS, 1), lambda ph, i: (0, 0)),
                   pl.BlockSpec((N_EXPERTS, 1), lambda ph, i: (0, 0)),
                   pl.BlockSpec((1, blk_lanes), lambda ph, i: (0, 0))],
        out_shape=[jax.ShapeDtypeStruct((TOP_K, t), jnp.int32), jax.ShapeDtypeStruct((N_EXPERTS, 1), jnp.int32),
                   jax.ShapeDtypeStruct((N_EXPERTS, 1), jnp.int32), jax.ShapeDtypeStruct((1, blk_lanes), jnp.int32)],
        scratch_shapes=[pltpu.VMEM((N_EXPERTS, 1), F32), pltpu.VMEM((N_EXPERTS, 1), F32)],
        compiler_params=pltpu.CompilerParams(dimension_semantics=("arbitrary", "arbitrary")),
        name="moe_routing",
    )(idx)


def _slot_kernel(pad_ref, pend_ref, d0_ref, d1_ref, d2_ref, d3_ref, slot_ref, *, n_tok):
    i = pl.program_id(0)
    dest_refs = (d0_ref, d1_ref, d2_ref, d3_ref)
    tt = d0_ref.shape[0]

    @pl.when(i == 0)
    def _():
        def fill(r, c):
            slot_ref[r] = TOP_K * n_tok + lax.rem(r, EXPERT_ROWS)
            return c

        def per_expert(e, carry):
            return lax.fori_loop(pad_ref[e], pend_ref[e], fill, carry)
        lax.fori_loop(0, N_EXPERTS, per_expert, 0)
        lax.fori_loop(pend_ref[N_EXPERTS - 1], slot_ref.shape[0], fill, 0)

    unroll = 16

    def body(g, carry):
        tl0 = g * unroll
        first = [k * n_tok + i * tt + tl0 for k in range(TOP_K)]
        for u in range(unroll):
            for k in range(TOP_K):
                slot_ref[dest_refs[k][tl0 + u]] = first[k] + u
        return carry
    lax.fori_loop(0, tt // unroll, body, 0)


def _row_slots(dest, pad, pend, cap):
    _, t = dest.shape
    tt = 1024
    n_t = t // tt
    smem = pl.BlockSpec(memory_space=pltpu.SMEM)
    choice = lambda k: pl.BlockSpec((tt,), lambda i: (k * n_t + i,), memory_space=pltpu.SMEM)
    flat = dest.reshape(TOP_K * t)
    return pl.pallas_call(
        functools.partial(_slot_kernel, n_tok=t),
        grid=(n_t,),
        in_specs=[smem, smem] + [choice(k) for k in range(TOP_K)],
        out_specs=smem,
        out_shape=jax.ShapeDtypeStruct((cap,), jnp.int32),
        compiler_params=pltpu.CompilerParams(dimension_semantics=("arbitrary",)),
        name="moe_row_slots",
    )(pad, pend, flat, flat, flat, flat)


def _expert_kernel(blk_e_ref, nused_ref, slot_ref, u_hbm, wgu_ref, bgu_ref, wd_ref, bd_ref, yk_hbm,
                   xg0, xg1, yo0, yo1, wgu_b, wd_b, gsem, ssem, fence):
    j = pl.program_id(0)
    nused = nused_ref[0]
    n_tok = u_hbm.shape[0]
    d_ff = wd_ref.shape[1]
    spare = TOP_K * n_tok

    token_of = (lambda s: s & (n_tok - 1)) if n_tok & (n_tok - 1) == 0 else (lambda s: lax.rem(s, n_tok))

    def gather(block, buf, sem, rows):
        for r in rows:
            tok = token_of(slot_ref[block * EXPERT_ROWS + r])
            pltpu.make_async_copy(u_hbm.at[pl.ds(tok, 1), :], buf.at[pl.ds(r, 1), :], sem).start()

    def scatter(block, real, buf, sem, rows):
        for r in rows:
            s = jnp.where(real, slot_ref[block * EXPERT_ROWS + r], spare + r)
            pltpu.make_async_copy(buf.at[pl.ds(r, 1), :], yk_hbm.at[pl.ds(s, 1), :], sem).start()

    def wait_gather(buf, sem):
        pltpu.make_async_copy(u_hbm.at[pl.ds(0, EXPERT_ROWS), :], buf, sem).wait()

    def wait_scatter(buf, sem):
        pltpu.make_async_copy(buf, yk_hbm.at[pl.ds(0, EXPERT_ROWS), :], sem).wait()

    def step(xg_c, yo_c, gs_c, ss_c, xg_o, yo_o, gs_o, ss_o):
        wait_gather(xg_c, gs_c)
        nxt = jnp.minimum(j + 1, nused - 1)
        prv = jnp.maximum(j - 1, 0)
        gather(nxt, xg_o, gs_o, range(EXPERT_ROWS))
        scatter(prv, j > 0, yo_o, ss_o, range(EXPERT_ROWS))
        pl.semaphore_signal(fence, 1)
        pl.semaphore_wait(fence, 1)
        gu = jnp.dot(xg_c[...].astype(BF16), wgu_b[...], preferred_element_type=F32) + bgu_ref[0]
        gate = jnp.minimum(gu[:, :d_ff], SWIGLU_LIMIT)
        up = jnp.clip(gu[:, d_ff:], -SWIGLU_LIMIT, SWIGLU_LIMIT)
        hid = (up + 1.0) * (gate * jax.nn.sigmoid(gate * SWIGLU_ALPHA))
        yo_c[...] = jnp.dot(hid.astype(BF16), wd_b[...], preferred_element_type=F32) + bd_ref[0]
        wait_scatter(yo_o, ss_o)

        @pl.when(j == nused - 1)
        def _():
            scatter(j, True, yo_c, ss_c, range(EXPERT_ROWS))
            wait_scatter(yo_c, ss_c)
            wait_gather(xg_o, gs_o)

    @pl.when(j == 0)
    def _():
        yo1[...] = jnp.zeros_like(yo1)
        gather(0, xg0, gsem.at[0], range(EXPERT_ROWS))

    blk = jnp.minimum(j, nused - 1)
    new_expert = (j == 0) | (blk_e_ref[blk] != blk_e_ref[jnp.maximum(blk - 1, 0)])

    @pl.when((j < nused) & new_expert)
    def _():
        def cast(src, dst):
            def body(i, carry):
                rows = pl.ds(pl.multiple_of(i * LANES, LANES), LANES)
                dst[rows, :] = src[0, rows, :].astype(BF16)
                return carry
            lax.fori_loop(0, dst.shape[0] // LANES, body, 0)
        cast(wgu_ref, wgu_b)
        cast(wd_ref, wd_b)

    even = (xg0, yo0, gsem.at[0], ssem.at[0])
    odd = (xg1, yo1, gsem.at[1], ssem.at[1])

    @pl.when((j < nused) & (j % 2 == 0))
    def _():
        step(*even, *odd)

    @pl.when((j < nused) & (j % 2 == 1))
    def _():
        step(*odd, *even)


def _experts(u, slot, blk_e, nused, w_gate_up, b_gate_up, w_down, b_down):
    t, d = u.shape
    e, _, ff2 = w_gate_up.shape
    d_ff = w_down.shape[1]
    n_blk = slot.shape[0] // EXPERT_ROWS
    expert = lambda j, be, nu, sl: (be[jnp.minimum(j, nu[0] - 1)], 0, 0)
    grid_spec = pltpu.PrefetchScalarGridSpec(
        num_scalar_prefetch=3,
        grid=(n_blk,),
        in_specs=[pl.BlockSpec(memory_space=pl.ANY),
                  pl.BlockSpec((1, d, ff2), expert), pl.BlockSpec((1, 1, ff2), expert),
                  pl.BlockSpec((1, d_ff, d), expert), pl.BlockSpec((1, 1, d), expert)],
        out_specs=pl.BlockSpec(memory_space=pl.ANY),
        scratch_shapes=([pltpu.VMEM((EXPERT_ROWS, d), F32)] * 4
                        + [pltpu.VMEM((d, ff2), BF16), pltpu.VMEM((d_ff, d), BF16)]
                        + [pltpu.SemaphoreType.DMA((2,))] * 2 + [pltpu.SemaphoreType.REGULAR]),
    )
    return pl.pallas_call(
        _expert_kernel,
        grid_spec=grid_spec,
        out_shape=jax.ShapeDtypeStruct((TOP_K * t + EXPERT_ROWS, d), F32),
        compiler_params=pltpu.CompilerParams(dimension_semantics=("arbitrary",), vmem_limit_bytes=VMEM_LIMIT),
        name="moe_experts",
    )(blk_e, nused, slot, u, w_gate_up, b_gate_up[:, None, :], w_down, b_down[:, None, :])


def _combine_kernel(gate_ref, h_ref, p_ref, y0_ref, y1_ref, y2_ref, y3_ref, gffn_ref, wple_ref, wpg_ref, gple_ref,
                    o_ref):
    gt = gate_ref[...].T
    y = gt[:, 0:1] * y0_ref[...]
    for k, y_ref in enumerate((y1_ref, y2_ref, y3_ref), start=1):
        y = y + gt[:, k:k + 1] * y_ref[...]
    h2 = h_ref[...] + _rms(y, gffn_ref[...])
    ple = (jnp.dot(p_ref[...].astype(BF16), wple_ref[...], preferred_element_type=F32)
           * jax.nn.sigmoid(jnp.dot(h2.astype(BF16), wpg_ref[...], preferred_element_type=F32)))
    o_ref[...] = h2 + _rms(ple, gple_ref[...])


def _combine(gates, h, p, yk, g_ffn_post, w_ple, w_ple_gate, g_ple_post):
    t, d = h.shape
    tt = 256
    n_t = t // tt
    const = lambda shape: pl.BlockSpec(shape, lambda i: (0, 0))
    choice = lambda k: pl.BlockSpec((tt, d), lambda i: (k * n_t + i, 0))
    return pl.pallas_call(
        _combine_kernel,
        grid=(n_t,),
        in_specs=[pl.BlockSpec((8, tt), lambda i: (0, i)),
                  pl.BlockSpec((tt, d), lambda i: (i, 0)),
                  pl.BlockSpec((tt, p.shape[1]), lambda i: (i, 0)),
                  choice(0), choice(1), choice(2), choice(3),
                  const((1, d)), const(w_ple.shape), const(w_ple_gate.shape), const((1, d))],
        out_specs=pl.BlockSpec((tt, d), lambda i: (i, 0)),
        out_shape=jax.ShapeDtypeStruct((t, d), F32),
        compiler_params=pltpu.CompilerParams(vmem_limit_bytes=VMEM_LIMIT),
        name="moe_combine_ple",
    )(gates, h, p, yk, yk, yk, yk, g_ffn_post[None, :], w_ple.astype(BF16), w_ple_gate.astype(BF16),
      g_ple_post[None, :])


def _layer(h, p, positions, g_mix_pre, w_in, w_dw, b_dw, g_conv_ln, b_conv_ln, w_out, g_mix_post, g_ffn_pre,
           w_router, b_router, w_gate_up, b_gate_up, w_down, b_down, g_ffn_post, w_ple, w_ple_gate, g_ple_post):
    b, s, d = h.shape
    t = b * s
    cos_t, sin_t = _rope_tables(positions)
    q, k, v, glu = _input_projection(h, cos_t, sin_t, g_mix_pre, w_in)
    attn = _dilated_attention(q, k, v)
    conv = _conformer_conv(glu, w_dw, b_dw, g_conv_ln, b_conv_ln)
    h1, u_ffn, idx, gates = _output_projection(attn, conv, h, w_out, g_mix_post, g_ffn_pre, w_router, b_router)
    cap = t * TOP_K + N_EXPERTS * EXPERT_ROWS
    n_blk = cap // EXPERT_ROWS
    dest, pend, pad, blk = _routing(idx, n_blk)
    blk_e = blk[0, :n_blk]
    nused = pend[N_EXPERTS - 1] // EXPERT_ROWS
    slot = _row_slots(dest, pad[:, 0], pend[:, 0], cap)
    yk = _experts(u_ffn, slot, blk_e, nused, w_gate_up, b_gate_up, w_down, b_down)
    out = _combine(gates, h1.reshape(t, d), p.reshape(t, -1), yk, g_ffn_post, w_ple, w_ple_gate, g_ple_post)
    return out.reshape(b, s, d)


def kernel(x, p, positions, g_mix_pre, w_in, w_dw, b_dw, g_conv_ln, b_conv_ln, w_out, g_mix_post, g_ffn_pre,
           w_router, b_router, w_gate_up, b_gate_up, w_down, b_down, g_ffn_post, w_ple, w_ple_gate, g_ple_post):
    h = x
    for i in range(p.shape[0]):
        h = _layer(h, p[i], positions, g_mix_pre[i], w_in[i], w_dw[i], b_dw[i], g_conv_ln[i], b_conv_ln[i],
                   w_out[i], g_mix_post[i], g_ffn_pre[i], w_router[i], b_router[i], w_gate_up[i], b_gate_up[i],
                   w_down[i], b_down[i], g_ffn_post[i], w_ple[i], w_ple_gate[i], g_ple_post[i])
    return h
```

```python
import functools

import numpy as np
import jax
import jax.numpy as jnp
from jax import lax
from jax.experimental import pallas as pl
from jax.experimental.pallas import tpu as pltpu

F32 = jnp.float32
BF16 = jnp.bfloat16

HEAD_DIM = 64
N_HEADS = 12
ATTN_WIDTH = N_HEADS * HEAD_DIM
CONV_CHANNELS = 256
CONV_WIDTH = 31
ROPE_DIM = HEAD_DIM // 4
ROPE_THETA = 500000.0
N_EXPERTS = 32
TOP_K = 4
SWIGLU_LIMIT = 7.0
SWIGLU_ALPHA = 1.702
NORM_EPS = 1e-6
WINDOW = 128
N_PLANES = 16
SPAN = N_PLANES * WINDOW
LANES = 128
SUBLANES = 8
NEG = -1e30
EXPERT_ROWS = 256
VMEM_LIMIT = 56 * 1024 * 1024


def _residue_of_plane(p):
    return 4 * (p % 4) + p // 4


def _rms(xv, g):
    var = jnp.mean(xv * xv, axis=-1, keepdims=True)
    return xv * lax.rsqrt(var + NORM_EPS) * g


def _rope_kernel(pos_ref, invf_ref, expand_ref, one_ref, sgn_ref, c_ref, s_ref):
    tn = (((0,), (0,)), ((), ()))
    for p in range(N_PLANES):
        ang = invf_ref[...] * pos_ref[0, p:p + 1, :].astype(F32)
        spread = lambda t: lax.dot_general(t, expand_ref[...], tn, precision=lax.Precision.HIGHEST,
                                           preferred_element_type=F32)
        c_ref[0, p] = spread(jnp.cos(ang)) + one_ref[...]
        s_ref[0, p] = spread(jnp.sin(ang)) * sgn_ref[...]


def _rope_tables(positions):
    b, s = positions.shape
    sm = s // N_PLANES
    mt = min(sm, 128)
    plane_res = np.array([_residue_of_plane(p) for p in range(N_PLANES)])
    pos_planes = positions.reshape(b, sm, N_PLANES).transpose(0, 2, 1)[:, plane_res]
    half = ROPE_DIM // 2
    lane = np.arange(LANES) % HEAD_DIM
    inv_freq = ROPE_THETA ** (-jnp.arange(0, ROPE_DIM, 2, dtype=F32) / ROPE_DIM)
    rotary = lane < ROPE_DIM
    expand = jnp.asarray((np.arange(half)[:, None] == lane[None, :] % half) & rotary[None, :], F32)
    one = jnp.asarray(~rotary, F32)[None, :]
    sgn = jnp.asarray(np.where(lane < half, -1.0, 1.0), F32)[None, :]
    row = pl.BlockSpec((1, LANES), lambda i, j: (0, 0))
    out = pl.BlockSpec((1, N_PLANES, mt, LANES), lambda i, j: (i, 0, j, 0))
    return pl.pallas_call(
        _rope_kernel,
        grid=(b, sm // mt),
        in_specs=[pl.BlockSpec((1, N_PLANES, mt), lambda i, j: (i, 0, j)),
                  pl.BlockSpec((half, 1), lambda i, j: (0, 0)), pl.BlockSpec((half, LANES), lambda i, j: (0, 0)),
                  row, row],
        out_specs=[out, out],
        out_shape=[jax.ShapeDtypeStruct((b, N_PLANES, sm, LANES), F32)] * 2,
        name="rope_tables",
    )(pos_planes, inv_freq[:, None], expand, one, sgn)


PERM_TOKENS = 256
PERM_ROWS = PERM_TOKENS // N_PLANES


def _plane_permutation():
    perm = np.zeros((PERM_TOKENS, PERM_TOKENS), np.float32)
    for p in range(N_PLANES):
        for ml in range(PERM_ROWS):
            perm[PERM_ROWS * p + ml, N_PLANES * ml + _residue_of_plane(p)] = 1.0
    return perm


def _inproj_kernel(x_ref, c_ref, s_ref, g_ref, perm_ref, wqkv_ref, wc_ref, q_ref, k_ref, v_ref, glu_ref):
    g = g_ref[...]
    tt = x_ref.shape[1]
    lane = lax.broadcasted_iota(jnp.int32, (1, LANES), 1) % HEAD_DIM
    first_half = lane < ROPE_DIM // 2

    def rotary(t, cos, sin):
        outs = []
        for j in range(ATTN_WIDTH // LANES):
            tj = t[:, j * LANES:(j + 1) * LANES]
            partner = jnp.where(first_half, pltpu.roll(tj, LANES - ROPE_DIM // 2, 1),
                                pltpu.roll(tj, ROPE_DIM // 2, 1))
            outs.append(tj * cos + partner * sin)
        return jnp.concatenate(outs, axis=1)

    un = _rms(x_ref[0], g).astype(BF16)
    pc = jnp.dot(un, wc_ref[...], preferred_element_type=F32)
    glu_ref[0] = pc[:, :CONV_CHANNELS] * jax.nn.sigmoid(pc[:, CONV_CHANNELS:])

    for sub in range(tt // PERM_TOKENS):
        rows = slice(sub * PERM_ROWS, (sub + 1) * PERM_ROWS)
        u = jnp.dot(perm_ref[...], un[sub * PERM_TOKENS:(sub + 1) * PERM_TOKENS],
                    preferred_element_type=F32).astype(BF16)
        cos = jnp.concatenate([c_ref[0, p, rows, :] for p in range(N_PLANES)], axis=0)
        sin = jnp.concatenate([s_ref[0, p, rows, :] for p in range(N_PLANES)], axis=0)
        proj = jnp.dot(u, wqkv_ref[...], preferred_element_type=F32)
        q = (rotary(proj[:, :ATTN_WIDTH], cos, sin) * (HEAD_DIM ** -0.5)).astype(BF16)
        k = rotary(proj[:, ATTN_WIDTH:2 * ATTN_WIDTH], cos, sin).astype(BF16)
        v = proj[:, 2 * ATTN_WIDTH:].astype(BF16)
        for p in range(N_PLANES):
            chunk = slice(p * PERM_ROWS, (p + 1) * PERM_ROWS)
            q_ref[0, p, rows, :] = q[chunk]
            k_ref[0, p, rows, :] = k[chunk]
            v_ref[0, p, rows, :] = v[chunk]


def _input_projection(x, cos_t, sin_t, g_mix_pre, w_in):
    b, s, d = x.shape
    tt = 1024
    mc = tt // N_PLANES
    wqkv = w_in[:, :3 * ATTN_WIDTH].astype(BF16)
    wc = w_in[:, 3 * ATTN_WIDTH:].astype(BF16)
    perm = jnp.asarray(_plane_permutation(), BF16)
    plane = lambda w: pl.BlockSpec((1, N_PLANES, mc, w), lambda i, j: (i, 0, j, 0))
    plane_shape = jax.ShapeDtypeStruct((b, N_PLANES, s // N_PLANES, ATTN_WIDTH), BF16)
    tok = lambda w: pl.BlockSpec((1, tt, w), lambda i, j: (i, j, 0))
    const = lambda shape: pl.BlockSpec(shape, lambda i, j: (0, 0))
    return pl.pallas_call(
        _inproj_kernel,
        grid=(b, s // tt),
        in_specs=[tok(d), plane(LANES), plane(LANES), const((1, d)), const(perm.shape),
                  const(wqkv.shape), const(wc.shape)],
        out_specs=[plane(ATTN_WIDTH), plane(ATTN_WIDTH), plane(ATTN_WIDTH), tok(CONV_CHANNELS)],
        out_shape=[plane_shape, plane_shape, plane_shape,
                   jax.ShapeDtypeStruct((b, s, CONV_CHANNELS), F32)],
        compiler_params=pltpu.CompilerParams(vmem_limit_bytes=VMEM_LIMIT),
        name="input_projection",
    )(x, cos_t, sin_t, g_mix_pre[None, :], perm, wqkv, wc)


def _attention_biases():
    band = lambda j: np.where((j >= 0) & (j <= WINDOW), 0.0, NEG).astype(np.float32)
    cols = lambda m: np.where(m, NEG, 0.0).astype(np.float32)
    twice = lambda a: np.concatenate([a, a], axis=0)
    mq = np.arange(WINDOW)[:, None]
    kj = np.arange(2 * WINDOW)[None, :]
    j16 = mq + WINDOW - kj
    prev16 = kj < WINDOW
    row = np.arange(128)[:, None]
    col = np.arange(256)[None, :]
    j4 = 4 * (row % 32 - (col % 64 - 32)) + row // 32 - col // 64
    prev4 = col % 64 < 32
    row = np.arange(256)[:, None]
    col = np.arange(512)[None, :]
    res = np.vectorize(_residue_of_plane)
    j1 = 16 * (row % 16 - (col % 32 - 16)) + res(row // 16) - res(col // 32)
    prev1 = col % 32 < 16
    b1 = np.stack([twice(band(j1[:128])), twice(band(j1[128:]))])
    return [jnp.asarray(a) for a in (twice(band(j16)), twice(band(j4)), b1, cols(prev16), cols(prev4), cols(prev1))]


UNROLL = 8


def _attn_kernel(q_ref, kc_ref, kp_ref, vc_ref, vp_ref, b16_ref, b4_ref, b1_ref, p16_ref, p4_ref, p1_ref,
                 o_ref, m_scr, l_scr, a_scr, bias16_scr):
    no_prev = (pl.program_id(1) == 0).astype(F32)
    head0 = lax.broadcasted_iota(jnp.int32, (1, LANES), 1) < HEAD_DIM
    bias16_scr[...] = b16_ref[...] + no_prev * p16_ref[...]

    def tile(qt, kt, vt, bias):
        n = qt.shape[0]
        zero = jnp.zeros_like(qt)
        q2 = jnp.concatenate([jnp.where(head0, qt, zero), jnp.where(head0, zero, qt)], axis=0)
        s = lax.dot_general(q2, kt, (((1,), (1,)), ((), ())), preferred_element_type=F32) + bias
        m = jnp.max(s, axis=-1, keepdims=True)
        e = jnp.exp(s - m).astype(BF16)
        va = jnp.concatenate([vt, jnp.ones_like(vt)], axis=1)
        o = jnp.dot(e, va, preferred_element_type=F32)
        pick = lambda top, bot: jnp.where(head0, top, bot)
        mm = pick(jnp.broadcast_to(m[:n], (n, LANES)), jnp.broadcast_to(m[n:], (n, LANES)))
        return mm, pick(o[:n, LANES:], o[n:, LANES:]), pick(o[:n, :LANES], o[n:, :LANES])

    def put(branch, start, size, stats, off):
        for scr, val in zip((m_scr, l_scr, a_scr), stats):
            scr[branch, pl.ds(start, size), :] = val[off:off + size]

    def body16(i, carry):
        for p in [UNROLL * i + u for u in range(UNROLL)]:
            kt = jnp.concatenate([kp_ref[0, p], kc_ref[0, p]], axis=0)
            vt = jnp.concatenate([vp_ref[0, p], vc_ref[0, p]], axis=0)
            put(0, pl.multiple_of(p * WINDOW, WINDOW), WINDOW, tile(q_ref[0, p], kt, vt, bias16_scr[...]), 0)
        return carry
    lax.fori_loop(0, N_PLANES // UNROLL, body16, 0)

    def body4(g, carry):
        for c, i in [(2 * g + cc, ii) for cc in range(2) for ii in range(4)]:
            qt = jnp.concatenate([q_ref[0, 4 * c + a, 32 * i:32 * i + 32, :] for a in range(4)], axis=0)
            if i == 0:
                ks = [x for a in range(4) for x in (kp_ref[0, 4 * c + a, 96:128, :], kc_ref[0, 4 * c + a, 0:32, :])]
                vs = [x for a in range(4) for x in (vp_ref[0, 4 * c + a, 96:128, :], vc_ref[0, 4 * c + a, 0:32, :])]
                bias = b4_ref[...] + no_prev * p4_ref[...]
            else:
                ks = [kc_ref[0, 4 * c + a, 32 * i - 32:32 * i + 32, :] for a in range(4)]
                vs = [vc_ref[0, 4 * c + a, 32 * i - 32:32 * i + 32, :] for a in range(4)]
                bias = b4_ref[...]
            stats = tile(qt, jnp.concatenate(ks, axis=0), jnp.concatenate(vs, axis=0), bias)
            for a in range(4):
                put(1, pl.multiple_of((4 * c + a) * WINDOW + 32 * i, 32), 32, stats, 32 * a)
        return carry
    lax.fori_loop(0, 2, body4, 0)

    def tile1(i, first):
        rq = pl.ds(pl.multiple_of(16 * i, 16), 16)
        if first:
            ks = [x for p in range(N_PLANES) for x in (kp_ref[0, p, 112:128, :], kc_ref[0, p, 0:16, :])]
            vs = [x for p in range(N_PLANES) for x in (vp_ref[0, p, 112:128, :], vc_ref[0, p, 0:16, :])]
        else:
            rk = pl.ds(pl.multiple_of(16 * i - 16, 16), 32)
            ks = [kc_ref[0, p, rk, :] for p in range(N_PLANES)]
            vs = [vc_ref[0, p, rk, :] for p in range(N_PLANES)]
        kt = jnp.concatenate(ks, axis=0)
        vt = jnp.concatenate(vs, axis=0)
        for half in range(2):
            planes = range(8 * half, 8 * half + 8)
            qt = jnp.concatenate([q_ref[0, p, rq, :] for p in planes], axis=0)
            bias = b1_ref[half] + no_prev * p1_ref[...] if first else b1_ref[half]
            stats = tile(qt, kt, vt, bias)
            for p in planes:
                put(2, pl.multiple_of(p * WINDOW + 16 * i, 16), 16, stats, 16 * (p - 8 * half))

    tile1(0, True)
    tile1(1, False)

    def body1(g, carry):
        tile1(2 * g, False)
        tile1(2 * g + 1, False)
        return carry
    lax.fori_loop(1, WINDOW // 32, body1, 0)

    def combine(i, carry):
        for p in (2 * i, 2 * i + 1):
            rows = pl.ds(pl.multiple_of(p * WINDOW, WINDOW), WINDOW)
            ms = [m_scr[b, rows, :] for b in range(3)]
            mx = jnp.maximum(jnp.maximum(ms[0], ms[1]), ms[2])
            ws = [jnp.exp(m - mx) for m in ms]
            den = ws[0] * l_scr[0, rows, :] + ws[1] * l_scr[1, rows, :] + ws[2] * l_scr[2, rows, :]
            num = ws[0] * a_scr[0, rows, :] + ws[1] * a_scr[1, rows, :] + ws[2] * a_scr[2, rows, :]
            o_ref[0, p] = (num / den).astype(BF16)
        return carry
    lax.fori_loop(0, N_PLANES // 2, combine, 0)


def _dilated_attention(q, k, v):
    b, _, sm, _ = q.shape
    n_span = sm // WINDOW
    cur = pl.BlockSpec((1, N_PLANES, WINDOW, LANES), lambda i, j, h: (i, 0, j, h))
    prev = pl.BlockSpec((1, N_PLANES, WINDOW, LANES), lambda i, j, h: (i, 0, jnp.maximum(j - 1, 0), h))
    biases = _attention_biases()
    bias_specs = [pl.BlockSpec(a.shape, lambda i, j, h, nd=a.ndim: (0,) * nd) for a in biases]
    stats = pltpu.VMEM((3, SPAN, LANES), F32)
    return pl.pallas_call(
        _attn_kernel,
        grid=(b, n_span, ATTN_WIDTH // LANES),
        in_specs=[cur, cur, prev, cur, prev] + bias_specs,
        out_specs=cur,
        out_shape=jax.ShapeDtypeStruct(q.shape, BF16),
        scratch_shapes=[stats, stats, stats, pltpu.VMEM((2 * WINDOW, 2 * WINDOW), F32)],
        compiler_params=pltpu.CompilerParams(vmem_limit_bytes=VMEM_LIMIT),
        name="dilated_attention",
    )(q, k, k, v, v, *biases)


CONV_HALO = 32


def _conv_kernel(cur_ref, prev_ref, w_ref, b_ref, g_ref, bl_ref, o_ref, scr, *, chunk):
    tt = cur_ref.shape[1]
    has_prev = pl.program_id(1) > 0
    scr[0, 0:CONV_HALO, :] = jnp.where(has_prev, prev_ref[0], 0.0)
    scr[0, CONV_HALO:CONV_HALO + tt, :] = cur_ref[0]
    aligned_rows = tt + CONV_HALO - SUBLANES
    for s in range(1, SUBLANES):
        scr[s, 0:aligned_rows, :] = scr[0, s:s + aligned_rows, :]
    lead = CONV_HALO - (CONV_WIDTH - 1)
    for c0 in range(0, tt, chunk):
        acc = jnp.zeros((chunk, CONV_CHANNELS), F32)
        for j in range(CONV_WIDTH):
            s, a = (lead + j) % SUBLANES, (lead + j) // SUBLANES * SUBLANES
            acc = acc + w_ref[j:j + 1, :] * scr[s, c0 + a:c0 + a + chunk, :]
        y = acc + b_ref[...]
        mu = jnp.mean(y, axis=-1, keepdims=True)
        var = jnp.mean(jnp.square(y - mu), axis=-1, keepdims=True)
        yn = (y - mu) * lax.rsqrt(var + NORM_EPS) * g_ref[...] + bl_ref[...]
        o_ref[0, c0:c0 + chunk, :] = (yn * jax.nn.sigmoid(yn)).astype(BF16)


def _conformer_conv(glu, w_dw, b_dw, g_ln, b_ln):
    b, s, c = glu.shape
    tt = 512
    row = pl.BlockSpec((1, c), lambda i, j: (0, 0))
    return pl.pallas_call(
        functools.partial(_conv_kernel, chunk=128),
        grid=(b, s // tt),
        in_specs=[pl.BlockSpec((1, tt, c), lambda i, j: (i, j, 0)),
                  pl.BlockSpec((1, CONV_HALO, c), lambda i, j: (i, jnp.maximum(j * (tt // CONV_HALO) - 1, 0), 0)),
                  pl.BlockSpec((CONV_WIDTH, c), lambda i, j: (0, 0)), row, row, row],
        out_specs=pl.BlockSpec((1, tt, c), lambda i, j: (i, j, 0)),
        out_shape=jax.ShapeDtypeStruct((b, s, c), BF16),
        scratch_shapes=[pltpu.VMEM((SUBLANES, CONV_HALO + tt, c), F32)],
        name="conformer_conv",
    )(glu, glu, w_dw[:, 0, :], b_dw[None, :], g_ln[None, :], b_ln[None, :])


def _outproj_kernel(attn_ref, conv_ref, x_ref, permt_ref, woa_ref, woc_ref, gpost_ref, gffn_ref, wr_ref, br_ref,
                    h_ref, u_ref, idx_ref, gate_ref):
    tt = x_ref.shape[1]
    nat = []
    for sub in range(tt // PERM_TOKENS):
        rows = slice(sub * PERM_ROWS, (sub + 1) * PERM_ROWS)
        a = jnp.concatenate([attn_ref[0, p, rows, :] for p in range(N_PLANES)], axis=0)
        nat.append(jnp.dot(permt_ref[...], a, preferred_element_type=F32).astype(BF16))
    mix = (jnp.dot(jnp.concatenate(nat, axis=0), woa_ref[...], preferred_element_type=F32)
           + jnp.dot(conv_ref[0], woc_ref[...], preferred_element_type=F32))
    h = x_ref[0] + _rms(mix, gpost_ref[...])
    h_ref[0] = h
    u = _rms(h, gffn_ref[...])
    u_ref[...] = u
    u_hi = u.astype(BF16)
    u_lo = (u - u_hi.astype(F32)).astype(BF16)
    nt = (((1,), (1,)), ((), ()))
    by_hi = lax.dot_general(wr_ref[...], u_hi, nt, preferred_element_type=F32)
    by_lo = lax.dot_general(wr_ref[:N_EXPERTS], u_lo, nt, preferred_element_type=F32)
    logits = by_hi[:N_EXPERTS] + (by_hi[N_EXPERTS:] + by_lo) + br_ref[...]
    rows = lax.broadcasted_iota(jnp.int32, logits.shape, 0)
    vals = logits
    tops, idxs = [], []
    for _ in range(TOP_K):
        mx = jnp.max(vals, axis=0, keepdims=True)
        ix = jnp.min(jnp.where(vals == mx, rows, N_EXPERTS), axis=0, keepdims=True)
        tops.append(mx)
        idxs.append(ix)
        vals = jnp.where(rows == ix, -jnp.inf, vals)
    ex = [jnp.exp(t - tops[0]) for t in tops]
    den = ex[0] + ex[1] + ex[2] + ex[3]
    idx_ref[...] = jnp.concatenate(idxs, axis=0)
    gate_ref[...] = jnp.concatenate([e / den for e in ex] + [jnp.zeros((8 - TOP_K, tt), F32)], axis=0)


def _output_projection(attn, conv, x, w_out, g_mix_post, g_ffn_pre, w_router, b_router):
    b, s, d = x.shape
    tt = 512
    mc = tt // N_PLANES
    n_t = s // tt
    woa = w_out[:ATTN_WIDTH].astype(BF16)
    woc = w_out[ATTN_WIDTH:].astype(BF16)
    wr_hi = w_router.T.astype(BF16)
    wr_split = jnp.concatenate([wr_hi, (w_router.T - wr_hi.astype(F32)).astype(BF16)], axis=0)
    permt =jnp.asarray(_plane_permutation().T, BF16)
    const = lambda shape: pl.BlockSpec(shape, lambda i, j: (0, 0))
    flat = lambda w: pl.BlockSpec((tt, w), lambda i, j: (i * n_t + j, 0))
    lanes = lambda r: pl.BlockSpec((r, tt), lambda i, j: (0, i * n_t + j))
    return pl.pallas_call(
        _outproj_kernel,
        grid=(b, n_t),
        in_specs=[pl.BlockSpec((1, N_PLANES, mc, ATTN_WIDTH), lambda i, j: (i, 0, j, 0)),
                  pl.BlockSpec((1, tt, CONV_CHANNELS), lambda i, j: (i, j, 0)),
                  pl.BlockSpec((1, tt, d), lambda i, j: (i, j, 0)),
                  const(permt.shape), const(woa.shape), const(woc.shape), const((1, d)), const((1, d)),
                  const((2 * N_EXPERTS, d)), const((N_EXPERTS, 1))],
        out_specs=[pl.BlockSpec((1, tt, d), lambda i, j: (i, j, 0)), flat(d), lanes(TOP_K), lanes(8)],
        out_shape=[jax.ShapeDtypeStruct((b, s, d), F32),
                   jax.ShapeDtypeStruct((b * s, d), F32),
                   jax.ShapeDtypeStruct((TOP_K, b * s), jnp.int32),
                   jax.ShapeDtypeStruct((8, b * s), F32)],
        compiler_params=pltpu.CompilerParams(vmem_limit_bytes=VMEM_LIMIT),
        name="output_projection_router",
    )(attn, conv, x, permt, woa, woc, g_mix_post[None, :], g_ffn_pre[None, :], wr_split, b_router[:, None])


def _route_kernel(idx_ref, dest_ref, pend_ref, pad_ref, blk_ref, carry, pstart):
    phase = pl.program_id(0)
    step = pl.program_id(1)
    tt = idx_ref.shape[1]
    rows = lax.broadcasted_iota(jnp.int32, (N_EXPERTS, tt), 0)
    hot = [rows == idx_ref[k:k + 1, :] for k in range(TOP_K)]
    memb = sum(h.astype(F32) for h in hot)

    @pl.when((phase == 0) & (step == 0))
    def _():
        carry[...] = jnp.zeros_like(carry)

    @pl.when((phase == 1) & (step == 0))
    def _():
        counts = carry[...]
        padded = jnp.floor((counts + (EXPERT_ROWS - 1)) * (1.0 / EXPERT_ROWS)) * EXPERT_ROWS
        tri = (lax.broadcasted_iota(jnp.int32, (N_EXPERTS, N_EXPERTS), 1)
               <= lax.broadcasted_iota(jnp.int32, (N_EXPERTS, N_EXPERTS), 0)).astype(F32)
        pend = jnp.dot(tri, padded, precision=lax.Precision.HIGHEST, preferred_element_type=F32)
        pstart[...] = pend - padded
        pend_ref[...] = pend.astype(jnp.int32)
        pad_ref[...] = (pend - padded + counts).astype(jnp.int32)
        starts = lax.broadcasted_iota(jnp.int32, (N_EXPERTS, blk_ref.shape[1]), 1) * EXPERT_ROWS
        ended = (pend.astype(jnp.int32) <= starts).astype(jnp.int32)
        blk_ref[...] = jnp.minimum(jnp.sum(ended, axis=0, keepdims=True), N_EXPERTS - 1)
        carry[...] = jnp.zeros_like(carry)

    @pl.when(phase == 1)
    def _():
        earlier = (lax.broadcasted_iota(jnp.int32, (tt, tt), 0)
                   < lax.broadcasted_iota(jnp.int32, (tt, tt), 1)).astype(BF16)
        row = jnp.dot(memb.astype(BF16), earlier, preferred_element_type=F32) + (carry[...] + pstart[...])
        dest_ref[...] = jnp.concatenate(
            [jnp.sum(jnp.where(h, row, 0.0), axis=0, keepdims=True) for h in hot], axis=0).astype(jnp.int32)

    carry[...] = carry[...] + jnp.sum(memb, axis=1, keepdims=True)


def _routing(idx, n_blk):
    _, t = idx.shape
    tt = 512
    blk_lanes = -(-n_blk // LANES) * LANES
    return pl.pallas_call(
        _route_kernel,
        grid=(2, t // tt),
        in_specs=[pl.BlockSpec((TOP_K, tt), lambda ph, i: (0, i))],
        out_specs=[pl.BlockSpec((TOP_K, tt), lambda ph, i: (0, i * ph)),
                   pl.BlockSpec((N_EXPERTS, 1), lambda ph, i: (0, 0)),
                   pl.BlockSpec((N_EXPERTS, 1), lambda ph, i: (0, 0)),
                   pl.BlockSpec((1, blk_lanes), lambda ph, i: (0, 0))],
        out_shape=[jax.ShapeDtypeStruct((TOP_K, t), jnp.int32), jax.ShapeDtypeStruct((N_EXPERTS, 1), jnp.int32),
                   jax.ShapeDtypeStruct((N_EXPERTS, 1), jnp.int32), jax.ShapeDtypeStruct((1, blk_lanes), jnp.int32)],
        scratch_shapes=[pltpu.VMEM((N_EXPERTS, 1), F32), pltpu.VMEM((N_EXPERTS, 1), F32)],
        compiler_params=pltpu.CompilerParams(dimension_semantics=("arbitrary", "arbitrary")),
        name="moe_routing",
    )(idx)


def _slot_kernel(pad_ref, pend_ref, d0_ref, d1_ref, d2_ref, d3_ref, slot_ref, *, n_tok):
    i = pl.program_id(0)
    dest_refs = (d0_ref, d1_ref, d2_ref, d3_ref)
    tt = d0_ref.shape[0]

    @pl.when(i == 0)
    def _():
        def fill(r, c):
            slot_ref[r] = TOP_K * n_tok + lax.rem(r, EXPERT_ROWS)
            return c

        def per_expert(e, carry):
            return lax.fori_loop(pad_ref[e], pend_ref[e], fill, carry)
        lax.fori_loop(0, N_EXPERTS, per_expert, 0)
        lax.fori_loop(pend_ref[N_EXPERTS - 1], slot_ref.shape[0], fill, 0)

    unroll = 16

    def body(g, carry):
        tl0 = g * unroll
        first = [k * n_tok + i * tt + tl0 for k in range(TOP_K)]
        for u in range(unroll):
            for k in range(TOP_K):
                slot_ref[dest_refs[k][tl0 + u]] = first[k] + u
        return carry
    lax.fori_loop(0, tt // unroll, body, 0)


def _row_slots(dest, pad, pend, cap):
    _, t = dest.shape
    tt = 1024
    n_t = t // tt
    smem = pl.BlockSpec(memory_space=pltpu.SMEM)
    choice = lambda k: pl.BlockSpec((tt,), lambda i: (k * n_t + i,), memory_space=pltpu.SMEM)
    flat = dest.reshape(TOP_K * t)
    return pl.pallas_call(
        functools.partial(_slot_kernel, n_tok=t),
        grid=(n_t,),
        in_specs=[smem, smem] + [choice(k) for k in range(TOP_K)],
        out_specs=smem,
        out_shape=jax.ShapeDtypeStruct((cap,), jnp.int32),
        compiler_params=pltpu.CompilerParams(dimension_semantics=("arbitrary",)),
        name="moe_row_slots",
    )(pad, pend, flat, flat, flat, flat)


def _expert_kernel(blk_e_ref, nused_ref, slot_ref, u_hbm, wgu_ref, bgu_ref, wd_ref, bd_ref, yk_hbm,
                   xg0, xg1, yo0, yo1, wgu_b, wd_b, gsem, ssem, fence):
    j = pl.program_id(0)
    nused = nused_ref[0]
    n_tok = u_hbm.shape[0]
    d_ff = wd_ref.shape[1]
    spare = TOP_K * n_tok

    token_of = (lambda s: s & (n_tok - 1)) if n_tok & (n_tok - 1) == 0 else (lambda s: lax.rem(s, n_tok))

    def gather(block, buf, sem, rows):
        for r in rows:
            tok = token_of(slot_ref[block * EXPERT_ROWS + r])
            pltpu.make_async_copy(u_hbm.at[pl.ds(tok, 1), :], buf.at[pl.ds(r, 1), :], sem).start()

    def scatter(block, real, buf, sem, rows):
        for r in rows:
            s = jnp.where(real, slot_ref[block * EXPERT_ROWS + r], spare + r)
            pltpu.make_async_copy(buf.at[pl.ds(r, 1), :], yk_hbm.at[pl.ds(s, 1), :], sem).start()

    def wait_gather(buf, sem):
        pltpu.make_async_copy(u_hbm.at[pl.ds(0, EXPERT_ROWS), :], buf, sem).wait()

    def wait_scatter(buf, sem):
        pltpu.make_async_copy(buf, yk_hbm.at[pl.ds(0, EXPERT_ROWS), :], sem).wait()

    def step(xg_c, yo_c, gs_c, ss_c, xg_o, yo_o, gs_o, ss_o):
        wait_gather(xg_c, gs_c)
        nxt = jnp.minimum(j + 1, nused - 1)
        prv = jnp.maximum(j - 1, 0)
        gather(nxt, xg_o, gs_o, range(EXPERT_ROWS))
        scatter(prv, j > 0, yo_o, ss_o, range(EXPERT_ROWS))
        pl.semaphore_signal(fence, 1)
        pl.semaphore_wait(fence, 1)
        gu = jnp.dot(xg_c[...].astype(BF16), wgu_b[...], preferred_element_type=F32) + bgu_ref[0]
        gate = jnp.minimum(gu[:, :d_ff], SWIGLU_LIMIT)
        up = jnp.clip(gu[:, d_ff:], -SWIGLU_LIMIT, SWIGLU_LIMIT)
        hid = (up + 1.0) * (gate * jax.nn.sigmoid(gate * SWIGLU_ALPHA))
        yo_c[...] = jnp.dot(hid.astype(BF16), wd_b[...], preferred_element_type=F32) + bd_ref[0]
        wait_scatter(yo_o, ss_o)

        @pl.when(j == nused - 1)
        def _():
            scatter(j, True, yo_c, ss_c, range(EXPERT_ROWS))
            wait_scatter(yo_c, ss_c)
            wait_gather(xg_o, gs_o)

    @pl.when(j == 0)
    def _():
        yo1[...] = jnp.zeros_like(yo1)
        gather(0, xg0, gsem.at[0], range(EXPERT_ROWS))

    blk = jnp.minimum(j, nused - 1)
    new_expert = (j == 0) | (blk_e_ref[blk] != blk_e_ref[jnp.maximum(blk - 1, 0)])

    @pl.when((j < nused) & new_expert)
    def _():
        def cast(src, dst):
            def body(i, carry):
                rows = pl.ds(pl.multiple_of(i * LANES, LANES), LANES)
                dst[rows, :] = src[0, rows, :].astype(BF16)
                return carry
            lax.fori_loop(0, dst.shape[0] // LANES, body, 0)
        cast(wgu_ref, wgu_b)
        cast(wd_ref, wd_b)

    even = (xg0, yo0, gsem.at[0], ssem.at[0])
    odd = (xg1, yo1, gsem.at[1], ssem.at[1])

    @pl.when((j < nused) & (j % 2 == 0))
    def _():
        step(*even, *odd)

    @pl.when((j < nused) & (j % 2 == 1))
    def _():
        step(*odd, *even)


def _experts(u, slot, blk_e, nused, w_gate_up, b_gate_up, w_down, b_down):
    t, d = u.shape
    e, _, ff2 = w_gate_up.shape
    d_ff = w_down.shape[1]
    n_blk = slot.shape[0] // EXPERT_ROWS
    expert = lambda j, be, nu, sl: (be[jnp.minimum(j, nu[0] - 1)], 0, 0)
    grid_spec = pltpu.PrefetchScalarGridSpec(
        num_scalar_prefetch=3,
        grid=(n_blk,),
        in_specs=[pl.BlockSpec(memory_space=pl.ANY),
                  pl.BlockSpec((1, d, ff2), expert), pl.BlockSpec((1, 1, ff2), expert),
                  pl.BlockSpec((1, d_ff, d), expert), pl.BlockSpec((1, 1, d), expert)],
        out_specs=pl.BlockSpec(memory_space=pl.ANY),
        scratch_shapes=([pltpu.VMEM((EXPERT_ROWS, d), F32)] * 4
                        + [pltpu.VMEM((d, ff2), BF16), pltpu.VMEM((d_ff, d), BF16)]
                        + [pltpu.SemaphoreType.DMA((2,))] * 2 + [pltpu.SemaphoreType.REGULAR]),
    )
    return pl.pallas_call(
        _expert_kernel,
        grid_spec=grid_spec,
        out_shape=jax.ShapeDtypeStruct((TOP_K * t + EXPERT_ROWS, d), F32),
        compiler_params=pltpu.CompilerParams(dimension_semantics=("arbitrary",), vmem_limit_bytes=VMEM_LIMIT),
        name="moe_experts",
    )(blk_e, nused, slot, u, w_gate_up, b_gate_up[:, None, :], w_down, b_down[:, None, :])


def _combine_kernel(gate_ref, h_ref, p_ref, y0_ref, y1_ref, y2_ref, y3_ref, gffn_ref, wple_ref, wpg_ref, gple_ref,
                    o_ref):
    gt = gate_ref[...].T
    y = gt[:, 0:1] * y0_ref[...]
    for k, y_ref in enumerate((y1_ref, y2_ref, y3_ref), start=1):
        y = y + gt[:, k:k + 1] * y_ref[...]
    h2 = h_ref[...] + _rms(y, gffn_ref[...])
    ple = (jnp.dot(p_ref[...].astype(BF16), wple_ref[...], preferred_element_type=F32)
           * jax.nn.sigmoid(jnp.dot(h2.astype(BF16), wpg_ref[...], preferred_element_type=F32)))
    o_ref[...] = h2 + _rms(ple, gple_ref[...])


def _combine(gates, h, p, yk, g_ffn_post, w_ple, w_ple_gate, g_ple_post):
    t, d = h.shape
    tt = 256
    n_t = t // tt
    const = lambda shape: pl.BlockSpec(shape, lambda i: (0, 0))
    choice = lambda k: pl.BlockSpec((tt, d), lambda i: (k * n_t + i, 0))
    return pl.pallas_call(
        _combine_kernel,
        grid=(n_t,),
        in_specs=[pl.BlockSpec((8, tt), lambda i: (0, i)),
                  pl.BlockSpec((tt, d), lambda i: (i, 0)),
                  pl.BlockSpec((tt, p.shape[1]), lambda i: (i, 0)),
                  choice(0), choice(1), choice(2), choice(3),
                  const((1, d)), const(w_ple.shape), const(w_ple_gate.shape), const((1, d))],
        out_specs=pl.BlockSpec((tt, d), lambda i: (i, 0)),
        out_shape=jax.ShapeDtypeStruct((t, d), F32),
        compiler_params=pltpu.CompilerParams(vmem_limit_bytes=VMEM_LIMIT),
        name="moe_combine_ple",
    )(gates, h, p, yk, yk, yk, yk, g_ffn_post[None, :], w_ple.astype(BF16), w_ple_gate.astype(BF16),
      g_ple_post[None, :])


def _layer(h, p, positions, g_mix_pre, w_in, w_dw, b_dw, g_conv_ln, b_conv_ln, w_out, g_mix_post, g_ffn_pre,
           w_router, b_router, w_gate_up, b_gate_up, w_down, b_down, g_ffn_post, w_ple, w_ple_gate, g_ple_post):
    b, s, d = h.shape
    t = b * s
    cos_t, sin_t = _rope_tables(positions)
    q, k, v, glu = _input_projection(h, cos_t, sin_t, g_mix_pre, w_in)
    attn = _dilated_attention(q, k, v)
    conv = _conformer_conv(glu, w_dw, b_dw, g_conv_ln, b_conv_ln)
    h1, u_ffn, idx, gates = _output_projection(attn, conv, h, w_out, g_mix_post, g_ffn_pre, w_router, b_router)
    cap = t * TOP_K + N_EXPERTS * EXPERT_ROWS
    n_blk = cap // EXPERT_ROWS
    dest, pend, pad, blk = _routing(idx, n_blk)
    blk_e = blk[0, :n_blk]
    nused = pend[N_EXPERTS - 1] // EXPERT_ROWS
    slot = _row_slots(dest, pad[:, 0], pend[:, 0], cap)
    yk = _experts(u_ffn, slot, blk_e, nused, w_gate_up, b_gate_up, w_down, b_down)
    out = _combine(gates, h1.reshape(t, d), p.reshape(t, -1), yk, g_ffn_post, w_ple, w_ple_gate, g_ple_post)
    return out.reshape(b, s, d)


def kernel(x, p, positions, g_mix_pre, w_in, w_dw, b_dw, g_conv_ln, b_conv_ln, w_out, g_mix_post, g_ffn_pre,
           w_router, b_router, w_gate_up, b_gate_up, w_down, b_down, g_ffn_post, w_ple, w_ple_gate, g_ple_post):
    h = x
    for i in range(p.shape[0]):
        h = _layer(h, p[i], positions, g_mix_pre[i], w_in[i], w_dw[i], b_dw[i], g_conv_ln[i], b_conv_ln[i],
                   w_out[i], g_mix_post[i], g_ffn_pre[i], w_router[i], b_router[i], w_gate_up[i], b_gate_up[i],
                   w_down[i], b_down[i], g_ffn_post[i], w_ple[i], w_ple_gate[i], g_ple_post[i])
    return h
```

```python
import functools

import numpy as np
import jax
import jax.numpy as jnp
from jax import lax
from jax.experimental import pallas as pl
from jax.experimental.pallas import tpu as pltpu

F32 = jnp.float32
BF16 = jnp.bfloat16

HEAD_DIM = 64
N_HEADS = 12
ATTN_WIDTH = N_HEADS * HEAD_DIM
CONV_CHANNELS = 256
CONV_WIDTH = 31
ROPE_DIM = HEAD_DIM // 4
ROPE_THETA = 500000.0
N_EXPERTS = 32
TOP_K = 4
SWIGLU_LIMIT = 7.0
SWIGLU_ALPHA = 1.702
NORM_EPS = 1e-6
WINDOW = 128
N_PLANES = 16
SPAN = N_PLANES * WINDOW
LANES = 128
SUBLANES = 8
NEG = -1e30
EXPERT_ROWS = 256
VMEM_LIMIT = 56 * 1024 * 1024


def _residue_of_plane(p):
    return 4 * (p % 4) + p // 4


def _rms(xv, g):
    var = jnp.mean(xv * xv, axis=-1, keepdims=True)
    return xv * lax.rsqrt(var + NORM_EPS) * g


def _rope_kernel(pos_ref, invf_ref, expand_ref, one_ref, sgn_ref, c_ref, s_ref):
    tn = (((0,), (0,)), ((), ()))
    for p in range(N_PLANES):
        ang = invf_ref[...] * pos_ref[0, p:p + 1, :].astype(F32)
        spread = lambda t: lax.dot_general(t, expand_ref[...], tn, precision=lax.Precision.HIGHEST,
                                           preferred_element_type=F32)
        c_ref[0, p] = spread(jnp.cos(ang)) + one_ref[...]
        s_ref[0, p] = spread(jnp.sin(ang)) * sgn_ref[...]


def _rope_tables(positions):
    b, s = positions.shape
    sm = s // N_PLANES
    mt = min(sm, 128)
    plane_res = np.array([_residue_of_plane(p) for p in range(N_PLANES)])
    pos_planes = positions.reshape(b, sm, N_PLANES).transpose(0, 2, 1)[:, plane_res]
    half = ROPE_DIM // 2
    lane = np.arange(LANES) % HEAD_DIM
    inv_freq = ROPE_THETA ** (-jnp.arange(0, ROPE_DIM, 2, dtype=F32) / ROPE_DIM)
    rotary = lane < ROPE_DIM
    expand = jnp.asarray((np.arange(half)[:, None] == lane[None, :] % half) & rotary[None, :], F32)
    one = jnp.asarray(~rotary, F32)[None, :]
    sgn = jnp.asarray(np.where(lane < half, -1.0, 1.0), F32)[None, :]
    row = pl.BlockSpec((1, LANES), lambda i, j: (0, 0))
    out = pl.BlockSpec((1, N_PLANES, mt, LANES), lambda i, j: (i, 0, j, 0))
    return pl.pallas_call(
        _rope_kernel,
        grid=(b, sm // mt),
        in_specs=[pl.BlockSpec((1, N_PLANES, mt), lambda i, j: (i, 0, j)),
                  pl.BlockSpec((half, 1), lambda i, j: (0, 0)), pl.BlockSpec((half, LANES), lambda i, j: (0, 0)),
                  row, row],
        out_specs=[out, out],
        out_shape=[jax.ShapeDtypeStruct((b, N_PLANES, sm, LANES), F32)] * 2,
        name="rope_tables",
    )(pos_planes, inv_freq[:, None], expand, one, sgn)


PERM_TOKENS = 256
PERM_ROWS = PERM_TOKENS // N_PLANES


def _plane_permutation():
    perm = np.zeros((PERM_TOKENS, PERM_TOKENS), np.float32)
    for p in range(N_PLANES):
        for ml in range(PERM_ROWS):
            perm[PERM_ROWS * p + ml, N_PLANES * ml + _residue_of_plane(p)] = 1.0
    return perm


def _inproj_kernel(x_ref, c_ref, s_ref, g_ref, perm_ref, wqkv_ref, wc_ref, q_ref, k_ref, v_ref, glu_ref):
    g = g_ref[...]
    tt = x_ref.shape[1]
    lane = lax.broadcasted_iota(jnp.int32, (1, LANES), 1) % HEAD_DIM
    first_half = lane < ROPE_DIM // 2

    def rotary(t, cos, sin):
        outs = []
        for j in range(ATTN_WIDTH // LANES):
            tj = t[:, j * LANES:(j + 1) * LANES]
            partner = jnp.where(first_half, pltpu.roll(tj, LANES - ROPE_DIM // 2, 1),
                                pltpu.roll(tj, ROPE_DIM // 2, 1))
            outs.append(tj * cos + partner * sin)
        return jnp.concatenate(outs, axis=1)

    un = _rms(x_ref[0], g).astype(BF16)
    pc = jnp.dot(un, wc_ref[...], preferred_element_type=F32)
    glu_ref[0] = pc[:, :CONV_CHANNELS] * jax.nn.sigmoid(pc[:, CONV_CHANNELS:])

    for sub in range(tt // PERM_TOKENS):
        rows = slice(sub * PERM_ROWS, (sub + 1) * PERM_ROWS)
        u = jnp.dot(perm_ref[...], un[sub * PERM_TOKENS:(sub + 1) * PERM_TOKENS],
                    preferred_element_type=F32).astype(BF16)
        cos = jnp.concatenate([c_ref[0, p, rows, :] for p in range(N_PLANES)], axis=0)
        sin = jnp.concatenate([s_ref[0, p, rows, :] for p in range(N_PLANES)], axis=0)
        proj = jnp.dot(u, wqkv_ref[...], preferred_element_type=F32)
        q = (rotary(proj[:, :ATTN_WIDTH], cos, sin) * (HEAD_DIM ** -0.5)).astype(BF16)
        k = rotary(proj[:, ATTN_WIDTH:2 * ATTN_WIDTH], cos, sin).astype(BF16)
        v = proj[:, 2 * ATTN_WIDTH:].astype(BF16)
        for p in range(N_PLANES):
            chunk = slice(p * PERM_ROWS, (p + 1) * PERM_ROWS)
            q_ref[0, p, rows, :] = q[chunk]
            k_ref[0, p, rows, :] = k[chunk]
            v_ref[0, p, rows, :] = v[chunk]


def _input_projection(x, cos_t, sin_t, g_mix_pre, w_in):
    b, s, d = x.shape
    tt = 1024
    mc = tt // N_PLANES
    wqkv = w_in[:, :3 * ATTN_WIDTH].astype(BF16)
    wc = w_in[:, 3 * ATTN_WIDTH:].astype(BF16)
    perm = jnp.asarray(_plane_permutation(), BF16)
    plane = lambda w: pl.BlockSpec((1, N_PLANES, mc, w), lambda i, j: (i, 0, j, 0))
    plane_shape = jax.ShapeDtypeStruct((b, N_PLANES, s // N_PLANES, ATTN_WIDTH), BF16)
    tok = lambda w: pl.BlockSpec((1, tt, w), lambda i, j: (i, j, 0))
    const = lambda shape: pl.BlockSpec(shape, lambda i, j: (0, 0))
    return pl.pallas_call(
        _inproj_kernel,
        grid=(b, s // tt),
        in_specs=[tok(d), plane(LANES), plane(LANES), const((1, d)), const(perm.shape),
                  const(wqkv.shape), const(wc.shape)],
        out_specs=[plane(ATTN_WIDTH), plane(ATTN_WIDTH), plane(ATTN_WIDTH), tok(CONV_CHANNELS)],
        out_shape=[plane_shape, plane_shape, plane_shape,
                   jax.ShapeDtypeStruct((b, s, CONV_CHANNELS), F32)],
        compiler_params=pltpu.CompilerParams(vmem_limit_bytes=VMEM_LIMIT),
        name="input_projection",
    )(x, cos_t, sin_t, g_mix_pre[None, :], perm, wqkv, wc)


def _attention_biases():
    band = lambda j: np.where((j >= 0) & (j <= WINDOW), 0.0, NEG).astype(np.float32)
    cols = lambda m: np.where(m, NEG, 0.0).astype(np.float32)
    twice = lambda a: np.concatenate([a, a], axis=0)
    mq = np.arange(WINDOW)[:, None]
    kj = np.arange(2 * WINDOW)[None, :]
    j16 = mq + WINDOW - kj
    prev16 = kj < WINDOW
    row = np.arange(128)[:, None]
    col = np.arange(256)[None, :]
    j4 = 4 * (row % 32 - (col % 64 - 32)) + row // 32 - col // 64
    prev4 = col % 64 < 32
    row = np.arange(256)[:, None]
    col = np.arange(512)[None, :]
    res = np.vectorize(_residue_of_plane)
    j1 = 16 * (row % 16 - (col % 32 - 16)) + res(row // 16) - res(col // 32)
    prev1 = col % 32 < 16
    b1 = np.stack([twice(band(j1[:128])), twice(band(j1[128:]))])
    return [jnp.asarray(a) for a in (twice(band(j16)), twice(band(j4)), b1, cols(prev16), cols(prev4), cols(prev1))]


UNROLL = 8


def _attn_kernel(q_ref, kc_ref, kp_ref, vc_ref, vp_ref, b16_ref, b4_ref, b1_ref, p16_ref, p4_ref, p1_ref,
                 o_ref, m_scr, l_scr, a_scr, bias16_scr):
    no_prev = (pl.program_id(1) == 0).astype(F32)
    head0 = lax.broadcasted_iota(jnp.int32, (1, LANES), 1) < HEAD_DIM
    bias16_scr[...] = b16_ref[...] + no_prev * p16_ref[...]

    def tile(qt, kt, vt, bias):
        n = qt.shape[0]
        zero = jnp.zeros_like(qt)
        q2 = jnp.concatenate([jnp.where(head0, qt, zero), jnp.where(head0, zero, qt)], axis=0)
        s = lax.dot_general(q2, kt, (((1,), (1,)), ((), ())), preferred_element_type=F32) + bias
        m = jnp.max(s, axis=-1, keepdims=True)
        e = jnp.exp(s - m).astype(BF16)
        va = jnp.concatenate([vt, jnp.ones_like(vt)], axis=1)
        o = jnp.dot(e, va, preferred_element_type=F32)
        pick = lambda top, bot: jnp.where(head0, top, bot)
        mm = pick(jnp.broadcast_to(m[:n], (n, LANES)), jnp.broadcast_to(m[n:], (n, LANES)))
        return mm, pick(o[:n, LANES:], o[n:, LANES:]), pick(o[:n, :LANES], o[n:, :LANES])

    def put(branch, start, size, stats, off):
        for scr, val in zip((m_scr, l_scr, a_scr), stats):
            scr[branch, pl.ds(start, size), :] = val[off:off + size]

    def body16(i, carry):
        for p in [UNROLL * i + u for u in range(UNROLL)]:
            kt = jnp.concatenate([kp_ref[0, p], kc_ref[0, p]], axis=0)
            vt = jnp.concatenate([vp_ref[0, p], vc_ref[0, p]], axis=0)
            put(0, pl.multiple_of(p * WINDOW, WINDOW), WINDOW, tile(q_ref[0, p], kt, vt, bias16_scr[...]), 0)
        return carry
    lax.fori_loop(0, N_PLANES // UNROLL, body16, 0)

    def body4(g, carry):
        for c, i in [(2 * g + cc, ii) for cc in range(2) for ii in range(4)]:
            qt = jnp.concatenate([q_ref[0, 4 * c + a, 32 * i:32 * i + 32, :] for a in range(4)], axis=0)
            if i == 0:
                ks = [x for a in range(4) for x in (kp_ref[0, 4 * c + a, 96:128, :], kc_ref[0, 4 * c + a, 0:32, :])]
                vs = [x for a in range(4) for x in (vp_ref[0, 4 * c + a, 96:128, :], vc_ref[0, 4 * c + a, 0:32, :])]
                bias = b4_ref[...] + no_prev * p4_ref[...]
            else:
                ks = [kc_ref[0, 4 * c + a, 32 * i - 32:32 * i + 32, :] for a in range(4)]
                vs = [vc_ref[0, 4 * c + a, 32 * i - 32:32 * i + 32, :] for a in range(4)]
                bias = b4_ref[...]
            stats = tile(qt, jnp.concatenate(ks, axis=0), jnp.concatenate(vs, axis=0), bias)
            for a in range(4):
                put(1, pl.multiple_of((4 * c + a) * WINDOW + 32 * i, 32), 32, stats, 32 * a)
        return carry
    lax.fori_loop(0, 2, body4, 0)

    def tile1(i, first):
        rq = pl.ds(pl.multiple_of(16 * i, 16), 16)
        if first:
            ks = [x for p in range(N_PLANES) for x in (kp_ref[0, p, 112:128, :], kc_ref[0, p, 0:16, :])]
            vs = [x for p in range(N_PLANES) for x in (vp_ref[0, p, 112:128, :], vc_ref[0, p, 0:16, :])]
        else:
            rk = pl.ds(pl.multiple_of(16 * i - 16, 16), 32)
            ks = [kc_ref[0, p, rk, :] for p in range(N_PLANES)]
            vs = [vc_ref[0, p, rk, :] for p in range(N_PLANES)]
        kt = jnp.concatenate(ks, axis=0)
        vt = jnp.concatenate(vs, axis=0)
        for half in range(2):
            planes = range(8 * half, 8 * half + 8)
            qt = jnp.concatenate([q_ref[0, p, rq, :] for p in planes], axis=0)
            bias = b1_ref[half] + no_prev * p1_ref[...] if first else b1_ref[half]
            stats = tile(qt, kt, vt, bias)
            for p in planes:
                put(2, pl.multiple_of(p * WINDOW + 16 * i, 16), 16, stats, 16 * (p - 8 * half))

    tile1(0, True)
    tile1(1, False)

    def body1(g, carry):
        tile1(2 * g, False)
        tile1(2 * g + 1, False)
        return carry
    lax.fori_loop(1, WINDOW // 32, body1, 0)

    def combine(i, carry):
        for p in (2 * i, 2 * i + 1):
            rows = pl.ds(pl.multiple_of(p * WINDOW, WINDOW), WINDOW)
            ms = [m_scr[b, rows, :] for b in range(3)]
            mx = jnp.maximum(jnp.maximum(ms[0], ms[1]), ms[2])
            ws = [jnp.exp(m - mx) for m in ms]
            den = ws[0] * l_scr[0, rows, :] + ws[1] * l_scr[1, rows, :] + ws[2] * l_scr[2, rows, :]
            num = ws[0] * a_scr[0, rows, :] + ws[1] * a_scr[1, rows, :] + ws[2] * a_scr[2, rows, :]
            o_ref[0, p] = (num / den).astype(BF16)
        return carry
    lax.fori_loop(0, N_PLANES // 2, combine, 0)


def _dilated_attention(q, k, v):
    b, _, sm, _ = q.shape
    n_span = sm // WINDOW
    cur = pl.BlockSpec((1, N_PLANES, WINDOW, LANES), lambda i, j, h: (i, 0, j, h))
    prev = pl.BlockSpec((1, N_PLANES, WINDOW, LANES), lambda i, j, h: (i, 0, jnp.maximum(j - 1, 0), h))
    biases = _attention_biases()
    bias_specs = [pl.BlockSpec(a.shape, lambda i, j, h, nd=a.ndim: (0,) * nd) for a in biases]
    stats = pltpu.VMEM((3, SPAN, LANES), F32)
    return pl.pallas_call(
        _attn_kernel,
        grid=(b, n_span, ATTN_WIDTH // LANES),
        in_specs=[cur, cur, prev, cur, prev] + bias_specs,
        out_specs=cur,
        out_shape=jax.ShapeDtypeStruct(q.shape, BF16),
        scratch_shapes=[stats, stats, stats, pltpu.VMEM((2 * WINDOW, 2 * WINDOW), F32)],
        compiler_params=pltpu.CompilerParams(vmem_limit_bytes=VMEM_LIMIT),
        name="dilated_attention",
    )(q, k, k, v, v, *biases)


CONV_HALO = 32


def _conv_kernel(cur_ref, prev_ref, w_ref, b_ref, g_ref, bl_ref, o_ref, scr, *, chunk):
    tt = cur_ref.shape[1]
    has_prev = pl.program_id(1) > 0
    scr[0, 0:CONV_HALO, :] = jnp.where(has_prev, prev_ref[0], 0.0)
    scr[0, CONV_HALO:CONV_HALO + tt, :] = cur_ref[0]
    aligned_rows = tt + CONV_HALO - SUBLANES
    for s in range(1, SUBLANES):
        scr[s, 0:aligned_rows, :] = scr[0, s:s + aligned_rows, :]
    lead = CONV_HALO - (CONV_WIDTH - 1)
    for c0 in range(0, tt, chunk):
        acc = jnp.zeros((chunk, CONV_CHANNELS), F32)
        for j in range(CONV_WIDTH):
            s, a = (lead + j) % SUBLANES, (lead + j) // SUBLANES * SUBLANES
            acc = acc + w_ref[j:j + 1, :] * scr[s, c0 + a:c0 + a + chunk, :]
        y = acc + b_ref[...]
        mu = jnp.mean(y, axis=-1, keepdims=True)
        var = jnp.mean(jnp.square(y - mu), axis=-1, keepdims=True)
        yn = (y - mu) * lax.rsqrt(var + NORM_EPS) * g_ref[...] + bl_ref[...]
        o_ref[0, c0:c0 + chunk, :] = (yn * jax.nn.sigmoid(yn)).astype(BF16)


def _conformer_conv(glu, w_dw, b_dw, g_ln, b_ln):
    b, s, c = glu.shape
    tt = 512
    row = pl.BlockSpec((1, c), lambda i, j: (0, 0))
    return pl.pallas_call(
        functools.partial(_conv_kernel, chunk=128),
        grid=(b, s // tt),
        in_specs=[pl.BlockSpec((1, tt, c), lambda i, j: (i, j, 0)),
                  pl.BlockSpec((1, CONV_HALO, c), lambda i, j: (i, jnp.maximum(j * (tt // CONV_HALO) - 1, 0), 0)),
                  pl.BlockSpec((CONV_WIDTH, c), lambda i, j: (0, 0)), row, row, row],
        out_specs=pl.BlockSpec((1, tt, c), lambda i, j: (i, j, 0)),
        out_shape=jax.ShapeDtypeStruct((b, s, c), BF16),
        scratch_shapes=[pltpu.VMEM((SUBLANES, CONV_HALO + tt, c), F32)],
        name="conformer_conv",
    )(glu, glu, w_dw[:, 0, :], b_dw[None, :], g_ln[None, :], b_ln[None, :])


def _outproj_kernel(attn_ref, conv_ref, x_ref, permt_ref, woa_ref, woc_ref, gpost_ref, gffn_ref, wr_ref, br_ref,
                    h_ref, u_ref, idx_ref, gate_ref):
    tt = x_ref.shape[1]
    nat = []
    for sub in range(tt // PERM_TOKENS):
        rows = slice(sub * PERM_ROWS, (sub + 1) * PERM_ROWS)
        a = jnp.concatenate([attn_ref[0, p, rows, :] for p in range(N_PLANES)], axis=0)
        nat.append(jnp.dot(permt_ref[...], a, preferred_element_type=F32).astype(BF16))
    mix = (jnp.dot(jnp.concatenate(nat, axis=0), woa_ref[...], preferred_element_type=F32)
           + jnp.dot(conv_ref[0], woc_ref[...], preferred_element_type=F32))
    h = x_ref[0] + _rms(mix, gpost_ref[...])
    h_ref[0] = h
    u = _rms(h, gffn_ref[...])
    u_ref[...] = u
    u_hi = u.astype(BF16)
    u_lo = (u - u_hi.astype(F32)).astype(BF16)
    nt = (((1,), (1,)), ((), ()))
    by_hi = lax.dot_general(wr_ref[...], u_hi, nt, preferred_element_type=F32)
    by_lo = lax.dot_general(wr_ref[:N_EXPERTS], u_lo, nt, preferred_element_type=F32)
    logits = by_hi[:N_EXPERTS] + (by_hi[N_EXPERTS:] + by_lo) + br_ref[...]
    rows = lax.broadcasted_iota(jnp.int32, logits.shape, 0)
    vals = logits
    tops, idxs = [], []
    for _ in range(TOP_K):
        mx = jnp.max(vals, axis=0, keepdims=True)
        ix = jnp.min(jnp.where(vals == mx, rows, N_EXPERTS), axis=0, keepdims=True)
        tops.append(mx)
        idxs.append(ix)
        vals = jnp.where(rows == ix, -jnp.inf, vals)
    ex = [jnp.exp(t - tops[0]) for t in tops]
    den = ex[0] + ex[1] + ex[2] + ex[3]
    idx_ref[...] = jnp.concatenate(idxs, axis=0)
    gate_ref[...] = jnp.concatenate([e / den for e in ex] + [jnp.zeros((8 - TOP_K, tt), F32)], axis=0)


def _output_projection(attn, conv, x, w_out, g_mix_post, g_ffn_pre, w_router, b_router):
    b, s, d = x.shape
    tt = 512
    mc = tt // N_PLANES
    n_t = s // tt
    woa = w_out[:ATTN_WIDTH].astype(BF16)
    woc = w_out[ATTN_WIDTH:].astype(BF16)
    wr_hi = w_router.T.astype(BF16)
    wr_split = jnp.concatenate([wr_hi, (w_router.T - wr_hi.astype(F32)).astype(BF16)], axis=0)
    permt =jnp.asarray(_plane_permutation().T, BF16)
    const = lambda shape: pl.BlockSpec(shape, lambda i, j: (0, 0))
    flat = lambda w: pl.BlockSpec((tt, w), lambda i, j: (i * n_t + j, 0))
    lanes = lambda r: pl.BlockSpec((r, tt), lambda i, j: (0, i * n_t + j))
    return pl.pallas_call(
        _outproj_kernel,
        grid=(b, n_t),
        in_specs=[pl.BlockSpec((1, N_PLANES, mc, ATTN_WIDTH), lambda i, j: (i, 0, j, 0)),
                  pl.BlockSpec((1, tt, CONV_CHANNELS), lambda i, j: (i, j, 0)),
                  pl.BlockSpec((1, tt, d), lambda i, j: (i, j, 0)),
                  const(permt.shape), const(woa.shape), const(woc.shape), const((1, d)), const((1, d)),
                  const((2 * N_EXPERTS, d)), const((N_EXPERTS, 1))],
        out_specs=[pl.BlockSpec((1, tt, d), lambda i, j: (i, j, 0)), flat(d), lanes(TOP_K), lanes(8)],
        out_shape=[jax.ShapeDtypeStruct((b, s, d), F32),
                   jax.ShapeDtypeStruct((b * s, d), F32),
                   jax.ShapeDtypeStruct((TOP_K, b * s), jnp.int32),
                   jax.ShapeDtypeStruct((8, b * s), F32)],
        compiler_params=pltpu.CompilerParams(vmem_limit_bytes=VMEM_LIMIT),
        name="output_projection_router",
    )(attn, conv, x, permt, woa, woc, g_mix_post[None, :], g_ffn_pre[None, :], wr_split, b_router[:, None])


def _route_kernel(idx_ref, dest_ref, pend_ref, pad_ref, blk_ref, carry, pstart):
    phase = pl.program_id(0)
    step = pl.program_id(1)
    tt = idx_ref.shape[1]
    rows = lax.broadcasted_iota(jnp.int32, (N_EXPERTS, tt), 0)
    hot = [rows == idx_ref[k:k + 1, :] for k in range(TOP_K)]
    memb = sum(h.astype(F32) for h in hot)

    @pl.when((phase == 0) & (step == 0))
    def _():
        carry[...] = jnp.zeros_like(carry)

    @pl.when((phase == 1) & (step == 0))
    def _():
        counts = carry[...]
        padded = jnp.floor((counts + (EXPERT_ROWS - 1)) * (1.0 / EXPERT_ROWS)) * EXPERT_ROWS
        tri = (lax.broadcasted_iota(jnp.int32, (N_EXPERTS, N_EXPERTS), 1)
               <= lax.broadcasted_iota(jnp.int32, (N_EXPERTS, N_EXPERTS), 0)).astype(F32)
        pend = jnp.dot(tri, padded, precision=lax.Precision.HIGHEST, preferred_element_type=F32)
        pstart[...] = pend - padded
        pend_ref[...] = pend.astype(jnp.int32)
        pad_ref[...] = (pend - padded + counts).astype(jnp.int32)
        starts = lax.broadcasted_iota(jnp.int32, (N_EXPERTS, blk_ref.shape[1]), 1) * EXPERT_ROWS
        ended = (pend.astype(jnp.int32) <= starts).astype(jnp.int32)
        blk_ref[...] = jnp.minimum(jnp.sum(ended, axis=0, keepdims=True), N_EXPERTS - 1)
        carry[...] = jnp.zeros_like(carry)

    @pl.when(phase == 1)
    def _():
        earlier = (lax.broadcasted_iota(jnp.int32, (tt, tt), 0)
                   < lax.broadcasted_iota(jnp.int32, (tt, tt), 1)).astype(BF16)
        row = jnp.dot(memb.astype(BF16), earlier, preferred_element_type=F32) + (carry[...] + pstart[...])
        dest_ref[...] = jnp.concatenate(
            [jnp.sum(jnp.where(h, row, 0.0), axis=0, keepdims=True) for h in hot], axis=0).astype(jnp.int32)

    carry[...] = carry[...] + jnp.sum(memb, axis=1, keepdims=True)


def _routing(idx, n_blk):
    _, t = idx.shape
    tt = 512
    blk_lanes = -(-n_blk // LANES) * LANES
    return pl.pallas_call(
        _route_kernel,
        grid=(2, t // tt),
        in_specs=[pl.BlockSpec((TOP_K, tt), lambda ph, i: (0, i))],
        out_specs=[pl.BlockSpec((TOP_K, tt), lambda ph, i: (0, i * ph)),
                   pl.BlockSpec((N_EXPERTS, 1), lambda ph, i: (0, 0)),
                   pl.BlockSpec((N_EXPERTS, 1), lambda ph, i: (0, 0)),
                   pl.BlockSpec((1, blk_lanes), lambda ph, i: (0, 0))],
        out_shape=[jax.ShapeDtypeStruct((TOP_K, t), jnp.int32), jax.ShapeDtypeStruct((N_EXPERTS, 1), jnp.int32),
                   jax.ShapeDtypeStruct((N_EXPERTS, 1), jnp.int32), jax.ShapeDtypeStruct((1, blk_lanes), jnp.int32)],
        scratch_shapes=[pltpu.VMEM((N_EXPERTS, 1), F32), pltpu.VMEM((N_EXPERTS, 1), F32)],
        compiler_params=pltpu.CompilerParams(dimension_semantics=("arbitrary", "arbitrary")),
        name="moe_routing",
    )(idx)


def _slot_kernel(pad_ref, pend_ref, d0_ref, d1_ref, d2_ref, d3_ref, slot_ref, *, n_tok):
    i = pl.program_id(0)
    dest_refs = (d0_ref, d1_ref, d2_ref, d3_ref)
    tt = d0_ref.shape[0]

    @pl.when(i == 0)
    def _():
        def fill(r, c):
            slot_ref[r] = TOP_K * n_tok + lax.rem(r, EXPERT_ROWS)
            return c

        def per_expert(e, carry):
            return lax.fori_loop(pad_ref[e], pend_ref[e], fill, carry)
        lax.fori_loop(0, N_EXPERTS, per_expert, 0)
        lax.fori_loop(pend_ref[N_EXPERTS - 1], slot_ref.shape[0], fill, 0)

    unroll = 16

    def body(g, carry):
        tl0 = g * unroll
        first = [k * n_tok + i * tt + tl0 for k in range(TOP_K)]
        for u in range(unroll):
            for k in range(TOP_K):
                slot_ref[dest_refs[k][tl0 + u]] = first[k] + u
        return carry
    lax.fori_loop(0, tt // unroll, body, 0)


def _row_slots(dest, pad, pend, cap):
    _, t = dest.shape
    tt = 1024
    n_t = t // tt
    smem = pl.BlockSpec(memory_space=pltpu.SMEM)
    choice = lambda k: pl.BlockSpec((tt,), lambda i: (k * n_t + i,), memory_space=pltpu.SMEM)
    flat = dest.reshape(TOP_K * t)
    return pl.pallas_call(
        functools.partial(_slot_kernel, n_tok=t),
        grid=(n_t,),
        in_specs=[smem, smem] + [choice(k) for k in range(TOP_K)],
        out_specs=smem,
        out_shape=jax.ShapeDtypeStruct((cap,), jnp.int32),
        compiler_params=pltpu.CompilerParams(dimension_semantics=("arbitrary",)),
        name="moe_row_slots",
    )(pad, pend, flat, flat, flat, flat)


def _expert_kernel(blk_e_ref, nused_ref, slot_ref, xb_ref, wgu_ref, bgu_ref, wd_ref, bd_ref, yk_hbm,
                   yo0, yo1, wgu_b, wd_b, ssem, fence):
    j = pl.program_id(0)
    nused = nused_ref[0]
    d_ff = wd_ref.shape[1]
    spare = yk_hbm.shape[0] - EXPERT_ROWS

    def scatter(block, real, buf, sem, rows):
        for r in rows:
            s = jnp.where(real, slot_ref[block * EXPERT_ROWS + r], spare + r)
            pltpu.make_async_copy(buf.at[pl.ds(r, 1), :], yk_hbm.at[pl.ds(s, 1), :], sem).start()

    def wait_scatter(buf, sem):
        pltpu.make_async_copy(buf, yk_hbm.at[pl.ds(0, EXPERT_ROWS), :], sem).wait()

    def step(yo_c, ss_c, yo_o, ss_o):
        prv = jnp.maximum(j - 1, 0)
        scatter(prv, j > 0, yo_o, ss_o, range(EXPERT_ROWS))
        pl.semaphore_signal(fence, 1)
        pl.semaphore_wait(fence, 1)
        gu = jnp.dot(xb_ref[...].astype(BF16), wgu_b[...], preferred_element_type=F32) + bgu_ref[0]
        gate = jnp.minimum(gu[:, :d_ff], SWIGLU_LIMIT)
        up = jnp.clip(gu[:, d_ff:], -SWIGLU_LIMIT, SWIGLU_LIMIT)
        hid = (up + 1.0) * (gate * jax.nn.sigmoid(gate * SWIGLU_ALPHA))
        yo_c[...] = jnp.dot(hid.astype(BF16), wd_b[...], preferred_element_type=F32) + bd_ref[0]
        wait_scatter(yo_o, ss_o)

        @pl.when(j == nused - 1)
        def _():
            scatter(j, True, yo_c, ss_c, range(EXPERT_ROWS))
            wait_scatter(yo_c, ss_c)

    @pl.when(j == 0)
    def _():
        yo1[...] = jnp.zeros_like(yo1)

    blk = jnp.minimum(j, nused - 1)
    new_expert = (j == 0) | (blk_e_ref[blk] != blk_e_ref[jnp.maximum(blk - 1, 0)])

    @pl.when((j < nused) & new_expert)
    def _():
        def cast(src, dst):
            def body(i, carry):
                rows = pl.ds(pl.multiple_of(i * LANES, LANES), LANES)
                dst[rows, :] = src[0, rows, :].astype(BF16)
                return carry
            lax.fori_loop(0, dst.shape[0] // LANES, body, 0)
        cast(wgu_ref, wgu_b)
        cast(wd_ref, wd_b)

    @pl.when((j < nused) & (j % 2 == 0))
    def _():
        step(yo0, ssem.at[0], yo1, ssem.at[1])

    @pl.when((j < nused) & (j % 2 == 1))
    def _():
        step(yo1, ssem.at[1], yo0, ssem.at[0])


def _experts(xb, n_tok, slot, blk_e, nused, w_gate_up, b_gate_up, w_down, b_down):
    cap, d = xb.shape
    e, _, ff2 = w_gate_up.shape
    d_ff = w_down.shape[1]
    n_blk = cap // EXPERT_ROWS
    block = lambda j, be, nu, sl: (jnp.minimum(j, nu[0] - 1), 0)
    expert = lambda j, be, nu, sl: (be[jnp.minimum(j, nu[0] - 1)], 0, 0)
    grid_spec = pltpu.PrefetchScalarGridSpec(
        num_scalar_prefetch=3,
        grid=(n_blk,),
        in_specs=[pl.BlockSpec((EXPERT_ROWS, d), block),
                  pl.BlockSpec((1, d, ff2), expert), pl.BlockSpec((1, 1, ff2), expert),
                  pl.BlockSpec((1, d_ff, d), expert), pl.BlockSpec((1, 1, d), expert)],
        out_specs=pl.BlockSpec(memory_space=pl.ANY),
        scratch_shapes=([pltpu.VMEM((EXPERT_ROWS, d), F32)] * 2
                        + [pltpu.VMEM((d, ff2), BF16), pltpu.VMEM((d_ff, d), BF16)]
                        + [pltpu.SemaphoreType.DMA((2,)), pltpu.SemaphoreType.REGULAR]),
    )
    return pl.pallas_call(
        _expert_kernel,
        grid_spec=grid_spec,
        out_shape=jax.ShapeDtypeStruct((TOP_K * n_tok + EXPERT_ROWS, d), F32),
        compiler_params=pltpu.CompilerParams(dimension_semantics=("arbitrary",), vmem_limit_bytes=VMEM_LIMIT),
        name="moe_experts",
    )(blk_e, nused, slot, xb, w_gate_up, b_gate_up[:, None, :], w_down, b_down[:, None, :])


SC_ROWS = 32


def _sc_gather_rows(table, idx):
    from jax.experimental.pallas import tpu_sc as plsc
    n_rows = idx.shape[0]
    d = table.shape[1]
    info = plsc.get_sparse_core_info()
    n_core, n_sub = info.num_cores, info.num_subcores
    per_w = n_rows // (n_core * n_sub)
    n_pair = per_w // (2 * SC_ROWS)
    assert per_w * n_core * n_sub == n_rows and n_pair * 2 * SC_ROWS == per_w
    mesh = plsc.VectorSubcoreMesh(core_axis_name="c", subcore_axis_name="s")

    def body(table_hbm, idx_hbm, out_hbm, idx_v, buf0, buf1, sem0, sem1):
        base = (lax.axis_index("s") * n_core + lax.axis_index("c")) * per_w
        pltpu.sync_copy(idx_hbm.at[pl.ds(pl.multiple_of(base, 8), per_w)], idx_v)

        def gather(chunk, buf, sem):
            rows = idx_v.at[pl.ds(pl.multiple_of(chunk * SC_ROWS, 8), SC_ROWS)]
            return pltpu.make_async_copy(table_hbm.at[rows], buf, sem)

        def write(chunk, buf):
            pltpu.sync_copy(buf, out_hbm.at[pl.ds(pl.multiple_of(base + chunk * SC_ROWS, 8), SC_ROWS)])

        gather(0, buf0, sem0).start()

        @pl.loop(0, n_pair)
        def _(i):
            gather(2 * i + 1, buf1, sem1).start()
            gather(2 * i, buf0, sem0).wait()
            write(2 * i, buf0)

            @pl.when(i + 1 < n_pair)
            def _():
                gather(2 * i + 2, buf0, sem0).start()
            gather(2 * i + 1, buf1, sem1).wait()
            write(2 * i + 1, buf1)

    return pl.kernel(
        body, mesh=mesh, out_type=jax.ShapeDtypeStruct((n_rows, d), table.dtype),
        scratch_types=[pltpu.VMEM((per_w,), jnp.int32), pltpu.VMEM((SC_ROWS, d), table.dtype),
                       pltpu.VMEM((SC_ROWS, d), table.dtype), pltpu.SemaphoreType.DMA, pltpu.SemaphoreType.DMA],
        name="moe_sc_gather",
    )(table, idx)


def _combine_kernel(gate_ref, h_ref, p_ref, y0_ref, y1_ref, y2_ref, y3_ref, gffn_ref, wple_ref, wpg_ref, gple_ref,
                    o_ref):
    gt = gate_ref[...].T
    y = gt[:, 0:1] * y0_ref[...]
    for k, y_ref in enumerate((y1_ref, y2_ref, y3_ref), start=1):
        y = y + gt[:, k:k + 1] * y_ref[...]
    h2 = h_ref[...] + _rms(y, gffn_ref[...])
    ple = (jnp.dot(p_ref[...].astype(BF16), wple_ref[...], preferred_element_type=F32)
           * jax.nn.sigmoid(jnp.dot(h2.astype(BF16), wpg_ref[...], preferred_element_type=F32)))
    o_ref[...] = h2 + _rms(ple, gple_ref[...])


def _combine(gates, h, p, yk, g_ffn_post, w_ple, w_ple_gate, g_ple_post):
    t, d = h.shape
    tt = 256
    n_t = t // tt
    const = lambda shape: pl.BlockSpec(shape, lambda i: (0, 0))
    choice = lambda k: pl.BlockSpec((tt, d), lambda i: (k * n_t + i, 0))
    return pl.pallas_call(
        _combine_kernel,
        grid=(n_t,),
        in_specs=[pl.BlockSpec((8, tt), lambda i: (0, i)),
                  pl.BlockSpec((tt, d), lambda i: (i, 0)),
                  pl.BlockSpec((tt, p.shape[1]), lambda i: (i, 0)),
                  choice(0), choice(1), choice(2), choice(3),
                  const((1, d)), const(w_ple.shape), const(w_ple_gate.shape), const((1, d))],
        out_specs=pl.BlockSpec((tt, d), lambda i: (i, 0)),
        out_shape=jax.ShapeDtypeStruct((t, d), F32),
        compiler_params=pltpu.CompilerParams(vmem_limit_bytes=VMEM_LIMIT),
        name="moe_combine_ple",
    )(gates, h, p, yk, yk, yk, yk, g_ffn_post[None, :], w_ple.astype(BF16), w_ple_gate.astype(BF16),
      g_ple_post[None, :])


def _layer(h, p, positions, g_mix_pre, w_in, w_dw, b_dw, g_conv_ln, b_conv_ln, w_out, g_mix_post, g_ffn_pre,
           w_router, b_router, w_gate_up, b_gate_up, w_down, b_down, g_ffn_post, w_ple, w_ple_gate, g_ple_post):
    b, s, d = h.shape
    t = b * s
    cos_t, sin_t = _rope_tables(positions)
    q, k, v, glu = _input_projection(h, cos_t, sin_t, g_mix_pre, w_in)
    attn = _dilated_attention(q, k, v)
    conv = _conformer_conv(glu, w_dw, b_dw, g_conv_ln, b_conv_ln)
    h1, u_ffn, idx, gates = _output_projection(attn, conv, h, w_out, g_mix_post, g_ffn_pre, w_router, b_router)
    cap = t * TOP_K + N_EXPERTS * EXPERT_ROWS
    n_blk = cap // EXPERT_ROWS
    dest, pend, pad, blk = _routing(idx, n_blk)
    blk_e = blk[0, :n_blk]
    nused = pend[N_EXPERTS - 1] // EXPERT_ROWS
    slot = _row_slots(dest, pad[:, 0], pend[:, 0], cap)
    tok_of_row = slot % t
    xb = _sc_gather_rows(u_ffn, tok_of_row)
    yk = _experts(xb, t, slot, blk_e, nused, w_gate_up, b_gate_up, w_down, b_down)
    out = _combine(gates, h1.reshape(t, d), p.reshape(t, -1), yk, g_ffn_post, w_ple, w_ple_gate, g_ple_post)
    return out.reshape(b, s, d)


def kernel(x, p, positions, g_mix_pre, w_in, w_dw, b_dw, g_conv_ln, b_conv_ln, w_out, g_mix_post, g_ffn_pre,
           w_router, b_router, w_gate_up, b_gate_up, w_down, b_down, g_ffn_post, w_ple, w_ple_gate, g_ple_post):
    h = x
    for i in range(p.shape[0]):
        h = _layer(h, p[i], positions, g_mix_pre[i], w_in[i], w_dw[i], b_dw[i], g_conv_ln[i], b_conv_ln[i],
                   w_out[i], g_mix_post[i], g_ffn_pre[i], w_router[i], b_router[i], w_gate_up[i], b_gate_up[i],
                   w_down[i], b_down[i], g_ffn_post[i], w_ple[i], w_ple_gate[i], g_ple_post[i])
    return h
```

```python
import functools

import numpy as np
import jax
import jax.numpy as jnp
from jax import lax
from jax.experimental import pallas as pl
from jax.experimental.pallas import tpu as pltpu

F32 = jnp.float32
BF16 = jnp.bfloat16

HEAD_DIM = 64
N_HEADS = 12
ATTN_WIDTH = N_HEADS * HEAD_DIM
CONV_CHANNELS = 256
CONV_WIDTH = 31
ROPE_DIM = HEAD_DIM // 4
ROPE_THETA = 500000.0
N_EXPERTS = 32
TOP_K = 4
SWIGLU_LIMIT = 7.0
SWIGLU_ALPHA = 1.702
NORM_EPS = 1e-6
WINDOW = 128
N_PLANES = 16
SPAN = N_PLANES * WINDOW
LANES = 128
SUBLANES = 8
NEG = -1e30
EXPERT_ROWS = 256
VMEM_LIMIT = 56 * 1024 * 1024


def _residue_of_plane(p):
    return 4 * (p % 4) + p // 4


def _rms(xv, g):
    var = jnp.mean(xv * xv, axis=-1, keepdims=True)
    return xv * lax.rsqrt(var + NORM_EPS) * g


def _rope_kernel(pos_ref, invf_ref, expand_ref, one_ref, sgn_ref, c_ref, s_ref):
    tn = (((0,), (0,)), ((), ()))
    for p in range(N_PLANES):
        ang = invf_ref[...] * pos_ref[0, p:p + 1, :].astype(F32)
        spread = lambda t: lax.dot_general(t, expand_ref[...], tn, precision=lax.Precision.HIGHEST,
                                           preferred_element_type=F32)
        c_ref[0, p] = spread(jnp.cos(ang)) + one_ref[...]
        s_ref[0, p] = spread(jnp.sin(ang)) * sgn_ref[...]


def _rope_tables(positions):
    b, s = positions.shape
    sm = s // N_PLANES
    mt = min(sm, 128)
    plane_res = np.array([_residue_of_plane(p) for p in range(N_PLANES)])
    pos_planes = positions.reshape(b, sm, N_PLANES).transpose(0, 2, 1)[:, plane_res]
    half = ROPE_DIM // 2
    lane = np.arange(LANES) % HEAD_DIM
    inv_freq = ROPE_THETA ** (-jnp.arange(0, ROPE_DIM, 2, dtype=F32) / ROPE_DIM)
    rotary = lane < ROPE_DIM
    expand = jnp.asarray((np.arange(half)[:, None] == lane[None, :] % half) & rotary[None, :], F32)
    one = jnp.asarray(~rotary, F32)[None, :]
    sgn = jnp.asarray(np.where(lane < half, -1.0, 1.0), F32)[None, :]
    row = pl.BlockSpec((1, LANES), lambda i, j: (0, 0))
    out = pl.BlockSpec((1, N_PLANES, mt, LANES), lambda i, j: (i, 0, j, 0))
    return pl.pallas_call(
        _rope_kernel,
        grid=(b, sm // mt),
        in_specs=[pl.BlockSpec((1, N_PLANES, mt), lambda i, j: (i, 0, j)),
                  pl.BlockSpec((half, 1), lambda i, j: (0, 0)), pl.BlockSpec((half, LANES), lambda i, j: (0, 0)),
                  row, row],
        out_specs=[out, out],
        out_shape=[jax.ShapeDtypeStruct((b, N_PLANES, sm, LANES), F32)] * 2,
        name="rope_tables",
    )(pos_planes, inv_freq[:, None], expand, one, sgn)


PERM_TOKENS = 256
PERM_ROWS = PERM_TOKENS // N_PLANES


def _plane_permutation():
    perm = np.zeros((PERM_TOKENS, PERM_TOKENS), np.float32)
    for p in range(N_PLANES):
        for ml in range(PERM_ROWS):
            perm[PERM_ROWS * p + ml, N_PLANES * ml + _residue_of_plane(p)] = 1.0
    return perm


def _inproj_kernel(x_ref, c_ref, s_ref, g_ref, perm_ref, wqkv_ref, wc_ref, q_ref, k_ref, v_ref, glu_ref):
    g = g_ref[...]
    tt = x_ref.shape[1]
    lane = lax.broadcasted_iota(jnp.int32, (1, LANES), 1) % HEAD_DIM
    first_half = lane < ROPE_DIM // 2

    def rotary(t, cos, sin):
        outs = []
        for j in range(ATTN_WIDTH // LANES):
            tj = t[:, j * LANES:(j + 1) * LANES]
            partner = jnp.where(first_half, pltpu.roll(tj, LANES - ROPE_DIM // 2, 1),
                                pltpu.roll(tj, ROPE_DIM // 2, 1))
            outs.append(tj * cos + partner * sin)
        return jnp.concatenate(outs, axis=1)

    un = _rms(x_ref[0], g).astype(BF16)
    pc = jnp.dot(un, wc_ref[...], preferred_element_type=F32)
    glu_ref[0] = pc[:, :CONV_CHANNELS] * jax.nn.sigmoid(pc[:, CONV_CHANNELS:])

    for sub in range(tt // PERM_TOKENS):
        rows = slice(sub * PERM_ROWS, (sub + 1) * PERM_ROWS)
        u = jnp.dot(perm_ref[...], un[sub * PERM_TOKENS:(sub + 1) * PERM_TOKENS],
                    preferred_element_type=F32).astype(BF16)
        cos = jnp.concatenate([c_ref[0, p, rows, :] for p in range(N_PLANES)], axis=0)
        sin = jnp.concatenate([s_ref[0, p, rows, :] for p in range(N_PLANES)], axis=0)
        proj = jnp.dot(u, wqkv_ref[...], preferred_element_type=F32)
        q = (rotary(proj[:, :ATTN_WIDTH], cos, sin) * (HEAD_DIM ** -0.5)).astype(BF16)
        k = rotary(proj[:, ATTN_WIDTH:2 * ATTN_WIDTH], cos, sin).astype(BF16)
        v = proj[:, 2 * ATTN_WIDTH:].astype(BF16)
        for p in range(N_PLANES):
            chunk = slice(p * PERM_ROWS, (p + 1) * PERM_ROWS)
            q_ref[0, p, rows, :] = q[chunk]
            k_ref[0, p, rows, :] = k[chunk]
            v_ref[0, p, rows, :] = v[chunk]


def _input_projection(x, cos_t, sin_t, g_mix_pre, w_in):
    b, s, d = x.shape
    tt = 1024
    mc = tt // N_PLANES
    wqkv = w_in[:, :3 * ATTN_WIDTH].astype(BF16)
    wc = w_in[:, 3 * ATTN_WIDTH:].astype(BF16)
    perm = jnp.asarray(_plane_permutation(), BF16)
    plane = lambda w: pl.BlockSpec((1, N_PLANES, mc, w), lambda i, j: (i, 0, j, 0))
    plane_shape = jax.ShapeDtypeStruct((b, N_PLANES, s // N_PLANES, ATTN_WIDTH), BF16)
    tok = lambda w: pl.BlockSpec((1, tt, w), lambda i, j: (i, j, 0))
    const = lambda shape: pl.BlockSpec(shape, lambda i, j: (0, 0))
    return pl.pallas_call(
        _inproj_kernel,
        grid=(b, s // tt),
        in_specs=[tok(d), plane(LANES), plane(LANES), const((1, d)), const(perm.shape),
                  const(wqkv.shape), const(wc.shape)],
        out_specs=[plane(ATTN_WIDTH), plane(ATTN_WIDTH), plane(ATTN_WIDTH), tok(CONV_CHANNELS)],
        out_shape=[plane_shape, plane_shape, plane_shape,
                   jax.ShapeDtypeStruct((b, s, CONV_CHANNELS), F32)],
        compiler_params=pltpu.CompilerParams(vmem_limit_bytes=VMEM_LIMIT),
        name="input_projection",
    )(x, cos_t, sin_t, g_mix_pre[None, :], perm, wqkv, wc)


def _attention_biases():
    band = lambda j: np.where((j >= 0) & (j <= WINDOW), 0.0, NEG).astype(np.float32)
    cols = lambda m: np.where(m, NEG, 0.0).astype(np.float32)
    twice = lambda a: np.concatenate([a, a], axis=0)
    mq = np.arange(WINDOW)[:, None]
    kj = np.arange(2 * WINDOW)[None, :]
    j16 = mq + WINDOW - kj
    prev16 = kj < WINDOW
    row = np.arange(128)[:, None]
    col = np.arange(256)[None, :]
    j4 = 4 * (row % 32 - (col % 64 - 32)) + row // 32 - col // 64
    prev4 = col % 64 < 32
    row = np.arange(256)[:, None]
    col = np.arange(512)[None, :]
    res = np.vectorize(_residue_of_plane)
    j1 = 16 * (row % 16 - (col % 32 - 16)) + res(row // 16) - res(col // 32)
    prev1 = col % 32 < 16
    b1 = np.stack([twice(band(j1[:128])), twice(band(j1[128:]))])
    return [jnp.asarray(a) for a in (twice(band(j16)), twice(band(j4)), b1, cols(prev16), cols(prev4), cols(prev1))]


UNROLL = 8


def _attn_kernel(q_ref, kc_ref, kp_ref, vc_ref, vp_ref, b16_ref, b4_ref, b1_ref, p16_ref, p4_ref, p1_ref,
                 o_ref, m_scr, l_scr, a_scr, bias16_scr):
    no_prev = (pl.program_id(1) == 0).astype(F32)
    head0 = lax.broadcasted_iota(jnp.int32, (1, LANES), 1) < HEAD_DIM
    bias16_scr[...] = b16_ref[...] + no_prev * p16_ref[...]

    def tile(qt, kt, vt, bias):
        n = qt.shape[0]
        zero = jnp.zeros_like(qt)
        q2 = jnp.concatenate([jnp.where(head0, qt, zero), jnp.where(head0, zero, qt)], axis=0)
        s = lax.dot_general(q2, kt, (((1,), (1,)), ((), ())), preferred_element_type=F32) + bias
        m = jnp.max(s, axis=-1, keepdims=True)
        e = jnp.exp(s - m).astype(BF16)
        va = jnp.concatenate([vt, jnp.ones_like(vt)], axis=1)
        o = jnp.dot(e, va, preferred_element_type=F32)
        pick = lambda top, bot: jnp.where(head0, top, bot)
        mm = pick(jnp.broadcast_to(m[:n], (n, LANES)), jnp.broadcast_to(m[n:], (n, LANES)))
        return mm, pick(o[:n, LANES:], o[n:, LANES:]), pick(o[:n, :LANES], o[n:, :LANES])

    def put(branch, start, size, stats, off):
        for scr, val in zip((m_scr, l_scr, a_scr), stats):
            scr[branch, pl.ds(start, size), :] = val[off:off + size]

    def body16(i, carry):
        for p in [UNROLL * i + u for u in range(UNROLL)]:
            kt = jnp.concatenate([kp_ref[0, p], kc_ref[0, p]], axis=0)
            vt = jnp.concatenate([vp_ref[0, p], vc_ref[0, p]], axis=0)
            put(0, pl.multiple_of(p * WINDOW, WINDOW), WINDOW, tile(q_ref[0, p], kt, vt, bias16_scr[...]), 0)
        return carry
    lax.fori_loop(0, N_PLANES // UNROLL, body16, 0)

    def body4(g, carry):
        for c, i in [(2 * g + cc, ii) for cc in range(2) for ii in range(4)]:
            qt = jnp.concatenate([q_ref[0, 4 * c + a, 32 * i:32 * i + 32, :] for a in range(4)], axis=0)
            if i == 0:
                ks = [x for a in range(4) for x in (kp_ref[0, 4 * c + a, 96:128, :], kc_ref[0, 4 * c + a, 0:32, :])]
                vs = [x for a in range(4) for x in (vp_ref[0, 4 * c + a, 96:128, :], vc_ref[0, 4 * c + a, 0:32, :])]
                bias = b4_ref[...] + no_prev * p4_ref[...]
            else:
                ks = [kc_ref[0, 4 * c + a, 32 * i - 32:32 * i + 32, :] for a in range(4)]
                vs = [vc_ref[0, 4 * c + a, 32 * i - 32:32 * i + 32, :] for a in range(4)]
                bias = b4_ref[...]
            stats = tile(qt, jnp.concatenate(ks, axis=0), jnp.concatenate(vs, axis=0), bias)
            for a in range(4):
                put(1, pl.multiple_of((4 * c + a) * WINDOW + 32 * i, 32), 32, stats, 32 * a)
        return carry
    lax.fori_loop(0, 2, body4, 0)

    def tile1(i, first):
        rq = pl.ds(pl.multiple_of(16 * i, 16), 16)
        if first:
            ks = [x for p in range(N_PLANES) for x in (kp_ref[0, p, 112:128, :], kc_ref[0, p, 0:16, :])]
            vs = [x for p in range(N_PLANES) for x in (vp_ref[0, p, 112:128, :], vc_ref[0, p, 0:16, :])]
        else:
            rk = pl.ds(pl.multiple_of(16 * i - 16, 16), 32)
            ks = [kc_ref[0, p, rk, :] for p in range(N_PLANES)]
            vs = [vc_ref[0, p, rk, :] for p in range(N_PLANES)]
        kt = jnp.concatenate(ks, axis=0)
        vt = jnp.concatenate(vs, axis=0)
        for half in range(2):
            planes = range(8 * half, 8 * half + 8)
            qt = jnp.concatenate([q_ref[0, p, rq, :] for p in planes], axis=0)
            bias = b1_ref[half] + no_prev * p1_ref[...] if first else b1_ref[half]
            stats = tile(qt, kt, vt, bias)
            for p in planes:
                put(2, pl.multiple_of(p * WINDOW + 16 * i, 16), 16, stats, 16 * (p - 8 * half))

    tile1(0, True)
    tile1(1, False)

    def body1(g, carry):
        tile1(2 * g, False)
        tile1(2 * g + 1, False)
        return carry
    lax.fori_loop(1, WINDOW // 32, body1, 0)

    def combine(i, carry):
        for p in (2 * i, 2 * i + 1):
            rows = pl.ds(pl.multiple_of(p * WINDOW, WINDOW), WINDOW)
            ms = [m_scr[b, rows, :] for b in range(3)]
            mx = jnp.maximum(jnp.maximum(ms[0], ms[1]), ms[2])
            ws = [jnp.exp(m - mx) for m in ms]
            den = ws[0] * l_scr[0, rows, :] + ws[1] * l_scr[1, rows, :] + ws[2] * l_scr[2, rows, :]
            num = ws[0] * a_scr[0, rows, :] + ws[1] * a_scr[1, rows, :] + ws[2] * a_scr[2, rows, :]
            o_ref[0, p] = (num / den).astype(BF16)
        return carry
    lax.fori_loop(0, N_PLANES // 2, combine, 0)


def _dilated_attention(q, k, v):
    b, _, sm, _ = q.shape
    n_span = sm // WINDOW
    cur = pl.BlockSpec((1, N_PLANES, WINDOW, LANES), lambda i, j, h: (i, 0, j, h))
    prev = pl.BlockSpec((1, N_PLANES, WINDOW, LANES), lambda i, j, h: (i, 0, jnp.maximum(j - 1, 0), h))
    biases = _attention_biases()
    bias_specs = [pl.BlockSpec(a.shape, lambda i, j, h, nd=a.ndim: (0,) * nd) for a in biases]
    stats = pltpu.VMEM((3, SPAN, LANES), F32)
    return pl.pallas_call(
        _attn_kernel,
        grid=(b, n_span, ATTN_WIDTH // LANES),
        in_specs=[cur, cur, prev, cur, prev] + bias_specs,
        out_specs=cur,
        out_shape=jax.ShapeDtypeStruct(q.shape, BF16),
        scratch_shapes=[stats, stats, stats, pltpu.VMEM((2 * WINDOW, 2 * WINDOW), F32)],
        compiler_params=pltpu.CompilerParams(vmem_limit_bytes=VMEM_LIMIT),
        name="dilated_attention",
    )(q, k, k, v, v, *biases)


CONV_HALO = 32


def _conv_kernel(cur_ref, prev_ref, w_ref, b_ref, g_ref, bl_ref, o_ref, scr, *, chunk):
    tt = cur_ref.shape[1]
    has_prev = pl.program_id(1) > 0
    scr[0, 0:CONV_HALO, :] = jnp.where(has_prev, prev_ref[0], 0.0)
    scr[0, CONV_HALO:CONV_HALO + tt, :] = cur_ref[0]
    aligned_rows = tt + CONV_HALO - SUBLANES
    for s in range(1, SUBLANES):
        scr[s, 0:aligned_rows, :] = scr[0, s:s + aligned_rows, :]
    lead = CONV_HALO - (CONV_WIDTH - 1)
    for c0 in range(0, tt, chunk):
        acc = jnp.zeros((chunk, CONV_CHANNELS), F32)
        for j in range(CONV_WIDTH):
            s, a = (lead + j) % SUBLANES, (lead + j) // SUBLANES * SUBLANES
            acc = acc + w_ref[j:j + 1, :] * scr[s, c0 + a:c0 + a + chunk, :]
        y = acc + b_ref[...]
        mu = jnp.mean(y, axis=-1, keepdims=True)
        var = jnp.mean(jnp.square(y - mu), axis=-1, keepdims=True)
        yn = (y - mu) * lax.rsqrt(var + NORM_EPS) * g_ref[...] + bl_ref[...]
        o_ref[0, c0:c0 + chunk, :] = (yn * jax.nn.sigmoid(yn)).astype(BF16)


def _conformer_conv(glu, w_dw, b_dw, g_ln, b_ln):
    b, s, c = glu.shape
    tt = 512
    row = pl.BlockSpec((1, c), lambda i, j: (0, 0))
    return pl.pallas_call(
        functools.partial(_conv_kernel, chunk=128),
        grid=(b, s // tt),
        in_specs=[pl.BlockSpec((1, tt, c), lambda i, j: (i, j, 0)),
                  pl.BlockSpec((1, CONV_HALO, c), lambda i, j: (i, jnp.maximum(j * (tt // CONV_HALO) - 1, 0), 0)),
                  pl.BlockSpec((CONV_WIDTH, c), lambda i, j: (0, 0)), row, row, row],
        out_specs=pl.BlockSpec((1, tt, c), lambda i, j: (i, j, 0)),
        out_shape=jax.ShapeDtypeStruct((b, s, c), BF16),
        scratch_shapes=[pltpu.VMEM((SUBLANES, CONV_HALO + tt, c), F32)],
        name="conformer_conv",
    )(glu, glu, w_dw[:, 0, :], b_dw[None, :], g_ln[None, :], b_ln[None, :])


def _outproj_kernel(attn_ref, conv_ref, x_ref, permt_ref, woa_ref, woc_ref, gpost_ref, gffn_ref, wr_ref, br_ref,
                    h_ref, u_ref, idx_ref, gate_ref):
    tt = x_ref.shape[1]
    nat = []
    for sub in range(tt // PERM_TOKENS):
        rows = slice(sub * PERM_ROWS, (sub + 1) * PERM_ROWS)
        a = jnp.concatenate([attn_ref[0, p, rows, :] for p in range(N_PLANES)], axis=0)
        nat.append(jnp.dot(permt_ref[...], a, preferred_element_type=F32).astype(BF16))
    mix = (jnp.dot(jnp.concatenate(nat, axis=0), woa_ref[...], preferred_element_type=F32)
           + jnp.dot(conv_ref[0], woc_ref[...], preferred_element_type=F32))
    h = x_ref[0] + _rms(mix, gpost_ref[...])
    h_ref[0] = h
    u = _rms(h, gffn_ref[...])
    u_ref[...] = u
    u_hi = u.astype(BF16)
    u_lo = (u - u_hi.astype(F32)).astype(BF16)
    nt = (((1,), (1,)), ((), ()))
    by_hi = lax.dot_general(wr_ref[...], u_hi, nt, preferred_element_type=F32)
    by_lo = lax.dot_general(wr_ref[:N_EXPERTS], u_lo, nt, preferred_element_type=F32)
    logits = by_hi[:N_EXPERTS] + (by_hi[N_EXPERTS:] + by_lo) + br_ref[...]
    rows = lax.broadcasted_iota(jnp.int32, logits.shape, 0)
    vals = logits
    tops, idxs = [], []
    for _ in range(TOP_K):
        mx = jnp.max(vals, axis=0, keepdims=True)
        ix = jnp.min(jnp.where(vals == mx, rows, N_EXPERTS), axis=0, keepdims=True)
        tops.append(mx)
        idxs.append(ix)
        vals = jnp.where(rows == ix, -jnp.inf, vals)
    ex = [jnp.exp(t - tops[0]) for t in tops]
    den = ex[0] + ex[1] + ex[2] + ex[3]
    idx_ref[...] = jnp.concatenate(idxs, axis=0)
    gate_ref[...] = jnp.concatenate([e / den for e in ex] + [jnp.zeros((8 - TOP_K, tt), F32)], axis=0)


def _output_projection(attn, conv, x, w_out, g_mix_post, g_ffn_pre, w_router, b_router):
    b, s, d = x.shape
    tt = 512
    mc = tt // N_PLANES
    n_t = s // tt
    woa = w_out[:ATTN_WIDTH].astype(BF16)
    woc = w_out[ATTN_WIDTH:].astype(BF16)
    wr_hi = w_router.T.astype(BF16)
    wr_split = jnp.concatenate([wr_hi, (w_router.T - wr_hi.astype(F32)).astype(BF16)], axis=0)
    permt =jnp.asarray(_plane_permutation().T, BF16)
    const = lambda shape: pl.BlockSpec(shape, lambda i, j: (0, 0))
    flat = lambda w: pl.BlockSpec((tt, w), lambda i, j: (i * n_t + j, 0))
    lanes = lambda r: pl.BlockSpec((r, tt), lambda i, j: (0, i * n_t + j))
    return pl.pallas_call(
        _outproj_kernel,
        grid=(b, n_t),
        in_specs=[pl.BlockSpec((1, N_PLANES, mc, ATTN_WIDTH), lambda i, j: (i, 0, j, 0)),
                  pl.BlockSpec((1, tt, CONV_CHANNELS), lambda i, j: (i, j, 0)),
                  pl.BlockSpec((1, tt, d), lambda i, j: (i, j, 0)),
                  const(permt.shape), const(woa.shape), const(woc.shape), const((1, d)), const((1, d)),
                  const((2 * N_EXPERTS, d)), const((N_EXPERTS, 1))],
        out_specs=[pl.BlockSpec((1, tt, d), lambda i, j: (i, j, 0)), flat(d), lanes(TOP_K), lanes(8)],
        out_shape=[jax.ShapeDtypeStruct((b, s, d), F32),
                   jax.ShapeDtypeStruct((b * s, d), F32),
                   jax.ShapeDtypeStruct((TOP_K, b * s), jnp.int32),
                   jax.ShapeDtypeStruct((8, b * s), F32)],
        compiler_params=pltpu.CompilerParams(vmem_limit_bytes=VMEM_LIMIT),
        name="output_projection_router",
    )(attn, conv, x, permt, woa, woc, g_mix_post[None, :], g_ffn_pre[None, :], wr_split, b_router[:, None])


def _route_kernel(idx_ref, dest_ref, pend_ref, pad_ref, blk_ref, carry, pstart):
    phase = pl.program_id(0)
    step = pl.program_id(1)
    tt = idx_ref.shape[1]
    rows = lax.broadcasted_iota(jnp.int32, (N_EXPERTS, tt), 0)
    hot = [rows == idx_ref[k:k + 1, :] for k in range(TOP_K)]
    memb = sum(h.astype(F32) for h in hot)

    @pl.when((phase == 0) & (step == 0))
    def _():
        carry[...] = jnp.zeros_like(carry)

    @pl.when((phase == 1) & (step == 0))
    def _():
        counts = carry[...]
        padded = jnp.floor((counts + (EXPERT_ROWS - 1)) * (1.0 / EXPERT_ROWS)) * EXPERT_ROWS
        tri = (lax.broadcasted_iota(jnp.int32, (N_EXPERTS, N_EXPERTS), 1)
               <= lax.broadcasted_iota(jnp.int32, (N_EXPERTS, N_EXPERTS), 0)).astype(F32)
        pend = jnp.dot(tri, padded, precision=lax.Precision.HIGHEST, preferred_element_type=F32)
        pstart[...] = pend - padded
        pend_ref[...] = pend.astype(jnp.int32)
        pad_ref[...] = (pend - padded + counts).astype(jnp.int32)
        starts = lax.broadcasted_iota(jnp.int32, (N_EXPERTS, blk_ref.shape[1]), 1) * EXPERT_ROWS
        ended = (pend.astype(jnp.int32) <= starts).astype(jnp.int32)
        blk_ref[...] = jnp.minimum(jnp.sum(ended, axis=0, keepdims=True), N_EXPERTS - 1)
        carry[...] = jnp.zeros_like(carry)

    @pl.when(phase == 1)
    def _():
        earlier = (lax.broadcasted_iota(jnp.int32, (tt, tt), 0)
                   < lax.broadcasted_iota(jnp.int32, (tt, tt), 1)).astype(BF16)
        row = jnp.dot(memb.astype(BF16), earlier, preferred_element_type=F32) + (carry[...] + pstart[...])
        dest_ref[...] = jnp.concatenate(
            [jnp.sum(jnp.where(h, row, 0.0), axis=0, keepdims=True) for h in hot], axis=0).astype(jnp.int32)

    carry[...] = carry[...] + jnp.sum(memb, axis=1, keepdims=True)


def _routing(idx, n_blk):
    _, t = idx.shape
    tt = 512
    blk_lanes = -(-n_blk // LANES) * LANES
    return pl.pallas_call(
        _route_kernel,
        grid=(2, t // tt),
        in_specs=[pl.BlockSpec((TOP_K, tt), lambda ph, i: (0, i))],
        out_specs=[pl.BlockSpec((TOP_K, tt), lambda ph, i: (0, i * ph)),
                   pl.BlockSpec((N_EXPERTS, 1), lambda ph, i: (0, 0)),
                   pl.BlockSpec((N_EXPERTS, 1), lambda ph, i: (0, 0)),
                   pl.BlockSpec((1, blk_lanes), lambda ph, i: (0, 0))],
        out_shape=[jax.ShapeDtypeStruct((TOP_K, t), jnp.int32), jax.ShapeDtypeStruct((N_EXPERTS, 1), jnp.int32),
                   jax.ShapeDtypeStruct((N_EXPERTS, 1), jnp.int32), jax.ShapeDtypeStruct((1, blk_lanes), jnp.int32)],
        scratch_shapes=[pltpu.VMEM((N_EXPERTS, 1), F32), pltpu.VMEM((N_EXPERTS, 1), F32)],
        compiler_params=pltpu.CompilerParams(dimension_semantics=("arbitrary", "arbitrary")),
        name="moe_routing",
    )(idx)


def _slot_kernel(pad_ref, pend_ref, d0_ref, d1_ref, d2_ref, d3_ref, slot_ref, *, n_tok):
    i = pl.program_id(0)
    dest_refs = (d0_ref, d1_ref, d2_ref, d3_ref)
    tt = d0_ref.shape[0]

    @pl.when(i == 0)
    def _():
        def fill(r, c):
            slot_ref[r] = TOP_K * n_tok + lax.rem(r, EXPERT_ROWS)
            return c

        def per_expert(e, carry):
            return lax.fori_loop(pad_ref[e], pend_ref[e], fill, carry)
        lax.fori_loop(0, N_EXPERTS, per_expert, 0)
        lax.fori_loop(pend_ref[N_EXPERTS - 1], slot_ref.shape[0], fill, 0)

    unroll = 16

    def body(g, carry):
        tl0 = g * unroll
        first = [k * n_tok + i * tt + tl0 for k in range(TOP_K)]
        for u in range(unroll):
            for k in range(TOP_K):
                slot_ref[dest_refs[k][tl0 + u]] = first[k] + u
        return carry
    lax.fori_loop(0, tt // unroll, body, 0)


def _row_slots(dest, pad, pend, cap):
    _, t = dest.shape
    tt = 1024
    n_t = t // tt
    smem = pl.BlockSpec(memory_space=pltpu.SMEM)
    choice = lambda k: pl.BlockSpec((tt,), lambda i: (k * n_t + i,), memory_space=pltpu.SMEM)
    flat = dest.reshape(TOP_K * t)
    return pl.pallas_call(
        functools.partial(_slot_kernel, n_tok=t),
        grid=(n_t,),
        in_specs=[smem, smem] + [choice(k) for k in range(TOP_K)],
        out_specs=smem,
        out_shape=jax.ShapeDtypeStruct((cap,), jnp.int32),
        compiler_params=pltpu.CompilerParams(dimension_semantics=("arbitrary",)),
        name="moe_row_slots",
    )(pad, pend, flat, flat, flat, flat)


def _expert_kernel(blk_e_ref, nused_ref, xb_ref, wgu_ref, bgu_ref, wd_ref, bd_ref, *rest):
    yb_ref, wgu_b, wd_b = rest[-3:]
    j = pl.program_id(0)
    nused = nused_ref[0]
    d_ff = wd_ref.shape[1]

    new_expert = (j == 0) | (blk_e_ref[j] != blk_e_ref[jnp.maximum(j - 1, 0)])

    @pl.when((j < nused) & new_expert)
    def _():
        def cast(src, dst):
            def body(i, carry):
                rows = pl.ds(pl.multiple_of(i * LANES, LANES), LANES)
                dst[rows, :] = src[0, rows, :].astype(BF16)
                return carry
            lax.fori_loop(0, dst.shape[0] // LANES, body, 0)
        cast(wgu_ref, wgu_b)
        cast(wd_ref, wd_b)

    @pl.when(j < nused)
    def _():
        gu = jnp.dot(xb_ref[...].astype(BF16), wgu_b[...], preferred_element_type=F32) + bgu_ref[0]
        gate = jnp.minimum(gu[:, :d_ff], SWIGLU_LIMIT)
        up = jnp.clip(gu[:, d_ff:], -SWIGLU_LIMIT, SWIGLU_LIMIT)
        hid = (up + 1.0) * (gate * jax.nn.sigmoid(gate * SWIGLU_ALPHA))
        yb_ref[...] = jnp.dot(hid.astype(BF16), wd_b[...], preferred_element_type=F32) + bd_ref[0]

    @pl.when(j >= nused)
    def _():
        yb_ref[...] = jnp.zeros_like(yb_ref)


def _experts(xb, chunk, n_chunk, yb_prev, blk_e, nused, w_gate_up, b_gate_up, w_down, b_down):
    rows, d = xb.shape
    e, _, ff2 = w_gate_up.shape
    d_ff = w_down.shape[1]
    n_blk = rows // EXPERT_ROWS
    live = lambda j, nu: jnp.maximum(jnp.minimum(j, nu[0] - 1), 0)
    expert = lambda j, be, nu: (be[live(j, nu)], 0, 0)
    in_specs = [pl.BlockSpec((EXPERT_ROWS, d), lambda j, be, nu: (live(j, nu), 0)),
                pl.BlockSpec((1, d, ff2), expert), pl.BlockSpec((1, 1, ff2), expert),
                pl.BlockSpec((1, d_ff, d), expert), pl.BlockSpec((1, 1, d), expert)]
    args = [blk_e, nused, xb, w_gate_up, b_gate_up[:, None, :], w_down, b_down[:, None, :]]
    aliases = {}
    if yb_prev is not None:
        in_specs.append(pl.BlockSpec(memory_space=pl.ANY))
        args.append(yb_prev)
        aliases = {len(args) - 1: 0}
    grid_spec = pltpu.PrefetchScalarGridSpec(
        num_scalar_prefetch=2,
        grid=(n_blk,),
        in_specs=in_specs,
        out_specs=pl.BlockSpec((EXPERT_ROWS, d), lambda j, be, nu: (chunk * n_blk + j, 0)),
        scratch_shapes=[pltpu.VMEM((d, ff2), BF16), pltpu.VMEM((d_ff, d), BF16)],
    )
    return pl.pallas_call(
        _expert_kernel,
        grid_spec=grid_spec,
        out_shape=jax.ShapeDtypeStruct((n_chunk * rows, d), F32),
        input_output_aliases=aliases,
        compiler_params=pltpu.CompilerParams(dimension_semantics=("arbitrary",), vmem_limit_bytes=VMEM_LIMIT),
        name="moe_experts",
    )(*args)


SC_ROWS = 32
MOE_CHUNKS = 4


def _sc_gather_rows(table, idx):
    from jax.experimental.pallas import tpu_sc as plsc
    n_rows = idx.shape[0]
    d = table.shape[1]
    info = plsc.get_sparse_core_info()
    n_core, n_sub = info.num_cores, info.num_subcores
    per_w = n_rows // (n_core * n_sub)
    n_pair = per_w // (2 * SC_ROWS)
    assert per_w * n_core * n_sub == n_rows and n_pair * 2 * SC_ROWS == per_w
    mesh = plsc.VectorSubcoreMesh(core_axis_name="c", subcore_axis_name="s")

    def body(table_hbm, idx_hbm, out_hbm, idx_v, buf0, buf1, sem0, sem1):
        base = (lax.axis_index("s") * n_core + lax.axis_index("c")) * per_w
        pltpu.sync_copy(idx_hbm.at[pl.ds(pl.multiple_of(base, 8), per_w)], idx_v)

        def gather(chunk, buf, sem):
            rows = idx_v.at[pl.ds(pl.multiple_of(chunk * SC_ROWS, 8), SC_ROWS)]
            return pltpu.make_async_copy(table_hbm.at[rows], buf, sem)

        def write(chunk, buf):
            pltpu.sync_copy(buf, out_hbm.at[pl.ds(pl.multiple_of(base + chunk * SC_ROWS, 8), SC_ROWS)])

        gather(0, buf0, sem0).start()

        @pl.loop(0, n_pair)
        def _(i):
            gather(2 * i + 1, buf1, sem1).start()
            gather(2 * i, buf0, sem0).wait()
            write(2 * i, buf0)

            @pl.when(i + 1 < n_pair)
            def _():
                gather(2 * i + 2, buf0, sem0).start()
            gather(2 * i + 1, buf1, sem1).wait()
            write(2 * i + 1, buf1)

    return pl.kernel(
        body, mesh=mesh, out_type=jax.ShapeDtypeStruct((n_rows, d), table.dtype),
        scratch_types=[pltpu.VMEM((per_w,), jnp.int32), pltpu.VMEM((SC_ROWS, d), table.dtype),
                       pltpu.VMEM((SC_ROWS, d), table.dtype), pltpu.SemaphoreType.DMA, pltpu.SemaphoreType.DMA],
        name="moe_sc_gather",
    )(table, idx)


def _combine_kernel(gate_ref, h_ref, p_ref, y0_ref, y1_ref, y2_ref, y3_ref, gffn_ref, wple_ref, wpg_ref, gple_ref,
                    o_ref):
    gt = gate_ref[...].T
    y = gt[:, 0:1] * y0_ref[...]
    for k, y_ref in enumerate((y1_ref, y2_ref, y3_ref), start=1):
        y = y + gt[:, k:k + 1] * y_ref[...]
    h2 = h_ref[...] + _rms(y, gffn_ref[...])
    ple = (jnp.dot(p_ref[...].astype(BF16), wple_ref[...], preferred_element_type=F32)
           * jax.nn.sigmoid(jnp.dot(h2.astype(BF16), wpg_ref[...], preferred_element_type=F32)))
    o_ref[...] = h2 + _rms(ple, gple_ref[...])


def _combine_kernel_aliased(gate_ref, h_ref, p_ref, y0_ref, y1_ref, y2_ref, y3_ref, gffn_ref, wple_ref, wpg_ref,
                            gple_ref, prev_ref, o_ref):
    del prev_ref
    _combine_kernel(gate_ref, h_ref, p_ref, y0_ref, y1_ref, y2_ref, y3_ref, gffn_ref, wple_ref, wpg_ref, gple_ref,
                    o_ref)


def _combine(group, n_group, out_prev, gates, h, p, yk, g_ffn_post, w_ple_bf16, w_ple_gate_bf16, g_ple_post):
    t, d = h.shape
    tt = 256
    n_tg = t // n_group // tt
    const = lambda shape: pl.BlockSpec(shape, lambda i: (0, 0))
    choice = lambda k: pl.BlockSpec((tt, d), lambda i: (k * n_tg + i, 0))
    tok = lambda w: pl.BlockSpec((tt, w), lambda i: (group * n_tg + i, 0))
    in_specs = [pl.BlockSpec((8, tt), lambda i: (0, group * n_tg + i)), tok(d), tok(p.shape[1]),
                choice(0), choice(1), choice(2), choice(3),
                const((1, d)), const(w_ple_bf16.shape), const(w_ple_gate_bf16.shape), const((1, d))]
    args = [gates, h, p, yk, yk, yk, yk, g_ffn_post[None, :], w_ple_bf16, w_ple_gate_bf16, g_ple_post[None, :]]
    body, aliases = _combine_kernel, {}
    if out_prev is not None:
        in_specs.append(pl.BlockSpec(memory_space=pl.ANY))
        args.append(out_prev)
        body, aliases = _combine_kernel_aliased, {len(args) - 1: 0}
    return pl.pallas_call(
        body,
        grid=(n_tg,),
        in_specs=in_specs,
        out_specs=tok(d),
        out_shape=jax.ShapeDtypeStruct((t, d), F32),
        input_output_aliases=aliases,
        compiler_params=pltpu.CompilerParams(vmem_limit_bytes=VMEM_LIMIT),
        name="moe_combine_ple",
    )(*args)


def _layer(h, p, positions, g_mix_pre, w_in, w_dw, b_dw, g_conv_ln, b_conv_ln, w_out, g_mix_post, g_ffn_pre,
           w_router, b_router, w_gate_up, b_gate_up, w_down, b_down, g_ffn_post, w_ple, w_ple_gate, g_ple_post):
    b, s, d = h.shape
    t = b * s
    cos_t, sin_t = _rope_tables(positions)
    q, k, v, glu = _input_projection(h, cos_t, sin_t, g_mix_pre, w_in)
    attn = _dilated_attention(q, k, v)
    conv = _conformer_conv(glu, w_dw, b_dw, g_conv_ln, b_conv_ln)
    h1, u_ffn, idx, gates = _output_projection(attn, conv, h, w_out, g_mix_post, g_ffn_pre, w_router, b_router)
    cap = t * TOP_K + N_EXPERTS * EXPERT_ROWS
    n_blk = cap // EXPERT_ROWS
    dest, pend, pad, blk = _routing(idx, n_blk)
    blk_e = blk[0, :n_blk]
    nused = pend[N_EXPERTS - 1] // EXPERT_ROWS
    slot = _row_slots(dest, pad[:, 0], pend[:, 0], cap)
    tok_of_row = slot % t
    rows_c = cap // MOE_CHUNKS
    blk_c = n_blk // MOE_CHUNKS
    yb = None
    for c in range(MOE_CHUNKS):
        xb_c = _sc_gather_rows(u_ffn, tok_of_row[c * rows_c:(c + 1) * rows_c])
        nused_c = jnp.clip(nused - c * blk_c, 0, blk_c)
        yb = _experts(xb_c, c, MOE_CHUNKS, yb, blk_e[c * blk_c:(c + 1) * blk_c], nused_c,
                      w_gate_up, b_gate_up, w_down, b_down)
    t_g = t // MOE_CHUNKS
    w_ple_bf16, w_ple_gate_bf16 = w_ple.astype(BF16), w_ple_gate.astype(BF16)
    h1_flat, p_flat = h1.reshape(t, d), p.reshape(t, -1)
    out = None
    for g in range(MOE_CHUNKS):
        yk_g = _sc_gather_rows(yb, dest[:, g * t_g:(g + 1) * t_g].reshape(TOP_K * t_g))
        out = _combine(g, MOE_CHUNKS, out, gates, h1_flat, p_flat, yk_g, g_ffn_post, w_ple_bf16, w_ple_gate_bf16,
                       g_ple_post)
    return out.reshape(b, s, d)


def kernel(x, p, positions, g_mix_pre, w_in, w_dw, b_dw, g_conv_ln, b_conv_ln, w_out, g_mix_post, g_ffn_pre,
           w_router, b_router, w_gate_up, b_gate_up, w_down, b_down, g_ffn_post, w_ple, w_ple_gate, g_ple_post):
    h = x
    for i in range(p.shape[0]):
        h = _layer(h, p[i], positions, g_mix_pre[i], w_in[i], w_dw[i], b_dw[i], g_conv_ln[i], b_conv_ln[i],
                   w_out[i], g_mix_post[i], g_ffn_pre[i], w_router[i], b_router[i], w_gate_up[i], b_gate_up[i],
                   w_down[i], b_down[i], g_ffn_post[i], w_ple[i], w_ple_gate[i], g_ple_post[i])
    return h
```

```python
import functools

import numpy as np
import jax
import jax.numpy as jnp
from jax import lax
from jax.experimental import pallas as pl
from jax.experimental.pallas import tpu as pltpu

F32 = jnp.float32
BF16 = jnp.bfloat16

HEAD_DIM = 64
N_HEADS = 12
ATTN_WIDTH = N_HEADS * HEAD_DIM
CONV_CHANNELS = 256
CONV_WIDTH = 31
ROPE_DIM = HEAD_DIM // 4
ROPE_THETA = 500000.0
N_EXPERTS = 32
TOP_K = 4
SWIGLU_LIMIT = 7.0
SWIGLU_ALPHA = 1.702
NORM_EPS = 1e-6
WINDOW = 128
N_PLANES = 16
SPAN = N_PLANES * WINDOW
LANES = 128
SUBLANES = 8
NEG = -1e30
EXPERT_ROWS = 256
VMEM_LIMIT = 56 * 1024 * 1024


def _residue_of_plane(p):
    return 4 * (p % 4) + p // 4


def _rms(xv, g):
    var = jnp.mean(xv * xv, axis=-1, keepdims=True)
    return xv * lax.rsqrt(var + NORM_EPS) * g


def _rope_kernel(pos_ref, invf_ref, expand_ref, one_ref, sgn_ref, c_ref, s_ref):
    tn = (((0,), (0,)), ((), ()))
    for p in range(N_PLANES):
        ang = invf_ref[...] * pos_ref[0, p:p + 1, :].astype(F32)
        spread = lambda t: lax.dot_general(t, expand_ref[...], tn, precision=lax.Precision.HIGHEST,
                                           preferred_element_type=F32)
        c_ref[0, p] = spread(jnp.cos(ang)) + one_ref[...]
        s_ref[0, p] = spread(jnp.sin(ang)) * sgn_ref[...]


def _rope_tables(positions):
    b, s = positions.shape
    sm = s // N_PLANES
    mt = min(sm, 128)
    plane_res = np.array([_residue_of_plane(p) for p in range(N_PLANES)])
    pos_planes = positions.reshape(b, sm, N_PLANES).transpose(0, 2, 1)[:, plane_res]
    half = ROPE_DIM // 2
    lane = np.arange(LANES) % HEAD_DIM
    inv_freq = ROPE_THETA ** (-jnp.arange(0, ROPE_DIM, 2, dtype=F32) / ROPE_DIM)
    rotary = lane < ROPE_DIM
    expand = jnp.asarray((np.arange(half)[:, None] == lane[None, :] % half) & rotary[None, :], F32)
    one = jnp.asarray(~rotary, F32)[None, :]
    sgn = jnp.asarray(np.where(lane < half, -1.0, 1.0), F32)[None, :]
    row = pl.BlockSpec((1, LANES), lambda i, j: (0, 0))
    out = pl.BlockSpec((1, N_PLANES, mt, LANES), lambda i, j: (i, 0, j, 0))
    return pl.pallas_call(
        _rope_kernel,
        grid=(b, sm // mt),
        in_specs=[pl.BlockSpec((1, N_PLANES, mt), lambda i, j: (i, 0, j)),
                  pl.BlockSpec((half, 1), lambda i, j: (0, 0)), pl.BlockSpec((half, LANES), lambda i, j: (0, 0)),
                  row, row],
        out_specs=[out, out],
        out_shape=[jax.ShapeDtypeStruct((b, N_PLANES, sm, LANES), F32)] * 2,
        name="rope_tables",
    )(pos_planes, inv_freq[:, None], expand, one, sgn)


PERM_TOKENS = 256
PERM_ROWS = PERM_TOKENS // N_PLANES


def _plane_permutation():
    perm = np.zeros((PERM_TOKENS, PERM_TOKENS), np.float32)
    for p in range(N_PLANES):
        for ml in range(PERM_ROWS):
            perm[PERM_ROWS * p + ml, N_PLANES * ml + _residue_of_plane(p)] = 1.0
    return perm


def _inproj_kernel(x_ref, c_ref, s_ref, g_ref, perm_ref, wqkv_ref, wc_ref, q_ref, k_ref, v_ref, glu_ref):
    g = g_ref[...]
    tt = x_ref.shape[1]
    lane = lax.broadcasted_iota(jnp.int32, (1, LANES), 1) % HEAD_DIM
    first_half = lane < ROPE_DIM // 2

    def rotary(t, cos, sin):
        outs = []
        for j in range(ATTN_WIDTH // LANES):
            tj = t[:, j * LANES:(j + 1) * LANES]
            partner = jnp.where(first_half, pltpu.roll(tj, LANES - ROPE_DIM // 2, 1),
                                pltpu.roll(tj, ROPE_DIM // 2, 1))
            outs.append(tj * cos + partner * sin)
        return jnp.concatenate(outs, axis=1)

    un = _rms(x_ref[0], g).astype(BF16)
    pc = jnp.dot(un, wc_ref[...], preferred_element_type=F32)
    glu_ref[0] = pc[:, :CONV_CHANNELS] * jax.nn.sigmoid(pc[:, CONV_CHANNELS:])

    for sub in range(tt // PERM_TOKENS):
        rows = slice(sub * PERM_ROWS, (sub + 1) * PERM_ROWS)
        u = jnp.dot(perm_ref[...], un[sub * PERM_TOKENS:(sub + 1) * PERM_TOKENS],
                    preferred_element_type=F32).astype(BF16)
        cos = jnp.concatenate([c_ref[0, p, rows, :] for p in range(N_PLANES)], axis=0)
        sin = jnp.concatenate([s_ref[0, p, rows, :] for p in range(N_PLANES)], axis=0)
        proj = jnp.dot(u, wqkv_ref[...], preferred_element_type=F32)
        q = (rotary(proj[:, :ATTN_WIDTH], cos, sin) * (HEAD_DIM ** -0.5)).astype(BF16)
        k = rotary(proj[:, ATTN_WIDTH:2 * ATTN_WIDTH], cos, sin).astype(BF16)
        v = proj[:, 2 * ATTN_WIDTH:].astype(BF16)
        for p in range(N_PLANES):
            chunk = slice(p * PERM_ROWS, (p + 1) * PERM_ROWS)
            q_ref[0, p, rows, :] = q[chunk]
            k_ref[0, p, rows, :] = k[chunk]
            v_ref[0, p, rows, :] = v[chunk]


def _input_projection(x, cos_t, sin_t, g_mix_pre, w_in):
    b, s, d = x.shape
    tt = 1024
    mc = tt // N_PLANES
    wqkv = w_in[:, :3 * ATTN_WIDTH].astype(BF16)
    wc = w_in[:, 3 * ATTN_WIDTH:].astype(BF16)
    perm = jnp.asarray(_plane_permutation(), BF16)
    plane = lambda w: pl.BlockSpec((1, N_PLANES, mc, w), lambda i, j: (i, 0, j, 0))
    plane_shape = jax.ShapeDtypeStruct((b, N_PLANES, s // N_PLANES, ATTN_WIDTH), BF16)
    tok = lambda w: pl.BlockSpec((1, tt, w), lambda i, j: (i, j, 0))
    const = lambda shape: pl.BlockSpec(shape, lambda i, j: (0, 0))
    return pl.pallas_call(
        _inproj_kernel,
        grid=(b, s // tt),
        in_specs=[tok(d), plane(LANES), plane(LANES), const((1, d)), const(perm.shape),
                  const(wqkv.shape), const(wc.shape)],
        out_specs=[plane(ATTN_WIDTH), plane(ATTN_WIDTH), plane(ATTN_WIDTH), tok(CONV_CHANNELS)],
        out_shape=[plane_shape, plane_shape, plane_shape,
                   jax.ShapeDtypeStruct((b, s, CONV_CHANNELS), F32)],
        compiler_params=pltpu.CompilerParams(vmem_limit_bytes=VMEM_LIMIT),
        name="input_projection",
    )(x, cos_t, sin_t, g_mix_pre[None, :], perm, wqkv, wc)


def _attention_biases():
    band = lambda j: np.where((j >= 0) & (j <= WINDOW), 0.0, NEG).astype(np.float32)
    cols = lambda m: np.where(m, NEG, 0.0).astype(np.float32)
    twice = lambda a: np.concatenate([a, a], axis=0)
    mq = np.arange(WINDOW)[:, None]
    kj = np.arange(2 * WINDOW)[None, :]
    j16 = mq + WINDOW - kj
    prev16 = kj < WINDOW
    row = np.arange(128)[:, None]
    col = np.arange(256)[None, :]
    j4 = 4 * (row % 32 - (col % 64 - 32)) + row // 32 - col // 64
    prev4 = col % 64 < 32
    row = np.arange(256)[:, None]
    col = np.arange(512)[None, :]
    res = np.vectorize(_residue_of_plane)
    j1 = 16 * (row % 16 - (col % 32 - 16)) + res(row // 16) - res(col // 32)
    prev1 = col % 32 < 16
    b1 = np.stack([twice(band(j1[:128])), twice(band(j1[128:]))])
    return [jnp.asarray(a) for a in (twice(band(j16)), twice(band(j4)), b1, cols(prev16), cols(prev4), cols(prev1))]


UNROLL = 8


def _attn_kernel(q_ref, kc_ref, kp_ref, vc_ref, vp_ref, b16_ref, b4_ref, b1_ref, p16_ref, p4_ref, p1_ref,
                 o_ref, m_scr, l_scr, a_scr, bias16_scr):
    no_prev = (pl.program_id(1) == 0).astype(F32)
    head0 = lax.broadcasted_iota(jnp.int32, (1, LANES), 1) < HEAD_DIM
    bias16_scr[...] = b16_ref[...] + no_prev * p16_ref[...]

    def tile(qt, kt, vt, bias):
        n = qt.shape[0]
        zero = jnp.zeros_like(qt)
        q2 = jnp.concatenate([jnp.where(head0, qt, zero), jnp.where(head0, zero, qt)], axis=0)
        s = lax.dot_general(q2, kt, (((1,), (1,)), ((), ())), preferred_element_type=F32) + bias
        m = jnp.max(s, axis=-1, keepdims=True)
        e = jnp.exp(s - m).astype(BF16)
        va = jnp.concatenate([vt, jnp.ones_like(vt)], axis=1)
        o = jnp.dot(e, va, preferred_element_type=F32)
        pick = lambda top, bot: jnp.where(head0, top, bot)
        mm = pick(jnp.broadcast_to(m[:n], (n, LANES)), jnp.broadcast_to(m[n:], (n, LANES)))
        return mm, pick(o[:n, LANES:], o[n:, LANES:]), pick(o[:n, :LANES], o[n:, :LANES])

    def put(branch, start, size, stats, off):
        for scr, val in zip((m_scr, l_scr, a_scr), stats):
            scr[branch, pl.ds(start, size), :] = val[off:off + size]

    def body16(i, carry):
        for p in [UNROLL * i + u for u in range(UNROLL)]:
            kt = jnp.concatenate([kp_ref[0, p], kc_ref[0, p]], axis=0)
            vt = jnp.concatenate([vp_ref[0, p], vc_ref[0, p]], axis=0)
            put(0, pl.multiple_of(p * WINDOW, WINDOW), WINDOW, tile(q_ref[0, p], kt, vt, bias16_scr[...]), 0)
        return carry
    lax.fori_loop(0, N_PLANES // UNROLL, body16, 0)

    def body4(g, carry):
        for c, i in [(2 * g + cc, ii) for cc in range(2) for ii in range(4)]:
            qt = jnp.concatenate([q_ref[0, 4 * c + a, 32 * i:32 * i + 32, :] for a in range(4)], axis=0)
            if i == 0:
                ks = [x for a in range(4) for x in (kp_ref[0, 4 * c + a, 96:128, :], kc_ref[0, 4 * c + a, 0:32, :])]
                vs = [x for a in range(4) for x in (vp_ref[0, 4 * c + a, 96:128, :], vc_ref[0, 4 * c + a, 0:32, :])]
                bias = b4_ref[...] + no_prev * p4_ref[...]
            else:
                ks = [kc_ref[0, 4 * c + a, 32 * i - 32:32 * i + 32, :] for a in range(4)]
                vs = [vc_ref[0, 4 * c + a, 32 * i - 32:32 * i + 32, :] for a in range(4)]
                bias = b4_ref[...]
            stats = tile(qt, jnp.concatenate(ks, axis=0), jnp.concatenate(vs, axis=0), bias)
            for a in range(4):
                put(1, pl.multiple_of((4 * c + a) * WINDOW + 32 * i, 32), 32, stats, 32 * a)
        return carry
    lax.fori_loop(0, 2, body4, 0)

    def tile1(i, first):
        rq = pl.ds(pl.multiple_of(16 * i, 16), 16)
        if first:
            ks = [x for p in range(N_PLANES) for x in (kp_ref[0, p, 112:128, :], kc_ref[0, p, 0:16, :])]
            vs = [x for p in range(N_PLANES) for x in (vp_ref[0, p, 112:128, :], vc_ref[0, p, 0:16, :])]
        else:
            rk = pl.ds(pl.multiple_of(16 * i - 16, 16), 32)
            ks = [kc_ref[0, p, rk, :] for p in range(N_PLANES)]
            vs = [vc_ref[0, p, rk, :] for p in range(N_PLANES)]
        kt = jnp.concatenate(ks, axis=0)
        vt = jnp.concatenate(vs, axis=0)
        for half in range(2):
            planes = range(8 * half, 8 * half + 8)
            qt = jnp.concatenate([q_ref[0, p, rq, :] for p in planes], axis=0)
            bias = b1_ref[half] + no_prev * p1_ref[...] if first else b1_ref[half]
            stats = tile(qt, kt, vt, bias)
            for p in planes:
                put(2, pl.multiple_of(p * WINDOW + 16 * i, 16), 16, stats, 16 * (p - 8 * half))

    tile1(0, True)
    tile1(1, False)

    def body1(g, carry):
        tile1(2 * g, False)
        tile1(2 * g + 1, False)
        return carry
    lax.fori_loop(1, WINDOW // 32, body1, 0)

    def combine(i, carry):
        for p in (2 * i, 2 * i + 1):
            rows = pl.ds(pl.multiple_of(p * WINDOW, WINDOW), WINDOW)
            ms = [m_scr[b, rows, :] for b in range(3)]
            mx = jnp.maximum(jnp.maximum(ms[0], ms[1]), ms[2])
            ws = [jnp.exp(m - mx) for m in ms]
            den = ws[0] * l_scr[0, rows, :] + ws[1] * l_scr[1, rows, :] + ws[2] * l_scr[2, rows, :]
            num = ws[0] * a_scr[0, rows, :] + ws[1] * a_scr[1, rows, :] + ws[2] * a_scr[2, rows, :]
            o_ref[0, p] = (num / den).astype(BF16)
        return carry
    lax.fori_loop(0, N_PLANES // 2, combine, 0)


def _dilated_attention(q, k, v):
    b, _, sm, _ = q.shape
    n_span = sm // WINDOW
    cur = pl.BlockSpec((1, N_PLANES, WINDOW, LANES), lambda i, j, h: (i, 0, j, h))
    prev = pl.BlockSpec((1, N_PLANES, WINDOW, LANES), lambda i, j, h: (i, 0, jnp.maximum(j - 1, 0), h))
    biases = _attention_biases()
    bias_specs = [pl.BlockSpec(a.shape, lambda i, j, h, nd=a.ndim: (0,) * nd) for a in biases]
    stats = pltpu.VMEM((3, SPAN, LANES), F32)
    return pl.pallas_call(
        _attn_kernel,
        grid=(b, n_span, ATTN_WIDTH // LANES),
        in_specs=[cur, cur, prev, cur, prev] + bias_specs,
        out_specs=cur,
        out_shape=jax.ShapeDtypeStruct(q.shape, BF16),
        scratch_shapes=[stats, stats, stats, pltpu.VMEM((2 * WINDOW, 2 * WINDOW), F32)],
        compiler_params=pltpu.CompilerParams(vmem_limit_bytes=VMEM_LIMIT),
        name="dilated_attention",
    )(q, k, k, v, v, *biases)


CONV_HALO = 32


def _conv_kernel(cur_ref, prev_ref, w_ref, b_ref, g_ref, bl_ref, o_ref, scr, *, chunk):
    tt = cur_ref.shape[1]
    has_prev = pl.program_id(1) > 0
    scr[0, 0:CONV_HALO, :] = jnp.where(has_prev, prev_ref[0], 0.0)
    scr[0, CONV_HALO:CONV_HALO + tt, :] = cur_ref[0]
    aligned_rows = tt + CONV_HALO - SUBLANES
    for s in range(1, SUBLANES):
        scr[s, 0:aligned_rows, :] = scr[0, s:s + aligned_rows, :]
    lead = CONV_HALO - (CONV_WIDTH - 1)
    for c0 in range(0, tt, chunk):
        acc = jnp.zeros((chunk, CONV_CHANNELS), F32)
        for j in range(CONV_WIDTH):
            s, a = (lead + j) % SUBLANES, (lead + j) // SUBLANES * SUBLANES
            acc = acc + w_ref[j:j + 1, :] * scr[s, c0 + a:c0 + a + chunk, :]
        y = acc + b_ref[...]
        mu = jnp.mean(y, axis=-1, keepdims=True)
        var = jnp.mean(jnp.square(y - mu), axis=-1, keepdims=True)
        yn = (y - mu) * lax.rsqrt(var + NORM_EPS) * g_ref[...] + bl_ref[...]
        o_ref[0, c0:c0 + chunk, :] = (yn * jax.nn.sigmoid(yn)).astype(BF16)


def _conformer_conv(glu, w_dw, b_dw, g_ln, b_ln):
    b, s, c = glu.shape
    tt = 512
    row = pl.BlockSpec((1, c), lambda i, j: (0, 0))
    return pl.pallas_call(
        functools.partial(_conv_kernel, chunk=128),
        grid=(b, s // tt),
        in_specs=[pl.BlockSpec((1, tt, c), lambda i, j: (i, j, 0)),
                  pl.BlockSpec((1, CONV_HALO, c), lambda i, j: (i, jnp.maximum(j * (tt // CONV_HALO) - 1, 0), 0)),
                  pl.BlockSpec((CONV_WIDTH, c), lambda i, j: (0, 0)), row, row, row],
        out_specs=pl.BlockSpec((1, tt, c), lambda i, j: (i, j, 0)),
        out_shape=jax.ShapeDtypeStruct((b, s, c), BF16),
        scratch_shapes=[pltpu.VMEM((SUBLANES, CONV_HALO + tt, c), F32)],
        name="conformer_conv",
    )(glu, glu, w_dw[:, 0, :], b_dw[None, :], g_ln[None, :], b_ln[None, :])


def _outproj_kernel(attn_ref, conv_ref, x_ref, permt_ref, woa_ref, woc_ref, gpost_ref, gffn_ref, wr_ref, br_ref,
                    h_ref, u_ref, idx_ref, gate_ref):
    tt = x_ref.shape[1]
    nat = []
    for sub in range(tt // PERM_TOKENS):
        rows = slice(sub * PERM_ROWS, (sub + 1) * PERM_ROWS)
        a = jnp.concatenate([attn_ref[0, p, rows, :] for p in range(N_PLANES)], axis=0)
        nat.append(jnp.dot(permt_ref[...], a, preferred_element_type=F32).astype(BF16))
    mix = (jnp.dot(jnp.concatenate(nat, axis=0), woa_ref[...], preferred_element_type=F32)
           + jnp.dot(conv_ref[0], woc_ref[...], preferred_element_type=F32))
    h = x_ref[0] + _rms(mix, gpost_ref[...])
    h_ref[0] = h
    u = _rms(h, gffn_ref[...])
    half = u.shape[1] // 2
    ub = u.astype(BF16).astype(F32)
    u_ref[...] = ((lax.bitcast_convert_type(ub[:, :half], jnp.uint32) >> 16)
                  | (lax.bitcast_convert_type(ub[:, half:], jnp.uint32) & jnp.uint32(0xFFFF0000)))
    u_hi = u.astype(BF16)
    u_lo = (u - u_hi.astype(F32)).astype(BF16)
    nt = (((1,), (1,)), ((), ()))
    by_hi = lax.dot_general(wr_ref[...], u_hi, nt, preferred_element_type=F32)
    by_lo = lax.dot_general(wr_ref[:N_EXPERTS], u_lo, nt, preferred_element_type=F32)
    logits = by_hi[:N_EXPERTS] + (by_hi[N_EXPERTS:] + by_lo) + br_ref[...]
    rows = lax.broadcasted_iota(jnp.int32, logits.shape, 0)
    vals = logits
    tops, idxs = [], []
    for _ in range(TOP_K):
        mx = jnp.max(vals, axis=0, keepdims=True)
        ix = jnp.min(jnp.where(vals == mx, rows, N_EXPERTS), axis=0, keepdims=True)
        tops.append(mx)
        idxs.append(ix)
        vals = jnp.where(rows == ix, -jnp.inf, vals)
    ex = [jnp.exp(t - tops[0]) for t in tops]
    den = ex[0] + ex[1] + ex[2] + ex[3]
    idx_ref[...] = jnp.concatenate(idxs, axis=0)
    gate_ref[...] = jnp.concatenate([e / den for e in ex] + [jnp.zeros((8 - TOP_K, tt), F32)], axis=0)


def _output_projection(attn, conv, x, w_out, g_mix_post, g_ffn_pre, w_router, b_router):
    b, s, d = x.shape
    tt = 512
    mc = tt // N_PLANES
    n_t = s // tt
    woa = w_out[:ATTN_WIDTH].astype(BF16)
    woc = w_out[ATTN_WIDTH:].astype(BF16)
    wr_hi = w_router.T.astype(BF16)
    wr_split = jnp.concatenate([wr_hi, (w_router.T - wr_hi.astype(F32)).astype(BF16)], axis=0)
    permt =jnp.asarray(_plane_permutation().T, BF16)
    const = lambda shape: pl.BlockSpec(shape, lambda i, j: (0, 0))
    flat = lambda w: pl.BlockSpec((tt, w), lambda i, j: (i * n_t + j, 0))
    lanes = lambda r: pl.BlockSpec((r, tt), lambda i, j: (0, i * n_t + j))
    return pl.pallas_call(
        _outproj_kernel,
        grid=(b, n_t),
        in_specs=[pl.BlockSpec((1, N_PLANES, mc, ATTN_WIDTH), lambda i, j: (i, 0, j, 0)),
                  pl.BlockSpec((1, tt, CONV_CHANNELS), lambda i, j: (i, j, 0)),
                  pl.BlockSpec((1, tt, d), lambda i, j: (i, j, 0)),
                  const(permt.shape), const(woa.shape), const(woc.shape), const((1, d)), const((1, d)),
                  const((2 * N_EXPERTS, d)), const((N_EXPERTS, 1))],
        out_specs=[pl.BlockSpec((1, tt, d), lambda i, j: (i, j, 0)), flat(d // 2), lanes(TOP_K), lanes(8)],
        out_shape=[jax.ShapeDtypeStruct((b, s, d), F32),
                   jax.ShapeDtypeStruct((b * s, d // 2), jnp.uint32),
                   jax.ShapeDtypeStruct((TOP_K, b * s), jnp.int32),
                   jax.ShapeDtypeStruct((8, b * s), F32)],
        compiler_params=pltpu.CompilerParams(vmem_limit_bytes=VMEM_LIMIT),
        name="output_projection_router",
    )(attn, conv, x, permt, woa, woc, g_mix_post[None, :], g_ffn_pre[None, :], wr_split, b_router[:, None])


def _route_kernel(idx_ref, dest_ref, pend_ref, pad_ref, blk_ref, carry, pstart):
    phase = pl.program_id(0)
    step = pl.program_id(1)
    tt = idx_ref.shape[1]
    rows = lax.broadcasted_iota(jnp.int32, (N_EXPERTS, tt), 0)
    hot = [rows == idx_ref[k:k + 1, :] for k in range(TOP_K)]
    memb = sum(h.astype(F32) for h in hot)

    @pl.when((phase == 0) & (step == 0))
    def _():
        carry[...] = jnp.zeros_like(carry)

    @pl.when((phase == 1) & (step == 0))
    def _():
        counts = carry[...]
        padded = jnp.floor((counts + (EXPERT_ROWS - 1)) * (1.0 / EXPERT_ROWS)) * EXPERT_ROWS
        tri = (lax.broadcasted_iota(jnp.int32, (N_EXPERTS, N_EXPERTS), 1)
               <= lax.broadcasted_iota(jnp.int32, (N_EXPERTS, N_EXPERTS), 0)).astype(F32)
        pend = jnp.dot(tri, padded, precision=lax.Precision.HIGHEST, preferred_element_type=F32)
        pstart[...] = pend - padded
        pend_ref[...] = pend.astype(jnp.int32)
        pad_ref[...] = (pend - padded + counts).astype(jnp.int32)
        starts = lax.broadcasted_iota(jnp.int32, (N_EXPERTS, blk_ref.shape[1]), 1) * EXPERT_ROWS
        ended = (pend.astype(jnp.int32) <= starts).astype(jnp.int32)
        blk_ref[...] = jnp.minimum(jnp.sum(ended, axis=0, keepdims=True), N_EXPERTS - 1)
        carry[...] = jnp.zeros_like(carry)

    @pl.when(phase == 1)
    def _():
        earlier = (lax.broadcasted_iota(jnp.int32, (tt, tt), 0)
                   < lax.broadcasted_iota(jnp.int32, (tt, tt), 1)).astype(BF16)
        row = jnp.dot(memb.astype(BF16), earlier, preferred_element_type=F32) + (carry[...] + pstart[...])
        dest_ref[...] = jnp.concatenate(
            [jnp.sum(jnp.where(h, row, 0.0), axis=0, keepdims=True) for h in hot], axis=0).astype(jnp.int32)

    carry[...] = carry[...] + jnp.sum(memb, axis=1, keepdims=True)


def _routing(idx, n_blk):
    _, t = idx.shape
    tt = 512
    blk_lanes = -(-n_blk // LANES) * LANES
    return pl.pallas_call(
        _route_kernel,
        grid=(2, t // tt),
        in_specs=[pl.BlockSpec((TOP_K, tt), lambda ph, i: (0, i))],
        out_specs=[pl.BlockSpec((TOP_K, tt), lambda ph, i: (0, i * ph)),
                   pl.BlockSpec((N_EXPERTS, 1), lambda ph, i: (0, 0)),
                   pl.BlockSpec((N_EXPERTS, 1), lambda ph, i: (0, 0)),
                   pl.BlockSpec((1, blk_lanes), lambda ph, i: (0, 0))],
        out_shape=[jax.ShapeDtypeStruct((TOP_K, t), jnp.int32), jax.ShapeDtypeStruct((N_EXPERTS, 1), jnp.int32),
                   jax.ShapeDtypeStruct((N_EXPERTS, 1), jnp.int32), jax.ShapeDtypeStruct((1, blk_lanes), jnp.int32)],
        scratch_shapes=[pltpu.VMEM((N_EXPERTS, 1), F32), pltpu.VMEM((N_EXPERTS, 1), F32)],
        compiler_params=pltpu.CompilerParams(dimension_semantics=("arbitrary", "arbitrary")),
        name="moe_routing",
    )(idx)


def _slot_kernel(pad_ref, pend_ref, d0_ref, d1_ref, d2_ref, d3_ref, slot_ref, *, n_tok):
    i = pl.program_id(0)
    dest_refs = (d0_ref, d1_ref, d2_ref, d3_ref)
    tt = d0_ref.shape[0]

    @pl.when(i == 0)
    def _():
        def fill(r, c):
            slot_ref[r] = TOP_K * n_tok + lax.rem(r, EXPERT_ROWS)
            return c

        def per_expert(e, carry):
            return lax.fori_loop(pad_ref[e], pend_ref[e], fill, carry)
        lax.fori_loop(0, N_EXPERTS, per_expert, 0)
        lax.fori_loop(pend_ref[N_EXPERTS - 1], slot_ref.shape[0], fill, 0)

    unroll = 16

    def body(g, carry):
        tl0 = g * unroll
        first = [k * n_tok + i * tt + tl0 for k in range(TOP_K)]
        for u in range(unroll):
            for k in range(TOP_K):
                slot_ref[dest_refs[k][tl0 + u]] = first[k] + u
        return carry
    lax.fori_loop(0, tt // unroll, body, 0)


def _row_slots(dest, pad, pend, cap):
    _, t = dest.shape
    tt = 1024
    n_t = t // tt
    smem = pl.BlockSpec(memory_space=pltpu.SMEM)
    choice = lambda k: pl.BlockSpec((tt,), lambda i: (k * n_t + i,), memory_space=pltpu.SMEM)
    flat = dest.reshape(TOP_K * t)
    return pl.pallas_call(
        functools.partial(_slot_kernel, n_tok=t),
        grid=(n_t,),
        in_specs=[smem, smem] + [choice(k) for k in range(TOP_K)],
        out_specs=smem,
        out_shape=jax.ShapeDtypeStruct((cap,), jnp.int32),
        compiler_params=pltpu.CompilerParams(dimension_semantics=("arbitrary",)),
        name="moe_row_slots",
    )(pad, pend, flat, flat, flat, flat)


def _expert_kernel(blk_e_ref, nused_ref, xb_ref, wgu_ref, bgu_ref, wd_ref, bd_ref, *rest):
    yb_ref, wgu_b, wd_b = rest[-3:]
    j = pl.program_id(0)
    nused = nused_ref[0]
    d_ff = wd_ref.shape[1]

    new_expert = (j == 0) | (blk_e_ref[j] != blk_e_ref[jnp.maximum(j - 1, 0)])

    @pl.when((j < nused) & new_expert)
    def _():
        def cast(src, dst):
            def body(i, carry):
                rows = pl.ds(pl.multiple_of(i * LANES, LANES), LANES)
                dst[rows, :] = src[0, rows, :].astype(BF16)
                return carry
            lax.fori_loop(0, dst.shape[0] // LANES, body, 0)
        cast(wgu_ref, wgu_b)
        cast(wd_ref, wd_b)

    @pl.when(j < nused)
    def _():
        word = xb_ref[...]
        half = word.shape[1]
        lo = lax.bitcast_convert_type(word << 16, F32).astype(BF16)
        hi = lax.bitcast_convert_type(word & jnp.uint32(0xFFFF0000), F32).astype(BF16)
        gu = (jnp.dot(lo, wgu_b[:half, :], preferred_element_type=F32)
              + jnp.dot(hi, wgu_b[half:, :], preferred_element_type=F32) + bgu_ref[0])
        gate = jnp.minimum(gu[:, :d_ff], SWIGLU_LIMIT)
        up = jnp.clip(gu[:, d_ff:], -SWIGLU_LIMIT, SWIGLU_LIMIT)
        hid = (up + 1.0) * (gate * jax.nn.sigmoid(gate * SWIGLU_ALPHA))
        yb_ref[...] = jnp.dot(hid.astype(BF16), wd_b[...], preferred_element_type=F32) + bd_ref[0]

    @pl.when(j >= nused)
    def _():
        yb_ref[...] = jnp.zeros_like(yb_ref)


def _experts(xb, chunk, n_chunk, yb_prev, blk_e, nused, w_gate_up, b_gate_up, w_down, b_down):
    rows, d_in = xb.shape
    e, d, ff2 = w_gate_up.shape
    d_ff = w_down.shape[1]
    n_blk = rows // EXPERT_ROWS
    live = lambda j, nu: jnp.maximum(jnp.minimum(j, nu[0] - 1), 0)
    expert = lambda j, be, nu: (be[live(j, nu)], 0, 0)
    in_specs = [pl.BlockSpec((EXPERT_ROWS, d_in), lambda j, be, nu: (live(j, nu), 0)),
                pl.BlockSpec((1, d, ff2), expert), pl.BlockSpec((1, 1, ff2), expert),
                pl.BlockSpec((1, d_ff, d), expert), pl.BlockSpec((1, 1, d), expert)]
    args = [blk_e, nused, xb, w_gate_up, b_gate_up[:, None, :], w_down, b_down[:, None, :]]
    aliases = {}
    if yb_prev is not None:
        in_specs.append(pl.BlockSpec(memory_space=pl.ANY))
        args.append(yb_prev)
        aliases = {len(args) - 1: 0}
    grid_spec = pltpu.PrefetchScalarGridSpec(
        num_scalar_prefetch=2,
        grid=(n_blk,),
        in_specs=in_specs,
        out_specs=pl.BlockSpec((EXPERT_ROWS, d), lambda j, be, nu: (chunk * n_blk + j, 0)),
        scratch_shapes=[pltpu.VMEM((d, ff2), BF16), pltpu.VMEM((d_ff, d), BF16)],
    )
    return pl.pallas_call(
        _expert_kernel,
        grid_spec=grid_spec,
        out_shape=jax.ShapeDtypeStruct((n_chunk * rows, d), F32),
        input_output_aliases=aliases,
        compiler_params=pltpu.CompilerParams(dimension_semantics=("arbitrary",), vmem_limit_bytes=VMEM_LIMIT),
        name="moe_experts",
    )(*args)


SC_BUFFER_BYTES = 128 * 1024
MOE_CHUNKS = 8


def _sc_gather_rows(table, idx):
    from jax.experimental.pallas import tpu_sc as plsc
    n_rows = idx.shape[0]
    d = table.shape[1]
    info = plsc.get_sparse_core_info()
    n_core, n_sub = info.num_cores, info.num_subcores
    per_w = n_rows // (n_core * n_sub)
    fit = SC_BUFFER_BYTES // (d * table.dtype.itemsize)
    g_rows = max(r for r in (8, 16, 32, 64, 128) if r <= fit and per_w % r == 0)
    n_chunk = per_w // g_rows
    assert per_w * n_core * n_sub == n_rows
    mesh = plsc.VectorSubcoreMesh(core_axis_name="c", subcore_axis_name="s")

    def body(table_hbm, idx_hbm, out_hbm, idx_v, buf0, buf1, sem0, sem1):
        base = (lax.axis_index("s") * n_core + lax.axis_index("c")) * per_w
        pltpu.sync_copy(idx_hbm.at[pl.ds(pl.multiple_of(base, 8), per_w)], idx_v)

        def gather(chunk, buf, sem):
            rows = idx_v.at[pl.ds(pl.multiple_of(chunk * g_rows, 8), g_rows)]
            return pltpu.make_async_copy(table_hbm.at[rows], buf, sem)

        def write(chunk, buf):
            pltpu.sync_copy(buf, out_hbm.at[pl.ds(pl.multiple_of(base + chunk * g_rows, 8), g_rows)])

        gather(0, buf0, sem0).start()

        @pl.loop(0, n_chunk // 2)
        def _(i):
            gather(2 * i + 1, buf1, sem1).start()
            gather(2 * i, buf0, sem0).wait()
            write(2 * i, buf0)

            @pl.when(2 * i + 2 < n_chunk)
            def _():
                gather(2 * i + 2, buf0, sem0).start()
            gather(2 * i + 1, buf1, sem1).wait()
            write(2 * i + 1, buf1)

        if n_chunk % 2:
            gather(n_chunk - 1, buf0, sem0).wait()
            write(n_chunk - 1, buf0)

    return pl.kernel(
        body, mesh=mesh, out_type=jax.ShapeDtypeStruct((n_rows, d), table.dtype),
        scratch_types=[pltpu.VMEM((per_w,), jnp.int32), pltpu.VMEM((g_rows, d), table.dtype),
                       pltpu.VMEM((g_rows, d), table.dtype), pltpu.SemaphoreType.DMA, pltpu.SemaphoreType.DMA],
        name="moe_sc_gather",
    )(table, idx)


def _combine_kernel(gate_ref, h_ref, p_ref, y0_ref, y1_ref, y2_ref, y3_ref, gffn_ref, wple_ref, wpg_ref, gple_ref,
                    o_ref):
    gt = gate_ref[...].T
    y = gt[:, 0:1] * y0_ref[...]
    for k, y_ref in enumerate((y1_ref, y2_ref, y3_ref), start=1):
        y = y + gt[:, k:k + 1] * y_ref[...]
    h2 = h_ref[...] + _rms(y, gffn_ref[...])
    ple = (jnp.dot(p_ref[...].astype(BF16), wple_ref[...], preferred_element_type=F32)
           * jax.nn.sigmoid(jnp.dot(h2.astype(BF16), wpg_ref[...], preferred_element_type=F32)))
    o_ref[...] = h2 + _rms(ple, gple_ref[...])


def _combine_kernel_aliased(gate_ref, h_ref, p_ref, y0_ref, y1_ref, y2_ref, y3_ref, gffn_ref, wple_ref, wpg_ref,
                            gple_ref, prev_ref, o_ref):
    del prev_ref
    _combine_kernel(gate_ref, h_ref, p_ref, y0_ref, y1_ref, y2_ref, y3_ref, gffn_ref, wple_ref, wpg_ref, gple_ref,
                    o_ref)


def _combine(group, n_group, out_prev, gates, h, p, yk, g_ffn_post, w_ple_bf16, w_ple_gate_bf16, g_ple_post):
    t, d = h.shape
    tt = 256
    n_tg = t // n_group // tt
    const = lambda shape: pl.BlockSpec(shape, lambda i: (0, 0))
    choice = lambda k: pl.BlockSpec((tt, d), lambda i: (k * n_tg + i, 0))
    tok = lambda w: pl.BlockSpec((tt, w), lambda i: (group * n_tg + i, 0))
    in_specs = [pl.BlockSpec((8, tt), lambda i: (0, group * n_tg + i)), tok(d), tok(p.shape[1]),
                choice(0), choice(1), choice(2), choice(3),
                const((1, d)), const(w_ple_bf16.shape), const(w_ple_gate_bf16.shape), const((1, d))]
    args = [gates, h, p, yk, yk, yk, yk, g_ffn_post[None, :], w_ple_bf16, w_ple_gate_bf16, g_ple_post[None, :]]
    body, aliases = _combine_kernel, {}
    if out_prev is not None:
        in_specs.append(pl.BlockSpec(memory_space=pl.ANY))
        args.append(out_prev)
        body, aliases = _combine_kernel_aliased, {len(args) - 1: 0}
    return pl.pallas_call(
        body,
        grid=(n_tg,),
        in_specs=in_specs,
        out_specs=tok(d),
        out_shape=jax.ShapeDtypeStruct((t, d), F32),
        input_output_aliases=aliases,
        compiler_params=pltpu.CompilerParams(vmem_limit_bytes=VMEM_LIMIT),
        name="moe_combine_ple",
    )(*args)


def _layer(h, p, positions, g_mix_pre, w_in, w_dw, b_dw, g_conv_ln, b_conv_ln, w_out, g_mix_post, g_ffn_pre,
           w_router, b_router, w_gate_up, b_gate_up, w_down, b_down, g_ffn_post, w_ple, w_ple_gate, g_ple_post):
    b, s, d = h.shape
    t = b * s
    cos_t, sin_t = _rope_tables(positions)
    q, k, v, glu = _input_projection(h, cos_t, sin_t, g_mix_pre, w_in)
    attn = _dilated_attention(q, k, v)
    conv = _conformer_conv(glu, w_dw, b_dw, g_conv_ln, b_conv_ln)
    h1, u_ffn, idx, gates = _output_projection(attn, conv, h, w_out, g_mix_post, g_ffn_pre, w_router, b_router)
    cap = t * TOP_K + N_EXPERTS * EXPERT_ROWS
    n_blk = cap // EXPERT_ROWS
    dest, pend, pad, blk = _routing(idx, n_blk)
    blk_e = blk[0, :n_blk]
    nused = pend[N_EXPERTS - 1] // EXPERT_ROWS
    slot = _row_slots(dest, pad[:, 0], pend[:, 0], cap)
    tok_of_row = slot % t
    rows_c = cap // MOE_CHUNKS
    blk_c = n_blk // MOE_CHUNKS
    yb = None
    for c in range(MOE_CHUNKS):
        xb_c = _sc_gather_rows(u_ffn, tok_of_row[c * rows_c:(c + 1) * rows_c])
        nused_c = jnp.clip(nused - c * blk_c, 0, blk_c)
        yb = _experts(xb_c, c, MOE_CHUNKS, yb, blk_e[c * blk_c:(c + 1) * blk_c], nused_c,
                      w_gate_up, b_gate_up, w_down, b_down)
    t_g = t // MOE_CHUNKS
    w_ple_bf16, w_ple_gate_bf16 = w_ple.astype(BF16), w_ple_gate.astype(BF16)
    h1_flat, p_flat = h1.reshape(t, d), p.reshape(t, -1)
    out = None
    for g in range(MOE_CHUNKS):
        yk_g = _sc_gather_rows(yb, dest[:, g * t_g:(g + 1) * t_g].reshape(TOP_K * t_g))
        out = _combine(g, MOE_CHUNKS, out, gates, h1_flat, p_flat, yk_g, g_ffn_post, w_ple_bf16, w_ple_gate_bf16,
                       g_ple_post)
    return out.reshape(b, s, d)


def kernel(x, p, positions, g_mix_pre, w_in, w_dw, b_dw, g_conv_ln, b_conv_ln, w_out, g_mix_post, g_ffn_pre,
           w_router, b_router, w_gate_up, b_gate_up, w_down, b_down, g_ffn_post, w_ple, w_ple_gate, g_ple_post):
    h = x
    for i in range(p.shape[0]):
        h = _layer(h, p[i], positions, g_mix_pre[i], w_in[i], w_dw[i], b_dw[i], g_conv_ln[i], b_conv_ln[i],
                   w_out[i], g_mix_post[i], g_ffn_pre[i], w_router[i], b_router[i], w_gate_up[i], b_gate_up[i],
                   w_down[i], b_down[i], g_ffn_post[i], w_ple[i], w_ple_gate[i], g_ple_post[i])
    return h
```

```python
import functools

import numpy as np
import jax
import jax.numpy as jnp
from jax import lax
from jax.experimental import pallas as pl
from jax.experimental.pallas import tpu as pltpu

F32 = jnp.float32
BF16 = jnp.bfloat16

HEAD_DIM = 64
N_HEADS = 12
ATTN_WIDTH = N_HEADS * HEAD_DIM
CONV_CHANNELS = 256
CONV_WIDTH = 31
ROPE_DIM = HEAD_DIM // 4
ROPE_THETA = 500000.0
N_EXPERTS = 32
TOP_K = 4
SWIGLU_LIMIT = 7.0
SWIGLU_ALPHA = 1.702
NORM_EPS = 1e-6
WINDOW = 128
N_PLANES = 16
SPAN = N_PLANES * WINDOW
LANES = 128
SUBLANES = 8
NEG = -1e30
EXPERT_ROWS = 256
VMEM_LIMIT = 56 * 1024 * 1024


def _residue_of_plane(p):
    return 4 * (p % 4) + p // 4


def _rms(xv, g):
    var = jnp.mean(xv * xv, axis=-1, keepdims=True)
    return xv * lax.rsqrt(var + NORM_EPS) * g


def _rope_kernel(pos_ref, invf_ref, expand_ref, one_ref, sgn_ref, c_ref, s_ref):
    tn = (((0,), (0,)), ((), ()))
    for p in range(N_PLANES):
        ang = invf_ref[...] * pos_ref[0, p:p + 1, :].astype(F32)
        spread = lambda t: lax.dot_general(t, expand_ref[...], tn, precision=lax.Precision.HIGHEST,
                                           preferred_element_type=F32)
        c_ref[0, p] = spread(jnp.cos(ang)) + one_ref[...]
        s_ref[0, p] = spread(jnp.sin(ang)) * sgn_ref[...]


def _rope_tables(positions):
    b, s = positions.shape
    sm = s // N_PLANES
    mt = min(sm, 128)
    plane_res = np.array([_residue_of_plane(p) for p in range(N_PLANES)])
    pos_planes = positions.reshape(b, sm, N_PLANES).transpose(0, 2, 1)[:, plane_res]
    half = ROPE_DIM // 2
    lane = np.arange(LANES) % HEAD_DIM
    inv_freq = ROPE_THETA ** (-jnp.arange(0, ROPE_DIM, 2, dtype=F32) / ROPE_DIM)
    rotary = lane < ROPE_DIM
    expand = jnp.asarray((np.arange(half)[:, None] == lane[None, :] % half) & rotary[None, :], F32)
    one = jnp.asarray(~rotary, F32)[None, :]
    sgn = jnp.asarray(np.where(lane < half, -1.0, 1.0), F32)[None, :]
    row = pl.BlockSpec((1, LANES), lambda i, j: (0, 0))
    out = pl.BlockSpec((1, N_PLANES, mt, LANES), lambda i, j: (i, 0, j, 0))
    return pl.pallas_call(
        _rope_kernel,
        grid=(b, sm // mt),
        in_specs=[pl.BlockSpec((1, N_PLANES, mt), lambda i, j: (i, 0, j)),
                  pl.BlockSpec((half, 1), lambda i, j: (0, 0)), pl.BlockSpec((half, LANES), lambda i, j: (0, 0)),
                  row, row],
        out_specs=[out, out],
        out_shape=[jax.ShapeDtypeStruct((b, N_PLANES, sm, LANES), F32)] * 2,
        name="rope_tables",
    )(pos_planes, inv_freq[:, None], expand, one, sgn)


PERM_TOKENS = 256
PERM_ROWS = PERM_TOKENS // N_PLANES


def _plane_permutation():
    perm = np.zeros((PERM_TOKENS, PERM_TOKENS), np.float32)
    for p in range(N_PLANES):
        for ml in range(PERM_ROWS):
            perm[PERM_ROWS * p + ml, N_PLANES * ml + _residue_of_plane(p)] = 1.0
    return perm


def _inproj_kernel(x_ref, c_ref, s_ref, g_ref, perm_ref, wqkv_ref, wc_ref, q_ref, k_ref, v_ref, glu_ref):
    g = g_ref[...]
    tt = x_ref.shape[1]
    lane = lax.broadcasted_iota(jnp.int32, (1, LANES), 1) % HEAD_DIM
    first_half = lane < ROPE_DIM // 2

    def rotary(t, cos, sin):
        outs = []
        for j in range(ATTN_WIDTH // LANES):
            tj = t[:, j * LANES:(j + 1) * LANES]
            partner = jnp.where(first_half, pltpu.roll(tj, LANES - ROPE_DIM // 2, 1),
                                pltpu.roll(tj, ROPE_DIM // 2, 1))
            outs.append(tj * cos + partner * sin)
        return jnp.concatenate(outs, axis=1)

    un = _rms(x_ref[0], g).astype(BF16)
    pc = jnp.dot(un, wc_ref[...], preferred_element_type=F32)
    glu_ref[0] = pc[:, :CONV_CHANNELS] * jax.nn.sigmoid(pc[:, CONV_CHANNELS:])

    for sub in range(tt // PERM_TOKENS):
        rows = slice(sub * PERM_ROWS, (sub + 1) * PERM_ROWS)
        u = jnp.dot(perm_ref[...], un[sub * PERM_TOKENS:(sub + 1) * PERM_TOKENS],
                    preferred_element_type=F32).astype(BF16)
        cos = jnp.concatenate([c_ref[0, p, rows, :] for p in range(N_PLANES)], axis=0)
        sin = jnp.concatenate([s_ref[0, p, rows, :] for p in range(N_PLANES)], axis=0)
        proj = jnp.dot(u, wqkv_ref[...], preferred_element_type=F32)
        q = (rotary(proj[:, :ATTN_WIDTH], cos, sin) * (HEAD_DIM ** -0.5)).astype(BF16)
        k = rotary(proj[:, ATTN_WIDTH:2 * ATTN_WIDTH], cos, sin).astype(BF16)
        v = proj[:, 2 * ATTN_WIDTH:].astype(BF16)
        for p in range(N_PLANES):
            chunk = slice(p * PERM_ROWS, (p + 1) * PERM_ROWS)
            q_ref[0, p, rows, :] = q[chunk]
            k_ref[0, p, rows, :] = k[chunk]
            v_ref[0, p, rows, :] = v[chunk]


def _input_projection(x, cos_t, sin_t, g_mix_pre, w_in):
    b, s, d = x.shape
    tt = 1024
    mc = tt // N_PLANES
    wqkv = w_in[:, :3 * ATTN_WIDTH].astype(BF16)
    wc = w_in[:, 3 * ATTN_WIDTH:].astype(BF16)
    perm = jnp.asarray(_plane_permutation(), BF16)
    plane = lambda w: pl.BlockSpec((1, N_PLANES, mc, w), lambda i, j: (i, 0, j, 0))
    plane_shape = jax.ShapeDtypeStruct((b, N_PLANES, s // N_PLANES, ATTN_WIDTH), BF16)
    tok = lambda w: pl.BlockSpec((1, tt, w), lambda i, j: (i, j, 0))
    const = lambda shape: pl.BlockSpec(shape, lambda i, j: (0, 0))
    return pl.pallas_call(
        _inproj_kernel,
        grid=(b, s // tt),
        in_specs=[tok(d), plane(LANES), plane(LANES), const((1, d)), const(perm.shape),
                  const(wqkv.shape), const(wc.shape)],
        out_specs=[plane(ATTN_WIDTH), plane(ATTN_WIDTH), plane(ATTN_WIDTH), tok(CONV_CHANNELS)],
        out_shape=[plane_shape, plane_shape, plane_shape,
                   jax.ShapeDtypeStruct((b, s, CONV_CHANNELS), F32)],
        compiler_params=pltpu.CompilerParams(vmem_limit_bytes=VMEM_LIMIT),
        name="input_projection",
    )(x, cos_t, sin_t, g_mix_pre[None, :], perm, wqkv, wc)


def _attention_biases():
    band = lambda j: np.where((j >= 0) & (j <= WINDOW), 0.0, NEG).astype(np.float32)
    cols = lambda m: np.where(m, NEG, 0.0).astype(np.float32)
    twice = lambda a: np.concatenate([a, a], axis=0)
    mq = np.arange(WINDOW)[:, None]
    kj = np.arange(2 * WINDOW)[None, :]
    j16 = mq + WINDOW - kj
    prev16 = kj < WINDOW
    row = np.arange(128)[:, None]
    col = np.arange(256)[None, :]
    j4 = 4 * (row % 32 - (col % 64 - 32)) + row // 32 - col // 64
    prev4 = col % 64 < 32
    row = np.arange(256)[:, None]
    col = np.arange(512)[None, :]
    res = np.vectorize(_residue_of_plane)
    j1 = 16 * (row % 16 - (col % 32 - 16)) + res(row // 16) - res(col // 32)
    prev1 = col % 32 < 16
    b1 = np.stack([twice(band(j1[:128])), twice(band(j1[128:]))])
    return [jnp.asarray(a) for a in (twice(band(j16)), twice(band(j4)), b1, cols(prev16), cols(prev4), cols(prev1))]


UNROLL = 8


def _attn_kernel(q_ref, kc_ref, kp_ref, vc_ref, vp_ref, b16_ref, b4_ref, b1_ref, p16_ref, p4_ref, p1_ref,
                 o_ref, m_scr, l_scr, a_scr, bias16_scr):
    no_prev = (pl.program_id(1) == 0).astype(F32)
    head0 = lax.broadcasted_iota(jnp.int32, (1, LANES), 1) < HEAD_DIM
    bias16_scr[...] = b16_ref[...] + no_prev * p16_ref[...]

    def tile(qt, kt, vt, bias):
        n = qt.shape[0]
        zero = jnp.zeros_like(qt)
        q2 = jnp.concatenate([jnp.where(head0, qt, zero), jnp.where(head0, zero, qt)], axis=0)
        s = lax.dot_general(q2, kt, (((1,), (1,)), ((), ())), preferred_element_type=F32) + bias
        m = jnp.max(s, axis=-1, keepdims=True)
        e = jnp.exp(s - m).astype(BF16)
        va = jnp.concatenate([vt, jnp.ones_like(vt)], axis=1)
        o = jnp.dot(e, va, preferred_element_type=F32)
        pick = lambda top, bot: jnp.where(head0, top, bot)
        mm = pick(jnp.broadcast_to(m[:n], (n, LANES)), jnp.broadcast_to(m[n:], (n, LANES)))
        return mm, pick(o[:n, LANES:], o[n:, LANES:]), pick(o[:n, :LANES], o[n:, :LANES])

    def put(branch, start, size, stats, off):
        for scr, val in zip((m_scr, l_scr, a_scr), stats):
            scr[branch, pl.ds(start, size), :] = val[off:off + size]

    def body16(i, carry):
        for p in [UNROLL * i + u for u in range(UNROLL)]:
            kt = jnp.concatenate([kp_ref[0, p], kc_ref[0, p]], axis=0)
            vt = jnp.concatenate([vp_ref[0, p], vc_ref[0, p]], axis=0)
            put(0, pl.multiple_of(p * WINDOW, WINDOW), WINDOW, tile(q_ref[0, p], kt, vt, bias16_scr[...]), 0)
        return carry
    lax.fori_loop(0, N_PLANES // UNROLL, body16, 0)

    def body4(g, carry):
        for c, i in [(2 * g + cc, ii) for cc in range(2) for ii in range(4)]:
            qt = jnp.concatenate([q_ref[0, 4 * c + a, 32 * i:32 * i + 32, :] for a in range(4)], axis=0)
            if i == 0:
                ks = [x for a in range(4) for x in (kp_ref[0, 4 * c + a, 96:128, :], kc_ref[0, 4 * c + a, 0:32, :])]
                vs = [x for a in range(4) for x in (vp_ref[0, 4 * c + a, 96:128, :], vc_ref[0, 4 * c + a, 0:32, :])]
                bias = b4_ref[...] + no_prev * p4_ref[...]
            else:
                ks = [kc_ref[0, 4 * c + a, 32 * i - 32:32 * i + 32, :] for a in range(4)]
                vs = [vc_ref[0, 4 * c + a, 32 * i - 32:32 * i + 32, :] for a in range(4)]
                bias = b4_ref[...]
            stats = tile(qt, jnp.concatenate(ks, axis=0), jnp.concatenate(vs, axis=0), bias)
            for a in range(4):
                put(1, pl.multiple_of((4 * c + a) * WINDOW + 32 * i, 32), 32, stats, 32 * a)
        return carry
    lax.fori_loop(0, 2, body4, 0)

    def tile1(i, first):
        rq = pl.ds(pl.multiple_of(16 * i, 16), 16)
        if first:
            ks = [x for p in range(N_PLANES) for x in (kp_ref[0, p, 112:128, :], kc_ref[0, p, 0:16, :])]
            vs = [x for p in range(N_PLANES) for x in (vp_ref[0, p, 112:128, :], vc_ref[0, p, 0:16, :])]
        else:
            rk = pl.ds(pl.multiple_of(16 * i - 16, 16), 32)
            ks = [kc_ref[0, p, rk, :] for p in range(N_PLANES)]
            vs = [vc_ref[0, p, rk, :] for p in range(N_PLANES)]
        kt = jnp.concatenate(ks, axis=0)
        vt = jnp.concatenate(vs, axis=0)
        for half in range(2):
            planes = range(8 * half, 8 * half + 8)
            qt = jnp.concatenate([q_ref[0, p, rq, :] for p in planes], axis=0)
            bias = b1_ref[half] + no_prev * p1_ref[...] if first else b1_ref[half]
            stats = tile(qt, kt, vt, bias)
            for p in planes:
                put(2, pl.multiple_of(p * WINDOW + 16 * i, 16), 16, stats, 16 * (p - 8 * half))

    tile1(0, True)
    tile1(1, False)

    def body1(g, carry):
        tile1(2 * g, False)
        tile1(2 * g + 1, False)
        return carry
    lax.fori_loop(1, WINDOW // 32, body1, 0)

    def combine(i, carry):
        for p in (2 * i, 2 * i + 1):
            rows = pl.ds(pl.multiple_of(p * WINDOW, WINDOW), WINDOW)
            ms = [m_scr[b, rows, :] for b in range(3)]
            mx = jnp.maximum(jnp.maximum(ms[0], ms[1]), ms[2])
            ws = [jnp.exp(m - mx) for m in ms]
            den = ws[0] * l_scr[0, rows, :] + ws[1] * l_scr[1, rows, :] + ws[2] * l_scr[2, rows, :]
            num = ws[0] * a_scr[0, rows, :] + ws[1] * a_scr[1, rows, :] + ws[2] * a_scr[2, rows, :]
            o_ref[0, p] = (num / den).astype(BF16)
        return carry
    lax.fori_loop(0, N_PLANES // 2, combine, 0)


def _dilated_attention(q, k, v):
    b, _, sm, _ = q.shape
    n_span = sm // WINDOW
    cur = pl.BlockSpec((1, N_PLANES, WINDOW, LANES), lambda i, j, h: (i, 0, j, h))
    prev = pl.BlockSpec((1, N_PLANES, WINDOW, LANES), lambda i, j, h: (i, 0, jnp.maximum(j - 1, 0), h))
    biases = _attention_biases()
    bias_specs = [pl.BlockSpec(a.shape, lambda i, j, h, nd=a.ndim: (0,) * nd) for a in biases]
    stats = pltpu.VMEM((3, SPAN, LANES), F32)
    return pl.pallas_call(
        _attn_kernel,
        grid=(b, n_span, ATTN_WIDTH // LANES),
        in_specs=[cur, cur, prev, cur, prev] + bias_specs,
        out_specs=cur,
        out_shape=jax.ShapeDtypeStruct(q.shape, BF16),
        scratch_shapes=[stats, stats, stats, pltpu.VMEM((2 * WINDOW, 2 * WINDOW), F32)],
        compiler_params=pltpu.CompilerParams(vmem_limit_bytes=VMEM_LIMIT),
        name="dilated_attention",
    )(q, k, k, v, v, *biases)


CONV_HALO = 32


def _conv_kernel(cur_ref, prev_ref, w_ref, b_ref, g_ref, bl_ref, o_ref, scr, *, chunk):
    tt = cur_ref.shape[1]
    has_prev = pl.program_id(1) > 0
    scr[0, 0:CONV_HALO, :] = jnp.where(has_prev, prev_ref[0], 0.0)
    scr[0, CONV_HALO:CONV_HALO + tt, :] = cur_ref[0]
    aligned_rows = tt + CONV_HALO - SUBLANES
    for s in range(1, SUBLANES):
        scr[s, 0:aligned_rows, :] = scr[0, s:s + aligned_rows, :]
    lead = CONV_HALO - (CONV_WIDTH - 1)
    for c0 in range(0, tt, chunk):
        acc = jnp.zeros((chunk, CONV_CHANNELS), F32)
        for j in range(CONV_WIDTH):
            s, a = (lead + j) % SUBLANES, (lead + j) // SUBLANES * SUBLANES
            acc = acc + w_ref[j:j + 1, :] * scr[s, c0 + a:c0 + a + chunk, :]
        y = acc + b_ref[...]
        mu = jnp.mean(y, axis=-1, keepdims=True)
        var = jnp.mean(jnp.square(y - mu), axis=-1, keepdims=True)
        yn = (y - mu) * lax.rsqrt(var + NORM_EPS) * g_ref[...] + bl_ref[...]
        o_ref[0, c0:c0 + chunk, :] = (yn * jax.nn.sigmoid(yn)).astype(BF16)


def _conformer_conv(glu, w_dw, b_dw, g_ln, b_ln):
    b, s, c = glu.shape
    tt = 512
    row = pl.BlockSpec((1, c), lambda i, j: (0, 0))
    return pl.pallas_call(
        functools.partial(_conv_kernel, chunk=128),
        grid=(b, s // tt),
        in_specs=[pl.BlockSpec((1, tt, c), lambda i, j: (i, j, 0)),
                  pl.BlockSpec((1, CONV_HALO, c), lambda i, j: (i, jnp.maximum(j * (tt // CONV_HALO) - 1, 0), 0)),
                  pl.BlockSpec((CONV_WIDTH, c), lambda i, j: (0, 0)), row, row, row],
        out_specs=pl.BlockSpec((1, tt, c), lambda i, j: (i, j, 0)),
        out_shape=jax.ShapeDtypeStruct((b, s, c), BF16),
        scratch_shapes=[pltpu.VMEM((SUBLANES, CONV_HALO + tt, c), F32)],
        name="conformer_conv",
    )(glu, glu, w_dw[:, 0, :], b_dw[None, :], g_ln[None, :], b_ln[None, :])


def _outproj_kernel(attn_ref, conv_ref, x_ref, permt_ref, woa_ref, woc_ref, gpost_ref, gffn_ref, wr_ref, br_ref,
                    h_ref, u_ref, idx_ref, gate_ref):
    tt = x_ref.shape[1]
    nat = []
    for sub in range(tt // PERM_TOKENS):
        rows = slice(sub * PERM_ROWS, (sub + 1) * PERM_ROWS)
        a = jnp.concatenate([attn_ref[0, p, rows, :] for p in range(N_PLANES)], axis=0)
        nat.append(jnp.dot(permt_ref[...], a, preferred_element_type=F32).astype(BF16))
    mix = (jnp.dot(jnp.concatenate(nat, axis=0), woa_ref[...], preferred_element_type=F32)
           + jnp.dot(conv_ref[0], woc_ref[...], preferred_element_type=F32))
    h = x_ref[0] + _rms(mix, gpost_ref[...])
    h_ref[0] = h
    u = _rms(h, gffn_ref[...])
    half = u.shape[1] // 2
    ub = u.astype(BF16).astype(F32)
    u_ref[...] = ((lax.bitcast_convert_type(ub[:, :half], jnp.uint32) >> 16)
                  | (lax.bitcast_convert_type(ub[:, half:], jnp.uint32) & jnp.uint32(0xFFFF0000)))
    u_hi = u.astype(BF16)
    u_lo = (u - u_hi.astype(F32)).astype(BF16)
    nt = (((1,), (1,)), ((), ()))
    by_hi = lax.dot_general(wr_ref[...], u_hi, nt, preferred_element_type=F32)
    by_lo = lax.dot_general(wr_ref[:N_EXPERTS], u_lo, nt, preferred_element_type=F32)
    logits = by_hi[:N_EXPERTS] + (by_hi[N_EXPERTS:] + by_lo) + br_ref[...]
    rows = lax.broadcasted_iota(jnp.int32, logits.shape, 0)
    vals = logits
    tops, idxs = [], []
    for _ in range(TOP_K):
        mx = jnp.max(vals, axis=0, keepdims=True)
        ix = jnp.min(jnp.where(vals == mx, rows, N_EXPERTS), axis=0, keepdims=True)
        tops.append(mx)
        idxs.append(ix)
        vals = jnp.where(rows == ix, -jnp.inf, vals)
    ex = [jnp.exp(t - tops[0]) for t in tops]
    den = ex[0] + ex[1] + ex[2] + ex[3]
    idx_ref[...] = jnp.concatenate(idxs, axis=0)
    gate_ref[...] = jnp.concatenate([e / den for e in ex] + [jnp.zeros((8 - TOP_K, tt), F32)], axis=0)


def _output_projection(attn, conv, x, w_out, g_mix_post, g_ffn_pre, w_router, b_router):
    b, s, d = x.shape
    tt = 512
    mc = tt // N_PLANES
    n_t = s // tt
    woa = w_out[:ATTN_WIDTH].astype(BF16)
    woc = w_out[ATTN_WIDTH:].astype(BF16)
    wr_hi = w_router.T.astype(BF16)
    wr_split = jnp.concatenate([wr_hi, (w_router.T - wr_hi.astype(F32)).astype(BF16)], axis=0)
    permt =jnp.asarray(_plane_permutation().T, BF16)
    const = lambda shape: pl.BlockSpec(shape, lambda i, j: (0, 0))
    flat = lambda w: pl.BlockSpec((tt, w), lambda i, j: (i * n_t + j, 0))
    lanes = lambda r: pl.BlockSpec((r, tt), lambda i, j: (0, i * n_t + j))
    return pl.pallas_call(
        _outproj_kernel,
        grid=(b, n_t),
        in_specs=[pl.BlockSpec((1, N_PLANES, mc, ATTN_WIDTH), lambda i, j: (i, 0, j, 0)),
                  pl.BlockSpec((1, tt, CONV_CHANNELS), lambda i, j: (i, j, 0)),
                  pl.BlockSpec((1, tt, d), lambda i, j: (i, j, 0)),
                  const(permt.shape), const(woa.shape), const(woc.shape), const((1, d)), const((1, d)),
                  const((2 * N_EXPERTS, d)), const((N_EXPERTS, 1))],
        out_specs=[pl.BlockSpec((1, tt, d), lambda i, j: (i, j, 0)), flat(d // 2), lanes(TOP_K), lanes(8)],
        out_shape=[jax.ShapeDtypeStruct((b, s, d), F32),
                   jax.ShapeDtypeStruct((b * s, d // 2), jnp.uint32),
                   jax.ShapeDtypeStruct((TOP_K, b * s), jnp.int32),
                   jax.ShapeDtypeStruct((8, b * s), F32)],
        compiler_params=pltpu.CompilerParams(vmem_limit_bytes=VMEM_LIMIT),
        name="output_projection_router",
    )(attn, conv, x, permt, woa, woc, g_mix_post[None, :], g_ffn_pre[None, :], wr_split, b_router[:, None])


def _route_kernel(idx_ref, dest_ref, pend_ref, pad_ref, blk_ref, carry, pstart):
    phase = pl.program_id(0)
    step = pl.program_id(1)
    tt = idx_ref.shape[1]
    rows = lax.broadcasted_iota(jnp.int32, (N_EXPERTS, tt), 0)
    hot = [rows == idx_ref[k:k + 1, :] for k in range(TOP_K)]
    memb = sum(h.astype(F32) for h in hot)

    @pl.when((phase == 0) & (step == 0))
    def _():
        carry[...] = jnp.zeros_like(carry)

    @pl.when((phase == 1) & (step == 0))
    def _():
        counts = carry[...]
        padded = jnp.floor((counts + (EXPERT_ROWS - 1)) * (1.0 / EXPERT_ROWS)) * EXPERT_ROWS
        tri = (lax.broadcasted_iota(jnp.int32, (N_EXPERTS, N_EXPERTS), 1)
               <= lax.broadcasted_iota(jnp.int32, (N_EXPERTS, N_EXPERTS), 0)).astype(F32)
        pend = jnp.dot(tri, padded, precision=lax.Precision.HIGHEST, preferred_element_type=F32)
        pstart[...] = pend - padded
        pend_ref[...] = pend.astype(jnp.int32)
        pad_ref[...] = (pend - padded + counts).astype(jnp.int32)
        starts = lax.broadcasted_iota(jnp.int32, (N_EXPERTS, blk_ref.shape[1]), 1) * EXPERT_ROWS
        ended = (pend.astype(jnp.int32) <= starts).astype(jnp.int32)
        blk_ref[...] = jnp.minimum(jnp.sum(ended, axis=0, keepdims=True), N_EXPERTS - 1)
        carry[...] = jnp.zeros_like(carry)

    @pl.when(phase == 1)
    def _():
        earlier = (lax.broadcasted_iota(jnp.int32, (tt, tt), 0)
                   < lax.broadcasted_iota(jnp.int32, (tt, tt), 1)).astype(BF16)
        row = jnp.dot(memb.astype(BF16), earlier, preferred_element_type=F32) + (carry[...] + pstart[...])
        dest_ref[...] = jnp.concatenate(
            [jnp.sum(jnp.where(h, row, 0.0), axis=0, keepdims=True) for h in hot], axis=0).astype(jnp.int32)

    carry[...] = carry[...] + jnp.sum(memb, axis=1, keepdims=True)


def _routing(idx, n_blk):
    _, t = idx.shape
    tt = 512
    blk_lanes = -(-n_blk // LANES) * LANES
    return pl.pallas_call(
        _route_kernel,
        grid=(2, t // tt),
        in_specs=[pl.BlockSpec((TOP_K, tt), lambda ph, i: (0, i))],
        out_specs=[pl.BlockSpec((TOP_K, tt), lambda ph, i: (0, i * ph)),
                   pl.BlockSpec((N_EXPERTS, 1), lambda ph, i: (0, 0)),
                   pl.BlockSpec((N_EXPERTS, 1), lambda ph, i: (0, 0)),
                   pl.BlockSpec((1, blk_lanes), lambda ph, i: (0, 0))],
        out_shape=[jax.ShapeDtypeStruct((TOP_K, t), jnp.int32), jax.ShapeDtypeStruct((N_EXPERTS, 1), jnp.int32),
                   jax.ShapeDtypeStruct((N_EXPERTS, 1), jnp.int32), jax.ShapeDtypeStruct((1, blk_lanes), jnp.int32)],
        scratch_shapes=[pltpu.VMEM((N_EXPERTS, 1), F32), pltpu.VMEM((N_EXPERTS, 1), F32)],
        compiler_params=pltpu.CompilerParams(dimension_semantics=("arbitrary", "arbitrary")),
        name="moe_routing",
    )(idx)


def _slot_kernel(pad_ref, pend_ref, d0_ref, d1_ref, d2_ref, d3_ref, slot_ref, *, n_tok):
    i = pl.program_id(0)
    dest_refs = (d0_ref, d1_ref, d2_ref, d3_ref)
    tt = d0_ref.shape[0]

    @pl.when(i == 0)
    def _():
        def fill(r, c):
            slot_ref[r] = TOP_K * n_tok + lax.rem(r, EXPERT_ROWS)
            return c

        def per_expert(e, carry):
            return lax.fori_loop(pad_ref[e], pend_ref[e], fill, carry)
        lax.fori_loop(0, N_EXPERTS, per_expert, 0)
        lax.fori_loop(pend_ref[N_EXPERTS - 1], slot_ref.shape[0], fill, 0)

    unroll = 16

    def body(g, carry):
        tl0 = g * unroll
        first = [k * n_tok + i * tt + tl0 for k in range(TOP_K)]
        for u in range(unroll):
            for k in range(TOP_K):
                slot_ref[dest_refs[k][tl0 + u]] = first[k] + u
        return carry
    lax.fori_loop(0, tt // unroll, body, 0)


def _row_slots(dest, pad, pend, cap):
    _, t = dest.shape
    tt = 1024
    n_t = t // tt
    smem = pl.BlockSpec(memory_space=pltpu.SMEM)
    choice = lambda k: pl.BlockSpec((tt,), lambda i: (k * n_t + i,), memory_space=pltpu.SMEM)
    flat = dest.reshape(TOP_K * t)
    return pl.pallas_call(
        functools.partial(_slot_kernel, n_tok=t),
        grid=(n_t,),
        in_specs=[smem, smem] + [choice(k) for k in range(TOP_K)],
        out_specs=smem,
        out_shape=jax.ShapeDtypeStruct((cap,), jnp.int32),
        compiler_params=pltpu.CompilerParams(dimension_semantics=("arbitrary",)),
        name="moe_row_slots",
    )(pad, pend, flat, flat, flat, flat)


def _expert_kernel(blk_e_ref, nused_ref, xb_ref, wgu_ref, bgu_ref, wd_ref, bd_ref, *rest):
    yb_ref, wgu_b, wd_b = rest[-3:]
    j = pl.program_id(0)
    nused = nused_ref[0]
    d_ff = wd_ref.shape[1]

    new_expert = (j == 0) | (blk_e_ref[j] != blk_e_ref[jnp.maximum(j - 1, 0)])

    @pl.when((j < nused) & new_expert)
    def _():
        def cast(src, dst):
            def body(i, carry):
                rows = pl.ds(pl.multiple_of(i * LANES, LANES), LANES)
                dst[rows, :] = src[0, rows, :].astype(BF16)
                return carry
            lax.fori_loop(0, dst.shape[0] // LANES, body, 0)
        cast(wgu_ref, wgu_b)
        cast(wd_ref, wd_b)

    @pl.when(j < nused)
    def _():
        word = xb_ref[...]
        half = word.shape[1]
        lo = lax.bitcast_convert_type(word << 16, F32).astype(BF16)
        hi = lax.bitcast_convert_type(word & jnp.uint32(0xFFFF0000), F32).astype(BF16)
        gu = (jnp.dot(lo, wgu_b[:half, :], preferred_element_type=F32)
              + jnp.dot(hi, wgu_b[half:, :], preferred_element_type=F32) + bgu_ref[0])
        gate = jnp.minimum(gu[:, :d_ff], SWIGLU_LIMIT)
        up = jnp.clip(gu[:, d_ff:], -SWIGLU_LIMIT, SWIGLU_LIMIT)
        hid = (up + 1.0) * (gate * jax.nn.sigmoid(gate * SWIGLU_ALPHA))
        yb_ref[...] = jnp.dot(hid.astype(BF16), wd_b[...], preferred_element_type=F32) + bd_ref[0]

    @pl.when(j >= nused)
    def _():
        yb_ref[...] = jnp.zeros_like(yb_ref)


def _experts(xb, chunk, n_chunk, yb_prev, blk_e, nused, w_gate_up, b_gate_up, w_down, b_down):
    rows, d_in = xb.shape
    e, d, ff2 = w_gate_up.shape
    d_ff = w_down.shape[1]
    n_blk = rows // EXPERT_ROWS
    live = lambda j, nu: jnp.maximum(jnp.minimum(j, nu[0] - 1), 0)
    expert = lambda j, be, nu: (be[live(j, nu)], 0, 0)
    in_specs = [pl.BlockSpec((EXPERT_ROWS, d_in), lambda j, be, nu: (live(j, nu), 0)),
                pl.BlockSpec((1, d, ff2), expert), pl.BlockSpec((1, 1, ff2), expert),
                pl.BlockSpec((1, d_ff, d), expert), pl.BlockSpec((1, 1, d), expert)]
    args = [blk_e, nused, xb, w_gate_up, b_gate_up[:, None, :], w_down, b_down[:, None, :]]
    aliases = {}
    if yb_prev is not None:
        in_specs.append(pl.BlockSpec(memory_space=pl.ANY))
        args.append(yb_prev)
        aliases = {len(args) - 1: 0}
    grid_spec = pltpu.PrefetchScalarGridSpec(
        num_scalar_prefetch=2,
        grid=(n_blk,),
        in_specs=in_specs,
        out_specs=pl.BlockSpec((EXPERT_ROWS, d), lambda j, be, nu: (chunk * n_blk + j, 0)),
        scratch_shapes=[pltpu.VMEM((d, ff2), BF16), pltpu.VMEM((d_ff, d), BF16)],
    )
    return pl.pallas_call(
        _expert_kernel,
        grid_spec=grid_spec,
        out_shape=jax.ShapeDtypeStruct((n_chunk * rows, d), F32),
        input_output_aliases=aliases,
        compiler_params=pltpu.CompilerParams(dimension_semantics=("arbitrary",), vmem_limit_bytes=VMEM_LIMIT),
        name="moe_experts",
    )(*args)


SC_BUFFER_BYTES = 128 * 1024
MOE_CHUNKS = 8


def _sc_gather_rows(table, idx):
    from jax.experimental.pallas import tpu_sc as plsc
    n_rows = idx.shape[0]
    d = table.shape[1]
    info = plsc.get_sparse_core_info()
    n_core, n_sub = info.num_cores, info.num_subcores
    per_w = n_rows // (n_core * n_sub)
    fit = SC_BUFFER_BYTES // (d * table.dtype.itemsize)
    g_rows = max(r for r in (8, 16, 32, 64, 128) if r <= fit and per_w % r == 0)
    n_chunk = per_w // g_rows
    assert per_w * n_core * n_sub == n_rows
    mesh = plsc.VectorSubcoreMesh(core_axis_name="c", subcore_axis_name="s")

    def body(table_hbm, idx_hbm, out_hbm, idx_v, buf0, buf1, sem0, sem1):
        base = (lax.axis_index("s") * n_core + lax.axis_index("c")) * per_w
        pltpu.sync_copy(idx_hbm.at[pl.ds(pl.multiple_of(base, 8), per_w)], idx_v)

        def gather(chunk, buf, sem):
            rows = idx_v.at[pl.ds(pl.multiple_of(chunk * g_rows, 8), g_rows)]
            return pltpu.make_async_copy(table_hbm.at[rows], buf, sem)

        def write(chunk, buf):
            pltpu.sync_copy(buf, out_hbm.at[pl.ds(pl.multiple_of(base + chunk * g_rows, 8), g_rows)])

        gather(0, buf0, sem0).start()

        @pl.loop(0, n_chunk // 2)
        def _(i):
            gather(2 * i + 1, buf1, sem1).start()
            gather(2 * i, buf0, sem0).wait()
            write(2 * i, buf0)

            @pl.when(2 * i + 2 < n_chunk)
            def _():
                gather(2 * i + 2, buf0, sem0).start()
            gather(2 * i + 1, buf1, sem1).wait()
            write(2 * i + 1, buf1)

        if n_chunk % 2:
            gather(n_chunk - 1, buf0, sem0).wait()
            write(n_chunk - 1, buf0)

    return pl.kernel(
        body, mesh=mesh, out_type=jax.ShapeDtypeStruct((n_rows, d), table.dtype),
        scratch_types=[pltpu.VMEM((per_w,), jnp.int32), pltpu.VMEM((g_rows, d), table.dtype),
                       pltpu.VMEM((g_rows, d), table.dtype), pltpu.SemaphoreType.DMA, pltpu.SemaphoreType.DMA],
        name="moe_sc_gather",
    )(table, idx)


def _sc_weighted_rows(table, idx, gate_lanes, n_tok):
    from jax.experimental.pallas import tpu_sc as plsc
    d = table.shape[1]
    info = plsc.get_sparse_core_info()
    n_core, n_sub, lanes = info.num_cores, info.num_subcores, info.num_lanes
    per_w = n_tok // (n_core * n_sub)
    w = 8
    n_chunk = per_w // w
    assert per_w * n_core * n_sub == n_tok and n_chunk * w == per_w and n_chunk % 2 == 0
    mesh = plsc.VectorSubcoreMesh(core_axis_name="c", subcore_axis_name="s")

    def body(table_hbm, idx_hbm, g_hbm, out_hbm, *scr):
        idx_v, g_v, rb, ob, sems = scr[0:4], scr[4:8], (scr[8:12], scr[12:16]), scr[16], scr[17:19]
        base = (lax.axis_index("s") * n_core + lax.axis_index("c")) * per_w
        for k in range(TOP_K):
            off = pl.multiple_of(k * n_tok + base, 8)
            pltpu.sync_copy(idx_hbm.at[pl.ds(off, per_w)], idx_v[k])
            pltpu.sync_copy(g_hbm.at[pl.ds(pl.multiple_of(off * lanes, 8), per_w * lanes)], g_v[k])

        def gathers(chunk, slot):
            rows = pl.ds(pl.multiple_of(chunk * w, 8), w)
            return [pltpu.make_async_copy(table_hbm.at[idx_v[k].at[rows]], rb[slot][k], sems[slot])
                    for k in range(TOP_K)]

        def combine(chunk, slot):
            @pl.loop(0, w)
            def _(i):
                gs = [g_v[k][pl.ds(pl.multiple_of((chunk * w + i) * lanes, lanes), lanes)] for k in range(TOP_K)]
                for j in range(d // lanes):
                    cols = pl.ds(j * lanes, lanes)
                    acc = gs[0] * rb[slot][0][i, cols]
                    for k in range(1, TOP_K):
                        acc = acc + gs[k] * rb[slot][k][i, cols]
                    ob[i, cols] = acc
            pltpu.sync_copy(ob, out_hbm.at[pl.ds(pl.multiple_of(base + chunk * w, 8), w)])

        for c in gathers(0, 0):
            c.start()

        @pl.loop(0, n_chunk // 2)
        def _(i):
            for c in gathers(2 * i + 1, 1):
                c.start()
            for c in gathers(2 * i, 0):
                c.wait()
            combine(2 * i, 0)

            @pl.when(2 * i + 2 < n_chunk)
            def _():
                for c in gathers(2 * i + 2, 0):
                    c.start()
            for c in gathers(2 * i + 1, 1):
                c.wait()
            combine(2 * i + 1, 1)

    scratch = ([pltpu.VMEM((per_w,), jnp.int32)] * TOP_K + [pltpu.VMEM((per_w * lanes,), F32)] * TOP_K
               + [pltpu.VMEM((w, d), F32)] * (2 * TOP_K + 1) + [pltpu.SemaphoreType.DMA] * 2)
    return pl.kernel(body, mesh=mesh, out_type=jax.ShapeDtypeStruct((n_tok, d), F32), scratch_types=scratch,
                     name="moe_sc_combine")(table, idx, gate_lanes)


def _combine_kernel(h_ref, p_ref, y_ref, gffn_ref, wple_ref, wpg_ref, gple_ref, o_ref):
    h2 = h_ref[...] + _rms(y_ref[...], gffn_ref[...])
    ple = (jnp.dot(p_ref[...].astype(BF16), wple_ref[...], preferred_element_type=F32)
           * jax.nn.sigmoid(jnp.dot(h2.astype(BF16), wpg_ref[...], preferred_element_type=F32)))
    o_ref[...] = h2 + _rms(ple, gple_ref[...])


def _combine_kernel_aliased(h_ref, p_ref, y_ref, gffn_ref, wple_ref, wpg_ref, gple_ref, prev_ref, o_ref):
    del prev_ref
    _combine_kernel(h_ref, p_ref, y_ref, gffn_ref, wple_ref, wpg_ref, gple_ref, o_ref)


def _combine(group, n_group, out_prev, h, p, y, g_ffn_post, w_ple_bf16, w_ple_gate_bf16, g_ple_post):
    t, d = h.shape
    tt = 256
    n_tg = t // n_group // tt
    const = lambda shape: pl.BlockSpec(shape, lambda i: (0, 0))
    tok = lambda w: pl.BlockSpec((tt, w), lambda i: (group * n_tg + i, 0))
    in_specs = [tok(d), tok(p.shape[1]), pl.BlockSpec((tt, d), lambda i: (i, 0)),
                const((1, d)), const(w_ple_bf16.shape), const(w_ple_gate_bf16.shape), const((1, d))]
    args = [h, p, y, g_ffn_post[None, :], w_ple_bf16, w_ple_gate_bf16, g_ple_post[None, :]]
    body, aliases = _combine_kernel, {}
    if out_prev is not None:
        in_specs.append(pl.BlockSpec(memory_space=pl.ANY))
        args.append(out_prev)
        body, aliases = _combine_kernel_aliased, {len(args) - 1: 0}
    return pl.pallas_call(
        body,
        grid=(n_tg,),
        in_specs=in_specs,
        out_specs=tok(d),
        out_shape=jax.ShapeDtypeStruct((t, d), F32),
        input_output_aliases=aliases,
        compiler_params=pltpu.CompilerParams(vmem_limit_bytes=VMEM_LIMIT),
        name="moe_combine_ple",
    )(*args)


def _layer(h, p, positions, g_mix_pre, w_in, w_dw, b_dw, g_conv_ln, b_conv_ln, w_out, g_mix_post, g_ffn_pre,
           w_router, b_router, w_gate_up, b_gate_up, w_down, b_down, g_ffn_post, w_ple, w_ple_gate, g_ple_post):
    b, s, d = h.shape
    t = b * s
    cos_t, sin_t = _rope_tables(positions)
    q, k, v, glu = _input_projection(h, cos_t, sin_t, g_mix_pre, w_in)
    attn = _dilated_attention(q, k, v)
    conv = _conformer_conv(glu, w_dw, b_dw, g_conv_ln, b_conv_ln)
    h1, u_ffn, idx, gates = _output_projection(attn, conv, h, w_out, g_mix_post, g_ffn_pre, w_router, b_router)
    cap = t * TOP_K + N_EXPERTS * EXPERT_ROWS
    n_blk = cap // EXPERT_ROWS
    dest, pend, pad, blk = _routing(idx, n_blk)
    blk_e = blk[0, :n_blk]
    nused = pend[N_EXPERTS - 1] // EXPERT_ROWS
    slot = _row_slots(dest, pad[:, 0], pend[:, 0], cap)
    tok_of_row = slot % t
    rows_c = cap // MOE_CHUNKS
    blk_c = n_blk // MOE_CHUNKS
    yb = None
    for c in range(MOE_CHUNKS):
        xb_c = _sc_gather_rows(u_ffn, tok_of_row[c * rows_c:(c + 1) * rows_c])
        nused_c = jnp.clip(nused - c * blk_c, 0, blk_c)
        yb = _experts(xb_c, c, MOE_CHUNKS, yb, blk_e[c * blk_c:(c + 1) * blk_c], nused_c,
                      w_gate_up, b_gate_up, w_down, b_down)
    t_g = t // MOE_CHUNKS
    w_ple_bf16, w_ple_gate_bf16 = w_ple.astype(BF16), w_ple_gate.astype(BF16)
    h1_flat, p_flat = h1.reshape(t, d), p.reshape(t, -1)
    out = None
    sc_lanes = 16
    for g in range(MOE_CHUNKS):
        toks = slice(g * t_g, (g + 1) * t_g)
        gate_lanes = jnp.broadcast_to(gates[:TOP_K, toks, None], (TOP_K, t_g, sc_lanes)).reshape(-1)
        y_g = _sc_weighted_rows(yb, dest[:, toks].reshape(TOP_K * t_g), gate_lanes, t_g)
        out = _combine(g, MOE_CHUNKS, out, h1_flat, p_flat, y_g, g_ffn_post, w_ple_bf16, w_ple_gate_bf16,
                       g_ple_post)
    return out.reshape(b, s, d)


def kernel(x, p, positions, g_mix_pre, w_in, w_dw, b_dw, g_conv_ln, b_conv_ln, w_out, g_mix_post, g_ffn_pre,
           w_router, b_router, w_gate_up, b_gate_up, w_down, b_down, g_ffn_post, w_ple, w_ple_gate, g_ple_post):
    h = x
    for i in range(p.shape[0]):
        h = _layer(h, p[i], positions, g_mix_pre[i], w_in[i], w_dw[i], b_dw[i], g_conv_ln[i], b_conv_ln[i],
                   w_out[i], g_mix_post[i], g_ffn_pre[i], w_router[i], b_router[i], w_gate_up[i], b_gate_up[i],
                   w_down[i], b_down[i], g_ffn_post[i], w_ple[i], w_ple_gate[i], g_ple_post[i])
    return h
```

```python
import functools

import numpy as np
import jax
import jax.numpy as jnp
from jax import lax
from jax.experimental import pallas as pl
from jax.experimental.pallas import tpu as pltpu

F32 = jnp.float32
BF16 = jnp.bfloat16

HEAD_DIM = 64
N_HEADS = 12
ATTN_WIDTH = N_HEADS * HEAD_DIM
CONV_CHANNELS = 256
CONV_WIDTH = 31
ROPE_DIM = HEAD_DIM // 4
ROPE_THETA = 500000.0
N_EXPERTS = 32
TOP_K = 4
SWIGLU_LIMIT = 7.0
SWIGLU_ALPHA = 1.702
NORM_EPS = 1e-6
WINDOW = 128
N_PLANES = 16
SPAN = N_PLANES * WINDOW
LANES = 128
SUBLANES = 8
NEG = -1e30
EXPERT_ROWS = 256
VMEM_LIMIT = 56 * 1024 * 1024


def _residue_of_plane(p):
    return 4 * (p % 4) + p // 4


def _rms(xv, g):
    var = jnp.mean(xv * xv, axis=-1, keepdims=True)
    return xv * lax.rsqrt(var + NORM_EPS) * g


def _rope_kernel(pos_ref, invf_ref, expand_ref, one_ref, sgn_ref, c_ref, s_ref):
    tn = (((0,), (0,)), ((), ()))
    for p in range(N_PLANES):
        ang = invf_ref[...] * pos_ref[0, p:p + 1, :].astype(F32)
        spread = lambda t: lax.dot_general(t, expand_ref[...], tn, precision=lax.Precision.HIGHEST,
                                           preferred_element_type=F32)
        c_ref[0, p] = spread(jnp.cos(ang)) + one_ref[...]
        s_ref[0, p] = spread(jnp.sin(ang)) * sgn_ref[...]


def _rope_tables(positions):
    b, s = positions.shape
    sm = s // N_PLANES
    mt = min(sm, 128)
    plane_res = np.array([_residue_of_plane(p) for p in range(N_PLANES)])
    pos_planes = positions.reshape(b, sm, N_PLANES).transpose(0, 2, 1)[:, plane_res]
    half = ROPE_DIM // 2
    lane = np.arange(LANES) % HEAD_DIM
    inv_freq = ROPE_THETA ** (-jnp.arange(0, ROPE_DIM, 2, dtype=F32) / ROPE_DIM)
    rotary = lane < ROPE_DIM
    expand = jnp.asarray((np.arange(half)[:, None] == lane[None, :] % half) & rotary[None, :], F32)
    one = jnp.asarray(~rotary, F32)[None, :]
    sgn = jnp.asarray(np.where(lane < half, -1.0, 1.0), F32)[None, :]
    row = pl.BlockSpec((1, LANES), lambda i, j: (0, 0))
    out = pl.BlockSpec((1, N_PLANES, mt, LANES), lambda i, j: (i, 0, j, 0))
    return pl.pallas_call(
        _rope_kernel,
        grid=(b, sm // mt),
        in_specs=[pl.BlockSpec((1, N_PLANES, mt), lambda i, j: (i, 0, j)),
                  pl.BlockSpec((half, 1), lambda i, j: (0, 0)), pl.BlockSpec((half, LANES), lambda i, j: (0, 0)),
                  row, row],
        out_specs=[out, out],
        out_shape=[jax.ShapeDtypeStruct((b, N_PLANES, sm, LANES), F32)] * 2,
        name="rope_tables",
    )(pos_planes, inv_freq[:, None], expand, one, sgn)


PERM_TOKENS = 256
PERM_ROWS = PERM_TOKENS // N_PLANES


def _plane_permutation():
    perm = np.zeros((PERM_TOKENS, PERM_TOKENS), np.float32)
    for p in range(N_PLANES):
        for ml in range(PERM_ROWS):
            perm[PERM_ROWS * p + ml, N_PLANES * ml + _residue_of_plane(p)] = 1.0
    return perm


def _inproj_kernel(x_ref, c_ref, s_ref, g_ref, perm_ref, wqkv_ref, wc_ref, q_ref, k_ref, v_ref, glu_ref):
    g = g_ref[...]
    tt = x_ref.shape[1]
    lane = lax.broadcasted_iota(jnp.int32, (1, LANES), 1) % HEAD_DIM
    first_half = lane < ROPE_DIM // 2

    def rotary(t, cos, sin):
        outs = []
        for j in range(ATTN_WIDTH // LANES):
            tj = t[:, j * LANES:(j + 1) * LANES]
            partner = jnp.where(first_half, pltpu.roll(tj, LANES - ROPE_DIM // 2, 1),
                                pltpu.roll(tj, ROPE_DIM // 2, 1))
            outs.append(tj * cos + partner * sin)
        return jnp.concatenate(outs, axis=1)

    un = _rms(x_ref[0], g).astype(BF16)
    pc = jnp.dot(un, wc_ref[...], preferred_element_type=F32)
    glu_ref[0] = pc[:, :CONV_CHANNELS] * jax.nn.sigmoid(pc[:, CONV_CHANNELS:])

    for sub in range(tt // PERM_TOKENS):
        rows = slice(sub * PERM_ROWS, (sub + 1) * PERM_ROWS)
        u = jnp.dot(perm_ref[...], un[sub * PERM_TOKENS:(sub + 1) * PERM_TOKENS],
                    preferred_element_type=F32).astype(BF16)
        cos = jnp.concatenate([c_ref[0, p, rows, :] for p in range(N_PLANES)], axis=0)
        sin = jnp.concatenate([s_ref[0, p, rows, :] for p in range(N_PLANES)], axis=0)
        proj = jnp.dot(u, wqkv_ref[...], preferred_element_type=F32)
        q = (rotary(proj[:, :ATTN_WIDTH], cos, sin) * (HEAD_DIM ** -0.5)).astype(BF16)
        k = rotary(proj[:, ATTN_WIDTH:2 * ATTN_WIDTH], cos, sin).astype(BF16)
        v = proj[:, 2 * ATTN_WIDTH:].astype(BF16)
        for p in range(N_PLANES):
            chunk = slice(p * PERM_ROWS, (p + 1) * PERM_ROWS)
            q_ref[0, p, rows, :] = q[chunk]
            k_ref[0, p, rows, :] = k[chunk]
            v_ref[0, p, rows, :] = v[chunk]


def _input_projection(x, cos_t, sin_t, g_mix_pre, w_in):
    b, s, d = x.shape
    tt = 1024
    mc = tt // N_PLANES
    wqkv = w_in[:, :3 * ATTN_WIDTH].astype(BF16)
    wc = w_in[:, 3 * ATTN_WIDTH:].astype(BF16)
    perm = jnp.asarray(_plane_permutation(), BF16)
    plane = lambda w: pl.BlockSpec((1, N_PLANES, mc, w), lambda i, j: (i, 0, j, 0))
    plane_shape = jax.ShapeDtypeStruct((b, N_PLANES, s // N_PLANES, ATTN_WIDTH), BF16)
    tok = lambda w: pl.BlockSpec((1, tt, w), lambda i, j: (i, j, 0))
    const = lambda shape: pl.BlockSpec(shape, lambda i, j: (0, 0))
    return pl.pallas_call(
        _inproj_kernel,
        grid=(b, s // tt),
        in_specs=[tok(d), plane(LANES), plane(LANES), const((1, d)), const(perm.shape),
                  const(wqkv.shape), const(wc.shape)],
        out_specs=[plane(ATTN_WIDTH), plane(ATTN_WIDTH), plane(ATTN_WIDTH), tok(CONV_CHANNELS)],
        out_shape=[plane_shape, plane_shape, plane_shape,
                   jax.ShapeDtypeStruct((b, s, CONV_CHANNELS), F32)],
        compiler_params=pltpu.CompilerParams(vmem_limit_bytes=VMEM_LIMIT),
        name="input_projection",
    )(x, cos_t, sin_t, g_mix_pre[None, :], perm, wqkv, wc)


def _attention_biases():
    band = lambda j: np.where((j >= 0) & (j <= WINDOW), 0.0, NEG).astype(np.float32)
    cols = lambda m: np.where(m, NEG, 0.0).astype(np.float32)
    twice = lambda a: np.concatenate([a, a], axis=0)
    mq = np.arange(WINDOW)[:, None]
    kj = np.arange(2 * WINDOW)[None, :]
    j16 = mq + WINDOW - kj
    prev16 = kj < WINDOW
    row = np.arange(128)[:, None]
    col = np.arange(256)[None, :]
    j4 = 4 * (row % 32 - (col % 64 - 32)) + row // 32 - col // 64
    prev4 = col % 64 < 32
    row = np.arange(256)[:, None]
    col = np.arange(512)[None, :]
    res = np.vectorize(_residue_of_plane)
    j1 = 16 * (row % 16 - (col % 32 - 16)) + res(row // 16) - res(col // 32)
    prev1 = col % 32 < 16
    b1 = np.stack([twice(band(j1[:128])), twice(band(j1[128:]))])
    return [jnp.asarray(a) for a in (twice(band(j16)), twice(band(j4)), b1, cols(prev16), cols(prev4), cols(prev1))]


UNROLL = 8


def _attn_kernel(q_ref, kc_ref, kp_ref, vc_ref, vp_ref, b16_ref, b4_ref, b1_ref, p16_ref, p4_ref, p1_ref,
                 o_ref, m_scr, l_scr, a_scr, bias16_scr):
    no_prev = (pl.program_id(1) == 0).astype(F32)
    head0 = lax.broadcasted_iota(jnp.int32, (1, LANES), 1) < HEAD_DIM
    bias16_scr[...] = b16_ref[...] + no_prev * p16_ref[...]

    def tile(qt, kt, vt, bias):
        n = qt.shape[0]
        zero = jnp.zeros_like(qt)
        q2 = jnp.concatenate([jnp.where(head0, qt, zero), jnp.where(head0, zero, qt)], axis=0)
        s = lax.dot_general(q2, kt, (((1,), (1,)), ((), ())), preferred_element_type=F32) + bias
        m = jnp.max(s, axis=-1, keepdims=True)
        e = jnp.exp(s - m).astype(BF16)
        va = jnp.concatenate([vt, jnp.ones_like(vt)], axis=1)
        o = jnp.dot(e, va, preferred_element_type=F32)
        pick = lambda top, bot: jnp.where(head0, top, bot)
        mm = pick(jnp.broadcast_to(m[:n], (n, LANES)), jnp.broadcast_to(m[n:], (n, LANES)))
        return mm, pick(o[:n, LANES:], o[n:, LANES:]), pick(o[:n, :LANES], o[n:, :LANES])

    def put(branch, start, size, stats, off):
        for scr, val in zip((m_scr, l_scr, a_scr), stats):
            scr[branch, pl.ds(start, size), :] = val[off:off + size]

    def body16(i, carry):
        for p in [UNROLL * i + u for u in range(UNROLL)]:
            kt = jnp.concatenate([kp_ref[0, p], kc_ref[0, p]], axis=0)
            vt = jnp.concatenate([vp_ref[0, p], vc_ref[0, p]], axis=0)
            put(0, pl.multiple_of(p * WINDOW, WINDOW), WINDOW, tile(q_ref[0, p], kt, vt, bias16_scr[...]), 0)
        return carry
    lax.fori_loop(0, N_PLANES // UNROLL, body16, 0)

    def body4(g, carry):
        for c, i in [(2 * g + cc, ii) for cc in range(2) for ii in range(4)]:
            qt = jnp.concatenate([q_ref[0, 4 * c + a, 32 * i:32 * i + 32, :] for a in range(4)], axis=0)
            if i == 0:
                ks = [x for a in range(4) for x in (kp_ref[0, 4 * c + a, 96:128, :], kc_ref[0, 4 * c + a, 0:32, :])]
                vs = [x for a in range(4) for x in (vp_ref[0, 4 * c + a, 96:128, :], vc_ref[0, 4 * c + a, 0:32, :])]
                bias = b4_ref[...] + no_prev * p4_ref[...]
            else:
                ks = [kc_ref[0, 4 * c + a, 32 * i - 32:32 * i + 32, :] for a in range(4)]
                vs = [vc_ref[0, 4 * c + a, 32 * i - 32:32 * i + 32, :] for a in range(4)]
                bias = b4_ref[...]
            stats = tile(qt, jnp.concatenate(ks, axis=0), jnp.concatenate(vs, axis=0), bias)
            for a in range(4):
                put(1, pl.multiple_of((4 * c + a) * WINDOW + 32 * i, 32), 32, stats, 32 * a)
        return carry
    lax.fori_loop(0, 2, body4, 0)

    def tile1(i, first):
        rq = pl.ds(pl.multiple_of(16 * i, 16), 16)
        if first:
            ks = [x for p in range(N_PLANES) for x in (kp_ref[0, p, 112:128, :], kc_ref[0, p, 0:16, :])]
            vs = [x for p in range(N_PLANES) for x in (vp_ref[0, p, 112:128, :], vc_ref[0, p, 0:16, :])]
        else:
            rk = pl.ds(pl.multiple_of(16 * i - 16, 16), 32)
            ks = [kc_ref[0, p, rk, :] for p in range(N_PLANES)]
            vs = [vc_ref[0, p, rk, :] for p in range(N_PLANES)]
        kt = jnp.concatenate(ks, axis=0)
        vt = jnp.concatenate(vs, axis=0)
        for half in range(2):
            planes = range(8 * half, 8 * half + 8)
            qt = jnp.concatenate([q_ref[0, p, rq, :] for p in planes], axis=0)
            bias = b1_ref[half] + no_prev * p1_ref[...] if first else b1_ref[half]
            stats = tile(qt, kt, vt, bias)
            for p in planes:
                put(2, pl.multiple_of(p * WINDOW + 16 * i, 16), 16, stats, 16 * (p - 8 * half))

    tile1(0, True)
    tile1(1, False)

    def body1(g, carry):
        tile1(2 * g, False)
        tile1(2 * g + 1, False)
        return carry
    lax.fori_loop(1, WINDOW // 32, body1, 0)

    def combine(i, carry):
        for p in (2 * i, 2 * i + 1):
            rows = pl.ds(pl.multiple_of(p * WINDOW, WINDOW), WINDOW)
            ms = [m_scr[b, rows, :] for b in range(3)]
            mx = jnp.maximum(jnp.maximum(ms[0], ms[1]), ms[2])
            ws = [jnp.exp(m - mx) for m in ms]
            den = ws[0] * l_scr[0, rows, :] + ws[1] * l_scr[1, rows, :] + ws[2] * l_scr[2, rows, :]
            num = ws[0] * a_scr[0, rows, :] + ws[1] * a_scr[1, rows, :] + ws[2] * a_scr[2, rows, :]
            o_ref[0, p] = (num / den).astype(BF16)
        return carry
    lax.fori_loop(0, N_PLANES // 2, combine, 0)


def _dilated_attention(q, k, v):
    b, _, sm, _ = q.shape
    n_span = sm // WINDOW
    cur = pl.BlockSpec((1, N_PLANES, WINDOW, LANES), lambda i, j, h: (i, 0, j, h))
    prev = pl.BlockSpec((1, N_PLANES, WINDOW, LANES), lambda i, j, h: (i, 0, jnp.maximum(j - 1, 0), h))
    biases = _attention_biases()
    bias_specs = [pl.BlockSpec(a.shape, lambda i, j, h, nd=a.ndim: (0,) * nd) for a in biases]
    stats = pltpu.VMEM((3, SPAN, LANES), F32)
    return pl.pallas_call(
        _attn_kernel,
        grid=(b, n_span, ATTN_WIDTH // LANES),
        in_specs=[cur, cur, prev, cur, prev] + bias_specs,
        out_specs=cur,
        out_shape=jax.ShapeDtypeStruct(q.shape, BF16),
        scratch_shapes=[stats, stats, stats, pltpu.VMEM((2 * WINDOW, 2 * WINDOW), F32)],
        compiler_params=pltpu.CompilerParams(vmem_limit_bytes=VMEM_LIMIT),
        name="dilated_attention",
    )(q, k, k, v, v, *biases)


CONV_HALO = 32


def _conv_kernel(cur_ref, prev_ref, w_ref, b_ref, g_ref, bl_ref, o_ref, scr, *, chunk):
    tt = cur_ref.shape[1]
    has_prev = pl.program_id(1) > 0
    scr[0, 0:CONV_HALO, :] = jnp.where(has_prev, prev_ref[0], 0.0)
    scr[0, CONV_HALO:CONV_HALO + tt, :] = cur_ref[0]
    aligned_rows = tt + CONV_HALO - SUBLANES
    for s in range(1, SUBLANES):
        scr[s, 0:aligned_rows, :] = scr[0, s:s + aligned_rows, :]
    lead = CONV_HALO - (CONV_WIDTH - 1)
    for c0 in range(0, tt, chunk):
        acc = jnp.zeros((chunk, CONV_CHANNELS), F32)
        for j in range(CONV_WIDTH):
            s, a = (lead + j) % SUBLANES, (lead + j) // SUBLANES * SUBLANES
            acc = acc + w_ref[j:j + 1, :] * scr[s, c0 + a:c0 + a + chunk, :]
        y = acc + b_ref[...]
        mu = jnp.mean(y, axis=-1, keepdims=True)
        var = jnp.mean(jnp.square(y - mu), axis=-1, keepdims=True)
        yn = (y - mu) * lax.rsqrt(var + NORM_EPS) * g_ref[...] + bl_ref[...]
        o_ref[0, c0:c0 + chunk, :] = (yn * jax.nn.sigmoid(yn)).astype(BF16)


def _conformer_conv(glu, w_dw, b_dw, g_ln, b_ln):
    b, s, c = glu.shape
    tt = 512
    row = pl.BlockSpec((1, c), lambda i, j: (0, 0))
    return pl.pallas_call(
        functools.partial(_conv_kernel, chunk=128),
        grid=(b, s // tt),
        in_specs=[pl.BlockSpec((1, tt, c), lambda i, j: (i, j, 0)),
                  pl.BlockSpec((1, CONV_HALO, c), lambda i, j: (i, jnp.maximum(j * (tt // CONV_HALO) - 1, 0), 0)),
                  pl.BlockSpec((CONV_WIDTH, c), lambda i, j: (0, 0)), row, row, row],
        out_specs=pl.BlockSpec((1, tt, c), lambda i, j: (i, j, 0)),
        out_shape=jax.ShapeDtypeStruct((b, s, c), BF16),
        scratch_shapes=[pltpu.VMEM((SUBLANES, CONV_HALO + tt, c), F32)],
        name="conformer_conv",
    )(glu, glu, w_dw[:, 0, :], b_dw[None, :], g_ln[None, :], b_ln[None, :])


def _outproj_kernel(attn_ref, conv_ref, x_ref, permt_ref, woa_ref, woc_ref, gpost_ref, gffn_ref, wr_ref, br_ref,
                    h_ref, u_ref, idx_ref, gate_ref):
    tt = x_ref.shape[1]
    nat = []
    for sub in range(tt // PERM_TOKENS):
        rows = slice(sub * PERM_ROWS, (sub + 1) * PERM_ROWS)
        a = jnp.concatenate([attn_ref[0, p, rows, :] for p in range(N_PLANES)], axis=0)
        nat.append(jnp.dot(permt_ref[...], a, preferred_element_type=F32).astype(BF16))
    mix = (jnp.dot(jnp.concatenate(nat, axis=0), woa_ref[...], preferred_element_type=F32)
           + jnp.dot(conv_ref[0], woc_ref[...], preferred_element_type=F32))
    h = x_ref[0] + _rms(mix, gpost_ref[...])
    h_ref[0] = h
    u = _rms(h, gffn_ref[...])
    half = u.shape[1] // 2
    ub = u.astype(BF16).astype(F32)
    u_ref[...] = ((lax.bitcast_convert_type(ub[:, :half], jnp.uint32) >> 16)
                  | (lax.bitcast_convert_type(ub[:, half:], jnp.uint32) & jnp.uint32(0xFFFF0000)))
    u_hi = u.astype(BF16)
    u_lo = (u - u_hi.astype(F32)).astype(BF16)
    nt = (((1,), (1,)), ((), ()))
    by_hi = lax.dot_general(wr_ref[...], u_hi, nt, preferred_element_type=F32)
    by_lo = lax.dot_general(wr_ref[:N_EXPERTS], u_lo, nt, preferred_element_type=F32)
    logits = by_hi[:N_EXPERTS] + (by_hi[N_EXPERTS:] + by_lo) + br_ref[...]
    rows = lax.broadcasted_iota(jnp.int32, logits.shape, 0)
    vals = logits
    tops, idxs = [], []
    for _ in range(TOP_K):
        mx = jnp.max(vals, axis=0, keepdims=True)
        ix = jnp.min(jnp.where(vals == mx, rows, N_EXPERTS), axis=0, keepdims=True)
        tops.append(mx)
        idxs.append(ix)
        vals = jnp.where(rows == ix, -jnp.inf, vals)
    ex = [jnp.exp(t - tops[0]) for t in tops]
    den = ex[0] + ex[1] + ex[2] + ex[3]
    idx_ref[...] = jnp.concatenate(idxs, axis=0)
    gate_ref[...] = jnp.concatenate([e / den for e in ex] + [jnp.zeros((8 - TOP_K, tt), F32)], axis=0)


def _output_projection(attn, conv, x, w_out, g_mix_post, g_ffn_pre, w_router, b_router):
    b, s, d = x.shape
    tt = 512
    mc = tt // N_PLANES
    n_t = s // tt
    woa = w_out[:ATTN_WIDTH].astype(BF16)
    woc = w_out[ATTN_WIDTH:].astype(BF16)
    wr_hi = w_router.T.astype(BF16)
    wr_split = jnp.concatenate([wr_hi, (w_router.T - wr_hi.astype(F32)).astype(BF16)], axis=0)
    permt =jnp.asarray(_plane_permutation().T, BF16)
    const = lambda shape: pl.BlockSpec(shape, lambda i, j: (0, 0))
    flat = lambda w: pl.BlockSpec((tt, w), lambda i, j: (i * n_t + j, 0))
    lanes = lambda r: pl.BlockSpec((r, tt), lambda i, j: (0, i * n_t + j))
    return pl.pallas_call(
        _outproj_kernel,
        grid=(b, n_t),
        in_specs=[pl.BlockSpec((1, N_PLANES, mc, ATTN_WIDTH), lambda i, j: (i, 0, j, 0)),
                  pl.BlockSpec((1, tt, CONV_CHANNELS), lambda i, j: (i, j, 0)),
                  pl.BlockSpec((1, tt, d), lambda i, j: (i, j, 0)),
                  const(permt.shape), const(woa.shape), const(woc.shape), const((1, d)), const((1, d)),
                  const((2 * N_EXPERTS, d)), const((N_EXPERTS, 1))],
        out_specs=[pl.BlockSpec((1, tt, d), lambda i, j: (i, j, 0)), flat(d // 2), lanes(TOP_K), lanes(8)],
        out_shape=[jax.ShapeDtypeStruct((b, s, d), F32),
                   jax.ShapeDtypeStruct((b * s, d // 2), jnp.uint32),
                   jax.ShapeDtypeStruct((TOP_K, b * s), jnp.int32),
                   jax.ShapeDtypeStruct((8, b * s), F32)],
        compiler_params=pltpu.CompilerParams(vmem_limit_bytes=VMEM_LIMIT),
        name="output_projection_router",
    )(attn, conv, x, permt, woa, woc, g_mix_post[None, :], g_ffn_pre[None, :], wr_split, b_router[:, None])


def _route_kernel(idx_ref, dest_ref, pend_ref, pad_ref, blk_ref, carry, pstart):
    phase = pl.program_id(0)
    step = pl.program_id(1)
    tt = idx_ref.shape[1]
    rows = lax.broadcasted_iota(jnp.int32, (N_EXPERTS, tt), 0)
    hot = [rows == idx_ref[k:k + 1, :] for k in range(TOP_K)]
    memb = sum(h.astype(F32) for h in hot)

    @pl.when((phase == 0) & (step == 0))
    def _():
        carry[...] = jnp.zeros_like(carry)

    @pl.when((phase == 1) & (step == 0))
    def _():
        counts = carry[...]
        padded = jnp.floor((counts + (EXPERT_ROWS - 1)) * (1.0 / EXPERT_ROWS)) * EXPERT_ROWS
        tri = (lax.broadcasted_iota(jnp.int32, (N_EXPERTS, N_EXPERTS), 1)
               <= lax.broadcasted_iota(jnp.int32, (N_EXPERTS, N_EXPERTS), 0)).astype(F32)
        pend = jnp.dot(tri, padded, precision=lax.Precision.HIGHEST, preferred_element_type=F32)
        pstart[...] = pend - padded
        pend_ref[...] = pend.astype(jnp.int32)
        pad_ref[...] = (pend - padded + counts).astype(jnp.int32)
        starts = lax.broadcasted_iota(jnp.int32, (N_EXPERTS, blk_ref.shape[1]), 1) * EXPERT_ROWS
        ended = (pend.astype(jnp.int32) <= starts).astype(jnp.int32)
        blk_ref[...] = jnp.minimum(jnp.sum(ended, axis=0, keepdims=True), N_EXPERTS - 1)
        carry[...] = jnp.zeros_like(carry)

    @pl.when(phase == 1)
    def _():
        earlier = (lax.broadcasted_iota(jnp.int32, (tt, tt), 0)
                   < lax.broadcasted_iota(jnp.int32, (tt, tt), 1)).astype(BF16)
        row = jnp.dot(memb.astype(BF16), earlier, preferred_element_type=F32) + (carry[...] + pstart[...])
        dest_ref[...] = jnp.concatenate(
            [jnp.sum(jnp.where(h, row, 0.0), axis=0, keepdims=True) for h in hot], axis=0).astype(jnp.int32)

    carry[...] = carry[...] + jnp.sum(memb, axis=1, keepdims=True)


def _routing(idx, n_blk):
    _, t = idx.shape
    tt = 512
    blk_lanes = -(-n_blk // LANES) * LANES
    return pl.pallas_call(
        _route_kernel,
        grid=(2, t // tt),
        in_specs=[pl.BlockSpec((TOP_K, tt), lambda ph, i: (0, i))],
        out_specs=[pl.BlockSpec((TOP_K, tt), lambda ph, i: (0, i * ph)),
                   pl.BlockSpec((N_EXPERTS, 1), lambda ph, i: (0, 0)),
                   pl.BlockSpec((N_EXPERTS, 1), lambda ph, i: (0, 0)),
                   pl.BlockSpec((1, blk_lanes), lambda ph, i: (0, 0))],
        out_shape=[jax.ShapeDtypeStruct((TOP_K, t), jnp.int32), jax.ShapeDtypeStruct((N_EXPERTS, 1), jnp.int32),
                   jax.ShapeDtypeStruct((N_EXPERTS, 1), jnp.int32), jax.ShapeDtypeStruct((1, blk_lanes), jnp.int32)],
        scratch_shapes=[pltpu.VMEM((N_EXPERTS, 1), F32), pltpu.VMEM((N_EXPERTS, 1), F32)],
        compiler_params=pltpu.CompilerParams(dimension_semantics=("arbitrary", "arbitrary")),
        name="moe_routing",
    )(idx)


def _slot_kernel(pad_ref, pend_ref, d0_ref, d1_ref, d2_ref, d3_ref, slot_ref, *, n_tok):
    i = pl.program_id(0)
    dest_refs = (d0_ref, d1_ref, d2_ref, d3_ref)
    tt = d0_ref.shape[0]

    @pl.when(i == 0)
    def _():
        def fill(r, c):
            slot_ref[r] = TOP_K * n_tok + lax.rem(r, EXPERT_ROWS)
            return c

        def per_expert(e, carry):
            return lax.fori_loop(pad_ref[e], pend_ref[e], fill, carry)
        lax.fori_loop(0, N_EXPERTS, per_expert, 0)
        lax.fori_loop(pend_ref[N_EXPERTS - 1], slot_ref.shape[0], fill, 0)

    unroll = 16

    def body(g, carry):
        tl0 = g * unroll
        first = [k * n_tok + i * tt + tl0 for k in range(TOP_K)]
        for u in range(unroll):
            for k in range(TOP_K):
                slot_ref[dest_refs[k][tl0 + u]] = first[k] + u
        return carry
    lax.fori_loop(0, tt // unroll, body, 0)


def _row_slots(dest, pad, pend, cap):
    _, t = dest.shape
    tt = 1024
    n_t = t // tt
    smem = pl.BlockSpec(memory_space=pltpu.SMEM)
    choice = lambda k: pl.BlockSpec((tt,), lambda i: (k * n_t + i,), memory_space=pltpu.SMEM)
    flat = dest.reshape(TOP_K * t)
    return pl.pallas_call(
        functools.partial(_slot_kernel, n_tok=t),
        grid=(n_t,),
        in_specs=[smem, smem] + [choice(k) for k in range(TOP_K)],
        out_specs=smem,
        out_shape=jax.ShapeDtypeStruct((cap,), jnp.int32),
        compiler_params=pltpu.CompilerParams(dimension_semantics=("arbitrary",)),
        name="moe_row_slots",
    )(pad, pend, flat, flat, flat, flat)


def _expert_kernel(blk_e_ref, nused_ref, xb_ref, wgu_ref, bgu_ref, wd_ref, bd_ref, *rest):
    yb_ref, wgu_b, wd_b = rest[-3:]
    j = pl.program_id(0)
    nused = nused_ref[0]
    d_ff = wd_ref.shape[1]

    new_expert = (j == 0) | (blk_e_ref[j] != blk_e_ref[jnp.maximum(j - 1, 0)])

    @pl.when((j < nused) & new_expert)
    def _():
        def cast(src, dst):
            def body(i, carry):
                rows = pl.ds(pl.multiple_of(i * LANES, LANES), LANES)
                dst[rows, :] = src[0, rows, :].astype(BF16)
                return carry
            lax.fori_loop(0, dst.shape[0] // LANES, body, 0)
        cast(wgu_ref, wgu_b)
        cast(wd_ref, wd_b)

    @pl.when(j < nused)
    def _():
        word = xb_ref[...]
        half = word.shape[1]
        lo = lax.bitcast_convert_type(word << 16, F32).astype(BF16)
        hi = lax.bitcast_convert_type(word & jnp.uint32(0xFFFF0000), F32).astype(BF16)
        gu = (jnp.dot(lo, wgu_b[:half, :], preferred_element_type=F32)
              + jnp.dot(hi, wgu_b[half:, :], preferred_element_type=F32) + bgu_ref[0])
        gate = jnp.minimum(gu[:, :d_ff], SWIGLU_LIMIT)
        up = jnp.clip(gu[:, d_ff:], -SWIGLU_LIMIT, SWIGLU_LIMIT)
        hid = (up + 1.0) * (gate * jax.nn.sigmoid(gate * SWIGLU_ALPHA))
        yb_ref[...] = jnp.dot(hid.astype(BF16), wd_b[...], preferred_element_type=F32) + bd_ref[0]

    @pl.when(j >= nused)
    def _():
        yb_ref[...] = jnp.zeros_like(yb_ref)


def _experts(xb, chunk, n_chunk, yb_prev, blk_e, nused, w_gate_up, b_gate_up, w_down, b_down):
    rows, d_in = xb.shape
    e, d, ff2 = w_gate_up.shape
    d_ff = w_down.shape[1]
    n_blk = rows // EXPERT_ROWS
    live = lambda j, nu: jnp.maximum(jnp.minimum(j, nu[0] - 1), 0)
    expert = lambda j, be, nu: (be[live(j, nu)], 0, 0)
    in_specs = [pl.BlockSpec((EXPERT_ROWS, d_in), lambda j, be, nu: (live(j, nu), 0)),
                pl.BlockSpec((1, d, ff2), expert), pl.BlockSpec((1, 1, ff2), expert),
                pl.BlockSpec((1, d_ff, d), expert), pl.BlockSpec((1, 1, d), expert)]
    args = [blk_e, nused, xb, w_gate_up, b_gate_up[:, None, :], w_down, b_down[:, None, :]]
    aliases = {}
    if yb_prev is not None:
        in_specs.append(pl.BlockSpec(memory_space=pl.ANY))
        args.append(yb_prev)
        aliases = {len(args) - 1: 0}
    grid_spec = pltpu.PrefetchScalarGridSpec(
        num_scalar_prefetch=2,
        grid=(n_blk,),
        in_specs=in_specs,
        out_specs=pl.BlockSpec((EXPERT_ROWS, d), lambda j, be, nu: (chunk * n_blk + j, 0)),
        scratch_shapes=[pltpu.VMEM((d, ff2), BF16), pltpu.VMEM((d_ff, d), BF16)],
    )
    return pl.pallas_call(
        _expert_kernel,
        grid_spec=grid_spec,
        out_shape=jax.ShapeDtypeStruct((n_chunk * rows, d), F32),
        input_output_aliases=aliases,
        compiler_params=pltpu.CompilerParams(dimension_semantics=("arbitrary",), vmem_limit_bytes=VMEM_LIMIT),
        name="moe_experts",
    )(*args)


SC_BUFFER_BYTES = 128 * 1024
MOE_CHUNKS = 8
MOE_GROUPS = 4


def _sc_gather_rows(table, idx):
    from jax.experimental.pallas import tpu_sc as plsc
    n_rows = idx.shape[0]
    d = table.shape[1]
    info = plsc.get_sparse_core_info()
    n_core, n_sub = info.num_cores, info.num_subcores
    per_w = n_rows // (n_core * n_sub)
    fit = SC_BUFFER_BYTES // (d * table.dtype.itemsize)
    g_rows = max(r for r in (8, 16, 32, 64, 128) if r <= fit and per_w % r == 0)
    n_chunk = per_w // g_rows
    assert per_w * n_core * n_sub == n_rows
    mesh = plsc.VectorSubcoreMesh(core_axis_name="c", subcore_axis_name="s")

    def body(table_hbm, idx_hbm, out_hbm, idx_v, buf0, buf1, sem0, sem1):
        base = (lax.axis_index("s") * n_core + lax.axis_index("c")) * per_w
        pltpu.sync_copy(idx_hbm.at[pl.ds(pl.multiple_of(base, 8), per_w)], idx_v)

        def gather(chunk, buf, sem):
            rows = idx_v.at[pl.ds(pl.multiple_of(chunk * g_rows, 8), g_rows)]
            return pltpu.make_async_copy(table_hbm.at[rows], buf, sem)

        def write(chunk, buf):
            pltpu.sync_copy(buf, out_hbm.at[pl.ds(pl.multiple_of(base + chunk * g_rows, 8), g_rows)])

        gather(0, buf0, sem0).start()

        @pl.loop(0, n_chunk // 2)
        def _(i):
            gather(2 * i + 1, buf1, sem1).start()
            gather(2 * i, buf0, sem0).wait()
            write(2 * i, buf0)

            @pl.when(2 * i + 2 < n_chunk)
            def _():
                gather(2 * i + 2, buf0, sem0).start()
            gather(2 * i + 1, buf1, sem1).wait()
            write(2 * i + 1, buf1)

        if n_chunk % 2:
            gather(n_chunk - 1, buf0, sem0).wait()
            write(n_chunk - 1, buf0)

    return pl.kernel(
        body, mesh=mesh, out_type=jax.ShapeDtypeStruct((n_rows, d), table.dtype),
        scratch_types=[pltpu.VMEM((per_w,), jnp.int32), pltpu.VMEM((g_rows, d), table.dtype),
                       pltpu.VMEM((g_rows, d), table.dtype), pltpu.SemaphoreType.DMA, pltpu.SemaphoreType.DMA],
        name="moe_sc_gather",
    )(table, idx)


def _sc_weighted_rows(table, idx, gate_lanes, n_tok):
    from jax.experimental.pallas import tpu_sc as plsc
    d = table.shape[1]
    info = plsc.get_sparse_core_info()
    n_core, n_sub, lanes = info.num_cores, info.num_subcores, info.num_lanes
    per_w = n_tok // (n_core * n_sub)
    w = 8
    n_chunk = per_w // w
    assert per_w * n_core * n_sub == n_tok and n_chunk * w == per_w and n_chunk % 2 == 0
    mesh = plsc.VectorSubcoreMesh(core_axis_name="c", subcore_axis_name="s")

    def body(table_hbm, idx_hbm, g_hbm, out_hbm, *scr):
        idx_v, g_v, rb, ob = scr[0:4], scr[4:8], (scr[8:12], scr[12:16]), scr[16:18]
        sems, wsems = scr[18:20], scr[20:22]
        base = (lax.axis_index("s") * n_core + lax.axis_index("c")) * per_w
        for k in range(TOP_K):
            off = pl.multiple_of(k * n_tok + base, 8)
            pltpu.sync_copy(idx_hbm.at[pl.ds(off, per_w)], idx_v[k])
            pltpu.sync_copy(g_hbm.at[pl.ds(pl.multiple_of(off * lanes, 8), per_w * lanes)], g_v[k])

        def gathers(chunk, slot):
            rows = pl.ds(pl.multiple_of(chunk * w, 8), w)
            return [pltpu.make_async_copy(table_hbm.at[idx_v[k].at[rows]], rb[slot][k], sems[slot])
                    for k in range(TOP_K)]

        def write(chunk, slot):
            return pltpu.make_async_copy(ob[slot], out_hbm.at[pl.ds(pl.multiple_of(base + chunk * w, 8), w)],
                                         wsems[slot])

        def combine(chunk, slot):
            @pl.loop(0, w)
            def _(i):
                gs = [g_v[k][pl.ds(pl.multiple_of((chunk * w + i) * lanes, lanes), lanes)] for k in range(TOP_K)]
                for j in range(d // lanes):
                    cols = pl.ds(j * lanes, lanes)
                    acc = gs[0] * rb[slot][0][i, cols]
                    for k in range(1, TOP_K):
                        acc = acc + gs[k] * rb[slot][k][i, cols]
                    ob[slot][i, cols] = acc
            write(chunk, slot).start()

        for c in gathers(0, 0):
            c.start()

        @pl.loop(0, n_chunk // 2)
        def _(i):
            for c in gathers(2 * i + 1, 1):
                c.start()
            for c in gathers(2 * i, 0):
                c.wait()

            @pl.when(i > 0)
            def _():
                write(2 * i - 2, 0).wait()
            combine(2 * i, 0)

            @pl.when(2 * i + 2 < n_chunk)
            def _():
                for c in gathers(2 * i + 2, 0):
                    c.start()
            for c in gathers(2 * i + 1, 1):
                c.wait()

            @pl.when(i > 0)
            def _():
                write(2 * i - 1, 1).wait()
            combine(2 * i + 1, 1)

        write(n_chunk - 2, 0).wait()
        write(n_chunk - 1, 1).wait()

    scratch = ([pltpu.VMEM((per_w,), jnp.int32)] * TOP_K + [pltpu.VMEM((per_w * lanes,), F32)] * TOP_K
               + [pltpu.VMEM((w, d), F32)] * (2 * TOP_K + 2) + [pltpu.SemaphoreType.DMA] * 4)
    return pl.kernel(body, mesh=mesh, out_type=jax.ShapeDtypeStruct((n_tok, d), F32), scratch_types=scratch,
                     name="moe_sc_combine")(table, idx, gate_lanes)


def _combine_kernel(h_ref, p_ref, y_ref, gffn_ref, wple_ref, wpg_ref, gple_ref, o_ref):
    h2 = h_ref[...] + _rms(y_ref[...], gffn_ref[...])
    ple = (jnp.dot(p_ref[...].astype(BF16), wple_ref[...], preferred_element_type=F32)
           * jax.nn.sigmoid(jnp.dot(h2.astype(BF16), wpg_ref[...], preferred_element_type=F32)))
    o_ref[...] = h2 + _rms(ple, gple_ref[...])


def _combine_kernel_aliased(h_ref, p_ref, y_ref, gffn_ref, wple_ref, wpg_ref, gple_ref, prev_ref, o_ref):
    del prev_ref
    _combine_kernel(h_ref, p_ref, y_ref, gffn_ref, wple_ref, wpg_ref, gple_ref, o_ref)


def _combine(group, n_group, out_prev, h, p, y, g_ffn_post, w_ple_bf16, w_ple_gate_bf16, g_ple_post):
    t, d = h.shape
    tt = 256
    n_tg = t // n_group // tt
    const = lambda shape: pl.BlockSpec(shape, lambda i: (0, 0))
    tok = lambda w: pl.BlockSpec((tt, w), lambda i: (group * n_tg + i, 0))
    in_specs = [tok(d), tok(p.shape[1]), pl.BlockSpec((tt, d), lambda i: (i, 0)),
                const((1, d)), const(w_ple_bf16.shape), const(w_ple_gate_bf16.shape), const((1, d))]
    args = [h, p, y, g_ffn_post[None, :], w_ple_bf16, w_ple_gate_bf16, g_ple_post[None, :]]
    body, aliases = _combine_kernel, {}
    if out_prev is not None:
        in_specs.append(pl.BlockSpec(memory_space=pl.ANY))
        args.append(out_prev)
        body, aliases = _combine_kernel_aliased, {len(args) - 1: 0}
    return pl.pallas_call(
        body,
        grid=(n_tg,),
        in_specs=in_specs,
        out_specs=tok(d),
        out_shape=jax.ShapeDtypeStruct((t, d), F32),
        input_output_aliases=aliases,
        compiler_params=pltpu.CompilerParams(vmem_limit_bytes=VMEM_LIMIT),
        name="moe_combine_ple",
    )(*args)


def _layer(h, p, positions, g_mix_pre, w_in, w_dw, b_dw, g_conv_ln, b_conv_ln, w_out, g_mix_post, g_ffn_pre,
           w_router, b_router, w_gate_up, b_gate_up, w_down, b_down, g_ffn_post, w_ple, w_ple_gate, g_ple_post):
    b, s, d = h.shape
    t = b * s
    cos_t, sin_t = _rope_tables(positions)
    q, k, v, glu = _input_projection(h, cos_t, sin_t, g_mix_pre, w_in)
    attn = _dilated_attention(q, k, v)
    conv = _conformer_conv(glu, w_dw, b_dw, g_conv_ln, b_conv_ln)
    h1, u_ffn, idx, gates = _output_projection(attn, conv, h, w_out, g_mix_post, g_ffn_pre, w_router, b_router)
    cap = t * TOP_K + N_EXPERTS * EXPERT_ROWS
    n_blk = cap // EXPERT_ROWS
    dest, pend, pad, blk = _routing(idx, n_blk)
    blk_e = blk[0, :n_blk]
    nused = pend[N_EXPERTS - 1] // EXPERT_ROWS
    slot = _row_slots(dest, pad[:, 0], pend[:, 0], cap)
    tok_of_row = slot % t
    rows_c = cap // MOE_CHUNKS
    blk_c = n_blk // MOE_CHUNKS
    yb = None
    for c in range(MOE_CHUNKS):
        xb_c = _sc_gather_rows(u_ffn, tok_of_row[c * rows_c:(c + 1) * rows_c])
        nused_c = jnp.clip(nused - c * blk_c, 0, blk_c)
        yb = _experts(xb_c, c, MOE_CHUNKS, yb, blk_e[c * blk_c:(c + 1) * blk_c], nused_c,
                      w_gate_up, b_gate_up, w_down, b_down)
    t_g = t // MOE_GROUPS
    w_ple_bf16, w_ple_gate_bf16 = w_ple.astype(BF16), w_ple_gate.astype(BF16)
    h1_flat, p_flat = h1.reshape(t, d), p.reshape(t, -1)
    sc_lanes = 16
    by_group = lambda a: a.reshape(TOP_K, MOE_GROUPS, t_g).transpose(1, 0, 2)
    dest_g = by_group(dest).reshape(MOE_GROUPS, TOP_K * t_g)
    gate_g = jnp.broadcast_to(by_group(gates[:TOP_K])[..., None],
                              (MOE_GROUPS, TOP_K, t_g, sc_lanes)).reshape(MOE_GROUPS, -1)
    out = None
    for g in range(MOE_GROUPS):
        y_g = _sc_weighted_rows(yb, dest_g[g], gate_g[g], t_g)
        out = _combine(g, MOE_GROUPS, out, h1_flat, p_flat, y_g, g_ffn_post, w_ple_bf16, w_ple_gate_bf16,
                       g_ple_post)
    return out.reshape(b, s, d)


def kernel(x, p, positions, g_mix_pre, w_in, w_dw, b_dw, g_conv_ln, b_conv_ln, w_out, g_mix_post, g_ffn_pre,
           w_router, b_router, w_gate_up, b_gate_up, w_down, b_down, g_ffn_post, w_ple, w_ple_gate, g_ple_post):
    h = x
    for i in range(p.shape[0]):
        h = _layer(h, p[i], positions, g_mix_pre[i], w_in[i], w_dw[i], b_dw[i], g_conv_ln[i], b_conv_ln[i],
                   w_out[i], g_mix_post[i], g_ffn_pre[i], w_router[i], b_router[i], w_gate_up[i], b_gate_up[i],
                   w_down[i], b_down[i], g_ffn_post[i], w_ple[i], w_ple_gate[i], g_ple_post[i])
    return h
```

```python
import functools

import numpy as np
import jax
import jax.numpy as jnp
from jax import lax
from jax.experimental import pallas as pl
from jax.experimental.pallas import tpu as pltpu

F32 = jnp.float32
BF16 = jnp.bfloat16

HEAD_DIM = 64
N_HEADS = 12
ATTN_WIDTH = N_HEADS * HEAD_DIM
CONV_CHANNELS = 256
CONV_WIDTH = 31
ROPE_DIM = HEAD_DIM // 4
ROPE_THETA = 500000.0
N_EXPERTS = 32
TOP_K = 4
SWIGLU_LIMIT = 7.0
SWIGLU_ALPHA = 1.702
NORM_EPS = 1e-6
WINDOW = 128
N_PLANES = 16
SPAN = N_PLANES * WINDOW
LANES = 128
SUBLANES = 8
NEG = -1e30
EXPERT_ROWS = 256
VMEM_LIMIT = 56 * 1024 * 1024


def _residue_of_plane(p):
    return 4 * (p % 4) + p // 4


def _rms(xv, g):
    var = jnp.mean(xv * xv, axis=-1, keepdims=True)
    return xv * lax.rsqrt(var + NORM_EPS) * g


def _rope_kernel(pos_ref, invf_ref, expand_ref, one_ref, sgn_ref, c_ref, s_ref):
    tn = (((0,), (0,)), ((), ()))
    for p in range(N_PLANES):
        ang = invf_ref[...] * pos_ref[0, p:p + 1, :].astype(F32)
        spread = lambda t: lax.dot_general(t, expand_ref[...], tn, precision=lax.Precision.HIGHEST,
                                           preferred_element_type=F32)
        c_ref[0, p] = spread(jnp.cos(ang)) + one_ref[...]
        s_ref[0, p] = spread(jnp.sin(ang)) * sgn_ref[...]


def _rope_tables(positions):
    b, s = positions.shape
    sm = s // N_PLANES
    mt = min(sm, 128)
    plane_res = np.array([_residue_of_plane(p) for p in range(N_PLANES)])
    pos_planes = positions.reshape(b, sm, N_PLANES).transpose(0, 2, 1)[:, plane_res]
    half = ROPE_DIM // 2
    lane = np.arange(LANES) % HEAD_DIM
    inv_freq = ROPE_THETA ** (-jnp.arange(0, ROPE_DIM, 2, dtype=F32) / ROPE_DIM)
    rotary = lane < ROPE_DIM
    expand = jnp.asarray((np.arange(half)[:, None] == lane[None, :] % half) & rotary[None, :], F32)
    one = jnp.asarray(~rotary, F32)[None, :]
    sgn = jnp.asarray(np.where(lane < half, -1.0, 1.0), F32)[None, :]
    row = pl.BlockSpec((1, LANES), lambda i, j: (0, 0))
    out = pl.BlockSpec((1, N_PLANES, mt, LANES), lambda i, j: (i, 0, j, 0))
    return pl.pallas_call(
        _rope_kernel,
        grid=(b, sm // mt),
        in_specs=[pl.BlockSpec((1, N_PLANES, mt), lambda i, j: (i, 0, j)),
                  pl.BlockSpec((half, 1), lambda i, j: (0, 0)), pl.BlockSpec((half, LANES), lambda i, j: (0, 0)),
                  row, row],
        out_specs=[out, out],
        out_shape=[jax.ShapeDtypeStruct((b, N_PLANES, sm, LANES), F32)] * 2,
        name="rope_tables",
    )(pos_planes, inv_freq[:, None], expand, one, sgn)


PERM_TOKENS = 256
PERM_ROWS = PERM_TOKENS // N_PLANES


def _plane_permutation():
    perm = np.zeros((PERM_TOKENS, PERM_TOKENS), np.float32)
    for p in range(N_PLANES):
        for ml in range(PERM_ROWS):
            perm[PERM_ROWS * p + ml, N_PLANES * ml + _residue_of_plane(p)] = 1.0
    return perm


def _inproj_kernel(x_ref, c_ref, s_ref, g_ref, perm_ref, wqkv_ref, wc_ref, q_ref, k_ref, v_ref, glu_ref):
    g = g_ref[...]
    tt = x_ref.shape[1]
    lane = lax.broadcasted_iota(jnp.int32, (1, LANES), 1) % HEAD_DIM
    first_half = lane < ROPE_DIM // 2

    def rotary(t, cos, sin):
        outs = []
        for j in range(ATTN_WIDTH // LANES):
            tj = t[:, j * LANES:(j + 1) * LANES]
            partner = jnp.where(first_half, pltpu.roll(tj, LANES - ROPE_DIM // 2, 1),
                                pltpu.roll(tj, ROPE_DIM // 2, 1))
            outs.append(tj * cos + partner * sin)
        return jnp.concatenate(outs, axis=1)

    un = _rms(x_ref[0], g).astype(BF16)
    pc = jnp.dot(un, wc_ref[...], preferred_element_type=F32)
    glu_ref[0] = pc[:, :CONV_CHANNELS] * jax.nn.sigmoid(pc[:, CONV_CHANNELS:])

    for sub in range(tt // PERM_TOKENS):
        rows = slice(sub * PERM_ROWS, (sub + 1) * PERM_ROWS)
        u = jnp.dot(perm_ref[...], un[sub * PERM_TOKENS:(sub + 1) * PERM_TOKENS],
                    preferred_element_type=F32).astype(BF16)
        cos = jnp.concatenate([c_ref[0, p, rows, :] for p in range(N_PLANES)], axis=0)
        sin = jnp.concatenate([s_ref[0, p, rows, :] for p in range(N_PLANES)], axis=0)
        proj = jnp.dot(u, wqkv_ref[...], preferred_element_type=F32)
        q = (rotary(proj[:, :ATTN_WIDTH], cos, sin) * (HEAD_DIM ** -0.5)).astype(BF16)
        k = rotary(proj[:, ATTN_WIDTH:2 * ATTN_WIDTH], cos, sin).astype(BF16)
        v = proj[:, 2 * ATTN_WIDTH:].astype(BF16)
        for p in range(N_PLANES):
            chunk = slice(p * PERM_ROWS, (p + 1) * PERM_ROWS)
            q_ref[0, p, rows, :] = q[chunk]
            k_ref[0, p, rows, :] = k[chunk]
            v_ref[0, p, rows, :] = v[chunk]


def _input_projection(x, cos_t, sin_t, g_mix_pre, w_in):
    b, s, d = x.shape
    tt = 1024
    mc = tt // N_PLANES
    wqkv = w_in[:, :3 * ATTN_WIDTH].astype(BF16)
    wc = w_in[:, 3 * ATTN_WIDTH:].astype(BF16)
    perm = jnp.asarray(_plane_permutation(), BF16)
    plane = lambda w: pl.BlockSpec((1, N_PLANES, mc, w), lambda i, j: (i, 0, j, 0))
    plane_shape = jax.ShapeDtypeStruct((b, N_PLANES, s // N_PLANES, ATTN_WIDTH), BF16)
    tok = lambda w: pl.BlockSpec((1, tt, w), lambda i, j: (i, j, 0))
    const = lambda shape: pl.BlockSpec(shape, lambda i, j: (0, 0))
    return pl.pallas_call(
        _inproj_kernel,
        grid=(b, s // tt),
        in_specs=[tok(d), plane(LANES), plane(LANES), const((1, d)), const(perm.shape),
                  const(wqkv.shape), const(wc.shape)],
        out_specs=[plane(ATTN_WIDTH), plane(ATTN_WIDTH), plane(ATTN_WIDTH), tok(CONV_CHANNELS)],
        out_shape=[plane_shape, plane_shape, plane_shape,
                   jax.ShapeDtypeStruct((b, s, CONV_CHANNELS), F32)],
        compiler_params=pltpu.CompilerParams(vmem_limit_bytes=VMEM_LIMIT),
        name="input_projection",
    )(x, cos_t, sin_t, g_mix_pre[None, :], perm, wqkv, wc)


def _attention_biases():
    band = lambda j: np.where((j >= 0) & (j <= WINDOW), 0.0, NEG).astype(np.float32)
    cols = lambda m: np.where(m, NEG, 0.0).astype(np.float32)
    twice = lambda a: np.concatenate([a, a], axis=0)
    mq = np.arange(WINDOW)[:, None]
    kj = np.arange(2 * WINDOW)[None, :]
    j16 = mq + WINDOW - kj
    prev16 = kj < WINDOW
    row = np.arange(128)[:, None]
    col = np.arange(256)[None, :]
    j4 = 4 * (row % 32 - (col % 64 - 32)) + row // 32 - col // 64
    prev4 = col % 64 < 32
    row = np.arange(256)[:, None]
    col = np.arange(512)[None, :]
    res = np.vectorize(_residue_of_plane)
    j1 = 16 * (row % 16 - (col % 32 - 16)) + res(row // 16) - res(col // 32)
    prev1 = col % 32 < 16
    b1 = np.stack([twice(band(j1[:128])), twice(band(j1[128:]))])
    return [jnp.asarray(a) for a in (twice(band(j16)), twice(band(j4)), b1, cols(prev16), cols(prev4), cols(prev1))]


UNROLL = 8


def _attn_kernel(q_ref, kc_ref, kp_ref, vc_ref, vp_ref, b16_ref, b4_ref, b1_ref, p16_ref, p4_ref, p1_ref,
                 o_ref, m_scr, l_scr, a_scr, bias16_scr):
    no_prev = (pl.program_id(1) == 0).astype(F32)
    head0 = lax.broadcasted_iota(jnp.int32, (1, LANES), 1) < HEAD_DIM
    bias16_scr[...] = b16_ref[...] + no_prev * p16_ref[...]

    def tile(qt, kt, vt, bias):
        n = qt.shape[0]
        zero = jnp.zeros_like(qt)
        q2 = jnp.concatenate([jnp.where(head0, qt, zero), jnp.where(head0, zero, qt)], axis=0)
        s = lax.dot_general(q2, kt, (((1,), (1,)), ((), ())), preferred_element_type=F32) + bias
        m = jnp.max(s, axis=-1, keepdims=True)
        e = jnp.exp(s - m).astype(BF16)
        va = jnp.concatenate([vt, jnp.ones_like(vt)], axis=1)
        o = jnp.dot(e, va, preferred_element_type=F32)
        pick = lambda top, bot: jnp.where(head0, top, bot)
        mm = pick(jnp.broadcast_to(m[:n], (n, LANES)), jnp.broadcast_to(m[n:], (n, LANES)))
        return mm, pick(o[:n, LANES:], o[n:, LANES:]), pick(o[:n, :LANES], o[n:, :LANES])

    def put(branch, start, size, stats, off):
        for scr, val in zip((m_scr, l_scr, a_scr), stats):
            scr[branch, pl.ds(start, size), :] = val[off:off + size]

    def body16(i, carry):
        for p in [UNROLL * i + u for u in range(UNROLL)]:
            kt = jnp.concatenate([kp_ref[0, p], kc_ref[0, p]], axis=0)
            vt = jnp.concatenate([vp_ref[0, p], vc_ref[0, p]], axis=0)
            put(0, pl.multiple_of(p * WINDOW, WINDOW), WINDOW, tile(q_ref[0, p], kt, vt, bias16_scr[...]), 0)
        return carry
    lax.fori_loop(0, N_PLANES // UNROLL, body16, 0)

    def body4(g, carry):
        for c, i in [(2 * g + cc, ii) for cc in range(2) for ii in range(4)]:
            qt = jnp.concatenate([q_ref[0, 4 * c + a, 32 * i:32 * i + 32, :] for a in range(4)], axis=0)
            if i == 0:
                ks = [x for a in range(4) for x in (kp_ref[0, 4 * c + a, 96:128, :], kc_ref[0, 4 * c + a, 0:32, :])]
                vs = [x for a in range(4) for x in (vp_ref[0, 4 * c + a, 96:128, :], vc_ref[0, 4 * c + a, 0:32, :])]
                bias = b4_ref[...] + no_prev * p4_ref[...]
            else:
                ks = [kc_ref[0, 4 * c + a, 32 * i - 32:32 * i + 32, :] for a in range(4)]
                vs = [vc_ref[0, 4 * c + a, 32 * i - 32:32 * i + 32, :] for a in range(4)]
                bias = b4_ref[...]
            stats = tile(qt, jnp.concatenate(ks, axis=0), jnp.concatenate(vs, axis=0), bias)
            for a in range(4):
                put(1, pl.multiple_of((4 * c + a) * WINDOW + 32 * i, 32), 32, stats, 32 * a)
        return carry
    lax.fori_loop(0, 2, body4, 0)

    def tile1(i, first):
        rq = pl.ds(pl.multiple_of(16 * i, 16), 16)
        if first:
            ks = [x for p in range(N_PLANES) for x in (kp_ref[0, p, 112:128, :], kc_ref[0, p, 0:16, :])]
            vs = [x for p in range(N_PLANES) for x in (vp_ref[0, p, 112:128, :], vc_ref[0, p, 0:16, :])]
        else:
            rk = pl.ds(pl.multiple_of(16 * i - 16, 16), 32)
            ks = [kc_ref[0, p, rk, :] for p in range(N_PLANES)]
            vs = [vc_ref[0, p, rk, :] for p in range(N_PLANES)]
        kt = jnp.concatenate(ks, axis=0)
        vt = jnp.concatenate(vs, axis=0)
        for half in range(2):
            planes = range(8 * half, 8 * half + 8)
            qt = jnp.concatenate([q_ref[0, p, rq, :] for p in planes], axis=0)
            bias = b1_ref[half] + no_prev * p1_ref[...] if first else b1_ref[half]
            stats = tile(qt, kt, vt, bias)
            for p in planes:
                put(2, pl.multiple_of(p * WINDOW + 16 * i, 16), 16, stats, 16 * (p - 8 * half))

    tile1(0, True)
    tile1(1, False)

    def body1(g, carry):
        tile1(2 * g, False)
        tile1(2 * g + 1, False)
        return carry
    lax.fori_loop(1, WINDOW // 32, body1, 0)

    def combine(i, carry):
        for p in (2 * i, 2 * i + 1):
            rows = pl.ds(pl.multiple_of(p * WINDOW, WINDOW), WINDOW)
            ms = [m_scr[b, rows, :] for b in range(3)]
            mx = jnp.maximum(jnp.maximum(ms[0], ms[1]), ms[2])
            ws = [jnp.exp(m - mx) for m in ms]
            den = ws[0] * l_scr[0, rows, :] + ws[1] * l_scr[1, rows, :] + ws[2] * l_scr[2, rows, :]
            num = ws[0] * a_scr[0, rows, :] + ws[1] * a_scr[1, rows, :] + ws[2] * a_scr[2, rows, :]
            o_ref[0, p] = (num / den).astype(BF16)
        return carry
    lax.fori_loop(0, N_PLANES // 2, combine, 0)


def _dilated_attention(q, k, v):
    b, _, sm, _ = q.shape
    n_span = sm // WINDOW
    cur = pl.BlockSpec((1, N_PLANES, WINDOW, LANES), lambda i, j, h: (i, 0, j, h))
    prev = pl.BlockSpec((1, N_PLANES, WINDOW, LANES), lambda i, j, h: (i, 0, jnp.maximum(j - 1, 0), h))
    biases = _attention_biases()
    bias_specs = [pl.BlockSpec(a.shape, lambda i, j, h, nd=a.ndim: (0,) * nd) for a in biases]
    stats = pltpu.VMEM((3, SPAN, LANES), F32)
    return pl.pallas_call(
        _attn_kernel,
        grid=(b, n_span, ATTN_WIDTH // LANES),
        in_specs=[cur, cur, prev, cur, prev] + bias_specs,
        out_specs=cur,
        out_shape=jax.ShapeDtypeStruct(q.shape, BF16),
        scratch_shapes=[stats, stats, stats, pltpu.VMEM((2 * WINDOW, 2 * WINDOW), F32)],
        compiler_params=pltpu.CompilerParams(vmem_limit_bytes=VMEM_LIMIT),
        name="dilated_attention",
    )(q, k, k, v, v, *biases)


CONV_HALO = 32


def _conv_kernel(cur_ref, prev_ref, w_ref, b_ref, g_ref, bl_ref, o_ref, scr, *, chunk):
    tt = cur_ref.shape[1]
    has_prev = pl.program_id(1) > 0
    scr[0, 0:CONV_HALO, :] = jnp.where(has_prev, prev_ref[0], 0.0)
    scr[0, CONV_HALO:CONV_HALO + tt, :] = cur_ref[0]
    aligned_rows = tt + CONV_HALO - SUBLANES
    for s in range(1, SUBLANES):
        scr[s, 0:aligned_rows, :] = scr[0, s:s + aligned_rows, :]
    lead = CONV_HALO - (CONV_WIDTH - 1)
    for c0 in range(0, tt, chunk):
        acc = jnp.zeros((chunk, CONV_CHANNELS), F32)
        for j in range(CONV_WIDTH):
            s, a = (lead + j) % SUBLANES, (lead + j) // SUBLANES * SUBLANES
            acc = acc + w_ref[j:j + 1, :] * scr[s, c0 + a:c0 + a + chunk, :]
        y = acc + b_ref[...]
        mu = jnp.mean(y, axis=-1, keepdims=True)
        var = jnp.mean(jnp.square(y - mu), axis=-1, keepdims=True)
        yn = (y - mu) * lax.rsqrt(var + NORM_EPS) * g_ref[...] + bl_ref[...]
        o_ref[0, c0:c0 + chunk, :] = (yn * jax.nn.sigmoid(yn)).astype(BF16)


def _conformer_conv(glu, w_dw, b_dw, g_ln, b_ln):
    b, s, c = glu.shape
    tt = 512
    row = pl.BlockSpec((1, c), lambda i, j: (0, 0))
    return pl.pallas_call(
        functools.partial(_conv_kernel, chunk=128),
        grid=(b, s // tt),
        in_specs=[pl.BlockSpec((1, tt, c), lambda i, j: (i, j, 0)),
                  pl.BlockSpec((1, CONV_HALO, c), lambda i, j: (i, jnp.maximum(j * (tt // CONV_HALO) - 1, 0), 0)),
                  pl.BlockSpec((CONV_WIDTH, c), lambda i, j: (0, 0)), row, row, row],
        out_specs=pl.BlockSpec((1, tt, c), lambda i, j: (i, j, 0)),
        out_shape=jax.ShapeDtypeStruct((b, s, c), BF16),
        scratch_shapes=[pltpu.VMEM((SUBLANES, CONV_HALO + tt, c), F32)],
        name="conformer_conv",
    )(glu, glu, w_dw[:, 0, :], b_dw[None, :], g_ln[None, :], b_ln[None, :])


def _outproj_kernel(attn_ref, conv_ref, x_ref, permt_ref, woa_ref, woc_ref, gpost_ref, gffn_ref, wr_ref, br_ref,
                    h_ref, u_ref, idx_ref, gate_ref):
    tt = x_ref.shape[1]
    nat = []
    for sub in range(tt // PERM_TOKENS):
        rows = slice(sub * PERM_ROWS, (sub + 1) * PERM_ROWS)
        a = jnp.concatenate([attn_ref[0, p, rows, :] for p in range(N_PLANES)], axis=0)
        nat.append(jnp.dot(permt_ref[...], a, preferred_element_type=F32).astype(BF16))
    mix = (jnp.dot(jnp.concatenate(nat, axis=0), woa_ref[...], preferred_element_type=F32)
           + jnp.dot(conv_ref[0], woc_ref[...], preferred_element_type=F32))
    h = x_ref[0] + _rms(mix, gpost_ref[...])
    h_ref[0] = h
    u = _rms(h, gffn_ref[...])
    half = u.shape[1] // 2
    ub = u.astype(BF16).astype(F32)
    u_ref[...] = ((lax.bitcast_convert_type(ub[:, :half], jnp.uint32) >> 16)
                  | (lax.bitcast_convert_type(ub[:, half:], jnp.uint32) & jnp.uint32(0xFFFF0000)))
    u_hi = u.astype(BF16)
    u_lo = (u - u_hi.astype(F32)).astype(BF16)
    nt = (((1,), (1,)), ((), ()))
    by_hi = lax.dot_general(wr_ref[...], u_hi, nt, preferred_element_type=F32)
    by_lo = lax.dot_general(wr_ref[:N_EXPERTS], u_lo, nt, preferred_element_type=F32)
    logits = by_hi[:N_EXPERTS] + (by_hi[N_EXPERTS:] + by_lo) + br_ref[...]
    rows = lax.broadcasted_iota(jnp.int32, logits.shape, 0)
    vals = logits
    tops, idxs = [], []
    for _ in range(TOP_K):
        mx = jnp.max(vals, axis=0, keepdims=True)
        ix = jnp.min(jnp.where(vals == mx, rows, N_EXPERTS), axis=0, keepdims=True)
        tops.append(mx)
        idxs.append(ix)
        vals = jnp.where(rows == ix, -jnp.inf, vals)
    ex = [jnp.exp(t - tops[0]) for t in tops]
    den = ex[0] + ex[1] + ex[2] + ex[3]
    idx_ref[...] = jnp.concatenate(idxs, axis=0)
    gate_ref[...] = jnp.concatenate([e / den for e in ex] + [jnp.zeros((8 - TOP_K, tt), F32)], axis=0)


def _output_projection(attn, conv, x, w_out, g_mix_post, g_ffn_pre, w_router, b_router):
    b, s, d = x.shape
    tt = 512
    mc = tt // N_PLANES
    n_t = s // tt
    woa = w_out[:ATTN_WIDTH].astype(BF16)
    woc = w_out[ATTN_WIDTH:].astype(BF16)
    wr_hi = w_router.T.astype(BF16)
    wr_split = jnp.concatenate([wr_hi, (w_router.T - wr_hi.astype(F32)).astype(BF16)], axis=0)
    permt =jnp.asarray(_plane_permutation().T, BF16)
    const = lambda shape: pl.BlockSpec(shape, lambda i, j: (0, 0))
    flat = lambda w: pl.BlockSpec((tt, w), lambda i, j: (i * n_t + j, 0))
    lanes = lambda r: pl.BlockSpec((r, tt), lambda i, j: (0, i * n_t + j))
    return pl.pallas_call(
        _outproj_kernel,
        grid=(b, n_t),
        in_specs=[pl.BlockSpec((1, N_PLANES, mc, ATTN_WIDTH), lambda i, j: (i, 0, j, 0)),
                  pl.BlockSpec((1, tt, CONV_CHANNELS), lambda i, j: (i, j, 0)),
                  pl.BlockSpec((1, tt, d), lambda i, j: (i, j, 0)),
                  const(permt.shape), const(woa.shape), const(woc.shape), const((1, d)), const((1, d)),
                  const((2 * N_EXPERTS, d)), const((N_EXPERTS, 1))],
        out_specs=[pl.BlockSpec((1, tt, d), lambda i, j: (i, j, 0)), flat(d // 2), lanes(TOP_K), lanes(8)],
        out_shape=[jax.ShapeDtypeStruct((b, s, d), F32),
                   jax.ShapeDtypeStruct((b * s, d // 2), jnp.uint32),
                   jax.ShapeDtypeStruct((TOP_K, b * s), jnp.int32),
                   jax.ShapeDtypeStruct((8, b * s), F32)],
        compiler_params=pltpu.CompilerParams(vmem_limit_bytes=VMEM_LIMIT),
        name="output_projection_router",
    )(attn, conv, x, permt, woa, woc, g_mix_post[None, :], g_ffn_pre[None, :], wr_split, b_router[:, None])


def _route_kernel(idx_ref, dest_ref, pend_ref, pad_ref, blk_ref, carry, pstart):
    phase = pl.program_id(0)
    step = pl.program_id(1)
    tt = idx_ref.shape[1]
    rows = lax.broadcasted_iota(jnp.int32, (N_EXPERTS, tt), 0)
    hot = [rows == idx_ref[k:k + 1, :] for k in range(TOP_K)]
    memb = sum(h.astype(F32) for h in hot)

    @pl.when((phase == 0) & (step == 0))
    def _():
        carry[...] = jnp.zeros_like(carry)

    @pl.when((phase == 1) & (step == 0))
    def _():
        counts = carry[...]
        padded = jnp.floor((counts + (EXPERT_ROWS - 1)) * (1.0 / EXPERT_ROWS)) * EXPERT_ROWS
        tri = (lax.broadcasted_iota(jnp.int32, (N_EXPERTS, N_EXPERTS), 1)
               <= lax.broadcasted_iota(jnp.int32, (N_EXPERTS, N_EXPERTS), 0)).astype(F32)
        pend = jnp.dot(tri, padded, precision=lax.Precision.HIGHEST, preferred_element_type=F32)
        pstart[...] = pend - padded
        pend_ref[...] = pend.astype(jnp.int32)
        pad_ref[...] = (pend - padded + counts).astype(jnp.int32)
        starts = lax.broadcasted_iota(jnp.int32, (N_EXPERTS, blk_ref.shape[1]), 1) * EXPERT_ROWS
        ended = (pend.astype(jnp.int32) <= starts).astype(jnp.int32)
        blk_ref[...] = jnp.minimum(jnp.sum(ended, axis=0, keepdims=True), N_EXPERTS - 1)
        carry[...] = jnp.zeros_like(carry)

    @pl.when(phase == 1)
    def _():
        earlier = (lax.broadcasted_iota(jnp.int32, (tt, tt), 0)
                   < lax.broadcasted_iota(jnp.int32, (tt, tt), 1)).astype(BF16)
        row = jnp.dot(memb.astype(BF16), earlier, preferred_element_type=F32) + (carry[...] + pstart[...])
        dest_ref[...] = jnp.concatenate(
            [jnp.sum(jnp.where(h, row, 0.0), axis=0, keepdims=True) for h in hot], axis=0).astype(jnp.int32)

    carry[...] = carry[...] + jnp.sum(memb, axis=1, keepdims=True)


def _routing(idx, n_blk):
    _, t = idx.shape
    tt = 512
    blk_lanes = -(-n_blk // LANES) * LANES
    return pl.pallas_call(
        _route_kernel,
        grid=(2, t // tt),
        in_specs=[pl.BlockSpec((TOP_K, tt), lambda ph, i: (0, i))],
        out_specs=[pl.BlockSpec((TOP_K, tt), lambda ph, i: (0, i * ph)),
                   pl.BlockSpec((N_EXPERTS, 1), lambda ph, i: (0, 0)),
                   pl.BlockSpec((N_EXPERTS, 1), lambda ph, i: (0, 0)),
                   pl.BlockSpec((1, blk_lanes), lambda ph, i: (0, 0))],
        out_shape=[jax.ShapeDtypeStruct((TOP_K, t), jnp.int32), jax.ShapeDtypeStruct((N_EXPERTS, 1), jnp.int32),
                   jax.ShapeDtypeStruct((N_EXPERTS, 1), jnp.int32), jax.ShapeDtypeStruct((1, blk_lanes), jnp.int32)],
        scratch_shapes=[pltpu.VMEM((N_EXPERTS, 1), F32), pltpu.VMEM((N_EXPERTS, 1), F32)],
        compiler_params=pltpu.CompilerParams(dimension_semantics=("arbitrary", "arbitrary")),
        name="moe_routing",
    )(idx)


def _slot_kernel(pad_ref, pend_ref, d0_ref, d1_ref, d2_ref, d3_ref, slot_ref, *, n_tok):
    i = pl.program_id(0)
    dest_refs = (d0_ref, d1_ref, d2_ref, d3_ref)
    tt = d0_ref.shape[0]

    @pl.when(i == 0)
    def _():
        def fill(r, c):
            slot_ref[r] = TOP_K * n_tok + lax.rem(r, EXPERT_ROWS)
            return c

        def per_expert(e, carry):
            return lax.fori_loop(pad_ref[e], pend_ref[e], fill, carry)
        lax.fori_loop(0, N_EXPERTS, per_expert, 0)
        lax.fori_loop(pend_ref[N_EXPERTS - 1], slot_ref.shape[0], fill, 0)

    unroll = 16

    def body(g, carry):
        tl0 = g * unroll
        first = [k * n_tok + i * tt + tl0 for k in range(TOP_K)]
        for u in range(unroll):
            for k in range(TOP_K):
                slot_ref[dest_refs[k][tl0 + u]] = first[k] + u
        return carry
    lax.fori_loop(0, tt // unroll, body, 0)


def _row_slots(dest, pad, pend, cap):
    _, t = dest.shape
    tt = 1024
    n_t = t // tt
    smem = pl.BlockSpec(memory_space=pltpu.SMEM)
    choice = lambda k: pl.BlockSpec((tt,), lambda i: (k * n_t + i,), memory_space=pltpu.SMEM)
    flat = dest.reshape(TOP_K * t)
    return pl.pallas_call(
        functools.partial(_slot_kernel, n_tok=t),
        grid=(n_t,),
        in_specs=[smem, smem] + [choice(k) for k in range(TOP_K)],
        out_specs=smem,
        out_shape=jax.ShapeDtypeStruct((cap,), jnp.int32),
        compiler_params=pltpu.CompilerParams(dimension_semantics=("arbitrary",)),
        name="moe_row_slots",
    )(pad, pend, flat, flat, flat, flat)


def _expert_kernel(blk_e_ref, nused_ref, xb_ref, wgu_ref, bgu_ref, wd_ref, bd_ref, *rest):
    yb_ref, wgu_b, wd_b = rest[-3:]
    j = pl.program_id(0)
    nused = nused_ref[0]
    d_ff = wd_ref.shape[1]

    new_expert = (j == 0) | (blk_e_ref[j] != blk_e_ref[jnp.maximum(j - 1, 0)])

    @pl.when((j < nused) & new_expert)
    def _():
        def cast(src, dst):
            def body(i, carry):
                rows = pl.ds(pl.multiple_of(i * LANES, LANES), LANES)
                dst[rows, :] = src[0, rows, :].astype(BF16)
                return carry
            lax.fori_loop(0, dst.shape[0] // LANES, body, 0)
        cast(wgu_ref, wgu_b)
        cast(wd_ref, wd_b)

    @pl.when(j < nused)
    def _():
        word = xb_ref[...]
        half = word.shape[1]
        lo = lax.bitcast_convert_type(word << 16, F32).astype(BF16)
        hi = lax.bitcast_convert_type(word & jnp.uint32(0xFFFF0000), F32).astype(BF16)
        gu = (jnp.dot(lo, wgu_b[:half, :], preferred_element_type=F32)
              + jnp.dot(hi, wgu_b[half:, :], preferred_element_type=F32) + bgu_ref[0])
        gate = jnp.minimum(gu[:, :d_ff], SWIGLU_LIMIT)
        up = jnp.clip(gu[:, d_ff:], -SWIGLU_LIMIT, SWIGLU_LIMIT)
        hid = (up + 1.0) * (gate * jax.nn.sigmoid(gate * SWIGLU_ALPHA))
        yb_ref[...] = jnp.dot(hid.astype(BF16), wd_b[...], preferred_element_type=F32) + bd_ref[0]

    @pl.when(j >= nused)
    def _():
        yb_ref[...] = jnp.zeros_like(yb_ref)


def _experts(xb, chunk, n_chunk, yb_prev, blk_e, nused, w_gate_up, b_gate_up, w_down, b_down):
    rows, d_in = xb.shape
    e, d, ff2 = w_gate_up.shape
    d_ff = w_down.shape[1]
    n_blk = rows // EXPERT_ROWS
    live = lambda j, nu: jnp.maximum(jnp.minimum(j, nu[0] - 1), 0)
    expert = lambda j, be, nu: (be[live(j, nu)], 0, 0)
    in_specs = [pl.BlockSpec((EXPERT_ROWS, d_in), lambda j, be, nu: (live(j, nu), 0)),
                pl.BlockSpec((1, d, ff2), expert), pl.BlockSpec((1, 1, ff2), expert),
                pl.BlockSpec((1, d_ff, d), expert), pl.BlockSpec((1, 1, d), expert)]
    args = [blk_e, nused, xb, w_gate_up, b_gate_up[:, None, :], w_down, b_down[:, None, :]]
    aliases = {}
    if yb_prev is not None:
        in_specs.append(pl.BlockSpec(memory_space=pl.ANY))
        args.append(yb_prev)
        aliases = {len(args) - 1: 0}
    grid_spec = pltpu.PrefetchScalarGridSpec(
        num_scalar_prefetch=2,
        grid=(n_blk,),
        in_specs=in_specs,
        out_specs=pl.BlockSpec((EXPERT_ROWS, d), lambda j, be, nu: (chunk * n_blk + j, 0)),
        scratch_shapes=[pltpu.VMEM((d, ff2), BF16), pltpu.VMEM((d_ff, d), BF16)],
    )
    return pl.pallas_call(
        _expert_kernel,
        grid_spec=grid_spec,
        out_shape=jax.ShapeDtypeStruct((n_chunk * rows, d), F32),
        input_output_aliases=aliases,
        compiler_params=pltpu.CompilerParams(dimension_semantics=("arbitrary",), vmem_limit_bytes=VMEM_LIMIT),
        name="moe_experts",
    )(*args)


SC_BUFFER_BYTES = 128 * 1024
MOE_CHUNKS = 8
MOE_GROUPS = 4


def _sc_gather_rows(table, idx, part, n_part):
    from jax.experimental.pallas import tpu_sc as plsc
    n_rows = idx.shape[0] // n_part
    d = table.shape[1]
    info = plsc.get_sparse_core_info()
    n_core, n_sub = info.num_cores, info.num_subcores
    per_w = n_rows // (n_core * n_sub)
    fit = SC_BUFFER_BYTES // (d * table.dtype.itemsize)
    g_rows = max(r for r in (8, 16, 32, 64, 128) if r <= fit and per_w % r == 0)
    n_chunk = per_w // g_rows
    assert per_w * n_core * n_sub == n_rows
    mesh = plsc.VectorSubcoreMesh(core_axis_name="c", subcore_axis_name="s")

    def body(table_hbm, idx_hbm, out_hbm, idx_v, buf0, buf1, sem0, sem1):
        base = (lax.axis_index("s") * n_core + lax.axis_index("c")) * per_w
        pltpu.sync_copy(idx_hbm.at[pl.ds(pl.multiple_of(part * n_rows + base, 8), per_w)], idx_v)

        def gather(chunk, buf, sem):
            rows = idx_v.at[pl.ds(pl.multiple_of(chunk * g_rows, 8), g_rows)]
            return pltpu.make_async_copy(table_hbm.at[rows], buf, sem)

        def write(chunk, buf):
            pltpu.sync_copy(buf, out_hbm.at[pl.ds(pl.multiple_of(base + chunk * g_rows, 8), g_rows)])

        gather(0, buf0, sem0).start()

        @pl.loop(0, n_chunk // 2)
        def _(i):
            gather(2 * i + 1, buf1, sem1).start()
            gather(2 * i, buf0, sem0).wait()
            write(2 * i, buf0)

            @pl.when(2 * i + 2 < n_chunk)
            def _():
                gather(2 * i + 2, buf0, sem0).start()
            gather(2 * i + 1, buf1, sem1).wait()
            write(2 * i + 1, buf1)

        if n_chunk % 2:
            gather(n_chunk - 1, buf0, sem0).wait()
            write(n_chunk - 1, buf0)

    return pl.kernel(
        body, mesh=mesh, out_type=jax.ShapeDtypeStruct((n_rows, d), table.dtype),
        scratch_types=[pltpu.VMEM((per_w,), jnp.int32), pltpu.VMEM((g_rows, d), table.dtype),
                       pltpu.VMEM((g_rows, d), table.dtype), pltpu.SemaphoreType.DMA, pltpu.SemaphoreType.DMA],
        name="moe_sc_gather",
    )(table, idx)


def _sc_weighted_rows(table, idx, gates, group, n_group):
    from jax.experimental.pallas import tpu_sc as plsc
    d = table.shape[1]
    info = plsc.get_sparse_core_info()
    n_core, n_sub, lanes = info.num_cores, info.num_subcores, info.num_lanes
    n_all = idx.shape[0] // TOP_K
    n_tok = n_all // n_group
    per_w = n_tok // (n_core * n_sub)
    w = 8
    n_chunk = per_w // w
    assert per_w * n_core * n_sub == n_tok and n_chunk * w == per_w and n_chunk % 2 == 0
    mesh = plsc.VectorSubcoreMesh(core_axis_name="c", subcore_axis_name="s")

    def body(table_hbm, idx_hbm, g_hbm, out_hbm, *scr):
        idx_v, g_v, rb, ob = scr[0:4], scr[4:8], (scr[8:12], scr[12:16]), scr[16:18]
        sems, wsems = scr[18:20], scr[20:22]
        base = (lax.axis_index("s") * n_core + lax.axis_index("c")) * per_w
        for k in range(TOP_K):
            off = pl.multiple_of(k * n_all + group * n_tok + base, 8)
            pltpu.sync_copy(idx_hbm.at[pl.ds(off, per_w)], idx_v[k])
            pltpu.sync_copy(g_hbm.at[pl.ds(off, per_w)], g_v[k])

        def gathers(chunk, slot):
            rows = pl.ds(pl.multiple_of(chunk * w, 8), w)
            return [pltpu.make_async_copy(table_hbm.at[idx_v[k].at[rows]], rb[slot][k], sems[slot])
                    for k in range(TOP_K)]

        def write(chunk, slot):
            return pltpu.make_async_copy(ob[slot], out_hbm.at[pl.ds(pl.multiple_of(base + chunk * w, 8), w)],
                                         wsems[slot])

        def combine(chunk, slot):
            @pl.loop(0, w)
            def _(i):
                token = jnp.full((lanes,), chunk * w + i, jnp.int32)
                gs = [plsc.load_gather(g_v[k], [token]) for k in range(TOP_K)]
                for j in range(d // lanes):
                    cols = pl.ds(j * lanes, lanes)
                    acc = gs[0] * rb[slot][0][i, cols]
                    for k in range(1, TOP_K):
                        acc = acc + gs[k] * rb[slot][k][i, cols]
                    ob[slot][i, cols] = acc
            write(chunk, slot).start()

        for c in gathers(0, 0):
            c.start()

        @pl.loop(0, n_chunk // 2)
        def _(i):
            for c in gathers(2 * i + 1, 1):
                c.start()
            for c in gathers(2 * i, 0):
                c.wait()

            @pl.when(i > 0)
            def _():
                write(2 * i - 2, 0).wait()
            combine(2 * i, 0)

            @pl.when(2 * i + 2 < n_chunk)
            def _():
                for c in gathers(2 * i + 2, 0):
                    c.start()
            for c in gathers(2 * i + 1, 1):
                c.wait()

            @pl.when(i > 0)
            def _():
                write(2 * i - 1, 1).wait()
            combine(2 * i + 1, 1)

        write(n_chunk - 2, 0).wait()
        write(n_chunk - 1, 1).wait()

    scratch = ([pltpu.VMEM((per_w,), jnp.int32)] * TOP_K + [pltpu.VMEM((per_w,), F32)] * TOP_K
               + [pltpu.VMEM((w, d), F32)] * (2 * TOP_K + 2) + [pltpu.SemaphoreType.DMA] * 4)
    return pl.kernel(body, mesh=mesh, out_type=jax.ShapeDtypeStruct((n_tok, d), F32), scratch_types=scratch,
                     compiler_params=pltpu.CompilerParams(needs_layout_passes=False),
                     name="moe_sc_combine")(table, idx, gates)


def _combine_kernel(h_ref, p_ref, y_ref, gffn_ref, wple_ref, wpg_ref, gple_ref, o_ref):
    h2 = h_ref[...] + _rms(y_ref[...], gffn_ref[...])
    ple = (jnp.dot(p_ref[...].astype(BF16), wple_ref[...], preferred_element_type=F32)
           * jax.nn.sigmoid(jnp.dot(h2.astype(BF16), wpg_ref[...], preferred_element_type=F32)))
    o_ref[...] = h2 + _rms(ple, gple_ref[...])


def _combine_kernel_aliased(h_ref, p_ref, y_ref, gffn_ref, wple_ref, wpg_ref, gple_ref, prev_ref, o_ref):
    del prev_ref
    _combine_kernel(h_ref, p_ref, y_ref, gffn_ref, wple_ref, wpg_ref, gple_ref, o_ref)


def _combine(group, n_group, out_prev, h, p, y, g_ffn_post, w_ple_bf16, w_ple_gate_bf16, g_ple_post):
    t, d = h.shape
    tt = 256
    n_tg = t // n_group // tt
    const = lambda shape: pl.BlockSpec(shape, lambda i: (0, 0))
    tok = lambda w: pl.BlockSpec((tt, w), lambda i: (group * n_tg + i, 0))
    in_specs = [tok(d), tok(p.shape[1]), pl.BlockSpec((tt, d), lambda i: (i, 0)),
                const((1, d)), const(w_ple_bf16.shape), const(w_ple_gate_bf16.shape), const((1, d))]
    args = [h, p, y, g_ffn_post[None, :], w_ple_bf16, w_ple_gate_bf16, g_ple_post[None, :]]
    body, aliases = _combine_kernel, {}
    if out_prev is not None:
        in_specs.append(pl.BlockSpec(memory_space=pl.ANY))
        args.append(out_prev)
        body, aliases = _combine_kernel_aliased, {len(args) - 1: 0}
    return pl.pallas_call(
        body,
        grid=(n_tg,),
        in_specs=in_specs,
        out_specs=tok(d),
        out_shape=jax.ShapeDtypeStruct((t, d), F32),
        input_output_aliases=aliases,
        compiler_params=pltpu.CompilerParams(vmem_limit_bytes=VMEM_LIMIT),
        name="moe_combine_ple",
    )(*args)


def _layer(h, p, positions, g_mix_pre, w_in, w_dw, b_dw, g_conv_ln, b_conv_ln, w_out, g_mix_post, g_ffn_pre,
           w_router, b_router, w_gate_up, b_gate_up, w_down, b_down, g_ffn_post, w_ple, w_ple_gate, g_ple_post):
    b, s, d = h.shape
    t = b * s
    cos_t, sin_t = _rope_tables(positions)
    q, k, v, glu = _input_projection(h, cos_t, sin_t, g_mix_pre, w_in)
    attn = _dilated_attention(q, k, v)
    conv = _conformer_conv(glu, w_dw, b_dw, g_conv_ln, b_conv_ln)
    h1, u_ffn, idx, gates = _output_projection(attn, conv, h, w_out, g_mix_post, g_ffn_pre, w_router, b_router)
    cap = t * TOP_K + N_EXPERTS * EXPERT_ROWS
    n_blk = cap // EXPERT_ROWS
    dest, pend, pad, blk = _routing(idx, n_blk)
    blk_e = blk[0, :n_blk]
    nused = pend[N_EXPERTS - 1] // EXPERT_ROWS
    slot = _row_slots(dest, pad[:, 0], pend[:, 0], cap)
    tok_of_row = slot % t
    rows_c = cap // MOE_CHUNKS
    blk_c = n_blk // MOE_CHUNKS
    yb = None
    for c in range(MOE_CHUNKS):
        xb_c = _sc_gather_rows(u_ffn, tok_of_row, c, MOE_CHUNKS)
        nused_c = jnp.clip(nused - c * blk_c, 0, blk_c)
        yb = _experts(xb_c, c, MOE_CHUNKS, yb, blk_e[c * blk_c:(c + 1) * blk_c], nused_c,
                      w_gate_up, b_gate_up, w_down, b_down)
    t_g = t // MOE_GROUPS
    w_ple_bf16, w_ple_gate_bf16 = w_ple.astype(BF16), w_ple_gate.astype(BF16)
    h1_flat, p_flat = h1.reshape(t, d), p.reshape(t, -1)
    dest_flat, gate_flat = dest.reshape(TOP_K * t), gates[:TOP_K].reshape(TOP_K * t)
    out = None
    for g in range(MOE_GROUPS):
        y_g = _sc_weighted_rows(yb, dest_flat, gate_flat, g, MOE_GROUPS)
        out = _combine(g, MOE_GROUPS, out, h1_flat, p_flat, y_g, g_ffn_post, w_ple_bf16, w_ple_gate_bf16,
                       g_ple_post)
    return out.reshape(b, s, d)


def kernel(x, p, positions, g_mix_pre, w_in, w_dw, b_dw, g_conv_ln, b_conv_ln, w_out, g_mix_post, g_ffn_pre,
           w_router, b_router, w_gate_up, b_gate_up, w_down, b_down, g_ffn_post, w_ple, w_ple_gate, g_ple_post):
    h = x
    for i in range(p.shape[0]):
        h = _layer(h, p[i], positions, g_mix_pre[i], w_in[i], w_dw[i], b_dw[i], g_conv_ln[i], b_conv_ln[i],
                   w_out[i], g_mix_post[i], g_ffn_pre[i], w_router[i], b_router[i], w_gate_up[i], b_gate_up[i],
                   w_down[i], b_down[i], g_ffn_post[i], w_ple[i], w_ple_gate[i], g_ple_post[i])
    return h
```

```python
import functools

import numpy as np
import jax
import jax.numpy as jnp
from jax import lax
from jax.experimental import pallas as pl
from jax.experimental.pallas import tpu as pltpu

F32 = jnp.float32
BF16 = jnp.bfloat16

HEAD_DIM = 64
N_HEADS = 12
ATTN_WIDTH = N_HEADS * HEAD_DIM
CONV_CHANNELS = 256
CONV_WIDTH = 31
ROPE_DIM = HEAD_DIM // 4
ROPE_THETA = 500000.0
N_EXPERTS = 32
TOP_K = 4
SWIGLU_LIMIT = 7.0
SWIGLU_ALPHA = 1.702
NORM_EPS = 1e-6
WINDOW = 128
N_PLANES = 16
SPAN = N_PLANES * WINDOW
LANES = 128
SUBLANES = 8
NEG = -1e30
EXPERT_ROWS = 256
VMEM_LIMIT = 56 * 1024 * 1024


def _residue_of_plane(p):
    return 4 * (p % 4) + p // 4


def _rms(xv, g):
    var = jnp.mean(xv * xv, axis=-1, keepdims=True)
    return xv * lax.rsqrt(var + NORM_EPS) * g


def _rope_kernel(pos_ref, invf_ref, expand_ref, one_ref, sgn_ref, c_ref, s_ref):
    tn = (((0,), (0,)), ((), ()))
    for p in range(N_PLANES):
        ang = invf_ref[...] * pos_ref[0, p:p + 1, :].astype(F32)
        spread = lambda t: lax.dot_general(t, expand_ref[...], tn, precision=lax.Precision.HIGHEST,
                                           preferred_element_type=F32)
        c_ref[0, p] = spread(jnp.cos(ang)) + one_ref[...]
        s_ref[0, p] = spread(jnp.sin(ang)) * sgn_ref[...]


def _rope_tables(positions):
    b, s = positions.shape
    sm = s // N_PLANES
    mt = min(sm, 128)
    plane_res = np.array([_residue_of_plane(p) for p in range(N_PLANES)])
    pos_planes = positions.reshape(b, sm, N_PLANES).transpose(0, 2, 1)[:, plane_res]
    half = ROPE_DIM // 2
    lane = np.arange(LANES) % HEAD_DIM
    inv_freq = ROPE_THETA ** (-jnp.arange(0, ROPE_DIM, 2, dtype=F32) / ROPE_DIM)
    rotary = lane < ROPE_DIM
    expand = jnp.asarray((np.arange(half)[:, None] == lane[None, :] % half) & rotary[None, :], F32)
    one = jnp.asarray(~rotary, F32)[None, :]
    sgn = jnp.asarray(np.where(lane < half, -1.0, 1.0), F32)[None, :]
    row = pl.BlockSpec((1, LANES), lambda i, j: (0, 0))
    out = pl.BlockSpec((1, N_PLANES, mt, LANES), lambda i, j: (i, 0, j, 0))
    return pl.pallas_call(
        _rope_kernel,
        grid=(b, sm // mt),
        in_specs=[pl.BlockSpec((1, N_PLANES, mt), lambda i, j: (i, 0, j)),
                  pl.BlockSpec((half, 1), lambda i, j: (0, 0)), pl.BlockSpec((half, LANES), lambda i, j: (0, 0)),
                  row, row],
        out_specs=[out, out],
        out_shape=[jax.ShapeDtypeStruct((b, N_PLANES, sm, LANES), F32)] * 2,
        name="rope_tables",
    )(pos_planes, inv_freq[:, None], expand, one, sgn)


PERM_TOKENS = 256
PERM_ROWS = PERM_TOKENS // N_PLANES


def _plane_permutation():
    perm = np.zeros((PERM_TOKENS, PERM_TOKENS), np.float32)
    for p in range(N_PLANES):
        for ml in range(PERM_ROWS):
            perm[PERM_ROWS * p + ml, N_PLANES * ml + _residue_of_plane(p)] = 1.0
    return perm


def _inproj_kernel(x_ref, c_ref, s_ref, g_ref, perm_ref, wqkv_ref, wc_ref, q_ref, k_ref, v_ref, glu_ref):
    g = g_ref[...]
    tt = x_ref.shape[1]
    lane = lax.broadcasted_iota(jnp.int32, (1, LANES), 1) % HEAD_DIM
    first_half = lane < ROPE_DIM // 2

    def rotary(t, cos, sin):
        outs = []
        for j in range(ATTN_WIDTH // LANES):
            tj = t[:, j * LANES:(j + 1) * LANES]
            partner = jnp.where(first_half, pltpu.roll(tj, LANES - ROPE_DIM // 2, 1),
                                pltpu.roll(tj, ROPE_DIM // 2, 1))
            outs.append(tj * cos + partner * sin)
        return jnp.concatenate(outs, axis=1)

    un = _rms(x_ref[0], g).astype(BF16)
    pc = jnp.dot(un, wc_ref[...], preferred_element_type=F32)
    glu_ref[0] = pc[:, :CONV_CHANNELS] * jax.nn.sigmoid(pc[:, CONV_CHANNELS:])

    for sub in range(tt // PERM_TOKENS):
        rows = slice(sub * PERM_ROWS, (sub + 1) * PERM_ROWS)
        u = jnp.dot(perm_ref[...], un[sub * PERM_TOKENS:(sub + 1) * PERM_TOKENS],
                    preferred_element_type=F32).astype(BF16)
        cos = jnp.concatenate([c_ref[0, p, rows, :] for p in range(N_PLANES)], axis=0)
        sin = jnp.concatenate([s_ref[0, p, rows, :] for p in range(N_PLANES)], axis=0)
        proj = jnp.dot(u, wqkv_ref[...], preferred_element_type=F32)
        q = (rotary(proj[:, :ATTN_WIDTH], cos, sin) * (HEAD_DIM ** -0.5)).astype(BF16)
        k = rotary(proj[:, ATTN_WIDTH:2 * ATTN_WIDTH], cos, sin).astype(BF16)
        v = proj[:, 2 * ATTN_WIDTH:].astype(BF16)
        for p in range(N_PLANES):
            chunk = slice(p * PERM_ROWS, (p + 1) * PERM_ROWS)
            q_ref[0, p, rows, :] = q[chunk]
            k_ref[0, p, rows, :] = k[chunk]
            v_ref[0, p, rows, :] = v[chunk]


def _input_projection(x, cos_t, sin_t, g_mix_pre, w_in):
    b, s, d = x.shape
    tt = 1024
    mc = tt // N_PLANES
    wqkv = w_in[:, :3 * ATTN_WIDTH].astype(BF16)
    wc = w_in[:, 3 * ATTN_WIDTH:].astype(BF16)
    perm = jnp.asarray(_plane_permutation(), BF16)
    plane = lambda w: pl.BlockSpec((1, N_PLANES, mc, w), lambda i, j: (i, 0, j, 0))
    plane_shape = jax.ShapeDtypeStruct((b, N_PLANES, s // N_PLANES, ATTN_WIDTH), BF16)
    tok = lambda w: pl.BlockSpec((1, tt, w), lambda i, j: (i, j, 0))
    const = lambda shape: pl.BlockSpec(shape, lambda i, j: (0, 0))
    return pl.pallas_call(
        _inproj_kernel,
        grid=(b, s // tt),
        in_specs=[tok(d), plane(LANES), plane(LANES), const((1, d)), const(perm.shape),
                  const(wqkv.shape), const(wc.shape)],
        out_specs=[plane(ATTN_WIDTH), plane(ATTN_WIDTH), plane(ATTN_WIDTH), tok(CONV_CHANNELS)],
        out_shape=[plane_shape, plane_shape, plane_shape,
                   jax.ShapeDtypeStruct((b, s, CONV_CHANNELS), F32)],
        compiler_params=pltpu.CompilerParams(vmem_limit_bytes=VMEM_LIMIT),
        name="input_projection",
    )(x, cos_t, sin_t, g_mix_pre[None, :], perm, wqkv, wc)


def _attention_biases():
    band = lambda j: np.where((j >= 0) & (j <= WINDOW), 0.0, NEG).astype(np.float32)
    cols = lambda m: np.where(m, NEG, 0.0).astype(np.float32)
    twice = lambda a: np.concatenate([a, a], axis=0)
    mq = np.arange(WINDOW)[:, None]
    kj = np.arange(2 * WINDOW)[None, :]
    j16 = mq + WINDOW - kj
    prev16 = kj < WINDOW
    row = np.arange(128)[:, None]
    col = np.arange(256)[None, :]
    j4 = 4 * (row % 32 - (col % 64 - 32)) + row // 32 - col // 64
    prev4 = col % 64 < 32
    row = np.arange(256)[:, None]
    col = np.arange(512)[None, :]
    res = np.vectorize(_residue_of_plane)
    j1 = 16 * (row % 16 - (col % 32 - 16)) + res(row // 16) - res(col // 32)
    prev1 = col % 32 < 16
    b1 = np.stack([twice(band(j1[:128])), twice(band(j1[128:]))])
    return [jnp.asarray(a) for a in (twice(band(j16)), twice(band(j4)), b1, cols(prev16), cols(prev4), cols(prev1))]


UNROLL = 8


def _attn_kernel(q_ref, kc_ref, kp_ref, vc_ref, vp_ref, b16_ref, b4_ref, b1_ref, p16_ref, p4_ref, p1_ref,
                 o_ref, m_scr, l_scr, a_scr, bias16_scr):
    no_prev = (pl.program_id(1) == 0).astype(F32)
    head0 = lax.broadcasted_iota(jnp.int32, (1, LANES), 1) < HEAD_DIM
    bias16_scr[...] = b16_ref[...] + no_prev * p16_ref[...]

    def tile(qt, kt, vt, bias):
        n = qt.shape[0]
        zero = jnp.zeros_like(qt)
        q2 = jnp.concatenate([jnp.where(head0, qt, zero), jnp.where(head0, zero, qt)], axis=0)
        s = lax.dot_general(q2, kt, (((1,), (1,)), ((), ())), preferred_element_type=F32) + bias
        m = jnp.max(s, axis=-1, keepdims=True)
        e = jnp.exp(s - m).astype(BF16)
        va = jnp.concatenate([vt, jnp.ones_like(vt)], axis=1)
        o = jnp.dot(e, va, preferred_element_type=F32)
        pick = lambda top, bot: jnp.where(head0, top, bot)
        mm = pick(jnp.broadcast_to(m[:n], (n, LANES)), jnp.broadcast_to(m[n:], (n, LANES)))
        return mm, pick(o[:n, LANES:], o[n:, LANES:]), pick(o[:n, :LANES], o[n:, :LANES])

    def put(branch, start, size, stats, off):
        for scr, val in zip((m_scr, l_scr, a_scr), stats):
            scr[branch, pl.ds(start, size), :] = val[off:off + size]

    def body16(i, carry):
        for p in [UNROLL * i + u for u in range(UNROLL)]:
            kt = jnp.concatenate([kp_ref[0, p], kc_ref[0, p]], axis=0)
            vt = jnp.concatenate([vp_ref[0, p], vc_ref[0, p]], axis=0)
            put(0, pl.multiple_of(p * WINDOW, WINDOW), WINDOW, tile(q_ref[0, p], kt, vt, bias16_scr[...]), 0)
        return carry
    lax.fori_loop(0, N_PLANES // UNROLL, body16, 0)

    def body4(g, carry):
        for c, i in [(2 * g + cc, ii) for cc in range(2) for ii in range(4)]:
            qt = jnp.concatenate([q_ref[0, 4 * c + a, 32 * i:32 * i + 32, :] for a in range(4)], axis=0)
            if i == 0:
                ks = [x for a in range(4) for x in (kp_ref[0, 4 * c + a, 96:128, :], kc_ref[0, 4 * c + a, 0:32, :])]
                vs = [x for a in range(4) for x in (vp_ref[0, 4 * c + a, 96:128, :], vc_ref[0, 4 * c + a, 0:32, :])]
                bias = b4_ref[...] + no_prev * p4_ref[...]
            else:
                ks = [kc_ref[0, 4 * c + a, 32 * i - 32:32 * i + 32, :] for a in range(4)]
                vs = [vc_ref[0, 4 * c + a, 32 * i - 32:32 * i + 32, :] for a in range(4)]
                bias = b4_ref[...]
            stats = tile(qt, jnp.concatenate(ks, axis=0), jnp.concatenate(vs, axis=0), bias)
            for a in range(4):
                put(1, pl.multiple_of((4 * c + a) * WINDOW + 32 * i, 32), 32, stats, 32 * a)
        return carry
    lax.fori_loop(0, 2, body4, 0)

    def tile1(i, first):
        rq = pl.ds(pl.multiple_of(16 * i, 16), 16)
        if first:
            ks = [x for p in range(N_PLANES) for x in (kp_ref[0, p, 112:128, :], kc_ref[0, p, 0:16, :])]
            vs = [x for p in range(N_PLANES) for x in (vp_ref[0, p, 112:128, :], vc_ref[0, p, 0:16, :])]
        else:
            rk = pl.ds(pl.multiple_of(16 * i - 16, 16), 32)
            ks = [kc_ref[0, p, rk, :] for p in range(N_PLANES)]
            vs = [vc_ref[0, p, rk, :] for p in range(N_PLANES)]
        kt = jnp.concatenate(ks, axis=0)
        vt = jnp.concatenate(vs, axis=0)
        for half in range(2):
            planes = range(8 * half, 8 * half + 8)
            qt = jnp.concatenate([q_ref[0, p, rq, :] for p in planes], axis=0)
            bias = b1_ref[half] + no_prev * p1_ref[...] if first else b1_ref[half]
            stats = tile(qt, kt, vt, bias)
            for p in planes:
                put(2, pl.multiple_of(p * WINDOW + 16 * i, 16), 16, stats, 16 * (p - 8 * half))

    tile1(0, True)
    tile1(1, False)

    def body1(g, carry):
        tile1(2 * g, False)
        tile1(2 * g + 1, False)
        return carry
    lax.fori_loop(1, WINDOW // 32, body1, 0)

    def combine(i, carry):
        for p in (2 * i, 2 * i + 1):
            rows = pl.ds(pl.multiple_of(p * WINDOW, WINDOW), WINDOW)
            ms = [m_scr[b, rows, :] for b in range(3)]
            mx = jnp.maximum(jnp.maximum(ms[0], ms[1]), ms[2])
            ws = [jnp.exp(m - mx) for m in ms]
            den = ws[0] * l_scr[0, rows, :] + ws[1] * l_scr[1, rows, :] + ws[2] * l_scr[2, rows, :]
            num = ws[0] * a_scr[0, rows, :] + ws[1] * a_scr[1, rows, :] + ws[2] * a_scr[2, rows, :]
            o_ref[0, p] = (num / den).astype(BF16)
        return carry
    lax.fori_loop(0, N_PLANES // 2, combine, 0)


def _dilated_attention(q, k, v):
    b, _, sm, _ = q.shape
    n_span = sm // WINDOW
    cur = pl.BlockSpec((1, N_PLANES, WINDOW, LANES), lambda i, j, h: (i, 0, j, h))
    prev = pl.BlockSpec((1, N_PLANES, WINDOW, LANES), lambda i, j, h: (i, 0, jnp.maximum(j - 1, 0), h))
    biases = _attention_biases()
    bias_specs = [pl.BlockSpec(a.shape, lambda i, j, h, nd=a.ndim: (0,) * nd) for a in biases]
    stats = pltpu.VMEM((3, SPAN, LANES), F32)
    return pl.pallas_call(
        _attn_kernel,
        grid=(b, n_span, ATTN_WIDTH // LANES),
        in_specs=[cur, cur, prev, cur, prev] + bias_specs,
        out_specs=cur,
        out_shape=jax.ShapeDtypeStruct(q.shape, BF16),
        scratch_shapes=[stats, stats, stats, pltpu.VMEM((2 * WINDOW, 2 * WINDOW), F32)],
        compiler_params=pltpu.CompilerParams(vmem_limit_bytes=VMEM_LIMIT),
        name="dilated_attention",
    )(q, k, k, v, v, *biases)


CONV_HALO = 32


def _conv_kernel(cur_ref, prev_ref, w_ref, b_ref, g_ref, bl_ref, o_ref, scr, *, chunk):
    tt = cur_ref.shape[1]
    has_prev = pl.program_id(1) > 0
    scr[0, 0:CONV_HALO, :] = jnp.where(has_prev, prev_ref[0], 0.0)
    scr[0, CONV_HALO:CONV_HALO + tt, :] = cur_ref[0]
    aligned_rows = tt + CONV_HALO - SUBLANES
    for s in range(1, SUBLANES):
        scr[s, 0:aligned_rows, :] = scr[0, s:s + aligned_rows, :]
    lead = CONV_HALO - (CONV_WIDTH - 1)
    for c0 in range(0, tt, chunk):
        acc = jnp.zeros((chunk, CONV_CHANNELS), F32)
        for j in range(CONV_WIDTH):
            s, a = (lead + j) % SUBLANES, (lead + j) // SUBLANES * SUBLANES
            acc = acc + w_ref[j:j + 1, :] * scr[s, c0 + a:c0 + a + chunk, :]
        y = acc + b_ref[...]
        mu = jnp.mean(y, axis=-1, keepdims=True)
        var = jnp.mean(jnp.square(y - mu), axis=-1, keepdims=True)
        yn = (y - mu) * lax.rsqrt(var + NORM_EPS) * g_ref[...] + bl_ref[...]
        o_ref[0, c0:c0 + chunk, :] = (yn * jax.nn.sigmoid(yn)).astype(BF16)


def _conformer_conv(glu, w_dw, b_dw, g_ln, b_ln):
    b, s, c = glu.shape
    tt = 512
    row = pl.BlockSpec((1, c), lambda i, j: (0, 0))
    return pl.pallas_call(
        functools.partial(_conv_kernel, chunk=128),
        grid=(b, s // tt),
        in_specs=[pl.BlockSpec((1, tt, c), lambda i, j: (i, j, 0)),
                  pl.BlockSpec((1, CONV_HALO, c), lambda i, j: (i, jnp.maximum(j * (tt // CONV_HALO) - 1, 0), 0)),
                  pl.BlockSpec((CONV_WIDTH, c), lambda i, j: (0, 0)), row, row, row],
        out_specs=pl.BlockSpec((1, tt, c), lambda i, j: (i, j, 0)),
        out_shape=jax.ShapeDtypeStruct((b, s, c), BF16),
        scratch_shapes=[pltpu.VMEM((SUBLANES, CONV_HALO + tt, c), F32)],
        name="conformer_conv",
    )(glu, glu, w_dw[:, 0, :], b_dw[None, :], g_ln[None, :], b_ln[None, :])


def _outproj_kernel(attn_ref, conv_ref, x_ref, permt_ref, woa_ref, woc_ref, gpost_ref, gffn_ref, wr_ref, br_ref,
                    h_ref, u_ref, idx_ref, gate_ref):
    tt = x_ref.shape[1]
    nat = []
    for sub in range(tt // PERM_TOKENS):
        rows = slice(sub * PERM_ROWS, (sub + 1) * PERM_ROWS)
        a = jnp.concatenate([attn_ref[0, p, rows, :] for p in range(N_PLANES)], axis=0)
        nat.append(jnp.dot(permt_ref[...], a, preferred_element_type=F32).astype(BF16))
    mix = (jnp.dot(jnp.concatenate(nat, axis=0), woa_ref[...], preferred_element_type=F32)
           + jnp.dot(conv_ref[0], woc_ref[...], preferred_element_type=F32))
    h = x_ref[0] + _rms(mix, gpost_ref[...])
    h_ref[0] = h
    u = _rms(h, gffn_ref[...])
    half = u.shape[1] // 2
    ub = u.astype(BF16).astype(F32)
    u_ref[...] = ((lax.bitcast_convert_type(ub[:, :half], jnp.uint32) >> 16)
                  | (lax.bitcast_convert_type(ub[:, half:], jnp.uint32) & jnp.uint32(0xFFFF0000)))
    u_hi = u.astype(BF16)
    u_lo = (u - u_hi.astype(F32)).astype(BF16)
    nt = (((1,), (1,)), ((), ()))
    by_hi = lax.dot_general(wr_ref[...], u_hi, nt, preferred_element_type=F32)
    by_lo = lax.dot_general(wr_ref[:N_EXPERTS], u_lo, nt, preferred_element_type=F32)
    logits = by_hi[:N_EXPERTS] + (by_hi[N_EXPERTS:] + by_lo) + br_ref[...]
    rows = lax.broadcasted_iota(jnp.int32, logits.shape, 0)
    vals = logits
    tops, idxs = [], []
    for _ in range(TOP_K):
        mx = jnp.max(vals, axis=0, keepdims=True)
        ix = jnp.min(jnp.where(vals == mx, rows, N_EXPERTS), axis=0, keepdims=True)
        tops.append(mx)
        idxs.append(ix)
        vals = jnp.where(rows == ix, -jnp.inf, vals)
    ex = [jnp.exp(t - tops[0]) for t in tops]
    den = ex[0] + ex[1] + ex[2] + ex[3]
    idx_ref[...] = jnp.concatenate(idxs, axis=0)
    gate_ref[...] = jnp.concatenate([e / den for e in ex] + [jnp.zeros((8 - TOP_K, tt), F32)], axis=0)


def _output_projection(attn, conv, x, w_out, g_mix_post, g_ffn_pre, w_router, b_router):
    b, s, d = x.shape
    tt = 512
    mc = tt // N_PLANES
    n_t = s // tt
    woa = w_out[:ATTN_WIDTH].astype(BF16)
    woc = w_out[ATTN_WIDTH:].astype(BF16)
    wr_hi = w_router.T.astype(BF16)
    wr_split = jnp.concatenate([wr_hi, (w_router.T - wr_hi.astype(F32)).astype(BF16)], axis=0)
    permt =jnp.asarray(_plane_permutation().T, BF16)
    const = lambda shape: pl.BlockSpec(shape, lambda i, j: (0, 0))
    flat = lambda w: pl.BlockSpec((tt, w), lambda i, j: (i * n_t + j, 0))
    lanes = lambda r: pl.BlockSpec((r, tt), lambda i, j: (0, i * n_t + j))
    return pl.pallas_call(
        _outproj_kernel,
        grid=(b, n_t),
        in_specs=[pl.BlockSpec((1, N_PLANES, mc, ATTN_WIDTH), lambda i, j: (i, 0, j, 0)),
                  pl.BlockSpec((1, tt, CONV_CHANNELS), lambda i, j: (i, j, 0)),
                  pl.BlockSpec((1, tt, d), lambda i, j: (i, j, 0)),
                  const(permt.shape), const(woa.shape), const(woc.shape), const((1, d)), const((1, d)),
                  const((2 * N_EXPERTS, d)), const((N_EXPERTS, 1))],
        out_specs=[pl.BlockSpec((1, tt, d), lambda i, j: (i, j, 0)), flat(d // 2), lanes(TOP_K), lanes(8)],
        out_shape=[jax.ShapeDtypeStruct((b, s, d), F32),
                   jax.ShapeDtypeStruct((b * s, d // 2), jnp.uint32),
                   jax.ShapeDtypeStruct((TOP_K, b * s), jnp.int32),
                   jax.ShapeDtypeStruct((8, b * s), F32)],
        compiler_params=pltpu.CompilerParams(vmem_limit_bytes=VMEM_LIMIT),
        name="output_projection_router",
    )(attn, conv, x, permt, woa, woc, g_mix_post[None, :], g_ffn_pre[None, :], wr_split, b_router[:, None])


def _route_kernel(idx_ref, dest_ref, pend_ref, pad_ref, blk_ref, carry, pstart):
    phase = pl.program_id(0)
    step = pl.program_id(1)
    tt = idx_ref.shape[1]
    rows = lax.broadcasted_iota(jnp.int32, (N_EXPERTS, tt), 0)
    hot = [rows == idx_ref[k:k + 1, :] for k in range(TOP_K)]
    memb = sum(h.astype(F32) for h in hot)

    @pl.when((phase == 0) & (step == 0))
    def _():
        carry[...] = jnp.zeros_like(carry)

    @pl.when((phase == 1) & (step == 0))
    def _():
        counts = carry[...]
        padded = jnp.floor((counts + (EXPERT_ROWS - 1)) * (1.0 / EXPERT_ROWS)) * EXPERT_ROWS
        tri = (lax.broadcasted_iota(jnp.int32, (N_EXPERTS, N_EXPERTS), 1)
               <= lax.broadcasted_iota(jnp.int32, (N_EXPERTS, N_EXPERTS), 0)).astype(F32)
        pend = jnp.dot(tri, padded, precision=lax.Precision.HIGHEST, preferred_element_type=F32)
        pstart[...] = pend - padded
        pend_ref[...] = pend.astype(jnp.int32)
        pad_ref[...] = (pend - padded + counts).astype(jnp.int32)
        starts = lax.broadcasted_iota(jnp.int32, (N_EXPERTS, blk_ref.shape[1]), 1) * EXPERT_ROWS
        ended = (pend.astype(jnp.int32) <= starts).astype(jnp.int32)
        blk_ref[...] = jnp.minimum(jnp.sum(ended, axis=0, keepdims=True), N_EXPERTS - 1)
        carry[...] = jnp.zeros_like(carry)

    @pl.when(phase == 1)
    def _():
        earlier = (lax.broadcasted_iota(jnp.int32, (tt, tt), 0)
                   < lax.broadcasted_iota(jnp.int32, (tt, tt), 1)).astype(BF16)
        row = jnp.dot(memb.astype(BF16), earlier, preferred_element_type=F32) + (carry[...] + pstart[...])
        dest_ref[...] = jnp.concatenate(
            [jnp.sum(jnp.where(h, row, 0.0), axis=0, keepdims=True) for h in hot], axis=0).astype(jnp.int32)

    carry[...] = carry[...] + jnp.sum(memb, axis=1, keepdims=True)


def _routing(idx, n_blk):
    _, t = idx.shape
    tt = 512
    blk_lanes = -(-n_blk // LANES) * LANES
    return pl.pallas_call(
        _route_kernel,
        grid=(2, t // tt),
        in_specs=[pl.BlockSpec((TOP_K, tt), lambda ph, i: (0, i))],
        out_specs=[pl.BlockSpec((TOP_K, tt), lambda ph, i: (0, i * ph)),
                   pl.BlockSpec((N_EXPERTS, 1), lambda ph, i: (0, 0)),
                   pl.BlockSpec((N_EXPERTS, 1), lambda ph, i: (0, 0)),
                   pl.BlockSpec((1, blk_lanes), lambda ph, i: (0, 0))],
        out_shape=[jax.ShapeDtypeStruct((TOP_K, t), jnp.int32), jax.ShapeDtypeStruct((N_EXPERTS, 1), jnp.int32),
                   jax.ShapeDtypeStruct((N_EXPERTS, 1), jnp.int32), jax.ShapeDtypeStruct((1, blk_lanes), jnp.int32)],
        scratch_shapes=[pltpu.VMEM((N_EXPERTS, 1), F32), pltpu.VMEM((N_EXPERTS, 1), F32)],
        compiler_params=pltpu.CompilerParams(dimension_semantics=("arbitrary", "arbitrary")),
        name="moe_routing",
    )(idx)


SC_SCAN = 4096


def _sc_token_of_row(dest_flat, cap, n_tok):
    from jax.experimental.pallas import tpu_sc as plsc
    assert n_tok & (n_tok - 1) == 0
    info = plsc.get_sparse_core_info()
    n_core, n_sub, lanes = info.num_cores, info.num_subcores, info.num_lanes
    n_src = dest_flat.shape[0]
    per_w = cap // (n_core * n_sub)
    assert per_w * n_core * n_sub == cap and per_w % lanes == 0 and n_src % SC_SCAN == 0
    mesh = plsc.VectorSubcoreMesh(core_axis_name="c", subcore_axis_name="s")

    def body(dest_hbm, out_hbm, src_v, out_v):
        lo = (lax.axis_index("s") * n_core + lax.axis_index("c")) * per_w
        zero = jnp.zeros((lanes,), jnp.int32)

        @pl.loop(0, per_w // lanes)
        def _(i):
            out_v[pl.ds(pl.multiple_of(i * lanes, lanes), lanes)] = zero
        lane = lax.iota(jnp.int32, lanes)

        @pl.loop(0, n_src // SC_SCAN)
        def _(c):
            pltpu.sync_copy(dest_hbm.at[pl.ds(pl.multiple_of(c * SC_SCAN, 8), SC_SCAN)], src_v)

            @pl.loop(0, SC_SCAN // lanes)
            def _(v):
                local = src_v[pl.ds(pl.multiple_of(v * lanes, lanes), lanes)] - lo
                mine = lax.bitcast_convert_type(local, jnp.uint32) < jnp.uint32(per_w)
                tok = (c * SC_SCAN + v * lanes + lane) & (n_tok - 1)
                plsc.store_scatter(out_v, [local], tok, mask=mine)
        pltpu.sync_copy(out_v, out_hbm.at[pl.ds(pl.multiple_of(lo, 8), per_w)])

    return pl.kernel(body, mesh=mesh, out_type=jax.ShapeDtypeStruct((cap,), jnp.int32),
                     scratch_types=[pltpu.VMEM((SC_SCAN,), jnp.int32), pltpu.VMEM((per_w,), jnp.int32)],
                     compiler_params=pltpu.CompilerParams(needs_layout_passes=False),
                     name="moe_sc_token_of_row")(dest_flat)


def _expert_kernel(blk_e_ref, nused_ref, xb_ref, wgu_ref, bgu_ref, wd_ref, bd_ref, *rest):
    yb_ref, wgu_b, wd_b = rest[-3:]
    j = pl.program_id(0)
    nused = nused_ref[0]
    d_ff = wd_ref.shape[1]

    new_expert = (j == 0) | (blk_e_ref[j] != blk_e_ref[jnp.maximum(j - 1, 0)])

    @pl.when((j < nused) & new_expert)
    def _():
        def cast(src, dst):
            def body(i, carry):
                rows = pl.ds(pl.multiple_of(i * LANES, LANES), LANES)
                dst[rows, :] = src[0, rows, :].astype(BF16)
                return carry
            lax.fori_loop(0, dst.shape[0] // LANES, body, 0)
        cast(wgu_ref, wgu_b)
        cast(wd_ref, wd_b)

    @pl.when(j < nused)
    def _():
        word = xb_ref[...]
        half = word.shape[1]
        lo = lax.bitcast_convert_type(word << 16, F32).astype(BF16)
        hi = lax.bitcast_convert_type(word & jnp.uint32(0xFFFF0000), F32).astype(BF16)
        gu = (jnp.dot(lo, wgu_b[:half, :], preferred_element_type=F32)
              + jnp.dot(hi, wgu_b[half:, :], preferred_element_type=F32) + bgu_ref[0])
        gate = jnp.minimum(gu[:, :d_ff], SWIGLU_LIMIT)
        up = jnp.clip(gu[:, d_ff:], -SWIGLU_LIMIT, SWIGLU_LIMIT)
        hid = (up + 1.0) * (gate * jax.nn.sigmoid(gate * SWIGLU_ALPHA))
        yb_ref[...] = jnp.dot(hid.astype(BF16), wd_b[...], preferred_element_type=F32) + bd_ref[0]

    @pl.when(j >= nused)
    def _():
        yb_ref[...] = jnp.zeros_like(yb_ref)


def _experts(xb, chunk, n_chunk, yb_prev, blk_e, nused, w_gate_up, b_gate_up, w_down, b_down):
    rows, d_in = xb.shape
    e, d, ff2 = w_gate_up.shape
    d_ff = w_down.shape[1]
    n_blk = rows // EXPERT_ROWS
    live = lambda j, nu: jnp.maximum(jnp.minimum(j, nu[0] - 1), 0)
    expert = lambda j, be, nu: (be[live(j, nu)], 0, 0)
    in_specs = [pl.BlockSpec((EXPERT_ROWS, d_in), lambda j, be, nu: (live(j, nu), 0)),
                pl.BlockSpec((1, d, ff2), expert), pl.BlockSpec((1, 1, ff2), expert),
                pl.BlockSpec((1, d_ff, d), expert), pl.BlockSpec((1, 1, d), expert)]
    args = [blk_e, nused, xb, w_gate_up, b_gate_up[:, None, :], w_down, b_down[:, None, :]]
    aliases = {}
    if yb_prev is not None:
        in_specs.append(pl.BlockSpec(memory_space=pl.ANY))
        args.append(yb_prev)
        aliases = {len(args) - 1: 0}
    grid_spec = pltpu.PrefetchScalarGridSpec(
        num_scalar_prefetch=2,
        grid=(n_blk,),
        in_specs=in_specs,
        out_specs=pl.BlockSpec((EXPERT_ROWS, d), lambda j, be, nu: (chunk * n_blk + j, 0)),
        scratch_shapes=[pltpu.VMEM((d, ff2), BF16), pltpu.VMEM((d_ff, d), BF16)],
    )
    return pl.pallas_call(
        _expert_kernel,
        grid_spec=grid_spec,
        out_shape=jax.ShapeDtypeStruct((n_chunk * rows, d), F32),
        input_output_aliases=aliases,
        compiler_params=pltpu.CompilerParams(dimension_semantics=("arbitrary",), vmem_limit_bytes=VMEM_LIMIT),
        name="moe_experts",
    )(*args)


SC_BUFFER_BYTES = 128 * 1024
MOE_CHUNKS = 8
MOE_GROUPS = 4


def _sc_gather_rows(table, idx, part, n_part):
    from jax.experimental.pallas import tpu_sc as plsc
    n_rows = idx.shape[0] // n_part
    d = table.shape[1]
    info = plsc.get_sparse_core_info()
    n_core, n_sub = info.num_cores, info.num_subcores
    per_w = n_rows // (n_core * n_sub)
    fit = SC_BUFFER_BYTES // (d * table.dtype.itemsize)
    g_rows = max(r for r in (8, 16, 32, 64, 128) if r <= fit and per_w % r == 0)
    n_chunk = per_w // g_rows
    assert per_w * n_core * n_sub == n_rows
    mesh = plsc.VectorSubcoreMesh(core_axis_name="c", subcore_axis_name="s")

    def body(table_hbm, idx_hbm, out_hbm, idx_v, buf0, buf1, sem0, sem1):
        base = (lax.axis_index("s") * n_core + lax.axis_index("c")) * per_w
        pltpu.sync_copy(idx_hbm.at[pl.ds(pl.multiple_of(part * n_rows + base, 8), per_w)], idx_v)

        def gather(chunk, buf, sem):
            rows = idx_v.at[pl.ds(pl.multiple_of(chunk * g_rows, 8), g_rows)]
            return pltpu.make_async_copy(table_hbm.at[rows], buf, sem)

        def write(chunk, buf):
            pltpu.sync_copy(buf, out_hbm.at[pl.ds(pl.multiple_of(base + chunk * g_rows, 8), g_rows)])

        gather(0, buf0, sem0).start()

        @pl.loop(0, n_chunk // 2)
        def _(i):
            gather(2 * i + 1, buf1, sem1).start()
            gather(2 * i, buf0, sem0).wait()
            write(2 * i, buf0)

            @pl.when(2 * i + 2 < n_chunk)
            def _():
                gather(2 * i + 2, buf0, sem0).start()
            gather(2 * i + 1, buf1, sem1).wait()
            write(2 * i + 1, buf1)

        if n_chunk % 2:
            gather(n_chunk - 1, buf0, sem0).wait()
            write(n_chunk - 1, buf0)

    return pl.kernel(
        body, mesh=mesh, out_type=jax.ShapeDtypeStruct((n_rows, d), table.dtype),
        scratch_types=[pltpu.VMEM((per_w,), jnp.int32), pltpu.VMEM((g_rows, d), table.dtype),
                       pltpu.VMEM((g_rows, d), table.dtype), pltpu.SemaphoreType.DMA, pltpu.SemaphoreType.DMA],
        name="moe_sc_gather",
    )(table, idx)


def _sc_weighted_rows(table, idx, gates, group, n_group):
    from jax.experimental.pallas import tpu_sc as plsc
    d = table.shape[1]
    info = plsc.get_sparse_core_info()
    n_core, n_sub, lanes = info.num_cores, info.num_subcores, info.num_lanes
    n_all = idx.shape[0] // TOP_K
    n_tok = n_all // n_group
    per_w = n_tok // (n_core * n_sub)
    w = 8
    n_chunk = per_w // w
    assert per_w * n_core * n_sub == n_tok and n_chunk * w == per_w and n_chunk % 2 == 0
    mesh = plsc.VectorSubcoreMesh(core_axis_name="c", subcore_axis_name="s")

    def body(table_hbm, idx_hbm, g_hbm, out_hbm, *scr):
        idx_v, g_v, rb, ob = scr[0:4], scr[4:8], (scr[8:12], scr[12:16]), scr[16:18]
        sems, wsems = scr[18:20], scr[20:22]
        base = (lax.axis_index("s") * n_core + lax.axis_index("c")) * per_w
        for k in range(TOP_K):
            off = pl.multiple_of(k * n_all + group * n_tok + base, 8)
            pltpu.sync_copy(idx_hbm.at[pl.ds(off, per_w)], idx_v[k])
            pltpu.sync_copy(g_hbm.at[pl.ds(off, per_w)], g_v[k])

        def gathers(chunk, slot):
            rows = pl.ds(pl.multiple_of(chunk * w, 8), w)
            return [pltpu.make_async_copy(table_hbm.at[idx_v[k].at[rows]], rb[slot][k], sems[slot])
                    for k in range(TOP_K)]

        def write(chunk, slot):
            return pltpu.make_async_copy(ob[slot], out_hbm.at[pl.ds(pl.multiple_of(base + chunk * w, 8), w)],
                                         wsems[slot])

        def combine(chunk, slot):
            @pl.loop(0, w)
            def _(i):
                token = jnp.full((lanes,), chunk * w + i, jnp.int32)
                gs = [plsc.load_gather(g_v[k], [token]) for k in range(TOP_K)]
                for j in range(d // lanes):
                    cols = pl.ds(j * lanes, lanes)
                    acc = gs[0] * rb[slot][0][i, cols]
                    for k in range(1, TOP_K):
                        acc = acc + gs[k] * rb[slot][k][i, cols]
                    ob[slot][i, cols] = acc
            write(chunk, slot).start()

        for c in gathers(0, 0):
            c.start()

        @pl.loop(0, n_chunk // 2)
        def _(i):
            for c in gathers(2 * i + 1, 1):
                c.start()
            for c in gathers(2 * i, 0):
                c.wait()

            @pl.when(i > 0)
            def _():
                write(2 * i - 2, 0).wait()
            combine(2 * i, 0)

            @pl.when(2 * i + 2 < n_chunk)
            def _():
                for c in gathers(2 * i + 2, 0):
                    c.start()
            for c in gathers(2 * i + 1, 1):
                c.wait()

            @pl.when(i > 0)
            def _():
                write(2 * i - 1, 1).wait()
            combine(2 * i + 1, 1)

        write(n_chunk - 2, 0).wait()
        write(n_chunk - 1, 1).wait()

    scratch = ([pltpu.VMEM((per_w,), jnp.int32)] * TOP_K + [pltpu.VMEM((per_w,), F32)] * TOP_K
               + [pltpu.VMEM((w, d), F32)] * (2 * TOP_K + 2) + [pltpu.SemaphoreType.DMA] * 4)
    return pl.kernel(body, mesh=mesh, out_type=jax.ShapeDtypeStruct((n_tok, d), F32), scratch_types=scratch,
                     compiler_params=pltpu.CompilerParams(needs_layout_passes=False),
                     name="moe_sc_combine")(table, idx, gates)


def _combine_kernel(h_ref, p_ref, y_ref, gffn_ref, wple_ref, wpg_ref, gple_ref, o_ref):
    h2 = h_ref[...] + _rms(y_ref[...], gffn_ref[...])
    ple = (jnp.dot(p_ref[...].astype(BF16), wple_ref[...], preferred_element_type=F32)
           * jax.nn.sigmoid(jnp.dot(h2.astype(BF16), wpg_ref[...], preferred_element_type=F32)))
    o_ref[...] = h2 + _rms(ple, gple_ref[...])


def _combine_kernel_aliased(h_ref, p_ref, y_ref, gffn_ref, wple_ref, wpg_ref, gple_ref, prev_ref, o_ref):
    del prev_ref
    _combine_kernel(h_ref, p_ref, y_ref, gffn_ref, wple_ref, wpg_ref, gple_ref, o_ref)


def _combine(group, n_group, out_prev, h, p, y, g_ffn_post, w_ple_bf16, w_ple_gate_bf16, g_ple_post):
    t, d = h.shape
    tt = 256
    n_tg = t // n_group // tt
    const = lambda shape: pl.BlockSpec(shape, lambda i: (0, 0))
    tok = lambda w: pl.BlockSpec((tt, w), lambda i: (group * n_tg + i, 0))
    in_specs = [tok(d), tok(p.shape[1]), pl.BlockSpec((tt, d), lambda i: (i, 0)),
                const((1, d)), const(w_ple_bf16.shape), const(w_ple_gate_bf16.shape), const((1, d))]
    args = [h, p, y, g_ffn_post[None, :], w_ple_bf16, w_ple_gate_bf16, g_ple_post[None, :]]
    body, aliases = _combine_kernel, {}
    if out_prev is not None:
        in_specs.append(pl.BlockSpec(memory_space=pl.ANY))
        args.append(out_prev)
        body, aliases = _combine_kernel_aliased, {len(args) - 1: 0}
    return pl.pallas_call(
        body,
        grid=(n_tg,),
        in_specs=in_specs,
        out_specs=tok(d),
        out_shape=jax.ShapeDtypeStruct((t, d), F32),
        input_output_aliases=aliases,
        compiler_params=pltpu.CompilerParams(vmem_limit_bytes=VMEM_LIMIT),
        name="moe_combine_ple",
    )(*args)


def _layer(h, p, positions, g_mix_pre, w_in, w_dw, b_dw, g_conv_ln, b_conv_ln, w_out, g_mix_post, g_ffn_pre,
           w_router, b_router, w_gate_up, b_gate_up, w_down, b_down, g_ffn_post, w_ple, w_ple_gate, g_ple_post):
    b, s, d = h.shape
    t = b * s
    cos_t, sin_t = _rope_tables(positions)
    q, k, v, glu = _input_projection(h, cos_t, sin_t, g_mix_pre, w_in)
    attn = _dilated_attention(q, k, v)
    conv = _conformer_conv(glu, w_dw, b_dw, g_conv_ln, b_conv_ln)
    h1, u_ffn, idx, gates = _output_projection(attn, conv, h, w_out, g_mix_post, g_ffn_pre, w_router, b_router)
    cap = t * TOP_K + N_EXPERTS * EXPERT_ROWS
    n_blk = cap // EXPERT_ROWS
    dest, pend, pad, blk = _routing(idx, n_blk)
    blk_e = blk[0, :n_blk]
    nused = pend[N_EXPERTS - 1] // EXPERT_ROWS
    dest_flat = dest.reshape(TOP_K * t)
    tok_of_row = _sc_token_of_row(dest_flat, cap, t)
    rows_c = cap // MOE_CHUNKS
    blk_c = n_blk // MOE_CHUNKS
    yb = None
    for c in range(MOE_CHUNKS):
        xb_c = _sc_gather_rows(u_ffn, tok_of_row, c, MOE_CHUNKS)
        nused_c = jnp.clip(nused - c * blk_c, 0, blk_c)
        yb = _experts(xb_c, c, MOE_CHUNKS, yb, blk_e[c * blk_c:(c + 1) * blk_c], nused_c,
                      w_gate_up, b_gate_up, w_down, b_down)
    t_g = t // MOE_GROUPS
    w_ple_bf16, w_ple_gate_bf16 = w_ple.astype(BF16), w_ple_gate.astype(BF16)
    h1_flat, p_flat = h1.reshape(t, d), p.reshape(t, -1)
    gate_flat = gates[:TOP_K].reshape(TOP_K * t)
    out = None
    for g in range(MOE_GROUPS):
        y_g = _sc_weighted_rows(yb, dest_flat, gate_flat, g, MOE_GROUPS)
        out = _combine(g, MOE_GROUPS, out, h1_flat, p_flat, y_g, g_ffn_post, w_ple_bf16, w_ple_gate_bf16,
                       g_ple_post)
    return out.reshape(b, s, d)


def kernel(x, p, positions, g_mix_pre, w_in, w_dw, b_dw, g_conv_ln, b_conv_ln, w_out, g_mix_post, g_ffn_pre,
           w_router, b_router, w_gate_up, b_gate_up, w_down, b_down, g_ffn_post, w_ple, w_ple_gate, g_ple_post):
    h = x
    for i in range(p.shape[0]):
        h = _layer(h, p[i], positions, g_mix_pre[i], w_in[i], w_dw[i], b_dw[i], g_conv_ln[i], b_conv_ln[i],
                   w_out[i], g_mix_post[i], g_ffn_pre[i], w_router[i], b_router[i], w_gate_up[i], b_gate_up[i],
                   w_down[i], b_down[i], g_ffn_post[i], w_ple[i], w_ple_gate[i], g_ple_post[i])
    return h
```

```python
import functools

import numpy as np
import jax
import jax.numpy as jnp
from jax import lax
from jax.experimental import pallas as pl
from jax.experimental.pallas import tpu as pltpu

F32 = jnp.float32
BF16 = jnp.bfloat16

HEAD_DIM = 64
N_HEADS = 12
ATTN_WIDTH = N_HEADS * HEAD_DIM
CONV_CHANNELS = 256
CONV_WIDTH = 31
ROPE_DIM = HEAD_DIM // 4
ROPE_THETA = 500000.0
N_EXPERTS = 32
TOP_K = 4
SWIGLU_LIMIT = 7.0
SWIGLU_ALPHA = 1.702
NORM_EPS = 1e-6
WINDOW = 128
N_PLANES = 16
SPAN = N_PLANES * WINDOW
LANES = 128
SUBLANES = 8
NEG = -1e30
EXPERT_ROWS = 256
VMEM_LIMIT = 56 * 1024 * 1024


def _residue_of_plane(p):
    return 4 * (p % 4) + p // 4


def _rms(xv, g):
    var = jnp.mean(xv * xv, axis=-1, keepdims=True)
    return xv * lax.rsqrt(var + NORM_EPS) * g


def _rope_kernel(pos_ref, invf_ref, expand_ref, one_ref, sgn_ref, c_ref, s_ref):
    tn = (((0,), (0,)), ((), ()))
    for p in range(N_PLANES):
        ang = invf_ref[...] * pos_ref[0, p:p + 1, :].astype(F32)
        spread = lambda t: lax.dot_general(t, expand_ref[...], tn, precision=lax.Precision.HIGHEST,
                                           preferred_element_type=F32)
        c_ref[0, p] = spread(jnp.cos(ang)) + one_ref[...]
        s_ref[0, p] = spread(jnp.sin(ang)) * sgn_ref[...]


def _rope_tables(positions):
    b, s = positions.shape
    sm = s // N_PLANES
    mt = min(sm, 128)
    plane_res = np.array([_residue_of_plane(p) for p in range(N_PLANES)])
    pos_planes = positions.reshape(b, sm, N_PLANES).transpose(0, 2, 1)[:, plane_res]
    half = ROPE_DIM // 2
    lane = np.arange(LANES) % HEAD_DIM
    inv_freq = ROPE_THETA ** (-jnp.arange(0, ROPE_DIM, 2, dtype=F32) / ROPE_DIM)
    rotary = lane < ROPE_DIM
    expand = jnp.asarray((np.arange(half)[:, None] == lane[None, :] % half) & rotary[None, :], F32)
    one = jnp.asarray(~rotary, F32)[None, :]
    sgn = jnp.asarray(np.where(lane < half, -1.0, 1.0), F32)[None, :]
    row = pl.BlockSpec((1, LANES), lambda i, j: (0, 0))
    out = pl.BlockSpec((1, N_PLANES, mt, LANES), lambda i, j: (i, 0, j, 0))
    return pl.pallas_call(
        _rope_kernel,
        grid=(b, sm // mt),
        in_specs=[pl.BlockSpec((1, N_PLANES, mt), lambda i, j: (i, 0, j)),
                  pl.BlockSpec((half, 1), lambda i, j: (0, 0)), pl.BlockSpec((half, LANES), lambda i, j: (0, 0)),
                  row, row],
        out_specs=[out, out],
        out_shape=[jax.ShapeDtypeStruct((b, N_PLANES, sm, LANES), F32)] * 2,
        name="rope_tables",
    )(pos_planes, inv_freq[:, None], expand, one, sgn)


PERM_TOKENS = 256
PERM_ROWS = PERM_TOKENS // N_PLANES


def _plane_permutation():
    perm = np.zeros((PERM_TOKENS, PERM_TOKENS), np.float32)
    for p in range(N_PLANES):
        for ml in range(PERM_ROWS):
            perm[PERM_ROWS * p + ml, N_PLANES * ml + _residue_of_plane(p)] = 1.0
    return perm


def _inproj_kernel(x_ref, c_ref, s_ref, g_ref, perm_ref, wqkv_ref, wc_ref, q_ref, k_ref, v_ref, glu_ref):
    g = g_ref[...]
    tt = x_ref.shape[1]
    lane = lax.broadcasted_iota(jnp.int32, (1, LANES), 1) % HEAD_DIM
    first_half = lane < ROPE_DIM // 2

    def rotary(t, cos, sin):
        outs = []
        for j in range(ATTN_WIDTH // LANES):
            tj = t[:, j * LANES:(j + 1) * LANES]
            partner = jnp.where(first_half, pltpu.roll(tj, LANES - ROPE_DIM // 2, 1),
                                pltpu.roll(tj, ROPE_DIM // 2, 1))
            outs.append(tj * cos + partner * sin)
        return jnp.concatenate(outs, axis=1)

    un = _rms(x_ref[0], g).astype(BF16)
    pc = jnp.dot(un, wc_ref[...], preferred_element_type=F32)
    glu_ref[0] = pc[:, :CONV_CHANNELS] * jax.nn.sigmoid(pc[:, CONV_CHANNELS:])

    for sub in range(tt // PERM_TOKENS):
        rows = slice(sub * PERM_ROWS, (sub + 1) * PERM_ROWS)
        u = jnp.dot(perm_ref[...], un[sub * PERM_TOKENS:(sub + 1) * PERM_TOKENS],
                    preferred_element_type=F32).astype(BF16)
        cos = jnp.concatenate([c_ref[0, p, rows, :] for p in range(N_PLANES)], axis=0)
        sin = jnp.concatenate([s_ref[0, p, rows, :] for p in range(N_PLANES)], axis=0)
        proj = jnp.dot(u, wqkv_ref[...], preferred_element_type=F32)
        q = (rotary(proj[:, :ATTN_WIDTH], cos, sin) * (HEAD_DIM ** -0.5)).astype(BF16)
        k = rotary(proj[:, ATTN_WIDTH:2 * ATTN_WIDTH], cos, sin).astype(BF16)
        v = proj[:, 2 * ATTN_WIDTH:].astype(BF16)
        for p in range(N_PLANES):
            chunk = slice(p * PERM_ROWS, (p + 1) * PERM_ROWS)
            q_ref[0, p, rows, :] = q[chunk]
            k_ref[0, p, rows, :] = k[chunk]
            v_ref[0, p, rows, :] = v[chunk]


def _input_projection(x, cos_t, sin_t, g_mix_pre, w_in):
    b, s, d = x.shape
    tt = 1024
    mc = tt // N_PLANES
    wqkv = w_in[:, :3 * ATTN_WIDTH].astype(BF16)
    wc = w_in[:, 3 * ATTN_WIDTH:].astype(BF16)
    perm = jnp.asarray(_plane_permutation(), BF16)
    plane = lambda w: pl.BlockSpec((1, N_PLANES, mc, w), lambda i, j: (i, 0, j, 0))
    plane_shape = jax.ShapeDtypeStruct((b, N_PLANES, s // N_PLANES, ATTN_WIDTH), BF16)
    tok = lambda w: pl.BlockSpec((1, tt, w), lambda i, j: (i, j, 0))
    const = lambda shape: pl.BlockSpec(shape, lambda i, j: (0, 0))
    return pl.pallas_call(
        _inproj_kernel,
        grid=(b, s // tt),
        in_specs=[tok(d), plane(LANES), plane(LANES), const((1, d)), const(perm.shape),
                  const(wqkv.shape), const(wc.shape)],
        out_specs=[plane(ATTN_WIDTH), plane(ATTN_WIDTH), plane(ATTN_WIDTH), tok(CONV_CHANNELS)],
        out_shape=[plane_shape, plane_shape, plane_shape,
                   jax.ShapeDtypeStruct((b, s, CONV_CHANNELS), F32)],
        compiler_params=pltpu.CompilerParams(vmem_limit_bytes=VMEM_LIMIT),
        name="input_projection",
    )(x, cos_t, sin_t, g_mix_pre[None, :], perm, wqkv, wc)


def _attention_biases():
    band = lambda j: np.where((j >= 0) & (j <= WINDOW), 0.0, NEG).astype(np.float32)
    cols = lambda m: np.where(m, NEG, 0.0).astype(np.float32)
    twice = lambda a: np.concatenate([a, a], axis=0)
    mq = np.arange(WINDOW)[:, None]
    kj = np.arange(2 * WINDOW)[None, :]
    j16 = mq + WINDOW - kj
    prev16 = kj < WINDOW
    row = np.arange(128)[:, None]
    col = np.arange(256)[None, :]
    j4 = 4 * (row % 32 - (col % 64 - 32)) + row // 32 - col // 64
    prev4 = col % 64 < 32
    row = np.arange(256)[:, None]
    col = np.arange(512)[None, :]
    res = np.vectorize(_residue_of_plane)
    j1 = 16 * (row % 16 - (col % 32 - 16)) + res(row // 16) - res(col // 32)
    prev1 = col % 32 < 16
    b1 = np.stack([twice(band(j1[:128])), twice(band(j1[128:]))])
    return [jnp.asarray(a) for a in (twice(band(j16)), twice(band(j4)), b1, cols(prev16), cols(prev4), cols(prev1))]


UNROLL = 8


def _attn_kernel(q_ref, kc_ref, kp_ref, vc_ref, vp_ref, b16_ref, b4_ref, b1_ref, p16_ref, p4_ref, p1_ref,
                 o_ref, m_scr, l_scr, a_scr, bias16_scr):
    no_prev = (pl.program_id(1) == 0).astype(F32)
    head0 = lax.broadcasted_iota(jnp.int32, (1, LANES), 1) < HEAD_DIM
    bias16_scr[...] = b16_ref[...] + no_prev * p16_ref[...]

    def tile(qt, kt, vt, bias):
        n = qt.shape[0]
        zero = jnp.zeros_like(qt)
        q2 = jnp.concatenate([jnp.where(head0, qt, zero), jnp.where(head0, zero, qt)], axis=0)
        s = lax.dot_general(q2, kt, (((1,), (1,)), ((), ())), preferred_element_type=F32) + bias
        m = jnp.max(s, axis=-1, keepdims=True)
        e = jnp.exp(s - m).astype(BF16)
        va = jnp.concatenate([vt, jnp.ones_like(vt)], axis=1)
        o = jnp.dot(e, va, preferred_element_type=F32)
        pick = lambda top, bot: jnp.where(head0, top, bot)
        mm = pick(jnp.broadcast_to(m[:n], (n, LANES)), jnp.broadcast_to(m[n:], (n, LANES)))
        return mm, pick(o[:n, LANES:], o[n:, LANES:]), pick(o[:n, :LANES], o[n:, :LANES])

    def put(branch, start, size, stats, off):
        for scr, val in zip((m_scr, l_scr, a_scr), stats):
            scr[branch, pl.ds(start, size), :] = val[off:off + size]

    def body16(i, carry):
        for p in [UNROLL * i + u for u in range(UNROLL)]:
            kt = jnp.concatenate([kp_ref[0, p], kc_ref[0, p]], axis=0)
            vt = jnp.concatenate([vp_ref[0, p], vc_ref[0, p]], axis=0)
            put(0, pl.multiple_of(p * WINDOW, WINDOW), WINDOW, tile(q_ref[0, p], kt, vt, bias16_scr[...]), 0)
        return carry
    lax.fori_loop(0, N_PLANES // UNROLL, body16, 0)

    def body4(g, carry):
        for c, i in [(2 * g + cc, ii) for cc in range(2) for ii in range(4)]:
            qt = jnp.concatenate([q_ref[0, 4 * c + a, 32 * i:32 * i + 32, :] for a in range(4)], axis=0)
            if i == 0:
                ks = [x for a in range(4) for x in (kp_ref[0, 4 * c + a, 96:128, :], kc_ref[0, 4 * c + a, 0:32, :])]
                vs = [x for a in range(4) for x in (vp_ref[0, 4 * c + a, 96:128, :], vc_ref[0, 4 * c + a, 0:32, :])]
                bias = b4_ref[...] + no_prev * p4_ref[...]
            else:
                ks = [kc_ref[0, 4 * c + a, 32 * i - 32:32 * i + 32, :] for a in range(4)]
                vs = [vc_ref[0, 4 * c + a, 32 * i - 32:32 * i + 32, :] for a in range(4)]
                bias = b4_ref[...]
            stats = tile(qt, jnp.concatenate(ks, axis=0), jnp.concatenate(vs, axis=0), bias)
            for a in range(4):
                put(1, pl.multiple_of((4 * c + a) * WINDOW + 32 * i, 32), 32, stats, 32 * a)
        return carry
    lax.fori_loop(0, 2, body4, 0)

    def tile1(i, first):
        rq = pl.ds(pl.multiple_of(16 * i, 16), 16)
        if first:
            ks = [x for p in range(N_PLANES) for x in (kp_ref[0, p, 112:128, :], kc_ref[0, p, 0:16, :])]
            vs = [x for p in range(N_PLANES) for x in (vp_ref[0, p, 112:128, :], vc_ref[0, p, 0:16, :])]
        else:
            rk = pl.ds(pl.multiple_of(16 * i - 16, 16), 32)
            ks = [kc_ref[0, p, rk, :] for p in range(N_PLANES)]
            vs = [vc_ref[0, p, rk, :] for p in range(N_PLANES)]
        kt = jnp.concatenate(ks, axis=0)
        vt = jnp.concatenate(vs, axis=0)
        for half in range(2):
            planes = range(8 * half, 8 * half + 8)
            qt = jnp.concatenate([q_ref[0, p, rq, :] for p in planes], axis=0)
            bias = b1_ref[half] + no_prev * p1_ref[...] if first else b1_ref[half]
            stats = tile(qt, kt, vt, bias)
            for p in planes:
                put(2, pl.multiple_of(p * WINDOW + 16 * i, 16), 16, stats, 16 * (p - 8 * half))

    tile1(0, True)
    tile1(1, False)

    def body1(g, carry):
        tile1(2 * g, False)
        tile1(2 * g + 1, False)
        return carry
    lax.fori_loop(1, WINDOW // 32, body1, 0)

    def combine(i, carry):
        for p in (2 * i, 2 * i + 1):
            rows = pl.ds(pl.multiple_of(p * WINDOW, WINDOW), WINDOW)
            ms = [m_scr[b, rows, :] for b in range(3)]
            mx = jnp.maximum(jnp.maximum(ms[0], ms[1]), ms[2])
            ws = [jnp.exp(m - mx) for m in ms]
            den = ws[0] * l_scr[0, rows, :] + ws[1] * l_scr[1, rows, :] + ws[2] * l_scr[2, rows, :]
            num = ws[0] * a_scr[0, rows, :] + ws[1] * a_scr[1, rows, :] + ws[2] * a_scr[2, rows, :]
            o_ref[0, p] = (num / den).astype(BF16)
        return carry
    lax.fori_loop(0, N_PLANES // 2, combine, 0)


def _dilated_attention(q, k, v):
    b, _, sm, _ = q.shape
    n_span = sm // WINDOW
    cur = pl.BlockSpec((1, N_PLANES, WINDOW, LANES), lambda i, j, h: (i, 0, j, h))
    prev = pl.BlockSpec((1, N_PLANES, WINDOW, LANES), lambda i, j, h: (i, 0, jnp.maximum(j - 1, 0), h))
    biases = _attention_biases()
    bias_specs = [pl.BlockSpec(a.shape, lambda i, j, h, nd=a.ndim: (0,) * nd) for a in biases]
    stats = pltpu.VMEM((3, SPAN, LANES), F32)
    return pl.pallas_call(
        _attn_kernel,
        grid=(b, n_span, ATTN_WIDTH // LANES),
        in_specs=[cur, cur, prev, cur, prev] + bias_specs,
        out_specs=cur,
        out_shape=jax.ShapeDtypeStruct(q.shape, BF16),
        scratch_shapes=[stats, stats, stats, pltpu.VMEM((2 * WINDOW, 2 * WINDOW), F32)],
        compiler_params=pltpu.CompilerParams(vmem_limit_bytes=VMEM_LIMIT),
        name="dilated_attention",
    )(q, k, k, v, v, *biases)


CONV_HALO = 32


def _conv_kernel(cur_ref, prev_ref, w_ref, b_ref, g_ref, bl_ref, o_ref, scr, *, chunk):
    tt = cur_ref.shape[1]
    has_prev = pl.program_id(1) > 0
    scr[0, 0:CONV_HALO, :] = jnp.where(has_prev, prev_ref[0], 0.0)
    scr[0, CONV_HALO:CONV_HALO + tt, :] = cur_ref[0]
    aligned_rows = tt + CONV_HALO - SUBLANES
    for s in range(1, SUBLANES):
        scr[s, 0:aligned_rows, :] = scr[0, s:s + aligned_rows, :]
    lead = CONV_HALO - (CONV_WIDTH - 1)
    for c0 in range(0, tt, chunk):
        acc = jnp.zeros((chunk, CONV_CHANNELS), F32)
        for j in range(CONV_WIDTH):
            s, a = (lead + j) % SUBLANES, (lead + j) // SUBLANES * SUBLANES
            acc = acc + w_ref[j:j + 1, :] * scr[s, c0 + a:c0 + a + chunk, :]
        y = acc + b_ref[...]
        mu = jnp.mean(y, axis=-1, keepdims=True)
        var = jnp.mean(jnp.square(y - mu), axis=-1, keepdims=True)
        yn = (y - mu) * lax.rsqrt(var + NORM_EPS) * g_ref[...] + bl_ref[...]
        o_ref[0, c0:c0 + chunk, :] = (yn * jax.nn.sigmoid(yn)).astype(BF16)


def _conformer_conv(glu, w_dw, b_dw, g_ln, b_ln):
    b, s, c = glu.shape
    tt = 512
    row = pl.BlockSpec((1, c), lambda i, j: (0, 0))
    return pl.pallas_call(
        functools.partial(_conv_kernel, chunk=128),
        grid=(b, s // tt),
        in_specs=[pl.BlockSpec((1, tt, c), lambda i, j: (i, j, 0)),
                  pl.BlockSpec((1, CONV_HALO, c), lambda i, j: (i, jnp.maximum(j * (tt // CONV_HALO) - 1, 0), 0)),
                  pl.BlockSpec((CONV_WIDTH, c), lambda i, j: (0, 0)), row, row, row],
        out_specs=pl.BlockSpec((1, tt, c), lambda i, j: (i, j, 0)),
        out_shape=jax.ShapeDtypeStruct((b, s, c), BF16),
        scratch_shapes=[pltpu.VMEM((SUBLANES, CONV_HALO + tt, c), F32)],
        name="conformer_conv",
    )(glu, glu, w_dw[:, 0, :], b_dw[None, :], g_ln[None, :], b_ln[None, :])


def _outproj_kernel(attn_ref, conv_ref, x_ref, permt_ref, woa_ref, woc_ref, gpost_ref, gffn_ref, wr_ref, br_ref,
                    h_ref, u_ref, idx_ref, gate_ref):
    tt = x_ref.shape[1]
    nat = []
    for sub in range(tt // PERM_TOKENS):
        rows = slice(sub * PERM_ROWS, (sub + 1) * PERM_ROWS)
        a = jnp.concatenate([attn_ref[0, p, rows, :] for p in range(N_PLANES)], axis=0)
        nat.append(jnp.dot(permt_ref[...], a, preferred_element_type=F32).astype(BF16))
    mix = (jnp.dot(jnp.concatenate(nat, axis=0), woa_ref[...], preferred_element_type=F32)
           + jnp.dot(conv_ref[0], woc_ref[...], preferred_element_type=F32))
    h = x_ref[0] + _rms(mix, gpost_ref[...])
    h_ref[0] = h
    u = _rms(h, gffn_ref[...])
    half = u.shape[1] // 2
    ub = u.astype(BF16).astype(F32)
    u_ref[...] = ((lax.bitcast_convert_type(ub[:, :half], jnp.uint32) >> 16)
                  | (lax.bitcast_convert_type(ub[:, half:], jnp.uint32) & jnp.uint32(0xFFFF0000)))
    u_hi = u.astype(BF16)
    u_lo = (u - u_hi.astype(F32)).astype(BF16)
    nt = (((1,), (1,)), ((), ()))
    by_hi = lax.dot_general(wr_ref[...], u_hi, nt, preferred_element_type=F32)
    by_lo = lax.dot_general(wr_ref[:N_EXPERTS], u_lo, nt, preferred_element_type=F32)
    logits = by_hi[:N_EXPERTS] + (by_hi[N_EXPERTS:] + by_lo) + br_ref[...]
    rows = lax.broadcasted_iota(jnp.int32, logits.shape, 0)
    vals = logits
    tops, idxs = [], []
    for _ in range(TOP_K):
        mx = jnp.max(vals, axis=0, keepdims=True)
        ix = jnp.min(jnp.where(vals == mx, rows, N_EXPERTS), axis=0, keepdims=True)
        tops.append(mx)
        idxs.append(ix)
        vals = jnp.where(rows == ix, -jnp.inf, vals)
    ex = [jnp.exp(t - tops[0]) for t in tops]
    den = ex[0] + ex[1] + ex[2] + ex[3]
    idx_ref[...] = jnp.concatenate(idxs, axis=0)
    gate_ref[...] = jnp.concatenate([e / den for e in ex] + [jnp.zeros((8 - TOP_K, tt), F32)], axis=0)


def _output_projection(attn, conv, x, w_out, g_mix_post, g_ffn_pre, w_router, b_router):
    b, s, d = x.shape
    tt = 512
    mc = tt // N_PLANES
    n_t = s // tt
    woa = w_out[:ATTN_WIDTH].astype(BF16)
    woc = w_out[ATTN_WIDTH:].astype(BF16)
    wr_hi = w_router.T.astype(BF16)
    wr_split = jnp.concatenate([wr_hi, (w_router.T - wr_hi.astype(F32)).astype(BF16)], axis=0)
    permt =jnp.asarray(_plane_permutation().T, BF16)
    const = lambda shape: pl.BlockSpec(shape, lambda i, j: (0, 0))
    flat = lambda w: pl.BlockSpec((tt, w), lambda i, j: (i * n_t + j, 0))
    lanes = lambda r: pl.BlockSpec((r, tt), lambda i, j: (0, i * n_t + j))
    return pl.pallas_call(
        _outproj_kernel,
        grid=(b, n_t),
        in_specs=[pl.BlockSpec((1, N_PLANES, mc, ATTN_WIDTH), lambda i, j: (i, 0, j, 0)),
                  pl.BlockSpec((1, tt, CONV_CHANNELS), lambda i, j: (i, j, 0)),
                  pl.BlockSpec((1, tt, d), lambda i, j: (i, j, 0)),
                  const(permt.shape), const(woa.shape), const(woc.shape), const((1, d)), const((1, d)),
                  const((2 * N_EXPERTS, d)), const((N_EXPERTS, 1))],
        out_specs=[pl.BlockSpec((1, tt, d), lambda i, j: (i, j, 0)), flat(d // 2), lanes(TOP_K), lanes(8)],
        out_shape=[jax.ShapeDtypeStruct((b, s, d), F32),
                   jax.ShapeDtypeStruct((b * s, d // 2), jnp.uint32),
                   jax.ShapeDtypeStruct((TOP_K, b * s), jnp.int32),
                   jax.ShapeDtypeStruct((8, b * s), F32)],
        compiler_params=pltpu.CompilerParams(vmem_limit_bytes=VMEM_LIMIT),
        name="output_projection_router",
    )(attn, conv, x, permt, woa, woc, g_mix_post[None, :], g_ffn_pre[None, :], wr_split, b_router[:, None])


def _route_kernel(idx_ref, dest_ref, pend_ref, pad_ref, blk_ref, carry, pstart):
    phase = pl.program_id(0)
    step = pl.program_id(1)
    tt = idx_ref.shape[1]
    rows = lax.broadcasted_iota(jnp.int32, (N_EXPERTS, tt), 0)
    hot = [rows == idx_ref[k:k + 1, :] for k in range(TOP_K)]
    memb = sum(h.astype(F32) for h in hot)

    @pl.when((phase == 0) & (step == 0))
    def _():
        carry[...] = jnp.zeros_like(carry)

    @pl.when((phase == 1) & (step == 0))
    def _():
        counts = carry[...]
        padded = jnp.floor((counts + (EXPERT_ROWS - 1)) * (1.0 / EXPERT_ROWS)) * EXPERT_ROWS
        tri = (lax.broadcasted_iota(jnp.int32, (N_EXPERTS, N_EXPERTS), 1)
               <= lax.broadcasted_iota(jnp.int32, (N_EXPERTS, N_EXPERTS), 0)).astype(F32)
        pend = jnp.dot(tri, padded, precision=lax.Precision.HIGHEST, preferred_element_type=F32)
        pstart[...] = pend - padded
        pend_ref[...] = pend.astype(jnp.int32)
        pad_ref[...] = (pend - padded + counts).astype(jnp.int32)
        starts = lax.broadcasted_iota(jnp.int32, (N_EXPERTS, blk_ref.shape[1]), 1) * EXPERT_ROWS
        ended = (pend.astype(jnp.int32) <= starts).astype(jnp.int32)
        blk_ref[...] = jnp.minimum(jnp.sum(ended, axis=0, keepdims=True), N_EXPERTS - 1)
        carry[...] = jnp.zeros_like(carry)

    @pl.when(phase == 1)
    def _():
        earlier = (lax.broadcasted_iota(jnp.int32, (tt, tt), 0)
                   < lax.broadcasted_iota(jnp.int32, (tt, tt), 1)).astype(BF16)
        row = jnp.dot(memb.astype(BF16), earlier, preferred_element_type=F32) + (carry[...] + pstart[...])
        dest_ref[...] = jnp.concatenate(
            [jnp.sum(jnp.where(h, row, 0.0), axis=0, keepdims=True) for h in hot], axis=0).astype(jnp.int32)

    carry[...] = carry[...] + jnp.sum(memb, axis=1, keepdims=True)


def _routing(idx, n_blk):
    _, t = idx.shape
    tt = 512
    blk_lanes = -(-n_blk // LANES) * LANES
    return pl.pallas_call(
        _route_kernel,
        grid=(2, t // tt),
        in_specs=[pl.BlockSpec((TOP_K, tt), lambda ph, i: (0, i))],
        out_specs=[pl.BlockSpec((TOP_K, tt), lambda ph, i: (0, i * ph)),
                   pl.BlockSpec((N_EXPERTS, 1), lambda ph, i: (0, 0)),
                   pl.BlockSpec((N_EXPERTS, 1), lambda ph, i: (0, 0)),
                   pl.BlockSpec((1, blk_lanes), lambda ph, i: (0, 0))],
        out_shape=[jax.ShapeDtypeStruct((TOP_K, t), jnp.int32), jax.ShapeDtypeStruct((N_EXPERTS, 1), jnp.int32),
                   jax.ShapeDtypeStruct((N_EXPERTS, 1), jnp.int32), jax.ShapeDtypeStruct((1, blk_lanes), jnp.int32)],
        scratch_shapes=[pltpu.VMEM((N_EXPERTS, 1), F32), pltpu.VMEM((N_EXPERTS, 1), F32)],
        compiler_params=pltpu.CompilerParams(dimension_semantics=("arbitrary", "arbitrary")),
        name="moe_routing",
    )(idx)


SC_SCAN = 4096


def _sc_token_of_row(dest_flat, cap, n_tok):
    from jax.experimental.pallas import tpu_sc as plsc
    assert n_tok & (n_tok - 1) == 0
    info = plsc.get_sparse_core_info()
    n_core, n_sub, lanes = info.num_cores, info.num_subcores, info.num_lanes
    n_src = dest_flat.shape[0]
    per_w = cap // (n_core * n_sub)
    assert per_w * n_core * n_sub == cap and per_w % lanes == 0 and n_src % SC_SCAN == 0
    mesh = plsc.VectorSubcoreMesh(core_axis_name="c", subcore_axis_name="s")

    def body(dest_hbm, out_hbm, src_v, out_v):
        lo = (lax.axis_index("s") * n_core + lax.axis_index("c")) * per_w
        zero = jnp.zeros((lanes,), jnp.int32)

        @pl.loop(0, per_w // lanes)
        def _(i):
            out_v[pl.ds(pl.multiple_of(i * lanes, lanes), lanes)] = zero
        lane = lax.iota(jnp.int32, lanes)

        @pl.loop(0, n_src // SC_SCAN)
        def _(c):
            pltpu.sync_copy(dest_hbm.at[pl.ds(pl.multiple_of(c * SC_SCAN, 8), SC_SCAN)], src_v)

            @pl.loop(0, SC_SCAN // lanes)
            def _(v):
                local = src_v[pl.ds(pl.multiple_of(v * lanes, lanes), lanes)] - lo
                mine = lax.bitcast_convert_type(local, jnp.uint32) < jnp.uint32(per_w)
                tok = (c * SC_SCAN + v * lanes + lane) & (n_tok - 1)
                plsc.store_scatter(out_v, [local], tok, mask=mine)
        pltpu.sync_copy(out_v, out_hbm.at[pl.ds(pl.multiple_of(lo, 8), per_w)])

    return pl.kernel(body, mesh=mesh, out_type=jax.ShapeDtypeStruct((cap,), jnp.int32),
                     scratch_types=[pltpu.VMEM((SC_SCAN,), jnp.int32), pltpu.VMEM((per_w,), jnp.int32)],
                     compiler_params=pltpu.CompilerParams(needs_layout_passes=False),
                     name="moe_sc_token_of_row")(dest_flat)


def _expert_kernel(blk_e_ref, nused_ref, xb_ref, wgu_ref, bgu_ref, wd_ref, bd_ref, *rest):
    yb_ref, done_ref, wgu_b, wd_b = rest[-4:]
    j = pl.program_id(0)
    done_ref[...] = jnp.zeros_like(done_ref)
    nused = nused_ref[0]
    d_ff = wd_ref.shape[1]

    new_expert = (j == 0) | (blk_e_ref[j] != blk_e_ref[jnp.maximum(j - 1, 0)])

    @pl.when((j < nused) & new_expert)
    def _():
        def cast(src, dst):
            def body(i, carry):
                rows = pl.ds(pl.multiple_of(i * LANES, LANES), LANES)
                dst[rows, :] = src[0, rows, :].astype(BF16)
                return carry
            lax.fori_loop(0, dst.shape[0] // LANES, body, 0)
        cast(wgu_ref, wgu_b)
        cast(wd_ref, wd_b)

    @pl.when(j < nused)
    def _():
        word = xb_ref[...]
        half = word.shape[1]
        lo = lax.bitcast_convert_type(word << 16, F32).astype(BF16)
        hi = lax.bitcast_convert_type(word & jnp.uint32(0xFFFF0000), F32).astype(BF16)
        gu = (jnp.dot(lo, wgu_b[:half, :], preferred_element_type=F32)
              + jnp.dot(hi, wgu_b[half:, :], preferred_element_type=F32) + bgu_ref[0])
        gate = jnp.minimum(gu[:, :d_ff], SWIGLU_LIMIT)
        up = jnp.clip(gu[:, d_ff:], -SWIGLU_LIMIT, SWIGLU_LIMIT)
        hid = (up + 1.0) * (gate * jax.nn.sigmoid(gate * SWIGLU_ALPHA))
        yb_ref[...] = jnp.dot(hid.astype(BF16), wd_b[...], preferred_element_type=F32) + bd_ref[0]

    @pl.when(j >= nused)
    def _():
        yb_ref[...] = jnp.zeros_like(yb_ref)


def _experts(xb, chunk, n_chunk, yb_prev, blk_e, nused, w_gate_up, b_gate_up, w_down, b_down):
    rows, d_in = xb.shape
    e, d, ff2 = w_gate_up.shape
    d_ff = w_down.shape[1]
    n_blk = rows // EXPERT_ROWS
    live = lambda j, nu: jnp.maximum(jnp.minimum(j, nu[0] - 1), 0)
    expert = lambda j, be, nu: (be[live(j, nu)], 0, 0)
    in_specs = [pl.BlockSpec((EXPERT_ROWS, d_in), lambda j, be, nu: (live(j, nu), 0)),
                pl.BlockSpec((1, d, ff2), expert), pl.BlockSpec((1, 1, ff2), expert),
                pl.BlockSpec((1, d_ff, d), expert), pl.BlockSpec((1, 1, d), expert)]
    args = [blk_e, nused, xb, w_gate_up, b_gate_up[:, None, :], w_down, b_down[:, None, :]]
    aliases = {}
    if yb_prev is not None:
        in_specs.append(pl.BlockSpec(memory_space=pl.ANY))
        args.append(yb_prev)
        aliases = {len(args) - 1: 0}
    grid_spec = pltpu.PrefetchScalarGridSpec(
        num_scalar_prefetch=2,
        grid=(n_blk,),
        in_specs=in_specs,
        out_specs=[pl.BlockSpec((EXPERT_ROWS, d), lambda j, be, nu: (chunk * n_blk + j, 0)),
                   pl.BlockSpec((SUBLANES, LANES), lambda j, be, nu: (0, 0))],
        scratch_shapes=[pltpu.VMEM((d, ff2), BF16), pltpu.VMEM((d_ff, d), BF16)],
    )
    return pl.pallas_call(
        _expert_kernel,
        grid_spec=grid_spec,
        out_shape=[jax.ShapeDtypeStruct((n_chunk * rows, d), F32), jax.ShapeDtypeStruct((SUBLANES, LANES), F32)],
        input_output_aliases=aliases,
        compiler_params=pltpu.CompilerParams(dimension_semantics=("arbitrary",), vmem_limit_bytes=VMEM_LIMIT),
        name="moe_experts",
    )(*args)


SC_BUFFER_BYTES = 128 * 1024
MOE_CHUNKS = 8
MOE_GROUPS = 4


def _sc_gather_rows(table, idx, part, n_part, after=None):
    from jax.experimental.pallas import tpu_sc as plsc
    n_rows = idx.shape[0] // n_part
    d = table.shape[1]
    info = plsc.get_sparse_core_info()
    n_core, n_sub = info.num_cores, info.num_subcores
    per_w = n_rows // (n_core * n_sub)
    fit = SC_BUFFER_BYTES // (d * table.dtype.itemsize)
    g_rows = max(r for r in (8, 16, 32, 64, 128) if r <= fit and per_w % r == 0)
    n_chunk = per_w // g_rows
    assert per_w * n_core * n_sub == n_rows
    mesh = plsc.VectorSubcoreMesh(core_axis_name="c", subcore_axis_name="s")

    def body(table_hbm, idx_hbm, *rest):
        out_hbm, idx_v, buf0, buf1, sem0, sem1 = rest[-6:]
        base = (lax.axis_index("s") * n_core + lax.axis_index("c")) * per_w
        pltpu.sync_copy(idx_hbm.at[pl.ds(pl.multiple_of(part * n_rows + base, 8), per_w)], idx_v)

        def gather(chunk, buf, sem):
            rows = idx_v.at[pl.ds(pl.multiple_of(chunk * g_rows, 8), g_rows)]
            return pltpu.make_async_copy(table_hbm.at[rows], buf, sem)

        def write(chunk, buf):
            pltpu.sync_copy(buf, out_hbm.at[pl.ds(pl.multiple_of(base + chunk * g_rows, 8), g_rows)])

        gather(0, buf0, sem0).start()

        @pl.loop(0, n_chunk // 2)
        def _(i):
            gather(2 * i + 1, buf1, sem1).start()
            gather(2 * i, buf0, sem0).wait()
            write(2 * i, buf0)

            @pl.when(2 * i + 2 < n_chunk)
            def _():
                gather(2 * i + 2, buf0, sem0).start()
            gather(2 * i + 1, buf1, sem1).wait()
            write(2 * i + 1, buf1)

        if n_chunk % 2:
            gather(n_chunk - 1, buf0, sem0).wait()
            write(n_chunk - 1, buf0)

    return pl.kernel(
        body, mesh=mesh, out_type=jax.ShapeDtypeStruct((n_rows, d), table.dtype),
        scratch_types=[pltpu.VMEM((per_w,), jnp.int32), pltpu.VMEM((g_rows, d), table.dtype),
                       pltpu.VMEM((g_rows, d), table.dtype), pltpu.SemaphoreType.DMA, pltpu.SemaphoreType.DMA],
        name="moe_sc_gather",
    )(table, idx, *([] if after is None else [after]))


def _sc_weighted_rows(table, idx, gates, group, n_group):
    from jax.experimental.pallas import tpu_sc as plsc
    d = table.shape[1]
    info = plsc.get_sparse_core_info()
    n_core, n_sub, lanes = info.num_cores, info.num_subcores, info.num_lanes
    n_all = idx.shape[0] // TOP_K
    n_tok = n_all // n_group
    per_w = n_tok // (n_core * n_sub)
    w = 8
    n_chunk = per_w // w
    assert per_w * n_core * n_sub == n_tok and n_chunk * w == per_w and n_chunk % 2 == 0
    mesh = plsc.VectorSubcoreMesh(core_axis_name="c", subcore_axis_name="s")

    def body(table_hbm, idx_hbm, g_hbm, out_hbm, *scr):
        idx_v, g_v, rb, ob = scr[0:4], scr[4:8], (scr[8:12], scr[12:16]), scr[16:18]
        sems, wsems = scr[18:20], scr[20:22]
        base = (lax.axis_index("s") * n_core + lax.axis_index("c")) * per_w
        for k in range(TOP_K):
            off = pl.multiple_of(k * n_all + group * n_tok + base, 8)
            pltpu.sync_copy(idx_hbm.at[pl.ds(off, per_w)], idx_v[k])
            pltpu.sync_copy(g_hbm.at[pl.ds(off, per_w)], g_v[k])

        def gathers(chunk, slot):
            rows = pl.ds(pl.multiple_of(chunk * w, 8), w)
            return [pltpu.make_async_copy(table_hbm.at[idx_v[k].at[rows]], rb[slot][k], sems[slot])
                    for k in range(TOP_K)]

        def write(chunk, slot):
            return pltpu.make_async_copy(ob[slot], out_hbm.at[pl.ds(pl.multiple_of(base + chunk * w, 8), w)],
                                         wsems[slot])

        def combine(chunk, slot):
            @pl.loop(0, w)
            def _(i):
                token = jnp.full((lanes,), chunk * w + i, jnp.int32)
                gs = [plsc.load_gather(g_v[k], [token]) for k in range(TOP_K)]
                for j in range(d // lanes):
                    cols = pl.ds(j * lanes, lanes)
                    acc = gs[0] * rb[slot][0][i, cols]
                    for k in range(1, TOP_K):
                        acc = acc + gs[k] * rb[slot][k][i, cols]
                    ob[slot][i, cols] = acc
            write(chunk, slot).start()

        for c in gathers(0, 0):
            c.start()

        @pl.loop(0, n_chunk // 2)
        def _(i):
            for c in gathers(2 * i + 1, 1):
                c.start()
            for c in gathers(2 * i, 0):
                c.wait()

            @pl.when(i > 0)
            def _():
                write(2 * i - 2, 0).wait()
            combine(2 * i, 0)

            @pl.when(2 * i + 2 < n_chunk)
            def _():
                for c in gathers(2 * i + 2, 0):
                    c.start()
            for c in gathers(2 * i + 1, 1):
                c.wait()

            @pl.when(i > 0)
            def _():
                write(2 * i - 1, 1).wait()
            combine(2 * i + 1, 1)

        write(n_chunk - 2, 0).wait()
        write(n_chunk - 1, 1).wait()

    scratch = ([pltpu.VMEM((per_w,), jnp.int32)] * TOP_K + [pltpu.VMEM((per_w,), F32)] * TOP_K
               + [pltpu.VMEM((w, d), F32)] * (2 * TOP_K + 2) + [pltpu.SemaphoreType.DMA] * 4)
    return pl.kernel(body, mesh=mesh, out_type=jax.ShapeDtypeStruct((n_tok, d), F32), scratch_types=scratch,
                     compiler_params=pltpu.CompilerParams(needs_layout_passes=False),
                     name="moe_sc_combine")(table, idx, gates)


def _combine_kernel(h_ref, p_ref, y_ref, gffn_ref, wple_ref, wpg_ref, gple_ref, o_ref):
    h2 = h_ref[...] + _rms(y_ref[...], gffn_ref[...])
    ple = (jnp.dot(p_ref[...].astype(BF16), wple_ref[...], preferred_element_type=F32)
           * jax.nn.sigmoid(jnp.dot(h2.astype(BF16), wpg_ref[...], preferred_element_type=F32)))
    o_ref[...] = h2 + _rms(ple, gple_ref[...])


def _combine_kernel_aliased(h_ref, p_ref, y_ref, gffn_ref, wple_ref, wpg_ref, gple_ref, prev_ref, o_ref):
    del prev_ref
    _combine_kernel(h_ref, p_ref, y_ref, gffn_ref, wple_ref, wpg_ref, gple_ref, o_ref)


def _combine(group, n_group, out_prev, h, p, y, g_ffn_post, w_ple_bf16, w_ple_gate_bf16, g_ple_post):
    t, d = h.shape
    tt = 256
    n_tg = t // n_group // tt
    const = lambda shape: pl.BlockSpec(shape, lambda i: (0, 0))
    tok = lambda w: pl.BlockSpec((tt, w), lambda i: (group * n_tg + i, 0))
    in_specs = [tok(d), tok(p.shape[1]), pl.BlockSpec((tt, d), lambda i: (i, 0)),
                const((1, d)), const(w_ple_bf16.shape), const(w_ple_gate_bf16.shape), const((1, d))]
    args = [h, p, y, g_ffn_post[None, :], w_ple_bf16, w_ple_gate_bf16, g_ple_post[None, :]]
    body, aliases = _combine_kernel, {}
    if out_prev is not None:
        in_specs.append(pl.BlockSpec(memory_space=pl.ANY))
        args.append(out_prev)
        body, aliases = _combine_kernel_aliased, {len(args) - 1: 0}
    return pl.pallas_call(
        body,
        grid=(n_tg,),
        in_specs=in_specs,
        out_specs=tok(d),
        out_shape=jax.ShapeDtypeStruct((t, d), F32),
        input_output_aliases=aliases,
        compiler_params=pltpu.CompilerParams(vmem_limit_bytes=VMEM_LIMIT),
        name="moe_combine_ple",
    )(*args)


def _layer(h, p, positions, g_mix_pre, w_in, w_dw, b_dw, g_conv_ln, b_conv_ln, w_out, g_mix_post, g_ffn_pre,
           w_router, b_router, w_gate_up, b_gate_up, w_down, b_down, g_ffn_post, w_ple, w_ple_gate, g_ple_post):
    b, s, d = h.shape
    t = b * s
    cos_t, sin_t = _rope_tables(positions)
    q, k, v, glu = _input_projection(h, cos_t, sin_t, g_mix_pre, w_in)
    attn = _dilated_attention(q, k, v)
    conv = _conformer_conv(glu, w_dw, b_dw, g_conv_ln, b_conv_ln)
    h1, u_ffn, idx, gates = _output_projection(attn, conv, h, w_out, g_mix_post, g_ffn_pre, w_router, b_router)
    cap = t * TOP_K + N_EXPERTS * EXPERT_ROWS
    n_blk = cap // EXPERT_ROWS
    dest, pend, pad, blk = _routing(idx, n_blk)
    blk_e = blk[0, :n_blk]
    nused = pend[N_EXPERTS - 1] // EXPERT_ROWS
    dest_flat = dest.reshape(TOP_K * t)
    tok_of_row = _sc_token_of_row(dest_flat, cap, t)
    rows_c = cap // MOE_CHUNKS
    blk_c = n_blk // MOE_CHUNKS
    yb = done = None
    xb_next = _sc_gather_rows(u_ffn, tok_of_row, 0, MOE_CHUNKS)
    for c in range(MOE_CHUNKS):
        xb_c = xb_next
        if c + 1 < MOE_CHUNKS:
            xb_next = _sc_gather_rows(u_ffn, tok_of_row, c + 1, MOE_CHUNKS, after=done)
        nused_c = jnp.clip(nused - c * blk_c, 0, blk_c)
        yb, done = _experts(xb_c, c, MOE_CHUNKS, yb, blk_e[c * blk_c:(c + 1) * blk_c], nused_c,
                            w_gate_up, b_gate_up, w_down, b_down)
    t_g = t // MOE_GROUPS
    w_ple_bf16, w_ple_gate_bf16 = w_ple.astype(BF16), w_ple_gate.astype(BF16)
    h1_flat, p_flat = h1.reshape(t, d), p.reshape(t, -1)
    gate_flat = gates[:TOP_K].reshape(TOP_K * t)
    out = None
    for g in range(MOE_GROUPS):
        y_g = _sc_weighted_rows(yb, dest_flat, gate_flat, g, MOE_GROUPS)
        out = _combine(g, MOE_GROUPS, out, h1_flat, p_flat, y_g, g_ffn_post, w_ple_bf16, w_ple_gate_bf16,
                       g_ple_post)
    return out.reshape(b, s, d)


def kernel(x, p, positions, g_mix_pre, w_in, w_dw, b_dw, g_conv_ln, b_conv_ln, w_out, g_mix_post, g_ffn_pre,
           w_router, b_router, w_gate_up, b_gate_up, w_down, b_down, g_ffn_post, w_ple, w_ple_gate, g_ple_post):
    h = x
    for i in range(p.shape[0]):
        h = _layer(h, p[i], positions, g_mix_pre[i], w_in[i], w_dw[i], b_dw[i], g_conv_ln[i], b_conv_ln[i],
                   w_out[i], g_mix_post[i], g_ffn_pre[i], w_router[i], b_router[i], w_gate_up[i], b_gate_up[i],
                   w_down[i], b_down[i], g_ffn_post[i], w_ple[i], w_ple_gate[i], g_ple_post[i])
    return h
```

```python
import functools

import numpy as np
import jax
import jax.numpy as jnp
from jax import lax
from jax.experimental import pallas as pl
from jax.experimental.pallas import tpu as pltpu

F32 = jnp.float32
BF16 = jnp.bfloat16

HEAD_DIM = 64
N_HEADS = 12
ATTN_WIDTH = N_HEADS * HEAD_DIM
CONV_CHANNELS = 256
CONV_WIDTH = 31
ROPE_DIM = HEAD_DIM // 4
ROPE_THETA = 500000.0
N_EXPERTS = 32
TOP_K = 4
SWIGLU_LIMIT = 7.0
SWIGLU_ALPHA = 1.702
NORM_EPS = 1e-6
WINDOW = 128
N_PLANES = 16
SPAN = N_PLANES * WINDOW
LANES = 128
SUBLANES = 8
NEG = -1e30
EXPERT_ROWS = 256
VMEM_LIMIT = 56 * 1024 * 1024


def _residue_of_plane(p):
    return 4 * (p % 4) + p // 4


def _rms(xv, g):
    var = jnp.mean(xv * xv, axis=-1, keepdims=True)
    return xv * lax.rsqrt(var + NORM_EPS) * g


def _rope_kernel(pos_ref, invf_ref, expand_ref, one_ref, sgn_ref, c_ref, s_ref):
    tn = (((0,), (0,)), ((), ()))
    for p in range(N_PLANES):
        ang = invf_ref[...] * pos_ref[0, p:p + 1, :].astype(F32)
        spread = lambda t: lax.dot_general(t, expand_ref[...], tn, precision=lax.Precision.HIGHEST,
                                           preferred_element_type=F32)
        c_ref[0, p] = spread(jnp.cos(ang)) + one_ref[...]
        s_ref[0, p] = spread(jnp.sin(ang)) * sgn_ref[...]


def _rope_tables(positions):
    b, s = positions.shape
    sm = s // N_PLANES
    mt = min(sm, 128)
    plane_res = np.array([_residue_of_plane(p) for p in range(N_PLANES)])
    pos_planes = positions.reshape(b, sm, N_PLANES).transpose(0, 2, 1)[:, plane_res]
    half = ROPE_DIM // 2
    lane = np.arange(LANES) % HEAD_DIM
    inv_freq = ROPE_THETA ** (-jnp.arange(0, ROPE_DIM, 2, dtype=F32) / ROPE_DIM)
    rotary = lane < ROPE_DIM
    expand = jnp.asarray((np.arange(half)[:, None] == lane[None, :] % half) & rotary[None, :], F32)
    one = jnp.asarray(~rotary, F32)[None, :]
    sgn = jnp.asarray(np.where(lane < half, -1.0, 1.0), F32)[None, :]
    row = pl.BlockSpec((1, LANES), lambda i, j: (0, 0))
    out = pl.BlockSpec((1, N_PLANES, mt, LANES), lambda i, j: (i, 0, j, 0))
    return pl.pallas_call(
        _rope_kernel,
        grid=(b, sm // mt),
        in_specs=[pl.BlockSpec((1, N_PLANES, mt), lambda i, j: (i, 0, j)),
                  pl.BlockSpec((half, 1), lambda i, j: (0, 0)), pl.BlockSpec((half, LANES), lambda i, j: (0, 0)),
                  row, row],
        out_specs=[out, out],
        out_shape=[jax.ShapeDtypeStruct((b, N_PLANES, sm, LANES), F32)] * 2,
        name="rope_tables",
    )(pos_planes, inv_freq[:, None], expand, one, sgn)


PERM_TOKENS = 256
PERM_ROWS = PERM_TOKENS // N_PLANES


def _plane_permutation():
    perm = np.zeros((PERM_TOKENS, PERM_TOKENS), np.float32)
    for p in range(N_PLANES):
        for ml in range(PERM_ROWS):
            perm[PERM_ROWS * p + ml, N_PLANES * ml + _residue_of_plane(p)] = 1.0
    return perm


def _inproj_kernel(x_ref, c_ref, s_ref, g_ref, perm_ref, wqkv_ref, wc_ref, q_ref, k_ref, v_ref, glu_ref):
    g = g_ref[...]
    tt = x_ref.shape[1]
    lane = lax.broadcasted_iota(jnp.int32, (1, LANES), 1) % HEAD_DIM
    first_half = lane < ROPE_DIM // 2

    def rotary(t, cos, sin):
        outs = []
        for j in range(ATTN_WIDTH // LANES):
            tj = t[:, j * LANES:(j + 1) * LANES]
            partner = jnp.where(first_half, pltpu.roll(tj, LANES - ROPE_DIM // 2, 1),
                                pltpu.roll(tj, ROPE_DIM // 2, 1))
            outs.append(tj * cos + partner * sin)
        return jnp.concatenate(outs, axis=1)

    un = _rms(x_ref[0], g).astype(BF16)
    pc = jnp.dot(un, wc_ref[...], preferred_element_type=F32)
    glu_ref[0] = pc[:, :CONV_CHANNELS] * jax.nn.sigmoid(pc[:, CONV_CHANNELS:])

    for sub in range(tt // PERM_TOKENS):
        rows = slice(sub * PERM_ROWS, (sub + 1) * PERM_ROWS)
        u = jnp.dot(perm_ref[...], un[sub * PERM_TOKENS:(sub + 1) * PERM_TOKENS],
                    preferred_element_type=F32).astype(BF16)
        cos = jnp.concatenate([c_ref[0, p, rows, :] for p in range(N_PLANES)], axis=0)
        sin = jnp.concatenate([s_ref[0, p, rows, :] for p in range(N_PLANES)], axis=0)
        proj = jnp.dot(u, wqkv_ref[...], preferred_element_type=F32)
        q = (rotary(proj[:, :ATTN_WIDTH], cos, sin) * (HEAD_DIM ** -0.5)).astype(BF16)
        k = rotary(proj[:, ATTN_WIDTH:2 * ATTN_WIDTH], cos, sin).astype(BF16)
        v = proj[:, 2 * ATTN_WIDTH:].astype(BF16)
        for p in range(N_PLANES):
            chunk = slice(p * PERM_ROWS, (p + 1) * PERM_ROWS)
            q_ref[0, p, rows, :] = q[chunk]
            k_ref[0, p, rows, :] = k[chunk]
            v_ref[0, p, rows, :] = v[chunk]


def _input_projection(x, cos_t, sin_t, g_mix_pre, w_in):
    b, s, d = x.shape
    tt = 1024
    mc = tt // N_PLANES
    wqkv = w_in[:, :3 * ATTN_WIDTH].astype(BF16)
    wc = w_in[:, 3 * ATTN_WIDTH:].astype(BF16)
    perm = jnp.asarray(_plane_permutation(), BF16)
    plane = lambda w: pl.BlockSpec((1, N_PLANES, mc, w), lambda i, j: (i, 0, j, 0))
    plane_shape = jax.ShapeDtypeStruct((b, N_PLANES, s // N_PLANES, ATTN_WIDTH), BF16)
    tok = lambda w: pl.BlockSpec((1, tt, w), lambda i, j: (i, j, 0))
    const = lambda shape: pl.BlockSpec(shape, lambda i, j: (0, 0))
    return pl.pallas_call(
        _inproj_kernel,
        grid=(b, s // tt),
        in_specs=[tok(d), plane(LANES), plane(LANES), const((1, d)), const(perm.shape),
                  const(wqkv.shape), const(wc.shape)],
        out_specs=[plane(ATTN_WIDTH), plane(ATTN_WIDTH), plane(ATTN_WIDTH), tok(CONV_CHANNELS)],
        out_shape=[plane_shape, plane_shape, plane_shape,
                   jax.ShapeDtypeStruct((b, s, CONV_CHANNELS), F32)],
        compiler_params=pltpu.CompilerParams(vmem_limit_bytes=VMEM_LIMIT),
        name="input_projection",
    )(x, cos_t, sin_t, g_mix_pre[None, :], perm, wqkv, wc)


def _attention_biases():
    band = lambda j: np.where((j >= 0) & (j <= WINDOW), 0.0, NEG).astype(np.float32)
    cols = lambda m: np.where(m, NEG, 0.0).astype(np.float32)
    twice = lambda a: np.concatenate([a, a], axis=0)
    mq = np.arange(WINDOW)[:, None]
    kj = np.arange(2 * WINDOW)[None, :]
    j16 = mq + WINDOW - kj
    prev16 = kj < WINDOW
    row = np.arange(128)[:, None]
    col = np.arange(256)[None, :]
    j4 = 4 * (row % 32 - (col % 64 - 32)) + row // 32 - col // 64
    prev4 = col % 64 < 32
    row = np.arange(256)[:, None]
    col = np.arange(512)[None, :]
    res = np.vectorize(_residue_of_plane)
    j1 = 16 * (row % 16 - (col % 32 - 16)) + res(row // 16) - res(col // 32)
    prev1 = col % 32 < 16
    b1 = np.stack([twice(band(j1[:128])), twice(band(j1[128:]))])
    return [jnp.asarray(a) for a in (twice(band(j16)), twice(band(j4)), b1, cols(prev16), cols(prev4), cols(prev1))]


UNROLL = 8


def _attn_kernel(q_ref, kc_ref, kp_ref, vc_ref, vp_ref, b16_ref, b4_ref, b1_ref, p16_ref, p4_ref, p1_ref,
                 o_ref, m_scr, l_scr, a_scr, bias16_scr):
    no_prev = (pl.program_id(1) == 0).astype(F32)
    head0 = lax.broadcasted_iota(jnp.int32, (1, LANES), 1) < HEAD_DIM
    bias16_scr[...] = b16_ref[...] + no_prev * p16_ref[...]

    def tile(qt, kt, vt, bias):
        n = qt.shape[0]
        zero = jnp.zeros_like(qt)
        q2 = jnp.concatenate([jnp.where(head0, qt, zero), jnp.where(head0, zero, qt)], axis=0)
        s = lax.dot_general(q2, kt, (((1,), (1,)), ((), ())), preferred_element_type=F32) + bias
        m = jnp.max(s, axis=-1, keepdims=True)
        e = jnp.exp(s - m).astype(BF16)
        va = jnp.concatenate([vt, jnp.ones_like(vt)], axis=1)
        o = jnp.dot(e, va, preferred_element_type=F32)
        pick = lambda top, bot: jnp.where(head0, top, bot)
        mm = pick(jnp.broadcast_to(m[:n], (n, LANES)), jnp.broadcast_to(m[n:], (n, LANES)))
        return mm, pick(o[:n, LANES:], o[n:, LANES:]), pick(o[:n, :LANES], o[n:, :LANES])

    def put(branch, start, size, stats, off):
        for scr, val in zip((m_scr, l_scr, a_scr), stats):
            scr[branch, pl.ds(start, size), :] = val[off:off + size]

    def body16(i, carry):
        for p in [UNROLL * i + u for u in range(UNROLL)]:
            kt = jnp.concatenate([kp_ref[0, p], kc_ref[0, p]], axis=0)
            vt = jnp.concatenate([vp_ref[0, p], vc_ref[0, p]], axis=0)
            put(0, pl.multiple_of(p * WINDOW, WINDOW), WINDOW, tile(q_ref[0, p], kt, vt, bias16_scr[...]), 0)
        return carry
    lax.fori_loop(0, N_PLANES // UNROLL, body16, 0)

    def body4(g, carry):
        for c, i in [(2 * g + cc, ii) for cc in range(2) for ii in range(4)]:
            qt = jnp.concatenate([q_ref[0, 4 * c + a, 32 * i:32 * i + 32, :] for a in range(4)], axis=0)
            if i == 0:
                ks = [x for a in range(4) for x in (kp_ref[0, 4 * c + a, 96:128, :], kc_ref[0, 4 * c + a, 0:32, :])]
                vs = [x for a in range(4) for x in (vp_ref[0, 4 * c + a, 96:128, :], vc_ref[0, 4 * c + a, 0:32, :])]
                bias = b4_ref[...] + no_prev * p4_ref[...]
            else:
                ks = [kc_ref[0, 4 * c + a, 32 * i - 32:32 * i + 32, :] for a in range(4)]
                vs = [vc_ref[0, 4 * c + a, 32 * i - 32:32 * i + 32, :] for a in range(4)]
                bias = b4_ref[...]
            stats = tile(qt, jnp.concatenate(ks, axis=0), jnp.concatenate(vs, axis=0), bias)
            for a in range(4):
                put(1, pl.multiple_of((4 * c + a) * WINDOW + 32 * i, 32), 32, stats, 32 * a)
        return carry
    lax.fori_loop(0, 2, body4, 0)

    def tile1(i, first):
        rq = pl.ds(pl.multiple_of(16 * i, 16), 16)
        if first:
            ks = [x for p in range(N_PLANES) for x in (kp_ref[0, p, 112:128, :], kc_ref[0, p, 0:16, :])]
            vs = [x for p in range(N_PLANES) for x in (vp_ref[0, p, 112:128, :], vc_ref[0, p, 0:16, :])]
        else:
            rk = pl.ds(pl.multiple_of(16 * i - 16, 16), 32)
            ks = [kc_ref[0, p, rk, :] for p in range(N_PLANES)]
            vs = [vc_ref[0, p, rk, :] for p in range(N_PLANES)]
        kt = jnp.concatenate(ks, axis=0)
        vt = jnp.concatenate(vs, axis=0)
        for half in range(2):
            planes = range(8 * half, 8 * half + 8)
            qt = jnp.concatenate([q_ref[0, p, rq, :] for p in planes], axis=0)
            bias = b1_ref[half] + no_prev * p1_ref[...] if first else b1_ref[half]
            stats = tile(qt, kt, vt, bias)
            for p in planes:
                put(2, pl.multiple_of(p * WINDOW + 16 * i, 16), 16, stats, 16 * (p - 8 * half))

    tile1(0, True)
    tile1(1, False)

    def body1(g, carry):
        tile1(2 * g, False)
        tile1(2 * g + 1, False)
        return carry
    lax.fori_loop(1, WINDOW // 32, body1, 0)

    def combine(i, carry):
        for p in (2 * i, 2 * i + 1):
            rows = pl.ds(pl.multiple_of(p * WINDOW, WINDOW), WINDOW)
            ms = [m_scr[b, rows, :] for b in range(3)]
            mx = jnp.maximum(jnp.maximum(ms[0], ms[1]), ms[2])
            ws = [jnp.exp(m - mx) for m in ms]
            den = ws[0] * l_scr[0, rows, :] + ws[1] * l_scr[1, rows, :] + ws[2] * l_scr[2, rows, :]
            num = ws[0] * a_scr[0, rows, :] + ws[1] * a_scr[1, rows, :] + ws[2] * a_scr[2, rows, :]
            o_ref[0, p] = (num / den).astype(BF16)
        return carry
    lax.fori_loop(0, N_PLANES // 2, combine, 0)


def _dilated_attention(q, k, v):
    b, _, sm, _ = q.shape
    n_span = sm // WINDOW
    cur = pl.BlockSpec((1, N_PLANES, WINDOW, LANES), lambda i, j, h: (i, 0, j, h))
    prev = pl.BlockSpec((1, N_PLANES, WINDOW, LANES), lambda i, j, h: (i, 0, jnp.maximum(j - 1, 0), h))
    biases = _attention_biases()
    bias_specs = [pl.BlockSpec(a.shape, lambda i, j, h, nd=a.ndim: (0,) * nd) for a in biases]
    stats = pltpu.VMEM((3, SPAN, LANES), F32)
    return pl.pallas_call(
        _attn_kernel,
        grid=(b, n_span, ATTN_WIDTH // LANES),
        in_specs=[cur, cur, prev, cur, prev] + bias_specs,
        out_specs=cur,
        out_shape=jax.ShapeDtypeStruct(q.shape, BF16),
        scratch_shapes=[stats, stats, stats, pltpu.VMEM((2 * WINDOW, 2 * WINDOW), F32)],
        compiler_params=pltpu.CompilerParams(vmem_limit_bytes=VMEM_LIMIT),
        name="dilated_attention",
    )(q, k, k, v, v, *biases)


CONV_HALO = 32


def _conv_kernel(cur_ref, prev_ref, w_ref, b_ref, g_ref, bl_ref, o_ref, scr, *, chunk):
    tt = cur_ref.shape[1]
    has_prev = pl.program_id(1) > 0
    scr[0, 0:CONV_HALO, :] = jnp.where(has_prev, prev_ref[0], 0.0)
    scr[0, CONV_HALO:CONV_HALO + tt, :] = cur_ref[0]
    aligned_rows = tt + CONV_HALO - SUBLANES
    for s in range(1, SUBLANES):
        scr[s, 0:aligned_rows, :] = scr[0, s:s + aligned_rows, :]
    lead = CONV_HALO - (CONV_WIDTH - 1)
    for c0 in range(0, tt, chunk):
        acc = jnp.zeros((chunk, CONV_CHANNELS), F32)
        for j in range(CONV_WIDTH):
            s, a = (lead + j) % SUBLANES, (lead + j) // SUBLANES * SUBLANES
            acc = acc + w_ref[j:j + 1, :] * scr[s, c0 + a:c0 + a + chunk, :]
        y = acc + b_ref[...]
        mu = jnp.mean(y, axis=-1, keepdims=True)
        var = jnp.mean(jnp.square(y - mu), axis=-1, keepdims=True)
        yn = (y - mu) * lax.rsqrt(var + NORM_EPS) * g_ref[...] + bl_ref[...]
        o_ref[0, c0:c0 + chunk, :] = (yn * jax.nn.sigmoid(yn)).astype(BF16)


def _conformer_conv(glu, w_dw, b_dw, g_ln, b_ln):
    b, s, c = glu.shape
    tt = 512
    row = pl.BlockSpec((1, c), lambda i, j: (0, 0))
    return pl.pallas_call(
        functools.partial(_conv_kernel, chunk=128),
        grid=(b, s // tt),
        in_specs=[pl.BlockSpec((1, tt, c), lambda i, j: (i, j, 0)),
                  pl.BlockSpec((1, CONV_HALO, c), lambda i, j: (i, jnp.maximum(j * (tt // CONV_HALO) - 1, 0), 0)),
                  pl.BlockSpec((CONV_WIDTH, c), lambda i, j: (0, 0)), row, row, row],
        out_specs=pl.BlockSpec((1, tt, c), lambda i, j: (i, j, 0)),
        out_shape=jax.ShapeDtypeStruct((b, s, c), BF16),
        scratch_shapes=[pltpu.VMEM((SUBLANES, CONV_HALO + tt, c), F32)],
        name="conformer_conv",
    )(glu, glu, w_dw[:, 0, :], b_dw[None, :], g_ln[None, :], b_ln[None, :])


def _outproj_kernel(attn_ref, conv_ref, x_ref, permt_ref, woa_ref, woc_ref, gpost_ref, gffn_ref, wr_ref, br_ref,
                    h_ref, u_ref, idx_ref, gate_ref):
    tt = x_ref.shape[1]
    nat = []
    for sub in range(tt // PERM_TOKENS):
        rows = slice(sub * PERM_ROWS, (sub + 1) * PERM_ROWS)
        a = jnp.concatenate([attn_ref[0, p, rows, :] for p in range(N_PLANES)], axis=0)
        nat.append(jnp.dot(permt_ref[...], a, preferred_element_type=F32).astype(BF16))
    mix = (jnp.dot(jnp.concatenate(nat, axis=0), woa_ref[...], preferred_element_type=F32)
           + jnp.dot(conv_ref[0], woc_ref[...], preferred_element_type=F32))
    h = x_ref[0] + _rms(mix, gpost_ref[...])
    h_ref[0] = h
    u = _rms(h, gffn_ref[...])
    half = u.shape[1] // 2
    ub = u.astype(BF16).astype(F32)
    u_ref[...] = ((lax.bitcast_convert_type(ub[:, :half], jnp.uint32) >> 16)
                  | (lax.bitcast_convert_type(ub[:, half:], jnp.uint32) & jnp.uint32(0xFFFF0000)))
    u_hi = u.astype(BF16)
    u_lo = (u - u_hi.astype(F32)).astype(BF16)
    nt = (((1,), (1,)), ((), ()))
    by_hi = lax.dot_general(wr_ref[...], u_hi, nt, preferred_element_type=F32)
    by_lo = lax.dot_general(wr_ref[:N_EXPERTS], u_lo, nt, preferred_element_type=F32)
    logits = by_hi[:N_EXPERTS] + (by_hi[N_EXPERTS:] + by_lo) + br_ref[...]
    rows = lax.broadcasted_iota(jnp.int32, logits.shape, 0)
    vals = logits
    tops, idxs = [], []
    for _ in range(TOP_K):
        mx = jnp.max(vals, axis=0, keepdims=True)
        ix = jnp.min(jnp.where(vals == mx, rows, N_EXPERTS), axis=0, keepdims=True)
        tops.append(mx)
        idxs.append(ix)
        vals = jnp.where(rows == ix, -jnp.inf, vals)
    ex = [jnp.exp(t - tops[0]) for t in tops]
    den = ex[0] + ex[1] + ex[2] + ex[3]
    idx_ref[...] = jnp.concatenate(idxs, axis=0)
    gate_ref[...] = jnp.concatenate([e / den for e in ex] + [jnp.zeros((8 - TOP_K, tt), F32)], axis=0)


def _output_projection(attn, conv, x, w_out, g_mix_post, g_ffn_pre, w_router, b_router):
    b, s, d = x.shape
    tt = 512
    mc = tt // N_PLANES
    n_t = s // tt
    woa = w_out[:ATTN_WIDTH].astype(BF16)
    woc = w_out[ATTN_WIDTH:].astype(BF16)
    wr_hi = w_router.T.astype(BF16)
    wr_split = jnp.concatenate([wr_hi, (w_router.T - wr_hi.astype(F32)).astype(BF16)], axis=0)
    permt =jnp.asarray(_plane_permutation().T, BF16)
    const = lambda shape: pl.BlockSpec(shape, lambda i, j: (0, 0))
    flat = lambda w: pl.BlockSpec((tt, w), lambda i, j: (i * n_t + j, 0))
    lanes = lambda r: pl.BlockSpec((r, tt), lambda i, j: (0, i * n_t + j))
    return pl.pallas_call(
        _outproj_kernel,
        grid=(b, n_t),
        in_specs=[pl.BlockSpec((1, N_PLANES, mc, ATTN_WIDTH), lambda i, j: (i, 0, j, 0)),
                  pl.BlockSpec((1, tt, CONV_CHANNELS), lambda i, j: (i, j, 0)),
                  pl.BlockSpec((1, tt, d), lambda i, j: (i, j, 0)),
                  const(permt.shape), const(woa.shape), const(woc.shape), const((1, d)), const((1, d)),
                  const((2 * N_EXPERTS, d)), const((N_EXPERTS, 1))],
        out_specs=[pl.BlockSpec((1, tt, d), lambda i, j: (i, j, 0)), flat(d // 2), lanes(TOP_K), lanes(8)],
        out_shape=[jax.ShapeDtypeStruct((b, s, d), F32),
                   jax.ShapeDtypeStruct((b * s, d // 2), jnp.uint32),
                   jax.ShapeDtypeStruct((TOP_K, b * s), jnp.int32),
                   jax.ShapeDtypeStruct((8, b * s), F32)],
        compiler_params=pltpu.CompilerParams(vmem_limit_bytes=VMEM_LIMIT),
        name="output_projection_router",
    )(attn, conv, x, permt, woa, woc, g_mix_post[None, :], g_ffn_pre[None, :], wr_split, b_router[:, None])


def _route_kernel(idx_ref, dest_ref, pend_ref, pad_ref, blk_ref, carry, pstart):
    phase = pl.program_id(0)
    step = pl.program_id(1)
    tt = idx_ref.shape[1]
    rows = lax.broadcasted_iota(jnp.int32, (N_EXPERTS, tt), 0)
    hot = [rows == idx_ref[k:k + 1, :] for k in range(TOP_K)]
    memb = sum(h.astype(F32) for h in hot)

    @pl.when((phase == 0) & (step == 0))
    def _():
        carry[...] = jnp.zeros_like(carry)

    @pl.when((phase == 1) & (step == 0))
    def _():
        counts = carry[...]
        padded = jnp.floor((counts + (EXPERT_ROWS - 1)) * (1.0 / EXPERT_ROWS)) * EXPERT_ROWS
        tri = (lax.broadcasted_iota(jnp.int32, (N_EXPERTS, N_EXPERTS), 1)
               <= lax.broadcasted_iota(jnp.int32, (N_EXPERTS, N_EXPERTS), 0)).astype(F32)
        pend = jnp.dot(tri, padded, precision=lax.Precision.HIGHEST, preferred_element_type=F32)
        pstart[...] = pend - padded
        pend_ref[...] = pend.astype(jnp.int32)
        pad_ref[...] = (pend - padded + counts).astype(jnp.int32)
        starts = lax.broadcasted_iota(jnp.int32, (N_EXPERTS, blk_ref.shape[1]), 1) * EXPERT_ROWS
        ended = (pend.astype(jnp.int32) <= starts).astype(jnp.int32)
        blk_ref[...] = jnp.minimum(jnp.sum(ended, axis=0, keepdims=True), N_EXPERTS - 1)
        carry[...] = jnp.zeros_like(carry)

    @pl.when(phase == 1)
    def _():
        earlier = (lax.broadcasted_iota(jnp.int32, (tt, tt), 0)
                   < lax.broadcasted_iota(jnp.int32, (tt, tt), 1)).astype(BF16)
        row = jnp.dot(memb.astype(BF16), earlier, preferred_element_type=F32) + (carry[...] + pstart[...])
        dest_ref[...] = jnp.concatenate(
            [jnp.sum(jnp.where(h, row, 0.0), axis=0, keepdims=True) for h in hot], axis=0).astype(jnp.int32)

    carry[...] = carry[...] + jnp.sum(memb, axis=1, keepdims=True)


def _routing(idx, n_blk):
    _, t = idx.shape
    tt = 512
    blk_lanes = -(-n_blk // LANES) * LANES
    return pl.pallas_call(
        _route_kernel,
        grid=(2, t // tt),
        in_specs=[pl.BlockSpec((TOP_K, tt), lambda ph, i: (0, i))],
        out_specs=[pl.BlockSpec((TOP_K, tt), lambda ph, i: (0, i * ph)),
                   pl.BlockSpec((N_EXPERTS, 1), lambda ph, i: (0, 0)),
                   pl.BlockSpec((N_EXPERTS, 1), lambda ph, i: (0, 0)),
                   pl.BlockSpec((1, blk_lanes), lambda ph, i: (0, 0))],
        out_shape=[jax.ShapeDtypeStruct((TOP_K, t), jnp.int32), jax.ShapeDtypeStruct((N_EXPERTS, 1), jnp.int32),
                   jax.ShapeDtypeStruct((N_EXPERTS, 1), jnp.int32), jax.ShapeDtypeStruct((1, blk_lanes), jnp.int32)],
        scratch_shapes=[pltpu.VMEM((N_EXPERTS, 1), F32), pltpu.VMEM((N_EXPERTS, 1), F32)],
        compiler_params=pltpu.CompilerParams(dimension_semantics=("arbitrary", "arbitrary")),
        name="moe_routing",
    )(idx)


SC_SCAN = 4096


def _sc_token_of_row(dest_flat, cap, n_tok):
    from jax.experimental.pallas import tpu_sc as plsc
    assert n_tok & (n_tok - 1) == 0
    info = plsc.get_sparse_core_info()
    n_core, n_sub, lanes = info.num_cores, info.num_subcores, info.num_lanes
    n_src = dest_flat.shape[0]
    per_w = cap // (n_core * n_sub)
    assert per_w * n_core * n_sub == cap and per_w % lanes == 0 and n_src % SC_SCAN == 0
    mesh = plsc.VectorSubcoreMesh(core_axis_name="c", subcore_axis_name="s")

    def body(dest_hbm, out_hbm, src_v, out_v):
        lo = (lax.axis_index("s") * n_core + lax.axis_index("c")) * per_w
        lane = lax.iota(jnp.int32, lanes)

        @pl.loop(0, per_w // lanes)
        def _(i):
            out_v[pl.ds(pl.multiple_of(i * lanes, lanes), lanes)] = (lo + i * lanes + lane) & (n_tok - 1)

        @pl.loop(0, n_src // SC_SCAN)
        def _(c):
            pltpu.sync_copy(dest_hbm.at[pl.ds(pl.multiple_of(c * SC_SCAN, 8), SC_SCAN)], src_v)

            @pl.loop(0, SC_SCAN // lanes)
            def _(v):
                local = src_v[pl.ds(pl.multiple_of(v * lanes, lanes), lanes)] - lo
                mine = lax.bitcast_convert_type(local, jnp.uint32) < jnp.uint32(per_w)
                tok = (c * SC_SCAN + v * lanes + lane) & (n_tok - 1)
                plsc.store_scatter(out_v, [local], tok, mask=mine)
        pltpu.sync_copy(out_v, out_hbm.at[pl.ds(pl.multiple_of(lo, 8), per_w)])

    return pl.kernel(body, mesh=mesh, out_type=jax.ShapeDtypeStruct((cap,), jnp.int32),
                     scratch_types=[pltpu.VMEM((SC_SCAN,), jnp.int32), pltpu.VMEM((per_w,), jnp.int32)],
                     compiler_params=pltpu.CompilerParams(needs_layout_passes=False),
                     name="moe_sc_token_of_row")(dest_flat)


def _expert_kernel(blk_e_ref, nused_ref, xb_ref, wgu_ref, bgu_ref, wd_ref, bd_ref, *rest):
    yb_ref, done_ref, wgu_b, wd_b = rest[-4:]
    j = pl.program_id(0)
    done_ref[...] = jnp.zeros_like(done_ref)
    nused = nused_ref[0]
    d_ff = wd_ref.shape[1]

    new_expert = (j == 0) | (blk_e_ref[j] != blk_e_ref[jnp.maximum(j - 1, 0)])

    @pl.when((j < nused) & new_expert)
    def _():
        def cast(src, dst):
            def body(i, carry):
                rows = pl.ds(pl.multiple_of(i * LANES, LANES), LANES)
                dst[rows, :] = src[0, rows, :].astype(BF16)
                return carry
            lax.fori_loop(0, dst.shape[0] // LANES, body, 0)
        cast(wgu_ref, wgu_b)
        cast(wd_ref, wd_b)

    @pl.when(j < nused)
    def _():
        word = xb_ref[...]
        half = word.shape[1]
        lo = lax.bitcast_convert_type(word << 16, F32).astype(BF16)
        hi = lax.bitcast_convert_type(word & jnp.uint32(0xFFFF0000), F32).astype(BF16)
        gu = (jnp.dot(lo, wgu_b[:half, :], preferred_element_type=F32)
              + jnp.dot(hi, wgu_b[half:, :], preferred_element_type=F32) + bgu_ref[0])
        gate = jnp.minimum(gu[:, :d_ff], SWIGLU_LIMIT)
        up = jnp.clip(gu[:, d_ff:], -SWIGLU_LIMIT, SWIGLU_LIMIT)
        hid = (up + 1.0) * (gate * jax.nn.sigmoid(gate * SWIGLU_ALPHA))
        yb_ref[...] = jnp.dot(hid.astype(BF16), wd_b[...], preferred_element_type=F32) + bd_ref[0]

    @pl.when(j >= nused)
    def _():
        yb_ref[...] = jnp.zeros_like(yb_ref)


def _experts(xb, chunk, n_chunk, yb_prev, blk_e, nused, w_gate_up, b_gate_up, w_down, b_down):
    rows, d_in = xb.shape
    e, d, ff2 = w_gate_up.shape
    d_ff = w_down.shape[1]
    n_blk = rows // EXPERT_ROWS
    live = lambda j, nu: jnp.maximum(jnp.minimum(j, nu[0] - 1), 0)
    expert = lambda j, be, nu: (be[live(j, nu)], 0, 0)
    in_specs = [pl.BlockSpec((EXPERT_ROWS, d_in), lambda j, be, nu: (live(j, nu), 0)),
                pl.BlockSpec((1, d, ff2), expert), pl.BlockSpec((1, 1, ff2), expert),
                pl.BlockSpec((1, d_ff, d), expert), pl.BlockSpec((1, 1, d), expert)]
    args = [blk_e, nused, xb, w_gate_up, b_gate_up[:, None, :], w_down, b_down[:, None, :]]
    aliases = {}
    if yb_prev is not None:
        in_specs.append(pl.BlockSpec(memory_space=pl.ANY))
        args.append(yb_prev)
        aliases = {len(args) - 1: 0}
    grid_spec = pltpu.PrefetchScalarGridSpec(
        num_scalar_prefetch=2,
        grid=(n_blk,),
        in_specs=in_specs,
        out_specs=[pl.BlockSpec((EXPERT_ROWS, d), lambda j, be, nu: (chunk * n_blk + j, 0)),
                   pl.BlockSpec((SUBLANES, LANES), lambda j, be, nu: (0, 0))],
        scratch_shapes=[pltpu.VMEM((d, ff2), BF16), pltpu.VMEM((d_ff, d), BF16)],
    )
    return pl.pallas_call(
        _expert_kernel,
        grid_spec=grid_spec,
        out_shape=[jax.ShapeDtypeStruct((n_chunk * rows, d), F32), jax.ShapeDtypeStruct((SUBLANES, LANES), F32)],
        input_output_aliases=aliases,
        compiler_params=pltpu.CompilerParams(dimension_semantics=("arbitrary",), vmem_limit_bytes=VMEM_LIMIT),
        name="moe_experts",
    )(*args)


SC_BUFFER_BYTES = 128 * 1024
MOE_CHUNKS = 8
MOE_GROUPS = 4


def _sc_gather_rows(table, idx, part, n_part, after=None):
    from jax.experimental.pallas import tpu_sc as plsc
    n_rows = idx.shape[0] // n_part
    d = table.shape[1]
    info = plsc.get_sparse_core_info()
    n_core, n_sub = info.num_cores, info.num_subcores
    per_w = n_rows // (n_core * n_sub)
    fit = SC_BUFFER_BYTES // (d * table.dtype.itemsize)
    g_rows = max(r for r in (8, 16, 32, 64, 128) if r <= fit and per_w % r == 0)
    n_chunk = per_w // g_rows
    assert per_w * n_core * n_sub == n_rows
    mesh = plsc.VectorSubcoreMesh(core_axis_name="c", subcore_axis_name="s")

    def body(table_hbm, idx_hbm, *rest):
        out_hbm, idx_v, buf0, buf1, sem0, sem1 = rest[-6:]
        base = (lax.axis_index("s") * n_core + lax.axis_index("c")) * per_w
        pltpu.sync_copy(idx_hbm.at[pl.ds(pl.multiple_of(part * n_rows + base, 8), per_w)], idx_v)

        def gather(chunk, buf, sem):
            rows = idx_v.at[pl.ds(pl.multiple_of(chunk * g_rows, 8), g_rows)]
            return pltpu.make_async_copy(table_hbm.at[rows], buf, sem)

        def write(chunk, buf):
            pltpu.sync_copy(buf, out_hbm.at[pl.ds(pl.multiple_of(base + chunk * g_rows, 8), g_rows)])

        gather(0, buf0, sem0).start()

        @pl.loop(0, n_chunk // 2)
        def _(i):
            gather(2 * i + 1, buf1, sem1).start()
            gather(2 * i, buf0, sem0).wait()
            write(2 * i, buf0)

            @pl.when(2 * i + 2 < n_chunk)
            def _():
                gather(2 * i + 2, buf0, sem0).start()
            gather(2 * i + 1, buf1, sem1).wait()
            write(2 * i + 1, buf1)

        if n_chunk % 2:
            gather(n_chunk - 1, buf0, sem0).wait()
            write(n_chunk - 1, buf0)

    return pl.kernel(
        body, mesh=mesh, out_type=jax.ShapeDtypeStruct((n_rows, d), table.dtype),
        scratch_types=[pltpu.VMEM((per_w,), jnp.int32), pltpu.VMEM((g_rows, d), table.dtype),
                       pltpu.VMEM((g_rows, d), table.dtype), pltpu.SemaphoreType.DMA, pltpu.SemaphoreType.DMA],
        name="moe_sc_gather",
    )(table, idx, *([] if after is None else [after]))


def _sc_weighted_rows(table, idx, gates, group, n_group):
    from jax.experimental.pallas import tpu_sc as plsc
    d = table.shape[1]
    info = plsc.get_sparse_core_info()
    n_core, n_sub, lanes = info.num_cores, info.num_subcores, info.num_lanes
    n_all = idx.shape[0] // TOP_K
    n_tok = n_all // n_group
    per_w = n_tok // (n_core * n_sub)
    w = 8
    n_chunk = per_w // w
    assert per_w * n_core * n_sub == n_tok and n_chunk * w == per_w and n_chunk % 2 == 0
    mesh = plsc.VectorSubcoreMesh(core_axis_name="c", subcore_axis_name="s")

    def body(table_hbm, idx_hbm, g_hbm, out_hbm, *scr):
        idx_v, g_v, rb, ob = scr[0:4], scr[4:8], (scr[8:12], scr[12:16]), scr[16:18]
        sems, wsems = scr[18:20], scr[20:22]
        base = (lax.axis_index("s") * n_core + lax.axis_index("c")) * per_w
        for k in range(TOP_K):
            off = pl.multiple_of(k * n_all + group * n_tok + base, 8)
            pltpu.sync_copy(idx_hbm.at[pl.ds(off, per_w)], idx_v[k])
            pltpu.sync_copy(g_hbm.at[pl.ds(off, per_w)], g_v[k])

        def gathers(chunk, slot):
            rows = pl.ds(pl.multiple_of(chunk * w, 8), w)
            return [pltpu.make_async_copy(table_hbm.at[idx_v[k].at[rows]], rb[slot][k], sems[slot])
                    for k in range(TOP_K)]

        def write(chunk, slot):
            return pltpu.make_async_copy(ob[slot], out_hbm.at[pl.ds(pl.multiple_of(base + chunk * w, 8), w)],
                                         wsems[slot])

        def combine(chunk, slot):
            @pl.loop(0, w)
            def _(i):
                token = jnp.full((lanes,), chunk * w + i, jnp.int32)
                gs = [plsc.load_gather(g_v[k], [token]) for k in range(TOP_K)]
                for j in range(d // lanes):
                    cols = pl.ds(j * lanes, lanes)
                    acc = gs[0] * rb[slot][0][i, cols]
                    for k in range(1, TOP_K):
                        acc = acc + gs[k] * rb[slot][k][i, cols]
                    ob[slot][i, cols] = acc
            write(chunk, slot).start()

        for c in gathers(0, 0):
            c.start()

        @pl.loop(0, n_chunk // 2)
        def _(i):
            for c in gathers(2 * i + 1, 1):
                c.start()
            for c in gathers(2 * i, 0):
                c.wait()

            @pl.when(i > 0)
            def _():
                write(2 * i - 2, 0).wait()
            combine(2 * i, 0)

            @pl.when(2 * i + 2 < n_chunk)
            def _():
                for c in gathers(2 * i + 2, 0):
                    c.start()
            for c in gathers(2 * i + 1, 1):
                c.wait()

            @pl.when(i > 0)
            def _():
                write(2 * i - 1, 1).wait()
            combine(2 * i + 1, 1)

        write(n_chunk - 2, 0).wait()
        write(n_chunk - 1, 1).wait()

    scratch = ([pltpu.VMEM((per_w,), jnp.int32)] * TOP_K + [pltpu.VMEM((per_w,), F32)] * TOP_K
               + [pltpu.VMEM((w, d), F32)] * (2 * TOP_K + 2) + [pltpu.SemaphoreType.DMA] * 4)
    return pl.kernel(body, mesh=mesh, out_type=jax.ShapeDtypeStruct((n_tok, d), F32), scratch_types=scratch,
                     compiler_params=pltpu.CompilerParams(needs_layout_passes=False),
                     name="moe_sc_combine")(table, idx, gates)


def _combine_kernel(h_ref, p_ref, y_ref, gffn_ref, wple_ref, wpg_ref, gple_ref, o_ref):
    h2 = h_ref[...] + _rms(y_ref[...], gffn_ref[...])
    ple = (jnp.dot(p_ref[...].astype(BF16), wple_ref[...], preferred_element_type=F32)
           * jax.nn.sigmoid(jnp.dot(h2.astype(BF16), wpg_ref[...], preferred_element_type=F32)))
    o_ref[...] = h2 + _rms(ple, gple_ref[...])


def _combine_kernel_aliased(h_ref, p_ref, y_ref, gffn_ref, wple_ref, wpg_ref, gple_ref, prev_ref, o_ref):
    del prev_ref
    _combine_kernel(h_ref, p_ref, y_ref, gffn_ref, wple_ref, wpg_ref, gple_ref, o_ref)


def _combine(group, n_group, out_prev, h, p, y, g_ffn_post, w_ple_bf16, w_ple_gate_bf16, g_ple_post):
    t, d = h.shape
    tt = 256
    n_tg = t // n_group // tt
    const = lambda shape: pl.BlockSpec(shape, lambda i: (0, 0))
    tok = lambda w: pl.BlockSpec((tt, w), lambda i: (group * n_tg + i, 0))
    in_specs = [tok(d), tok(p.shape[1]), pl.BlockSpec((tt, d), lambda i: (i, 0)),
                const((1, d)), const(w_ple_bf16.shape), const(w_ple_gate_bf16.shape), const((1, d))]
    args = [h, p, y, g_ffn_post[None, :], w_ple_bf16, w_ple_gate_bf16, g_ple_post[None, :]]
    body, aliases = _combine_kernel, {}
    if out_prev is not None:
        in_specs.append(pl.BlockSpec(memory_space=pl.ANY))
        args.append(out_prev)
        body, aliases = _combine_kernel_aliased, {len(args) - 1: 0}
    return pl.pallas_call(
        body,
        grid=(n_tg,),
        in_specs=in_specs,
        out_specs=tok(d),
        out_shape=jax.ShapeDtypeStruct((t, d), F32),
        input_output_aliases=aliases,
        compiler_params=pltpu.CompilerParams(vmem_limit_bytes=VMEM_LIMIT),
        name="moe_combine_ple",
    )(*args)


def _layer(h, p, positions, g_mix_pre, w_in, w_dw, b_dw, g_conv_ln, b_conv_ln, w_out, g_mix_post, g_ffn_pre,
           w_router, b_router, w_gate_up, b_gate_up, w_down, b_down, g_ffn_post, w_ple, w_ple_gate, g_ple_post):
    b, s, d = h.shape
    t = b * s
    cos_t, sin_t = _rope_tables(positions)
    q, k, v, glu = _input_projection(h, cos_t, sin_t, g_mix_pre, w_in)
    attn = _dilated_attention(q, k, v)
    conv = _conformer_conv(glu, w_dw, b_dw, g_conv_ln, b_conv_ln)
    h1, u_ffn, idx, gates = _output_projection(attn, conv, h, w_out, g_mix_post, g_ffn_pre, w_router, b_router)
    cap = t * TOP_K + N_EXPERTS * EXPERT_ROWS
    n_blk = cap // EXPERT_ROWS
    dest, pend, pad, blk = _routing(idx, n_blk)
    blk_e = blk[0, :n_blk]
    nused = pend[N_EXPERTS - 1] // EXPERT_ROWS
    dest_flat = dest.reshape(TOP_K * t)
    tok_of_row = _sc_token_of_row(dest_flat, cap, t)
    rows_c = cap // MOE_CHUNKS
    blk_c = n_blk // MOE_CHUNKS
    yb = done = None
    xb_next = _sc_gather_rows(u_ffn, tok_of_row, 0, MOE_CHUNKS)
    for c in range(MOE_CHUNKS):
        xb_c = xb_next
        if c + 1 < MOE_CHUNKS:
            xb_next = _sc_gather_rows(u_ffn, tok_of_row, c + 1, MOE_CHUNKS, after=done)
        nused_c = jnp.clip(nused - c * blk_c, 0, blk_c)
        yb, done = _experts(xb_c, c, MOE_CHUNKS, yb, blk_e[c * blk_c:(c + 1) * blk_c], nused_c,
                            w_gate_up, b_gate_up, w_down, b_down)
    t_g = t // MOE_GROUPS
    w_ple_bf16, w_ple_gate_bf16 = w_ple.astype(BF16), w_ple_gate.astype(BF16)
    h1_flat, p_flat = h1.reshape(t, d), p.reshape(t, -1)
    gate_flat = gates[:TOP_K].reshape(TOP_K * t)
    out = None
    for g in range(MOE_GROUPS):
        y_g = _sc_weighted_rows(yb, dest_flat, gate_flat, g, MOE_GROUPS)
        out = _combine(g, MOE_GROUPS, out, h1_flat, p_flat, y_g, g_ffn_post, w_ple_bf16, w_ple_gate_bf16,
                       g_ple_post)
    return out.reshape(b, s, d)


def kernel(x, p, positions, g_mix_pre, w_in, w_dw, b_dw, g_conv_ln, b_conv_ln, w_out, g_mix_post, g_ffn_pre,
           w_router, b_router, w_gate_up, b_gate_up, w_down, b_down, g_ffn_post, w_ple, w_ple_gate, g_ple_post):
    h = x
    for i in range(p.shape[0]):
        h = _layer(h, p[i], positions, g_mix_pre[i], w_in[i], w_dw[i], b_dw[i], g_conv_ln[i], b_conv_ln[i],
                   w_out[i], g_mix_post[i], g_ffn_pre[i], w_router[i], b_router[i], w_gate_up[i], b_gate_up[i],
                   w_down[i], b_down[i], g_ffn_post[i], w_ple[i], w_ple_gate[i], g_ple_post[i])
    return h
```

```python
import functools

import numpy as np
import jax
import jax.numpy as jnp
from jax import lax
from jax.experimental import pallas as pl
from jax.experimental.pallas import tpu as pltpu

F32 = jnp.float32
BF16 = jnp.bfloat16

HEAD_DIM = 64
N_HEADS = 12
ATTN_WIDTH = N_HEADS * HEAD_DIM
CONV_CHANNELS = 256
CONV_WIDTH = 31
ROPE_DIM = HEAD_DIM // 4
ROPE_THETA = 500000.0
N_EXPERTS = 32
TOP_K = 4
SWIGLU_LIMIT = 7.0
SWIGLU_ALPHA = 1.702
NORM_EPS = 1e-6
WINDOW = 128
N_PLANES = 16
SPAN = N_PLANES * WINDOW
LANES = 128
SUBLANES = 8
NEG = -1e30
EXPERT_ROWS = 256
VMEM_LIMIT = 56 * 1024 * 1024


def _residue_of_plane(p):
    return 4 * (p % 4) + p // 4


def _rms(xv, g):
    var = jnp.mean(xv * xv, axis=-1, keepdims=True)
    return xv * lax.rsqrt(var + NORM_EPS) * g


def _rope_kernel(pos_ref, invf_ref, expand_ref, one_ref, sgn_ref, c_ref, s_ref):
    tn = (((0,), (0,)), ((), ()))
    for p in range(N_PLANES):
        ang = invf_ref[...] * pos_ref[0, p:p + 1, :].astype(F32)
        spread = lambda t: lax.dot_general(t, expand_ref[...], tn, precision=lax.Precision.HIGHEST,
                                           preferred_element_type=F32)
        c_ref[0, p] = spread(jnp.cos(ang)) + one_ref[...]
        s_ref[0, p] = spread(jnp.sin(ang)) * sgn_ref[...]


def _rope_tables(positions):
    b, s = positions.shape
    sm = s // N_PLANES
    mt = min(sm, 128)
    plane_res = np.array([_residue_of_plane(p) for p in range(N_PLANES)])
    pos_planes = positions.reshape(b, sm, N_PLANES).transpose(0, 2, 1)[:, plane_res]
    half = ROPE_DIM // 2
    lane = np.arange(LANES) % HEAD_DIM
    inv_freq = ROPE_THETA ** (-jnp.arange(0, ROPE_DIM, 2, dtype=F32) / ROPE_DIM)
    rotary = lane < ROPE_DIM
    expand = jnp.asarray((np.arange(half)[:, None] == lane[None, :] % half) & rotary[None, :], F32)
    one = jnp.asarray(~rotary, F32)[None, :]
    sgn = jnp.asarray(np.where(lane < half, -1.0, 1.0), F32)[None, :]
    row = pl.BlockSpec((1, LANES), lambda i, j: (0, 0))
    out = pl.BlockSpec((1, N_PLANES, mt, LANES), lambda i, j: (i, 0, j, 0))
    return pl.pallas_call(
        _rope_kernel,
        grid=(b, sm // mt),
        in_specs=[pl.BlockSpec((1, N_PLANES, mt), lambda i, j: (i, 0, j)),
                  pl.BlockSpec((half, 1), lambda i, j: (0, 0)), pl.BlockSpec((half, LANES), lambda i, j: (0, 0)),
                  row, row],
        out_specs=[out, out],
        out_shape=[jax.ShapeDtypeStruct((b, N_PLANES, sm, LANES), F32)] * 2,
        name="rope_tables",
    )(pos_planes, inv_freq[:, None], expand, one, sgn)


PERM_TOKENS = 256
PERM_ROWS = PERM_TOKENS // N_PLANES


def _plane_permutation():
    perm = np.zeros((PERM_TOKENS, PERM_TOKENS), np.float32)
    for p in range(N_PLANES):
        for ml in range(PERM_ROWS):
            perm[PERM_ROWS * p + ml, N_PLANES * ml + _residue_of_plane(p)] = 1.0
    return perm


def _inproj_kernel(x_ref, c_ref, s_ref, g_ref, perm_ref, wqkv_ref, wc_ref, q_ref, k_ref, v_ref, glu_ref):
    g = g_ref[...]
    tt = x_ref.shape[1]
    lane = lax.broadcasted_iota(jnp.int32, (1, LANES), 1) % HEAD_DIM
    first_half = lane < ROPE_DIM // 2

    def rotary(t, cos, sin):
        outs = []
        for j in range(ATTN_WIDTH // LANES):
            tj = t[:, j * LANES:(j + 1) * LANES]
            partner = jnp.where(first_half, pltpu.roll(tj, LANES - ROPE_DIM // 2, 1),
                                pltpu.roll(tj, ROPE_DIM // 2, 1))
            outs.append(tj * cos + partner * sin)
        return jnp.concatenate(outs, axis=1)

    un = _rms(x_ref[0], g).astype(BF16)
    pc = jnp.dot(un, wc_ref[...], preferred_element_type=F32)
    glu_ref[0] = pc[:, :CONV_CHANNELS] * jax.nn.sigmoid(pc[:, CONV_CHANNELS:])

    for sub in range(tt // PERM_TOKENS):
        rows = slice(sub * PERM_ROWS, (sub + 1) * PERM_ROWS)
        u = jnp.dot(perm_ref[...], un[sub * PERM_TOKENS:(sub + 1) * PERM_TOKENS],
                    preferred_element_type=F32).astype(BF16)
        cos = jnp.concatenate([c_ref[0, p, rows, :] for p in range(N_PLANES)], axis=0)
        sin = jnp.concatenate([s_ref[0, p, rows, :] for p in range(N_PLANES)], axis=0)
        proj = jnp.dot(u, wqkv_ref[...], preferred_element_type=F32)
        q = (rotary(proj[:, :ATTN_WIDTH], cos, sin) * (HEAD_DIM ** -0.5)).astype(BF16)
        k = rotary(proj[:, ATTN_WIDTH:2 * ATTN_WIDTH], cos, sin).astype(BF16)
        v = proj[:, 2 * ATTN_WIDTH:].astype(BF16)
        for p in range(N_PLANES):
            chunk = slice(p * PERM_ROWS, (p + 1) * PERM_ROWS)
            q_ref[0, p, rows, :] = q[chunk]
            k_ref[0, p, rows, :] = k[chunk]
            v_ref[0, p, rows, :] = v[chunk]


def _input_projection(x, cos_t, sin_t, g_mix_pre, w_in):
    b, s, d = x.shape
    tt = 1024
    mc = tt // N_PLANES
    wqkv = w_in[:, :3 * ATTN_WIDTH].astype(BF16)
    wc = w_in[:, 3 * ATTN_WIDTH:].astype(BF16)
    perm = jnp.asarray(_plane_permutation(), BF16)
    plane = lambda w: pl.BlockSpec((1, N_PLANES, mc, w), lambda i, j: (i, 0, j, 0))
    plane_shape = jax.ShapeDtypeStruct((b, N_PLANES, s // N_PLANES, ATTN_WIDTH), BF16)
    tok = lambda w: pl.BlockSpec((1, tt, w), lambda i, j: (i, j, 0))
    const = lambda shape: pl.BlockSpec(shape, lambda i, j: (0, 0))
    return pl.pallas_call(
        _inproj_kernel,
        grid=(b, s // tt),
        in_specs=[tok(d), plane(LANES), plane(LANES), const((1, d)), const(perm.shape),
                  const(wqkv.shape), const(wc.shape)],
        out_specs=[plane(ATTN_WIDTH), plane(ATTN_WIDTH), plane(ATTN_WIDTH), tok(CONV_CHANNELS)],
        out_shape=[plane_shape, plane_shape, plane_shape,
                   jax.ShapeDtypeStruct((b, s, CONV_CHANNELS), F32)],
        compiler_params=pltpu.CompilerParams(vmem_limit_bytes=VMEM_LIMIT),
        name="input_projection",
    )(x, cos_t, sin_t, g_mix_pre[None, :], perm, wqkv, wc)


def _attention_biases():
    band = lambda j: np.where((j >= 0) & (j <= WINDOW), 0.0, NEG).astype(np.float32)
    cols = lambda m: np.where(m, NEG, 0.0).astype(np.float32)
    twice = lambda a: np.concatenate([a, a], axis=0)
    mq = np.arange(WINDOW)[:, None]
    kj = np.arange(2 * WINDOW)[None, :]
    j16 = mq + WINDOW - kj
    prev16 = kj < WINDOW
    row = np.arange(128)[:, None]
    col = np.arange(256)[None, :]
    j4 = 4 * (row % 32 - (col % 64 - 32)) + row // 32 - col // 64
    prev4 = col % 64 < 32
    row = np.arange(256)[:, None]
    col = np.arange(512)[None, :]
    res = np.vectorize(_residue_of_plane)
    j1 = 16 * (row % 16 - (col % 32 - 16)) + res(row // 16) - res(col // 32)
    prev1 = col % 32 < 16
    b1 = np.stack([twice(band(j1[:128])), twice(band(j1[128:]))])
    return [jnp.asarray(a) for a in (twice(band(j16)), twice(band(j4)), b1, cols(prev16), cols(prev4), cols(prev1))]


UNROLL = 8


def _attn_kernel(q_ref, kc_ref, kp_ref, vc_ref, vp_ref, b16_ref, b4_ref, b1_ref, p16_ref, p4_ref, p1_ref,
                 o_ref, m_scr, l_scr, a_scr, bias16_scr):
    no_prev = (pl.program_id(1) == 0).astype(F32)
    head0 = lax.broadcasted_iota(jnp.int32, (1, LANES), 1) < HEAD_DIM
    bias16_scr[...] = b16_ref[...] + no_prev * p16_ref[...]

    def tile(qt, kt, vt, bias):
        n = qt.shape[0]
        zero = jnp.zeros_like(qt)
        q2 = jnp.concatenate([jnp.where(head0, qt, zero), jnp.where(head0, zero, qt)], axis=0)
        s = lax.dot_general(q2, kt, (((1,), (1,)), ((), ())), preferred_element_type=F32) + bias
        m = jnp.max(s, axis=-1, keepdims=True)
        e = jnp.exp(s - m).astype(BF16)
        va = jnp.concatenate([vt, jnp.ones_like(vt)], axis=1)
        o = jnp.dot(e, va, preferred_element_type=F32)
        pick = lambda top, bot: jnp.where(head0, top, bot)
        mm = pick(jnp.broadcast_to(m[:n], (n, LANES)), jnp.broadcast_to(m[n:], (n, LANES)))
        return mm, pick(o[:n, LANES:], o[n:, LANES:]), pick(o[:n, :LANES], o[n:, :LANES])

    def put(branch, start, size, stats, off):
        for scr, val in zip((m_scr, l_scr, a_scr), stats):
            scr[branch, pl.ds(start, size), :] = val[off:off + size]

    def body16(i, carry):
        for p in [UNROLL * i + u for u in range(UNROLL)]:
            kt = jnp.concatenate([kp_ref[0, p], kc_ref[0, p]], axis=0)
            vt = jnp.concatenate([vp_ref[0, p], vc_ref[0, p]], axis=0)
            put(0, pl.multiple_of(p * WINDOW, WINDOW), WINDOW, tile(q_ref[0, p], kt, vt, bias16_scr[...]), 0)
        return carry
    lax.fori_loop(0, N_PLANES // UNROLL, body16, 0)

    def body4(g, carry):
        for c, i in [(2 * g + cc, ii) for cc in range(2) for ii in range(4)]:
            qt = jnp.concatenate([q_ref[0, 4 * c + a, 32 * i:32 * i + 32, :] for a in range(4)], axis=0)
            if i == 0:
                ks = [x for a in range(4) for x in (kp_ref[0, 4 * c + a, 96:128, :], kc_ref[0, 4 * c + a, 0:32, :])]
                vs = [x for a in range(4) for x in (vp_ref[0, 4 * c + a, 96:128, :], vc_ref[0, 4 * c + a, 0:32, :])]
                bias = b4_ref[...] + no_prev * p4_ref[...]
            else:
                ks = [kc_ref[0, 4 * c + a, 32 * i - 32:32 * i + 32, :] for a in range(4)]
                vs = [vc_ref[0, 4 * c + a, 32 * i - 32:32 * i + 32, :] for a in range(4)]
                bias = b4_ref[...]
            stats = tile(qt, jnp.concatenate(ks, axis=0), jnp.concatenate(vs, axis=0), bias)
            for a in range(4):
                put(1, pl.multiple_of((4 * c + a) * WINDOW + 32 * i, 32), 32, stats, 32 * a)
        return carry
    lax.fori_loop(0, 2, body4, 0)

    def tile1(i, first):
        rq = pl.ds(pl.multiple_of(16 * i, 16), 16)
        if first:
            ks = [x for p in range(N_PLANES) for x in (kp_ref[0, p, 112:128, :], kc_ref[0, p, 0:16, :])]
            vs = [x for p in range(N_PLANES) for x in (vp_ref[0, p, 112:128, :], vc_ref[0, p, 0:16, :])]
        else:
            rk = pl.ds(pl.multiple_of(16 * i - 16, 16), 32)
            ks = [kc_ref[0, p, rk, :] for p in range(N_PLANES)]
            vs = [vc_ref[0, p, rk, :] for p in range(N_PLANES)]
        kt = jnp.concatenate(ks, axis=0)
        vt = jnp.concatenate(vs, axis=0)
        for half in range(2):
            planes = range(8 * half, 8 * half + 8)
            qt = jnp.concatenate([q_ref[0, p, rq, :] for p in planes], axis=0)
            bias = b1_ref[half] + no_prev * p1_ref[...] if first else b1_ref[half]
            stats = tile(qt, kt, vt, bias)
            for p in planes:
                put(2, pl.multiple_of(p * WINDOW + 16 * i, 16), 16, stats, 16 * (p - 8 * half))

    tile1(0, True)
    tile1(1, False)

    def body1(g, carry):
        tile1(2 * g, False)
        tile1(2 * g + 1, False)
        return carry
    lax.fori_loop(1, WINDOW // 32, body1, 0)

    def combine(i, carry):
        for p in (2 * i, 2 * i + 1):
            rows = pl.ds(pl.multiple_of(p * WINDOW, WINDOW), WINDOW)
            ms = [m_scr[b, rows, :] for b in range(3)]
            mx = jnp.maximum(jnp.maximum(ms[0], ms[1]), ms[2])
            ws = [jnp.exp(m - mx) for m in ms]
            den = ws[0] * l_scr[0, rows, :] + ws[1] * l_scr[1, rows, :] + ws[2] * l_scr[2, rows, :]
            num = ws[0] * a_scr[0, rows, :] + ws[1] * a_scr[1, rows, :] + ws[2] * a_scr[2, rows, :]
            o_ref[0, p] = (num / den).astype(BF16)
        return carry
    lax.fori_loop(0, N_PLANES // 2, combine, 0)


def _dilated_attention(q, k, v):
    b, _, sm, _ = q.shape
    n_span = sm // WINDOW
    cur = pl.BlockSpec((1, N_PLANES, WINDOW, LANES), lambda i, j, h: (i, 0, j, h))
    prev = pl.BlockSpec((1, N_PLANES, WINDOW, LANES), lambda i, j, h: (i, 0, jnp.maximum(j - 1, 0), h))
    biases = _attention_biases()
    bias_specs = [pl.BlockSpec(a.shape, lambda i, j, h, nd=a.ndim: (0,) * nd) for a in biases]
    stats = pltpu.VMEM((3, SPAN, LANES), F32)
    return pl.pallas_call(
        _attn_kernel,
        grid=(b, n_span, ATTN_WIDTH // LANES),
        in_specs=[cur, cur, prev, cur, prev] + bias_specs,
        out_specs=cur,
        out_shape=jax.ShapeDtypeStruct(q.shape, BF16),
        scratch_shapes=[stats, stats, stats, pltpu.VMEM((2 * WINDOW, 2 * WINDOW), F32)],
        compiler_params=pltpu.CompilerParams(vmem_limit_bytes=VMEM_LIMIT),
        name="dilated_attention",
    )(q, k, k, v, v, *biases)


CONV_HALO = 32


def _conv_kernel(cur_ref, prev_ref, w_ref, b_ref, g_ref, bl_ref, o_ref, scr, *, chunk):
    tt = cur_ref.shape[1]
    has_prev = pl.program_id(1) > 0
    scr[0, 0:CONV_HALO, :] = jnp.where(has_prev, prev_ref[0], 0.0)
    scr[0, CONV_HALO:CONV_HALO + tt, :] = cur_ref[0]
    aligned_rows = tt + CONV_HALO - SUBLANES
    for s in range(1, SUBLANES):
        scr[s, 0:aligned_rows, :] = scr[0, s:s + aligned_rows, :]
    lead = CONV_HALO - (CONV_WIDTH - 1)
    for c0 in range(0, tt, chunk):
        acc = jnp.zeros((chunk, CONV_CHANNELS), F32)
        for j in range(CONV_WIDTH):
            s, a = (lead + j) % SUBLANES, (lead + j) // SUBLANES * SUBLANES
            acc = acc + w_ref[j:j + 1, :] * scr[s, c0 + a:c0 + a + chunk, :]
        y = acc + b_ref[...]
        mu = jnp.mean(y, axis=-1, keepdims=True)
        var = jnp.mean(jnp.square(y - mu), axis=-1, keepdims=True)
        yn = (y - mu) * lax.rsqrt(var + NORM_EPS) * g_ref[...] + bl_ref[...]
        o_ref[0, c0:c0 + chunk, :] = (yn * jax.nn.sigmoid(yn)).astype(BF16)


def _conformer_conv(glu, w_dw, b_dw, g_ln, b_ln):
    b, s, c = glu.shape
    tt = 512
    row = pl.BlockSpec((1, c), lambda i, j: (0, 0))
    return pl.pallas_call(
        functools.partial(_conv_kernel, chunk=128),
        grid=(b, s // tt),
        in_specs=[pl.BlockSpec((1, tt, c), lambda i, j: (i, j, 0)),
                  pl.BlockSpec((1, CONV_HALO, c), lambda i, j: (i, jnp.maximum(j * (tt // CONV_HALO) - 1, 0), 0)),
                  pl.BlockSpec((CONV_WIDTH, c), lambda i, j: (0, 0)), row, row, row],
        out_specs=pl.BlockSpec((1, tt, c), lambda i, j: (i, j, 0)),
        out_shape=jax.ShapeDtypeStruct((b, s, c), BF16),
        scratch_shapes=[pltpu.VMEM((SUBLANES, CONV_HALO + tt, c), F32)],
        name="conformer_conv",
    )(glu, glu, w_dw[:, 0, :], b_dw[None, :], g_ln[None, :], b_ln[None, :])


def _outproj_kernel(attn_ref, conv_ref, x_ref, permt_ref, woa_ref, woc_ref, gpost_ref, gffn_ref, wr_ref, br_ref,
                    h_ref, u_ref, idx_ref, gate_ref):
    tt = x_ref.shape[1]
    nat = []
    for sub in range(tt // PERM_TOKENS):
        rows = slice(sub * PERM_ROWS, (sub + 1) * PERM_ROWS)
        a = jnp.concatenate([attn_ref[0, p, rows, :] for p in range(N_PLANES)], axis=0)
        nat.append(jnp.dot(permt_ref[...], a, preferred_element_type=F32).astype(BF16))
    mix = (jnp.dot(jnp.concatenate(nat, axis=0), woa_ref[...], preferred_element_type=F32)
           + jnp.dot(conv_ref[0], woc_ref[...], preferred_element_type=F32))
    h = x_ref[0] + _rms(mix, gpost_ref[...])
    h_ref[0] = h
    u = _rms(h, gffn_ref[...])
    half = u.shape[1] // 2
    ub = u.astype(BF16).astype(F32)
    u_ref[...] = ((lax.bitcast_convert_type(ub[:, :half], jnp.uint32) >> 16)
                  | (lax.bitcast_convert_type(ub[:, half:], jnp.uint32) & jnp.uint32(0xFFFF0000)))
    u_hi = u.astype(BF16)
    u_lo = (u - u_hi.astype(F32)).astype(BF16)
    nt = (((1,), (1,)), ((), ()))
    by_hi = lax.dot_general(wr_ref[...], u_hi, nt, preferred_element_type=F32)
    by_lo = lax.dot_general(wr_ref[:N_EXPERTS], u_lo, nt, preferred_element_type=F32)
    logits = by_hi[:N_EXPERTS] + (by_hi[N_EXPERTS:] + by_lo) + br_ref[...]
    rows = lax.broadcasted_iota(jnp.int32, logits.shape, 0)
    vals = logits
    tops, idxs = [], []
    for _ in range(TOP_K):
        mx = jnp.max(vals, axis=0, keepdims=True)
        ix = jnp.min(jnp.where(vals == mx, rows, N_EXPERTS), axis=0, keepdims=True)
        tops.append(mx)
        idxs.append(ix)
        vals = jnp.where(rows == ix, -jnp.inf, vals)
    ex = [jnp.exp(t - tops[0]) for t in tops]
    den = ex[0] + ex[1] + ex[2] + ex[3]
    idx_ref[...] = jnp.concatenate(idxs, axis=0)
    gate_ref[...] = jnp.concatenate([e / den for e in ex] + [jnp.zeros((8 - TOP_K, tt), F32)], axis=0)


def _output_projection(attn, conv, x, w_out, g_mix_post, g_ffn_pre, w_router, b_router):
    b, s, d = x.shape
    tt = 512
    mc = tt // N_PLANES
    n_t = s // tt
    woa = w_out[:ATTN_WIDTH].astype(BF16)
    woc = w_out[ATTN_WIDTH:].astype(BF16)
    wr_hi = w_router.T.astype(BF16)
    wr_split = jnp.concatenate([wr_hi, (w_router.T - wr_hi.astype(F32)).astype(BF16)], axis=0)
    permt =jnp.asarray(_plane_permutation().T, BF16)
    const = lambda shape: pl.BlockSpec(shape, lambda i, j: (0, 0))
    flat = lambda w: pl.BlockSpec((tt, w), lambda i, j: (i * n_t + j, 0))
    lanes = lambda r: pl.BlockSpec((r, tt), lambda i, j: (0, i * n_t + j))
    return pl.pallas_call(
        _outproj_kernel,
        grid=(b, n_t),
        in_specs=[pl.BlockSpec((1, N_PLANES, mc, ATTN_WIDTH), lambda i, j: (i, 0, j, 0)),
                  pl.BlockSpec((1, tt, CONV_CHANNELS), lambda i, j: (i, j, 0)),
                  pl.BlockSpec((1, tt, d), lambda i, j: (i, j, 0)),
                  const(permt.shape), const(woa.shape), const(woc.shape), const((1, d)), const((1, d)),
                  const((2 * N_EXPERTS, d)), const((N_EXPERTS, 1))],
        out_specs=[pl.BlockSpec((1, tt, d), lambda i, j: (i, j, 0)), flat(d // 2), lanes(TOP_K), lanes(8)],
        out_shape=[jax.ShapeDtypeStruct((b, s, d), F32),
                   jax.ShapeDtypeStruct((b * s, d // 2), jnp.uint32),
                   jax.ShapeDtypeStruct((TOP_K, b * s), jnp.int32),
                   jax.ShapeDtypeStruct((8, b * s), F32)],
        compiler_params=pltpu.CompilerParams(vmem_limit_bytes=VMEM_LIMIT),
        name="output_projection_router",
    )(attn, conv, x, permt, woa, woc, g_mix_post[None, :], g_ffn_pre[None, :], wr_split, b_router[:, None])


def _route_kernel(idx_ref, dest_ref, pend_ref, pad_ref, blk_ref, carry, pstart):
    phase = pl.program_id(0)
    step = pl.program_id(1)
    tt = idx_ref.shape[1]
    rows = lax.broadcasted_iota(jnp.int32, (N_EXPERTS, tt), 0)
    hot = [rows == idx_ref[k:k + 1, :] for k in range(TOP_K)]
    memb = sum(h.astype(F32) for h in hot)

    @pl.when((phase == 0) & (step == 0))
    def _():
        carry[...] = jnp.zeros_like(carry)

    @pl.when((phase == 1) & (step == 0))
    def _():
        counts = carry[...]
        padded = jnp.floor((counts + (EXPERT_ROWS - 1)) * (1.0 / EXPERT_ROWS)) * EXPERT_ROWS
        tri = (lax.broadcasted_iota(jnp.int32, (N_EXPERTS, N_EXPERTS), 1)
               <= lax.broadcasted_iota(jnp.int32, (N_EXPERTS, N_EXPERTS), 0)).astype(F32)
        pend = jnp.dot(tri, padded, precision=lax.Precision.HIGHEST, preferred_element_type=F32)
        pstart[...] = pend - padded
        pend_ref[...] = pend.astype(jnp.int32)
        pad_ref[...] = (pend - padded + counts).astype(jnp.int32)
        starts = lax.broadcasted_iota(jnp.int32, (N_EXPERTS, blk_ref.shape[1]), 1) * EXPERT_ROWS
        ended = (pend.astype(jnp.int32) <= starts).astype(jnp.int32)
        blk_ref[...] = jnp.minimum(jnp.sum(ended, axis=0, keepdims=True), N_EXPERTS - 1)
        carry[...] = jnp.zeros_like(carry)

    @pl.when(phase == 1)
    def _():
        earlier = (lax.broadcasted_iota(jnp.int32, (tt, tt), 0)
                   < lax.broadcasted_iota(jnp.int32, (tt, tt), 1)).astype(BF16)
        row = jnp.dot(memb.astype(BF16), earlier, preferred_element_type=F32) + (carry[...] + pstart[...])
        dest_ref[...] = jnp.concatenate(
            [jnp.sum(jnp.where(h, row, 0.0), axis=0, keepdims=True) for h in hot], axis=0).astype(jnp.int32)

    carry[...] = carry[...] + jnp.sum(memb, axis=1, keepdims=True)


def _routing(idx, n_blk):
    _, t = idx.shape
    tt = 512
    blk_lanes = -(-n_blk // LANES) * LANES
    return pl.pallas_call(
        _route_kernel,
        grid=(2, t // tt),
        in_specs=[pl.BlockSpec((TOP_K, tt), lambda ph, i: (0, i))],
        out_specs=[pl.BlockSpec((TOP_K, tt), lambda ph, i: (0, i * ph)),
                   pl.BlockSpec((N_EXPERTS, 1), lambda ph, i: (0, 0)),
                   pl.BlockSpec((N_EXPERTS, 1), lambda ph, i: (0, 0)),
                   pl.BlockSpec((1, blk_lanes), lambda ph, i: (0, 0))],
        out_shape=[jax.ShapeDtypeStruct((TOP_K, t), jnp.int32), jax.ShapeDtypeStruct((N_EXPERTS, 1), jnp.int32),
                   jax.ShapeDtypeStruct((N_EXPERTS, 1), jnp.int32), jax.ShapeDtypeStruct((1, blk_lanes), jnp.int32)],
        scratch_shapes=[pltpu.VMEM((N_EXPERTS, 1), F32), pltpu.VMEM((N_EXPERTS, 1), F32)],
        compiler_params=pltpu.CompilerParams(dimension_semantics=("arbitrary", "arbitrary")),
        name="moe_routing",
    )(idx)


SC_SCAN = 16384


def _sc_token_of_row(dest_flat, cap, n_tok):
    from jax.experimental.pallas import tpu_sc as plsc
    assert n_tok & (n_tok - 1) == 0
    info = plsc.get_sparse_core_info()
    n_core, n_sub, lanes = info.num_cores, info.num_subcores, info.num_lanes
    n_src = dest_flat.shape[0]
    per_w = cap // (n_core * n_sub)
    assert per_w * n_core * n_sub == cap and per_w % lanes == 0 and n_src % SC_SCAN == 0
    mesh = plsc.VectorSubcoreMesh(core_axis_name="c", subcore_axis_name="s")

    def body(dest_hbm, out_hbm, src_v, out_v):
        lo = (lax.axis_index("s") * n_core + lax.axis_index("c")) * per_w
        lane = lax.iota(jnp.int32, lanes)

        @pl.loop(0, per_w // lanes)
        def _(i):
            out_v[pl.ds(pl.multiple_of(i * lanes, lanes), lanes)] = (lo + i * lanes + lane) & (n_tok - 1)

        @pl.loop(0, n_src // SC_SCAN)
        def _(c):
            pltpu.sync_copy(dest_hbm.at[pl.ds(pl.multiple_of(c * SC_SCAN, 8), SC_SCAN)], src_v)

            @pl.loop(0, SC_SCAN // lanes)
            def _(v):
                local = src_v[pl.ds(pl.multiple_of(v * lanes, lanes), lanes)] - lo
                mine = lax.bitcast_convert_type(local, jnp.uint32) < jnp.uint32(per_w)
                tok = (c * SC_SCAN + v * lanes + lane) & (n_tok - 1)
                plsc.store_scatter(out_v, [local], tok, mask=mine)
        pltpu.sync_copy(out_v, out_hbm.at[pl.ds(pl.multiple_of(lo, 8), per_w)])

    return pl.kernel(body, mesh=mesh, out_type=jax.ShapeDtypeStruct((cap,), jnp.int32),
                     scratch_types=[pltpu.VMEM((SC_SCAN,), jnp.int32), pltpu.VMEM((per_w,), jnp.int32)],
                     compiler_params=pltpu.CompilerParams(needs_layout_passes=False),
                     name="moe_sc_token_of_row")(dest_flat)


def _expert_kernel(blk_e_ref, nused_ref, xb_ref, wgu_ref, bgu_ref, wd_ref, bd_ref, *rest):
    yb_ref, done_ref, wgu_b, wd_b = rest[-4:]
    j = pl.program_id(0)
    done_ref[...] = jnp.zeros_like(done_ref)
    nused = nused_ref[0]
    d_ff = wd_ref.shape[1]

    new_expert = (j == 0) | (blk_e_ref[j] != blk_e_ref[jnp.maximum(j - 1, 0)])

    @pl.when((j < nused) & new_expert)
    def _():
        def cast(src, dst):
            def body(i, carry):
                rows = pl.ds(pl.multiple_of(i * LANES, LANES), LANES)
                dst[rows, :] = src[0, rows, :].astype(BF16)
                return carry
            lax.fori_loop(0, dst.shape[0] // LANES, body, 0)
        cast(wgu_ref, wgu_b)
        cast(wd_ref, wd_b)

    @pl.when(j < nused)
    def _():
        word = xb_ref[...]
        half = word.shape[1]
        lo = lax.bitcast_convert_type(word << 16, F32).astype(BF16)
        hi = lax.bitcast_convert_type(word & jnp.uint32(0xFFFF0000), F32).astype(BF16)
        gu = (jnp.dot(lo, wgu_b[:half, :], preferred_element_type=F32)
              + jnp.dot(hi, wgu_b[half:, :], preferred_element_type=F32) + bgu_ref[0])
        gate = jnp.minimum(gu[:, :d_ff], SWIGLU_LIMIT)
        up = jnp.clip(gu[:, d_ff:], -SWIGLU_LIMIT, SWIGLU_LIMIT)
        hid = (up + 1.0) * (gate * jax.nn.sigmoid(gate * SWIGLU_ALPHA))
        yb_ref[...] = jnp.dot(hid.astype(BF16), wd_b[...], preferred_element_type=F32) + bd_ref[0]

    @pl.when(j >= nused)
    def _():
        yb_ref[...] = jnp.zeros_like(yb_ref)


def _experts(xb, chunk, n_chunk, yb_prev, blk_e, nused, w_gate_up, b_gate_up, w_down, b_down):
    rows, d_in = xb.shape
    e, d, ff2 = w_gate_up.shape
    d_ff = w_down.shape[1]
    n_blk = rows // EXPERT_ROWS
    live = lambda j, nu: jnp.maximum(jnp.minimum(j, nu[0] - 1), 0)
    expert = lambda j, be, nu: (be[live(j, nu)], 0, 0)
    in_specs = [pl.BlockSpec((EXPERT_ROWS, d_in), lambda j, be, nu: (live(j, nu), 0)),
                pl.BlockSpec((1, d, ff2), expert), pl.BlockSpec((1, 1, ff2), expert),
                pl.BlockSpec((1, d_ff, d), expert), pl.BlockSpec((1, 1, d), expert)]
    args = [blk_e, nused, xb, w_gate_up, b_gate_up[:, None, :], w_down, b_down[:, None, :]]
    aliases = {}
    if yb_prev is not None:
        in_specs.append(pl.BlockSpec(memory_space=pl.ANY))
        args.append(yb_prev)
        aliases = {len(args) - 1: 0}
    grid_spec = pltpu.PrefetchScalarGridSpec(
        num_scalar_prefetch=2,
        grid=(n_blk,),
        in_specs=in_specs,
        out_specs=[pl.BlockSpec((EXPERT_ROWS, d), lambda j, be, nu: (chunk * n_blk + j, 0)),
                   pl.BlockSpec((SUBLANES, LANES), lambda j, be, nu: (0, 0))],
        scratch_shapes=[pltpu.VMEM((d, ff2), BF16), pltpu.VMEM((d_ff, d), BF16)],
    )
    return pl.pallas_call(
        _expert_kernel,
        grid_spec=grid_spec,
        out_shape=[jax.ShapeDtypeStruct((n_chunk * rows, d), F32), jax.ShapeDtypeStruct((SUBLANES, LANES), F32)],
        input_output_aliases=aliases,
        compiler_params=pltpu.CompilerParams(dimension_semantics=("arbitrary",), vmem_limit_bytes=VMEM_LIMIT),
        name="moe_experts",
    )(*args)


SC_BUFFER_BYTES = 128 * 1024
MOE_CHUNKS = 8
MOE_GROUPS = 8


def _sc_gather_rows(table, idx, part, n_part, after=None):
    from jax.experimental.pallas import tpu_sc as plsc
    n_rows = idx.shape[0] // n_part
    d = table.shape[1]
    info = plsc.get_sparse_core_info()
    n_core, n_sub = info.num_cores, info.num_subcores
    per_w = n_rows // (n_core * n_sub)
    fit = SC_BUFFER_BYTES // (d * table.dtype.itemsize)
    g_rows = max(r for r in (8, 16, 32, 64, 128) if r <= fit and per_w % r == 0)
    n_chunk = per_w // g_rows
    assert per_w * n_core * n_sub == n_rows
    mesh = plsc.VectorSubcoreMesh(core_axis_name="c", subcore_axis_name="s")

    def body(table_hbm, idx_hbm, *rest):
        out_hbm, idx_v, buf0, buf1, sem0, sem1 = rest[-6:]
        base = (lax.axis_index("s") * n_core + lax.axis_index("c")) * per_w
        pltpu.sync_copy(idx_hbm.at[pl.ds(pl.multiple_of(part * n_rows + base, 8), per_w)], idx_v)

        def gather(chunk, buf, sem):
            rows = idx_v.at[pl.ds(pl.multiple_of(chunk * g_rows, 8), g_rows)]
            return pltpu.make_async_copy(table_hbm.at[rows], buf, sem)

        def write(chunk, buf):
            pltpu.sync_copy(buf, out_hbm.at[pl.ds(pl.multiple_of(base + chunk * g_rows, 8), g_rows)])

        gather(0, buf0, sem0).start()

        @pl.loop(0, n_chunk // 2)
        def _(i):
            gather(2 * i + 1, buf1, sem1).start()
            gather(2 * i, buf0, sem0).wait()
            write(2 * i, buf0)

            @pl.when(2 * i + 2 < n_chunk)
            def _():
                gather(2 * i + 2, buf0, sem0).start()
            gather(2 * i + 1, buf1, sem1).wait()
            write(2 * i + 1, buf1)

        if n_chunk % 2:
            gather(n_chunk - 1, buf0, sem0).wait()
            write(n_chunk - 1, buf0)

    return pl.kernel(
        body, mesh=mesh, out_type=jax.ShapeDtypeStruct((n_rows, d), table.dtype),
        scratch_types=[pltpu.VMEM((per_w,), jnp.int32), pltpu.VMEM((g_rows, d), table.dtype),
                       pltpu.VMEM((g_rows, d), table.dtype), pltpu.SemaphoreType.DMA, pltpu.SemaphoreType.DMA],
        name="moe_sc_gather",
    )(table, idx, *([] if after is None else [after]))


def _sc_weighted_rows(table, idx, gates, group, n_group):
    from jax.experimental.pallas import tpu_sc as plsc
    d = table.shape[1]
    info = plsc.get_sparse_core_info()
    n_core, n_sub, lanes = info.num_cores, info.num_subcores, info.num_lanes
    n_all = idx.shape[0] // TOP_K
    n_tok = n_all // n_group
    per_w = n_tok // (n_core * n_sub)
    w = 8
    n_chunk = per_w // w
    assert per_w * n_core * n_sub == n_tok and n_chunk * w == per_w and n_chunk % 2 == 0
    mesh = plsc.VectorSubcoreMesh(core_axis_name="c", subcore_axis_name="s")

    def body(table_hbm, idx_hbm, g_hbm, out_hbm, *scr):
        idx_v, g_v, rb, ob = scr[0:4], scr[4:8], (scr[8:12], scr[12:16]), scr[16:18]
        sems, wsems = scr[18:20], scr[20:22]
        base = (lax.axis_index("s") * n_core + lax.axis_index("c")) * per_w
        for k in range(TOP_K):
            off = pl.multiple_of(k * n_all + group * n_tok + base, 8)
            pltpu.sync_copy(idx_hbm.at[pl.ds(off, per_w)], idx_v[k])
            pltpu.sync_copy(g_hbm.at[pl.ds(off, per_w)], g_v[k])

        def gathers(chunk, slot):
            rows = pl.ds(pl.multiple_of(chunk * w, 8), w)
            return [pltpu.make_async_copy(table_hbm.at[idx_v[k].at[rows]], rb[slot][k], sems[slot])
                    for k in range(TOP_K)]

        def write(chunk, slot):
            return pltpu.make_async_copy(ob[slot], out_hbm.at[pl.ds(pl.multiple_of(base + chunk * w, 8), w)],
                                         wsems[slot])

        def combine(chunk, slot):
            @pl.loop(0, w)
            def _(i):
                token = jnp.full((lanes,), chunk * w + i, jnp.int32)
                gs = [plsc.load_gather(g_v[k], [token]) for k in range(TOP_K)]
                for j in range(d // lanes):
                    cols = pl.ds(j * lanes, lanes)
                    acc = gs[0] * rb[slot][0][i, cols]
                    for k in range(1, TOP_K):
                        acc = acc + gs[k] * rb[slot][k][i, cols]
                    ob[slot][i, cols] = acc
            write(chunk, slot).start()

        for c in gathers(0, 0):
            c.start()

        @pl.loop(0, n_chunk // 2)
        def _(i):
            for c in gathers(2 * i + 1, 1):
                c.start()
            for c in gathers(2 * i, 0):
                c.wait()

            @pl.when(i > 0)
            def _():
                write(2 * i - 2, 0).wait()
            combine(2 * i, 0)

            @pl.when(2 * i + 2 < n_chunk)
            def _():
                for c in gathers(2 * i + 2, 0):
                    c.start()
            for c in gathers(2 * i + 1, 1):
                c.wait()

            @pl.when(i > 0)
            def _():
                write(2 * i - 1, 1).wait()
            combine(2 * i + 1, 1)

        write(n_chunk - 2, 0).wait()
        write(n_chunk - 1, 1).wait()

    scratch = ([pltpu.VMEM((per_w,), jnp.int32)] * TOP_K + [pltpu.VMEM((per_w,), F32)] * TOP_K
               + [pltpu.VMEM((w, d), F32)] * (2 * TOP_K + 2) + [pltpu.SemaphoreType.DMA] * 4)
    return pl.kernel(body, mesh=mesh, out_type=jax.ShapeDtypeStruct((n_tok, d), F32), scratch_types=scratch,
                     compiler_params=pltpu.CompilerParams(needs_layout_passes=False),
                     name="moe_sc_combine")(table, idx, gates)


def _combine_kernel(h_ref, p_ref, y_ref, gffn_ref, wple_ref, wpg_ref, gple_ref, o_ref):
    h2 = h_ref[...] + _rms(y_ref[...], gffn_ref[...])
    ple = (jnp.dot(p_ref[...].astype(BF16), wple_ref[...], preferred_element_type=F32)
           * jax.nn.sigmoid(jnp.dot(h2.astype(BF16), wpg_ref[...], preferred_element_type=F32)))
    o_ref[...] = h2 + _rms(ple, gple_ref[...])


def _combine_kernel_aliased(h_ref, p_ref, y_ref, gffn_ref, wple_ref, wpg_ref, gple_ref, prev_ref, o_ref):
    del prev_ref
    _combine_kernel(h_ref, p_ref, y_ref, gffn_ref, wple_ref, wpg_ref, gple_ref, o_ref)


def _combine(group, n_group, out_prev, h, p, y, g_ffn_post, w_ple_bf16, w_ple_gate_bf16, g_ple_post):
    t, d = h.shape
    tt = 256
    n_tg = t // n_group // tt
    const = lambda shape: pl.BlockSpec(shape, lambda i: (0, 0))
    tok = lambda w: pl.BlockSpec((tt, w), lambda i: (group * n_tg + i, 0))
    in_specs = [tok(d), tok(p.shape[1]), pl.BlockSpec((tt, d), lambda i: (i, 0)),
                const((1, d)), const(w_ple_bf16.shape), const(w_ple_gate_bf16.shape), const((1, d))]
    args = [h, p, y, g_ffn_post[None, :], w_ple_bf16, w_ple_gate_bf16, g_ple_post[None, :]]
    body, aliases = _combine_kernel, {}
    if out_prev is not None:
        in_specs.append(pl.BlockSpec(memory_space=pl.ANY))
        args.append(out_prev)
        body, aliases = _combine_kernel_aliased, {len(args) - 1: 0}
    return pl.pallas_call(
        body,
        grid=(n_tg,),
        in_specs=in_specs,
        out_specs=tok(d),
        out_shape=jax.ShapeDtypeStruct((t, d), F32),
        input_output_aliases=aliases,
        compiler_params=pltpu.CompilerParams(vmem_limit_bytes=VMEM_LIMIT),
        name="moe_combine_ple",
    )(*args)


def _layer(h, p, positions, g_mix_pre, w_in, w_dw, b_dw, g_conv_ln, b_conv_ln, w_out, g_mix_post, g_ffn_pre,
           w_router, b_router, w_gate_up, b_gate_up, w_down, b_down, g_ffn_post, w_ple, w_ple_gate, g_ple_post):
    b, s, d = h.shape
    t = b * s
    cos_t, sin_t = _rope_tables(positions)
    q, k, v, glu = _input_projection(h, cos_t, sin_t, g_mix_pre, w_in)
    attn = _dilated_attention(q, k, v)
    conv = _conformer_conv(glu, w_dw, b_dw, g_conv_ln, b_conv_ln)
    h1, u_ffn, idx, gates = _output_projection(attn, conv, h, w_out, g_mix_post, g_ffn_pre, w_router, b_router)
    cap = t * TOP_K + N_EXPERTS * EXPERT_ROWS
    n_blk = cap // EXPERT_ROWS
    dest, pend, pad, blk = _routing(idx, n_blk)
    blk_e = blk[0, :n_blk]
    nused = pend[N_EXPERTS - 1] // EXPERT_ROWS
    dest_flat = dest.reshape(TOP_K * t)
    tok_of_row = _sc_token_of_row(dest_flat, cap, t)
    rows_c = cap // MOE_CHUNKS
    blk_c = n_blk // MOE_CHUNKS
    yb = done = None
    xb_next = _sc_gather_rows(u_ffn, tok_of_row, 0, MOE_CHUNKS)
    for c in range(MOE_CHUNKS):
        xb_c = xb_next
        if c + 1 < MOE_CHUNKS:
            xb_next = _sc_gather_rows(u_ffn, tok_of_row, c + 1, MOE_CHUNKS, after=done)
        nused_c = jnp.clip(nused - c * blk_c, 0, blk_c)
        yb, done = _experts(xb_c, c, MOE_CHUNKS, yb, blk_e[c * blk_c:(c + 1) * blk_c], nused_c,
                            w_gate_up, b_gate_up, w_down, b_down)
    t_g = t // MOE_GROUPS
    w_ple_bf16, w_ple_gate_bf16 = w_ple.astype(BF16), w_ple_gate.astype(BF16)
    h1_flat, p_flat = h1.reshape(t, d), p.reshape(t, -1)
    gate_flat = gates[:TOP_K].reshape(TOP_K * t)
    out = None
    for g in range(MOE_GROUPS):
        y_g = _sc_weighted_rows(yb, dest_flat, gate_flat, g, MOE_GROUPS)
        out = _combine(g, MOE_GROUPS, out, h1_flat, p_flat, y_g, g_ffn_post, w_ple_bf16, w_ple_gate_bf16,
                       g_ple_post)
    return out.reshape(b, s, d)


def kernel(x, p, positions, g_mix_pre, w_in, w_dw, b_dw, g_conv_ln, b_conv_ln, w_out, g_mix_post, g_ffn_pre,
           w_router, b_router, w_gate_up, b_gate_up, w_down, b_down, g_ffn_post, w_ple, w_ple_gate, g_ple_post):
    h = x
    for i in range(p.shape[0]):
        h = _layer(h, p[i], positions, g_mix_pre[i], w_in[i], w_dw[i], b_dw[i], g_conv_ln[i], b_conv_ln[i],
                   w_out[i], g_mix_post[i], g_ffn_pre[i], w_router[i], b_router[i], w_gate_up[i], b_gate_up[i],
                   w_down[i], b_down[i], g_ffn_post[i], w_ple[i], w_ple_gate[i], g_ple_post[i])
    return h
```

```python
import functools

import numpy as np
import jax
import jax.numpy as jnp
from jax import lax
from jax.experimental import pallas as pl
from jax.experimental.pallas import tpu as pltpu

F32 = jnp.float32
BF16 = jnp.bfloat16

HEAD_DIM = 64
N_HEADS = 12
ATTN_WIDTH = N_HEADS * HEAD_DIM
CONV_CHANNELS = 256
CONV_WIDTH = 31
ROPE_DIM = HEAD_DIM // 4
ROPE_THETA = 500000.0
N_EXPERTS = 32
TOP_K = 4
SWIGLU_LIMIT = 7.0
SWIGLU_ALPHA = 1.702
NORM_EPS = 1e-6
WINDOW = 128
N_PLANES = 16
SPAN = N_PLANES * WINDOW
LANES = 128
SUBLANES = 8
NEG = -1e30
EXPERT_ROWS = 256
VMEM_LIMIT = 56 * 1024 * 1024


def _residue_of_plane(p):
    return 4 * (p % 4) + p // 4


def _rms(xv, g):
    var = jnp.mean(xv * xv, axis=-1, keepdims=True)
    return xv * lax.rsqrt(var + NORM_EPS) * g


def _rope_kernel(pos_ref, invf_ref, expand_ref, one_ref, sgn_ref, c_ref, s_ref):
    tn = (((0,), (0,)), ((), ()))

    def spread(t):
        t1 = t.astype(BF16)
        r1 = t - t1.astype(F32)
        t2 = r1.astype(BF16)
        t3 = (r1 - t2.astype(F32)).astype(BF16)
        return sum(lax.dot_general(piece, expand_ref[...], tn, preferred_element_type=F32) for piece in (t1, t2, t3))

    for p in range(N_PLANES):
        ang = invf_ref[...] * pos_ref[0, p:p + 1, :].astype(F32)
        c_ref[0, p] = spread(jnp.cos(ang)) + one_ref[...]
        s_ref[0, p] = spread(jnp.sin(ang)) * sgn_ref[...]


def _rope_tables(positions):
    b, s = positions.shape
    sm = s // N_PLANES
    mt = min(sm, 128)
    plane_res = np.array([_residue_of_plane(p) for p in range(N_PLANES)])
    pos_planes = positions.reshape(b, sm, N_PLANES).transpose(0, 2, 1)[:, plane_res]
    half = ROPE_DIM // 2
    lane = np.arange(LANES) % HEAD_DIM
    inv_freq = ROPE_THETA ** (-jnp.arange(0, ROPE_DIM, 2, dtype=F32) / ROPE_DIM)
    rotary = lane < ROPE_DIM
    expand = jnp.asarray((np.arange(half)[:, None] == lane[None, :] % half) & rotary[None, :], BF16)
    one = jnp.asarray(~rotary, F32)[None, :]
    sgn = jnp.asarray(np.where(lane < half, -1.0, 1.0), F32)[None, :]
    row = pl.BlockSpec((1, LANES), lambda i, j: (0, 0))
    out = pl.BlockSpec((1, N_PLANES, mt, LANES), lambda i, j: (i, 0, j, 0))
    return pl.pallas_call(
        _rope_kernel,
        grid=(b, sm // mt),
        in_specs=[pl.BlockSpec((1, N_PLANES, mt), lambda i, j: (i, 0, j)),
                  pl.BlockSpec((half, 1), lambda i, j: (0, 0)), pl.BlockSpec((half, LANES), lambda i, j: (0, 0)),
                  row, row],
        out_specs=[out, out],
        out_shape=[jax.ShapeDtypeStruct((b, N_PLANES, sm, LANES), F32)] * 2,
        name="rope_tables",
    )(pos_planes, inv_freq[:, None], expand, one, sgn)


PERM_TOKENS = 256
PERM_ROWS = PERM_TOKENS // N_PLANES


def _plane_permutation():
    perm = np.zeros((PERM_TOKENS, PERM_TOKENS), np.float32)
    for p in range(N_PLANES):
        for ml in range(PERM_ROWS):
            perm[PERM_ROWS * p + ml, N_PLANES * ml + _residue_of_plane(p)] = 1.0
    return perm


def _inproj_kernel(x_ref, c_ref, s_ref, g_ref, perm_ref, wqkv_ref, wc_ref, q_ref, k_ref, v_ref, glu_ref):
    g = g_ref[...]
    tt = x_ref.shape[1]
    lane = lax.broadcasted_iota(jnp.int32, (1, LANES), 1) % HEAD_DIM
    first_half = lane < ROPE_DIM // 2

    def rotary(t, cos, sin):
        outs = []
        for j in range(ATTN_WIDTH // LANES):
            tj = t[:, j * LANES:(j + 1) * LANES]
            partner = jnp.where(first_half, pltpu.roll(tj, LANES - ROPE_DIM // 2, 1),
                                pltpu.roll(tj, ROPE_DIM // 2, 1))
            outs.append(tj * cos + partner * sin)
        return jnp.concatenate(outs, axis=1)

    un = _rms(x_ref[0], g).astype(BF16)
    pc = jnp.dot(un, wc_ref[...], preferred_element_type=F32)
    glu_ref[0] = pc[:, :CONV_CHANNELS] * jax.nn.sigmoid(pc[:, CONV_CHANNELS:])

    for sub in range(tt // PERM_TOKENS):
        rows = slice(sub * PERM_ROWS, (sub + 1) * PERM_ROWS)
        u = jnp.dot(perm_ref[...], un[sub * PERM_TOKENS:(sub + 1) * PERM_TOKENS],
                    preferred_element_type=F32).astype(BF16)
        cos = jnp.concatenate([c_ref[0, p, rows, :] for p in range(N_PLANES)], axis=0)
        sin = jnp.concatenate([s_ref[0, p, rows, :] for p in range(N_PLANES)], axis=0)
        proj = jnp.dot(u, wqkv_ref[...], preferred_element_type=F32)
        q = (rotary(proj[:, :ATTN_WIDTH], cos, sin) * (HEAD_DIM ** -0.5)).astype(BF16)
        k = rotary(proj[:, ATTN_WIDTH:2 * ATTN_WIDTH], cos, sin).astype(BF16)
        v = proj[:, 2 * ATTN_WIDTH:].astype(BF16)
        for p in range(N_PLANES):
            chunk = slice(p * PERM_ROWS, (p + 1) * PERM_ROWS)
            q_ref[0, p, rows, :] = q[chunk]
            k_ref[0, p, rows, :] = k[chunk]
            v_ref[0, p, rows, :] = v[chunk]


def _input_projection(x, cos_t, sin_t, g_mix_pre, w_in):
    b, s, d = x.shape
    tt = 1024
    mc = tt // N_PLANES
    wqkv = w_in[:, :3 * ATTN_WIDTH].astype(BF16)
    wc = w_in[:, 3 * ATTN_WIDTH:].astype(BF16)
    perm = jnp.asarray(_plane_permutation(), BF16)
    plane = lambda w: pl.BlockSpec((1, N_PLANES, mc, w), lambda i, j: (i, 0, j, 0))
    plane_shape = jax.ShapeDtypeStruct((b, N_PLANES, s // N_PLANES, ATTN_WIDTH), BF16)
    tok = lambda w: pl.BlockSpec((1, tt, w), lambda i, j: (i, j, 0))
    const = lambda shape: pl.BlockSpec(shape, lambda i, j: (0, 0))
    return pl.pallas_call(
        _inproj_kernel,
        grid=(b, s // tt),
        in_specs=[tok(d), plane(LANES), plane(LANES), const((1, d)), const(perm.shape),
                  const(wqkv.shape), const(wc.shape)],
        out_specs=[plane(ATTN_WIDTH), plane(ATTN_WIDTH), plane(ATTN_WIDTH), tok(CONV_CHANNELS)],
        out_shape=[plane_shape, plane_shape, plane_shape,
                   jax.ShapeDtypeStruct((b, s, CONV_CHANNELS), F32)],
        compiler_params=pltpu.CompilerParams(vmem_limit_bytes=VMEM_LIMIT),
        name="input_projection",
    )(x, cos_t, sin_t, g_mix_pre[None, :], perm, wqkv, wc)


def _attention_biases():
    band = lambda j: np.where((j >= 0) & (j <= WINDOW), 0.0, NEG).astype(np.float32)
    cols = lambda m: np.where(m, NEG, 0.0).astype(np.float32)
    twice = lambda a: np.concatenate([a, a], axis=0)
    mq = np.arange(WINDOW)[:, None]
    kj = np.arange(2 * WINDOW)[None, :]
    j16 = mq + WINDOW - kj
    prev16 = kj < WINDOW
    row = np.arange(128)[:, None]
    col = np.arange(256)[None, :]
    j4 = 4 * (row % 32 - (col % 64 - 32)) + row // 32 - col // 64
    prev4 = col % 64 < 32
    row = np.arange(256)[:, None]
    col = np.arange(512)[None, :]
    res = np.vectorize(_residue_of_plane)
    j1 = 16 * (row % 16 - (col % 32 - 16)) + res(row // 16) - res(col // 32)
    prev1 = col % 32 < 16
    b1 = np.stack([twice(band(j1[:128])), twice(band(j1[128:]))])
    return [jnp.asarray(a) for a in (twice(band(j16)), twice(band(j4)), b1, cols(prev16), cols(prev4), cols(prev1))]


UNROLL = 8


def _attn_kernel(q_ref, kc_ref, kp_ref, vc_ref, vp_ref, b16_ref, b4_ref, b1_ref, p16_ref, p4_ref, p1_ref,
                 o_ref, m_scr, l_scr, a_scr, bias16_scr):
    no_prev = (pl.program_id(1) == 0).astype(F32)
    head0 = lax.broadcasted_iota(jnp.int32, (1, LANES), 1) < HEAD_DIM
    bias16_scr[...] = b16_ref[...] + no_prev * p16_ref[...]

    def tile(qt, kt, vt, bias):
        n = qt.shape[0]
        zero = jnp.zeros_like(qt)
        q2 = jnp.concatenate([jnp.where(head0, qt, zero), jnp.where(head0, zero, qt)], axis=0)
        s = lax.dot_general(q2, kt, (((1,), (1,)), ((), ())), preferred_element_type=F32) + bias
        m = jnp.max(s, axis=-1, keepdims=True)
        e = jnp.exp(s - m).astype(BF16)
        va = jnp.concatenate([vt, jnp.ones_like(vt)], axis=1)
        o = jnp.dot(e, va, preferred_element_type=F32)
        pick = lambda top, bot: jnp.where(head0, top, bot)
        mm = pick(jnp.broadcast_to(m[:n], (n, LANES)), jnp.broadcast_to(m[n:], (n, LANES)))
        return mm, pick(o[:n, LANES:], o[n:, LANES:]), pick(o[:n, :LANES], o[n:, :LANES])

    def put(branch, start, size, stats, off):
        for scr, val in zip((m_scr, l_scr, a_scr), stats):
            scr[branch, pl.ds(start, size), :] = val[off:off + size]

    def body16(i, carry):
        for p in [UNROLL * i + u for u in range(UNROLL)]:
            kt = jnp.concatenate([kp_ref[0, p], kc_ref[0, p]], axis=0)
            vt = jnp.concatenate([vp_ref[0, p], vc_ref[0, p]], axis=0)
            put(0, pl.multiple_of(p * WINDOW, WINDOW), WINDOW, tile(q_ref[0, p], kt, vt, bias16_scr[...]), 0)
        return carry
    lax.fori_loop(0, N_PLANES // UNROLL, body16, 0)

    def body4(g, carry):
        for c, i in [(2 * g + cc, ii) for cc in range(2) for ii in range(4)]:
            qt = jnp.concatenate([q_ref[0, 4 * c + a, 32 * i:32 * i + 32, :] for a in range(4)], axis=0)
            if i == 0:
                ks = [x for a in range(4) for x in (kp_ref[0, 4 * c + a, 96:128, :], kc_ref[0, 4 * c + a, 0:32, :])]
                vs = [x for a in range(4) for x in (vp_ref[0, 4 * c + a, 96:128, :], vc_ref[0, 4 * c + a, 0:32, :])]
                bias = b4_ref[...] + no_prev * p4_ref[...]
            else:
                ks = [kc_ref[0, 4 * c + a, 32 * i - 32:32 * i + 32, :] for a in range(4)]
                vs = [vc_ref[0, 4 * c + a, 32 * i - 32:32 * i + 32, :] for a in range(4)]
                bias = b4_ref[...]
            stats = tile(qt, jnp.concatenate(ks, axis=0), jnp.concatenate(vs, axis=0), bias)
            for a in range(4):
                put(1, pl.multiple_of((4 * c + a) * WINDOW + 32 * i, 32), 32, stats, 32 * a)
        return carry
    lax.fori_loop(0, 2, body4, 0)

    def tile1(i, first):
        rq = pl.ds(pl.multiple_of(16 * i, 16), 16)
        if first:
            ks = [x for p in range(N_PLANES) for x in (kp_ref[0, p, 112:128, :], kc_ref[0, p, 0:16, :])]
            vs = [x for p in range(N_PLANES) for x in (vp_ref[0, p, 112:128, :], vc_ref[0, p, 0:16, :])]
        else:
            rk = pl.ds(pl.multiple_of(16 * i - 16, 16), 32)
            ks = [kc_ref[0, p, rk, :] for p in range(N_PLANES)]
            vs = [vc_ref[0, p, rk, :] for p in range(N_PLANES)]
        kt = jnp.concatenate(ks, axis=0)
        vt = jnp.concatenate(vs, axis=0)
        for half in range(2):
            planes = range(8 * half, 8 * half + 8)
            qt = jnp.concatenate([q_ref[0, p, rq, :] for p in planes], axis=0)
            bias = b1_ref[half] + no_prev * p1_ref[...] if first else b1_ref[half]
            stats = tile(qt, kt, vt, bias)
            for p in planes:
                put(2, pl.multiple_of(p * WINDOW + 16 * i, 16), 16, stats, 16 * (p - 8 * half))

    tile1(0, True)
    tile1(1, False)

    def body1(g, carry):
        tile1(2 * g, False)
        tile1(2 * g + 1, False)
        return carry
    lax.fori_loop(1, WINDOW // 32, body1, 0)

    def combine(i, carry):
        for p in (2 * i, 2 * i + 1):
            rows = pl.ds(pl.multiple_of(p * WINDOW, WINDOW), WINDOW)
            ms = [m_scr[b, rows, :] for b in range(3)]
            mx = jnp.maximum(jnp.maximum(ms[0], ms[1]), ms[2])
            ws = [jnp.exp(m - mx) for m in ms]
            den = ws[0] * l_scr[0, rows, :] + ws[1] * l_scr[1, rows, :] + ws[2] * l_scr[2, rows, :]
            num = ws[0] * a_scr[0, rows, :] + ws[1] * a_scr[1, rows, :] + ws[2] * a_scr[2, rows, :]
            o_ref[0, p] = (num / den).astype(BF16)
        return carry
    lax.fori_loop(0, N_PLANES // 2, combine, 0)


def _dilated_attention(q, k, v):
    b, _, sm, _ = q.shape
    n_span = sm // WINDOW
    cur = pl.BlockSpec((1, N_PLANES, WINDOW, LANES), lambda i, j, h: (i, 0, j, h))
    prev = pl.BlockSpec((1, N_PLANES, WINDOW, LANES), lambda i, j, h: (i, 0, jnp.maximum(j - 1, 0), h))
    biases = _attention_biases()
    bias_specs = [pl.BlockSpec(a.shape, lambda i, j, h, nd=a.ndim: (0,) * nd) for a in biases]
    stats = pltpu.VMEM((3, SPAN, LANES), F32)
    return pl.pallas_call(
        _attn_kernel,
        grid=(b, n_span, ATTN_WIDTH // LANES),
        in_specs=[cur, cur, prev, cur, prev] + bias_specs,
        out_specs=cur,
        out_shape=jax.ShapeDtypeStruct(q.shape, BF16),
        scratch_shapes=[stats, stats, stats, pltpu.VMEM((2 * WINDOW, 2 * WINDOW), F32)],
        compiler_params=pltpu.CompilerParams(vmem_limit_bytes=VMEM_LIMIT),
        name="dilated_attention",
    )(q, k, k, v, v, *biases)


CONV_HALO = 32


def _conv_kernel(cur_ref, prev_ref, w_ref, b_ref, g_ref, bl_ref, o_ref, scr, *, chunk):
    tt = cur_ref.shape[1]
    has_prev = pl.program_id(1) > 0
    scr[0, 0:CONV_HALO, :] = jnp.where(has_prev, prev_ref[0], 0.0)
    scr[0, CONV_HALO:CONV_HALO + tt, :] = cur_ref[0]
    aligned_rows = tt + CONV_HALO - SUBLANES
    for s in range(1, SUBLANES):
        scr[s, 0:aligned_rows, :] = scr[0, s:s + aligned_rows, :]
    lead = CONV_HALO - (CONV_WIDTH - 1)
    for c0 in range(0, tt, chunk):
        acc = jnp.zeros((chunk, CONV_CHANNELS), F32)
        for j in range(CONV_WIDTH):
            s, a = (lead + j) % SUBLANES, (lead + j) // SUBLANES * SUBLANES
            acc = acc + w_ref[j:j + 1, :] * scr[s, c0 + a:c0 + a + chunk, :]
        y = acc + b_ref[...]
        mu = jnp.mean(y, axis=-1, keepdims=True)
        var = jnp.mean(jnp.square(y - mu), axis=-1, keepdims=True)
        yn = (y - mu) * lax.rsqrt(var + NORM_EPS) * g_ref[...] + bl_ref[...]
        o_ref[0, c0:c0 + chunk, :] = (yn * jax.nn.sigmoid(yn)).astype(BF16)


def _conformer_conv(glu, w_dw, b_dw, g_ln, b_ln):
    b, s, c = glu.shape
    tt = 512
    row = pl.BlockSpec((1, c), lambda i, j: (0, 0))
    return pl.pallas_call(
        functools.partial(_conv_kernel, chunk=128),
        grid=(b, s // tt),
        in_specs=[pl.BlockSpec((1, tt, c), lambda i, j: (i, j, 0)),
                  pl.BlockSpec((1, CONV_HALO, c), lambda i, j: (i, jnp.maximum(j * (tt // CONV_HALO) - 1, 0), 0)),
                  pl.BlockSpec((CONV_WIDTH, c), lambda i, j: (0, 0)), row, row, row],
        out_specs=pl.BlockSpec((1, tt, c), lambda i, j: (i, j, 0)),
        out_shape=jax.ShapeDtypeStruct((b, s, c), BF16),
        scratch_shapes=[pltpu.VMEM((SUBLANES, CONV_HALO + tt, c), F32)],
        name="conformer_conv",
    )(glu, glu, w_dw[:, 0, :], b_dw[None, :], g_ln[None, :], b_ln[None, :])


def _outproj_kernel(attn_ref, conv_ref, x_ref, permt_ref, woa_ref, woc_ref, gpost_ref, gffn_ref, wr_ref, br_ref,
                    h_ref, u_ref, idx_ref, gate_ref):
    tt = x_ref.shape[1]
    nat = []
    for sub in range(tt // PERM_TOKENS):
        rows = slice(sub * PERM_ROWS, (sub + 1) * PERM_ROWS)
        a = jnp.concatenate([attn_ref[0, p, rows, :] for p in range(N_PLANES)], axis=0)
        nat.append(jnp.dot(permt_ref[...], a, preferred_element_type=F32).astype(BF16))
    mix = (jnp.dot(jnp.concatenate(nat, axis=0), woa_ref[...], preferred_element_type=F32)
           + jnp.dot(conv_ref[0], woc_ref[...], preferred_element_type=F32))
    h = x_ref[0] + _rms(mix, gpost_ref[...])
    h_ref[0] = h
    u = _rms(h, gffn_ref[...])
    half = u.shape[1] // 2
    ub = u.astype(BF16).astype(F32)
    u_ref[...] = ((lax.bitcast_convert_type(ub[:, :half], jnp.uint32) >> 16)
                  | (lax.bitcast_convert_type(ub[:, half:], jnp.uint32) & jnp.uint32(0xFFFF0000)))
    u_hi = u.astype(BF16)
    u_lo = (u - u_hi.astype(F32)).astype(BF16)
    nt = (((1,), (1,)), ((), ()))
    by_hi = lax.dot_general(wr_ref[...], u_hi, nt, preferred_element_type=F32)
    by_lo = lax.dot_general(wr_ref[:N_EXPERTS], u_lo, nt, preferred_element_type=F32)
    logits = by_hi[:N_EXPERTS] + (by_hi[N_EXPERTS:] + by_lo) + br_ref[...]
    rows = lax.broadcasted_iota(jnp.int32, logits.shape, 0)
    vals = logits
    tops, idxs = [], []
    for _ in range(TOP_K):
        mx = jnp.max(vals, axis=0, keepdims=True)
        ix = jnp.min(jnp.where(vals == mx, rows, N_EXPERTS), axis=0, keepdims=True)
        tops.append(mx)
        idxs.append(ix)
        vals = jnp.where(rows == ix, -jnp.inf, vals)
    ex = [jnp.exp(t - tops[0]) for t in tops]
    den = ex[0] + ex[1] + ex[2] + ex[3]
    idx_ref[...] = jnp.concatenate(idxs, axis=0)
    gate_ref[...] = jnp.concatenate([e / den for e in ex] + [jnp.zeros((8 - TOP_K, tt), F32)], axis=0)


def _output_projection(attn, conv, x, w_out, g_mix_post, g_ffn_pre, w_router, b_router):
    b, s, d = x.shape
    tt = 512
    mc = tt // N_PLANES
    n_t = s // tt
    woa = w_out[:ATTN_WIDTH].astype(BF16)
    woc = w_out[ATTN_WIDTH:].astype(BF16)
    wr_hi = w_router.T.astype(BF16)
    wr_split = jnp.concatenate([wr_hi, (w_router.T - wr_hi.astype(F32)).astype(BF16)], axis=0)
    permt =jnp.asarray(_plane_permutation().T, BF16)
    const = lambda shape: pl.BlockSpec(shape, lambda i, j: (0, 0))
    flat = lambda w: pl.BlockSpec((tt, w), lambda i, j: (i * n_t + j, 0))
    lanes = lambda r: pl.BlockSpec((r, tt), lambda i, j: (0, i * n_t + j))
    return pl.pallas_call(
        _outproj_kernel,
        grid=(b, n_t),
        in_specs=[pl.BlockSpec((1, N_PLANES, mc, ATTN_WIDTH), lambda i, j: (i, 0, j, 0)),
                  pl.BlockSpec((1, tt, CONV_CHANNELS), lambda i, j: (i, j, 0)),
                  pl.BlockSpec((1, tt, d), lambda i, j: (i, j, 0)),
                  const(permt.shape), const(woa.shape), const(woc.shape), const((1, d)), const((1, d)),
                  const((2 * N_EXPERTS, d)), const((N_EXPERTS, 1))],
        out_specs=[pl.BlockSpec((1, tt, d), lambda i, j: (i, j, 0)), flat(d // 2), lanes(TOP_K), lanes(8)],
        out_shape=[jax.ShapeDtypeStruct((b, s, d), F32),
                   jax.ShapeDtypeStruct((b * s, d // 2), jnp.uint32),
                   jax.ShapeDtypeStruct((TOP_K, b * s), jnp.int32),
                   jax.ShapeDtypeStruct((8, b * s), F32)],
        compiler_params=pltpu.CompilerParams(vmem_limit_bytes=VMEM_LIMIT),
        name="output_projection_router",
    )(attn, conv, x, permt, woa, woc, g_mix_post[None, :], g_ffn_pre[None, :], wr_split, b_router[:, None])


def _route_kernel(idx_ref, dest_ref, pend_ref, blk_ref, carry, pstart):
    phase = pl.program_id(0)
    step = pl.program_id(1)
    tt = idx_ref.shape[1]
    rows = lax.broadcasted_iota(jnp.int32, (N_EXPERTS, tt), 0)
    hot = [rows == idx_ref[k:k + 1, :] for k in range(TOP_K)]
    memb = sum(h.astype(F32) for h in hot)

    @pl.when((phase == 0) & (step == 0))
    def _():
        carry[...] = jnp.zeros_like(carry)

    @pl.when((phase == 1) & (step == 0))
    def _():
        counts = carry[...]
        padded = jnp.floor((counts + (EXPERT_ROWS - 1)) * (1.0 / EXPERT_ROWS)) * EXPERT_ROWS
        tri = (lax.broadcasted_iota(jnp.int32, (N_EXPERTS, N_EXPERTS), 1)
               <= lax.broadcasted_iota(jnp.int32, (N_EXPERTS, N_EXPERTS), 0)).astype(F32)
        pend = jnp.dot(tri, padded, precision=lax.Precision.HIGHEST, preferred_element_type=F32)
        pstart[...] = pend - padded
        pend_ref[...] = pend.astype(jnp.int32)
        starts = lax.broadcasted_iota(jnp.int32, (N_EXPERTS, blk_ref.shape[1]), 1) * EXPERT_ROWS
        ended = (pend.astype(jnp.int32) <= starts).astype(jnp.int32)
        blk_ref[...] = jnp.minimum(jnp.sum(ended, axis=0, keepdims=True), N_EXPERTS - 1)
        carry[...] = jnp.zeros_like(carry)

    @pl.when(phase == 1)
    def _():
        earlier = (lax.broadcasted_iota(jnp.int32, (tt, tt), 0)
                   < lax.broadcasted_iota(jnp.int32, (tt, tt), 1)).astype(BF16)
        row = jnp.dot(memb.astype(BF16), earlier, preferred_element_type=F32) + (carry[...] + pstart[...])
        dest_ref[...] = jnp.concatenate(
            [jnp.sum(jnp.where(h, row, 0.0), axis=0, keepdims=True) for h in hot], axis=0).astype(jnp.int32)

    carry[...] = carry[...] + jnp.sum(memb, axis=1, keepdims=True)


def _routing(idx, n_blk):
    _, t = idx.shape
    tt = 512
    blk_lanes = -(-n_blk // LANES) * LANES
    return pl.pallas_call(
        _route_kernel,
        grid=(2, t // tt),
        in_specs=[pl.BlockSpec((TOP_K, tt), lambda ph, i: (0, i))],
        out_specs=[pl.BlockSpec((TOP_K, tt), lambda ph, i: (0, i * ph)),
                   pl.BlockSpec((N_EXPERTS, 1), lambda ph, i: (0, 0)),
                   pl.BlockSpec((1, blk_lanes), lambda ph, i: (0, 0))],
        out_shape=[jax.ShapeDtypeStruct((TOP_K, t), jnp.int32), jax.ShapeDtypeStruct((N_EXPERTS, 1), jnp.int32),
                   jax.ShapeDtypeStruct((1, blk_lanes), jnp.int32)],
        scratch_shapes=[pltpu.VMEM((N_EXPERTS, 1), F32), pltpu.VMEM((N_EXPERTS, 1), F32)],
        compiler_params=pltpu.CompilerParams(dimension_semantics=("arbitrary", "arbitrary")),
        name="moe_routing",
    )(idx)


SC_SCAN = 16384


def _sc_token_of_row(dest_flat, cap, n_tok):
    from jax.experimental.pallas import tpu_sc as plsc
    assert n_tok & (n_tok - 1) == 0
    info = plsc.get_sparse_core_info()
    n_core, n_sub, lanes = info.num_cores, info.num_subcores, info.num_lanes
    n_src = dest_flat.shape[0]
    per_w = cap // (n_core * n_sub)
    assert per_w * n_core * n_sub == cap and per_w % lanes == 0 and n_src % SC_SCAN == 0
    mesh = plsc.VectorSubcoreMesh(core_axis_name="c", subcore_axis_name="s")

    def body(dest_hbm, out_hbm, src_v, out_v):
        lo = (lax.axis_index("s") * n_core + lax.axis_index("c")) * per_w
        lane = lax.iota(jnp.int32, lanes)

        @pl.loop(0, per_w // lanes)
        def _(i):
            out_v[pl.ds(pl.multiple_of(i * lanes, lanes), lanes)] = (lo + i * lanes + lane) & (n_tok - 1)

        @pl.loop(0, n_src // SC_SCAN)
        def _(c):
            pltpu.sync_copy(dest_hbm.at[pl.ds(pl.multiple_of(c * SC_SCAN, 8), SC_SCAN)], src_v)

            @pl.loop(0, SC_SCAN // lanes)
            def _(v):
                local = src_v[pl.ds(pl.multiple_of(v * lanes, lanes), lanes)] - lo
                mine = lax.bitcast_convert_type(local, jnp.uint32) < jnp.uint32(per_w)
                tok = (c * SC_SCAN + v * lanes + lane) & (n_tok - 1)
                plsc.store_scatter(out_v, [local], tok, mask=mine)
        pltpu.sync_copy(out_v, out_hbm.at[pl.ds(pl.multiple_of(lo, 8), per_w)])

    return pl.kernel(body, mesh=mesh, out_type=jax.ShapeDtypeStruct((cap,), jnp.int32),
                     scratch_types=[pltpu.VMEM((SC_SCAN,), jnp.int32), pltpu.VMEM((per_w,), jnp.int32)],
                     compiler_params=pltpu.CompilerParams(needs_layout_passes=False),
                     name="moe_sc_token_of_row")(dest_flat)


def _expert_kernel(blk_e_ref, nused_ref, xb_ref, wgu_ref, bgu_ref, wd_ref, bd_ref, *rest):
    yb_ref, done_ref, wgu_b, wd_b = rest[-4:]
    j = pl.program_id(0)
    done_ref[...] = jnp.zeros_like(done_ref)
    nused = nused_ref[0]
    d_ff = wd_ref.shape[1]

    new_expert = (j == 0) | (blk_e_ref[j] != blk_e_ref[jnp.maximum(j - 1, 0)])

    @pl.when((j < nused) & new_expert)
    def _():
        def cast(src, dst):
            def body(i, carry):
                rows = pl.ds(pl.multiple_of(i * LANES, LANES), LANES)
                dst[rows, :] = src[0, rows, :].astype(BF16)
                return carry
            lax.fori_loop(0, dst.shape[0] // LANES, body, 0)
        cast(wgu_ref, wgu_b)
        cast(wd_ref, wd_b)

    @pl.when(j < nused)
    def _():
        word = xb_ref[...]
        half = word.shape[1]
        lo = lax.bitcast_convert_type(word << 16, F32).astype(BF16)
        hi = lax.bitcast_convert_type(word & jnp.uint32(0xFFFF0000), F32).astype(BF16)
        gu = (jnp.dot(lo, wgu_b[:half, :], preferred_element_type=F32)
              + jnp.dot(hi, wgu_b[half:, :], preferred_element_type=F32) + bgu_ref[0])
        gate = jnp.minimum(gu[:, :d_ff], SWIGLU_LIMIT)
        up = jnp.clip(gu[:, d_ff:], -SWIGLU_LIMIT, SWIGLU_LIMIT)
        hid = (up + 1.0) * (gate * jax.nn.sigmoid(gate * SWIGLU_ALPHA))
        yb_ref[...] = jnp.dot(hid.astype(BF16), wd_b[...], preferred_element_type=F32) + bd_ref[0]

    @pl.when(j >= nused)
    def _():
        yb_ref[...] = jnp.zeros_like(yb_ref)


def _experts(xb, chunk, n_chunk, yb_prev, blk_e, nused, w_gate_up, b_gate_up, w_down, b_down):
    rows, d_in = xb.shape
    e, d, ff2 = w_gate_up.shape
    d_ff = w_down.shape[1]
    n_blk = rows // EXPERT_ROWS
    live = lambda j, nu: jnp.maximum(jnp.minimum(j, nu[0] - 1), 0)
    expert = lambda j, be, nu: (be[live(j, nu)], 0, 0)
    in_specs = [pl.BlockSpec((EXPERT_ROWS, d_in), lambda j, be, nu: (live(j, nu), 0)),
                pl.BlockSpec((1, d, ff2), expert), pl.BlockSpec((1, 1, ff2), expert),
                pl.BlockSpec((1, d_ff, d), expert), pl.BlockSpec((1, 1, d), expert)]
    args = [blk_e, nused, xb, w_gate_up, b_gate_up[:, None, :], w_down, b_down[:, None, :]]
    aliases = {}
    if yb_prev is not None:
        in_specs.append(pl.BlockSpec(memory_space=pl.ANY))
        args.append(yb_prev)
        aliases = {len(args) - 1: 0}
    grid_spec = pltpu.PrefetchScalarGridSpec(
        num_scalar_prefetch=2,
        grid=(n_blk,),
        in_specs=in_specs,
        out_specs=[pl.BlockSpec((EXPERT_ROWS, d), lambda j, be, nu: (chunk * n_blk + j, 0)),
                   pl.BlockSpec((SUBLANES, LANES), lambda j, be, nu: (0, 0))],
        scratch_shapes=[pltpu.VMEM((d, ff2), BF16), pltpu.VMEM((d_ff, d), BF16)],
    )
    return pl.pallas_call(
        _expert_kernel,
        grid_spec=grid_spec,
        out_shape=[jax.ShapeDtypeStruct((n_chunk * rows, d), F32), jax.ShapeDtypeStruct((SUBLANES, LANES), F32)],
        input_output_aliases=aliases,
        compiler_params=pltpu.CompilerParams(dimension_semantics=("arbitrary",), vmem_limit_bytes=VMEM_LIMIT),
        name="moe_experts",
    )(*args)


SC_BUFFER_BYTES = 128 * 1024
MOE_CHUNKS = 8
MOE_GROUPS = 4


def _sc_gather_rows(table, idx, part, n_part, after=None):
    from jax.experimental.pallas import tpu_sc as plsc
    n_rows = idx.shape[0] // n_part
    d = table.shape[1]
    info = plsc.get_sparse_core_info()
    n_core, n_sub = info.num_cores, info.num_subcores
    per_w = n_rows // (n_core * n_sub)
    fit = SC_BUFFER_BYTES // (d * table.dtype.itemsize)
    g_rows = max(r for r in (8, 16, 32, 64, 128) if r <= fit and per_w % r == 0)
    n_chunk = per_w // g_rows
    assert per_w * n_core * n_sub == n_rows
    mesh = plsc.VectorSubcoreMesh(core_axis_name="c", subcore_axis_name="s")

    def body(table_hbm, idx_hbm, *rest):
        out_hbm, idx_v, buf0, buf1, sem0, sem1 = rest[-6:]
        base = (lax.axis_index("s") * n_core + lax.axis_index("c")) * per_w
        pltpu.sync_copy(idx_hbm.at[pl.ds(pl.multiple_of(part * n_rows + base, 8), per_w)], idx_v)

        def gather(chunk, buf, sem):
            rows = idx_v.at[pl.ds(pl.multiple_of(chunk * g_rows, 8), g_rows)]
            return pltpu.make_async_copy(table_hbm.at[rows], buf, sem)

        def write(chunk, buf):
            pltpu.sync_copy(buf, out_hbm.at[pl.ds(pl.multiple_of(base + chunk * g_rows, 8), g_rows)])

        gather(0, buf0, sem0).start()

        @pl.loop(0, n_chunk // 2)
        def _(i):
            gather(2 * i + 1, buf1, sem1).start()
            gather(2 * i, buf0, sem0).wait()
            write(2 * i, buf0)

            @pl.when(2 * i + 2 < n_chunk)
            def _():
                gather(2 * i + 2, buf0, sem0).start()
            gather(2 * i + 1, buf1, sem1).wait()
            write(2 * i + 1, buf1)

        if n_chunk % 2:
            gather(n_chunk - 1, buf0, sem0).wait()
            write(n_chunk - 1, buf0)

    return pl.kernel(
        body, mesh=mesh, out_type=jax.ShapeDtypeStruct((n_rows, d), table.dtype),
        scratch_types=[pltpu.VMEM((per_w,), jnp.int32), pltpu.VMEM((g_rows, d), table.dtype),
                       pltpu.VMEM((g_rows, d), table.dtype), pltpu.SemaphoreType.DMA, pltpu.SemaphoreType.DMA],
        name="moe_sc_gather",
    )(table, idx, *([] if after is None else [after]))


def _sc_weighted_rows(table, idx, gates, group, n_group):
    from jax.experimental.pallas import tpu_sc as plsc
    d = table.shape[1]
    info = plsc.get_sparse_core_info()
    n_core, n_sub, lanes = info.num_cores, info.num_subcores, info.num_lanes
    n_all = idx.shape[0] // TOP_K
    n_tok = n_all // n_group
    per_w = n_tok // (n_core * n_sub)
    w = 8
    n_chunk = per_w // w
    assert per_w * n_core * n_sub == n_tok and n_chunk * w == per_w and n_chunk % 2 == 0
    mesh = plsc.VectorSubcoreMesh(core_axis_name="c", subcore_axis_name="s")

    def body(table_hbm, idx_hbm, g_hbm, out_hbm, *scr):
        idx_v, g_v, rb, ob = scr[0:4], scr[4:8], (scr[8:12], scr[12:16]), scr[16:18]
        sems, wsems = scr[18:20], scr[20:22]
        base = (lax.axis_index("s") * n_core + lax.axis_index("c")) * per_w
        for k in range(TOP_K):
            off = pl.multiple_of(k * n_all + group * n_tok + base, 8)
            pltpu.sync_copy(idx_hbm.at[pl.ds(off, per_w)], idx_v[k])
            pltpu.sync_copy(g_hbm.at[pl.ds(off, per_w)], g_v[k])

        def gathers(chunk, slot):
            rows = pl.ds(pl.multiple_of(chunk * w, 8), w)
            return [pltpu.make_async_copy(table_hbm.at[idx_v[k].at[rows]], rb[slot][k], sems[slot])
                    for k in range(TOP_K)]

        def write(chunk, slot):
            return pltpu.make_async_copy(ob[slot], out_hbm.at[pl.ds(pl.multiple_of(base + chunk * w, 8), w)],
                                         wsems[slot])

        def combine(chunk, slot):
            @pl.loop(0, w)
            def _(i):
                token = jnp.full((lanes,), chunk * w + i, jnp.int32)
                gs = [plsc.load_gather(g_v[k], [token]) for k in range(TOP_K)]
                for j in range(d // lanes):
                    cols = pl.ds(j * lanes, lanes)
                    acc = gs[0] * rb[slot][0][i, cols]
                    for k in range(1, TOP_K):
                        acc = acc + gs[k] * rb[slot][k][i, cols]
                    ob[slot][i, cols] = acc
            write(chunk, slot).start()

        for c in gathers(0, 0):
            c.start()

        @pl.loop(0, n_chunk // 2)
        def _(i):
            for c in gathers(2 * i + 1, 1):
                c.start()
            for c in gathers(2 * i, 0):
                c.wait()

            @pl.when(i > 0)
            def _():
                write(2 * i - 2, 0).wait()
            combine(2 * i, 0)

            @pl.when(2 * i + 2 < n_chunk)
            def _():
                for c in gathers(2 * i + 2, 0):
                    c.start()
            for c in gathers(2 * i + 1, 1):
                c.wait()

            @pl.when(i > 0)
            def _():
                write(2 * i - 1, 1).wait()
            combine(2 * i + 1, 1)

        write(n_chunk - 2, 0).wait()
        write(n_chunk - 1, 1).wait()

    scratch = ([pltpu.VMEM((per_w,), jnp.int32)] * TOP_K + [pltpu.VMEM((per_w,), F32)] * TOP_K
               + [pltpu.VMEM((w, d), F32)] * (2 * TOP_K + 2) + [pltpu.SemaphoreType.DMA] * 4)
    return pl.kernel(body, mesh=mesh, out_type=jax.ShapeDtypeStruct((n_tok, d), F32), scratch_types=scratch,
                     compiler_params=pltpu.CompilerParams(needs_layout_passes=False),
                     name="moe_sc_combine")(table, idx, gates)


def _combine_kernel(h_ref, p_ref, y_ref, gffn_ref, wple_ref, wpg_ref, gple_ref, o_ref):
    h2 = h_ref[...] + _rms(y_ref[...], gffn_ref[...])
    ple = (jnp.dot(p_ref[...].astype(BF16), wple_ref[...], preferred_element_type=F32)
           * jax.nn.sigmoid(jnp.dot(h2.astype(BF16), wpg_ref[...], preferred_element_type=F32)))
    o_ref[...] = h2 + _rms(ple, gple_ref[...])


def _combine_kernel_aliased(h_ref, p_ref, y_ref, gffn_ref, wple_ref, wpg_ref, gple_ref, prev_ref, o_ref):
    del prev_ref
    _combine_kernel(h_ref, p_ref, y_ref, gffn_ref, wple_ref, wpg_ref, gple_ref, o_ref)


def _combine(group, n_group, out_prev, h, p, y, g_ffn_post, w_ple_bf16, w_ple_gate_bf16, g_ple_post):
    t, d = h.shape
    tt = 256
    n_tg = t // n_group // tt
    const = lambda shape: pl.BlockSpec(shape, lambda i: (0, 0))
    tok = lambda w: pl.BlockSpec((tt, w), lambda i: (group * n_tg + i, 0))
    in_specs = [tok(d), tok(p.shape[1]), pl.BlockSpec((tt, d), lambda i: (i, 0)),
                const((1, d)), const(w_ple_bf16.shape), const(w_ple_gate_bf16.shape), const((1, d))]
    args = [h, p, y, g_ffn_post[None, :], w_ple_bf16, w_ple_gate_bf16, g_ple_post[None, :]]
    body, aliases = _combine_kernel, {}
    if out_prev is not None:
        in_specs.append(pl.BlockSpec(memory_space=pl.ANY))
        args.append(out_prev)
        body, aliases = _combine_kernel_aliased, {len(args) - 1: 0}
    return pl.pallas_call(
        body,
        grid=(n_tg,),
        in_specs=in_specs,
        out_specs=tok(d),
        out_shape=jax.ShapeDtypeStruct((t, d), F32),
        input_output_aliases=aliases,
        compiler_params=pltpu.CompilerParams(vmem_limit_bytes=VMEM_LIMIT),
        name="moe_combine_ple",
    )(*args)


def _layer(h, p, positions, g_mix_pre, w_in, w_dw, b_dw, g_conv_ln, b_conv_ln, w_out, g_mix_post, g_ffn_pre,
           w_router, b_router, w_gate_up, b_gate_up, w_down, b_down, g_ffn_post, w_ple, w_ple_gate, g_ple_post):
    b, s, d = h.shape
    t = b * s
    cos_t, sin_t = _rope_tables(positions)
    q, k, v, glu = _input_projection(h, cos_t, sin_t, g_mix_pre, w_in)
    attn = _dilated_attention(q, k, v)
    conv = _conformer_conv(glu, w_dw, b_dw, g_conv_ln, b_conv_ln)
    h1, u_ffn, idx, gates = _output_projection(attn, conv, h, w_out, g_mix_post, g_ffn_pre, w_router, b_router)
    cap = t * TOP_K + N_EXPERTS * EXPERT_ROWS
    n_blk = cap // EXPERT_ROWS
    dest, pend, blk = _routing(idx, n_blk)
    blk_e = blk[0, :n_blk]
    nused = pend[N_EXPERTS - 1] // EXPERT_ROWS
    dest_flat = dest.reshape(TOP_K * t)
    tok_of_row = _sc_token_of_row(dest_flat, cap, t)
    rows_c = cap // MOE_CHUNKS
    blk_c = n_blk // MOE_CHUNKS
    yb = done = None
    xb_next = _sc_gather_rows(u_ffn, tok_of_row, 0, MOE_CHUNKS)
    for c in range(MOE_CHUNKS):
        xb_c = xb_next
        if c + 1 < MOE_CHUNKS:
            xb_next = _sc_gather_rows(u_ffn, tok_of_row, c + 1, MOE_CHUNKS, after=done)
        nused_c = jnp.clip(nused - c * blk_c, 0, blk_c)
        yb, done = _experts(xb_c, c, MOE_CHUNKS, yb, blk_e[c * blk_c:(c + 1) * blk_c], nused_c,
                            w_gate_up, b_gate_up, w_down, b_down)
    t_g = t // MOE_GROUPS
    w_ple_bf16, w_ple_gate_bf16 = w_ple.astype(BF16), w_ple_gate.astype(BF16)
    h1_flat, p_flat = h1.reshape(t, d), p.reshape(t, -1)
    gate_flat = gates[:TOP_K].reshape(TOP_K * t)
    out = None
    for g in range(MOE_GROUPS):
        y_g = _sc_weighted_rows(yb, dest_flat, gate_flat, g, MOE_GROUPS)
        out = _combine(g, MOE_GROUPS, out, h1_flat, p_flat, y_g, g_ffn_post, w_ple_bf16, w_ple_gate_bf16,
                       g_ple_post)
    return out.reshape(b, s, d)


def kernel(x, p, positions, g_mix_pre, w_in, w_dw, b_dw, g_conv_ln, b_conv_ln, w_out, g_mix_post, g_ffn_pre,
           w_router, b_router, w_gate_up, b_gate_up, w_down, b_down, g_ffn_post, w_ple, w_ple_gate, g_ple_post):
    h = x
    for i in range(p.shape[0]):
        h = _layer(h, p[i], positions, g_mix_pre[i], w_in[i], w_dw[i], b_dw[i], g_conv_ln[i], b_conv_ln[i],
                   w_out[i], g_mix_post[i], g_ffn_pre[i], w_router[i], b_router[i], w_gate_up[i], b_gate_up[i],
                   w_down[i], b_down[i], g_ffn_post[i], w_ple[i], w_ple_gate[i], g_ple_post[i])
    return h
```

```python
import functools

import numpy as np
import jax
import jax.numpy as jnp
from jax import lax
from jax.experimental import pallas as pl
from jax.experimental.pallas import tpu as pltpu

F32 = jnp.float32
BF16 = jnp.bfloat16

HEAD_DIM = 64
N_HEADS = 12
ATTN_WIDTH = N_HEADS * HEAD_DIM
CONV_CHANNELS = 256
CONV_WIDTH = 31
ROPE_DIM = HEAD_DIM // 4
ROPE_THETA = 500000.0
N_EXPERTS = 32
TOP_K = 4
SWIGLU_LIMIT = 7.0
SWIGLU_ALPHA = 1.702
NORM_EPS = 1e-6
WINDOW = 128
N_PLANES = 16
SPAN = N_PLANES * WINDOW
LANES = 128
SUBLANES = 8
NEG = -1e30
EXPERT_ROWS = 256
VMEM_LIMIT = 56 * 1024 * 1024


def _residue_of_plane(p):
    return 4 * (p % 4) + p // 4


def _rms(xv, g):
    var = jnp.mean(xv * xv, axis=-1, keepdims=True)
    return xv * lax.rsqrt(var + NORM_EPS) * g


def _rope_kernel(pos_ref, invf_ref, expand_ref, one_ref, sgn_ref, c_ref, s_ref):
    tn = (((0,), (0,)), ((), ()))

    def spread(t):
        t1 = t.astype(BF16)
        r1 = t - t1.astype(F32)
        t2 = r1.astype(BF16)
        t3 = (r1 - t2.astype(F32)).astype(BF16)
        return sum(lax.dot_general(piece, expand_ref[...], tn, preferred_element_type=F32) for piece in (t1, t2, t3))

    for p in range(N_PLANES):
        ang = invf_ref[...] * pos_ref[0, p:p + 1, :].astype(F32)
        c_ref[0, p] = spread(jnp.cos(ang)) + one_ref[...]
        s_ref[0, p] = spread(jnp.sin(ang)) * sgn_ref[...]


def _rope_tables(positions):
    b, s = positions.shape
    sm = s // N_PLANES
    mt = min(sm, 128)
    plane_res = np.array([_residue_of_plane(p) for p in range(N_PLANES)])
    pos_planes = positions.reshape(b, sm, N_PLANES).transpose(0, 2, 1)[:, plane_res]
    half = ROPE_DIM // 2
    lane = np.arange(LANES) % HEAD_DIM
    inv_freq = ROPE_THETA ** (-jnp.arange(0, ROPE_DIM, 2, dtype=F32) / ROPE_DIM)
    rotary = lane < ROPE_DIM
    expand = jnp.asarray((np.arange(half)[:, None] == lane[None, :] % half) & rotary[None, :], BF16)
    one = jnp.asarray(~rotary, F32)[None, :]
    sgn = jnp.asarray(np.where(lane < half, -1.0, 1.0), F32)[None, :]
    row = pl.BlockSpec((1, LANES), lambda i, j: (0, 0))
    out = pl.BlockSpec((1, N_PLANES, mt, LANES), lambda i, j: (i, 0, j, 0))
    return pl.pallas_call(
        _rope_kernel,
        grid=(b, sm // mt),
        in_specs=[pl.BlockSpec((1, N_PLANES, mt), lambda i, j: (i, 0, j)),
                  pl.BlockSpec((half, 1), lambda i, j: (0, 0)), pl.BlockSpec((half, LANES), lambda i, j: (0, 0)),
                  row, row],
        out_specs=[out, out],
        out_shape=[jax.ShapeDtypeStruct((b, N_PLANES, sm, LANES), F32)] * 2,
        name="rope_tables",
    )(pos_planes, inv_freq[:, None], expand, one, sgn)


PERM_TOKENS = 256
PERM_ROWS = PERM_TOKENS // N_PLANES


def _plane_permutation():
    perm = np.zeros((PERM_TOKENS, PERM_TOKENS), np.float32)
    for p in range(N_PLANES):
        for ml in range(PERM_ROWS):
            perm[PERM_ROWS * p + ml, N_PLANES * ml + _residue_of_plane(p)] = 1.0
    return perm


def _inproj_kernel(x_ref, c_ref, s_ref, g_ref, perm_ref, wqkv_ref, wc_ref, q_ref, k_ref, v_ref, glu_ref):
    g = g_ref[...]
    tt = x_ref.shape[1]
    lane = lax.broadcasted_iota(jnp.int32, (1, LANES), 1) % HEAD_DIM
    first_half = lane < ROPE_DIM // 2

    def rotary(t, cos, sin):
        outs = []
        for j in range(ATTN_WIDTH // LANES):
            tj = t[:, j * LANES:(j + 1) * LANES]
            partner = jnp.where(first_half, pltpu.roll(tj, LANES - ROPE_DIM // 2, 1),
                                pltpu.roll(tj, ROPE_DIM // 2, 1))
            outs.append(tj * cos + partner * sin)
        return jnp.concatenate(outs, axis=1)

    un = _rms(x_ref[0], g).astype(BF16)
    pc = jnp.dot(un, wc_ref[...], preferred_element_type=F32)
    glu_ref[0] = pc[:, :CONV_CHANNELS] * jax.nn.sigmoid(pc[:, CONV_CHANNELS:])

    for sub in range(tt // PERM_TOKENS):
        rows = slice(sub * PERM_ROWS, (sub + 1) * PERM_ROWS)
        u = jnp.dot(perm_ref[...], un[sub * PERM_TOKENS:(sub + 1) * PERM_TOKENS],
                    preferred_element_type=F32).astype(BF16)
        cos = jnp.concatenate([c_ref[0, p, rows, :] for p in range(N_PLANES)], axis=0)
        sin = jnp.concatenate([s_ref[0, p, rows, :] for p in range(N_PLANES)], axis=0)
        proj = jnp.dot(u, wqkv_ref[...], preferred_element_type=F32)
        q = (rotary(proj[:, :ATTN_WIDTH], cos, sin) * (HEAD_DIM ** -0.5)).astype(BF16)
        k = rotary(proj[:, ATTN_WIDTH:2 * ATTN_WIDTH], cos, sin).astype(BF16)
        v = proj[:, 2 * ATTN_WIDTH:].astype(BF16)
        for p in range(N_PLANES):
            chunk = slice(p * PERM_ROWS, (p + 1) * PERM_ROWS)
            q_ref[0, p, rows, :] = q[chunk]
            k_ref[0, p, rows, :] = k[chunk]
            v_ref[0, p, rows, :] = v[chunk]


def _input_projection(x, cos_t, sin_t, g_mix_pre, w_in):
    b, s, d = x.shape
    tt = 1024
    mc = tt // N_PLANES
    wqkv = w_in[:, :3 * ATTN_WIDTH].astype(BF16)
    wc = w_in[:, 3 * ATTN_WIDTH:].astype(BF16)
    perm = jnp.asarray(_plane_permutation(), BF16)
    plane = lambda w: pl.BlockSpec((1, N_PLANES, mc, w), lambda i, j: (i, 0, j, 0))
    plane_shape = jax.ShapeDtypeStruct((b, N_PLANES, s // N_PLANES, ATTN_WIDTH), BF16)
    tok = lambda w: pl.BlockSpec((1, tt, w), lambda i, j: (i, j, 0))
    const = lambda shape: pl.BlockSpec(shape, lambda i, j: (0, 0))
    return pl.pallas_call(
        _inproj_kernel,
        grid=(b, s // tt),
        in_specs=[tok(d), plane(LANES), plane(LANES), const((1, d)), const(perm.shape),
                  const(wqkv.shape), const(wc.shape)],
        out_specs=[plane(ATTN_WIDTH), plane(ATTN_WIDTH), plane(ATTN_WIDTH), tok(CONV_CHANNELS)],
        out_shape=[plane_shape, plane_shape, plane_shape,
                   jax.ShapeDtypeStruct((b, s, CONV_CHANNELS), F32)],
        compiler_params=pltpu.CompilerParams(vmem_limit_bytes=VMEM_LIMIT),
        name="input_projection",
    )(x, cos_t, sin_t, g_mix_pre[None, :], perm, wqkv, wc)


def _attention_biases():
    band = lambda j: np.where((j >= 0) & (j <= WINDOW), 0.0, NEG).astype(np.float32)
    cols = lambda m: np.where(m, NEG, 0.0).astype(np.float32)
    twice = lambda a: np.concatenate([a, a], axis=0)
    mq = np.arange(WINDOW)[:, None]
    kj = np.arange(2 * WINDOW)[None, :]
    j16 = mq + WINDOW - kj
    prev16 = kj < WINDOW
    row = np.arange(128)[:, None]
    col = np.arange(256)[None, :]
    j4 = 4 * (row % 32 - (col % 64 - 32)) + row // 32 - col // 64
    prev4 = col % 64 < 32
    row = np.arange(256)[:, None]
    col = np.arange(512)[None, :]
    res = np.vectorize(_residue_of_plane)
    j1 = 16 * (row % 16 - (col % 32 - 16)) + res(row // 16) - res(col // 32)
    prev1 = col % 32 < 16
    b1 = np.stack([twice(band(j1[:128])), twice(band(j1[128:]))])
    return [jnp.asarray(a) for a in (twice(band(j16)), twice(band(j4)), b1, cols(prev16), cols(prev4), cols(prev1))]


UNROLL = 8


def _attn_kernel(q_ref, kc_ref, kp_ref, vc_ref, vp_ref, b16_ref, b4_ref, b1_ref, p16_ref, p4_ref, p1_ref,
                 o_ref, m_scr, l_scr, a_scr, bias16_scr):
    no_prev = (pl.program_id(1) == 0).astype(F32)
    head0 = lax.broadcasted_iota(jnp.int32, (1, LANES), 1) < HEAD_DIM
    bias16_scr[...] = b16_ref[...] + no_prev * p16_ref[...]

    def tile(qt, kt, vt, bias):
        n = qt.shape[0]
        zero = jnp.zeros_like(qt)
        q2 = jnp.concatenate([jnp.where(head0, qt, zero), jnp.where(head0, zero, qt)], axis=0)
        s = lax.dot_general(q2, kt, (((1,), (1,)), ((), ())), preferred_element_type=F32) + bias
        m = jnp.max(s, axis=-1, keepdims=True)
        e = jnp.exp(s - m).astype(BF16)
        va = jnp.concatenate([vt, jnp.ones_like(vt)], axis=1)
        o = jnp.dot(e, va, preferred_element_type=F32)
        pick = lambda top, bot: jnp.where(head0, top, bot)
        mm = pick(jnp.broadcast_to(m[:n], (n, LANES)), jnp.broadcast_to(m[n:], (n, LANES)))
        return mm, pick(o[:n, LANES:], o[n:, LANES:]), pick(o[:n, :LANES], o[n:, :LANES])

    def put(branch, start, size, stats, off):
        for scr, val in zip((m_scr, l_scr, a_scr), stats):
            scr[branch, pl.ds(start, size), :] = val[off:off + size]

    def body16(i, carry):
        for p in [UNROLL * i + u for u in range(UNROLL)]:
            kt = jnp.concatenate([kp_ref[0, p], kc_ref[0, p]], axis=0)
            vt = jnp.concatenate([vp_ref[0, p], vc_ref[0, p]], axis=0)
            put(0, pl.multiple_of(p * WINDOW, WINDOW), WINDOW, tile(q_ref[0, p], kt, vt, bias16_scr[...]), 0)
        return carry
    lax.fori_loop(0, N_PLANES // UNROLL, body16, 0)

    def body4(g, carry):
        for c, i in [(2 * g + cc, ii) for cc in range(2) for ii in range(4)]:
            qt = jnp.concatenate([q_ref[0, 4 * c + a, 32 * i:32 * i + 32, :] for a in range(4)], axis=0)
            if i == 0:
                ks = [x for a in range(4) for x in (kp_ref[0, 4 * c + a, 96:128, :], kc_ref[0, 4 * c + a, 0:32, :])]
                vs = [x for a in range(4) for x in (vp_ref[0, 4 * c + a, 96:128, :], vc_ref[0, 4 * c + a, 0:32, :])]
                bias = b4_ref[...] + no_prev * p4_ref[...]
            else:
                ks = [kc_ref[0, 4 * c + a, 32 * i - 32:32 * i + 32, :] for a in range(4)]
                vs = [vc_ref[0, 4 * c + a, 32 * i - 32:32 * i + 32, :] for a in range(4)]
                bias = b4_ref[...]
            stats = tile(qt, jnp.concatenate(ks, axis=0), jnp.concatenate(vs, axis=0), bias)
            for a in range(4):
                put(1, pl.multiple_of((4 * c + a) * WINDOW + 32 * i, 32), 32, stats, 32 * a)
        return carry
    lax.fori_loop(0, 2, body4, 0)

    def tile1(i, first):
        rq = pl.ds(pl.multiple_of(16 * i, 16), 16)
        if first:
            ks = [x for p in range(N_PLANES) for x in (kp_ref[0, p, 112:128, :], kc_ref[0, p, 0:16, :])]
            vs = [x for p in range(N_PLANES) for x in (vp_ref[0, p, 112:128, :], vc_ref[0, p, 0:16, :])]
        else:
            rk = pl.ds(pl.multiple_of(16 * i - 16, 16), 32)
            ks = [kc_ref[0, p, rk, :] for p in range(N_PLANES)]
            vs = [vc_ref[0, p, rk, :] for p in range(N_PLANES)]
        kt = jnp.concatenate(ks, axis=0)
        vt = jnp.concatenate(vs, axis=0)
        for half in range(2):
            planes = range(8 * half, 8 * half + 8)
            qt = jnp.concatenate([q_ref[0, p, rq, :] for p in planes], axis=0)
            bias = b1_ref[half] + no_prev * p1_ref[...] if first else b1_ref[half]
            stats = tile(qt, kt, vt, bias)
            for p in planes:
                put(2, pl.multiple_of(p * WINDOW + 16 * i, 16), 16, stats, 16 * (p - 8 * half))

    tile1(0, True)
    tile1(1, False)

    def body1(g, carry):
        tile1(2 * g, False)
        tile1(2 * g + 1, False)
        return carry
    lax.fori_loop(1, WINDOW // 32, body1, 0)

    def combine(i, carry):
        for p in (2 * i, 2 * i + 1):
            rows = pl.ds(pl.multiple_of(p * WINDOW, WINDOW), WINDOW)
            ms = [m_scr[b, rows, :] for b in range(3)]
            mx = jnp.maximum(jnp.maximum(ms[0], ms[1]), ms[2])
            ws = [jnp.exp(m - mx) for m in ms]
            den = ws[0] * l_scr[0, rows, :] + ws[1] * l_scr[1, rows, :] + ws[2] * l_scr[2, rows, :]
            num = ws[0] * a_scr[0, rows, :] + ws[1] * a_scr[1, rows, :] + ws[2] * a_scr[2, rows, :]
            o_ref[0, p] = (num / den).astype(BF16)
        return carry
    lax.fori_loop(0, N_PLANES // 2, combine, 0)


def _dilated_attention(q, k, v):
    b, _, sm, _ = q.shape
    n_span = sm // WINDOW
    cur = pl.BlockSpec((1, N_PLANES, WINDOW, LANES), lambda i, j, h: (i, 0, j, h))
    prev = pl.BlockSpec((1, N_PLANES, WINDOW, LANES), lambda i, j, h: (i, 0, jnp.maximum(j - 1, 0), h))
    biases = _attention_biases()
    bias_specs = [pl.BlockSpec(a.shape, lambda i, j, h, nd=a.ndim: (0,) * nd) for a in biases]
    stats = pltpu.VMEM((3, SPAN, LANES), F32)
    return pl.pallas_call(
        _attn_kernel,
        grid=(b, n_span, ATTN_WIDTH // LANES),
        in_specs=[cur, cur, prev, cur, prev] + bias_specs,
        out_specs=cur,
        out_shape=jax.ShapeDtypeStruct(q.shape, BF16),
        scratch_shapes=[stats, stats, stats, pltpu.VMEM((2 * WINDOW, 2 * WINDOW), F32)],
        compiler_params=pltpu.CompilerParams(vmem_limit_bytes=VMEM_LIMIT),
        name="dilated_attention",
    )(q, k, k, v, v, *biases)


CONV_HALO = 32


def _conv_kernel(cur_ref, prev_ref, w_ref, b_ref, g_ref, bl_ref, o_ref, scr, *, chunk):
    tt = cur_ref.shape[1]
    has_prev = pl.program_id(1) > 0
    scr[0, 0:CONV_HALO, :] = jnp.where(has_prev, prev_ref[0], 0.0)
    scr[0, CONV_HALO:CONV_HALO + tt, :] = cur_ref[0]
    aligned_rows = tt + CONV_HALO - SUBLANES
    for s in range(1, SUBLANES):
        scr[s, 0:aligned_rows, :] = scr[0, s:s + aligned_rows, :]
    lead = CONV_HALO - (CONV_WIDTH - 1)
    for c0 in range(0, tt, chunk):
        acc = jnp.zeros((chunk, CONV_CHANNELS), F32)
        for j in range(CONV_WIDTH):
            s, a = (lead + j) % SUBLANES, (lead + j) // SUBLANES * SUBLANES
            acc = acc + w_ref[j:j + 1, :] * scr[s, c0 + a:c0 + a + chunk, :]
        y = acc + b_ref[...]
        mu = jnp.mean(y, axis=-1, keepdims=True)
        var = jnp.mean(jnp.square(y - mu), axis=-1, keepdims=True)
        yn = (y - mu) * lax.rsqrt(var + NORM_EPS) * g_ref[...] + bl_ref[...]
        o_ref[0, c0:c0 + chunk, :] = (yn * jax.nn.sigmoid(yn)).astype(BF16)


def _conformer_conv(glu, w_dw, b_dw, g_ln, b_ln):
    b, s, c = glu.shape
    tt = 512
    row = pl.BlockSpec((1, c), lambda i, j: (0, 0))
    return pl.pallas_call(
        functools.partial(_conv_kernel, chunk=128),
        grid=(b, s // tt),
        in_specs=[pl.BlockSpec((1, tt, c), lambda i, j: (i, j, 0)),
                  pl.BlockSpec((1, CONV_HALO, c), lambda i, j: (i, jnp.maximum(j * (tt // CONV_HALO) - 1, 0), 0)),
                  pl.BlockSpec((CONV_WIDTH, c), lambda i, j: (0, 0)), row, row, row],
        out_specs=pl.BlockSpec((1, tt, c), lambda i, j: (i, j, 0)),
        out_shape=jax.ShapeDtypeStruct((b, s, c), BF16),
        scratch_shapes=[pltpu.VMEM((SUBLANES, CONV_HALO + tt, c), F32)],
        name="conformer_conv",
    )(glu, glu, w_dw[:, 0, :], b_dw[None, :], g_ln[None, :], b_ln[None, :])


def _outproj_kernel(attn_ref, conv_ref, x_ref, permt_ref, woa_ref, woc_ref, gpost_ref, gffn_ref, wr_ref, br_ref,
                    h_ref, u_ref, idx_ref, gate_ref):
    tt = x_ref.shape[1]
    nat = []
    for sub in range(tt // PERM_TOKENS):
        rows = slice(sub * PERM_ROWS, (sub + 1) * PERM_ROWS)
        a = jnp.concatenate([attn_ref[0, p, rows, :] for p in range(N_PLANES)], axis=0)
        nat.append(jnp.dot(permt_ref[...], a, preferred_element_type=F32).astype(BF16))
    mix = (jnp.dot(jnp.concatenate(nat, axis=0), woa_ref[...], preferred_element_type=F32)
           + jnp.dot(conv_ref[0], woc_ref[...], preferred_element_type=F32))
    h = x_ref[0] + _rms(mix, gpost_ref[...])
    h_ref[0] = h
    u = _rms(h, gffn_ref[...])
    half = u.shape[1] // 2
    ub = u.astype(BF16).astype(F32)
    u_ref[...] = ((lax.bitcast_convert_type(ub[:, :half], jnp.uint32) >> 16)
                  | (lax.bitcast_convert_type(ub[:, half:], jnp.uint32) & jnp.uint32(0xFFFF0000)))
    u_hi = u.astype(BF16)
    u_lo = (u - u_hi.astype(F32)).astype(BF16)
    nt = (((1,), (1,)), ((), ()))
    by_hi = lax.dot_general(wr_ref[...], u_hi, nt, preferred_element_type=F32)
    by_lo = lax.dot_general(wr_ref[:N_EXPERTS], u_lo, nt, preferred_element_type=F32)
    logits = by_hi[:N_EXPERTS] + (by_hi[N_EXPERTS:] + by_lo) + br_ref[...]
    rows = lax.broadcasted_iota(jnp.int32, logits.shape, 0)
    vals = logits
    tops, idxs = [], []
    for _ in range(TOP_K):
        mx = jnp.max(vals, axis=0, keepdims=True)
        ix = jnp.min(jnp.where(vals == mx, rows, N_EXPERTS), axis=0, keepdims=True)
        tops.append(mx)
        idxs.append(ix)
        vals = jnp.where(rows == ix, -jnp.inf, vals)
    ex = [jnp.exp(t - tops[0]) for t in tops]
    den = ex[0] + ex[1] + ex[2] + ex[3]
    idx_ref[...] = jnp.concatenate(idxs, axis=0)
    gate_ref[...] = jnp.concatenate([e / den for e in ex] + [jnp.zeros((8 - TOP_K, tt), F32)], axis=0)


def _output_projection(attn, conv, x, w_out, g_mix_post, g_ffn_pre, w_router, b_router):
    b, s, d = x.shape
    tt = 512
    mc = tt // N_PLANES
    n_t = s // tt
    woa = w_out[:ATTN_WIDTH].astype(BF16)
    woc = w_out[ATTN_WIDTH:].astype(BF16)
    wr_hi = w_router.T.astype(BF16)
    wr_split = jnp.concatenate([wr_hi, (w_router.T - wr_hi.astype(F32)).astype(BF16)], axis=0)
    permt =jnp.asarray(_plane_permutation().T, BF16)
    const = lambda shape: pl.BlockSpec(shape, lambda i, j: (0, 0))
    flat = lambda w: pl.BlockSpec((tt, w), lambda i, j: (i * n_t + j, 0))
    lanes = lambda r: pl.BlockSpec((r, tt), lambda i, j: (0, i * n_t + j))
    return pl.pallas_call(
        _outproj_kernel,
        grid=(b, n_t),
        in_specs=[pl.BlockSpec((1, N_PLANES, mc, ATTN_WIDTH), lambda i, j: (i, 0, j, 0)),
                  pl.BlockSpec((1, tt, CONV_CHANNELS), lambda i, j: (i, j, 0)),
                  pl.BlockSpec((1, tt, d), lambda i, j: (i, j, 0)),
                  const(permt.shape), const(woa.shape), const(woc.shape), const((1, d)), const((1, d)),
                  const((2 * N_EXPERTS, d)), const((N_EXPERTS, 1))],
        out_specs=[pl.BlockSpec((1, tt, d), lambda i, j: (i, j, 0)), flat(d // 2), lanes(TOP_K), lanes(8)],
        out_shape=[jax.ShapeDtypeStruct((b, s, d), F32),
                   jax.ShapeDtypeStruct((b * s, d // 2), jnp.uint32),
                   jax.ShapeDtypeStruct((TOP_K, b * s), jnp.int32),
                   jax.ShapeDtypeStruct((8, b * s), F32)],
        compiler_params=pltpu.CompilerParams(vmem_limit_bytes=VMEM_LIMIT),
        name="output_projection_router",
    )(attn, conv, x, permt, woa, woc, g_mix_post[None, :], g_ffn_pre[None, :], wr_split, b_router[:, None])


def _route_kernel(idx_ref, dest_ref, pend_ref, blk_ref, carry, pstart):
    phase = pl.program_id(0)
    step = pl.program_id(1)
    tt = idx_ref.shape[1]
    rows = lax.broadcasted_iota(jnp.int32, (N_EXPERTS, tt), 0)
    hot = [rows == idx_ref[k:k + 1, :] for k in range(TOP_K)]
    memb = sum(h.astype(F32) for h in hot)

    @pl.when((phase == 0) & (step == 0))
    def _():
        carry[...] = jnp.zeros_like(carry)

    @pl.when((phase == 1) & (step == 0))
    def _():
        counts = carry[...]
        padded = jnp.floor((counts + (EXPERT_ROWS - 1)) * (1.0 / EXPERT_ROWS)) * EXPERT_ROWS
        tri = (lax.broadcasted_iota(jnp.int32, (N_EXPERTS, N_EXPERTS), 1)
               <= lax.broadcasted_iota(jnp.int32, (N_EXPERTS, N_EXPERTS), 0)).astype(F32)
        pend = jnp.dot(tri, padded, precision=lax.Precision.HIGHEST, preferred_element_type=F32)
        pstart[...] = pend - padded
        pend_ref[...] = pend.astype(jnp.int32)
        starts = lax.broadcasted_iota(jnp.int32, (N_EXPERTS, blk_ref.shape[1]), 1) * EXPERT_ROWS
        ended = (pend.astype(jnp.int32) <= starts).astype(jnp.int32)
        blk_ref[...] = jnp.minimum(jnp.sum(ended, axis=0, keepdims=True), N_EXPERTS - 1)
        carry[...] = jnp.zeros_like(carry)

    @pl.when(phase == 1)
    def _():
        earlier = (lax.broadcasted_iota(jnp.int32, (tt, tt), 0)
                   < lax.broadcasted_iota(jnp.int32, (tt, tt), 1)).astype(BF16)
        row = jnp.dot(memb.astype(BF16), earlier, preferred_element_type=F32) + (carry[...] + pstart[...])
        dest_ref[...] = jnp.concatenate(
            [jnp.sum(jnp.where(h, row, 0.0), axis=0, keepdims=True) for h in hot], axis=0).astype(jnp.int32)

    carry[...] = carry[...] + jnp.sum(memb, axis=1, keepdims=True)


def _routing(idx, n_blk):
    _, t = idx.shape
    tt = 512
    blk_lanes = -(-n_blk // LANES) * LANES
    return pl.pallas_call(
        _route_kernel,
        grid=(2, t // tt),
        in_specs=[pl.BlockSpec((TOP_K, tt), lambda ph, i: (0, i))],
        out_specs=[pl.BlockSpec((TOP_K, tt), lambda ph, i: (0, i * ph)),
                   pl.BlockSpec((N_EXPERTS, 1), lambda ph, i: (0, 0)),
                   pl.BlockSpec((1, blk_lanes), lambda ph, i: (0, 0))],
        out_shape=[jax.ShapeDtypeStruct((TOP_K, t), jnp.int32), jax.ShapeDtypeStruct((N_EXPERTS, 1), jnp.int32),
                   jax.ShapeDtypeStruct((1, blk_lanes), jnp.int32)],
        scratch_shapes=[pltpu.VMEM((N_EXPERTS, 1), F32), pltpu.VMEM((N_EXPERTS, 1), F32)],
        compiler_params=pltpu.CompilerParams(dimension_semantics=("arbitrary", "arbitrary")),
        name="moe_routing",
    )(idx)


SC_SCAN = 16384
SC_UNROLL = 8


def _sc_token_of_row(dest_flat, cap, n_tok):
    from jax.experimental.pallas import tpu_sc as plsc
    assert n_tok & (n_tok - 1) == 0
    info = plsc.get_sparse_core_info()
    n_core, n_sub, lanes = info.num_cores, info.num_subcores, info.num_lanes
    n_src = dest_flat.shape[0]
    per_w = cap // (n_core * n_sub)
    assert per_w * n_core * n_sub == cap and per_w % lanes == 0 and n_src % SC_SCAN == 0
    mesh = plsc.VectorSubcoreMesh(core_axis_name="c", subcore_axis_name="s")

    def body(dest_hbm, out_hbm, src_v, out_v):
        lo = (lax.axis_index("s") * n_core + lax.axis_index("c")) * per_w
        lane = lax.iota(jnp.int32, lanes)

        @pl.loop(0, per_w // lanes)
        def _(i):
            out_v[pl.ds(pl.multiple_of(i * lanes, lanes), lanes)] = (lo + i * lanes + lane) & (n_tok - 1)

        @pl.loop(0, n_src // SC_SCAN)
        def _(c):
            pltpu.sync_copy(dest_hbm.at[pl.ds(pl.multiple_of(c * SC_SCAN, 8), SC_SCAN)], src_v)

            @pl.loop(0, SC_SCAN // (lanes * SC_UNROLL))
            def _(g):
                for u in range(SC_UNROLL):
                    at = (g * SC_UNROLL + u) * lanes
                    local = src_v[pl.ds(pl.multiple_of(at, lanes), lanes)] - lo
                    mine = lax.bitcast_convert_type(local, jnp.uint32) < jnp.uint32(per_w)
                    tok = (c * SC_SCAN + at + lane) & (n_tok - 1)
                    plsc.store_scatter(out_v, [local], tok, mask=mine)
        pltpu.sync_copy(out_v, out_hbm.at[pl.ds(pl.multiple_of(lo, 8), per_w)])

    return pl.kernel(body, mesh=mesh, out_type=jax.ShapeDtypeStruct((cap,), jnp.int32),
                     scratch_types=[pltpu.VMEM((SC_SCAN,), jnp.int32), pltpu.VMEM((per_w,), jnp.int32)],
                     compiler_params=pltpu.CompilerParams(needs_layout_passes=False),
                     name="moe_sc_token_of_row")(dest_flat)


def _expert_kernel(blk_e_ref, nused_ref, xb_ref, wgu_ref, bgu_ref, wd_ref, bd_ref, *rest):
    yb_ref, done_ref, wgu_b, wd_b = rest[-4:]
    j = pl.program_id(0)
    done_ref[...] = jnp.zeros_like(done_ref)
    nused = nused_ref[0]
    d_ff = wd_ref.shape[1]

    new_expert = (j == 0) | (blk_e_ref[j] != blk_e_ref[jnp.maximum(j - 1, 0)])

    @pl.when((j < nused) & new_expert)
    def _():
        def cast(src, dst):
            def body(i, carry):
                rows = pl.ds(pl.multiple_of(i * LANES, LANES), LANES)
                dst[rows, :] = src[0, rows, :].astype(BF16)
                return carry
            lax.fori_loop(0, dst.shape[0] // LANES, body, 0)
        cast(wgu_ref, wgu_b)
        cast(wd_ref, wd_b)

    @pl.when(j < nused)
    def _():
        word = xb_ref[...]
        half = word.shape[1]
        lo = lax.bitcast_convert_type(word << 16, F32).astype(BF16)
        hi = lax.bitcast_convert_type(word & jnp.uint32(0xFFFF0000), F32).astype(BF16)
        gu = (jnp.dot(lo, wgu_b[:half, :], preferred_element_type=F32)
              + jnp.dot(hi, wgu_b[half:, :], preferred_element_type=F32) + bgu_ref[0])
        gate = jnp.minimum(gu[:, :d_ff], SWIGLU_LIMIT)
        up = jnp.clip(gu[:, d_ff:], -SWIGLU_LIMIT, SWIGLU_LIMIT)
        hid = (up + 1.0) * (gate * jax.nn.sigmoid(gate * SWIGLU_ALPHA))
        yb_ref[...] = jnp.dot(hid.astype(BF16), wd_b[...], preferred_element_type=F32) + bd_ref[0]

    @pl.when(j >= nused)
    def _():
        yb_ref[...] = jnp.zeros_like(yb_ref)


def _experts(xb, chunk, n_chunk, yb_prev, blk_e, nused, w_gate_up, b_gate_up, w_down, b_down):
    rows, d_in = xb.shape
    e, d, ff2 = w_gate_up.shape
    d_ff = w_down.shape[1]
    n_blk = rows // EXPERT_ROWS
    live = lambda j, nu: jnp.maximum(jnp.minimum(j, nu[0] - 1), 0)
    expert = lambda j, be, nu: (be[live(j, nu)], 0, 0)
    in_specs = [pl.BlockSpec((EXPERT_ROWS, d_in), lambda j, be, nu: (live(j, nu), 0)),
                pl.BlockSpec((1, d, ff2), expert), pl.BlockSpec((1, 1, ff2), expert),
                pl.BlockSpec((1, d_ff, d), expert), pl.BlockSpec((1, 1, d), expert)]
    args = [blk_e, nused, xb, w_gate_up, b_gate_up[:, None, :], w_down, b_down[:, None, :]]
    aliases = {}
    if yb_prev is not None:
        in_specs.append(pl.BlockSpec(memory_space=pl.ANY))
        args.append(yb_prev)
        aliases = {len(args) - 1: 0}
    grid_spec = pltpu.PrefetchScalarGridSpec(
        num_scalar_prefetch=2,
        grid=(n_blk,),
        in_specs=in_specs,
        out_specs=[pl.BlockSpec((EXPERT_ROWS, d), lambda j, be, nu: (chunk * n_blk + j, 0)),
                   pl.BlockSpec((SUBLANES, LANES), lambda j, be, nu: (0, 0))],
        scratch_shapes=[pltpu.VMEM((d, ff2), BF16), pltpu.VMEM((d_ff, d), BF16)],
    )
    return pl.pallas_call(
        _expert_kernel,
        grid_spec=grid_spec,
        out_shape=[jax.ShapeDtypeStruct((n_chunk * rows, d), F32), jax.ShapeDtypeStruct((SUBLANES, LANES), F32)],
        input_output_aliases=aliases,
        compiler_params=pltpu.CompilerParams(dimension_semantics=("arbitrary",), vmem_limit_bytes=VMEM_LIMIT),
        name="moe_experts",
    )(*args)


SC_BUFFER_BYTES = 128 * 1024
MOE_CHUNKS = 8
MOE_GROUPS = 4


def _sc_gather_rows(table, idx, part, n_part, after=None):
    from jax.experimental.pallas import tpu_sc as plsc
    n_rows = idx.shape[0] // n_part
    d = table.shape[1]
    info = plsc.get_sparse_core_info()
    n_core, n_sub = info.num_cores, info.num_subcores
    per_w = n_rows // (n_core * n_sub)
    fit = SC_BUFFER_BYTES // (d * table.dtype.itemsize)
    g_rows = max(r for r in (8, 16, 32, 64, 128) if r <= fit and per_w % r == 0)
    n_chunk = per_w // g_rows
    assert per_w * n_core * n_sub == n_rows
    mesh = plsc.VectorSubcoreMesh(core_axis_name="c", subcore_axis_name="s")

    def body(table_hbm, idx_hbm, *rest):
        out_hbm, idx_v, buf0, buf1, sem0, sem1 = rest[-6:]
        base = (lax.axis_index("s") * n_core + lax.axis_index("c")) * per_w
        pltpu.sync_copy(idx_hbm.at[pl.ds(pl.multiple_of(part * n_rows + base, 8), per_w)], idx_v)

        def gather(chunk, buf, sem):
            rows = idx_v.at[pl.ds(pl.multiple_of(chunk * g_rows, 8), g_rows)]
            return pltpu.make_async_copy(table_hbm.at[rows], buf, sem)

        def write(chunk, buf):
            pltpu.sync_copy(buf, out_hbm.at[pl.ds(pl.multiple_of(base + chunk * g_rows, 8), g_rows)])

        gather(0, buf0, sem0).start()

        @pl.loop(0, n_chunk // 2)
        def _(i):
            gather(2 * i + 1, buf1, sem1).start()
            gather(2 * i, buf0, sem0).wait()
            write(2 * i, buf0)

            @pl.when(2 * i + 2 < n_chunk)
            def _():
                gather(2 * i + 2, buf0, sem0).start()
            gather(2 * i + 1, buf1, sem1).wait()
            write(2 * i + 1, buf1)

        if n_chunk % 2:
            gather(n_chunk - 1, buf0, sem0).wait()
            write(n_chunk - 1, buf0)

    return pl.kernel(
        body, mesh=mesh, out_type=jax.ShapeDtypeStruct((n_rows, d), table.dtype),
        scratch_types=[pltpu.VMEM((per_w,), jnp.int32), pltpu.VMEM((g_rows, d), table.dtype),
                       pltpu.VMEM((g_rows, d), table.dtype), pltpu.SemaphoreType.DMA, pltpu.SemaphoreType.DMA],
        name="moe_sc_gather",
    )(table, idx, *([] if after is None else [after]))


def _sc_weighted_rows(table, idx, gates, group, n_group):
    from jax.experimental.pallas import tpu_sc as plsc
    d = table.shape[1]
    info = plsc.get_sparse_core_info()
    n_core, n_sub, lanes = info.num_cores, info.num_subcores, info.num_lanes
    n_all = idx.shape[0] // TOP_K
    n_tok = n_all // n_group
    per_w = n_tok // (n_core * n_sub)
    w = 8
    n_chunk = per_w // w
    assert per_w * n_core * n_sub == n_tok and n_chunk * w == per_w and n_chunk % 2 == 0
    mesh = plsc.VectorSubcoreMesh(core_axis_name="c", subcore_axis_name="s")

    def body(table_hbm, idx_hbm, g_hbm, out_hbm, *scr):
        idx_v, g_v, rb, ob = scr[0:4], scr[4:8], (scr[8:12], scr[12:16]), scr[16:18]
        sems, wsems = scr[18:20], scr[20:22]
        base = (lax.axis_index("s") * n_core + lax.axis_index("c")) * per_w
        for k in range(TOP_K):
            off = pl.multiple_of(k * n_all + group * n_tok + base, 8)
            pltpu.sync_copy(idx_hbm.at[pl.ds(off, per_w)], idx_v[k])
            pltpu.sync_copy(g_hbm.at[pl.ds(off, per_w)], g_v[k])

        def gathers(chunk, slot):
            rows = pl.ds(pl.multiple_of(chunk * w, 8), w)
            return [pltpu.make_async_copy(table_hbm.at[idx_v[k].at[rows]], rb[slot][k], sems[slot])
                    for k in range(TOP_K)]

        def write(chunk, slot):
            return pltpu.make_async_copy(ob[slot], out_hbm.at[pl.ds(pl.multiple_of(base + chunk * w, 8), w)],
                                         wsems[slot])

        def combine(chunk, slot):
            @pl.loop(0, w)
            def _(i):
                token = jnp.full((lanes,), chunk * w + i, jnp.int32)
                gs = [plsc.load_gather(g_v[k], [token]) for k in range(TOP_K)]
                for j in range(d // lanes):
                    cols = pl.ds(j * lanes, lanes)
                    acc = gs[0] * rb[slot][0][i, cols]
                    for k in range(1, TOP_K):
                        acc = acc + gs[k] * rb[slot][k][i, cols]
                    ob[slot][i, cols] = acc
            write(chunk, slot).start()

        for c in gathers(0, 0):
            c.start()

        @pl.loop(0, n_chunk // 2)
        def _(i):
            for c in gathers(2 * i + 1, 1):
                c.start()
            for c in gathers(2 * i, 0):
                c.wait()

            @pl.when(i > 0)
            def _():
                write(2 * i - 2, 0).wait()
            combine(2 * i, 0)

            @pl.when(2 * i + 2 < n_chunk)
            def _():
                for c in gathers(2 * i + 2, 0):
                    c.start()
            for c in gathers(2 * i + 1, 1):
                c.wait()

            @pl.when(i > 0)
            def _():
                write(2 * i - 1, 1).wait()
            combine(2 * i + 1, 1)

        write(n_chunk - 2, 0).wait()
        write(n_chunk - 1, 1).wait()

    scratch = ([pltpu.VMEM((per_w,), jnp.int32)] * TOP_K + [pltpu.VMEM((per_w,), F32)] * TOP_K
               + [pltpu.VMEM((w, d), F32)] * (2 * TOP_K + 2) + [pltpu.SemaphoreType.DMA] * 4)
    return pl.kernel(body, mesh=mesh, out_type=jax.ShapeDtypeStruct((n_tok, d), F32), scratch_types=scratch,
                     compiler_params=pltpu.CompilerParams(needs_layout_passes=False),
                     name="moe_sc_combine")(table, idx, gates)


def _combine_kernel(h_ref, p_ref, y_ref, gffn_ref, wple_ref, wpg_ref, gple_ref, o_ref):
    h2 = h_ref[...] + _rms(y_ref[...], gffn_ref[...])
    ple = (jnp.dot(p_ref[...].astype(BF16), wple_ref[...], preferred_element_type=F32)
           * jax.nn.sigmoid(jnp.dot(h2.astype(BF16), wpg_ref[...], preferred_element_type=F32)))
    o_ref[...] = h2 + _rms(ple, gple_ref[...])


def _combine_kernel_aliased(h_ref, p_ref, y_ref, gffn_ref, wple_ref, wpg_ref, gple_ref, prev_ref, o_ref):
    del prev_ref
    _combine_kernel(h_ref, p_ref, y_ref, gffn_ref, wple_ref, wpg_ref, gple_ref, o_ref)


def _combine(group, n_group, out_prev, h, p, y, g_ffn_post, w_ple_bf16, w_ple_gate_bf16, g_ple_post):
    t, d = h.shape
    tt = 256
    n_tg = t // n_group // tt
    const = lambda shape: pl.BlockSpec(shape, lambda i: (0, 0))
    tok = lambda w: pl.BlockSpec((tt, w), lambda i: (group * n_tg + i, 0))
    in_specs = [tok(d), tok(p.shape[1]), pl.BlockSpec((tt, d), lambda i: (i, 0)),
                const((1, d)), const(w_ple_bf16.shape), const(w_ple_gate_bf16.shape), const((1, d))]
    args = [h, p, y, g_ffn_post[None, :], w_ple_bf16, w_ple_gate_bf16, g_ple_post[None, :]]
    body, aliases = _combine_kernel, {}
    if out_prev is not None:
        in_specs.append(pl.BlockSpec(memory_space=pl.ANY))
        args.append(out_prev)
        body, aliases = _combine_kernel_aliased, {len(args) - 1: 0}
    return pl.pallas_call(
        body,
        grid=(n_tg,),
        in_specs=in_specs,
        out_specs=tok(d),
        out_shape=jax.ShapeDtypeStruct((t, d), F32),
        input_output_aliases=aliases,
        compiler_params=pltpu.CompilerParams(vmem_limit_bytes=VMEM_LIMIT),
        name="moe_combine_ple",
    )(*args)


def _layer(h, p, positions, g_mix_pre, w_in, w_dw, b_dw, g_conv_ln, b_conv_ln, w_out, g_mix_post, g_ffn_pre,
           w_router, b_router, w_gate_up, b_gate_up, w_down, b_down, g_ffn_post, w_ple, w_ple_gate, g_ple_post):
    b, s, d = h.shape
    t = b * s
    cos_t, sin_t = _rope_tables(positions)
    q, k, v, glu = _input_projection(h, cos_t, sin_t, g_mix_pre, w_in)
    attn = _dilated_attention(q, k, v)
    conv = _conformer_conv(glu, w_dw, b_dw, g_conv_ln, b_conv_ln)
    h1, u_ffn, idx, gates = _output_projection(attn, conv, h, w_out, g_mix_post, g_ffn_pre, w_router, b_router)
    cap = t * TOP_K + N_EXPERTS * EXPERT_ROWS
    n_blk = cap // EXPERT_ROWS
    dest, pend, blk = _routing(idx, n_blk)
    blk_e = blk[0, :n_blk]
    nused = pend[N_EXPERTS - 1] // EXPERT_ROWS
    dest_flat = dest.reshape(TOP_K * t)
    tok_of_row = _sc_token_of_row(dest_flat, cap, t)
    blk_c = n_blk // MOE_CHUNKS
    yb = done = None
    xb_next = _sc_gather_rows(u_ffn, tok_of_row, 0, MOE_CHUNKS)
    for c in range(MOE_CHUNKS):
        xb_c = xb_next
        if c + 1 < MOE_CHUNKS:
            xb_next = _sc_gather_rows(u_ffn, tok_of_row, c + 1, MOE_CHUNKS, after=done)
        nused_c = jnp.clip(nused - c * blk_c, 0, blk_c)
        yb, done = _experts(xb_c, c, MOE_CHUNKS, yb, blk_e[c * blk_c:(c + 1) * blk_c], nused_c,
                            w_gate_up, b_gate_up, w_down, b_down)
    w_ple_bf16, w_ple_gate_bf16 = w_ple.astype(BF16), w_ple_gate.astype(BF16)
    h1_flat, p_flat = h1.reshape(t, d), p.reshape(t, -1)
    gate_flat = gates[:TOP_K].reshape(TOP_K * t)
    out = None
    for g in range(MOE_GROUPS):
        y_g = _sc_weighted_rows(yb, dest_flat, gate_flat, g, MOE_GROUPS)
        out = _combine(g, MOE_GROUPS, out, h1_flat, p_flat, y_g, g_ffn_post, w_ple_bf16, w_ple_gate_bf16,
                       g_ple_post)
    return out.reshape(b, s, d)


def kernel(x, p, positions, g_mix_pre, w_in, w_dw, b_dw, g_conv_ln, b_conv_ln, w_out, g_mix_post, g_ffn_pre,
           w_router, b_router, w_gate_up, b_gate_up, w_down, b_down, g_ffn_post, w_ple, w_ple_gate, g_ple_post):
    h = x
    for i in range(p.shape[0]):
        h = _layer(h, p[i], positions, g_mix_pre[i], w_in[i], w_dw[i], b_dw[i], g_conv_ln[i], b_conv_ln[i],
                   w_out[i], g_mix_post[i], g_ffn_pre[i], w_router[i], b_router[i], w_gate_up[i], b_gate_up[i],
                   w_down[i], b_down[i], g_ffn_post[i], w_ple[i], w_ple_gate[i], g_ple_post[i])
    return h
```

```python
import functools

import numpy as np
import jax
import jax.numpy as jnp
from jax import lax
from jax.experimental import pallas as pl
from jax.experimental.pallas import tpu as pltpu

F32 = jnp.float32
BF16 = jnp.bfloat16

HEAD_DIM = 64
N_HEADS = 12
ATTN_WIDTH = N_HEADS * HEAD_DIM
CONV_CHANNELS = 256
CONV_WIDTH = 31
ROPE_DIM = HEAD_DIM // 4
ROPE_THETA = 500000.0
N_EXPERTS = 32
TOP_K = 4
SWIGLU_LIMIT = 7.0
SWIGLU_ALPHA = 1.702
NORM_EPS = 1e-6
WINDOW = 128
N_PLANES = 16
SPAN = N_PLANES * WINDOW
LANES = 128
SUBLANES = 8
NEG = -1e30
EXPERT_ROWS = 256
VMEM_LIMIT = 56 * 1024 * 1024


def _residue_of_plane(p):
    return 4 * (p % 4) + p // 4


def _rms(xv, g):
    var = jnp.mean(xv * xv, axis=-1, keepdims=True)
    return xv * lax.rsqrt(var + NORM_EPS) * g


def _rope_kernel(pos_ref, invf_ref, expand_ref, one_ref, sgn_ref, c_ref, s_ref):
    tn = (((0,), (0,)), ((), ()))

    def spread(t):
        t1 = t.astype(BF16)
        r1 = t - t1.astype(F32)
        t2 = r1.astype(BF16)
        t3 = (r1 - t2.astype(F32)).astype(BF16)
        return sum(lax.dot_general(piece, expand_ref[...], tn, preferred_element_type=F32) for piece in (t1, t2, t3))

    for p in range(N_PLANES):
        ang = invf_ref[...] * pos_ref[0, p:p + 1, :].astype(F32)
        c_ref[0, p] = spread(jnp.cos(ang)) + one_ref[...]
        s_ref[0, p] = spread(jnp.sin(ang)) * sgn_ref[...]


def _rope_tables(positions):
    b, s = positions.shape
    sm = s // N_PLANES
    mt = min(sm, 128)
    plane_res = np.array([_residue_of_plane(p) for p in range(N_PLANES)])
    pos_planes = positions.reshape(b, sm, N_PLANES).transpose(0, 2, 1)[:, plane_res]
    half = ROPE_DIM // 2
    lane = np.arange(LANES) % HEAD_DIM
    inv_freq = ROPE_THETA ** (-jnp.arange(0, ROPE_DIM, 2, dtype=F32) / ROPE_DIM)
    rotary = lane < ROPE_DIM
    expand = jnp.asarray((np.arange(half)[:, None] == lane[None, :] % half) & rotary[None, :], BF16)
    one = jnp.asarray(~rotary, F32)[None, :]
    sgn = jnp.asarray(np.where(lane < half, -1.0, 1.0), F32)[None, :]
    row = pl.BlockSpec((1, LANES), lambda i, j: (0, 0))
    out = pl.BlockSpec((1, N_PLANES, mt, LANES), lambda i, j: (i, 0, j, 0))
    return pl.pallas_call(
        _rope_kernel,
        grid=(b, sm // mt),
        in_specs=[pl.BlockSpec((1, N_PLANES, mt), lambda i, j: (i, 0, j)),
                  pl.BlockSpec((half, 1), lambda i, j: (0, 0)), pl.BlockSpec((half, LANES), lambda i, j: (0, 0)),
                  row, row],
        out_specs=[out, out],
        out_shape=[jax.ShapeDtypeStruct((b, N_PLANES, sm, LANES), F32)] * 2,
        name="rope_tables",
    )(pos_planes, inv_freq[:, None], expand, one, sgn)


PERM_TOKENS = 256
PERM_ROWS = PERM_TOKENS // N_PLANES


def _plane_permutation():
    perm = np.zeros((PERM_TOKENS, PERM_TOKENS), np.float32)
    for p in range(N_PLANES):
        for ml in range(PERM_ROWS):
            perm[PERM_ROWS * p + ml, N_PLANES * ml + _residue_of_plane(p)] = 1.0
    return perm


def _inproj_kernel(x_ref, c_ref, s_ref, g_ref, perm_ref, wqkv_ref, wc_ref, q_ref, k_ref, v_ref, glu_ref):
    g = g_ref[...]
    tt = x_ref.shape[1]
    lane = lax.broadcasted_iota(jnp.int32, (1, LANES), 1) % HEAD_DIM
    first_half = lane < ROPE_DIM // 2

    def rotary(t, cos, sin):
        outs = []
        for j in range(ATTN_WIDTH // LANES):
            tj = t[:, j * LANES:(j + 1) * LANES]
            partner = jnp.where(first_half, pltpu.roll(tj, LANES - ROPE_DIM // 2, 1),
                                pltpu.roll(tj, ROPE_DIM // 2, 1))
            outs.append(tj * cos + partner * sin)
        return jnp.concatenate(outs, axis=1)

    un = _rms(x_ref[0], g).astype(BF16)
    pc = jnp.dot(un, wc_ref[...], preferred_element_type=F32)
    glu_ref[0] = pc[:, :CONV_CHANNELS] * jax.nn.sigmoid(pc[:, CONV_CHANNELS:])

    for sub in range(tt // PERM_TOKENS):
        rows = slice(sub * PERM_ROWS, (sub + 1) * PERM_ROWS)
        u = jnp.dot(perm_ref[...], un[sub * PERM_TOKENS:(sub + 1) * PERM_TOKENS],
                    preferred_element_type=F32).astype(BF16)
        cos = jnp.concatenate([c_ref[0, p, rows, :] for p in range(N_PLANES)], axis=0)
        sin = jnp.concatenate([s_ref[0, p, rows, :] for p in range(N_PLANES)], axis=0)
        proj = jnp.dot(u, wqkv_ref[...], preferred_element_type=F32)
        q = (rotary(proj[:, :ATTN_WIDTH], cos, sin) * (HEAD_DIM ** -0.5)).astype(BF16)
        k = rotary(proj[:, ATTN_WIDTH:2 * ATTN_WIDTH], cos, sin).astype(BF16)
        v = proj[:, 2 * ATTN_WIDTH:].astype(BF16)
        for p in range(N_PLANES):
            chunk = slice(p * PERM_ROWS, (p + 1) * PERM_ROWS)
            q_ref[0, p, rows, :] = q[chunk]
            k_ref[0, p, rows, :] = k[chunk]
            v_ref[0, p, rows, :] = v[chunk]


def _input_projection(x, cos_t, sin_t, g_mix_pre, w_in):
    b, s, d = x.shape
    tt = 1024
    mc = tt // N_PLANES
    wqkv = w_in[:, :3 * ATTN_WIDTH].astype(BF16)
    wc = w_in[:, 3 * ATTN_WIDTH:].astype(BF16)
    perm = jnp.asarray(_plane_permutation(), BF16)
    plane = lambda w: pl.BlockSpec((1, N_PLANES, mc, w), lambda i, j: (i, 0, j, 0))
    plane_shape = jax.ShapeDtypeStruct((b, N_PLANES, s // N_PLANES, ATTN_WIDTH), BF16)
    tok = lambda w: pl.BlockSpec((1, tt, w), lambda i, j: (i, j, 0))
    const = lambda shape: pl.BlockSpec(shape, lambda i, j: (0, 0))
    return pl.pallas_call(
        _inproj_kernel,
        grid=(b, s // tt),
        in_specs=[tok(d), plane(LANES), plane(LANES), const((1, d)), const(perm.shape),
                  const(wqkv.shape), const(wc.shape)],
        out_specs=[plane(ATTN_WIDTH), plane(ATTN_WIDTH), plane(ATTN_WIDTH), tok(CONV_CHANNELS)],
        out_shape=[plane_shape, plane_shape, plane_shape,
                   jax.ShapeDtypeStruct((b, s, CONV_CHANNELS), F32)],
        compiler_params=pltpu.CompilerParams(vmem_limit_bytes=VMEM_LIMIT),
        name="input_projection",
    )(x, cos_t, sin_t, g_mix_pre[None, :], perm, wqkv, wc)


def _attention_biases():
    band = lambda j: np.where((j >= 0) & (j <= WINDOW), 0.0, NEG).astype(np.float32)
    cols = lambda m: np.where(m, NEG, 0.0).astype(np.float32)
    twice = lambda a: np.concatenate([a, a], axis=0)
    mq = np.arange(WINDOW)[:, None]
    kj = np.arange(2 * WINDOW)[None, :]
    j16 = mq + WINDOW - kj
    prev16 = kj < WINDOW
    row = np.arange(128)[:, None]
    col = np.arange(256)[None, :]
    j4 = 4 * (row % 32 - (col % 64 - 32)) + row // 32 - col // 64
    prev4 = col % 64 < 32
    row = np.arange(256)[:, None]
    col = np.arange(512)[None, :]
    res = np.vectorize(_residue_of_plane)
    j1 = 16 * (row % 16 - (col % 32 - 16)) + res(row // 16) - res(col // 32)
    prev1 = col % 32 < 16
    b1 = np.stack([twice(band(j1[:128])), twice(band(j1[128:]))])
    return [jnp.asarray(a) for a in (twice(band(j16)), twice(band(j4)), b1, cols(prev16), cols(prev4), cols(prev1))]


UNROLL = 16


def _attn_kernel(q_ref, kc_ref, kp_ref, vc_ref, vp_ref, b16_ref, b4_ref, b1_ref, p16_ref, p4_ref, p1_ref,
                 o_ref, m_scr, l_scr, a_scr, bias16_scr):
    no_prev = (pl.program_id(1) == 0).astype(F32)
    head0 = lax.broadcasted_iota(jnp.int32, (1, LANES), 1) < HEAD_DIM
    bias16_scr[...] = b16_ref[...] + no_prev * p16_ref[...]

    def tile(qt, kt, vt, bias):
        n = qt.shape[0]
        zero = jnp.zeros_like(qt)
        q2 = jnp.concatenate([jnp.where(head0, qt, zero), jnp.where(head0, zero, qt)], axis=0)
        s = lax.dot_general(q2, kt, (((1,), (1,)), ((), ())), preferred_element_type=F32) + bias
        m = jnp.max(s, axis=-1, keepdims=True)
        e = jnp.exp(s - m).astype(BF16)
        va = jnp.concatenate([vt, jnp.ones_like(vt)], axis=1)
        o = jnp.dot(e, va, preferred_element_type=F32)
        pick = lambda top, bot: jnp.where(head0, top, bot)
        mm = pick(jnp.broadcast_to(m[:n], (n, LANES)), jnp.broadcast_to(m[n:], (n, LANES)))
        return mm, pick(o[:n, LANES:], o[n:, LANES:]), pick(o[:n, :LANES], o[n:, :LANES])

    def put(branch, start, size, stats, off):
        for scr, val in zip((m_scr, l_scr, a_scr), stats):
            scr[branch, pl.ds(start, size), :] = val[off:off + size]

    def body16(i, carry):
        for p in [UNROLL * i + u for u in range(UNROLL)]:
            kt = jnp.concatenate([kp_ref[0, p], kc_ref[0, p]], axis=0)
            vt = jnp.concatenate([vp_ref[0, p], vc_ref[0, p]], axis=0)
            put(0, pl.multiple_of(p * WINDOW, WINDOW), WINDOW, tile(q_ref[0, p], kt, vt, bias16_scr[...]), 0)
        return carry
    lax.fori_loop(0, N_PLANES // UNROLL, body16, 0)

    def body4(g, carry):
        for c, i in [(4 * g + cc, ii) for cc in range(4) for ii in range(4)]:
            qt = jnp.concatenate([q_ref[0, 4 * c + a, 32 * i:32 * i + 32, :] for a in range(4)], axis=0)
            if i == 0:
                ks = [x for a in range(4) for x in (kp_ref[0, 4 * c + a, 96:128, :], kc_ref[0, 4 * c + a, 0:32, :])]
                vs = [x for a in range(4) for x in (vp_ref[0, 4 * c + a, 96:128, :], vc_ref[0, 4 * c + a, 0:32, :])]
                bias = b4_ref[...] + no_prev * p4_ref[...]
            else:
                ks = [kc_ref[0, 4 * c + a, 32 * i - 32:32 * i + 32, :] for a in range(4)]
                vs = [vc_ref[0, 4 * c + a, 32 * i - 32:32 * i + 32, :] for a in range(4)]
                bias = b4_ref[...]
            stats = tile(qt, jnp.concatenate(ks, axis=0), jnp.concatenate(vs, axis=0), bias)
            for a in range(4):
                put(1, pl.multiple_of((4 * c + a) * WINDOW + 32 * i, 32), 32, stats, 32 * a)
        return carry
    lax.fori_loop(0, 1, body4, 0)

    def tile1(i, first):
        rq = pl.ds(pl.multiple_of(16 * i, 16), 16)
        if first:
            ks = [x for p in range(N_PLANES) for x in (kp_ref[0, p, 112:128, :], kc_ref[0, p, 0:16, :])]
            vs = [x for p in range(N_PLANES) for x in (vp_ref[0, p, 112:128, :], vc_ref[0, p, 0:16, :])]
        else:
            rk = pl.ds(pl.multiple_of(16 * i - 16, 16), 32)
            ks = [kc_ref[0, p, rk, :] for p in range(N_PLANES)]
            vs = [vc_ref[0, p, rk, :] for p in range(N_PLANES)]
        kt = jnp.concatenate(ks, axis=0)
        vt = jnp.concatenate(vs, axis=0)
        for half in range(2):
            planes = range(8 * half, 8 * half + 8)
            qt = jnp.concatenate([q_ref[0, p, rq, :] for p in planes], axis=0)
            bias = b1_ref[half] + no_prev * p1_ref[...] if first else b1_ref[half]
            stats = tile(qt, kt, vt, bias)
            for p in planes:
                put(2, pl.multiple_of(p * WINDOW + 16 * i, 16), 16, stats, 16 * (p - 8 * half))

    tile1(0, True)
    for i in range(1, 4):
        tile1(i, False)

    def body1(g, carry):
        for u in range(4):
            tile1(4 * g + u, False)
        return carry
    lax.fori_loop(1, WINDOW // 64, body1, 0)

    def combine(i, carry):
        for p in (2 * i, 2 * i + 1):
            rows = pl.ds(pl.multiple_of(p * WINDOW, WINDOW), WINDOW)
            ms = [m_scr[b, rows, :] for b in range(3)]
            mx = jnp.maximum(jnp.maximum(ms[0], ms[1]), ms[2])
            ws = [jnp.exp(m - mx) for m in ms]
            den = ws[0] * l_scr[0, rows, :] + ws[1] * l_scr[1, rows, :] + ws[2] * l_scr[2, rows, :]
            num = ws[0] * a_scr[0, rows, :] + ws[1] * a_scr[1, rows, :] + ws[2] * a_scr[2, rows, :]
            o_ref[0, p] = (num / den).astype(BF16)
        return carry
    lax.fori_loop(0, N_PLANES // 2, combine, 0)


def _dilated_attention(q, k, v):
    b, _, sm, _ = q.shape
    n_span = sm // WINDOW
    cur = pl.BlockSpec((1, N_PLANES, WINDOW, LANES), lambda i, j, h: (i, 0, j, h))
    prev = pl.BlockSpec((1, N_PLANES, WINDOW, LANES), lambda i, j, h: (i, 0, jnp.maximum(j - 1, 0), h))
    biases = _attention_biases()
    bias_specs = [pl.BlockSpec(a.shape, lambda i, j, h, nd=a.ndim: (0,) * nd) for a in biases]
    stats = pltpu.VMEM((3, SPAN, LANES), F32)
    return pl.pallas_call(
        _attn_kernel,
        grid=(b, n_span, ATTN_WIDTH // LANES),
        in_specs=[cur, cur, prev, cur, prev] + bias_specs,
        out_specs=cur,
        out_shape=jax.ShapeDtypeStruct(q.shape, BF16),
        scratch_shapes=[stats, stats, stats, pltpu.VMEM((2 * WINDOW, 2 * WINDOW), F32)],
        compiler_params=pltpu.CompilerParams(vmem_limit_bytes=VMEM_LIMIT),
        name="dilated_attention",
    )(q, k, k, v, v, *biases)


CONV_HALO = 32


def _conv_kernel(cur_ref, prev_ref, w_ref, b_ref, g_ref, bl_ref, o_ref, scr, *, chunk):
    tt = cur_ref.shape[1]
    has_prev = pl.program_id(1) > 0
    scr[0, 0:CONV_HALO, :] = jnp.where(has_prev, prev_ref[0], 0.0)
    scr[0, CONV_HALO:CONV_HALO + tt, :] = cur_ref[0]
    aligned_rows = tt + CONV_HALO - SUBLANES
    for s in range(1, SUBLANES):
        scr[s, 0:aligned_rows, :] = scr[0, s:s + aligned_rows, :]
    lead = CONV_HALO - (CONV_WIDTH - 1)
    for c0 in range(0, tt, chunk):
        acc = jnp.zeros((chunk, CONV_CHANNELS), F32)
        for j in range(CONV_WIDTH):
            s, a = (lead + j) % SUBLANES, (lead + j) // SUBLANES * SUBLANES
            acc = acc + w_ref[j:j + 1, :] * scr[s, c0 + a:c0 + a + chunk, :]
        y = acc + b_ref[...]
        mu = jnp.mean(y, axis=-1, keepdims=True)
        var = jnp.mean(jnp.square(y - mu), axis=-1, keepdims=True)
        yn = (y - mu) * lax.rsqrt(var + NORM_EPS) * g_ref[...] + bl_ref[...]
        o_ref[0, c0:c0 + chunk, :] = (yn * jax.nn.sigmoid(yn)).astype(BF16)


def _conformer_conv(glu, w_dw, b_dw, g_ln, b_ln):
    b, s, c = glu.shape
    tt = 512
    row = pl.BlockSpec((1, c), lambda i, j: (0, 0))
    return pl.pallas_call(
        functools.partial(_conv_kernel, chunk=128),
        grid=(b, s // tt),
        in_specs=[pl.BlockSpec((1, tt, c), lambda i, j: (i, j, 0)),
                  pl.BlockSpec((1, CONV_HALO, c), lambda i, j: (i, jnp.maximum(j * (tt // CONV_HALO) - 1, 0), 0)),
                  pl.BlockSpec((CONV_WIDTH, c), lambda i, j: (0, 0)), row, row, row],
        out_specs=pl.BlockSpec((1, tt, c), lambda i, j: (i, j, 0)),
        out_shape=jax.ShapeDtypeStruct((b, s, c), BF16),
        scratch_shapes=[pltpu.VMEM((SUBLANES, CONV_HALO + tt, c), F32)],
        name="conformer_conv",
    )(glu, glu, w_dw[:, 0, :], b_dw[None, :], g_ln[None, :], b_ln[None, :])


def _outproj_kernel(attn_ref, conv_ref, x_ref, permt_ref, woa_ref, woc_ref, gpost_ref, gffn_ref, wr_ref, br_ref,
                    h_ref, u_ref, idx_ref, gate_ref):
    tt = x_ref.shape[1]
    nat = []
    for sub in range(tt // PERM_TOKENS):
        rows = slice(sub * PERM_ROWS, (sub + 1) * PERM_ROWS)
        a = jnp.concatenate([attn_ref[0, p, rows, :] for p in range(N_PLANES)], axis=0)
        nat.append(jnp.dot(permt_ref[...], a, preferred_element_type=F32).astype(BF16))
    mix = (jnp.dot(jnp.concatenate(nat, axis=0), woa_ref[...], preferred_element_type=F32)
           + jnp.dot(conv_ref[0], woc_ref[...], preferred_element_type=F32))
    h = x_ref[0] + _rms(mix, gpost_ref[...])
    h_ref[0] = h
    u = _rms(h, gffn_ref[...])
    half = u.shape[1] // 2
    ub = u.astype(BF16).astype(F32)
    u_ref[...] = ((lax.bitcast_convert_type(ub[:, :half], jnp.uint32) >> 16)
                  | (lax.bitcast_convert_type(ub[:, half:], jnp.uint32) & jnp.uint32(0xFFFF0000)))
    u_hi = u.astype(BF16)
    u_lo = (u - u_hi.astype(F32)).astype(BF16)
    nt = (((1,), (1,)), ((), ()))
    by_hi = lax.dot_general(wr_ref[...], u_hi, nt, preferred_element_type=F32)
    by_lo = lax.dot_general(wr_ref[:N_EXPERTS], u_lo, nt, preferred_element_type=F32)
    logits = by_hi[:N_EXPERTS] + (by_hi[N_EXPERTS:] + by_lo) + br_ref[...]
    rows = lax.broadcasted_iota(jnp.int32, logits.shape, 0)
    vals = logits
    tops, idxs = [], []
    for _ in range(TOP_K):
        mx = jnp.max(vals, axis=0, keepdims=True)
        ix = jnp.min(jnp.where(vals == mx, rows, N_EXPERTS), axis=0, keepdims=True)
        tops.append(mx)
        idxs.append(ix)
        vals = jnp.where(rows == ix, -jnp.inf, vals)
    ex = [jnp.exp(t - tops[0]) for t in tops]
    den = ex[0] + ex[1] + ex[2] + ex[3]
    idx_ref[...] = jnp.concatenate(idxs, axis=0)
    gate_ref[...] = jnp.concatenate([e / den for e in ex] + [jnp.zeros((8 - TOP_K, tt), F32)], axis=0)


def _output_projection(attn, conv, x, w_out, g_mix_post, g_ffn_pre, w_router, b_router):
    b, s, d = x.shape
    tt = 512
    mc = tt // N_PLANES
    n_t = s // tt
    woa = w_out[:ATTN_WIDTH].astype(BF16)
    woc = w_out[ATTN_WIDTH:].astype(BF16)
    wr_hi = w_router.T.astype(BF16)
    wr_split = jnp.concatenate([wr_hi, (w_router.T - wr_hi.astype(F32)).astype(BF16)], axis=0)
    permt =jnp.asarray(_plane_permutation().T, BF16)
    const = lambda shape: pl.BlockSpec(shape, lambda i, j: (0, 0))
    flat = lambda w: pl.BlockSpec((tt, w), lambda i, j: (i * n_t + j, 0))
    lanes = lambda r: pl.BlockSpec((r, tt), lambda i, j: (0, i * n_t + j))
    return pl.pallas_call(
        _outproj_kernel,
        grid=(b, n_t),
        in_specs=[pl.BlockSpec((1, N_PLANES, mc, ATTN_WIDTH), lambda i, j: (i, 0, j, 0)),
                  pl.BlockSpec((1, tt, CONV_CHANNELS), lambda i, j: (i, j, 0)),
                  pl.BlockSpec((1, tt, d), lambda i, j: (i, j, 0)),
                  const(permt.shape), const(woa.shape), const(woc.shape), const((1, d)), const((1, d)),
                  const((2 * N_EXPERTS, d)), const((N_EXPERTS, 1))],
        out_specs=[pl.BlockSpec((1, tt, d), lambda i, j: (i, j, 0)), flat(d // 2), lanes(TOP_K), lanes(8)],
        out_shape=[jax.ShapeDtypeStruct((b, s, d), F32),
                   jax.ShapeDtypeStruct((b * s, d // 2), jnp.uint32),
                   jax.ShapeDtypeStruct((TOP_K, b * s), jnp.int32),
                   jax.ShapeDtypeStruct((8, b * s), F32)],
        compiler_params=pltpu.CompilerParams(vmem_limit_bytes=VMEM_LIMIT),
        name="output_projection_router",
    )(attn, conv, x, permt, woa, woc, g_mix_post[None, :], g_ffn_pre[None, :], wr_split, b_router[:, None])


def _route_kernel(idx_ref, dest_ref, pend_ref, blk_ref, carry, pstart):
    phase = pl.program_id(0)
    step = pl.program_id(1)
    tt = idx_ref.shape[1]
    rows = lax.broadcasted_iota(jnp.int32, (N_EXPERTS, tt), 0)
    hot = [rows == idx_ref[k:k + 1, :] for k in range(TOP_K)]
    memb = sum(h.astype(F32) for h in hot)

    @pl.when((phase == 0) & (step == 0))
    def _():
        carry[...] = jnp.zeros_like(carry)

    @pl.when((phase == 1) & (step == 0))
    def _():
        counts = carry[...]
        padded = jnp.floor((counts + (EXPERT_ROWS - 1)) * (1.0 / EXPERT_ROWS)) * EXPERT_ROWS
        tri = (lax.broadcasted_iota(jnp.int32, (N_EXPERTS, N_EXPERTS), 1)
               <= lax.broadcasted_iota(jnp.int32, (N_EXPERTS, N_EXPERTS), 0)).astype(F32)
        pend = jnp.dot(tri, padded, precision=lax.Precision.HIGHEST, preferred_element_type=F32)
        pstart[...] = pend - padded
        pend_ref[...] = pend.astype(jnp.int32)
        starts = lax.broadcasted_iota(jnp.int32, (N_EXPERTS, blk_ref.shape[1]), 1) * EXPERT_ROWS
        ended = (pend.astype(jnp.int32) <= starts).astype(jnp.int32)
        blk_ref[...] = jnp.minimum(jnp.sum(ended, axis=0, keepdims=True), N_EXPERTS - 1)
        carry[...] = jnp.zeros_like(carry)

    @pl.when(phase == 1)
    def _():
        earlier = (lax.broadcasted_iota(jnp.int32, (tt, tt), 0)
                   < lax.broadcasted_iota(jnp.int32, (tt, tt), 1)).astype(BF16)
        row = jnp.dot(memb.astype(BF16), earlier, preferred_element_type=F32) + (carry[...] + pstart[...])
        dest_ref[...] = jnp.concatenate(
            [jnp.sum(jnp.where(h, row, 0.0), axis=0, keepdims=True) for h in hot], axis=0).astype(jnp.int32)

    carry[...] = carry[...] + jnp.sum(memb, axis=1, keepdims=True)


def _routing(idx, n_blk):
    _, t = idx.shape
    tt = 512
    blk_lanes = -(-n_blk // LANES) * LANES
    return pl.pallas_call(
        _route_kernel,
        grid=(2, t // tt),
        in_specs=[pl.BlockSpec((TOP_K, tt), lambda ph, i: (0, i))],
        out_specs=[pl.BlockSpec((TOP_K, tt), lambda ph, i: (0, i * ph)),
                   pl.BlockSpec((N_EXPERTS, 1), lambda ph, i: (0, 0)),
                   pl.BlockSpec((1, blk_lanes), lambda ph, i: (0, 0))],
        out_shape=[jax.ShapeDtypeStruct((TOP_K, t), jnp.int32), jax.ShapeDtypeStruct((N_EXPERTS, 1), jnp.int32),
                   jax.ShapeDtypeStruct((1, blk_lanes), jnp.int32)],
        scratch_shapes=[pltpu.VMEM((N_EXPERTS, 1), F32), pltpu.VMEM((N_EXPERTS, 1), F32)],
        compiler_params=pltpu.CompilerParams(dimension_semantics=("arbitrary", "arbitrary")),
        name="moe_routing",
    )(idx)


SC_SCAN = 16384
SC_UNROLL = 8


def _sc_token_of_row(dest_flat, cap, n_tok):
    from jax.experimental.pallas import tpu_sc as plsc
    assert n_tok & (n_tok - 1) == 0
    info = plsc.get_sparse_core_info()
    n_core, n_sub, lanes = info.num_cores, info.num_subcores, info.num_lanes
    n_src = dest_flat.shape[0]
    per_w = cap // (n_core * n_sub)
    assert per_w * n_core * n_sub == cap and per_w % lanes == 0 and n_src % SC_SCAN == 0
    mesh = plsc.VectorSubcoreMesh(core_axis_name="c", subcore_axis_name="s")

    def body(dest_hbm, out_hbm, src_v, out_v):
        lo = (lax.axis_index("s") * n_core + lax.axis_index("c")) * per_w
        lane = lax.iota(jnp.int32, lanes)

        @pl.loop(0, per_w // lanes)
        def _(i):
            out_v[pl.ds(pl.multiple_of(i * lanes, lanes), lanes)] = (lo + i * lanes + lane) & (n_tok - 1)

        @pl.loop(0, n_src // SC_SCAN)
        def _(c):
            pltpu.sync_copy(dest_hbm.at[pl.ds(pl.multiple_of(c * SC_SCAN, 8), SC_SCAN)], src_v)

            @pl.loop(0, SC_SCAN // (lanes * SC_UNROLL))
            def _(g):
                for u in range(SC_UNROLL):
                    at = (g * SC_UNROLL + u) * lanes
                    local = src_v[pl.ds(pl.multiple_of(at, lanes), lanes)] - lo
                    mine = lax.bitcast_convert_type(local, jnp.uint32) < jnp.uint32(per_w)
                    tok = (c * SC_SCAN + at + lane) & (n_tok - 1)
                    plsc.store_scatter(out_v, [local], tok, mask=mine)
        pltpu.sync_copy(out_v, out_hbm.at[pl.ds(pl.multiple_of(lo, 8), per_w)])

    return pl.kernel(body, mesh=mesh, out_type=jax.ShapeDtypeStruct((cap,), jnp.int32),
                     scratch_types=[pltpu.VMEM((SC_SCAN,), jnp.int32), pltpu.VMEM((per_w,), jnp.int32)],
                     compiler_params=pltpu.CompilerParams(needs_layout_passes=False),
                     name="moe_sc_token_of_row")(dest_flat)


def _expert_kernel(blk_e_ref, nused_ref, xb_ref, wgu_ref, bgu_ref, wd_ref, bd_ref, *rest):
    yb_ref, done_ref, wgu_b, wd_b = rest[-4:]
    j = pl.program_id(0)
    done_ref[...] = jnp.zeros_like(done_ref)
    nused = nused_ref[0]
    d_ff = wd_ref.shape[1]

    new_expert = (j == 0) | (blk_e_ref[j] != blk_e_ref[jnp.maximum(j - 1, 0)])

    @pl.when((j < nused) & new_expert)
    def _():
        def cast(src, dst):
            def body(i, carry):
                rows = pl.ds(pl.multiple_of(i * LANES, LANES), LANES)
                dst[rows, :] = src[0, rows, :].astype(BF16)
                return carry
            lax.fori_loop(0, dst.shape[0] // LANES, body, 0)
        cast(wgu_ref, wgu_b)
        cast(wd_ref, wd_b)

    @pl.when(j < nused)
    def _():
        word = xb_ref[...]
        half = word.shape[1]
        lo = lax.bitcast_convert_type(word << 16, F32).astype(BF16)
        hi = lax.bitcast_convert_type(word & jnp.uint32(0xFFFF0000), F32).astype(BF16)
        gu = (jnp.dot(lo, wgu_b[:half, :], preferred_element_type=F32)
              + jnp.dot(hi, wgu_b[half:, :], preferred_element_type=F32) + bgu_ref[0])
        gate = jnp.minimum(gu[:, :d_ff], SWIGLU_LIMIT)
        up = jnp.clip(gu[:, d_ff:], -SWIGLU_LIMIT, SWIGLU_LIMIT)
        hid = (up + 1.0) * (gate * jax.nn.sigmoid(gate * SWIGLU_ALPHA))
        yb_ref[...] = jnp.dot(hid.astype(BF16), wd_b[...], preferred_element_type=F32) + bd_ref[0]

    @pl.when(j >= nused)
    def _():
        yb_ref[...] = jnp.zeros_like(yb_ref)


def _experts(xb, chunk, n_chunk, yb_prev, blk_e, nused, w_gate_up, b_gate_up, w_down, b_down):
    rows, d_in = xb.shape
    e, d, ff2 = w_gate_up.shape
    d_ff = w_down.shape[1]
    n_blk = rows // EXPERT_ROWS
    live = lambda j, nu: jnp.maximum(jnp.minimum(j, nu[0] - 1), 0)
    expert = lambda j, be, nu: (be[live(j, nu)], 0, 0)
    in_specs = [pl.BlockSpec((EXPERT_ROWS, d_in), lambda j, be, nu: (live(j, nu), 0)),
                pl.BlockSpec((1, d, ff2), expert), pl.BlockSpec((1, 1, ff2), expert),
                pl.BlockSpec((1, d_ff, d), expert), pl.BlockSpec((1, 1, d), expert)]
    args = [blk_e, nused, xb, w_gate_up, b_gate_up[:, None, :], w_down, b_down[:, None, :]]
    aliases = {}
    if yb_prev is not None:
        in_specs.append(pl.BlockSpec(memory_space=pl.ANY))
        args.append(yb_prev)
        aliases = {len(args) - 1: 0}
    grid_spec = pltpu.PrefetchScalarGridSpec(
        num_scalar_prefetch=2,
        grid=(n_blk,),
        in_specs=in_specs,
        out_specs=[pl.BlockSpec((EXPERT_ROWS, d), lambda j, be, nu: (chunk * n_blk + j, 0)),
                   pl.BlockSpec((SUBLANES, LANES), lambda j, be, nu: (0, 0))],
        scratch_shapes=[pltpu.VMEM((d, ff2), BF16), pltpu.VMEM((d_ff, d), BF16)],
    )
    return pl.pallas_call(
        _expert_kernel,
        grid_spec=grid_spec,
        out_shape=[jax.ShapeDtypeStruct((n_chunk * rows, d), F32), jax.ShapeDtypeStruct((SUBLANES, LANES), F32)],
        input_output_aliases=aliases,
        compiler_params=pltpu.CompilerParams(dimension_semantics=("arbitrary",), vmem_limit_bytes=VMEM_LIMIT),
        name="moe_experts",
    )(*args)


SC_BUFFER_BYTES = 128 * 1024
MOE_CHUNKS = 8
MOE_GROUPS = 4


def _sc_gather_rows(table, idx, part, n_part, after=None):
    from jax.experimental.pallas import tpu_sc as plsc
    n_rows = idx.shape[0] // n_part
    d = table.shape[1]
    info = plsc.get_sparse_core_info()
    n_core, n_sub = info.num_cores, info.num_subcores
    per_w = n_rows // (n_core * n_sub)
    fit = SC_BUFFER_BYTES // (d * table.dtype.itemsize)
    g_rows = max(r for r in (8, 16, 32, 64, 128) if r <= fit and per_w % r == 0)
    n_chunk = per_w // g_rows
    assert per_w * n_core * n_sub == n_rows
    mesh = plsc.VectorSubcoreMesh(core_axis_name="c", subcore_axis_name="s")

    def body(table_hbm, idx_hbm, *rest):
        out_hbm, idx_v, buf0, buf1, sem0, sem1 = rest[-6:]
        base = (lax.axis_index("s") * n_core + lax.axis_index("c")) * per_w
        pltpu.sync_copy(idx_hbm.at[pl.ds(pl.multiple_of(part * n_rows + base, 8), per_w)], idx_v)

        def gather(chunk, buf, sem):
            rows = idx_v.at[pl.ds(pl.multiple_of(chunk * g_rows, 8), g_rows)]
            return pltpu.make_async_copy(table_hbm.at[rows], buf, sem)

        def write(chunk, buf):
            pltpu.sync_copy(buf, out_hbm.at[pl.ds(pl.multiple_of(base + chunk * g_rows, 8), g_rows)])

        gather(0, buf0, sem0).start()

        @pl.loop(0, n_chunk // 2)
        def _(i):
            gather(2 * i + 1, buf1, sem1).start()
            gather(2 * i, buf0, sem0).wait()
            write(2 * i, buf0)

            @pl.when(2 * i + 2 < n_chunk)
            def _():
                gather(2 * i + 2, buf0, sem0).start()
            gather(2 * i + 1, buf1, sem1).wait()
            write(2 * i + 1, buf1)

        if n_chunk % 2:
            gather(n_chunk - 1, buf0, sem0).wait()
            write(n_chunk - 1, buf0)

    return pl.kernel(
        body, mesh=mesh, out_type=jax.ShapeDtypeStruct((n_rows, d), table.dtype),
        scratch_types=[pltpu.VMEM((per_w,), jnp.int32), pltpu.VMEM((g_rows, d), table.dtype),
                       pltpu.VMEM((g_rows, d), table.dtype), pltpu.SemaphoreType.DMA, pltpu.SemaphoreType.DMA],
        name="moe_sc_gather",
    )(table, idx, *([] if after is None else [after]))


def _sc_weighted_rows(table, idx, gates, group, n_group):
    from jax.experimental.pallas import tpu_sc as plsc
    d = table.shape[1]
    info = plsc.get_sparse_core_info()
    n_core, n_sub, lanes = info.num_cores, info.num_subcores, info.num_lanes
    n_all = idx.shape[0] // TOP_K
    n_tok = n_all // n_group
    per_w = n_tok // (n_core * n_sub)
    w = 8
    n_chunk = per_w // w
    assert per_w * n_core * n_sub == n_tok and n_chunk * w == per_w and n_chunk % 2 == 0
    mesh = plsc.VectorSubcoreMesh(core_axis_name="c", subcore_axis_name="s")

    def body(table_hbm, idx_hbm, g_hbm, out_hbm, *scr):
        idx_v, g_v, rb, ob = scr[0:4], scr[4:8], (scr[8:12], scr[12:16]), scr[16:18]
        sems, wsems = scr[18:20], scr[20:22]
        base = (lax.axis_index("s") * n_core + lax.axis_index("c")) * per_w
        for k in range(TOP_K):
            off = pl.multiple_of(k * n_all + group * n_tok + base, 8)
            pltpu.sync_copy(idx_hbm.at[pl.ds(off, per_w)], idx_v[k])
            pltpu.sync_copy(g_hbm.at[pl.ds(off, per_w)], g_v[k])

        def gathers(chunk, slot):
            rows = pl.ds(pl.multiple_of(chunk * w, 8), w)
            return [pltpu.make_async_copy(table_hbm.at[idx_v[k].at[rows]], rb[slot][k], sems[slot])
                    for k in range(TOP_K)]

        def write(chunk, slot):
            return pltpu.make_async_copy(ob[slot], out_hbm.at[pl.ds(pl.multiple_of(base + chunk * w, 8), w)],
                                         wsems[slot])

        def combine(chunk, slot):
            @pl.loop(0, w)
            def _(i):
                token = jnp.full((lanes,), chunk * w + i, jnp.int32)
                gs = [plsc.load_gather(g_v[k], [token]) for k in range(TOP_K)]
                for j in range(d // lanes):
                    cols = pl.ds(j * lanes, lanes)
                    acc = gs[0] * rb[slot][0][i, cols]
                    for k in range(1, TOP_K):
                        acc = acc + gs[k] * rb[slot][k][i, cols]
                    ob[slot][i, cols] = acc
            write(chunk, slot).start()

        for c in gathers(0, 0):
            c.start()

        @pl.loop(0, n_chunk // 2)
        def _(i):
            for c in gathers(2 * i + 1, 1):
                c.start()
            for c in gathers(2 * i, 0):
                c.wait()

            @pl.when(i > 0)
            def _():
                write(2 * i - 2, 0).wait()
            combine(2 * i, 0)

            @pl.when(2 * i + 2 < n_chunk)
            def _():
                for c in gathers(2 * i + 2, 0):
                    c.start()
            for c in gathers(2 * i + 1, 1):
                c.wait()

            @pl.when(i > 0)
            def _():
                write(2 * i - 1, 1).wait()
            combine(2 * i + 1, 1)

        write(n_chunk - 2, 0).wait()
        write(n_chunk - 1, 1).wait()

    scratch = ([pltpu.VMEM((per_w,), jnp.int32)] * TOP_K + [pltpu.VMEM((per_w,), F32)] * TOP_K
               + [pltpu.VMEM((w, d), F32)] * (2 * TOP_K + 2) + [pltpu.SemaphoreType.DMA] * 4)
    return pl.kernel(body, mesh=mesh, out_type=jax.ShapeDtypeStruct((n_tok, d), F32), scratch_types=scratch,
                     compiler_params=pltpu.CompilerParams(needs_layout_passes=False),
                     name="moe_sc_combine")(table, idx, gates)


def _combine_kernel(h_ref, p_ref, y_ref, gffn_ref, wple_ref, wpg_ref, gple_ref, o_ref):
    h2 = h_ref[...] + _rms(y_ref[...], gffn_ref[...])
    ple = (jnp.dot(p_ref[...].astype(BF16), wple_ref[...], preferred_element_type=F32)
           * jax.nn.sigmoid(jnp.dot(h2.astype(BF16), wpg_ref[...], preferred_element_type=F32)))
    o_ref[...] = h2 + _rms(ple, gple_ref[...])


def _combine_kernel_aliased(h_ref, p_ref, y_ref, gffn_ref, wple_ref, wpg_ref, gple_ref, prev_ref, o_ref):
    del prev_ref
    _combine_kernel(h_ref, p_ref, y_ref, gffn_ref, wple_ref, wpg_ref, gple_ref, o_ref)


def _combine(group, n_group, out_prev, h, p, y, g_ffn_post, w_ple_bf16, w_ple_gate_bf16, g_ple_post):
    t, d = h.shape
    tt = 256
    n_tg = t // n_group // tt
    const = lambda shape: pl.BlockSpec(shape, lambda i: (0, 0))
    tok = lambda w: pl.BlockSpec((tt, w), lambda i: (group * n_tg + i, 0))
    in_specs = [tok(d), tok(p.shape[1]), pl.BlockSpec((tt, d), lambda i: (i, 0)),
                const((1, d)), const(w_ple_bf16.shape), const(w_ple_gate_bf16.shape), const((1, d))]
    args = [h, p, y, g_ffn_post[None, :], w_ple_bf16, w_ple_gate_bf16, g_ple_post[None, :]]
    body, aliases = _combine_kernel, {}
    if out_prev is not None:
        in_specs.append(pl.BlockSpec(memory_space=pl.ANY))
        args.append(out_prev)
        body, aliases = _combine_kernel_aliased, {len(args) - 1: 0}
    return pl.pallas_call(
        body,
        grid=(n_tg,),
        in_specs=in_specs,
        out_specs=tok(d),
        out_shape=jax.ShapeDtypeStruct((t, d), F32),
        input_output_aliases=aliases,
        compiler_params=pltpu.CompilerParams(vmem_limit_bytes=VMEM_LIMIT),
        name="moe_combine_ple",
    )(*args)


def _layer(h, p, positions, g_mix_pre, w_in, w_dw, b_dw, g_conv_ln, b_conv_ln, w_out, g_mix_post, g_ffn_pre,
           w_router, b_router, w_gate_up, b_gate_up, w_down, b_down, g_ffn_post, w_ple, w_ple_gate, g_ple_post):
    b, s, d = h.shape
    t = b * s
    cos_t, sin_t = _rope_tables(positions)
    q, k, v, glu = _input_projection(h, cos_t, sin_t, g_mix_pre, w_in)
    attn = _dilated_attention(q, k, v)
    conv = _conformer_conv(glu, w_dw, b_dw, g_conv_ln, b_conv_ln)
    h1, u_ffn, idx, gates = _output_projection(attn, conv, h, w_out, g_mix_post, g_ffn_pre, w_router, b_router)
    cap = t * TOP_K + N_EXPERTS * EXPERT_ROWS
    n_blk = cap // EXPERT_ROWS
    dest, pend, blk = _routing(idx, n_blk)
    blk_e = blk[0, :n_blk]
    nused = pend[N_EXPERTS - 1] // EXPERT_ROWS
    dest_flat = dest.reshape(TOP_K * t)
    tok_of_row = _sc_token_of_row(dest_flat, cap, t)
    blk_c = n_blk // MOE_CHUNKS
    yb = done = None
    xb_next = _sc_gather_rows(u_ffn, tok_of_row, 0, MOE_CHUNKS)
    for c in range(MOE_CHUNKS):
        xb_c = xb_next
        if c + 1 < MOE_CHUNKS:
            xb_next = _sc_gather_rows(u_ffn, tok_of_row, c + 1, MOE_CHUNKS, after=done)
        nused_c = jnp.clip(nused - c * blk_c, 0, blk_c)
        yb, done = _experts(xb_c, c, MOE_CHUNKS, yb, blk_e[c * blk_c:(c + 1) * blk_c], nused_c,
                            w_gate_up, b_gate_up, w_down, b_down)
    w_ple_bf16, w_ple_gate_bf16 = w_ple.astype(BF16), w_ple_gate.astype(BF16)
    h1_flat, p_flat = h1.reshape(t, d), p.reshape(t, -1)
    gate_flat = gates[:TOP_K].reshape(TOP_K * t)
    out = None
    for g in range(MOE_GROUPS):
        y_g = _sc_weighted_rows(yb, dest_flat, gate_flat, g, MOE_GROUPS)
        out = _combine(g, MOE_GROUPS, out, h1_flat, p_flat, y_g, g_ffn_post, w_ple_bf16, w_ple_gate_bf16,
                       g_ple_post)
    return out.reshape(b, s, d)


def kernel(x, p, positions, g_mix_pre, w_in, w_dw, b_dw, g_conv_ln, b_conv_ln, w_out, g_mix_post, g_ffn_pre,
           w_router, b_router, w_gate_up, b_gate_up, w_down, b_down, g_ffn_post, w_ple, w_ple_gate, g_ple_post):
    h = x
    for i in range(p.shape[0]):
        h = _layer(h, p[i], positions, g_mix_pre[i], w_in[i], w_dw[i], b_dw[i], g_conv_ln[i], b_conv_ln[i],
                   w_out[i], g_mix_post[i], g_ffn_pre[i], w_router[i], b_router[i], w_gate_up[i], b_gate_up[i],
                   w_down[i], b_down[i], g_ffn_post[i], w_ple[i], w_ple_gate[i], g_ple_post[i])
    return h
```

```python
import functools

import numpy as np
import jax
import jax.numpy as jnp
from jax import lax
from jax.experimental import pallas as pl
from jax.experimental.pallas import tpu as pltpu

F32 = jnp.float32
BF16 = jnp.bfloat16

HEAD_DIM = 64
N_HEADS = 12
ATTN_WIDTH = N_HEADS * HEAD_DIM
CONV_CHANNELS = 256
CONV_WIDTH = 31
ROPE_DIM = HEAD_DIM // 4
ROPE_THETA = 500000.0
N_EXPERTS = 32
TOP_K = 4
SWIGLU_LIMIT = 7.0
SWIGLU_ALPHA = 1.702
NORM_EPS = 1e-6
WINDOW = 128
N_PLANES = 16
SPAN = N_PLANES * WINDOW
LANES = 128
SUBLANES = 8
NEG = -1e30
EXPERT_ROWS = 512
VMEM_LIMIT = 56 * 1024 * 1024


def _residue_of_plane(p):
    return 4 * (p % 4) + p // 4


def _rms(xv, g):
    var = jnp.mean(xv * xv, axis=-1, keepdims=True)
    return xv * lax.rsqrt(var + NORM_EPS) * g


def _rope_kernel(pos_ref, invf_ref, expand_ref, one_ref, sgn_ref, c_ref, s_ref):
    tn = (((0,), (0,)), ((), ()))

    def spread(t):
        t1 = t.astype(BF16)
        r1 = t - t1.astype(F32)
        t2 = r1.astype(BF16)
        t3 = (r1 - t2.astype(F32)).astype(BF16)
        return sum(lax.dot_general(piece, expand_ref[...], tn, preferred_element_type=F32) for piece in (t1, t2, t3))

    for p in range(N_PLANES):
        ang = invf_ref[...] * pos_ref[0, p:p + 1, :].astype(F32)
        c_ref[0, p] = spread(jnp.cos(ang)) + one_ref[...]
        s_ref[0, p] = spread(jnp.sin(ang)) * sgn_ref[...]


def _rope_tables(positions):
    b, s = positions.shape
    sm = s // N_PLANES
    mt = min(sm, 128)
    plane_res = np.array([_residue_of_plane(p) for p in range(N_PLANES)])
    pos_planes = positions.reshape(b, sm, N_PLANES).transpose(0, 2, 1)[:, plane_res]
    half = ROPE_DIM // 2
    lane = np.arange(LANES) % HEAD_DIM
    inv_freq = ROPE_THETA ** (-jnp.arange(0, ROPE_DIM, 2, dtype=F32) / ROPE_DIM)
    rotary = lane < ROPE_DIM
    expand = jnp.asarray((np.arange(half)[:, None] == lane[None, :] % half) & rotary[None, :], BF16)
    one = jnp.asarray(~rotary, F32)[None, :]
    sgn = jnp.asarray(np.where(lane < half, -1.0, 1.0), F32)[None, :]
    row = pl.BlockSpec((1, LANES), lambda i, j: (0, 0))
    out = pl.BlockSpec((1, N_PLANES, mt, LANES), lambda i, j: (i, 0, j, 0))
    return pl.pallas_call(
        _rope_kernel,
        grid=(b, sm // mt),
        in_specs=[pl.BlockSpec((1, N_PLANES, mt), lambda i, j: (i, 0, j)),
                  pl.BlockSpec((half, 1), lambda i, j: (0, 0)), pl.BlockSpec((half, LANES), lambda i, j: (0, 0)),
                  row, row],
        out_specs=[out, out],
        out_shape=[jax.ShapeDtypeStruct((b, N_PLANES, sm, LANES), F32)] * 2,
        name="rope_tables",
    )(pos_planes, inv_freq[:, None], expand, one, sgn)


PERM_TOKENS = 256
PERM_ROWS = PERM_TOKENS // N_PLANES


def _plane_permutation():
    perm = np.zeros((PERM_TOKENS, PERM_TOKENS), np.float32)
    for p in range(N_PLANES):
        for ml in range(PERM_ROWS):
            perm[PERM_ROWS * p + ml, N_PLANES * ml + _residue_of_plane(p)] = 1.0
    return perm


def _inproj_kernel(x_ref, c_ref, s_ref, g_ref, perm_ref, wqkv_ref, wc_ref, q_ref, k_ref, v_ref, glu_ref):
    g = g_ref[...]
    tt = x_ref.shape[1]
    lane = lax.broadcasted_iota(jnp.int32, (1, LANES), 1) % HEAD_DIM
    first_half = lane < ROPE_DIM // 2

    def rotary(t, cos, sin):
        outs = []
        for j in range(ATTN_WIDTH // LANES):
            tj = t[:, j * LANES:(j + 1) * LANES]
            partner = jnp.where(first_half, pltpu.roll(tj, LANES - ROPE_DIM // 2, 1),
                                pltpu.roll(tj, ROPE_DIM // 2, 1))
            outs.append(tj * cos + partner * sin)
        return jnp.concatenate(outs, axis=1)

    un = _rms(x_ref[0], g).astype(BF16)
    pc = jnp.dot(un, wc_ref[...], preferred_element_type=F32)
    glu_ref[0] = pc[:, :CONV_CHANNELS] * jax.nn.sigmoid(pc[:, CONV_CHANNELS:])

    for sub in range(tt // PERM_TOKENS):
        rows = slice(sub * PERM_ROWS, (sub + 1) * PERM_ROWS)
        u = jnp.dot(perm_ref[...], un[sub * PERM_TOKENS:(sub + 1) * PERM_TOKENS],
                    preferred_element_type=F32).astype(BF16)
        cos = jnp.concatenate([c_ref[0, p, rows, :] for p in range(N_PLANES)], axis=0)
        sin = jnp.concatenate([s_ref[0, p, rows, :] for p in range(N_PLANES)], axis=0)
        proj = jnp.dot(u, wqkv_ref[...], preferred_element_type=F32)
        q = (rotary(proj[:, :ATTN_WIDTH], cos, sin) * (HEAD_DIM ** -0.5)).astype(BF16)
        k = rotary(proj[:, ATTN_WIDTH:2 * ATTN_WIDTH], cos, sin).astype(BF16)
        v = proj[:, 2 * ATTN_WIDTH:].astype(BF16)
        for p in range(N_PLANES):
            chunk = slice(p * PERM_ROWS, (p + 1) * PERM_ROWS)
            q_ref[0, p, rows, :] = q[chunk]
            k_ref[0, p, rows, :] = k[chunk]
            v_ref[0, p, rows, :] = v[chunk]


def _input_projection(x, cos_t, sin_t, g_mix_pre, w_in):
    b, s, d = x.shape
    tt = 1024
    mc = tt // N_PLANES
    wqkv = w_in[:, :3 * ATTN_WIDTH].astype(BF16)
    wc = w_in[:, 3 * ATTN_WIDTH:].astype(BF16)
    perm = jnp.asarray(_plane_permutation(), BF16)
    plane = lambda w: pl.BlockSpec((1, N_PLANES, mc, w), lambda i, j: (i, 0, j, 0))
    plane_shape = jax.ShapeDtypeStruct((b, N_PLANES, s // N_PLANES, ATTN_WIDTH), BF16)
    tok = lambda w: pl.BlockSpec((1, tt, w), lambda i, j: (i, j, 0))
    const = lambda shape: pl.BlockSpec(shape, lambda i, j: (0, 0))
    return pl.pallas_call(
        _inproj_kernel,
        grid=(b, s // tt),
        in_specs=[tok(d), plane(LANES), plane(LANES), const((1, d)), const(perm.shape),
                  const(wqkv.shape), const(wc.shape)],
        out_specs=[plane(ATTN_WIDTH), plane(ATTN_WIDTH), plane(ATTN_WIDTH), tok(CONV_CHANNELS)],
        out_shape=[plane_shape, plane_shape, plane_shape,
                   jax.ShapeDtypeStruct((b, s, CONV_CHANNELS), F32)],
        compiler_params=pltpu.CompilerParams(vmem_limit_bytes=VMEM_LIMIT),
        name="input_projection",
    )(x, cos_t, sin_t, g_mix_pre[None, :], perm, wqkv, wc)


def _attention_biases():
    band = lambda j: np.where((j >= 0) & (j <= WINDOW), 0.0, NEG).astype(np.float32)
    cols = lambda m: np.where(m, NEG, 0.0).astype(np.float32)
    twice = lambda a: np.concatenate([a, a], axis=0)
    mq = np.arange(WINDOW)[:, None]
    kj = np.arange(2 * WINDOW)[None, :]
    j16 = mq + WINDOW - kj
    prev16 = kj < WINDOW
    row = np.arange(128)[:, None]
    col = np.arange(256)[None, :]
    j4 = 4 * (row % 32 - (col % 64 - 32)) + row // 32 - col // 64
    prev4 = col % 64 < 32
    row = np.arange(256)[:, None]
    col = np.arange(512)[None, :]
    res = np.vectorize(_residue_of_plane)
    j1 = 16 * (row % 16 - (col % 32 - 16)) + res(row // 16) - res(col // 32)
    prev1 = col % 32 < 16
    b1 = np.stack([twice(band(j1[:128])), twice(band(j1[128:]))])
    return [jnp.asarray(a) for a in (twice(band(j16)), twice(band(j4)), b1, cols(prev16), cols(prev4), cols(prev1))]


UNROLL = 16


def _attn_kernel(q_ref, kc_ref, kp_ref, vc_ref, vp_ref, b16_ref, b4_ref, b1_ref, p16_ref, p4_ref, p1_ref,
                 o_ref, m_scr, l_scr, a_scr, bias16_scr):
    no_prev = (pl.program_id(1) == 0).astype(F32)
    head0 = lax.broadcasted_iota(jnp.int32, (1, LANES), 1) < HEAD_DIM
    bias16_scr[...] = b16_ref[...] + no_prev * p16_ref[...]

    def tile(qt, kt, vt, bias):
        n = qt.shape[0]
        zero = jnp.zeros_like(qt)
        q2 = jnp.concatenate([jnp.where(head0, qt, zero), jnp.where(head0, zero, qt)], axis=0)
        s = lax.dot_general(q2, kt, (((1,), (1,)), ((), ())), preferred_element_type=F32) + bias
        m = jnp.max(s, axis=-1, keepdims=True)
        e = jnp.exp(s - m).astype(BF16)
        va = jnp.concatenate([vt, jnp.ones_like(vt)], axis=1)
        o = jnp.dot(e, va, preferred_element_type=F32)
        pick = lambda top, bot: jnp.where(head0, top, bot)
        mm = pick(jnp.broadcast_to(m[:n], (n, LANES)), jnp.broadcast_to(m[n:], (n, LANES)))
        return mm, pick(o[:n, LANES:], o[n:, LANES:]), pick(o[:n, :LANES], o[n:, :LANES])

    def put(branch, start, size, stats, off):
        for scr, val in zip((m_scr, l_scr, a_scr), stats):
            scr[branch, pl.ds(start, size), :] = val[off:off + size]

    def body16(i, carry):
        for p in [UNROLL * i + u for u in range(UNROLL)]:
            kt = jnp.concatenate([kp_ref[0, p], kc_ref[0, p]], axis=0)
            vt = jnp.concatenate([vp_ref[0, p], vc_ref[0, p]], axis=0)
            put(0, pl.multiple_of(p * WINDOW, WINDOW), WINDOW, tile(q_ref[0, p], kt, vt, bias16_scr[...]), 0)
        return carry
    lax.fori_loop(0, N_PLANES // UNROLL, body16, 0)

    def body4(g, carry):
        for c, i in [(4 * g + cc, ii) for cc in range(4) for ii in range(4)]:
            qt = jnp.concatenate([q_ref[0, 4 * c + a, 32 * i:32 * i + 32, :] for a in range(4)], axis=0)
            if i == 0:
                ks = [x for a in range(4) for x in (kp_ref[0, 4 * c + a, 96:128, :], kc_ref[0, 4 * c + a, 0:32, :])]
                vs = [x for a in range(4) for x in (vp_ref[0, 4 * c + a, 96:128, :], vc_ref[0, 4 * c + a, 0:32, :])]
                bias = b4_ref[...] + no_prev * p4_ref[...]
            else:
                ks = [kc_ref[0, 4 * c + a, 32 * i - 32:32 * i + 32, :] for a in range(4)]
                vs = [vc_ref[0, 4 * c + a, 32 * i - 32:32 * i + 32, :] for a in range(4)]
                bias = b4_ref[...]
            stats = tile(qt, jnp.concatenate(ks, axis=0), jnp.concatenate(vs, axis=0), bias)
            for a in range(4):
                put(1, pl.multiple_of((4 * c + a) * WINDOW + 32 * i, 32), 32, stats, 32 * a)
        return carry
    lax.fori_loop(0, 1, body4, 0)

    def tile1(i, first):
        rq = pl.ds(pl.multiple_of(16 * i, 16), 16)
        if first:
            ks = [x for p in range(N_PLANES) for x in (kp_ref[0, p, 112:128, :], kc_ref[0, p, 0:16, :])]
            vs = [x for p in range(N_PLANES) for x in (vp_ref[0, p, 112:128, :], vc_ref[0, p, 0:16, :])]
        else:
            rk = pl.ds(pl.multiple_of(16 * i - 16, 16), 32)
            ks = [kc_ref[0, p, rk, :] for p in range(N_PLANES)]
            vs = [vc_ref[0, p, rk, :] for p in range(N_PLANES)]
        kt = jnp.concatenate(ks, axis=0)
        vt = jnp.concatenate(vs, axis=0)
        for half in range(2):
            planes = range(8 * half, 8 * half + 8)
            qt = jnp.concatenate([q_ref[0, p, rq, :] for p in planes], axis=0)
            bias = b1_ref[half] + no_prev * p1_ref[...] if first else b1_ref[half]
            stats = tile(qt, kt, vt, bias)
            for p in planes:
                put(2, pl.multiple_of(p * WINDOW + 16 * i, 16), 16, stats, 16 * (p - 8 * half))

    tile1(0, True)
    for i in range(1, 4):
        tile1(i, False)

    def body1(g, carry):
        for u in range(4):
            tile1(4 * g + u, False)
        return carry
    lax.fori_loop(1, WINDOW // 64, body1, 0)

    def combine(i, carry):
        for p in (2 * i, 2 * i + 1):
            rows = pl.ds(pl.multiple_of(p * WINDOW, WINDOW), WINDOW)
            ms = [m_scr[b, rows, :] for b in range(3)]
            mx = jnp.maximum(jnp.maximum(ms[0], ms[1]), ms[2])
            ws = [jnp.exp(m - mx) for m in ms]
            den = ws[0] * l_scr[0, rows, :] + ws[1] * l_scr[1, rows, :] + ws[2] * l_scr[2, rows, :]
            num = ws[0] * a_scr[0, rows, :] + ws[1] * a_scr[1, rows, :] + ws[2] * a_scr[2, rows, :]
            o_ref[0, p] = (num / den).astype(BF16)
        return carry
    lax.fori_loop(0, N_PLANES // 2, combine, 0)


def _dilated_attention(q, k, v):
    b, _, sm, _ = q.shape
    n_span = sm // WINDOW
    cur = pl.BlockSpec((1, N_PLANES, WINDOW, LANES), lambda i, j, h: (i, 0, j, h))
    prev = pl.BlockSpec((1, N_PLANES, WINDOW, LANES), lambda i, j, h: (i, 0, jnp.maximum(j - 1, 0), h))
    biases = _attention_biases()
    bias_specs = [pl.BlockSpec(a.shape, lambda i, j, h, nd=a.ndim: (0,) * nd) for a in biases]
    stats = pltpu.VMEM((3, SPAN, LANES), F32)
    return pl.pallas_call(
        _attn_kernel,
        grid=(b, n_span, ATTN_WIDTH // LANES),
        in_specs=[cur, cur, prev, cur, prev] + bias_specs,
        out_specs=cur,
        out_shape=jax.ShapeDtypeStruct(q.shape, BF16),
        scratch_shapes=[stats, stats, stats, pltpu.VMEM((2 * WINDOW, 2 * WINDOW), F32)],
        compiler_params=pltpu.CompilerParams(vmem_limit_bytes=VMEM_LIMIT),
        name="dilated_attention",
    )(q, k, k, v, v, *biases)


CONV_HALO = 32


def _conv_kernel(cur_ref, prev_ref, w_ref, b_ref, g_ref, bl_ref, o_ref, scr, *, chunk):
    tt = cur_ref.shape[1]
    has_prev = pl.program_id(1) > 0
    scr[0, 0:CONV_HALO, :] = jnp.where(has_prev, prev_ref[0], 0.0)
    scr[0, CONV_HALO:CONV_HALO + tt, :] = cur_ref[0]
    aligned_rows = tt + CONV_HALO - SUBLANES
    for s in range(1, SUBLANES):
        scr[s, 0:aligned_rows, :] = scr[0, s:s + aligned_rows, :]
    lead = CONV_HALO - (CONV_WIDTH - 1)
    for c0 in range(0, tt, chunk):
        acc = jnp.zeros((chunk, CONV_CHANNELS), F32)
        for j in range(CONV_WIDTH):
            s, a = (lead + j) % SUBLANES, (lead + j) // SUBLANES * SUBLANES
            acc = acc + w_ref[j:j + 1, :] * scr[s, c0 + a:c0 + a + chunk, :]
        y = acc + b_ref[...]
        mu = jnp.mean(y, axis=-1, keepdims=True)
        var = jnp.mean(jnp.square(y - mu), axis=-1, keepdims=True)
        yn = (y - mu) * lax.rsqrt(var + NORM_EPS) * g_ref[...] + bl_ref[...]
        o_ref[0, c0:c0 + chunk, :] = (yn * jax.nn.sigmoid(yn)).astype(BF16)


def _conformer_conv(glu, w_dw, b_dw, g_ln, b_ln):
    b, s, c = glu.shape
    tt = 512
    row = pl.BlockSpec((1, c), lambda i, j: (0, 0))
    return pl.pallas_call(
        functools.partial(_conv_kernel, chunk=128),
        grid=(b, s // tt),
        in_specs=[pl.BlockSpec((1, tt, c), lambda i, j: (i, j, 0)),
                  pl.BlockSpec((1, CONV_HALO, c), lambda i, j: (i, jnp.maximum(j * (tt // CONV_HALO) - 1, 0), 0)),
                  pl.BlockSpec((CONV_WIDTH, c), lambda i, j: (0, 0)), row, row, row],
        out_specs=pl.BlockSpec((1, tt, c), lambda i, j: (i, j, 0)),
        out_shape=jax.ShapeDtypeStruct((b, s, c), BF16),
        scratch_shapes=[pltpu.VMEM((SUBLANES, CONV_HALO + tt, c), F32)],
        name="conformer_conv",
    )(glu, glu, w_dw[:, 0, :], b_dw[None, :], g_ln[None, :], b_ln[None, :])


def _outproj_kernel(attn_ref, conv_ref, x_ref, permt_ref, woa_ref, woc_ref, gpost_ref, gffn_ref, wr_ref, br_ref,
                    h_ref, u_ref, idx_ref, gate_ref):
    tt = x_ref.shape[1]
    nat = []
    for sub in range(tt // PERM_TOKENS):
        rows = slice(sub * PERM_ROWS, (sub + 1) * PERM_ROWS)
        a = jnp.concatenate([attn_ref[0, p, rows, :] for p in range(N_PLANES)], axis=0)
        nat.append(jnp.dot(permt_ref[...], a, preferred_element_type=F32).astype(BF16))
    mix = (jnp.dot(jnp.concatenate(nat, axis=0), woa_ref[...], preferred_element_type=F32)
           + jnp.dot(conv_ref[0], woc_ref[...], preferred_element_type=F32))
    h = x_ref[0] + _rms(mix, gpost_ref[...])
    h_ref[0] = h
    u = _rms(h, gffn_ref[...])
    half = u.shape[1] // 2
    ub = u.astype(BF16).astype(F32)
    u_ref[...] = ((lax.bitcast_convert_type(ub[:, :half], jnp.uint32) >> 16)
                  | (lax.bitcast_convert_type(ub[:, half:], jnp.uint32) & jnp.uint32(0xFFFF0000)))
    u_hi = u.astype(BF16)
    u_lo = (u - u_hi.astype(F32)).astype(BF16)
    nt = (((1,), (1,)), ((), ()))
    by_hi = lax.dot_general(wr_ref[...], u_hi, nt, preferred_element_type=F32)
    by_lo = lax.dot_general(wr_ref[:N_EXPERTS], u_lo, nt, preferred_element_type=F32)
    logits = by_hi[:N_EXPERTS] + (by_hi[N_EXPERTS:] + by_lo) + br_ref[...]
    rows = lax.broadcasted_iota(jnp.int32, logits.shape, 0)
    vals = logits
    tops, idxs = [], []
    for _ in range(TOP_K):
        mx = jnp.max(vals, axis=0, keepdims=True)
        ix = jnp.min(jnp.where(vals == mx, rows, N_EXPERTS), axis=0, keepdims=True)
        tops.append(mx)
        idxs.append(ix)
        vals = jnp.where(rows == ix, -jnp.inf, vals)
    ex = [jnp.exp(t - tops[0]) for t in tops]
    den = ex[0] + ex[1] + ex[2] + ex[3]
    idx_ref[...] = jnp.concatenate(idxs, axis=0)
    gate_ref[...] = jnp.concatenate([e / den for e in ex] + [jnp.zeros((8 - TOP_K, tt), F32)], axis=0)


def _output_projection(attn, conv, x, w_out, g_mix_post, g_ffn_pre, w_router, b_router):
    b, s, d = x.shape
    tt = 512
    mc = tt // N_PLANES
    n_t = s // tt
    woa = w_out[:ATTN_WIDTH].astype(BF16)
    woc = w_out[ATTN_WIDTH:].astype(BF16)
    wr_hi = w_router.T.astype(BF16)
    wr_split = jnp.concatenate([wr_hi, (w_router.T - wr_hi.astype(F32)).astype(BF16)], axis=0)
    permt =jnp.asarray(_plane_permutation().T, BF16)
    const = lambda shape: pl.BlockSpec(shape, lambda i, j: (0, 0))
    flat = lambda w: pl.BlockSpec((tt, w), lambda i, j: (i * n_t + j, 0))
    lanes = lambda r: pl.BlockSpec((r, tt), lambda i, j: (0, i * n_t + j))
    return pl.pallas_call(
        _outproj_kernel,
        grid=(b, n_t),
        in_specs=[pl.BlockSpec((1, N_PLANES, mc, ATTN_WIDTH), lambda i, j: (i, 0, j, 0)),
                  pl.BlockSpec((1, tt, CONV_CHANNELS), lambda i, j: (i, j, 0)),
                  pl.BlockSpec((1, tt, d), lambda i, j: (i, j, 0)),
                  const(permt.shape), const(woa.shape), const(woc.shape), const((1, d)), const((1, d)),
                  const((2 * N_EXPERTS, d)), const((N_EXPERTS, 1))],
        out_specs=[pl.BlockSpec((1, tt, d), lambda i, j: (i, j, 0)), flat(d // 2), lanes(TOP_K), lanes(8)],
        out_shape=[jax.ShapeDtypeStruct((b, s, d), F32),
                   jax.ShapeDtypeStruct((b * s, d // 2), jnp.uint32),
                   jax.ShapeDtypeStruct((TOP_K, b * s), jnp.int32),
                   jax.ShapeDtypeStruct((8, b * s), F32)],
        compiler_params=pltpu.CompilerParams(vmem_limit_bytes=VMEM_LIMIT),
        name="output_projection_router",
    )(attn, conv, x, permt, woa, woc, g_mix_post[None, :], g_ffn_pre[None, :], wr_split, b_router[:, None])


def _route_kernel(idx_ref, dest_ref, pend_ref, blk_ref, carry, pstart):
    phase = pl.program_id(0)
    step = pl.program_id(1)
    tt = idx_ref.shape[1]
    rows = lax.broadcasted_iota(jnp.int32, (N_EXPERTS, tt), 0)
    hot = [rows == idx_ref[k:k + 1, :] for k in range(TOP_K)]
    memb = sum(h.astype(F32) for h in hot)

    @pl.when((phase == 0) & (step == 0))
    def _():
        carry[...] = jnp.zeros_like(carry)

    @pl.when((phase == 1) & (step == 0))
    def _():
        counts = carry[...]
        padded = jnp.floor((counts + (EXPERT_ROWS - 1)) * (1.0 / EXPERT_ROWS)) * EXPERT_ROWS
        tri = (lax.broadcasted_iota(jnp.int32, (N_EXPERTS, N_EXPERTS), 1)
               <= lax.broadcasted_iota(jnp.int32, (N_EXPERTS, N_EXPERTS), 0)).astype(F32)
        pend = jnp.dot(tri, padded, precision=lax.Precision.HIGHEST, preferred_element_type=F32)
        pstart[...] = pend - padded
        pend_ref[...] = pend.astype(jnp.int32)
        starts = lax.broadcasted_iota(jnp.int32, (N_EXPERTS, blk_ref.shape[1]), 1) * EXPERT_ROWS
        ended = (pend.astype(jnp.int32) <= starts).astype(jnp.int32)
        blk_ref[...] = jnp.minimum(jnp.sum(ended, axis=0, keepdims=True), N_EXPERTS - 1)
        carry[...] = jnp.zeros_like(carry)

    @pl.when(phase == 1)
    def _():
        earlier = (lax.broadcasted_iota(jnp.int32, (tt, tt), 0)
                   < lax.broadcasted_iota(jnp.int32, (tt, tt), 1)).astype(BF16)
        row = jnp.dot(memb.astype(BF16), earlier, preferred_element_type=F32) + (carry[...] + pstart[...])
        dest_ref[...] = jnp.concatenate(
            [jnp.sum(jnp.where(h, row, 0.0), axis=0, keepdims=True) for h in hot], axis=0).astype(jnp.int32)

    carry[...] = carry[...] + jnp.sum(memb, axis=1, keepdims=True)


def _routing(idx, n_blk):
    _, t = idx.shape
    tt = 512
    blk_lanes = -(-n_blk // LANES) * LANES
    return pl.pallas_call(
        _route_kernel,
        grid=(2, t // tt),
        in_specs=[pl.BlockSpec((TOP_K, tt), lambda ph, i: (0, i))],
        out_specs=[pl.BlockSpec((TOP_K, tt), lambda ph, i: (0, i * ph)),
                   pl.BlockSpec((N_EXPERTS, 1), lambda ph, i: (0, 0)),
                   pl.BlockSpec((1, blk_lanes), lambda ph, i: (0, 0))],
        out_shape=[jax.ShapeDtypeStruct((TOP_K, t), jnp.int32), jax.ShapeDtypeStruct((N_EXPERTS, 1), jnp.int32),
                   jax.ShapeDtypeStruct((1, blk_lanes), jnp.int32)],
        scratch_shapes=[pltpu.VMEM((N_EXPERTS, 1), F32), pltpu.VMEM((N_EXPERTS, 1), F32)],
        compiler_params=pltpu.CompilerParams(dimension_semantics=("arbitrary", "arbitrary")),
        name="moe_routing",
    )(idx)


SC_SCAN = 16384
SC_UNROLL = 8


def _sc_token_of_row(dest_flat, cap, n_tok):
    from jax.experimental.pallas import tpu_sc as plsc
    assert n_tok & (n_tok - 1) == 0
    info = plsc.get_sparse_core_info()
    n_core, n_sub, lanes = info.num_cores, info.num_subcores, info.num_lanes
    n_src = dest_flat.shape[0]
    per_w = cap // (n_core * n_sub)
    assert per_w * n_core * n_sub == cap and per_w % lanes == 0 and n_src % SC_SCAN == 0
    mesh = plsc.VectorSubcoreMesh(core_axis_name="c", subcore_axis_name="s")

    def body(dest_hbm, out_hbm, src_v, out_v):
        lo = (lax.axis_index("s") * n_core + lax.axis_index("c")) * per_w
        lane = lax.iota(jnp.int32, lanes)

        @pl.loop(0, per_w // lanes)
        def _(i):
            out_v[pl.ds(pl.multiple_of(i * lanes, lanes), lanes)] = (lo + i * lanes + lane) & (n_tok - 1)

        @pl.loop(0, n_src // SC_SCAN)
        def _(c):
            pltpu.sync_copy(dest_hbm.at[pl.ds(pl.multiple_of(c * SC_SCAN, 8), SC_SCAN)], src_v)

            @pl.loop(0, SC_SCAN // (lanes * SC_UNROLL))
            def _(g):
                for u in range(SC_UNROLL):
                    at = (g * SC_UNROLL + u) * lanes
                    local = src_v[pl.ds(pl.multiple_of(at, lanes), lanes)] - lo
                    mine = lax.bitcast_convert_type(local, jnp.uint32) < jnp.uint32(per_w)
                    tok = (c * SC_SCAN + at + lane) & (n_tok - 1)
                    plsc.store_scatter(out_v, [local], tok, mask=mine)
        pltpu.sync_copy(out_v, out_hbm.at[pl.ds(pl.multiple_of(lo, 8), per_w)])

    return pl.kernel(body, mesh=mesh, out_type=jax.ShapeDtypeStruct((cap,), jnp.int32),
                     scratch_types=[pltpu.VMEM((SC_SCAN,), jnp.int32), pltpu.VMEM((per_w,), jnp.int32)],
                     compiler_params=pltpu.CompilerParams(needs_layout_passes=False),
                     name="moe_sc_token_of_row")(dest_flat)


def _expert_kernel(blk_e_ref, nused_ref, xb_ref, wgu_ref, bgu_ref, wd_ref, bd_ref, *rest):
    yb_ref, done_ref, wgu_b, wd_b = rest[-4:]
    j = pl.program_id(0)
    done_ref[...] = jnp.zeros_like(done_ref)
    nused = nused_ref[0]
    d_ff = wd_ref.shape[1]

    new_expert = (j == 0) | (blk_e_ref[j] != blk_e_ref[jnp.maximum(j - 1, 0)])

    @pl.when((j < nused) & new_expert)
    def _():
        def cast(src, dst):
            def body(i, carry):
                rows = pl.ds(pl.multiple_of(i * LANES, LANES), LANES)
                dst[rows, :] = src[0, rows, :].astype(BF16)
                return carry
            lax.fori_loop(0, dst.shape[0] // LANES, body, 0)
        cast(wgu_ref, wgu_b)
        cast(wd_ref, wd_b)

    @pl.when(j < nused)
    def _():
        word = xb_ref[...]
        half = word.shape[1]
        lo = lax.bitcast_convert_type(word << 16, F32).astype(BF16)
        hi = lax.bitcast_convert_type(word & jnp.uint32(0xFFFF0000), F32).astype(BF16)
        gu = (jnp.dot(lo, wgu_b[:half, :], preferred_element_type=F32)
              + jnp.dot(hi, wgu_b[half:, :], preferred_element_type=F32) + bgu_ref[0])
        gate = jnp.minimum(gu[:, :d_ff], SWIGLU_LIMIT)
        up = jnp.clip(gu[:, d_ff:], -SWIGLU_LIMIT, SWIGLU_LIMIT)
        hid = (up + 1.0) * (gate * jax.nn.sigmoid(gate * SWIGLU_ALPHA))
        yb_ref[...] = jnp.dot(hid.astype(BF16), wd_b[...], preferred_element_type=F32) + bd_ref[0]

    @pl.when(j >= nused)
    def _():
        yb_ref[...] = jnp.zeros_like(yb_ref)


def _experts(xb, chunk, n_chunk, yb_prev, blk_e, nused, w_gate_up, b_gate_up, w_down, b_down):
    rows, d_in = xb.shape
    e, d, ff2 = w_gate_up.shape
    d_ff = w_down.shape[1]
    n_blk = rows // EXPERT_ROWS
    live = lambda j, nu: jnp.maximum(jnp.minimum(j, nu[0] - 1), 0)
    expert = lambda j, be, nu: (be[live(j, nu)], 0, 0)
    in_specs = [pl.BlockSpec((EXPERT_ROWS, d_in), lambda j, be, nu: (live(j, nu), 0)),
                pl.BlockSpec((1, d, ff2), expert), pl.BlockSpec((1, 1, ff2), expert),
                pl.BlockSpec((1, d_ff, d), expert), pl.BlockSpec((1, 1, d), expert)]
    args = [blk_e, nused, xb, w_gate_up, b_gate_up[:, None, :], w_down, b_down[:, None, :]]
    aliases = {}
    if yb_prev is not None:
        in_specs.append(pl.BlockSpec(memory_space=pl.ANY))
        args.append(yb_prev)
        aliases = {len(args) - 1: 0}
    grid_spec = pltpu.PrefetchScalarGridSpec(
        num_scalar_prefetch=2,
        grid=(n_blk,),
        in_specs=in_specs,
        out_specs=[pl.BlockSpec((EXPERT_ROWS, d), lambda j, be, nu: (chunk * n_blk + j, 0)),
                   pl.BlockSpec((SUBLANES, LANES), lambda j, be, nu: (0, 0))],
        scratch_shapes=[pltpu.VMEM((d, ff2), BF16), pltpu.VMEM((d_ff, d), BF16)],
    )
    return pl.pallas_call(
        _expert_kernel,
        grid_spec=grid_spec,
        out_shape=[jax.ShapeDtypeStruct((n_chunk * rows, d), F32), jax.ShapeDtypeStruct((SUBLANES, LANES), F32)],
        input_output_aliases=aliases,
        compiler_params=pltpu.CompilerParams(dimension_semantics=("arbitrary",), vmem_limit_bytes=VMEM_LIMIT),
        name="moe_experts",
    )(*args)


SC_BUFFER_BYTES = 128 * 1024
MOE_CHUNKS = 8
MOE_GROUPS = 4


def _sc_gather_rows(table, idx, part, n_part, after=None):
    from jax.experimental.pallas import tpu_sc as plsc
    n_rows = idx.shape[0] // n_part
    d = table.shape[1]
    info = plsc.get_sparse_core_info()
    n_core, n_sub = info.num_cores, info.num_subcores
    per_w = n_rows // (n_core * n_sub)
    fit = SC_BUFFER_BYTES // (d * table.dtype.itemsize)
    g_rows = max(r for r in (8, 16, 32, 64, 128) if r <= fit and per_w % r == 0)
    n_chunk = per_w // g_rows
    assert per_w * n_core * n_sub == n_rows
    mesh = plsc.VectorSubcoreMesh(core_axis_name="c", subcore_axis_name="s")

    def body(table_hbm, idx_hbm, *rest):
        out_hbm, idx_v, buf0, buf1, sem0, sem1 = rest[-6:]
        base = (lax.axis_index("s") * n_core + lax.axis_index("c")) * per_w
        pltpu.sync_copy(idx_hbm.at[pl.ds(pl.multiple_of(part * n_rows + base, 8), per_w)], idx_v)

        def gather(chunk, buf, sem):
            rows = idx_v.at[pl.ds(pl.multiple_of(chunk * g_rows, 8), g_rows)]
            return pltpu.make_async_copy(table_hbm.at[rows], buf, sem)

        def write(chunk, buf):
            pltpu.sync_copy(buf, out_hbm.at[pl.ds(pl.multiple_of(base + chunk * g_rows, 8), g_rows)])

        gather(0, buf0, sem0).start()

        @pl.loop(0, n_chunk // 2)
        def _(i):
            gather(2 * i + 1, buf1, sem1).start()
            gather(2 * i, buf0, sem0).wait()
            write(2 * i, buf0)

            @pl.when(2 * i + 2 < n_chunk)
            def _():
                gather(2 * i + 2, buf0, sem0).start()
            gather(2 * i + 1, buf1, sem1).wait()
            write(2 * i + 1, buf1)

        if n_chunk % 2:
            gather(n_chunk - 1, buf0, sem0).wait()
            write(n_chunk - 1, buf0)

    return pl.kernel(
        body, mesh=mesh, out_type=jax.ShapeDtypeStruct((n_rows, d), table.dtype),
        scratch_types=[pltpu.VMEM((per_w,), jnp.int32), pltpu.VMEM((g_rows, d), table.dtype),
                       pltpu.VMEM((g_rows, d), table.dtype), pltpu.SemaphoreType.DMA, pltpu.SemaphoreType.DMA],
        name="moe_sc_gather",
    )(table, idx, *([] if after is None else [after]))


def _sc_weighted_rows(table, idx, gates, group, n_group):
    from jax.experimental.pallas import tpu_sc as plsc
    d = table.shape[1]
    info = plsc.get_sparse_core_info()
    n_core, n_sub, lanes = info.num_cores, info.num_subcores, info.num_lanes
    n_all = idx.shape[0] // TOP_K
    n_tok = n_all // n_group
    per_w = n_tok // (n_core * n_sub)
    w = 8
    n_chunk = per_w // w
    assert per_w * n_core * n_sub == n_tok and n_chunk * w == per_w and n_chunk % 2 == 0
    mesh = plsc.VectorSubcoreMesh(core_axis_name="c", subcore_axis_name="s")

    def body(table_hbm, idx_hbm, g_hbm, out_hbm, *scr):
        idx_v, g_v, rb, ob = scr[0:4], scr[4:8], (scr[8:12], scr[12:16]), scr[16:18]
        sems, wsems = scr[18:20], scr[20:22]
        base = (lax.axis_index("s") * n_core + lax.axis_index("c")) * per_w
        for k in range(TOP_K):
            off = pl.multiple_of(k * n_all + group * n_tok + base, 8)
            pltpu.sync_copy(idx_hbm.at[pl.ds(off, per_w)], idx_v[k])
            pltpu.sync_copy(g_hbm.at[pl.ds(off, per_w)], g_v[k])

        def gathers(chunk, slot):
            rows = pl.ds(pl.multiple_of(chunk * w, 8), w)
            return [pltpu.make_async_copy(table_hbm.at[idx_v[k].at[rows]], rb[slot][k], sems[slot])
                    for k in range(TOP_K)]

        def write(chunk, slot):
            return pltpu.make_async_copy(ob[slot], out_hbm.at[pl.ds(pl.multiple_of(base + chunk * w, 8), w)],
                                         wsems[slot])

        def combine(chunk, slot):
            @pl.loop(0, w)
            def _(i):
                token = jnp.full((lanes,), chunk * w + i, jnp.int32)
                gs = [plsc.load_gather(g_v[k], [token]) for k in range(TOP_K)]
                for j in range(d // lanes):
                    cols = pl.ds(j * lanes, lanes)
                    acc = gs[0] * rb[slot][0][i, cols]
                    for k in range(1, TOP_K):
                        acc = acc + gs[k] * rb[slot][k][i, cols]
                    ob[slot][i, cols] = acc
            write(chunk, slot).start()

        for c in gathers(0, 0):
            c.start()

        @pl.loop(0, n_chunk // 2)
        def _(i):
            for c in gathers(2 * i + 1, 1):
                c.start()
            for c in gathers(2 * i, 0):
                c.wait()

            @pl.when(i > 0)
            def _():
                write(2 * i - 2, 0).wait()
            combine(2 * i, 0)

            @pl.when(2 * i + 2 < n_chunk)
            def _():
                for c in gathers(2 * i + 2, 0):
                    c.start()
            for c in gathers(2 * i + 1, 1):
                c.wait()

            @pl.when(i > 0)
            def _():
                write(2 * i - 1, 1).wait()
            combine(2 * i + 1, 1)

        write(n_chunk - 2, 0).wait()
        write(n_chunk - 1, 1).wait()

    scratch = ([pltpu.VMEM((per_w,), jnp.int32)] * TOP_K + [pltpu.VMEM((per_w,), F32)] * TOP_K
               + [pltpu.VMEM((w, d), F32)] * (2 * TOP_K + 2) + [pltpu.SemaphoreType.DMA] * 4)
    return pl.kernel(body, mesh=mesh, out_type=jax.ShapeDtypeStruct((n_tok, d), F32), scratch_types=scratch,
                     compiler_params=pltpu.CompilerParams(needs_layout_passes=False),
                     name="moe_sc_combine")(table, idx, gates)


def _combine_kernel(h_ref, p_ref, y_ref, gffn_ref, wple_ref, wpg_ref, gple_ref, o_ref):
    h2 = h_ref[...] + _rms(y_ref[...], gffn_ref[...])
    ple = (jnp.dot(p_ref[...].astype(BF16), wple_ref[...], preferred_element_type=F32)
           * jax.nn.sigmoid(jnp.dot(h2.astype(BF16), wpg_ref[...], preferred_element_type=F32)))
    o_ref[...] = h2 + _rms(ple, gple_ref[...])


def _combine_kernel_aliased(h_ref, p_ref, y_ref, gffn_ref, wple_ref, wpg_ref, gple_ref, prev_ref, o_ref):
    del prev_ref
    _combine_kernel(h_ref, p_ref, y_ref, gffn_ref, wple_ref, wpg_ref, gple_ref, o_ref)


def _combine(group, n_group, out_prev, h, p, y, g_ffn_post, w_ple_bf16, w_ple_gate_bf16, g_ple_post):
    t, d = h.shape
    tt = 256
    n_tg = t // n_group // tt
    const = lambda shape: pl.BlockSpec(shape, lambda i: (0, 0))
    tok = lambda w: pl.BlockSpec((tt, w), lambda i: (group * n_tg + i, 0))
    in_specs = [tok(d), tok(p.shape[1]), pl.BlockSpec((tt, d), lambda i: (i, 0)),
                const((1, d)), const(w_ple_bf16.shape), const(w_ple_gate_bf16.shape), const((1, d))]
    args = [h, p, y, g_ffn_post[None, :], w_ple_bf16, w_ple_gate_bf16, g_ple_post[None, :]]
    body, aliases = _combine_kernel, {}
    if out_prev is not None:
        in_specs.append(pl.BlockSpec(memory_space=pl.ANY))
        args.append(out_prev)
        body, aliases = _combine_kernel_aliased, {len(args) - 1: 0}
    return pl.pallas_call(
        body,
        grid=(n_tg,),
        in_specs=in_specs,
        out_specs=tok(d),
        out_shape=jax.ShapeDtypeStruct((t, d), F32),
        input_output_aliases=aliases,
        compiler_params=pltpu.CompilerParams(vmem_limit_bytes=VMEM_LIMIT),
        name="moe_combine_ple",
    )(*args)


def _layer(h, p, positions, g_mix_pre, w_in, w_dw, b_dw, g_conv_ln, b_conv_ln, w_out, g_mix_post, g_ffn_pre,
           w_router, b_router, w_gate_up, b_gate_up, w_down, b_down, g_ffn_post, w_ple, w_ple_gate, g_ple_post):
    b, s, d = h.shape
    t = b * s
    cos_t, sin_t = _rope_tables(positions)
    q, k, v, glu = _input_projection(h, cos_t, sin_t, g_mix_pre, w_in)
    attn = _dilated_attention(q, k, v)
    conv = _conformer_conv(glu, w_dw, b_dw, g_conv_ln, b_conv_ln)
    h1, u_ffn, idx, gates = _output_projection(attn, conv, h, w_out, g_mix_post, g_ffn_pre, w_router, b_router)
    cap = t * TOP_K + N_EXPERTS * EXPERT_ROWS
    n_blk = cap // EXPERT_ROWS
    dest, pend, blk = _routing(idx, n_blk)
    blk_e = blk[0, :n_blk]
    nused = pend[N_EXPERTS - 1] // EXPERT_ROWS
    dest_flat = dest.reshape(TOP_K * t)
    tok_of_row = _sc_token_of_row(dest_flat, cap, t)
    blk_c = n_blk // MOE_CHUNKS
    yb = done = None
    xb_next = _sc_gather_rows(u_ffn, tok_of_row, 0, MOE_CHUNKS)
    for c in range(MOE_CHUNKS):
        xb_c = xb_next
        if c + 1 < MOE_CHUNKS:
            xb_next = _sc_gather_rows(u_ffn, tok_of_row, c + 1, MOE_CHUNKS, after=done)
        nused_c = jnp.clip(nused - c * blk_c, 0, blk_c)
        yb, done = _experts(xb_c, c, MOE_CHUNKS, yb, blk_e[c * blk_c:(c + 1) * blk_c], nused_c,
                            w_gate_up, b_gate_up, w_down, b_down)
    w_ple_bf16, w_ple_gate_bf16 = w_ple.astype(BF16), w_ple_gate.astype(BF16)
    h1_flat, p_flat = h1.reshape(t, d), p.reshape(t, -1)
    gate_flat = gates[:TOP_K].reshape(TOP_K * t)
    out = None
    for g in range(MOE_GROUPS):
        y_g = _sc_weighted_rows(yb, dest_flat, gate_flat, g, MOE_GROUPS)
        out = _combine(g, MOE_GROUPS, out, h1_flat, p_flat, y_g, g_ffn_post, w_ple_bf16, w_ple_gate_bf16,
                       g_ple_post)
    return out.reshape(b, s, d)


def kernel(x, p, positions, g_mix_pre, w_in, w_dw, b_dw, g_conv_ln, b_conv_ln, w_out, g_mix_post, g_ffn_pre,
           w_router, b_router, w_gate_up, b_gate_up, w_down, b_down, g_ffn_post, w_ple, w_ple_gate, g_ple_post):
    h = x
    for i in range(p.shape[0]):
        h = _layer(h, p[i], positions, g_mix_pre[i], w_in[i], w_dw[i], b_dw[i], g_conv_ln[i], b_conv_ln[i],
                   w_out[i], g_mix_post[i], g_ffn_pre[i], w_router[i], b_router[i], w_gate_up[i], b_gate_up[i],
                   w_down[i], b_down[i], g_ffn_post[i], w_ple[i], w_ple_gate[i], g_ple_post[i])
    return h
```

```python
import functools

import numpy as np
import jax
import jax.numpy as jnp
from jax import lax
from jax.experimental import pallas as pl
from jax.experimental.pallas import tpu as pltpu

F32 = jnp.float32
BF16 = jnp.bfloat16

HEAD_DIM = 64
N_HEADS = 12
ATTN_WIDTH = N_HEADS * HEAD_DIM
CONV_CHANNELS = 256
CONV_WIDTH = 31
ROPE_DIM = HEAD_DIM // 4
ROPE_THETA = 500000.0
N_EXPERTS = 32
TOP_K = 4
SWIGLU_LIMIT = 7.0
SWIGLU_ALPHA = 1.702
NORM_EPS = 1e-6
WINDOW = 128
N_PLANES = 16
SPAN = N_PLANES * WINDOW
LANES = 128
SUBLANES = 8
NEG = -1e30
EXPERT_ROWS = 512
VMEM_LIMIT = 56 * 1024 * 1024


def _residue_of_plane(p):
    return 4 * (p % 4) + p // 4


def _rms(xv, g):
    var = jnp.mean(xv * xv, axis=-1, keepdims=True)
    return xv * lax.rsqrt(var + NORM_EPS) * g


def _rope_kernel(pos_ref, invf_ref, expand_ref, one_ref, sgn_ref, c_ref, s_ref):
    tn = (((0,), (0,)), ((), ()))

    def spread(t):
        t1 = t.astype(BF16)
        r1 = t - t1.astype(F32)
        t2 = r1.astype(BF16)
        t3 = (r1 - t2.astype(F32)).astype(BF16)
        return sum(lax.dot_general(piece, expand_ref[...], tn, preferred_element_type=F32) for piece in (t1, t2, t3))

    for p in range(N_PLANES):
        ang = invf_ref[...] * pos_ref[0, p:p + 1, :].astype(F32)
        c_ref[0, p] = spread(jnp.cos(ang)) + one_ref[...]
        s_ref[0, p] = spread(jnp.sin(ang)) * sgn_ref[...]


def _rope_tables(positions):
    b, s = positions.shape
    sm = s // N_PLANES
    mt = min(sm, 128)
    plane_res = np.array([_residue_of_plane(p) for p in range(N_PLANES)])
    pos_planes = positions.reshape(b, sm, N_PLANES).transpose(0, 2, 1)[:, plane_res]
    half = ROPE_DIM // 2
    lane = np.arange(LANES) % HEAD_DIM
    inv_freq = ROPE_THETA ** (-jnp.arange(0, ROPE_DIM, 2, dtype=F32) / ROPE_DIM)
    rotary = lane < ROPE_DIM
    expand = jnp.asarray((np.arange(half)[:, None] == lane[None, :] % half) & rotary[None, :], BF16)
    one = jnp.asarray(~rotary, F32)[None, :]
    sgn = jnp.asarray(np.where(lane < half, -1.0, 1.0), F32)[None, :]
    row = pl.BlockSpec((1, LANES), lambda i, j: (0, 0))
    out = pl.BlockSpec((1, N_PLANES, mt, LANES), lambda i, j: (i, 0, j, 0))
    return pl.pallas_call(
        _rope_kernel,
        grid=(b, sm // mt),
        in_specs=[pl.BlockSpec((1, N_PLANES, mt), lambda i, j: (i, 0, j)),
                  pl.BlockSpec((half, 1), lambda i, j: (0, 0)), pl.BlockSpec((half, LANES), lambda i, j: (0, 0)),
                  row, row],
        out_specs=[out, out],
        out_shape=[jax.ShapeDtypeStruct((b, N_PLANES, sm, LANES), F32)] * 2,
        name="rope_tables",
    )(pos_planes, inv_freq[:, None], expand, one, sgn)


PERM_TOKENS = 256
PERM_ROWS = PERM_TOKENS // N_PLANES


def _plane_permutation():
    perm = np.zeros((PERM_TOKENS, PERM_TOKENS), np.float32)
    for p in range(N_PLANES):
        for ml in range(PERM_ROWS):
            perm[PERM_ROWS * p + ml, N_PLANES * ml + _residue_of_plane(p)] = 1.0
    return perm


def _inproj_kernel(x_ref, c_ref, s_ref, g_ref, perm_ref, wqkv_ref, wc_ref, q_ref, k_ref, v_ref, glu_ref):
    g = g_ref[...]
    tt = x_ref.shape[1]
    lane = lax.broadcasted_iota(jnp.int32, (1, LANES), 1) % HEAD_DIM
    first_half = lane < ROPE_DIM // 2

    def rotary(t, cos, sin):
        outs = []
        for j in range(ATTN_WIDTH // LANES):
            tj = t[:, j * LANES:(j + 1) * LANES]
            partner = jnp.where(first_half, pltpu.roll(tj, LANES - ROPE_DIM // 2, 1),
                                pltpu.roll(tj, ROPE_DIM // 2, 1))
            outs.append(tj * cos + partner * sin)
        return jnp.concatenate(outs, axis=1)

    un = _rms(x_ref[0], g).astype(BF16)
    pc = jnp.dot(un, wc_ref[...], preferred_element_type=F32)
    glu_ref[0] = pc[:, :CONV_CHANNELS] * jax.nn.sigmoid(pc[:, CONV_CHANNELS:])

    for sub in range(tt // PERM_TOKENS):
        rows = slice(sub * PERM_ROWS, (sub + 1) * PERM_ROWS)
        u = jnp.dot(perm_ref[...], un[sub * PERM_TOKENS:(sub + 1) * PERM_TOKENS],
                    preferred_element_type=F32).astype(BF16)
        cos = jnp.concatenate([c_ref[0, p, rows, :] for p in range(N_PLANES)], axis=0)
        sin = jnp.concatenate([s_ref[0, p, rows, :] for p in range(N_PLANES)], axis=0)
        proj = jnp.dot(u, wqkv_ref[...], preferred_element_type=F32)
        q = (rotary(proj[:, :ATTN_WIDTH], cos, sin) * (HEAD_DIM ** -0.5)).astype(BF16)
        k = rotary(proj[:, ATTN_WIDTH:2 * ATTN_WIDTH], cos, sin).astype(BF16)
        v = proj[:, 2 * ATTN_WIDTH:].astype(BF16)
        for p in range(N_PLANES):
            chunk = slice(p * PERM_ROWS, (p + 1) * PERM_ROWS)
            q_ref[0, p, rows, :] = q[chunk]
            k_ref[0, p, rows, :] = k[chunk]
            v_ref[0, p, rows, :] = v[chunk]


def _input_projection(x, cos_t, sin_t, g_mix_pre, w_in):
    b, s, d = x.shape
    tt = 1024
    mc = tt // N_PLANES
    wqkv = w_in[:, :3 * ATTN_WIDTH].astype(BF16)
    wc = w_in[:, 3 * ATTN_WIDTH:].astype(BF16)
    perm = jnp.asarray(_plane_permutation(), BF16)
    plane = lambda w: pl.BlockSpec((1, N_PLANES, mc, w), lambda i, j: (i, 0, j, 0))
    plane_shape = jax.ShapeDtypeStruct((b, N_PLANES, s // N_PLANES, ATTN_WIDTH), BF16)
    tok = lambda w: pl.BlockSpec((1, tt, w), lambda i, j: (i, j, 0))
    const = lambda shape: pl.BlockSpec(shape, lambda i, j: (0, 0))
    return pl.pallas_call(
        _inproj_kernel,
        grid=(b, s // tt),
        in_specs=[tok(d), plane(LANES), plane(LANES), const((1, d)), const(perm.shape),
                  const(wqkv.shape), const(wc.shape)],
        out_specs=[plane(ATTN_WIDTH), plane(ATTN_WIDTH), plane(ATTN_WIDTH), tok(CONV_CHANNELS)],
        out_shape=[plane_shape, plane_shape, plane_shape,
                   jax.ShapeDtypeStruct((b, s, CONV_CHANNELS), F32)],
        compiler_params=pltpu.CompilerParams(vmem_limit_bytes=VMEM_LIMIT),
        name="input_projection",
    )(x, cos_t, sin_t, g_mix_pre[None, :], perm, wqkv, wc)


def _attention_biases():
    band = lambda j: np.where((j >= 0) & (j <= WINDOW), 0.0, NEG).astype(np.float32)
    cols = lambda m: np.where(m, NEG, 0.0).astype(np.float32)
    twice = lambda a: np.concatenate([a, a], axis=0)
    mq = np.arange(WINDOW)[:, None]
    kj = np.arange(2 * WINDOW)[None, :]
    j16 = mq + WINDOW - kj
    prev16 = kj < WINDOW
    row = np.arange(128)[:, None]
    col = np.arange(256)[None, :]
    j4 = 4 * (row % 32 - (col % 64 - 32)) + row // 32 - col // 64
    prev4 = col % 64 < 32
    row = np.arange(256)[:, None]
    col = np.arange(512)[None, :]
    res = np.vectorize(_residue_of_plane)
    j1 = 16 * (row % 16 - (col % 32 - 16)) + res(row // 16) - res(col // 32)
    prev1 = col % 32 < 16
    b1 = np.stack([twice(band(j1[:128])), twice(band(j1[128:]))])
    return [jnp.asarray(a) for a in (twice(band(j16)), twice(band(j4)), b1, cols(prev16), cols(prev4), cols(prev1))]


UNROLL = 16


def _attn_kernel(q_ref, kc_ref, kp_ref, vc_ref, vp_ref, b16_ref, b4_ref, b1_ref, p16_ref, p4_ref, p1_ref,
                 o_ref, m_scr, l_scr, a_scr, bias16_scr):
    no_prev = (pl.program_id(1) == 0).astype(F32)
    head0 = lax.broadcasted_iota(jnp.int32, (1, LANES), 1) < HEAD_DIM
    bias16_scr[...] = b16_ref[...] + no_prev * p16_ref[...]

    def tile(qt, kt, vt, bias):
        n = qt.shape[0]
        zero = jnp.zeros_like(qt)
        q2 = jnp.concatenate([jnp.where(head0, qt, zero), jnp.where(head0, zero, qt)], axis=0)
        s = lax.dot_general(q2, kt, (((1,), (1,)), ((), ())), preferred_element_type=F32) + bias
        m = jnp.max(s, axis=-1, keepdims=True)
        e = jnp.exp(s - m).astype(BF16)
        va = jnp.concatenate([vt, jnp.ones_like(vt)], axis=1)
        o = jnp.dot(e, va, preferred_element_type=F32)
        pick = lambda top, bot: jnp.where(head0, top, bot)
        mm = pick(jnp.broadcast_to(m[:n], (n, LANES)), jnp.broadcast_to(m[n:], (n, LANES)))
        return mm, pick(o[:n, LANES:], o[n:, LANES:]), pick(o[:n, :LANES], o[n:, :LANES])

    def put(branch, start, size, stats, off):
        for scr, val in zip((m_scr, l_scr, a_scr), stats):
            scr[branch, pl.ds(start, size), :] = val[off:off + size]

    def body16(i, carry):
        for p in [UNROLL * i + u for u in range(UNROLL)]:
            kt = jnp.concatenate([kp_ref[0, p], kc_ref[0, p]], axis=0)
            vt = jnp.concatenate([vp_ref[0, p], vc_ref[0, p]], axis=0)
            put(0, pl.multiple_of(p * WINDOW, WINDOW), WINDOW, tile(q_ref[0, p], kt, vt, bias16_scr[...]), 0)
        return carry
    lax.fori_loop(0, N_PLANES // UNROLL, body16, 0)

    def body4(g, carry):
        for c, i in [(4 * g + cc, ii) for cc in range(4) for ii in range(4)]:
            qt = jnp.concatenate([q_ref[0, 4 * c + a, 32 * i:32 * i + 32, :] for a in range(4)], axis=0)
            if i == 0:
                ks = [x for a in range(4) for x in (kp_ref[0, 4 * c + a, 96:128, :], kc_ref[0, 4 * c + a, 0:32, :])]
                vs = [x for a in range(4) for x in (vp_ref[0, 4 * c + a, 96:128, :], vc_ref[0, 4 * c + a, 0:32, :])]
                bias = b4_ref[...] + no_prev * p4_ref[...]
            else:
                ks = [kc_ref[0, 4 * c + a, 32 * i - 32:32 * i + 32, :] for a in range(4)]
                vs = [vc_ref[0, 4 * c + a, 32 * i - 32:32 * i + 32, :] for a in range(4)]
                bias = b4_ref[...]
            stats = tile(qt, jnp.concatenate(ks, axis=0), jnp.concatenate(vs, axis=0), bias)
            for a in range(4):
                put(1, pl.multiple_of((4 * c + a) * WINDOW + 32 * i, 32), 32, stats, 32 * a)
        return carry
    lax.fori_loop(0, 1, body4, 0)

    def tile1(i, first):
        rq = pl.ds(pl.multiple_of(16 * i, 16), 16)
        if first:
            ks = [x for p in range(N_PLANES) for x in (kp_ref[0, p, 112:128, :], kc_ref[0, p, 0:16, :])]
            vs = [x for p in range(N_PLANES) for x in (vp_ref[0, p, 112:128, :], vc_ref[0, p, 0:16, :])]
        else:
            rk = pl.ds(pl.multiple_of(16 * i - 16, 16), 32)
            ks = [kc_ref[0, p, rk, :] for p in range(N_PLANES)]
            vs = [vc_ref[0, p, rk, :] for p in range(N_PLANES)]
        kt = jnp.concatenate(ks, axis=0)
        vt = jnp.concatenate(vs, axis=0)
        for half in range(2):
            planes = range(8 * half, 8 * half + 8)
            qt = jnp.concatenate([q_ref[0, p, rq, :] for p in planes], axis=0)
            bias = b1_ref[half] + no_prev * p1_ref[...] if first else b1_ref[half]
            stats = tile(qt, kt, vt, bias)
            for p in planes:
                put(2, pl.multiple_of(p * WINDOW + 16 * i, 16), 16, stats, 16 * (p - 8 * half))

    tile1(0, True)
    for i in range(1, 4):
        tile1(i, False)

    def body1(g, carry):
        for u in range(4):
            tile1(4 * g + u, False)
        return carry
    lax.fori_loop(1, WINDOW // 64, body1, 0)

    def combine(i, carry):
        for p in (2 * i, 2 * i + 1):
            rows = pl.ds(pl.multiple_of(p * WINDOW, WINDOW), WINDOW)
            ms = [m_scr[b, rows, :] for b in range(3)]
            mx = jnp.maximum(jnp.maximum(ms[0], ms[1]), ms[2])
            ws = [jnp.exp(m - mx) for m in ms]
            den = ws[0] * l_scr[0, rows, :] + ws[1] * l_scr[1, rows, :] + ws[2] * l_scr[2, rows, :]
            num = ws[0] * a_scr[0, rows, :] + ws[1] * a_scr[1, rows, :] + ws[2] * a_scr[2, rows, :]
            o_ref[0, p] = (num / den).astype(BF16)
        return carry
    lax.fori_loop(0, N_PLANES // 2, combine, 0)


def _dilated_attention(q, k, v):
    b, _, sm, _ = q.shape
    n_span = sm // WINDOW
    cur = pl.BlockSpec((1, N_PLANES, WINDOW, LANES), lambda i, j, h: (i, 0, j, h))
    prev = pl.BlockSpec((1, N_PLANES, WINDOW, LANES), lambda i, j, h: (i, 0, jnp.maximum(j - 1, 0), h))
    biases = _attention_biases()
    bias_specs = [pl.BlockSpec(a.shape, lambda i, j, h, nd=a.ndim: (0,) * nd) for a in biases]
    stats = pltpu.VMEM((3, SPAN, LANES), F32)
    return pl.pallas_call(
        _attn_kernel,
        grid=(b, n_span, ATTN_WIDTH // LANES),
        in_specs=[cur, cur, prev, cur, prev] + bias_specs,
        out_specs=cur,
        out_shape=jax.ShapeDtypeStruct(q.shape, BF16),
        scratch_shapes=[stats, stats, stats, pltpu.VMEM((2 * WINDOW, 2 * WINDOW), F32)],
        compiler_params=pltpu.CompilerParams(vmem_limit_bytes=VMEM_LIMIT),
        name="dilated_attention",
    )(q, k, k, v, v, *biases)


CONV_HALO = 32


def _conv_kernel(cur_ref, prev_ref, w_ref, b_ref, g_ref, bl_ref, o_ref, scr, *, chunk):
    tt = cur_ref.shape[1]
    has_prev = pl.program_id(1) > 0
    scr[0, 0:CONV_HALO, :] = jnp.where(has_prev, prev_ref[0], 0.0)
    scr[0, CONV_HALO:CONV_HALO + tt, :] = cur_ref[0]
    aligned_rows = tt + CONV_HALO - SUBLANES
    for s in range(1, SUBLANES):
        scr[s, 0:aligned_rows, :] = scr[0, s:s + aligned_rows, :]
    lead = CONV_HALO - (CONV_WIDTH - 1)
    for c0 in range(0, tt, chunk):
        acc = jnp.zeros((chunk, CONV_CHANNELS), F32)
        for j in range(CONV_WIDTH):
            s, a = (lead + j) % SUBLANES, (lead + j) // SUBLANES * SUBLANES
            acc = acc + w_ref[j:j + 1, :] * scr[s, c0 + a:c0 + a + chunk, :]
        y = acc + b_ref[...]
        mu = jnp.mean(y, axis=-1, keepdims=True)
        var = jnp.mean(jnp.square(y - mu), axis=-1, keepdims=True)
        yn = (y - mu) * lax.rsqrt(var + NORM_EPS) * g_ref[...] + bl_ref[...]
        o_ref[0, c0:c0 + chunk, :] = (yn * jax.nn.sigmoid(yn)).astype(BF16)


def _conformer_conv(glu, w_dw, b_dw, g_ln, b_ln):
    b, s, c = glu.shape
    tt = 512
    row = pl.BlockSpec((1, c), lambda i, j: (0, 0))
    return pl.pallas_call(
        functools.partial(_conv_kernel, chunk=128),
        grid=(b, s // tt),
        in_specs=[pl.BlockSpec((1, tt, c), lambda i, j: (i, j, 0)),
                  pl.BlockSpec((1, CONV_HALO, c), lambda i, j: (i, jnp.maximum(j * (tt // CONV_HALO) - 1, 0), 0)),
                  pl.BlockSpec((CONV_WIDTH, c), lambda i, j: (0, 0)), row, row, row],
        out_specs=pl.BlockSpec((1, tt, c), lambda i, j: (i, j, 0)),
        out_shape=jax.ShapeDtypeStruct((b, s, c), BF16),
        scratch_shapes=[pltpu.VMEM((SUBLANES, CONV_HALO + tt, c), F32)],
        name="conformer_conv",
    )(glu, glu, w_dw[:, 0, :], b_dw[None, :], g_ln[None, :], b_ln[None, :])


def _outproj_kernel(attn_ref, conv_ref, x_ref, permt_ref, woa_ref, woc_ref, gpost_ref, gffn_ref, wr_ref, br_ref,
                    h_ref, u_ref, idx_ref, gate_ref):
    tt = x_ref.shape[1]
    nat = []
    for sub in range(tt // PERM_TOKENS):
        rows = slice(sub * PERM_ROWS, (sub + 1) * PERM_ROWS)
        a = jnp.concatenate([attn_ref[0, p, rows, :] for p in range(N_PLANES)], axis=0)
        nat.append(jnp.dot(permt_ref[...], a, preferred_element_type=F32).astype(BF16))
    mix = (jnp.dot(jnp.concatenate(nat, axis=0), woa_ref[...], preferred_element_type=F32)
           + jnp.dot(conv_ref[0], woc_ref[...], preferred_element_type=F32))
    h = x_ref[0] + _rms(mix, gpost_ref[...])
    h_ref[0] = h
    u = _rms(h, gffn_ref[...])
    half = u.shape[1] // 2
    ub = u.astype(BF16).astype(F32)
    u_ref[...] = ((lax.bitcast_convert_type(ub[:, :half], jnp.uint32) >> 16)
                  | (lax.bitcast_convert_type(ub[:, half:], jnp.uint32) & jnp.uint32(0xFFFF0000)))
    u_hi = u.astype(BF16)
    u_lo = (u - u_hi.astype(F32)).astype(BF16)
    nt = (((1,), (1,)), ((), ()))
    by_hi = lax.dot_general(wr_ref[...], u_hi, nt, preferred_element_type=F32)
    by_lo = lax.dot_general(wr_ref[:N_EXPERTS], u_lo, nt, preferred_element_type=F32)
    logits = by_hi[:N_EXPERTS] + (by_hi[N_EXPERTS:] + by_lo) + br_ref[...]
    rows = lax.broadcasted_iota(jnp.int32, logits.shape, 0)
    vals = logits
    tops, idxs = [], []
    for _ in range(TOP_K):
        mx = jnp.max(vals, axis=0, keepdims=True)
        ix = jnp.min(jnp.where(vals == mx, rows, N_EXPERTS), axis=0, keepdims=True)
        tops.append(mx)
        idxs.append(ix)
        vals = jnp.where(rows == ix, -jnp.inf, vals)
    ex = [jnp.exp(t - tops[0]) for t in tops]
    den = ex[0] + ex[1] + ex[2] + ex[3]
    idx_ref[...] = jnp.concatenate(idxs, axis=0)
    gate_ref[...] = jnp.concatenate([e / den for e in ex] + [jnp.zeros((8 - TOP_K, tt), F32)], axis=0)


def _output_projection(attn, conv, x, w_out, g_mix_post, g_ffn_pre, w_router, b_router):
    b, s, d = x.shape
    tt = 1024
    mc = tt // N_PLANES
    n_t = s // tt
    woa = w_out[:ATTN_WIDTH].astype(BF16)
    woc = w_out[ATTN_WIDTH:].astype(BF16)
    wr_hi = w_router.T.astype(BF16)
    wr_split = jnp.concatenate([wr_hi, (w_router.T - wr_hi.astype(F32)).astype(BF16)], axis=0)
    permt =jnp.asarray(_plane_permutation().T, BF16)
    const = lambda shape: pl.BlockSpec(shape, lambda i, j: (0, 0))
    flat = lambda w: pl.BlockSpec((tt, w), lambda i, j: (i * n_t + j, 0))
    lanes = lambda r: pl.BlockSpec((r, tt), lambda i, j: (0, i * n_t + j))
    return pl.pallas_call(
        _outproj_kernel,
        grid=(b, n_t),
        in_specs=[pl.BlockSpec((1, N_PLANES, mc, ATTN_WIDTH), lambda i, j: (i, 0, j, 0)),
                  pl.BlockSpec((1, tt, CONV_CHANNELS), lambda i, j: (i, j, 0)),
                  pl.BlockSpec((1, tt, d), lambda i, j: (i, j, 0)),
                  const(permt.shape), const(woa.shape), const(woc.shape), const((1, d)), const((1, d)),
                  const((2 * N_EXPERTS, d)), const((N_EXPERTS, 1))],
        out_specs=[pl.BlockSpec((1, tt, d), lambda i, j: (i, j, 0)), flat(d // 2), lanes(TOP_K), lanes(8)],
        out_shape=[jax.ShapeDtypeStruct((b, s, d), F32),
                   jax.ShapeDtypeStruct((b * s, d // 2), jnp.uint32),
                   jax.ShapeDtypeStruct((TOP_K, b * s), jnp.int32),
                   jax.ShapeDtypeStruct((8, b * s), F32)],
        compiler_params=pltpu.CompilerParams(vmem_limit_bytes=VMEM_LIMIT),
        name="output_projection_router",
    )(attn, conv, x, permt, woa, woc, g_mix_post[None, :], g_ffn_pre[None, :], wr_split, b_router[:, None])


def _route_kernel(idx_ref, dest_ref, pend_ref, blk_ref, carry, pstart):
    phase = pl.program_id(0)
    step = pl.program_id(1)
    tt = idx_ref.shape[1]
    rows = lax.broadcasted_iota(jnp.int32, (N_EXPERTS, tt), 0)
    hot = [rows == idx_ref[k:k + 1, :] for k in range(TOP_K)]
    memb = sum(h.astype(F32) for h in hot)

    @pl.when((phase == 0) & (step == 0))
    def _():
        carry[...] = jnp.zeros_like(carry)

    @pl.when((phase == 1) & (step == 0))
    def _():
        counts = carry[...]
        padded = jnp.floor((counts + (EXPERT_ROWS - 1)) * (1.0 / EXPERT_ROWS)) * EXPERT_ROWS
        tri = (lax.broadcasted_iota(jnp.int32, (N_EXPERTS, N_EXPERTS), 1)
               <= lax.broadcasted_iota(jnp.int32, (N_EXPERTS, N_EXPERTS), 0)).astype(F32)
        pend = jnp.dot(tri, padded, precision=lax.Precision.HIGHEST, preferred_element_type=F32)
        pstart[...] = pend - padded
        pend_ref[...] = pend.astype(jnp.int32)
        starts = lax.broadcasted_iota(jnp.int32, (N_EXPERTS, blk_ref.shape[1]), 1) * EXPERT_ROWS
        ended = (pend.astype(jnp.int32) <= starts).astype(jnp.int32)
        blk_ref[...] = jnp.minimum(jnp.sum(ended, axis=0, keepdims=True), N_EXPERTS - 1)
        carry[...] = jnp.zeros_like(carry)

    @pl.when(phase == 1)
    def _():
        earlier = (lax.broadcasted_iota(jnp.int32, (tt, tt), 0)
                   < lax.broadcasted_iota(jnp.int32, (tt, tt), 1)).astype(BF16)
        row = jnp.dot(memb.astype(BF16), earlier, preferred_element_type=F32) + (carry[...] + pstart[...])
        dest_ref[...] = jnp.concatenate(
            [jnp.sum(jnp.where(h, row, 0.0), axis=0, keepdims=True) for h in hot], axis=0).astype(jnp.int32)

    carry[...] = carry[...] + jnp.sum(memb, axis=1, keepdims=True)


def _routing(idx, n_blk):
    _, t = idx.shape
    tt = 512
    blk_lanes = -(-n_blk // LANES) * LANES
    return pl.pallas_call(
        _route_kernel,
        grid=(2, t // tt),
        in_specs=[pl.BlockSpec((TOP_K, tt), lambda ph, i: (0, i))],
        out_specs=[pl.BlockSpec((TOP_K, tt), lambda ph, i: (0, i * ph)),
                   pl.BlockSpec((N_EXPERTS, 1), lambda ph, i: (0, 0)),
                   pl.BlockSpec((1, blk_lanes), lambda ph, i: (0, 0))],
        out_shape=[jax.ShapeDtypeStruct((TOP_K, t), jnp.int32), jax.ShapeDtypeStruct((N_EXPERTS, 1), jnp.int32),
                   jax.ShapeDtypeStruct((1, blk_lanes), jnp.int32)],
        scratch_shapes=[pltpu.VMEM((N_EXPERTS, 1), F32), pltpu.VMEM((N_EXPERTS, 1), F32)],
        compiler_params=pltpu.CompilerParams(dimension_semantics=("arbitrary", "arbitrary")),
        name="moe_routing",
    )(idx)


SC_SCAN = 16384
SC_UNROLL = 8


def _sc_token_of_row(dest_flat, cap, n_tok):
    from jax.experimental.pallas import tpu_sc as plsc
    assert n_tok & (n_tok - 1) == 0
    info = plsc.get_sparse_core_info()
    n_core, n_sub, lanes = info.num_cores, info.num_subcores, info.num_lanes
    n_src = dest_flat.shape[0]
    per_w = cap // (n_core * n_sub)
    assert per_w * n_core * n_sub == cap and per_w % lanes == 0 and n_src % SC_SCAN == 0
    mesh = plsc.VectorSubcoreMesh(core_axis_name="c", subcore_axis_name="s")

    def body(dest_hbm, out_hbm, src_v, out_v):
        lo = (lax.axis_index("s") * n_core + lax.axis_index("c")) * per_w
        lane = lax.iota(jnp.int32, lanes)

        @pl.loop(0, per_w // lanes)
        def _(i):
            out_v[pl.ds(pl.multiple_of(i * lanes, lanes), lanes)] = (lo + i * lanes + lane) & (n_tok - 1)

        @pl.loop(0, n_src // SC_SCAN)
        def _(c):
            pltpu.sync_copy(dest_hbm.at[pl.ds(pl.multiple_of(c * SC_SCAN, 8), SC_SCAN)], src_v)

            @pl.loop(0, SC_SCAN // (lanes * SC_UNROLL))
            def _(g):
                for u in range(SC_UNROLL):
                    at = (g * SC_UNROLL + u) * lanes
                    local = src_v[pl.ds(pl.multiple_of(at, lanes), lanes)] - lo
                    mine = lax.bitcast_convert_type(local, jnp.uint32) < jnp.uint32(per_w)
                    tok = (c * SC_SCAN + at + lane) & (n_tok - 1)
                    plsc.store_scatter(out_v, [local], tok, mask=mine)
        pltpu.sync_copy(out_v, out_hbm.at[pl.ds(pl.multiple_of(lo, 8), per_w)])

    return pl.kernel(body, mesh=mesh, out_type=jax.ShapeDtypeStruct((cap,), jnp.int32),
                     scratch_types=[pltpu.VMEM((SC_SCAN,), jnp.int32), pltpu.VMEM((per_w,), jnp.int32)],
                     compiler_params=pltpu.CompilerParams(needs_layout_passes=False),
                     name="moe_sc_token_of_row")(dest_flat)


def _expert_kernel(blk_e_ref, nused_ref, xb_ref, wgu_ref, bgu_ref, wd_ref, bd_ref, *rest):
    yb_ref, done_ref, wgu_b, wd_b = rest[-4:]
    j = pl.program_id(0)
    done_ref[...] = jnp.zeros_like(done_ref)
    nused = nused_ref[0]
    d_ff = wd_ref.shape[1]

    new_expert = (j == 0) | (blk_e_ref[j] != blk_e_ref[jnp.maximum(j - 1, 0)])

    @pl.when((j < nused) & new_expert)
    def _():
        def cast(src, dst):
            def body(i, carry):
                rows = pl.ds(pl.multiple_of(i * LANES, LANES), LANES)
                dst[rows, :] = src[0, rows, :].astype(BF16)
                return carry
            lax.fori_loop(0, dst.shape[0] // LANES, body, 0)
        cast(wgu_ref, wgu_b)
        cast(wd_ref, wd_b)

    @pl.when(j < nused)
    def _():
        word = xb_ref[...]
        half = word.shape[1]
        lo = lax.bitcast_convert_type(word << 16, F32).astype(BF16)
        hi = lax.bitcast_convert_type(word & jnp.uint32(0xFFFF0000), F32).astype(BF16)
        gu = (jnp.dot(lo, wgu_b[:half, :], preferred_element_type=F32)
              + jnp.dot(hi, wgu_b[half:, :], preferred_element_type=F32) + bgu_ref[0])
        gate = jnp.minimum(gu[:, :d_ff], SWIGLU_LIMIT)
        up = jnp.clip(gu[:, d_ff:], -SWIGLU_LIMIT, SWIGLU_LIMIT)
        hid = (up + 1.0) * (gate * jax.nn.sigmoid(gate * SWIGLU_ALPHA))
        yb_ref[...] = jnp.dot(hid.astype(BF16), wd_b[...], preferred_element_type=F32) + bd_ref[0]

    @pl.when(j >= nused)
    def _():
        yb_ref[...] = jnp.zeros_like(yb_ref)


def _experts(xb, chunk, n_chunk, yb_prev, blk_e, nused, w_gate_up, b_gate_up, w_down, b_down):
    rows, d_in = xb.shape
    e, d, ff2 = w_gate_up.shape
    d_ff = w_down.shape[1]
    n_blk = rows // EXPERT_ROWS
    live = lambda j, nu: jnp.maximum(jnp.minimum(j, nu[0] - 1), 0)
    expert = lambda j, be, nu: (be[live(j, nu)], 0, 0)
    in_specs = [pl.BlockSpec((EXPERT_ROWS, d_in), lambda j, be, nu: (live(j, nu), 0)),
                pl.BlockSpec((1, d, ff2), expert), pl.BlockSpec((1, 1, ff2), expert),
                pl.BlockSpec((1, d_ff, d), expert), pl.BlockSpec((1, 1, d), expert)]
    args = [blk_e, nused, xb, w_gate_up, b_gate_up[:, None, :], w_down, b_down[:, None, :]]
    aliases = {}
    if yb_prev is not None:
        in_specs.append(pl.BlockSpec(memory_space=pl.ANY))
        args.append(yb_prev)
        aliases = {len(args) - 1: 0}
    grid_spec = pltpu.PrefetchScalarGridSpec(
        num_scalar_prefetch=2,
        grid=(n_blk,),
        in_specs=in_specs,
        out_specs=[pl.BlockSpec((EXPERT_ROWS, d), lambda j, be, nu: (chunk * n_blk + j, 0)),
                   pl.BlockSpec((SUBLANES, LANES), lambda j, be, nu: (0, 0))],
        scratch_shapes=[pltpu.VMEM((d, ff2), BF16), pltpu.VMEM((d_ff, d), BF16)],
    )
    return pl.pallas_call(
        _expert_kernel,
        grid_spec=grid_spec,
        out_shape=[jax.ShapeDtypeStruct((n_chunk * rows, d), F32), jax.ShapeDtypeStruct((SUBLANES, LANES), F32)],
        input_output_aliases=aliases,
        compiler_params=pltpu.CompilerParams(dimension_semantics=("arbitrary",), vmem_limit_bytes=VMEM_LIMIT),
        name="moe_experts",
    )(*args)


SC_BUFFER_BYTES = 128 * 1024
MOE_CHUNKS = 8
MOE_GROUPS = 4


def _sc_gather_rows(table, idx, part, n_part, after=None):
    from jax.experimental.pallas import tpu_sc as plsc
    n_rows = idx.shape[0] // n_part
    d = table.shape[1]
    info = plsc.get_sparse_core_info()
    n_core, n_sub = info.num_cores, info.num_subcores
    per_w = n_rows // (n_core * n_sub)
    fit = SC_BUFFER_BYTES // (d * table.dtype.itemsize)
    g_rows = max(r for r in (8, 16, 32, 64, 128) if r <= fit and per_w % r == 0)
    n_chunk = per_w // g_rows
    assert per_w * n_core * n_sub == n_rows
    mesh = plsc.VectorSubcoreMesh(core_axis_name="c", subcore_axis_name="s")

    def body(table_hbm, idx_hbm, *rest):
        out_hbm, idx_v, buf0, buf1, sem0, sem1 = rest[-6:]
        base = (lax.axis_index("s") * n_core + lax.axis_index("c")) * per_w
        pltpu.sync_copy(idx_hbm.at[pl.ds(pl.multiple_of(part * n_rows + base, 8), per_w)], idx_v)

        def gather(chunk, buf, sem):
            rows = idx_v.at[pl.ds(pl.multiple_of(chunk * g_rows, 8), g_rows)]
            return pltpu.make_async_copy(table_hbm.at[rows], buf, sem)

        def write(chunk, buf):
            pltpu.sync_copy(buf, out_hbm.at[pl.ds(pl.multiple_of(base + chunk * g_rows, 8), g_rows)])

        gather(0, buf0, sem0).start()

        @pl.loop(0, n_chunk // 2)
        def _(i):
            gather(2 * i + 1, buf1, sem1).start()
            gather(2 * i, buf0, sem0).wait()
            write(2 * i, buf0)

            @pl.when(2 * i + 2 < n_chunk)
            def _():
                gather(2 * i + 2, buf0, sem0).start()
            gather(2 * i + 1, buf1, sem1).wait()
            write(2 * i + 1, buf1)

        if n_chunk % 2:
            gather(n_chunk - 1, buf0, sem0).wait()
            write(n_chunk - 1, buf0)

    return pl.kernel(
        body, mesh=mesh, out_type=jax.ShapeDtypeStruct((n_rows, d), table.dtype),
        scratch_types=[pltpu.VMEM((per_w,), jnp.int32), pltpu.VMEM((g_rows, d), table.dtype),
                       pltpu.VMEM((g_rows, d), table.dtype), pltpu.SemaphoreType.DMA, pltpu.SemaphoreType.DMA],
        name="moe_sc_gather",
    )(table, idx, *([] if after is None else [after]))


def _sc_weighted_rows(table, idx, gates, group, n_group):
    from jax.experimental.pallas import tpu_sc as plsc
    d = table.shape[1]
    info = plsc.get_sparse_core_info()
    n_core, n_sub, lanes = info.num_cores, info.num_subcores, info.num_lanes
    n_all = idx.shape[0] // TOP_K
    n_tok = n_all // n_group
    per_w = n_tok // (n_core * n_sub)
    w = 8
    n_chunk = per_w // w
    assert per_w * n_core * n_sub == n_tok and n_chunk * w == per_w and n_chunk % 2 == 0
    mesh = plsc.VectorSubcoreMesh(core_axis_name="c", subcore_axis_name="s")

    def body(table_hbm, idx_hbm, g_hbm, out_hbm, *scr):
        idx_v, g_v, rb, ob = scr[0:4], scr[4:8], (scr[8:12], scr[12:16]), scr[16:18]
        sems, wsems = scr[18:20], scr[20:22]
        base = (lax.axis_index("s") * n_core + lax.axis_index("c")) * per_w
        for k in range(TOP_K):
            off = pl.multiple_of(k * n_all + group * n_tok + base, 8)
            pltpu.sync_copy(idx_hbm.at[pl.ds(off, per_w)], idx_v[k])
            pltpu.sync_copy(g_hbm.at[pl.ds(off, per_w)], g_v[k])

        def gathers(chunk, slot):
            rows = pl.ds(pl.multiple_of(chunk * w, 8), w)
            return [pltpu.make_async_copy(table_hbm.at[idx_v[k].at[rows]], rb[slot][k], sems[slot])
                    for k in range(TOP_K)]

        def write(chunk, slot):
            return pltpu.make_async_copy(ob[slot], out_hbm.at[pl.ds(pl.multiple_of(base + chunk * w, 8), w)],
                                         wsems[slot])

        def combine(chunk, slot):
            @pl.loop(0, w)
            def _(i):
                token = jnp.full((lanes,), chunk * w + i, jnp.int32)
                gs = [plsc.load_gather(g_v[k], [token]) for k in range(TOP_K)]
                for j in range(d // lanes):
                    cols = pl.ds(j * lanes, lanes)
                    acc = gs[0] * rb[slot][0][i, cols]
                    for k in range(1, TOP_K):
                        acc = acc + gs[k] * rb[slot][k][i, cols]
                    ob[slot][i, cols] = acc
            write(chunk, slot).start()

        for c in gathers(0, 0):
            c.start()

        @pl.loop(0, n_chunk // 2)
        def _(i):
            for c in gathers(2 * i + 1, 1):
                c.start()
            for c in gathers(2 * i, 0):
                c.wait()

            @pl.when(i > 0)
            def _():
                write(2 * i - 2, 0).wait()
            combine(2 * i, 0)

            @pl.when(2 * i + 2 < n_chunk)
            def _():
                for c in gathers(2 * i + 2, 0):
                    c.start()
            for c in gathers(2 * i + 1, 1):
                c.wait()

            @pl.when(i > 0)
            def _():
                write(2 * i - 1, 1).wait()
            combine(2 * i + 1, 1)

        write(n_chunk - 2, 0).wait()
        write(n_chunk - 1, 1).wait()

    scratch = ([pltpu.VMEM((per_w,), jnp.int32)] * TOP_K + [pltpu.VMEM((per_w,), F32)] * TOP_K
               + [pltpu.VMEM((w, d), F32)] * (2 * TOP_K + 2) + [pltpu.SemaphoreType.DMA] * 4)
    return pl.kernel(body, mesh=mesh, out_type=jax.ShapeDtypeStruct((n_tok, d), F32), scratch_types=scratch,
                     compiler_params=pltpu.CompilerParams(needs_layout_passes=False),
                     name="moe_sc_combine")(table, idx, gates)


def _combine_kernel(h_ref, p_ref, y_ref, gffn_ref, wple_ref, wpg_ref, gple_ref, o_ref):
    h2 = h_ref[...] + _rms(y_ref[...], gffn_ref[...])
    ple = (jnp.dot(p_ref[...].astype(BF16), wple_ref[...], preferred_element_type=F32)
           * jax.nn.sigmoid(jnp.dot(h2.astype(BF16), wpg_ref[...], preferred_element_type=F32)))
    o_ref[...] = h2 + _rms(ple, gple_ref[...])


def _combine_kernel_aliased(h_ref, p_ref, y_ref, gffn_ref, wple_ref, wpg_ref, gple_ref, prev_ref, o_ref):
    del prev_ref
    _combine_kernel(h_ref, p_ref, y_ref, gffn_ref, wple_ref, wpg_ref, gple_ref, o_ref)


def _combine(group, n_group, out_prev, h, p, y, g_ffn_post, w_ple_bf16, w_ple_gate_bf16, g_ple_post):
    t, d = h.shape
    tt = 256
    n_tg = t // n_group // tt
    const = lambda shape: pl.BlockSpec(shape, lambda i: (0, 0))
    tok = lambda w: pl.BlockSpec((tt, w), lambda i: (group * n_tg + i, 0))
    in_specs = [tok(d), tok(p.shape[1]), pl.BlockSpec((tt, d), lambda i: (i, 0)),
                const((1, d)), const(w_ple_bf16.shape), const(w_ple_gate_bf16.shape), const((1, d))]
    args = [h, p, y, g_ffn_post[None, :], w_ple_bf16, w_ple_gate_bf16, g_ple_post[None, :]]
    body, aliases = _combine_kernel, {}
    if out_prev is not None:
        in_specs.append(pl.BlockSpec(memory_space=pl.ANY))
        args.append(out_prev)
        body, aliases = _combine_kernel_aliased, {len(args) - 1: 0}
    return pl.pallas_call(
        body,
        grid=(n_tg,),
        in_specs=in_specs,
        out_specs=tok(d),
        out_shape=jax.ShapeDtypeStruct((t, d), F32),
        input_output_aliases=aliases,
        compiler_params=pltpu.CompilerParams(vmem_limit_bytes=VMEM_LIMIT),
        name="moe_combine_ple",
    )(*args)


def _layer(h, p, positions, g_mix_pre, w_in, w_dw, b_dw, g_conv_ln, b_conv_ln, w_out, g_mix_post, g_ffn_pre,
           w_router, b_router, w_gate_up, b_gate_up, w_down, b_down, g_ffn_post, w_ple, w_ple_gate, g_ple_post):
    b, s, d = h.shape
    t = b * s
    cos_t, sin_t = _rope_tables(positions)
    q, k, v, glu = _input_projection(h, cos_t, sin_t, g_mix_pre, w_in)
    attn = _dilated_attention(q, k, v)
    conv = _conformer_conv(glu, w_dw, b_dw, g_conv_ln, b_conv_ln)
    h1, u_ffn, idx, gates = _output_projection(attn, conv, h, w_out, g_mix_post, g_ffn_pre, w_router, b_router)
    cap = t * TOP_K + N_EXPERTS * EXPERT_ROWS
    n_blk = cap // EXPERT_ROWS
    dest, pend, blk = _routing(idx, n_blk)
    blk_e = blk[0, :n_blk]
    nused = pend[N_EXPERTS - 1] // EXPERT_ROWS
    dest_flat = dest.reshape(TOP_K * t)
    tok_of_row = _sc_token_of_row(dest_flat, cap, t)
    blk_c = n_blk // MOE_CHUNKS
    yb = done = None
    xb_next = _sc_gather_rows(u_ffn, tok_of_row, 0, MOE_CHUNKS)
    for c in range(MOE_CHUNKS):
        xb_c = xb_next
        if c + 1 < MOE_CHUNKS:
            xb_next = _sc_gather_rows(u_ffn, tok_of_row, c + 1, MOE_CHUNKS, after=done)
        nused_c = jnp.clip(nused - c * blk_c, 0, blk_c)
        yb, done = _experts(xb_c, c, MOE_CHUNKS, yb, blk_e[c * blk_c:(c + 1) * blk_c], nused_c,
                            w_gate_up, b_gate_up, w_down, b_down)
    w_ple_bf16, w_ple_gate_bf16 = w_ple.astype(BF16), w_ple_gate.astype(BF16)
    h1_flat, p_flat = h1.reshape(t, d), p.reshape(t, -1)
    gate_flat = gates[:TOP_K].reshape(TOP_K * t)
    out = None
    for g in range(MOE_GROUPS):
        y_g = _sc_weighted_rows(yb, dest_flat, gate_flat, g, MOE_GROUPS)
        out = _combine(g, MOE_GROUPS, out, h1_flat, p_flat, y_g, g_ffn_post, w_ple_bf16, w_ple_gate_bf16,
                       g_ple_post)
    return out.reshape(b, s, d)


def kernel(x, p, positions, g_mix_pre, w_in, w_dw, b_dw, g_conv_ln, b_conv_ln, w_out, g_mix_post, g_ffn_pre,
           w_router, b_router, w_gate_up, b_gate_up, w_down, b_down, g_ffn_post, w_ple, w_ple_gate, g_ple_post):
    h = x
    for i in range(p.shape[0]):
        h = _layer(h, p[i], positions, g_mix_pre[i], w_in[i], w_dw[i], b_dw[i], g_conv_ln[i], b_conv_ln[i],
                   w_out[i], g_mix_post[i], g_ffn_pre[i], w_router[i], b_router[i], w_gate_up[i], b_gate_up[i],
                   w_down[i], b_down[i], g_ffn_post[i], w_ple[i], w_ple_gate[i], g_ple_post[i])
    return h
```

```python
import functools

import numpy as np
import jax
import jax.numpy as jnp
from jax import lax
from jax.experimental import pallas as pl
from jax.experimental.pallas import tpu as pltpu

F32 = jnp.float32
BF16 = jnp.bfloat16

HEAD_DIM = 64
N_HEADS = 12
ATTN_WIDTH = N_HEADS * HEAD_DIM
CONV_CHANNELS = 256
CONV_WIDTH = 31
ROPE_DIM = HEAD_DIM // 4
ROPE_THETA = 500000.0
N_EXPERTS = 32
TOP_K = 4
SWIGLU_LIMIT = 7.0
SWIGLU_ALPHA = 1.702
NORM_EPS = 1e-6
WINDOW = 128
N_PLANES = 16
SPAN = N_PLANES * WINDOW
LANES = 128
SUBLANES = 8
NEG = -1e30
EXPERT_ROWS = 512
VMEM_LIMIT = 56 * 1024 * 1024


def _residue_of_plane(p):
    return 4 * (p % 4) + p // 4


def _rms(xv, g):
    var = jnp.mean(xv * xv, axis=-1, keepdims=True)
    return xv * lax.rsqrt(var + NORM_EPS) * g


def _rope_kernel(pos_ref, invf_ref, expand_ref, one_ref, sgn_ref, c_ref, s_ref):
    tn = (((0,), (0,)), ((), ()))

    def spread(t):
        t1 = t.astype(BF16)
        r1 = t - t1.astype(F32)
        t2 = r1.astype(BF16)
        t3 = (r1 - t2.astype(F32)).astype(BF16)
        return sum(lax.dot_general(piece, expand_ref[...], tn, preferred_element_type=F32) for piece in (t1, t2, t3))

    for p in range(N_PLANES):
        ang = invf_ref[...] * pos_ref[0, p:p + 1, :].astype(F32)
        c_ref[0, p] = spread(jnp.cos(ang)) + one_ref[...]
        s_ref[0, p] = spread(jnp.sin(ang)) * sgn_ref[...]


def _rope_tables(positions):
    b, s = positions.shape
    sm = s // N_PLANES
    mt = min(sm, 128)
    plane_res = np.array([_residue_of_plane(p) for p in range(N_PLANES)])
    pos_planes = positions.reshape(b, sm, N_PLANES).transpose(0, 2, 1)[:, plane_res]
    half = ROPE_DIM // 2
    lane = np.arange(LANES) % HEAD_DIM
    inv_freq = ROPE_THETA ** (-jnp.arange(0, ROPE_DIM, 2, dtype=F32) / ROPE_DIM)
    rotary = lane < ROPE_DIM
    expand = jnp.asarray((np.arange(half)[:, None] == lane[None, :] % half) & rotary[None, :], BF16)
    one = jnp.asarray(~rotary, F32)[None, :]
    sgn = jnp.asarray(np.where(lane < half, -1.0, 1.0), F32)[None, :]
    row = pl.BlockSpec((1, LANES), lambda i, j: (0, 0))
    out = pl.BlockSpec((1, N_PLANES, mt, LANES), lambda i, j: (i, 0, j, 0))
    return pl.pallas_call(
        _rope_kernel,
        grid=(b, sm // mt),
        in_specs=[pl.BlockSpec((1, N_PLANES, mt), lambda i, j: (i, 0, j)),
                  pl.BlockSpec((half, 1), lambda i, j: (0, 0)), pl.BlockSpec((half, LANES), lambda i, j: (0, 0)),
                  row, row],
        out_specs=[out, out],
        out_shape=[jax.ShapeDtypeStruct((b, N_PLANES, sm, LANES), F32)] * 2,
        name="rope_tables",
    )(pos_planes, inv_freq[:, None], expand, one, sgn)


PERM_TOKENS = 256
PERM_ROWS = PERM_TOKENS // N_PLANES


def _plane_permutation():
    perm = np.zeros((PERM_TOKENS, PERM_TOKENS), np.float32)
    for p in range(N_PLANES):
        for ml in range(PERM_ROWS):
            perm[PERM_ROWS * p + ml, N_PLANES * ml + _residue_of_plane(p)] = 1.0
    return perm


def _inproj_kernel(x_ref, c_ref, s_ref, g_ref, perm_ref, wqkv_ref, wc_ref, q_ref, k_ref, v_ref, glu_ref):
    g = g_ref[...]
    tt = x_ref.shape[1]
    lane = lax.broadcasted_iota(jnp.int32, (1, LANES), 1) % HEAD_DIM
    first_half = lane < ROPE_DIM // 2

    def rotary(t, cos, sin):
        outs = []
        for j in range(ATTN_WIDTH // LANES):
            tj = t[:, j * LANES:(j + 1) * LANES]
            partner = jnp.where(first_half, pltpu.roll(tj, LANES - ROPE_DIM // 2, 1),
                                pltpu.roll(tj, ROPE_DIM // 2, 1))
            outs.append(tj * cos + partner * sin)
        return jnp.concatenate(outs, axis=1)

    un = _rms(x_ref[0], g).astype(BF16)
    pc = jnp.dot(un, wc_ref[...], preferred_element_type=F32)
    glu_ref[0] = pc[:, :CONV_CHANNELS] * jax.nn.sigmoid(pc[:, CONV_CHANNELS:])

    for sub in range(tt // PERM_TOKENS):
        rows = slice(sub * PERM_ROWS, (sub + 1) * PERM_ROWS)
        u = jnp.dot(perm_ref[...], un[sub * PERM_TOKENS:(sub + 1) * PERM_TOKENS],
                    preferred_element_type=F32).astype(BF16)
        cos = jnp.concatenate([c_ref[0, p, rows, :] for p in range(N_PLANES)], axis=0)
        sin = jnp.concatenate([s_ref[0, p, rows, :] for p in range(N_PLANES)], axis=0)
        proj = jnp.dot(u, wqkv_ref[...], preferred_element_type=F32)
        q = (rotary(proj[:, :ATTN_WIDTH], cos, sin) * (HEAD_DIM ** -0.5)).astype(BF16)
        k = rotary(proj[:, ATTN_WIDTH:2 * ATTN_WIDTH], cos, sin).astype(BF16)
        v = proj[:, 2 * ATTN_WIDTH:].astype(BF16)
        for p in range(N_PLANES):
            chunk = slice(p * PERM_ROWS, (p + 1) * PERM_ROWS)
            q_ref[0, p, rows, :] = q[chunk]
            k_ref[0, p, rows, :] = k[chunk]
            v_ref[0, p, rows, :] = v[chunk]


def _input_projection(x, cos_t, sin_t, g_mix_pre, w_in):
    b, s, d = x.shape
    tt = 1024
    mc = tt // N_PLANES
    wqkv = w_in[:, :3 * ATTN_WIDTH].astype(BF16)
    wc = w_in[:, 3 * ATTN_WIDTH:].astype(BF16)
    perm = jnp.asarray(_plane_permutation(), BF16)
    plane = lambda w: pl.BlockSpec((1, N_PLANES, mc, w), lambda i, j: (i, 0, j, 0))
    plane_shape = jax.ShapeDtypeStruct((b, N_PLANES, s // N_PLANES, ATTN_WIDTH), BF16)
    tok = lambda w: pl.BlockSpec((1, tt, w), lambda i, j: (i, j, 0))
    const = lambda shape: pl.BlockSpec(shape, lambda i, j: (0, 0))
    return pl.pallas_call(
        _inproj_kernel,
        grid=(b, s // tt),
        in_specs=[tok(d), plane(LANES), plane(LANES), const((1, d)), const(perm.shape),
                  const(wqkv.shape), const(wc.shape)],
        out_specs=[plane(ATTN_WIDTH), plane(ATTN_WIDTH), plane(ATTN_WIDTH), tok(CONV_CHANNELS)],
        out_shape=[plane_shape, plane_shape, plane_shape,
                   jax.ShapeDtypeStruct((b, s, CONV_CHANNELS), F32)],
        compiler_params=pltpu.CompilerParams(vmem_limit_bytes=VMEM_LIMIT),
        name="input_projection",
    )(x, cos_t, sin_t, g_mix_pre[None, :], perm, wqkv, wc)


def _attention_biases():
    band = lambda j: np.where((j >= 0) & (j <= WINDOW), 0.0, NEG).astype(np.float32)
    cols = lambda m: np.where(m, NEG, 0.0).astype(np.float32)
    twice = lambda a: np.concatenate([a, a], axis=0)
    mq = np.arange(WINDOW)[:, None]
    kj = np.arange(2 * WINDOW)[None, :]
    j16 = mq + WINDOW - kj
    prev16 = kj < WINDOW
    row = np.arange(128)[:, None]
    col = np.arange(256)[None, :]
    j4 = 4 * (row % 32 - (col % 64 - 32)) + row // 32 - col // 64
    prev4 = col % 64 < 32
    row = np.arange(256)[:, None]
    col = np.arange(512)[None, :]
    res = np.vectorize(_residue_of_plane)
    j1 = 16 * (row % 16 - (col % 32 - 16)) + res(row // 16) - res(col // 32)
    prev1 = col % 32 < 16
    b1 = np.stack([twice(band(j1[:128])), twice(band(j1[128:]))])
    return [jnp.asarray(a) for a in (twice(band(j16)), twice(band(j4)), b1, cols(prev16), cols(prev4), cols(prev1))]


UNROLL = 16


def _attn_kernel(q_ref, kc_ref, kp_ref, vc_ref, vp_ref, b16_ref, b4_ref, b1_ref, p16_ref, p4_ref, p1_ref,
                 o_ref, m_scr, l_scr, a_scr, bias16_scr):
    no_prev = (pl.program_id(1) == 0).astype(F32)
    head0 = lax.broadcasted_iota(jnp.int32, (1, LANES), 1) < HEAD_DIM
    bias16_scr[...] = b16_ref[...] + no_prev * p16_ref[...]

    def tile(qt, kt, vt, bias):
        n = qt.shape[0]
        zero = jnp.zeros_like(qt)
        q2 = jnp.concatenate([jnp.where(head0, qt, zero), jnp.where(head0, zero, qt)], axis=0)
        s = lax.dot_general(q2, kt, (((1,), (1,)), ((), ())), preferred_element_type=F32) + bias
        m = jnp.max(s, axis=-1, keepdims=True)
        e = jnp.exp(s - m).astype(BF16)
        va = jnp.concatenate([vt, jnp.ones_like(vt)], axis=1)
        o = jnp.dot(e, va, preferred_element_type=F32)
        pick = lambda top, bot: jnp.where(head0, top, bot)
        mm = pick(jnp.broadcast_to(m[:n], (n, LANES)), jnp.broadcast_to(m[n:], (n, LANES)))
        return mm, pick(o[:n, LANES:], o[n:, LANES:]), pick(o[:n, :LANES], o[n:, :LANES])

    def put(branch, start, size, stats, off):
        for scr, val in zip((m_scr, l_scr, a_scr), stats):
            scr[branch, pl.ds(start, size), :] = val[off:off + size]

    def body16(i, carry):
        for p in [UNROLL * i + u for u in range(UNROLL)]:
            kt = jnp.concatenate([kp_ref[0, p], kc_ref[0, p]], axis=0)
            vt = jnp.concatenate([vp_ref[0, p], vc_ref[0, p]], axis=0)
            put(0, pl.multiple_of(p * WINDOW, WINDOW), WINDOW, tile(q_ref[0, p], kt, vt, bias16_scr[...]), 0)
        return carry
    lax.fori_loop(0, N_PLANES // UNROLL, body16, 0)

    def body4(g, carry):
        for c, i in [(4 * g + cc, ii) for cc in range(4) for ii in range(4)]:
            qt = jnp.concatenate([q_ref[0, 4 * c + a, 32 * i:32 * i + 32, :] for a in range(4)], axis=0)
            if i == 0:
                ks = [x for a in range(4) for x in (kp_ref[0, 4 * c + a, 96:128, :], kc_ref[0, 4 * c + a, 0:32, :])]
                vs = [x for a in range(4) for x in (vp_ref[0, 4 * c + a, 96:128, :], vc_ref[0, 4 * c + a, 0:32, :])]
                bias = b4_ref[...] + no_prev * p4_ref[...]
            else:
                ks = [kc_ref[0, 4 * c + a, 32 * i - 32:32 * i + 32, :] for a in range(4)]
                vs = [vc_ref[0, 4 * c + a, 32 * i - 32:32 * i + 32, :] for a in range(4)]
                bias = b4_ref[...]
            stats = tile(qt, jnp.concatenate(ks, axis=0), jnp.concatenate(vs, axis=0), bias)
            for a in range(4):
                put(1, pl.multiple_of((4 * c + a) * WINDOW + 32 * i, 32), 32, stats, 32 * a)
        return carry
    lax.fori_loop(0, 1, body4, 0)

    def tile1(i, first):
        rq = pl.ds(pl.multiple_of(16 * i, 16), 16)
        if first:
            ks = [x for p in range(N_PLANES) for x in (kp_ref[0, p, 112:128, :], kc_ref[0, p, 0:16, :])]
            vs = [x for p in range(N_PLANES) for x in (vp_ref[0, p, 112:128, :], vc_ref[0, p, 0:16, :])]
        else:
            rk = pl.ds(pl.multiple_of(16 * i - 16, 16), 32)
            ks = [kc_ref[0, p, rk, :] for p in range(N_PLANES)]
            vs = [vc_ref[0, p, rk, :] for p in range(N_PLANES)]
        kt = jnp.concatenate(ks, axis=0)
        vt = jnp.concatenate(vs, axis=0)
        for half in range(2):
            planes = range(8 * half, 8 * half + 8)
            qt = jnp.concatenate([q_ref[0, p, rq, :] for p in planes], axis=0)
            bias = b1_ref[half] + no_prev * p1_ref[...] if first else b1_ref[half]
            stats = tile(qt, kt, vt, bias)
            for p in planes:
                put(2, pl.multiple_of(p * WINDOW + 16 * i, 16), 16, stats, 16 * (p - 8 * half))

    tile1(0, True)
    for i in range(1, 4):
        tile1(i, False)

    def body1(g, carry):
        for u in range(4):
            tile1(4 * g + u, False)
        return carry
    lax.fori_loop(1, WINDOW // 64, body1, 0)

    def combine(i, carry):
        for p in (2 * i, 2 * i + 1):
            rows = pl.ds(pl.multiple_of(p * WINDOW, WINDOW), WINDOW)
            ms = [m_scr[b, rows, :] for b in range(3)]
            mx = jnp.maximum(jnp.maximum(ms[0], ms[1]), ms[2])
            ws = [jnp.exp(m - mx) for m in ms]
            den = ws[0] * l_scr[0, rows, :] + ws[1] * l_scr[1, rows, :] + ws[2] * l_scr[2, rows, :]
            num = ws[0] * a_scr[0, rows, :] + ws[1] * a_scr[1, rows, :] + ws[2] * a_scr[2, rows, :]
            o_ref[0, p] = (num / den).astype(BF16)
        return carry
    lax.fori_loop(0, N_PLANES // 2, combine, 0)


def _dilated_attention(q, k, v):
    b, _, sm, _ = q.shape
    n_span = sm // WINDOW
    cur = pl.BlockSpec((1, N_PLANES, WINDOW, LANES), lambda i, j, h: (i, 0, j, h))
    prev = pl.BlockSpec((1, N_PLANES, WINDOW, LANES), lambda i, j, h: (i, 0, jnp.maximum(j - 1, 0), h))
    biases = _attention_biases()
    bias_specs = [pl.BlockSpec(a.shape, lambda i, j, h, nd=a.ndim: (0,) * nd) for a in biases]
    stats = pltpu.VMEM((3, SPAN, LANES), F32)
    return pl.pallas_call(
        _attn_kernel,
        grid=(b, n_span, ATTN_WIDTH // LANES),
        in_specs=[cur, cur, prev, cur, prev] + bias_specs,
        out_specs=cur,
        out_shape=jax.ShapeDtypeStruct(q.shape, BF16),
        scratch_shapes=[stats, stats, stats, pltpu.VMEM((2 * WINDOW, 2 * WINDOW), F32)],
        compiler_params=pltpu.CompilerParams(vmem_limit_bytes=VMEM_LIMIT),
        name="dilated_attention",
    )(q, k, k, v, v, *biases)


CONV_HALO = 32


def _conv_kernel(cur_ref, prev_ref, w_ref, b_ref, g_ref, bl_ref, o_ref, scr, *, chunk):
    tt = cur_ref.shape[1]
    has_prev = pl.program_id(1) > 0
    scr[0, 0:CONV_HALO, :] = jnp.where(has_prev, prev_ref[0], 0.0)
    scr[0, CONV_HALO:CONV_HALO + tt, :] = cur_ref[0]
    aligned_rows = tt + CONV_HALO - SUBLANES
    for s in range(1, SUBLANES):
        scr[s, 0:aligned_rows, :] = scr[0, s:s + aligned_rows, :]
    lead = CONV_HALO - (CONV_WIDTH - 1)
    for c0 in range(0, tt, chunk):
        acc = jnp.zeros((chunk, CONV_CHANNELS), F32)
        for j in range(CONV_WIDTH):
            s, a = (lead + j) % SUBLANES, (lead + j) // SUBLANES * SUBLANES
            acc = acc + w_ref[j:j + 1, :] * scr[s, c0 + a:c0 + a + chunk, :]
        y = acc + b_ref[...]
        mu = jnp.mean(y, axis=-1, keepdims=True)
        var = jnp.mean(jnp.square(y - mu), axis=-1, keepdims=True)
        yn = (y - mu) * lax.rsqrt(var + NORM_EPS) * g_ref[...] + bl_ref[...]
        o_ref[0, c0:c0 + chunk, :] = (yn * jax.nn.sigmoid(yn)).astype(BF16)


def _conformer_conv(glu, w_dw, b_dw, g_ln, b_ln):
    b, s, c = glu.shape
    tt = 512
    row = pl.BlockSpec((1, c), lambda i, j: (0, 0))
    return pl.pallas_call(
        functools.partial(_conv_kernel, chunk=128),
        grid=(b, s // tt),
        in_specs=[pl.BlockSpec((1, tt, c), lambda i, j: (i, j, 0)),
                  pl.BlockSpec((1, CONV_HALO, c), lambda i, j: (i, jnp.maximum(j * (tt // CONV_HALO) - 1, 0), 0)),
                  pl.BlockSpec((CONV_WIDTH, c), lambda i, j: (0, 0)), row, row, row],
        out_specs=pl.BlockSpec((1, tt, c), lambda i, j: (i, j, 0)),
        out_shape=jax.ShapeDtypeStruct((b, s, c), BF16),
        scratch_shapes=[pltpu.VMEM((SUBLANES, CONV_HALO + tt, c), F32)],
        name="conformer_conv",
    )(glu, glu, w_dw[:, 0, :], b_dw[None, :], g_ln[None, :], b_ln[None, :])


def _outproj_kernel(attn_ref, conv_ref, x_ref, permt_ref, woa_ref, woc_ref, gpost_ref, gffn_ref, wr_ref, br_ref,
                    h_ref, u_ref, idx_ref, gate_ref):
    tt = x_ref.shape[1]
    nat = []
    for sub in range(tt // PERM_TOKENS):
        rows = slice(sub * PERM_ROWS, (sub + 1) * PERM_ROWS)
        a = jnp.concatenate([attn_ref[0, p, rows, :] for p in range(N_PLANES)], axis=0)
        nat.append(jnp.dot(permt_ref[...], a, preferred_element_type=F32).astype(BF16))
    mix = (jnp.dot(jnp.concatenate(nat, axis=0), woa_ref[...], preferred_element_type=F32)
           + jnp.dot(conv_ref[0], woc_ref[...], preferred_element_type=F32))
    h = x_ref[0] + _rms(mix, gpost_ref[...])
    h_ref[0] = h
    u = _rms(h, gffn_ref[...])
    half = u.shape[1] // 2
    ub = u.astype(BF16).astype(F32)
    u_ref[...] = ((lax.bitcast_convert_type(ub[:, :half], jnp.uint32) >> 16)
                  | (lax.bitcast_convert_type(ub[:, half:], jnp.uint32) & jnp.uint32(0xFFFF0000)))
    u_hi = u.astype(BF16)
    u_lo = (u - u_hi.astype(F32)).astype(BF16)
    nt = (((1,), (1,)), ((), ()))
    by_hi = lax.dot_general(wr_ref[...], u_hi, nt, preferred_element_type=F32)
    by_lo = lax.dot_general(wr_ref[:N_EXPERTS], u_lo, nt, preferred_element_type=F32)
    logits = by_hi[:N_EXPERTS] + (by_hi[N_EXPERTS:] + by_lo) + br_ref[...]
    rows = lax.broadcasted_iota(jnp.int32, logits.shape, 0)
    vals = logits
    tops, idxs = [], []
    for _ in range(TOP_K):
        mx = jnp.max(vals, axis=0, keepdims=True)
        ix = jnp.min(jnp.where(vals == mx, rows, N_EXPERTS), axis=0, keepdims=True)
        tops.append(mx)
        idxs.append(ix)
        vals = jnp.where(rows == ix, -jnp.inf, vals)
    ex = [jnp.exp(t - tops[0]) for t in tops]
    den = ex[0] + ex[1] + ex[2] + ex[3]
    idx_ref[...] = jnp.concatenate(idxs, axis=0)
    gate_ref[...] = jnp.concatenate([e / den for e in ex] + [jnp.zeros((8 - TOP_K, tt), F32)], axis=0)


def _output_projection(attn, conv, x, w_out, g_mix_post, g_ffn_pre, w_router, b_router):
    b, s, d = x.shape
    tt = 1024
    mc = tt // N_PLANES
    n_t = s // tt
    woa = w_out[:ATTN_WIDTH].astype(BF16)
    woc = w_out[ATTN_WIDTH:].astype(BF16)
    wr_hi = w_router.T.astype(BF16)
    wr_split = jnp.concatenate([wr_hi, (w_router.T - wr_hi.astype(F32)).astype(BF16)], axis=0)
    permt =jnp.asarray(_plane_permutation().T, BF16)
    const = lambda shape: pl.BlockSpec(shape, lambda i, j: (0, 0))
    flat = lambda w: pl.BlockSpec((tt, w), lambda i, j: (i * n_t + j, 0))
    lanes = lambda r: pl.BlockSpec((r, tt), lambda i, j: (0, i * n_t + j))
    return pl.pallas_call(
        _outproj_kernel,
        grid=(b, n_t),
        in_specs=[pl.BlockSpec((1, N_PLANES, mc, ATTN_WIDTH), lambda i, j: (i, 0, j, 0)),
                  pl.BlockSpec((1, tt, CONV_CHANNELS), lambda i, j: (i, j, 0)),
                  pl.BlockSpec((1, tt, d), lambda i, j: (i, j, 0)),
                  const(permt.shape), const(woa.shape), const(woc.shape), const((1, d)), const((1, d)),
                  const((2 * N_EXPERTS, d)), const((N_EXPERTS, 1))],
        out_specs=[pl.BlockSpec((1, tt, d), lambda i, j: (i, j, 0)), flat(d // 2), lanes(TOP_K), lanes(8)],
        out_shape=[jax.ShapeDtypeStruct((b, s, d), F32),
                   jax.ShapeDtypeStruct((b * s, d // 2), jnp.uint32),
                   jax.ShapeDtypeStruct((TOP_K, b * s), jnp.int32),
                   jax.ShapeDtypeStruct((8, b * s), F32)],
        compiler_params=pltpu.CompilerParams(vmem_limit_bytes=VMEM_LIMIT),
        name="output_projection_router",
    )(attn, conv, x, permt, woa, woc, g_mix_post[None, :], g_ffn_pre[None, :], wr_split, b_router[:, None])


def _route_kernel(idx_ref, dest_ref, pend_ref, blk_ref, carry, pstart):
    phase = pl.program_id(0)
    step = pl.program_id(1)
    tt = idx_ref.shape[1]
    rows = lax.broadcasted_iota(jnp.int32, (N_EXPERTS, tt), 0)
    hot = [rows == idx_ref[k:k + 1, :] for k in range(TOP_K)]
    memb = sum(h.astype(F32) for h in hot)

    @pl.when((phase == 0) & (step == 0))
    def _():
        carry[...] = jnp.zeros_like(carry)

    @pl.when((phase == 1) & (step == 0))
    def _():
        counts = carry[...]
        padded = jnp.floor((counts + (EXPERT_ROWS - 1)) * (1.0 / EXPERT_ROWS)) * EXPERT_ROWS
        tri = (lax.broadcasted_iota(jnp.int32, (N_EXPERTS, N_EXPERTS), 1)
               <= lax.broadcasted_iota(jnp.int32, (N_EXPERTS, N_EXPERTS), 0)).astype(F32)
        pend = jnp.dot(tri, padded, precision=lax.Precision.HIGHEST, preferred_element_type=F32)
        pstart[...] = pend - padded
        pend_ref[...] = pend.astype(jnp.int32)
        starts = lax.broadcasted_iota(jnp.int32, (N_EXPERTS, blk_ref.shape[1]), 1) * EXPERT_ROWS
        ended = (pend.astype(jnp.int32) <= starts).astype(jnp.int32)
        blk_ref[...] = jnp.minimum(jnp.sum(ended, axis=0, keepdims=True), N_EXPERTS - 1)
        carry[...] = jnp.zeros_like(carry)

    @pl.when(phase == 1)
    def _():
        earlier = (lax.broadcasted_iota(jnp.int32, (tt, tt), 0)
                   < lax.broadcasted_iota(jnp.int32, (tt, tt), 1)).astype(BF16)
        row = jnp.dot(memb.astype(BF16), earlier, preferred_element_type=F32) + (carry[...] + pstart[...])
        dest_ref[...] = jnp.concatenate(
            [jnp.sum(jnp.where(h, row, 0.0), axis=0, keepdims=True) for h in hot], axis=0).astype(jnp.int32)

    carry[...] = carry[...] + jnp.sum(memb, axis=1, keepdims=True)


def _routing(idx, n_blk):
    _, t = idx.shape
    tt = 1024
    blk_lanes = -(-n_blk // LANES) * LANES
    return pl.pallas_call(
        _route_kernel,
        grid=(2, t // tt),
        in_specs=[pl.BlockSpec((TOP_K, tt), lambda ph, i: (0, i))],
        out_specs=[pl.BlockSpec((TOP_K, tt), lambda ph, i: (0, i * ph)),
                   pl.BlockSpec((N_EXPERTS, 1), lambda ph, i: (0, 0)),
                   pl.BlockSpec((1, blk_lanes), lambda ph, i: (0, 0))],
        out_shape=[jax.ShapeDtypeStruct((TOP_K, t), jnp.int32), jax.ShapeDtypeStruct((N_EXPERTS, 1), jnp.int32),
                   jax.ShapeDtypeStruct((1, blk_lanes), jnp.int32)],
        scratch_shapes=[pltpu.VMEM((N_EXPERTS, 1), F32), pltpu.VMEM((N_EXPERTS, 1), F32)],
        compiler_params=pltpu.CompilerParams(dimension_semantics=("arbitrary", "arbitrary")),
        name="moe_routing",
    )(idx)


SC_SCAN = 16384
SC_UNROLL = 8


def _sc_token_of_row(dest_flat, cap, n_tok):
    from jax.experimental.pallas import tpu_sc as plsc
    assert n_tok & (n_tok - 1) == 0
    info = plsc.get_sparse_core_info()
    n_core, n_sub, lanes = info.num_cores, info.num_subcores, info.num_lanes
    n_src = dest_flat.shape[0]
    per_w = cap // (n_core * n_sub)
    assert per_w * n_core * n_sub == cap and per_w % lanes == 0 and n_src % SC_SCAN == 0
    mesh = plsc.VectorSubcoreMesh(core_axis_name="c", subcore_axis_name="s")

    def body(dest_hbm, out_hbm, src_v, out_v):
        lo = (lax.axis_index("s") * n_core + lax.axis_index("c")) * per_w
        lane = lax.iota(jnp.int32, lanes)

        @pl.loop(0, per_w // lanes)
        def _(i):
            out_v[pl.ds(pl.multiple_of(i * lanes, lanes), lanes)] = (lo + i * lanes + lane) & (n_tok - 1)

        @pl.loop(0, n_src // SC_SCAN)
        def _(c):
            pltpu.sync_copy(dest_hbm.at[pl.ds(pl.multiple_of(c * SC_SCAN, 8), SC_SCAN)], src_v)

            @pl.loop(0, SC_SCAN // (lanes * SC_UNROLL))
            def _(g):
                for u in range(SC_UNROLL):
                    at = (g * SC_UNROLL + u) * lanes
                    local = src_v[pl.ds(pl.multiple_of(at, lanes), lanes)] - lo
                    mine = lax.bitcast_convert_type(local, jnp.uint32) < jnp.uint32(per_w)
                    tok = (c * SC_SCAN + at + lane) & (n_tok - 1)
                    plsc.store_scatter(out_v, [local], tok, mask=mine)
        pltpu.sync_copy(out_v, out_hbm.at[pl.ds(pl.multiple_of(lo, 8), per_w)])

    return pl.kernel(body, mesh=mesh, out_type=jax.ShapeDtypeStruct((cap,), jnp.int32),
                     scratch_types=[pltpu.VMEM((SC_SCAN,), jnp.int32), pltpu.VMEM((per_w,), jnp.int32)],
                     compiler_params=pltpu.CompilerParams(needs_layout_passes=False),
                     name="moe_sc_token_of_row")(dest_flat)


def _expert_kernel(blk_e_ref, nused_ref, xb_ref, wgu_ref, bgu_ref, wd_ref, bd_ref, *rest):
    yb_ref, done_ref, wgu_b, wd_b = rest[-4:]
    j = pl.program_id(0)
    done_ref[...] = jnp.zeros_like(done_ref)
    nused = nused_ref[0]
    d_ff = wd_ref.shape[1]

    new_expert = (j == 0) | (blk_e_ref[j] != blk_e_ref[jnp.maximum(j - 1, 0)])

    @pl.when((j < nused) & new_expert)
    def _():
        def cast(src, dst):
            def body(i, carry):
                rows = pl.ds(pl.multiple_of(i * LANES, LANES), LANES)
                dst[rows, :] = src[0, rows, :].astype(BF16)
                return carry
            lax.fori_loop(0, dst.shape[0] // LANES, body, 0)
        cast(wgu_ref, wgu_b)
        cast(wd_ref, wd_b)

    @pl.when(j < nused)
    def _():
        word = xb_ref[...]
        half = word.shape[1]
        lo = lax.bitcast_convert_type(word << 16, F32).astype(BF16)
        hi = lax.bitcast_convert_type(word & jnp.uint32(0xFFFF0000), F32).astype(BF16)
        gu = (jnp.dot(lo, wgu_b[:half, :], preferred_element_type=F32)
              + jnp.dot(hi, wgu_b[half:, :], preferred_element_type=F32) + bgu_ref[0])
        gate = jnp.minimum(gu[:, :d_ff], SWIGLU_LIMIT)
        up = jnp.clip(gu[:, d_ff:], -SWIGLU_LIMIT, SWIGLU_LIMIT)
        hid = (up + 1.0) * (gate * jax.nn.sigmoid(gate * SWIGLU_ALPHA))
        yb_ref[...] = jnp.dot(hid.astype(BF16), wd_b[...], preferred_element_type=F32) + bd_ref[0]

    @pl.when(j >= nused)
    def _():
        yb_ref[...] = jnp.zeros_like(yb_ref)


def _experts(xb, chunk, n_chunk, yb_prev, blk_e, nused, w_gate_up, b_gate_up, w_down, b_down):
    rows, d_in = xb.shape
    e, d, ff2 = w_gate_up.shape
    d_ff = w_down.shape[1]
    n_blk = rows // EXPERT_ROWS
    live = lambda j, nu: jnp.maximum(jnp.minimum(j, nu[0] - 1), 0)
    expert = lambda j, be, nu: (be[live(j, nu)], 0, 0)
    in_specs = [pl.BlockSpec((EXPERT_ROWS, d_in), lambda j, be, nu: (live(j, nu), 0)),
                pl.BlockSpec((1, d, ff2), expert), pl.BlockSpec((1, 1, ff2), expert),
                pl.BlockSpec((1, d_ff, d), expert), pl.BlockSpec((1, 1, d), expert)]
    args = [blk_e, nused, xb, w_gate_up, b_gate_up[:, None, :], w_down, b_down[:, None, :]]
    aliases = {}
    if yb_prev is not None:
        in_specs.append(pl.BlockSpec(memory_space=pl.ANY))
        args.append(yb_prev)
        aliases = {len(args) - 1: 0}
    grid_spec = pltpu.PrefetchScalarGridSpec(
        num_scalar_prefetch=2,
        grid=(n_blk,),
        in_specs=in_specs,
        out_specs=[pl.BlockSpec((EXPERT_ROWS, d), lambda j, be, nu: (chunk * n_blk + j, 0)),
                   pl.BlockSpec((SUBLANES, LANES), lambda j, be, nu: (0, 0))],
        scratch_shapes=[pltpu.VMEM((d, ff2), BF16), pltpu.VMEM((d_ff, d), BF16)],
    )
    return pl.pallas_call(
        _expert_kernel,
        grid_spec=grid_spec,
        out_shape=[jax.ShapeDtypeStruct((n_chunk * rows, d), F32), jax.ShapeDtypeStruct((SUBLANES, LANES), F32)],
        input_output_aliases=aliases,
        compiler_params=pltpu.CompilerParams(dimension_semantics=("arbitrary",), vmem_limit_bytes=VMEM_LIMIT),
        name="moe_experts",
    )(*args)


SC_BUFFER_BYTES = 128 * 1024
MOE_CHUNKS = 8
MOE_GROUPS = 4


def _sc_gather_rows(table, idx, part, n_part, after=None):
    from jax.experimental.pallas import tpu_sc as plsc
    n_rows = idx.shape[0] // n_part
    d = table.shape[1]
    info = plsc.get_sparse_core_info()
    n_core, n_sub = info.num_cores, info.num_subcores
    per_w = n_rows // (n_core * n_sub)
    fit = SC_BUFFER_BYTES // (d * table.dtype.itemsize)
    g_rows = max(r for r in (8, 16, 32, 64, 128) if r <= fit and per_w % r == 0)
    n_chunk = per_w // g_rows
    assert per_w * n_core * n_sub == n_rows
    mesh = plsc.VectorSubcoreMesh(core_axis_name="c", subcore_axis_name="s")

    def body(table_hbm, idx_hbm, *rest):
        out_hbm, idx_v, buf0, buf1, sem0, sem1 = rest[-6:]
        base = (lax.axis_index("s") * n_core + lax.axis_index("c")) * per_w
        pltpu.sync_copy(idx_hbm.at[pl.ds(pl.multiple_of(part * n_rows + base, 8), per_w)], idx_v)

        def gather(chunk, buf, sem):
            rows = idx_v.at[pl.ds(pl.multiple_of(chunk * g_rows, 8), g_rows)]
            return pltpu.make_async_copy(table_hbm.at[rows], buf, sem)

        def write(chunk, buf):
            pltpu.sync_copy(buf, out_hbm.at[pl.ds(pl.multiple_of(base + chunk * g_rows, 8), g_rows)])

        gather(0, buf0, sem0).start()

        @pl.loop(0, n_chunk // 2)
        def _(i):
            gather(2 * i + 1, buf1, sem1).start()
            gather(2 * i, buf0, sem0).wait()
            write(2 * i, buf0)

            @pl.when(2 * i + 2 < n_chunk)
            def _():
                gather(2 * i + 2, buf0, sem0).start()
            gather(2 * i + 1, buf1, sem1).wait()
            write(2 * i + 1, buf1)

        if n_chunk % 2:
            gather(n_chunk - 1, buf0, sem0).wait()
            write(n_chunk - 1, buf0)

    return pl.kernel(
        body, mesh=mesh, out_type=jax.ShapeDtypeStruct((n_rows, d), table.dtype),
        scratch_types=[pltpu.VMEM((per_w,), jnp.int32), pltpu.VMEM((g_rows, d), table.dtype),
                       pltpu.VMEM((g_rows, d), table.dtype), pltpu.SemaphoreType.DMA, pltpu.SemaphoreType.DMA],
        name="moe_sc_gather",
    )(table, idx, *([] if after is None else [after]))


def _sc_weighted_rows(table, idx, gates, group, n_group):
    from jax.experimental.pallas import tpu_sc as plsc
    d = table.shape[1]
    info = plsc.get_sparse_core_info()
    n_core, n_sub, lanes = info.num_cores, info.num_subcores, info.num_lanes
    n_all = idx.shape[0] // TOP_K
    n_tok = n_all // n_group
    per_w = n_tok // (n_core * n_sub)
    w = 8
    n_chunk = per_w // w
    assert per_w * n_core * n_sub == n_tok and n_chunk * w == per_w and n_chunk % 2 == 0
    mesh = plsc.VectorSubcoreMesh(core_axis_name="c", subcore_axis_name="s")

    def body(table_hbm, idx_hbm, g_hbm, out_hbm, *scr):
        idx_v, g_v, rb, ob = scr[0:4], scr[4:8], (scr[8:12], scr[12:16]), scr[16:18]
        sems, wsems = scr[18:20], scr[20:22]
        base = (lax.axis_index("s") * n_core + lax.axis_index("c")) * per_w
        for k in range(TOP_K):
            off = pl.multiple_of(k * n_all + group * n_tok + base, 8)
            pltpu.sync_copy(idx_hbm.at[pl.ds(off, per_w)], idx_v[k])
            pltpu.sync_copy(g_hbm.at[pl.ds(off, per_w)], g_v[k])

        def gathers(chunk, slot):
            rows = pl.ds(pl.multiple_of(chunk * w, 8), w)
            return [pltpu.make_async_copy(table_hbm.at[idx_v[k].at[rows]], rb[slot][k], sems[slot])
                    for k in range(TOP_K)]

        def write(chunk, slot):
            return pltpu.make_async_copy(ob[slot], out_hbm.at[pl.ds(pl.multiple_of(base + chunk * w, 8), w)],
                                         wsems[slot])

        def combine(chunk, slot):
            @pl.loop(0, w)
            def _(i):
                token = jnp.full((lanes,), chunk * w + i, jnp.int32)
                gs = [plsc.load_gather(g_v[k], [token]) for k in range(TOP_K)]
                for j in range(d // lanes):
                    cols = pl.ds(j * lanes, lanes)
                    acc = gs[0] * rb[slot][0][i, cols]
                    for k in range(1, TOP_K):
                        acc = acc + gs[k] * rb[slot][k][i, cols]
                    ob[slot][i, cols] = acc
            write(chunk, slot).start()

        for c in gathers(0, 0):
            c.start()

        @pl.loop(0, n_chunk // 2)
        def _(i):
            for c in gathers(2 * i + 1, 1):
                c.start()
            for c in gathers(2 * i, 0):
                c.wait()

            @pl.when(i > 0)
            def _():
                write(2 * i - 2, 0).wait()
            combine(2 * i, 0)

            @pl.when(2 * i + 2 < n_chunk)
            def _():
                for c in gathers(2 * i + 2, 0):
                    c.start()
            for c in gathers(2 * i + 1, 1):
                c.wait()

            @pl.when(i > 0)
            def _():
                write(2 * i - 1, 1).wait()
            combine(2 * i + 1, 1)

        write(n_chunk - 2, 0).wait()
        write(n_chunk - 1, 1).wait()

    scratch = ([pltpu.VMEM((per_w,), jnp.int32)] * TOP_K + [pltpu.VMEM((per_w,), F32)] * TOP_K
               + [pltpu.VMEM((w, d), F32)] * (2 * TOP_K + 2) + [pltpu.SemaphoreType.DMA] * 4)
    return pl.kernel(body, mesh=mesh, out_type=jax.ShapeDtypeStruct((n_tok, d), F32), scratch_types=scratch,
                     compiler_params=pltpu.CompilerParams(needs_layout_passes=False),
                     name="moe_sc_combine")(table, idx, gates)


def _combine_kernel(h_ref, p_ref, y_ref, gffn_ref, wple_ref, wpg_ref, gple_ref, o_ref):
    h2 = h_ref[...] + _rms(y_ref[...], gffn_ref[...])
    ple = (jnp.dot(p_ref[...].astype(BF16), wple_ref[...], preferred_element_type=F32)
           * jax.nn.sigmoid(jnp.dot(h2.astype(BF16), wpg_ref[...], preferred_element_type=F32)))
    o_ref[...] = h2 + _rms(ple, gple_ref[...])


def _combine_kernel_aliased(h_ref, p_ref, y_ref, gffn_ref, wple_ref, wpg_ref, gple_ref, prev_ref, o_ref):
    del prev_ref
    _combine_kernel(h_ref, p_ref, y_ref, gffn_ref, wple_ref, wpg_ref, gple_ref, o_ref)


def _combine(group, n_group, out_prev, h, p, y, g_ffn_post, w_ple_bf16, w_ple_gate_bf16, g_ple_post):
    t, d = h.shape
    tt = 256
    n_tg = t // n_group // tt
    const = lambda shape: pl.BlockSpec(shape, lambda i: (0, 0))
    tok = lambda w: pl.BlockSpec((tt, w), lambda i: (group * n_tg + i, 0))
    in_specs = [tok(d), tok(p.shape[1]), pl.BlockSpec((tt, d), lambda i: (i, 0)),
                const((1, d)), const(w_ple_bf16.shape), const(w_ple_gate_bf16.shape), const((1, d))]
    args = [h, p, y, g_ffn_post[None, :], w_ple_bf16, w_ple_gate_bf16, g_ple_post[None, :]]
    body, aliases = _combine_kernel, {}
    if out_prev is not None:
        in_specs.append(pl.BlockSpec(memory_space=pl.ANY))
        args.append(out_prev)
        body, aliases = _combine_kernel_aliased, {len(args) - 1: 0}
    return pl.pallas_call(
        body,
        grid=(n_tg,),
        in_specs=in_specs,
        out_specs=tok(d),
        out_shape=jax.ShapeDtypeStruct((t, d), F32),
        input_output_aliases=aliases,
        compiler_params=pltpu.CompilerParams(vmem_limit_bytes=VMEM_LIMIT),
        name="moe_combine_ple",
    )(*args)


def _layer(h, p, positions, g_mix_pre, w_in, w_dw, b_dw, g_conv_ln, b_conv_ln, w_out, g_mix_post, g_ffn_pre,
           w_router, b_router, w_gate_up, b_gate_up, w_down, b_down, g_ffn_post, w_ple, w_ple_gate, g_ple_post):
    b, s, d = h.shape
    t = b * s
    cos_t, sin_t = _rope_tables(positions)
    q, k, v, glu = _input_projection(h, cos_t, sin_t, g_mix_pre, w_in)
    attn = _dilated_attention(q, k, v)
    conv = _conformer_conv(glu, w_dw, b_dw, g_conv_ln, b_conv_ln)
    h1, u_ffn, idx, gates = _output_projection(attn, conv, h, w_out, g_mix_post, g_ffn_pre, w_router, b_router)
    cap = t * TOP_K + N_EXPERTS * EXPERT_ROWS
    n_blk = cap // EXPERT_ROWS
    dest, pend, blk = _routing(idx, n_blk)
    blk_e = blk[0, :n_blk]
    nused = pend[N_EXPERTS - 1] // EXPERT_ROWS
    dest_flat = dest.reshape(TOP_K * t)
    tok_of_row = _sc_token_of_row(dest_flat, cap, t)
    blk_c = n_blk // MOE_CHUNKS
    yb = done = None
    xb_next = _sc_gather_rows(u_ffn, tok_of_row, 0, MOE_CHUNKS)
    for c in range(MOE_CHUNKS):
        xb_c = xb_next
        if c + 1 < MOE_CHUNKS:
            xb_next = _sc_gather_rows(u_ffn, tok_of_row, c + 1, MOE_CHUNKS, after=done)
        nused_c = jnp.clip(nused - c * blk_c, 0, blk_c)
        yb, done = _experts(xb_c, c, MOE_CHUNKS, yb, blk_e[c * blk_c:(c + 1) * blk_c], nused_c,
                            w_gate_up, b_gate_up, w_down, b_down)
    w_ple_bf16, w_ple_gate_bf16 = w_ple.astype(BF16), w_ple_gate.astype(BF16)
    h1_flat, p_flat = h1.reshape(t, d), p.reshape(t, -1)
    gate_flat = gates[:TOP_K].reshape(TOP_K * t)
    out = None
    for g in range(MOE_GROUPS):
        y_g = _sc_weighted_rows(yb, dest_flat, gate_flat, g, MOE_GROUPS)
        out = _combine(g, MOE_GROUPS, out, h1_flat, p_flat, y_g, g_ffn_post, w_ple_bf16, w_ple_gate_bf16,
                       g_ple_post)
    return out.reshape(b, s, d)


def kernel(x, p, positions, g_mix_pre, w_in, w_dw, b_dw, g_conv_ln, b_conv_ln, w_out, g_mix_post, g_ffn_pre,
           w_router, b_router, w_gate_up, b_gate_up, w_down, b_down, g_ffn_post, w_ple, w_ple_gate, g_ple_post):
    h = x
    for i in range(p.shape[0]):
        h = _layer(h, p[i], positions, g_mix_pre[i], w_in[i], w_dw[i], b_dw[i], g_conv_ln[i], b_conv_ln[i],
                   w_out[i], g_mix_post[i], g_ffn_pre[i], w_router[i], b_router[i], w_gate_up[i], b_gate_up[i],
                   w_down[i], b_down[i], g_ffn_post[i], w_ple[i], w_ple_gate[i], g_ple_post[i])
    return h
```

```python
import functools

import numpy as np
import jax
import jax.numpy as jnp
from jax import lax
from jax.experimental import pallas as pl
from jax.experimental.pallas import tpu as pltpu

F32 = jnp.float32
BF16 = jnp.bfloat16

HEAD_DIM = 64
N_HEADS = 12
ATTN_WIDTH = N_HEADS * HEAD_DIM
CONV_CHANNELS = 256
CONV_WIDTH = 31
ROPE_DIM = HEAD_DIM // 4
ROPE_THETA = 500000.0
N_EXPERTS = 32
TOP_K = 4
SWIGLU_LIMIT = 7.0
SWIGLU_ALPHA = 1.702
NORM_EPS = 1e-6
WINDOW = 128
N_PLANES = 16
SPAN = N_PLANES * WINDOW
LANES = 128
SUBLANES = 8
NEG = -1e30
EXPERT_ROWS = 512
VMEM_LIMIT = 56 * 1024 * 1024


def _residue_of_plane(p):
    return 4 * (p % 4) + p // 4


def _rms(xv, g):
    var = jnp.mean(xv * xv, axis=-1, keepdims=True)
    return xv * lax.rsqrt(var + NORM_EPS) * g


def _rope_kernel(pos_ref, invf_ref, expand_ref, one_ref, sgn_ref, c_ref, s_ref):
    tn = (((0,), (0,)), ((), ()))

    def spread(t):
        t1 = t.astype(BF16)
        r1 = t - t1.astype(F32)
        t2 = r1.astype(BF16)
        t3 = (r1 - t2.astype(F32)).astype(BF16)
        return sum(lax.dot_general(piece, expand_ref[...], tn, preferred_element_type=F32) for piece in (t1, t2, t3))

    for p in range(N_PLANES):
        ang = invf_ref[...] * pos_ref[0, p:p + 1, :].astype(F32)
        c_ref[0, p] = spread(jnp.cos(ang)) + one_ref[...]
        s_ref[0, p] = spread(jnp.sin(ang)) * sgn_ref[...]


def _rope_tables(positions):
    b, s = positions.shape
    sm = s // N_PLANES
    mt = min(sm, 128)
    plane_res = np.array([_residue_of_plane(p) for p in range(N_PLANES)])
    pos_planes = positions.reshape(b, sm, N_PLANES).transpose(0, 2, 1)[:, plane_res]
    half = ROPE_DIM // 2
    lane = np.arange(LANES) % HEAD_DIM
    inv_freq = ROPE_THETA ** (-jnp.arange(0, ROPE_DIM, 2, dtype=F32) / ROPE_DIM)
    rotary = lane < ROPE_DIM
    expand = jnp.asarray((np.arange(half)[:, None] == lane[None, :] % half) & rotary[None, :], BF16)
    one = jnp.asarray(~rotary, F32)[None, :]
    sgn = jnp.asarray(np.where(lane < half, -1.0, 1.0), F32)[None, :]
    row = pl.BlockSpec((1, LANES), lambda i, j: (0, 0))
    out = pl.BlockSpec((1, N_PLANES, mt, LANES), lambda i, j: (i, 0, j, 0))
    return pl.pallas_call(
        _rope_kernel,
        grid=(b, sm // mt),
        in_specs=[pl.BlockSpec((1, N_PLANES, mt), lambda i, j: (i, 0, j)),
                  pl.BlockSpec((half, 1), lambda i, j: (0, 0)), pl.BlockSpec((half, LANES), lambda i, j: (0, 0)),
                  row, row],
        out_specs=[out, out],
        out_shape=[jax.ShapeDtypeStruct((b, N_PLANES, sm, LANES), F32)] * 2,
        name="rope_tables",
    )(pos_planes, inv_freq[:, None], expand, one, sgn)


PERM_TOKENS = 256
PERM_ROWS = PERM_TOKENS // N_PLANES


def _plane_permutation():
    perm = np.zeros((PERM_TOKENS, PERM_TOKENS), np.float32)
    for p in range(N_PLANES):
        for ml in range(PERM_ROWS):
            perm[PERM_ROWS * p + ml, N_PLANES * ml + _residue_of_plane(p)] = 1.0
    return perm


def _inproj_kernel(x_ref, c_ref, s_ref, g_ref, perm_ref, wqkv_ref, wc_ref, q_ref, k_ref, v_ref, glu_ref):
    g = g_ref[...]
    tt = x_ref.shape[1]
    lane = lax.broadcasted_iota(jnp.int32, (1, LANES), 1) % HEAD_DIM
    first_half = lane < ROPE_DIM // 2

    def rotary(t, cos, sin):
        outs = []
        for j in range(ATTN_WIDTH // LANES):
            tj = t[:, j * LANES:(j + 1) * LANES]
            partner = jnp.where(first_half, pltpu.roll(tj, LANES - ROPE_DIM // 2, 1),
                                pltpu.roll(tj, ROPE_DIM // 2, 1))
            outs.append(tj * cos + partner * sin)
        return jnp.concatenate(outs, axis=1)

    un = _rms(x_ref[0], g).astype(BF16)
    pc = jnp.dot(un, wc_ref[...], preferred_element_type=F32)
    glu_ref[0] = pc[:, :CONV_CHANNELS] * jax.nn.sigmoid(pc[:, CONV_CHANNELS:])

    for sub in range(tt // PERM_TOKENS):
        rows = slice(sub * PERM_ROWS, (sub + 1) * PERM_ROWS)
        u = jnp.dot(perm_ref[...], un[sub * PERM_TOKENS:(sub + 1) * PERM_TOKENS],
                    preferred_element_type=F32).astype(BF16)
        cos = jnp.concatenate([c_ref[0, p, rows, :] for p in range(N_PLANES)], axis=0)
        sin = jnp.concatenate([s_ref[0, p, rows, :] for p in range(N_PLANES)], axis=0)
        proj = jnp.dot(u, wqkv_ref[...], preferred_element_type=F32)
        q = (rotary(proj[:, :ATTN_WIDTH], cos, sin) * (HEAD_DIM ** -0.5)).astype(BF16)
        k = rotary(proj[:, ATTN_WIDTH:2 * ATTN_WIDTH], cos, sin).astype(BF16)
        v = proj[:, 2 * ATTN_WIDTH:].astype(BF16)
        for p in range(N_PLANES):
            chunk = slice(p * PERM_ROWS, (p + 1) * PERM_ROWS)
            q_ref[0, p, rows, :] = q[chunk]
            k_ref[0, p, rows, :] = k[chunk]
            v_ref[0, p, rows, :] = v[chunk]


def _input_projection(x, cos_t, sin_t, g_mix_pre, w_in):
    b, s, d = x.shape
    tt = 1024
    mc = tt // N_PLANES
    wqkv = w_in[:, :3 * ATTN_WIDTH].astype(BF16)
    wc = w_in[:, 3 * ATTN_WIDTH:].astype(BF16)
    perm = jnp.asarray(_plane_permutation(), BF16)
    plane = lambda w: pl.BlockSpec((1, N_PLANES, mc, w), lambda i, j: (i, 0, j, 0))
    plane_shape = jax.ShapeDtypeStruct((b, N_PLANES, s // N_PLANES, ATTN_WIDTH), BF16)
    tok = lambda w: pl.BlockSpec((1, tt, w), lambda i, j: (i, j, 0))
    const = lambda shape: pl.BlockSpec(shape, lambda i, j: (0, 0))
    return pl.pallas_call(
        _inproj_kernel,
        grid=(b, s // tt),
        in_specs=[tok(d), plane(LANES), plane(LANES), const((1, d)), const(perm.shape),
                  const(wqkv.shape), const(wc.shape)],
        out_specs=[plane(ATTN_WIDTH), plane(ATTN_WIDTH), plane(ATTN_WIDTH), tok(CONV_CHANNELS)],
        out_shape=[plane_shape, plane_shape, plane_shape,
                   jax.ShapeDtypeStruct((b, s, CONV_CHANNELS), F32)],
        compiler_params=pltpu.CompilerParams(vmem_limit_bytes=VMEM_LIMIT),
        name="input_projection",
    )(x, cos_t, sin_t, g_mix_pre[None, :], perm, wqkv, wc)


def _attention_biases():
    band = lambda j: np.where((j >= 0) & (j <= WINDOW), 0.0, NEG).astype(np.float32)
    cols = lambda m: np.where(m, NEG, 0.0).astype(np.float32)
    twice = lambda a: np.concatenate([a, a], axis=0)
    mq = np.arange(WINDOW)[:, None]
    kj = np.arange(2 * WINDOW)[None, :]
    j16 = mq + WINDOW - kj
    prev16 = kj < WINDOW
    row = np.arange(128)[:, None]
    col = np.arange(256)[None, :]
    j4 = 4 * (row % 32 - (col % 64 - 32)) + row // 32 - col // 64
    prev4 = col % 64 < 32
    row = np.arange(256)[:, None]
    col = np.arange(512)[None, :]
    res = np.vectorize(_residue_of_plane)
    j1 = 16 * (row % 16 - (col % 32 - 16)) + res(row // 16) - res(col // 32)
    prev1 = col % 32 < 16
    b1 = np.stack([twice(band(j1[:128])), twice(band(j1[128:]))])
    return [jnp.asarray(a) for a in (twice(band(j16)), twice(band(j4)), b1, cols(prev16), cols(prev4), cols(prev1))]


UNROLL = 16


def _attn_kernel(q_ref, kc_ref, kp_ref, vc_ref, vp_ref, b16_ref, b4_ref, b1_ref, p16_ref, p4_ref, p1_ref,
                 o_ref, m_scr, l_scr, a_scr, bias16_scr):
    no_prev = (pl.program_id(1) == 0).astype(F32)
    head0 = lax.broadcasted_iota(jnp.int32, (1, LANES), 1) < HEAD_DIM
    bias16_scr[...] = b16_ref[...] + no_prev * p16_ref[...]

    def tile(qt, kt, vt, bias):
        n = qt.shape[0]
        zero = jnp.zeros_like(qt)
        q2 = jnp.concatenate([jnp.where(head0, qt, zero), jnp.where(head0, zero, qt)], axis=0)
        s = lax.dot_general(q2, kt, (((1,), (1,)), ((), ())), preferred_element_type=F32) + bias
        m = jnp.max(s, axis=-1, keepdims=True)
        e = jnp.exp(s - m).astype(BF16)
        va = jnp.concatenate([vt, jnp.ones_like(vt)], axis=1)
        o = jnp.dot(e, va, preferred_element_type=F32)
        pick = lambda top, bot: jnp.where(head0, top, bot)
        mm = pick(jnp.broadcast_to(m[:n], (n, LANES)), jnp.broadcast_to(m[n:], (n, LANES)))
        return mm, pick(o[:n, LANES:], o[n:, LANES:]), pick(o[:n, :LANES], o[n:, :LANES])

    def put(branch, start, size, stats, off):
        for scr, val in zip((m_scr, l_scr, a_scr), stats):
            scr[branch, pl.ds(start, size), :] = val[off:off + size]

    def body16(i, carry):
        for p in [UNROLL * i + u for u in range(UNROLL)]:
            kt = jnp.concatenate([kp_ref[0, p], kc_ref[0, p]], axis=0)
            vt = jnp.concatenate([vp_ref[0, p], vc_ref[0, p]], axis=0)
            put(0, pl.multiple_of(p * WINDOW, WINDOW), WINDOW, tile(q_ref[0, p], kt, vt, bias16_scr[...]), 0)
        return carry
    lax.fori_loop(0, N_PLANES // UNROLL, body16, 0)

    def body4(g, carry):
        for c, i in [(4 * g + cc, ii) for cc in range(4) for ii in range(4)]:
            qt = jnp.concatenate([q_ref[0, 4 * c + a, 32 * i:32 * i + 32, :] for a in range(4)], axis=0)
            if i == 0:
                ks = [x for a in range(4) for x in (kp_ref[0, 4 * c + a, 96:128, :], kc_ref[0, 4 * c + a, 0:32, :])]
                vs = [x for a in range(4) for x in (vp_ref[0, 4 * c + a, 96:128, :], vc_ref[0, 4 * c + a, 0:32, :])]
                bias = b4_ref[...] + no_prev * p4_ref[...]
            else:
                ks = [kc_ref[0, 4 * c + a, 32 * i - 32:32 * i + 32, :] for a in range(4)]
                vs = [vc_ref[0, 4 * c + a, 32 * i - 32:32 * i + 32, :] for a in range(4)]
                bias = b4_ref[...]
            stats = tile(qt, jnp.concatenate(ks, axis=0), jnp.concatenate(vs, axis=0), bias)
            for a in range(4):
                put(1, pl.multiple_of((4 * c + a) * WINDOW + 32 * i, 32), 32, stats, 32 * a)
        return carry
    lax.fori_loop(0, 1, body4, 0)

    def tile1(i, first):
        rq = pl.ds(pl.multiple_of(16 * i, 16), 16)
        if first:
            ks = [x for p in range(N_PLANES) for x in (kp_ref[0, p, 112:128, :], kc_ref[0, p, 0:16, :])]
            vs = [x for p in range(N_PLANES) for x in (vp_ref[0, p, 112:128, :], vc_ref[0, p, 0:16, :])]
        else:
            rk = pl.ds(pl.multiple_of(16 * i - 16, 16), 32)
            ks = [kc_ref[0, p, rk, :] for p in range(N_PLANES)]
            vs = [vc_ref[0, p, rk, :] for p in range(N_PLANES)]
        kt = jnp.concatenate(ks, axis=0)
        vt = jnp.concatenate(vs, axis=0)
        for half in range(2):
            planes = range(8 * half, 8 * half + 8)
            qt = jnp.concatenate([q_ref[0, p, rq, :] for p in planes], axis=0)
            bias = b1_ref[half] + no_prev * p1_ref[...] if first else b1_ref[half]
            stats = tile(qt, kt, vt, bias)
            for p in planes:
                put(2, pl.multiple_of(p * WINDOW + 16 * i, 16), 16, stats, 16 * (p - 8 * half))

    tile1(0, True)
    for i in range(1, 4):
        tile1(i, False)

    def body1(g, carry):
        for u in range(4):
            tile1(4 * g + u, False)
        return carry
    lax.fori_loop(1, WINDOW // 64, body1, 0)

    def combine(i, carry):
        for p in (2 * i, 2 * i + 1):
            rows = pl.ds(pl.multiple_of(p * WINDOW, WINDOW), WINDOW)
            ms = [m_scr[b, rows, :] for b in range(3)]
            mx = jnp.maximum(jnp.maximum(ms[0], ms[1]), ms[2])
            ws = [jnp.exp(m - mx) for m in ms]
            den = ws[0] * l_scr[0, rows, :] + ws[1] * l_scr[1, rows, :] + ws[2] * l_scr[2, rows, :]
            num = ws[0] * a_scr[0, rows, :] + ws[1] * a_scr[1, rows, :] + ws[2] * a_scr[2, rows, :]
            o_ref[0, p] = (num / den).astype(BF16)
        return carry
    lax.fori_loop(0, N_PLANES // 2, combine, 0)


def _dilated_attention(q, k, v):
    b, _, sm, _ = q.shape
    n_span = sm // WINDOW
    cur = pl.BlockSpec((1, N_PLANES, WINDOW, LANES), lambda i, j, h: (i, 0, j, h))
    prev = pl.BlockSpec((1, N_PLANES, WINDOW, LANES), lambda i, j, h: (i, 0, jnp.maximum(j - 1, 0), h))
    biases = _attention_biases()
    bias_specs = [pl.BlockSpec(a.shape, lambda i, j, h, nd=a.ndim: (0,) * nd) for a in biases]
    stats = pltpu.VMEM((3, SPAN, LANES), F32)
    return pl.pallas_call(
        _attn_kernel,
        grid=(b, n_span, ATTN_WIDTH // LANES),
        in_specs=[cur, cur, prev, cur, prev] + bias_specs,
        out_specs=cur,
        out_shape=jax.ShapeDtypeStruct(q.shape, BF16),
        scratch_shapes=[stats, stats, stats, pltpu.VMEM((2 * WINDOW, 2 * WINDOW), F32)],
        compiler_params=pltpu.CompilerParams(vmem_limit_bytes=VMEM_LIMIT),
        name="dilated_attention",
    )(q, k, k, v, v, *biases)


CONV_HALO = 32


def _conv_kernel(cur_ref, prev_ref, w_ref, b_ref, g_ref, bl_ref, o_ref, scr, *, chunk):
    tt = cur_ref.shape[1]
    has_prev = pl.program_id(1) > 0
    scr[0, 0:CONV_HALO, :] = jnp.where(has_prev, prev_ref[0], 0.0)
    scr[0, CONV_HALO:CONV_HALO + tt, :] = cur_ref[0]
    aligned_rows = tt + CONV_HALO - SUBLANES
    for s in range(1, SUBLANES):
        scr[s, 0:aligned_rows, :] = scr[0, s:s + aligned_rows, :]
    lead = CONV_HALO - (CONV_WIDTH - 1)
    for c0 in range(0, tt, chunk):
        acc = jnp.zeros((chunk, CONV_CHANNELS), F32)
        for j in range(CONV_WIDTH):
            s, a = (lead + j) % SUBLANES, (lead + j) // SUBLANES * SUBLANES
            acc = acc + w_ref[j:j + 1, :] * scr[s, c0 + a:c0 + a + chunk, :]
        y = acc + b_ref[...]
        mu = jnp.mean(y, axis=-1, keepdims=True)
        var = jnp.mean(jnp.square(y - mu), axis=-1, keepdims=True)
        yn = (y - mu) * lax.rsqrt(var + NORM_EPS) * g_ref[...] + bl_ref[...]
        o_ref[0, c0:c0 + chunk, :] = (yn * jax.nn.sigmoid(yn)).astype(BF16)


def _conformer_conv(glu, w_dw, b_dw, g_ln, b_ln):
    b, s, c = glu.shape
    tt = 1024
    row = pl.BlockSpec((1, c), lambda i, j: (0, 0))
    return pl.pallas_call(
        functools.partial(_conv_kernel, chunk=128),
        grid=(b, s // tt),
        in_specs=[pl.BlockSpec((1, tt, c), lambda i, j: (i, j, 0)),
                  pl.BlockSpec((1, CONV_HALO, c), lambda i, j: (i, jnp.maximum(j * (tt // CONV_HALO) - 1, 0), 0)),
                  pl.BlockSpec((CONV_WIDTH, c), lambda i, j: (0, 0)), row, row, row],
        out_specs=pl.BlockSpec((1, tt, c), lambda i, j: (i, j, 0)),
        out_shape=jax.ShapeDtypeStruct((b, s, c), BF16),
        scratch_shapes=[pltpu.VMEM((SUBLANES, CONV_HALO + tt, c), F32)],
        name="conformer_conv",
    )(glu, glu, w_dw[:, 0, :], b_dw[None, :], g_ln[None, :], b_ln[None, :])


def _outproj_kernel(attn_ref, conv_ref, x_ref, permt_ref, woa_ref, woc_ref, gpost_ref, gffn_ref, wr_ref, br_ref,
                    h_ref, u_ref, idx_ref, gate_ref):
    tt = x_ref.shape[1]
    nat = []
    for sub in range(tt // PERM_TOKENS):
        rows = slice(sub * PERM_ROWS, (sub + 1) * PERM_ROWS)
        a = jnp.concatenate([attn_ref[0, p, rows, :] for p in range(N_PLANES)], axis=0)
        nat.append(jnp.dot(permt_ref[...], a, preferred_element_type=F32).astype(BF16))
    mix = (jnp.dot(jnp.concatenate(nat, axis=0), woa_ref[...], preferred_element_type=F32)
           + jnp.dot(conv_ref[0], woc_ref[...], preferred_element_type=F32))
    h = x_ref[0] + _rms(mix, gpost_ref[...])
    h_ref[0] = h
    u = _rms(h, gffn_ref[...])
    half = u.shape[1] // 2
    ub = u.astype(BF16).astype(F32)
    u_ref[...] = ((lax.bitcast_convert_type(ub[:, :half], jnp.uint32) >> 16)
                  | (lax.bitcast_convert_type(ub[:, half:], jnp.uint32) & jnp.uint32(0xFFFF0000)))
    u_hi = u.astype(BF16)
    u_lo = (u - u_hi.astype(F32)).astype(BF16)
    nt = (((1,), (1,)), ((), ()))
    by_hi = lax.dot_general(wr_ref[...], u_hi, nt, preferred_element_type=F32)
    by_lo = lax.dot_general(wr_ref[:N_EXPERTS], u_lo, nt, preferred_element_type=F32)
    logits = by_hi[:N_EXPERTS] + (by_hi[N_EXPERTS:] + by_lo) + br_ref[...]
    rows = lax.broadcasted_iota(jnp.int32, logits.shape, 0)
    vals = logits
    tops, idxs = [], []
    for _ in range(TOP_K):
        mx = jnp.max(vals, axis=0, keepdims=True)
        ix = jnp.min(jnp.where(vals == mx, rows, N_EXPERTS), axis=0, keepdims=True)
        tops.append(mx)
        idxs.append(ix)
        vals = jnp.where(rows == ix, -jnp.inf, vals)
    ex = [jnp.exp(t - tops[0]) for t in tops]
    den = ex[0] + ex[1] + ex[2] + ex[3]
    idx_ref[...] = jnp.concatenate(idxs, axis=0)
    gate_ref[...] = jnp.concatenate([e / den for e in ex] + [jnp.zeros((8 - TOP_K, tt), F32)], axis=0)


def _output_projection(attn, conv, x, w_out, g_mix_post, g_ffn_pre, w_router, b_router):
    b, s, d = x.shape
    tt = 1024
    mc = tt // N_PLANES
    n_t = s // tt
    woa = w_out[:ATTN_WIDTH].astype(BF16)
    woc = w_out[ATTN_WIDTH:].astype(BF16)
    wr_hi = w_router.T.astype(BF16)
    wr_split = jnp.concatenate([wr_hi, (w_router.T - wr_hi.astype(F32)).astype(BF16)], axis=0)
    permt =jnp.asarray(_plane_permutation().T, BF16)
    const = lambda shape: pl.BlockSpec(shape, lambda i, j: (0, 0))
    flat = lambda w: pl.BlockSpec((tt, w), lambda i, j: (i * n_t + j, 0))
    lanes = lambda r: pl.BlockSpec((r, tt), lambda i, j: (0, i * n_t + j))
    return pl.pallas_call(
        _outproj_kernel,
        grid=(b, n_t),
        in_specs=[pl.BlockSpec((1, N_PLANES, mc, ATTN_WIDTH), lambda i, j: (i, 0, j, 0)),
                  pl.BlockSpec((1, tt, CONV_CHANNELS), lambda i, j: (i, j, 0)),
                  pl.BlockSpec((1, tt, d), lambda i, j: (i, j, 0)),
                  const(permt.shape), const(woa.shape), const(woc.shape), const((1, d)), const((1, d)),
                  const((2 * N_EXPERTS, d)), const((N_EXPERTS, 1))],
        out_specs=[pl.BlockSpec((1, tt, d), lambda i, j: (i, j, 0)), flat(d // 2), lanes(TOP_K), lanes(8)],
        out_shape=[jax.ShapeDtypeStruct((b, s, d), F32),
                   jax.ShapeDtypeStruct((b * s, d // 2), jnp.uint32),
                   jax.ShapeDtypeStruct((TOP_K, b * s), jnp.int32),
                   jax.ShapeDtypeStruct((8, b * s), F32)],
        compiler_params=pltpu.CompilerParams(vmem_limit_bytes=VMEM_LIMIT),
        name="output_projection_router",
    )(attn, conv, x, permt, woa, woc, g_mix_post[None, :], g_ffn_pre[None, :], wr_split, b_router[:, None])


def _route_kernel(idx_ref, dest_ref, pend_ref, blk_ref, carry, pstart):
    phase = pl.program_id(0)
    step = pl.program_id(1)
    tt = idx_ref.shape[1]
    rows = lax.broadcasted_iota(jnp.int32, (N_EXPERTS, tt), 0)
    hot = [rows == idx_ref[k:k + 1, :] for k in range(TOP_K)]
    memb = sum(h.astype(F32) for h in hot)

    @pl.when((phase == 0) & (step == 0))
    def _():
        carry[...] = jnp.zeros_like(carry)

    @pl.when((phase == 1) & (step == 0))
    def _():
        counts = carry[...]
        padded = jnp.floor((counts + (EXPERT_ROWS - 1)) * (1.0 / EXPERT_ROWS)) * EXPERT_ROWS
        tri = (lax.broadcasted_iota(jnp.int32, (N_EXPERTS, N_EXPERTS), 1)
               <= lax.broadcasted_iota(jnp.int32, (N_EXPERTS, N_EXPERTS), 0)).astype(F32)
        pend = jnp.dot(tri, padded, precision=lax.Precision.HIGHEST, preferred_element_type=F32)
        pstart[...] = pend - padded
        pend_ref[...] = pend.astype(jnp.int32)
        starts = lax.broadcasted_iota(jnp.int32, (N_EXPERTS, blk_ref.shape[1]), 1) * EXPERT_ROWS
        ended = (pend.astype(jnp.int32) <= starts).astype(jnp.int32)
        blk_ref[...] = jnp.minimum(jnp.sum(ended, axis=0, keepdims=True), N_EXPERTS - 1)
        carry[...] = jnp.zeros_like(carry)

    @pl.when(phase == 1)
    def _():
        earlier = (lax.broadcasted_iota(jnp.int32, (tt, tt), 0)
                   < lax.broadcasted_iota(jnp.int32, (tt, tt), 1)).astype(BF16)
        row = jnp.dot(memb.astype(BF16), earlier, preferred_element_type=F32) + (carry[...] + pstart[...])
        dest_ref[...] = jnp.concatenate(
            [jnp.sum(jnp.where(h, row, 0.0), axis=0, keepdims=True) for h in hot], axis=0).astype(jnp.int32)

    carry[...] = carry[...] + jnp.sum(memb, axis=1, keepdims=True)


def _routing(idx, n_blk):
    _, t = idx.shape
    tt = 1024
    blk_lanes = -(-n_blk // LANES) * LANES
    return pl.pallas_call(
        _route_kernel,
        grid=(2, t // tt),
        in_specs=[pl.BlockSpec((TOP_K, tt), lambda ph, i: (0, i))],
        out_specs=[pl.BlockSpec((TOP_K, tt), lambda ph, i: (0, i * ph)),
                   pl.BlockSpec((N_EXPERTS, 1), lambda ph, i: (0, 0)),
                   pl.BlockSpec((1, blk_lanes), lambda ph, i: (0, 0))],
        out_shape=[jax.ShapeDtypeStruct((TOP_K, t), jnp.int32), jax.ShapeDtypeStruct((N_EXPERTS, 1), jnp.int32),
                   jax.ShapeDtypeStruct((1, blk_lanes), jnp.int32)],
        scratch_shapes=[pltpu.VMEM((N_EXPERTS, 1), F32), pltpu.VMEM((N_EXPERTS, 1), F32)],
        compiler_params=pltpu.CompilerParams(dimension_semantics=("arbitrary", "arbitrary")),
        name="moe_routing",
    )(idx)


SC_SCAN = 16384
SC_UNROLL = 8


def _sc_token_of_row(dest_flat, cap, n_tok):
    from jax.experimental.pallas import tpu_sc as plsc
    assert n_tok & (n_tok - 1) == 0
    info = plsc.get_sparse_core_info()
    n_core, n_sub, lanes = info.num_cores, info.num_subcores, info.num_lanes
    n_src = dest_flat.shape[0]
    per_w = cap // (n_core * n_sub)
    assert per_w * n_core * n_sub == cap and per_w % lanes == 0 and n_src % SC_SCAN == 0
    mesh = plsc.VectorSubcoreMesh(core_axis_name="c", subcore_axis_name="s")

    def body(dest_hbm, out_hbm, src_v, out_v):
        lo = (lax.axis_index("s") * n_core + lax.axis_index("c")) * per_w
        lane = lax.iota(jnp.int32, lanes)

        @pl.loop(0, per_w // lanes)
        def _(i):
            out_v[pl.ds(pl.multiple_of(i * lanes, lanes), lanes)] = (lo + i * lanes + lane) & (n_tok - 1)

        @pl.loop(0, n_src // SC_SCAN)
        def _(c):
            pltpu.sync_copy(dest_hbm.at[pl.ds(pl.multiple_of(c * SC_SCAN, 8), SC_SCAN)], src_v)

            @pl.loop(0, SC_SCAN // (lanes * SC_UNROLL))
            def _(g):
                for u in range(SC_UNROLL):
                    at = (g * SC_UNROLL + u) * lanes
                    local = src_v[pl.ds(pl.multiple_of(at, lanes), lanes)] - lo
                    mine = lax.bitcast_convert_type(local, jnp.uint32) < jnp.uint32(per_w)
                    tok = (c * SC_SCAN + at + lane) & (n_tok - 1)
                    plsc.store_scatter(out_v, [local], tok, mask=mine)
        pltpu.sync_copy(out_v, out_hbm.at[pl.ds(pl.multiple_of(lo, 8), per_w)])

    return pl.kernel(body, mesh=mesh, out_type=jax.ShapeDtypeStruct((cap,), jnp.int32),
                     scratch_types=[pltpu.VMEM((SC_SCAN,), jnp.int32), pltpu.VMEM((per_w,), jnp.int32)],
                     compiler_params=pltpu.CompilerParams(needs_layout_passes=False),
                     name="moe_sc_token_of_row")(dest_flat)


def _expert_kernel(blk_e_ref, nused_ref, xb_ref, wgu_ref, bgu_ref, wd_ref, bd_ref, *rest):
    yb_ref, done_ref, wgu_b, wd_b = rest[-4:]
    j = pl.program_id(0)
    done_ref[...] = jnp.zeros_like(done_ref)
    nused = nused_ref[0]
    d_ff = wd_ref.shape[1]

    new_expert = (j == 0) | (blk_e_ref[j] != blk_e_ref[jnp.maximum(j - 1, 0)])

    @pl.when((j < nused) & new_expert)
    def _():
        def cast(src, dst):
            def body(i, carry):
                rows = pl.ds(pl.multiple_of(i * LANES, LANES), LANES)
                dst[rows, :] = src[0, rows, :].astype(BF16)
                return carry
            lax.fori_loop(0, dst.shape[0] // LANES, body, 0)
        cast(wgu_ref, wgu_b)
        cast(wd_ref, wd_b)

    @pl.when(j < nused)
    def _():
        word = xb_ref[...]
        half = word.shape[1]
        lo = lax.bitcast_convert_type(word << 16, F32).astype(BF16)
        hi = lax.bitcast_convert_type(word & jnp.uint32(0xFFFF0000), F32).astype(BF16)
        gu = (jnp.dot(lo, wgu_b[:half, :], preferred_element_type=F32)
              + jnp.dot(hi, wgu_b[half:, :], preferred_element_type=F32) + bgu_ref[0])
        gate = jnp.minimum(gu[:, :d_ff], SWIGLU_LIMIT)
        up = jnp.clip(gu[:, d_ff:], -SWIGLU_LIMIT, SWIGLU_LIMIT)
        hid = (up + 1.0) * (gate * jax.nn.sigmoid(gate * SWIGLU_ALPHA))
        yb_ref[...] = jnp.dot(hid.astype(BF16), wd_b[...], preferred_element_type=F32) + bd_ref[0]

    @pl.when(j >= nused)
    def _():
        yb_ref[...] = jnp.zeros_like(yb_ref)


def _experts(xb, chunk, n_chunk, yb_prev, blk_e, nused, w_gate_up, b_gate_up, w_down, b_down):
    rows, d_in = xb.shape
    e, d, ff2 = w_gate_up.shape
    d_ff = w_down.shape[1]
    n_blk = rows // EXPERT_ROWS
    live = lambda j, nu: jnp.maximum(jnp.minimum(j, nu[0] - 1), 0)
    expert = lambda j, be, nu: (be[live(j, nu)], 0, 0)
    in_specs = [pl.BlockSpec((EXPERT_ROWS, d_in), lambda j, be, nu: (live(j, nu), 0)),
                pl.BlockSpec((1, d, ff2), expert), pl.BlockSpec((1, 1, ff2), expert),
                pl.BlockSpec((1, d_ff, d), expert), pl.BlockSpec((1, 1, d), expert)]
    args = [blk_e, nused, xb, w_gate_up, b_gate_up[:, None, :], w_down, b_down[:, None, :]]
    aliases = {}
    if yb_prev is not None:
        in_specs.append(pl.BlockSpec(memory_space=pl.ANY))
        args.append(yb_prev)
        aliases = {len(args) - 1: 0}
    grid_spec = pltpu.PrefetchScalarGridSpec(
        num_scalar_prefetch=2,
        grid=(n_blk,),
        in_specs=in_specs,
        out_specs=[pl.BlockSpec((EXPERT_ROWS, d), lambda j, be, nu: (chunk * n_blk + j, 0)),
                   pl.BlockSpec((SUBLANES, LANES), lambda j, be, nu: (0, 0))],
        scratch_shapes=[pltpu.VMEM((d, ff2), BF16), pltpu.VMEM((d_ff, d), BF16)],
    )
    return pl.pallas_call(
        _expert_kernel,
        grid_spec=grid_spec,
        out_shape=[jax.ShapeDtypeStruct((n_chunk * rows, d), F32), jax.ShapeDtypeStruct((SUBLANES, LANES), F32)],
        input_output_aliases=aliases,
        compiler_params=pltpu.CompilerParams(dimension_semantics=("arbitrary",), vmem_limit_bytes=VMEM_LIMIT),
        name="moe_experts",
    )(*args)


SC_BUFFER_BYTES = 128 * 1024
MOE_CHUNKS = 8
MOE_GROUPS = 4


def _sc_gather_rows(table, idx, part, n_part, after=None):
    from jax.experimental.pallas import tpu_sc as plsc
    n_rows = idx.shape[0] // n_part
    d = table.shape[1]
    info = plsc.get_sparse_core_info()
    n_core, n_sub = info.num_cores, info.num_subcores
    per_w = n_rows // (n_core * n_sub)
    fit = SC_BUFFER_BYTES // (d * table.dtype.itemsize)
    g_rows = max(r for r in (8, 16, 32, 64, 128) if r <= fit and per_w % r == 0)
    n_chunk = per_w // g_rows
    assert per_w * n_core * n_sub == n_rows
    mesh = plsc.VectorSubcoreMesh(core_axis_name="c", subcore_axis_name="s")

    def body(table_hbm, idx_hbm, *rest):
        out_hbm, idx_v, buf0, buf1, sem0, sem1 = rest[-6:]
        base = (lax.axis_index("s") * n_core + lax.axis_index("c")) * per_w
        pltpu.sync_copy(idx_hbm.at[pl.ds(pl.multiple_of(part * n_rows + base, 8), per_w)], idx_v)

        def gather(chunk, buf, sem):
            rows = idx_v.at[pl.ds(pl.multiple_of(chunk * g_rows, 8), g_rows)]
            return pltpu.make_async_copy(table_hbm.at[rows], buf, sem)

        def write(chunk, buf):
            pltpu.sync_copy(buf, out_hbm.at[pl.ds(pl.multiple_of(base + chunk * g_rows, 8), g_rows)])

        gather(0, buf0, sem0).start()

        @pl.loop(0, n_chunk // 2)
        def _(i):
            gather(2 * i + 1, buf1, sem1).start()
            gather(2 * i, buf0, sem0).wait()
            write(2 * i, buf0)

            @pl.when(2 * i + 2 < n_chunk)
            def _():
                gather(2 * i + 2, buf0, sem0).start()
            gather(2 * i + 1, buf1, sem1).wait()
            write(2 * i + 1, buf1)

        if n_chunk % 2:
            gather(n_chunk - 1, buf0, sem0).wait()
            write(n_chunk - 1, buf0)

    return pl.kernel(
        body, mesh=mesh, out_type=jax.ShapeDtypeStruct((n_rows, d), table.dtype),
        scratch_types=[pltpu.VMEM((per_w,), jnp.int32), pltpu.VMEM((g_rows, d), table.dtype),
                       pltpu.VMEM((g_rows, d), table.dtype), pltpu.SemaphoreType.DMA, pltpu.SemaphoreType.DMA],
        name="moe_sc_gather",
    )(table, idx, *([] if after is None else [after]))


def _sc_weighted_rows(table, idx, gates, group, n_group):
    from jax.experimental.pallas import tpu_sc as plsc
    d = table.shape[1]
    info = plsc.get_sparse_core_info()
    n_core, n_sub, lanes = info.num_cores, info.num_subcores, info.num_lanes
    n_all = idx.shape[0] // TOP_K
    n_tok = n_all // n_group
    per_w = n_tok // (n_core * n_sub)
    w = 8
    n_chunk = per_w // w
    assert per_w * n_core * n_sub == n_tok and n_chunk * w == per_w and n_chunk % 2 == 0
    mesh = plsc.VectorSubcoreMesh(core_axis_name="c", subcore_axis_name="s")

    def body(table_hbm, idx_hbm, g_hbm, out_hbm, *scr):
        idx_v, g_v, rb, ob = scr[0:4], scr[4:8], (scr[8:12], scr[12:16]), scr[16:18]
        sems, wsems = scr[18:20], scr[20:22]
        base = (lax.axis_index("s") * n_core + lax.axis_index("c")) * per_w
        for k in range(TOP_K):
            off = pl.multiple_of(k * n_all + group * n_tok + base, 8)
            pltpu.sync_copy(idx_hbm.at[pl.ds(off, per_w)], idx_v[k])
            pltpu.sync_copy(g_hbm.at[pl.ds(off, per_w)], g_v[k])

        def gathers(chunk, slot):
            rows = pl.ds(pl.multiple_of(chunk * w, 8), w)
            return [pltpu.make_async_copy(table_hbm.at[idx_v[k].at[rows]], rb[slot][k], sems[slot])
                    for k in range(TOP_K)]

        def write(chunk, slot):
            return pltpu.make_async_copy(ob[slot], out_hbm.at[pl.ds(pl.multiple_of(base + chunk * w, 8), w)],
                                         wsems[slot])

        def combine(chunk, slot):
            @pl.loop(0, w)
            def _(i):
                token = jnp.full((lanes,), chunk * w + i, jnp.int32)
                gs = [plsc.load_gather(g_v[k], [token]) for k in range(TOP_K)]
                for j in range(d // lanes):
                    cols = pl.ds(j * lanes, lanes)
                    acc = gs[0] * rb[slot][0][i, cols]
                    for k in range(1, TOP_K):
                        acc = acc + gs[k] * rb[slot][k][i, cols]
                    ob[slot][i, cols] = acc
            write(chunk, slot).start()

        for c in gathers(0, 0):
            c.start()

        @pl.loop(0, n_chunk // 2)
        def _(i):
            for c in gathers(2 * i + 1, 1):
                c.start()
            for c in gathers(2 * i, 0):
                c.wait()

            @pl.when(i > 0)
            def _():
                write(2 * i - 2, 0).wait()
            combine(2 * i, 0)

            @pl.when(2 * i + 2 < n_chunk)
            def _():
                for c in gathers(2 * i + 2, 0):
                    c.start()
            for c in gathers(2 * i + 1, 1):
                c.wait()

            @pl.when(i > 0)
            def _():
                write(2 * i - 1, 1).wait()
            combine(2 * i + 1, 1)

        write(n_chunk - 2, 0).wait()
        write(n_chunk - 1, 1).wait()

    scratch = ([pltpu.VMEM((per_w,), jnp.int32)] * TOP_K + [pltpu.VMEM((per_w,), F32)] * TOP_K
               + [pltpu.VMEM((w, d), F32)] * (2 * TOP_K + 2) + [pltpu.SemaphoreType.DMA] * 4)
    return pl.kernel(body, mesh=mesh, out_type=jax.ShapeDtypeStruct((n_tok, d), F32), scratch_types=scratch,
                     compiler_params=pltpu.CompilerParams(needs_layout_passes=False),
                     name="moe_sc_combine")(table, idx, gates)


def _combine_kernel(h_ref, p_ref, y_ref, gffn_ref, wple_ref, wpg_ref, gple_ref, o_ref):
    h2 = h_ref[...] + _rms(y_ref[...], gffn_ref[...])
    ple = (jnp.dot(p_ref[...].astype(BF16), wple_ref[...], preferred_element_type=F32)
           * jax.nn.sigmoid(jnp.dot(h2.astype(BF16), wpg_ref[...], preferred_element_type=F32)))
    o_ref[...] = h2 + _rms(ple, gple_ref[...])


def _combine_kernel_aliased(h_ref, p_ref, y_ref, gffn_ref, wple_ref, wpg_ref, gple_ref, prev_ref, o_ref):
    del prev_ref
    _combine_kernel(h_ref, p_ref, y_ref, gffn_ref, wple_ref, wpg_ref, gple_ref, o_ref)


def _combine(group, n_group, out_prev, h, p, y, g_ffn_post, w_ple_bf16, w_ple_gate_bf16, g_ple_post):
    t, d = h.shape
    tt = 256
    n_tg = t // n_group // tt
    const = lambda shape: pl.BlockSpec(shape, lambda i: (0, 0))
    tok = lambda w: pl.BlockSpec((tt, w), lambda i: (group * n_tg + i, 0))
    in_specs = [tok(d), tok(p.shape[1]), pl.BlockSpec((tt, d), lambda i: (i, 0)),
                const((1, d)), const(w_ple_bf16.shape), const(w_ple_gate_bf16.shape), const((1, d))]
    args = [h, p, y, g_ffn_post[None, :], w_ple_bf16, w_ple_gate_bf16, g_ple_post[None, :]]
    body, aliases = _combine_kernel, {}
    if out_prev is not None:
        in_specs.append(pl.BlockSpec(memory_space=pl.ANY))
        args.append(out_prev)
        body, aliases = _combine_kernel_aliased, {len(args) - 1: 0}
    return pl.pallas_call(
        body,
        grid=(n_tg,),
        in_specs=in_specs,
        out_specs=tok(d),
        out_shape=jax.ShapeDtypeStruct((t, d), F32),
        input_output_aliases=aliases,
        compiler_params=pltpu.CompilerParams(vmem_limit_bytes=VMEM_LIMIT),
        name="moe_combine_ple",
    )(*args)


def _layer(h, p, positions, g_mix_pre, w_in, w_dw, b_dw, g_conv_ln, b_conv_ln, w_out, g_mix_post, g_ffn_pre,
           w_router, b_router, w_gate_up, b_gate_up, w_down, b_down, g_ffn_post, w_ple, w_ple_gate, g_ple_post):
    b, s, d = h.shape
    t = b * s
    cos_t, sin_t = _rope_tables(positions)
    q, k, v, glu = _input_projection(h, cos_t, sin_t, g_mix_pre, w_in)
    attn = _dilated_attention(q, k, v)
    conv = _conformer_conv(glu, w_dw, b_dw, g_conv_ln, b_conv_ln)
    h1, u_ffn, idx, gates = _output_projection(attn, conv, h, w_out, g_mix_post, g_ffn_pre, w_router, b_router)
    cap = t * TOP_K + N_EXPERTS * EXPERT_ROWS
    n_blk = cap // EXPERT_ROWS
    dest, pend, blk = _routing(idx, n_blk)
    blk_e = blk[0, :n_blk]
    nused = pend[N_EXPERTS - 1] // EXPERT_ROWS
    dest_flat = dest.reshape(TOP_K * t)
    tok_of_row = _sc_token_of_row(dest_flat, cap, t)
    blk_c = n_blk // MOE_CHUNKS
    yb = done = None
    xb_next = _sc_gather_rows(u_ffn, tok_of_row, 0, MOE_CHUNKS)
    for c in range(MOE_CHUNKS):
        xb_c = xb_next
        if c + 1 < MOE_CHUNKS:
            xb_next = _sc_gather_rows(u_ffn, tok_of_row, c + 1, MOE_CHUNKS, after=done)
        nused_c = jnp.clip(nused - c * blk_c, 0, blk_c)
        yb, done = _experts(xb_c, c, MOE_CHUNKS, yb, blk_e[c * blk_c:(c + 1) * blk_c], nused_c,
                            w_gate_up, b_gate_up, w_down, b_down)
    w_ple_bf16, w_ple_gate_bf16 = w_ple.astype(BF16), w_ple_gate.astype(BF16)
    h1_flat, p_flat = h1.reshape(t, d), p.reshape(t, -1)
    gate_flat = gates[:TOP_K].reshape(TOP_K * t)
    out = None
    for g in range(MOE_GROUPS):
        y_g = _sc_weighted_rows(yb, dest_flat, gate_flat, g, MOE_GROUPS)
        out = _combine(g, MOE_GROUPS, out, h1_flat, p_flat, y_g, g_ffn_post, w_ple_bf16, w_ple_gate_bf16,
                       g_ple_post)
    return out.reshape(b, s, d)


def kernel(x, p, positions, g_mix_pre, w_in, w_dw, b_dw, g_conv_ln, b_conv_ln, w_out, g_mix_post, g_ffn_pre,
           w_router, b_router, w_gate_up, b_gate_up, w_down, b_down, g_ffn_post, w_ple, w_ple_gate, g_ple_post):
    h = x
    for i in range(p.shape[0]):
        h = _layer(h, p[i], positions, g_mix_pre[i], w_in[i], w_dw[i], b_dw[i], g_conv_ln[i], b_conv_ln[i],
                   w_out[i], g_mix_post[i], g_ffn_pre[i], w_router[i], b_router[i], w_gate_up[i], b_gate_up[i],
                   w_down[i], b_down[i], g_ffn_post[i], w_ple[i], w_ple_gate[i], g_ple_post[i])
    return h
```
